```python
import math
import jax, jax.numpy as jnp
from jax import lax
import numpy as np

D_MODEL = 1024
BATCH = 8
SEQ = 2048
DEPTH = 4

CHUNK = 64
N_A = DEPTH // 2
N_B = DEPTH - N_A
D_FF = 2816
GMLP_WIDTH = 4 * D_MODEL
GMLP_HALF = GMLP_WIDTH // 2
GMLP_WINDOW = 128
GMLP_GROUPS = 8
GMLP_GROUP_DIM = GMLP_HALF // GMLP_GROUPS
N_HEADS = 16
HEAD_DIM = D_MODEL // N_HEADS
LEFT_CHUNKS = 8
BAND = (LEFT_CHUNKS + 1) * CHUNK
LEFT_PAD = LEFT_CHUNKS * CHUNK
MAX_REL = 4 * CHUNK
N_REL = (CHUNK - 1) + MAX_REL + 1
ALPHA = (2.0 * DEPTH) ** 0.25
BETA = (8.0 * DEPTH) ** -0.25
LN_EPS = 1e-5
N_MOD = 9

kernel_name = "hybrid_gmlp_yoco_chunk_attention_encoder"


def layer_norm(x, g, b):
    xf = x.astype(jnp.float32)
    mu = jnp.mean(xf, axis=-1, keepdims=True)
    var = jnp.mean(jnp.square(xf - mu), axis=-1, keepdims=True)
    return ((xf - mu) * lax.rsqrt(var + LN_EPS)).astype(x.dtype) * g + b


def swiglu(h, w_gu, w_down):
    gu = h @ w_gu
    g, u = jnp.split(gu, 2, axis=-1)
    return (jax.nn.silu(g) * u) @ w_down


def gmlp_mixer(h, w_in, b_in, ln_g, ln_b, w_s, b_s, w_out):
    B, S, _ = h.shape
    z = jax.nn.gelu(h @ w_in + b_in, approximate=False)
    u, v = jnp.split(z, 2, axis=-1)
    v = layer_norm(v, ln_g, ln_b)
    v = v.reshape(B, S // GMLP_WINDOW, GMLP_WINDOW, GMLP_GROUPS, GMLP_GROUP_DIM)
    t = np.arange(GMLP_WINDOW)
    mask = ((t[None, :] // CHUNK) <= (t[:, None] // CHUNK)).astype(np.float32)
    ws = w_s * jnp.asarray(mask, dtype=w_s.dtype)[None]
    s = jnp.einsum('gts,bnsgc->bntgc', ws, v) + b_s.T[None, None, :, :, None]
    return (u * s.reshape(B, S, GMLP_HALF)) @ w_out


def chunk_band_attention(h, w_q, rel_bias, w_o, k_pad, v_pad):
    B, S, _ = h.shape
    n_chunks = S // CHUNK
    q = (h @ w_q).reshape(B, n_chunks, CHUNK, N_HEADS, HEAD_DIM)
    q = jnp.transpose(q, (1, 0, 2, 3, 4))
    t = np.arange(CHUNK)
    r = np.arange(BAND)
    dist = t[:, None] + LEFT_PAD - r[None, :]
    idx = np.clip(dist, -(CHUNK - 1), MAX_REL) + (CHUNK - 1)
    bias = rel_bias[:, idx].astype(jnp.float32)
    scale = HEAD_DIM ** -0.5
    r_j = jnp.arange(BAND, dtype=jnp.int32)

    def one_chunk(args):
        n, qn = args
        start = n * CHUNK
        kn = lax.dynamic_slice_in_dim(k_pad, start, BAND, axis=1)
        vn = lax.dynamic_slice_in_dim(v_pad, start, BAND, axis=1)
        valid = (start - LEFT_PAD + r_j) >= 0
        sc = jnp.einsum('bthd,brhd->bhtr', qn, kn).astype(jnp.float32) * scale + bias
        sc = jnp.where(valid[None, None, None, :], sc, -jnp.inf)
        p = jax.nn.softmax(sc, axis=-1).astype(vn.dtype)
        return jnp.einsum('bhtr,brhd->bthd', p, vn)

    out = lax.map(one_chunk, (jnp.arange(n_chunks, dtype=jnp.int32), q))
    out = jnp.transpose(out, (1, 0, 2, 3, 4)).reshape(B, S, D_MODEL)
    return out @ w_o


def _fwd_setup_inputs(seed: int = 0) -> dict:
    key = jax.random.key(seed)
    ks = jax.random.split(key, 24)
    D = D_MODEL
    f32 = jnp.float32
    nrm = lambda k, shape: jax.random.normal(k, shape, dtype=f32)
    v_scale = jnp.concatenate([jnp.ones((D,), f32), jnp.full((D,), BETA, f32)])
    return {
        "x": nrm(ks[0], (BATCH, SEQ, D)),
        "c": nrm(ks[1], (BATCH, D)),
        "w_ada": nrm(ks[2], (DEPTH, D, N_MOD * D)) * (0.1 * D ** -0.5),
        "b_ada": nrm(ks[3], (DEPTH, N_MOD * D)) * 0.01,
        "ln_g": 1.0 + 0.01 * nrm(ks[4], (DEPTH, 3, D)),
        "ln_b": 0.01 * nrm(ks[5], (DEPTH, 3, D)),
        "ffn_gu": nrm(ks[6], (DEPTH, 2, D, 2 * D_FF)) * D ** -0.5,
        "ffn_down": nrm(ks[7], (DEPTH, 2, D_FF, D)) * (BETA * D_FF ** -0.5),
        "gmlp_w_in": nrm(ks[8], (N_A, D, GMLP_WIDTH)) * D ** -0.5,
        "gmlp_b_in": 0.01 * nrm(ks[9], (N_A, GMLP_WIDTH)),
        "gmlp_ln_g": 1.0 + 0.01 * nrm(ks[10], (N_A, GMLP_HALF)),
        "gmlp_ln_b": 0.01 * nrm(ks[11], (N_A, GMLP_HALF)),
        "gmlp_w_s": nrm(ks[12], (N_A, GMLP_GROUPS, GMLP_WINDOW, GMLP_WINDOW)) * (0.5 * GMLP_WINDOW ** -0.5),
        "gmlp_b_s": 1.0 + 0.01 * nrm(ks[13], (N_A, GMLP_GROUPS, GMLP_WINDOW)),
        "gmlp_w_out": nrm(ks[14], (N_A, GMLP_HALF, D)) * (BETA * GMLP_HALF ** -0.5),
        "w_ada_kv": nrm(ks[15], (D, 2 * D)) * (0.1 * D ** -0.5),
        "b_ada_kv": 0.01 * nrm(ks[16], (2 * D,)),
        "w_kv": nrm(ks[17], (D, 2 * D)) * D ** -0.5 * v_scale[None, :],
        "attn_w_q": nrm(ks[18], (N_B, D, D)) * D ** -0.5,
        "attn_rel_bias": 0.5 * nrm(ks[19], (N_B, N_HEADS, N_REL)),
        "attn_w_o": nrm(ks[20], (N_B, D, D)) * (BETA * D ** -0.5),
    }


def _fwd_reference(x, c, w_ada, b_ada, ln_g, ln_b, ffn_gu, ffn_down,
              gmlp_w_in, gmlp_b_in, gmlp_ln_g, gmlp_ln_b, gmlp_w_s, gmlp_b_s, gmlp_w_out,
              w_ada_kv, b_ada_kv, w_kv, attn_w_q, attn_rel_bias, attn_w_o):
    B, S, D = x.shape
    c_act = jax.nn.silu(c)
    k_pad = None
    v_pad = None
    for l in range(DEPTH):
        mod = (c_act @ w_ada[l] + b_ada[l]).reshape(B, 1, N_MOD, D)
        shift = [mod[:, :, 3 * i] for i in range(3)]
        scl = [mod[:, :, 3 * i + 1] for i in range(3)]
        gate = [1.0 + mod[:, :, 3 * i + 2] for i in range(3)]

        h = x * (1.0 + scl[0]) + shift[0]
        y = swiglu(h, ffn_gu[l, 0], ffn_down[l, 0])
        x = layer_norm(ALPHA * x + 0.5 * gate[0] * y, ln_g[l, 0], ln_b[l, 0])

        h = x * (1.0 + scl[1]) + shift[1]
        if l < N_A:
            y = gmlp_mixer(h, gmlp_w_in[l], gmlp_b_in[l], gmlp_ln_g[l], gmlp_ln_b[l],
                           gmlp_w_s[l], gmlp_b_s[l], gmlp_w_out[l])
        else:
            j = l - N_A
            y = chunk_band_attention(h, attn_w_q[j], attn_rel_bias[j], attn_w_o[j], k_pad, v_pad)
        x = layer_norm(ALPHA * x + gate[1] * y, ln_g[l, 1], ln_b[l, 1])

        h = x * (1.0 + scl[2]) + shift[2]
        y = swiglu(h, ffn_gu[l, 1], ffn_down[l, 1])
        x = layer_norm(ALPHA * x + 0.5 * gate[2] * y, ln_g[l, 2], ln_b[l, 2])

        if l == N_A - 1:
            mkv = (c_act @ w_ada_kv + b_ada_kv).reshape(B, 1, 2, D)
            hkv = x * (1.0 + mkv[:, :, 1]) + mkv[:, :, 0]
            kv = hkv @ w_kv
            k = kv[..., :D].reshape(B, S, N_HEADS, HEAD_DIM)
            v = kv[..., D:].reshape(B, S, N_HEADS, HEAD_DIM)
            pad = ((0, 0), (LEFT_PAD, 0), (0, 0), (0, 0))
            k_pad = jnp.pad(k, pad)
            v_pad = jnp.pad(v, pad)
    return x


import jax as _jax
import jax.numpy as _jnp

TWIN_FORMAT = 'train_step'
FWD_PARAMS = ['x', 'c', 'w_ada', 'b_ada', 'ln_g', 'ln_b', 'ffn_gu', 'ffn_down', 'gmlp_w_in', 'gmlp_b_in', 'gmlp_ln_g', 'gmlp_ln_b', 'gmlp_w_s', 'gmlp_b_s', 'gmlp_w_out', 'w_ada_kv', 'b_ada_kv', 'w_kv', 'attn_w_q', 'attn_rel_bias', 'attn_w_o']
TWIN_WEIGHTS = ['w_ada', 'b_ada', 'ln_g', 'ln_b', 'ffn_gu', 'ffn_down', 'gmlp_w_in', 'gmlp_b_in', 'gmlp_ln_g', 'gmlp_ln_b', 'gmlp_w_s', 'gmlp_b_s', 'gmlp_w_out', 'w_ada_kv', 'b_ada_kv', 'w_kv', 'attn_w_q', 'attn_rel_bias', 'attn_w_o']
TWIN_DIFF_INPUT = 'x'
TWIN_INPUTS = ['x', 'c', 'w_ada', 'b_ada', 'ln_g', 'ln_b', 'ffn_gu', 'ffn_down', 'gmlp_w_in', 'gmlp_b_in', 'gmlp_ln_g', 'gmlp_ln_b', 'gmlp_w_s', 'gmlp_b_s', 'gmlp_w_out', 'w_ada_kv', 'b_ada_kv', 'w_kv', 'attn_w_q', 'attn_rel_bias', 'attn_w_o', 'loss_target', 'm_w_ada', 'm_b_ada', 'm_ln_g', 'm_ln_b', 'm_ffn_gu', 'm_ffn_down', 'm_gmlp_w_in', 'm_gmlp_b_in', 'm_gmlp_ln_g', 'm_gmlp_ln_b', 'm_gmlp_w_s', 'm_gmlp_b_s', 'm_gmlp_w_out', 'm_w_ada_kv', 'm_b_ada_kv', 'm_w_kv', 'm_attn_w_q', 'm_attn_rel_bias', 'm_attn_w_o', 'v_w_ada', 'v_b_ada', 'v_ln_g', 'v_ln_b', 'v_ffn_gu', 'v_ffn_down', 'v_gmlp_w_in', 'v_gmlp_b_in', 'v_gmlp_ln_g', 'v_gmlp_ln_b', 'v_gmlp_w_s', 'v_gmlp_b_s', 'v_gmlp_w_out', 'v_w_ada_kv', 'v_b_ada_kv', 'v_w_kv', 'v_attn_w_q', 'v_attn_rel_bias', 'v_attn_w_o']
TWIN_OUTPUTS = ['loss', 'grad_x', 'grad_w_ada', 'grad_b_ada', 'grad_ln_g', 'grad_ln_b', 'grad_ffn_gu', 'grad_ffn_down', 'grad_gmlp_w_in', 'grad_gmlp_b_in', 'grad_gmlp_ln_g', 'grad_gmlp_ln_b', 'grad_gmlp_w_s', 'grad_gmlp_b_s', 'grad_gmlp_w_out', 'grad_w_ada_kv', 'grad_b_ada_kv', 'grad_w_kv', 'grad_attn_w_q', 'grad_attn_rel_bias', 'grad_attn_w_o', 'delta_w_ada', 'delta_b_ada', 'delta_ln_g', 'delta_ln_b', 'delta_ffn_gu', 'delta_ffn_down', 'delta_gmlp_w_in', 'delta_gmlp_b_in', 'delta_gmlp_ln_g', 'delta_gmlp_ln_b', 'delta_gmlp_w_s', 'delta_gmlp_b_s', 'delta_gmlp_w_out', 'delta_w_ada_kv', 'delta_b_ada_kv', 'delta_w_kv', 'delta_attn_w_q', 'delta_attn_rel_bias', 'delta_attn_w_o', 'new_m_w_ada', 'new_m_b_ada', 'new_m_ln_g', 'new_m_ln_b', 'new_m_ffn_gu', 'new_m_ffn_down', 'new_m_gmlp_w_in', 'new_m_gmlp_b_in', 'new_m_gmlp_ln_g', 'new_m_gmlp_ln_b', 'new_m_gmlp_w_s', 'new_m_gmlp_b_s', 'new_m_gmlp_w_out', 'new_m_w_ada_kv', 'new_m_b_ada_kv', 'new_m_w_kv', 'new_m_attn_w_q', 'new_m_attn_rel_bias', 'new_m_attn_w_o', 'new_v_w_ada', 'new_v_b_ada', 'new_v_ln_g', 'new_v_ln_b', 'new_v_ffn_gu', 'new_v_ffn_down', 'new_v_gmlp_w_in', 'new_v_gmlp_b_in', 'new_v_gmlp_ln_g', 'new_v_gmlp_ln_b', 'new_v_gmlp_w_s', 'new_v_gmlp_b_s', 'new_v_gmlp_w_out', 'new_v_w_ada_kv', 'new_v_b_ada_kv', 'new_v_w_kv', 'new_v_attn_w_q', 'new_v_attn_rel_bias', 'new_v_attn_w_o']
TWIN_LEAF_KINDS = {'loss': 'loss', 'grad_x': 'grad_x', 'grad_w_ada': 'grad_w', 'grad_b_ada': 'grad_w', 'grad_ln_g': 'grad_w', 'grad_ln_b': 'grad_w', 'grad_ffn_gu': 'grad_w', 'grad_ffn_down': 'grad_w', 'grad_gmlp_w_in': 'grad_w', 'grad_gmlp_b_in': 'grad_w', 'grad_gmlp_ln_g': 'grad_w', 'grad_gmlp_ln_b': 'grad_w', 'grad_gmlp_w_s': 'grad_w', 'grad_gmlp_b_s': 'grad_w', 'grad_gmlp_w_out': 'grad_w', 'grad_w_ada_kv': 'grad_w', 'grad_b_ada_kv': 'grad_w', 'grad_w_kv': 'grad_w', 'grad_attn_w_q': 'grad_w', 'grad_attn_rel_bias': 'grad_w', 'grad_attn_w_o': 'grad_w', 'delta_w_ada': 'delta_w', 'delta_b_ada': 'delta_w', 'delta_ln_g': 'delta_w', 'delta_ln_b': 'delta_w', 'delta_ffn_gu': 'delta_w', 'delta_ffn_down': 'delta_w', 'delta_gmlp_w_in': 'delta_w', 'delta_gmlp_b_in': 'delta_w', 'delta_gmlp_ln_g': 'delta_w', 'delta_gmlp_ln_b': 'delta_w', 'delta_gmlp_w_s': 'delta_w', 'delta_gmlp_b_s': 'delta_w', 'delta_gmlp_w_out': 'delta_w', 'delta_w_ada_kv': 'delta_w', 'delta_b_ada_kv': 'delta_w', 'delta_w_kv': 'delta_w', 'delta_attn_w_q': 'delta_w', 'delta_attn_rel_bias': 'delta_w', 'delta_attn_w_o': 'delta_w', 'new_m_w_ada': 'new_m', 'new_m_b_ada': 'new_m', 'new_m_ln_g': 'new_m', 'new_m_ln_b': 'new_m', 'new_m_ffn_gu': 'new_m', 'new_m_ffn_down': 'new_m', 'new_m_gmlp_w_in': 'new_m', 'new_m_gmlp_b_in': 'new_m', 'new_m_gmlp_ln_g': 'new_m', 'new_m_gmlp_ln_b': 'new_m', 'new_m_gmlp_w_s': 'new_m', 'new_m_gmlp_b_s': 'new_m', 'new_m_gmlp_w_out': 'new_m', 'new_m_w_ada_kv': 'new_m', 'new_m_b_ada_kv': 'new_m', 'new_m_w_kv': 'new_m', 'new_m_attn_w_q': 'new_m', 'new_m_attn_rel_bias': 'new_m', 'new_m_attn_w_o': 'new_m', 'new_v_w_ada': 'new_v', 'new_v_b_ada': 'new_v', 'new_v_ln_g': 'new_v', 'new_v_ln_b': 'new_v', 'new_v_ffn_gu': 'new_v', 'new_v_ffn_down': 'new_v', 'new_v_gmlp_w_in': 'new_v', 'new_v_gmlp_b_in': 'new_v', 'new_v_gmlp_ln_g': 'new_v', 'new_v_gmlp_ln_b': 'new_v', 'new_v_gmlp_w_s': 'new_v', 'new_v_gmlp_b_s': 'new_v', 'new_v_gmlp_w_out': 'new_v', 'new_v_w_ada_kv': 'new_v', 'new_v_b_ada_kv': 'new_v', 'new_v_w_kv': 'new_v', 'new_v_attn_w_q': 'new_v', 'new_v_attn_rel_bias': 'new_v', 'new_v_attn_w_o': 'new_v'}


def _forward(args):
    return _fwd_reference(*[args[k] for k in FWD_PARAMS])


def _output_shape():
    out = _jax.eval_shape(lambda: _forward(_fwd_setup_inputs(0)))
    return out.shape, out.dtype

N_MICROBATCH = 1
ADAM_LR = 0.001
ADAM_B1 = 0.9
ADAM_B2 = 0.999
ADAM_EPS = 1e-08
ADAM_WD = 0.01
ADAM_STEP = 10
PER_EXAMPLE_BATCH_AXIS = {'x': 0, 'c': 0, 'loss_target': 0}
SHARED_INPUTS = []
_WEIGHT_DTYPES = {'w_ada': _jnp.float32, 'b_ada': _jnp.float32, 'ln_g': _jnp.float32, 'ln_b': _jnp.float32, 'ffn_gu': _jnp.float32, 'ffn_down': _jnp.float32, 'gmlp_w_in': _jnp.float32, 'gmlp_b_in': _jnp.float32, 'gmlp_ln_g': _jnp.float32, 'gmlp_ln_b': _jnp.float32, 'gmlp_w_s': _jnp.float32, 'gmlp_b_s': _jnp.float32, 'gmlp_w_out': _jnp.float32, 'w_ada_kv': _jnp.float32, 'b_ada_kv': _jnp.float32, 'w_kv': _jnp.float32, 'attn_w_q': _jnp.float32, 'attn_rel_bias': _jnp.float32, 'attn_w_o': _jnp.float32}
MOMENT_SCALE = {'w_ada': 8.877020e-03, 'b_ada': 1.512502e-02, 'ln_g': 4.623593e+00, 'ln_b': 4.335406e-01, 'ffn_gu': 5.826452e-03, 'ffn_down': 2.260058e-02, 'gmlp_w_in': 1.271347e-02, 'gmlp_b_in': 1.433987e-02, 'gmlp_ln_g': 6.397724e-03, 'gmlp_ln_b': 6.337674e-03, 'gmlp_w_s': 1.783579e-02, 'gmlp_b_s': 2.054367e-02, 'gmlp_w_out': 5.614030e-02, 'w_ada_kv': 9.350110e-03, 'b_ada_kv': 1.870184e-02, 'w_kv': 6.724445e-03, 'attn_w_q': 2.019518e-03, 'attn_rel_bias': 8.677442e-04, 'attn_w_o': 6.588898e-03}


def _to_microbatches(a, axis):
    t = _jnp.moveaxis(a, axis, 0)
    t = t.reshape((N_MICROBATCH, t.shape[0] // N_MICROBATCH) + t.shape[1:])
    return _jnp.moveaxis(t, 1, axis + 1)


def setup_inputs(seed: int = 0) -> dict:
    inp = _fwd_setup_inputs(seed)
    key = _jax.random.fold_in(_jax.random.key(seed), 7919)
    shape, _ = _output_shape()
    out = dict(inp)
    out["loss_target"] = _jax.random.normal(_jax.random.fold_in(key, 0), shape, _jnp.float32)
    for i, name in enumerate(TWIN_WEIGHTS):
        w = inp[name].astype(_jnp.float32)
        if MOMENT_SCALE is None:
            s = _jnp.sqrt(_jnp.mean(_jnp.square(w)) + 1e-30)
        else:
            s = MOMENT_SCALE[name]
        km, kv = _jax.random.split(_jax.random.fold_in(key, i + 1))
        out[name] = w
        out["m_" + name] = s * _jax.random.normal(km, w.shape, _jnp.float32)
        out["v_" + name] = (s * s) * _jax.random.uniform(kv, w.shape, _jnp.float32, 0.5, 1.5)
    if N_MICROBATCH > 1:
        for name, axis in PER_EXAMPLE_BATCH_AXIS.items():
            out[name] = _to_microbatches(out[name], axis)
    return {'x': out['x'], 'c': out['c'], 'w_ada': out['w_ada'], 'b_ada': out['b_ada'], 'ln_g': out['ln_g'], 'ln_b': out['ln_b'], 'ffn_gu': out['ffn_gu'], 'ffn_down': out['ffn_down'], 'gmlp_w_in': out['gmlp_w_in'], 'gmlp_b_in': out['gmlp_b_in'], 'gmlp_ln_g': out['gmlp_ln_g'], 'gmlp_ln_b': out['gmlp_ln_b'], 'gmlp_w_s': out['gmlp_w_s'], 'gmlp_b_s': out['gmlp_b_s'], 'gmlp_w_out': out['gmlp_w_out'], 'w_ada_kv': out['w_ada_kv'], 'b_ada_kv': out['b_ada_kv'], 'w_kv': out['w_kv'], 'attn_w_q': out['attn_w_q'], 'attn_rel_bias': out['attn_rel_bias'], 'attn_w_o': out['attn_w_o'], 'loss_target': out['loss_target'], 'm_w_ada': out['m_w_ada'], 'm_b_ada': out['m_b_ada'], 'm_ln_g': out['m_ln_g'], 'm_ln_b': out['m_ln_b'], 'm_ffn_gu': out['m_ffn_gu'], 'm_ffn_down': out['m_ffn_down'], 'm_gmlp_w_in': out['m_gmlp_w_in'], 'm_gmlp_b_in': out['m_gmlp_b_in'], 'm_gmlp_ln_g': out['m_gmlp_ln_g'], 'm_gmlp_ln_b': out['m_gmlp_ln_b'], 'm_gmlp_w_s': out['m_gmlp_w_s'], 'm_gmlp_b_s': out['m_gmlp_b_s'], 'm_gmlp_w_out': out['m_gmlp_w_out'], 'm_w_ada_kv': out['m_w_ada_kv'], 'm_b_ada_kv': out['m_b_ada_kv'], 'm_w_kv': out['m_w_kv'], 'm_attn_w_q': out['m_attn_w_q'], 'm_attn_rel_bias': out['m_attn_rel_bias'], 'm_attn_w_o': out['m_attn_w_o'], 'v_w_ada': out['v_w_ada'], 'v_b_ada': out['v_b_ada'], 'v_ln_g': out['v_ln_g'], 'v_ln_b': out['v_ln_b'], 'v_ffn_gu': out['v_ffn_gu'], 'v_ffn_down': out['v_ffn_down'], 'v_gmlp_w_in': out['v_gmlp_w_in'], 'v_gmlp_b_in': out['v_gmlp_b_in'], 'v_gmlp_ln_g': out['v_gmlp_ln_g'], 'v_gmlp_ln_b': out['v_gmlp_ln_b'], 'v_gmlp_w_s': out['v_gmlp_w_s'], 'v_gmlp_b_s': out['v_gmlp_b_s'], 'v_gmlp_w_out': out['v_gmlp_w_out'], 'v_w_ada_kv': out['v_w_ada_kv'], 'v_b_ada_kv': out['v_b_ada_kv'], 'v_w_kv': out['v_w_kv'], 'v_attn_w_q': out['v_attn_w_q'], 'v_attn_rel_bias': out['v_attn_rel_bias'], 'v_attn_w_o': out['v_attn_w_o']}


def _loss(weights, diff, rest, loss_target):
    with _jax.named_scope("forward"):
        args = {**rest, TWIN_DIFF_INPUT: diff, **{k: w.astype(_WEIGHT_DTYPES[k]) for k, w in weights.items()}}
        y = _forward(args)
    with _jax.named_scope("loss_head"):
        err = _jnp.square(y.astype(_jnp.float32) - loss_target)
        return 0.5 * _jnp.sum(_jnp.mean(err, axis=-1)) if err.ndim else 0.5 * err


def _adamw(w, g, m, v):
    m = ADAM_B1 * m + (1.0 - ADAM_B1) * g
    v = ADAM_B2 * v + (1.0 - ADAM_B2) * _jnp.square(g)
    m_hat = m / (1.0 - ADAM_B1 ** ADAM_STEP)
    v_hat = v / (1.0 - ADAM_B2 ** ADAM_STEP)
    delta = -ADAM_LR * (m_hat / (_jnp.sqrt(v_hat) + ADAM_EPS) + ADAM_WD * w)
    return delta, m, v


def reference(x, c, w_ada, b_ada, ln_g, ln_b, ffn_gu, ffn_down, gmlp_w_in, gmlp_b_in, gmlp_ln_g, gmlp_ln_b, gmlp_w_s, gmlp_b_s, gmlp_w_out, w_ada_kv, b_ada_kv, w_kv, attn_w_q, attn_rel_bias, attn_w_o, loss_target, m_w_ada, m_b_ada, m_ln_g, m_ln_b, m_ffn_gu, m_ffn_down, m_gmlp_w_in, m_gmlp_b_in, m_gmlp_ln_g, m_gmlp_ln_b, m_gmlp_w_s, m_gmlp_b_s, m_gmlp_w_out, m_w_ada_kv, m_b_ada_kv, m_w_kv, m_attn_w_q, m_attn_rel_bias, m_attn_w_o, v_w_ada, v_b_ada, v_ln_g, v_ln_b, v_ffn_gu, v_ffn_down, v_gmlp_w_in, v_gmlp_b_in, v_gmlp_ln_g, v_gmlp_ln_b, v_gmlp_w_s, v_gmlp_b_s, v_gmlp_w_out, v_w_ada_kv, v_b_ada_kv, v_w_kv, v_attn_w_q, v_attn_rel_bias, v_attn_w_o):
    given = dict(x=x, c=c, w_ada=w_ada, b_ada=b_ada, ln_g=ln_g, ln_b=ln_b, ffn_gu=ffn_gu, ffn_down=ffn_down, gmlp_w_in=gmlp_w_in, gmlp_b_in=gmlp_b_in, gmlp_ln_g=gmlp_ln_g, gmlp_ln_b=gmlp_ln_b, gmlp_w_s=gmlp_w_s, gmlp_b_s=gmlp_b_s, gmlp_w_out=gmlp_w_out, w_ada_kv=w_ada_kv, b_ada_kv=b_ada_kv, w_kv=w_kv, attn_w_q=attn_w_q, attn_rel_bias=attn_rel_bias, attn_w_o=attn_w_o, loss_target=loss_target, m_w_ada=m_w_ada, m_b_ada=m_b_ada, m_ln_g=m_ln_g, m_ln_b=m_ln_b, m_ffn_gu=m_ffn_gu, m_ffn_down=m_ffn_down, m_gmlp_w_in=m_gmlp_w_in, m_gmlp_b_in=m_gmlp_b_in, m_gmlp_ln_g=m_gmlp_ln_g, m_gmlp_ln_b=m_gmlp_ln_b, m_gmlp_w_s=m_gmlp_w_s, m_gmlp_b_s=m_gmlp_b_s, m_gmlp_w_out=m_gmlp_w_out, m_w_ada_kv=m_w_ada_kv, m_b_ada_kv=m_b_ada_kv, m_w_kv=m_w_kv, m_attn_w_q=m_attn_w_q, m_attn_rel_bias=m_attn_rel_bias, m_attn_w_o=m_attn_w_o, v_w_ada=v_w_ada, v_b_ada=v_b_ada, v_ln_g=v_ln_g, v_ln_b=v_ln_b, v_ffn_gu=v_ffn_gu, v_ffn_down=v_ffn_down, v_gmlp_w_in=v_gmlp_w_in, v_gmlp_b_in=v_gmlp_b_in, v_gmlp_ln_g=v_gmlp_ln_g, v_gmlp_ln_b=v_gmlp_ln_b, v_gmlp_w_s=v_gmlp_w_s, v_gmlp_b_s=v_gmlp_b_s, v_gmlp_w_out=v_gmlp_w_out, v_w_ada_kv=v_w_ada_kv, v_b_ada_kv=v_b_ada_kv, v_w_kv=v_w_kv, v_attn_w_q=v_attn_w_q, v_attn_rel_bias=v_attn_rel_bias, v_attn_w_o=v_attn_w_o)
    weights = {n: given[n] for n in TWIN_WEIGHTS}
    shared = {n: given[n] for n in SHARED_INPUTS}
    per_example = {n: given[n] for n in ['x', 'c']}
    grad_fn = _jax.value_and_grad(_loss, argnums=(0, 1))

    def one_microbatch(ex, loss_target):
        ex = dict(ex)
        diff = ex.pop(TWIN_DIFF_INPUT)
        return grad_fn(weights, diff, {**shared, **ex}, loss_target)

    if N_MICROBATCH == 1:
        loss, (grad_w, grad_x) = one_microbatch(per_example, given["loss_target"])
    else:
        def body(carry, xs):
            loss_sum, grad_sum = carry
            l_k, (gw_k, gx_k) = one_microbatch(xs[0], xs[1])
            with _jax.named_scope("update"):
                return (loss_sum + l_k, _jax.tree.map(_jnp.add, grad_sum, gw_k)), gx_k

        init = (_jnp.zeros((), _jnp.float32), _jax.tree.map(_jnp.zeros_like, weights))
        (loss, grad_w), grad_x = _jax.lax.scan(body, init, (per_example, given["loss_target"]))
    with _jax.named_scope("update"):
        delta_w, new_m, new_v = {}, {}, {}
        for n in TWIN_WEIGHTS:
            delta_w[n], new_m[n], new_v[n] = _adamw(weights[n], grad_w[n], given["m_" + n], given["v_" + n])
    return (loss, grad_x, *[grad_w[n] for n in TWIN_WEIGHTS], *[delta_w[n] for n in TWIN_WEIGHTS],
            *[new_m[n] for n in TWIN_WEIGHTS], *[new_v[n] for n in TWIN_WEIGHTS])
```

```python
import functools

import numpy as np
import jax
import jax.numpy as jnp
from jax import lax
from jax.experimental import pallas as pl
from jax.experimental.pallas import tpu as pltpu

F32 = jnp.float32
BF16 = jnp.bfloat16
MESH_AXES = ("x", "y", "c")
N_DEV = 8
MESH_ID = pl.DeviceIdType.MESH

DEPTH = 4
N_A = 2
CHUNK = 64
N_HEADS = 16
LEFT_CHUNKS = 8
BAND = (LEFT_CHUNKS + 1) * CHUNK
LEFT_PAD = LEFT_CHUNKS * CHUNK
MAX_REL = 4 * CHUNK
N_REL = (CHUNK - 1) + MAX_REL + 1
GMLP_WINDOW = 128
GMLP_GROUPS = 8
ALPHA = (2.0 * DEPTH) ** 0.25
LN_EPS = 1e-5
ADAM_LR = 0.001
ADAM_B1 = 0.9
ADAM_B2 = 0.999
ADAM_EPS = 1e-08
ADAM_WD = 0.01
ADAM_STEP = 10

V7X_VMEM_BYTES = 64 * 1024 * 1024
VMEM_LIMIT = V7X_VMEM_BYTES - 8 * 1024 * 1024
LANES = 128
SUBLANES = 8
MM_BLOCK = 1024
ROW_BLOCK = 256
OPT_ROW_BLOCK = 128

_ANY = pl.BlockSpec(memory_space=pl.ANY)
_VMEM = pl.BlockSpec(memory_space=pltpu.VMEM)


def _params(sem=None):
    return pltpu.CompilerParams(dimension_semantics=sem, vmem_limit_bytes=VMEM_LIMIT)


def _row_block(rows, target):
    for d in range(min(rows, target), 0, -1):
        if rows % d == 0 and (d % SUBLANES == 0 or d == rows):
            return d
    return rows


def _matmul(name, a, b, out_shape4, out_dtype, *, la=0, lb=0, lo=0, ta=False, tb=False,
            reduce=False, b_merge=1, out_merge=1, out_buf=None, a_silu=False):
    ja_n, _, a_r, a_c = a.shape
    jb_n, _, b_r, b_c = b.shape
    jo_n, _, o_r, o_c = out_shape4
    m_tot = a_c if ta else a_r
    k_a = a_r if ta else a_c
    b_rows = b_merge * b_r
    k_c = b_c if tb else b_rows
    n = b_rows if tb else b_c
    n_chunks = (jb_n // b_merge) if reduce else 1
    natural_k = reduce and ja_n == 1
    assert n == o_c, (name, n, o_c)
    assert k_a ==(k_c * n_chunks if natural_k else k_c), (name, k_a, k_c, n_chunks)
    bk = k_c if (k_c <= MM_BLOCK or (b_merge > 1 and not tb)) else MM_BLOCK
    assert k_c % bk == 0
    nkk = k_c // bk
    nk = n_chunks * nkk
    m_out = out_merge * o_r
    assert m_tot == m_out, (name, m_tot, m_out)
    bm = m_tot if (m_tot <= MM_BLOCK or out_merge > 1) else MM_BLOCK
    assert m_tot % bm == 0
    jo_blocks = jo_n // out_merge

    def a_index(j, m, k):
        kj, kk = k // nkk, k % nkk
        ja = 0 if ja_n == 1 else (kj if reduce else j)
        ke = kk + kj * nkk if natural_k else kk
        return (ja, la, ke, m) if ta else (ja, la, m, ke)

    def b_index(j, m, k):
        kj, kk = k // nkk, k % nkk
        jb = 0 if jb_n == b_merge else (kj if reduce else j)
        return (jb, lb, 0, kk) if tb else (jb, lb, kk, 0)

    def o_index(j, m, k):
        return (j, lo, 0, 0) if out_merge > 1 else (j, lo, m, 0)

    a_block = (None, None, bk, bm) if ta else (None, None, bm, bk)
    if b_merge > 1:
        b_block = (b_merge, None, b_r, bk if tb else n)
    else:
        b_block = (None, None, n, bk) if tb else (None, None, bk, n)
    o_block = (out_merge, None, o_r, n) if out_merge > 1 else (None, None, bm, n)
    dims = (((0 if ta else 1,), (1 if tb else 0,)), ((), ()))

    def body(a_ref, b_ref, *rest):
        o_ref, acc_ref = rest[-2], rest[-1]
        k = pl.program_id(2)

        @pl.when(k == 0)
        def _():
            acc_ref[...] = jnp.zeros_like(acc_ref)

        av = a_ref[...]
        if a_silu:
            af = av.astype(F32)
            av = af * jax.nn.sigmoid(af)
        bv = b_ref[...]
        if b_merge > 1:
            bv = bv.reshape(b_rows, bv.shape[-1])
        acc_ref[...] += lax.dot_general(av.astype(BF16), bv.astype(BF16), dims,
                                        preferred_element_type=F32)

        @pl.when(k == nk - 1)
        def _():
            ov = acc_ref[...].astype(out_dtype)
            if out_merge > 1:
                ov = ov.reshape(out_merge, o_r, n)
            o_ref[...] = ov

    in_specs = [pl.BlockSpec(a_block, a_index), pl.BlockSpec(b_block, b_index)]
    operands = [a, b]
    aliases = {}
    if out_buf is not None:
        assert out_buf.shape == tuple(out_shape4) and out_buf.dtype == out_dtype
        in_specs.append(_ANY)
        operands.append(out_buf)
        aliases = {2: 0}
    return pl.pallas_call(
        body, name=name,
        grid=(jo_blocks, m_tot // bm, nk),
        in_specs=in_specs,
        out_specs=pl.BlockSpec(o_block, o_index),
        out_shape=jax.ShapeDtypeStruct(tuple(out_shape4), out_dtype),
        scratch_shapes=[pltpu.VMEM((bm, n), F32)],
        input_output_aliases=aliases,
        compiler_params=_params(("parallel", "parallel", "arbitrary")),
    )(*operands)


def _as4(a):
    return a.reshape((1,) * (4 - a.ndim) + a.shape)


def _row_call(name, body, ins, outs, t, *, acc_outs=()):
    bt = _row_block(t, ROW_BLOCK)

    def spec(arr, tiled):
        if tiled:
            return pl.BlockSpec((bt,) + tuple(arr.shape[1:]), lambda i: (i,) + (0,) * (arr.ndim - 1))
        return pl.BlockSpec(tuple(arr.shape), lambda i: (0,) * arr.ndim)

    return pl.pallas_call(
        body, name=name, grid=(t // bt,),
        in_specs=[spec(a, tl) for a, tl in ins],
        out_specs=[spec(o, tl) for o, tl in outs],
        out_shape=[jax.ShapeDtypeStruct(o.shape, o.dtype) for o, _ in outs],
        compiler_params=_params(("arbitrary",) if acc_outs else ("parallel",)),
    )(*[a for a, _ in ins])


def _sds(shape, dtype):
    return jax.ShapeDtypeStruct(tuple(shape), dtype)


def _modulate(x, scl, shift):
    t, d = x.shape

    def body(x_ref, s_ref, b_ref, h_ref):
        h_ref[...] = (x_ref[...] * (1.0 + s_ref[...]) + b_ref[...]).astype(BF16)

    return _row_call("modulate", body, [(x, True), (scl, False), (shift, False)],
                     [(_sds((t, d), BF16), True)], t)[0]


def _ln_stats(r):
    mu = jnp.mean(r, axis=-1, keepdims=True)
    rc = r - mu
    var = jnp.mean(rc * rc, axis=-1, keepdims=True)
    rstd = lax.rsqrt(var + LN_EPS)
    return rc * rstd, rstd


def _ln_res_fwd(x, y, gw, g, b):
    t, d = x.shape

    def body(x_ref, y_ref, gw_ref, g_ref, b_ref, o_ref):
        r = ALPHA * x_ref[...] + gw_ref[...] * y_ref[...]
        xhat, _ = _ln_stats(r)
        o_ref[...] = xhat * g_ref[...] + b_ref[...]

    return _row_call("ln_res_fwd", body,
                     [(x, True), (y, True), (gw, False), (g, False), (b, False)],
                     [(_sds((t, d), F32), True)], t)[0]


def _ln_res_bwd(x, y, gw, g, dxn):
    t, d = x.shape

    def body(x_ref, y_ref, gw_ref, g_ref, dxn_ref, dx_ref, dy_ref, dgw_ref, dg_ref, db_ref):
        @pl.when(pl.program_id(0) == 0)
        def _():
            dgw_ref[...] = jnp.zeros_like(dgw_ref)
            dg_ref[...] = jnp.zeros_like(dg_ref)
            db_ref[...] = jnp.zeros_like(db_ref)

        yv = y_ref[...]
        gwv = gw_ref[...]
        dxn = dxn_ref[...]
        xhat, rstd = _ln_stats(ALPHA * x_ref[...] + gwv * yv)
        dxh = dxn * g_ref[...]
        m1 = jnp.mean(dxh, axis=-1, keepdims=True)
        m2 = jnp.mean(dxh * xhat, axis=-1, keepdims=True)
        dr = rstd * (dxh - m1 - xhat * m2)
        dx_ref[...] = ALPHA * dr
        dy_ref[...] = (gwv * dr).astype(BF16)
        dgw_ref[...] += jnp.sum(dr * yv, axis=0, keepdims=True)
        dg_ref[...] += jnp.sum(dxn * xhat, axis=0, keepdims=True)
        db_ref[...] += jnp.sum(dxn, axis=0, keepdims=True)

    vec = _sds((1, d), F32)
    return _row_call("ln_res_bwd", body,
                     [(x, True), (y, True), (gw, False), (g, False), (dxn, True)],
                     [(_sds((t, d), F32), True), (_sds((t, d), BF16), True),
                      (vec, False), (vec, False), (vec, False)], t, acc_outs=(2, 3, 4))


def _mod_bwd(dx_res, dh, x, scl):
    t, d = x.shape

    def body(dxr_ref, dh_ref, x_ref, s_ref, dx_ref, ds_ref, db_ref):
        @pl.when(pl.program_id(0) == 0)
        def _():
            ds_ref[...] = jnp.zeros_like(ds_ref)
            db_ref[...] = jnp.zeros_like(db_ref)

        dh = dh_ref[...]
        dx_ref[...] = dxr_ref[...] + dh * (1.0 + s_ref[...])
        ds_ref[...] += jnp.sum(dh * x_ref[...], axis=0, keepdims=True)
        db_ref[...] += jnp.sum(dh, axis=0, keepdims=True)

    vec = _sds((1, d), F32)
    return _row_call("mod_bwd", body, [(dx_res, True), (dh, True), (x, True), (scl, False)],
                     [(_sds((t, d), F32), True), (vec, False), (vec, False)], t, acc_outs=(1, 2))


def _add(a, b):
    rows = a.shape[0]

    def body(a_ref, b_ref, o_ref):
        o_ref[...] = a_ref[...] + b_ref[...]

    return _row_call("add", body, [(a, True), (b, True)], [(_sds(a.shape, a.dtype), True)], rows)[0]


def _loss_head(y, target):
    t, d = y.shape

    def body(y_ref, t_ref, l_ref, dy_ref):
        @pl.when(pl.program_id(0) == 0)
        def _():
            l_ref[...] = jnp.zeros_like(l_ref)

        err = y_ref[...] - t_ref[...]
        dy_ref[...] = err * (1.0 / d)
        part = 0.5 * jnp.sum(jnp.mean(err * err, axis=-1, keepdims=True), axis=0, keepdims=True)
        l_ref[...] += jnp.broadcast_to(part, l_ref.shape)

    return _row_call("loss_head", body, [(y, True), (target, True)],
                     [(_sds((SUBLANES, LANES), F32), False), (_sds((t, d), F32), True)], t,
                     acc_outs=(0,))


def _swiglu_fwd(gu):
    _, t, n = gu.shape
    half = N_DEV // 2
    bt = _row_block(t, ROW_BLOCK)
    gu4 = gu.reshape(2, half, t, n)

    def body(gu_ref, a_ref):
        g = gu_ref[0]
        a_ref[...] = (g * jax.nn.sigmoid(g) * gu_ref[1]).astype(BF16)

    return pl.pallas_call(
        body, name="swiglu_fwd", grid=(half, t // bt),
        in_specs=[pl.BlockSpec((2, None, bt, n), lambda j, i: (0, j, i, 0))],
        out_specs=pl.BlockSpec((None, bt, n), lambda j, i: (j, i, 0)),
        out_shape=_sds((half, t, n), BF16),
        compiler_params=_params(("parallel", "parallel")),
    )(gu4)


def _swiglu_bwd(gu, da):
    _, t, n = gu.shape
    half = N_DEV // 2
    bt = _row_block(t, ROW_BLOCK)
    gu4 = gu.reshape(2, half, t, n)

    def body(gu_ref, da_ref, d_ref):
        g = gu_ref[0]
        u = gu_ref[1]
        da = da_ref[...]
        sig = jax.nn.sigmoid(g)
        d_ref[0] = (da * u * sig * (1.0 + g * (1.0 - sig))).astype(BF16)
        d_ref[1] = (da * g * sig).astype(BF16)

    out = pl.pallas_call(
        body, name="swiglu_bwd", grid=(half, t // bt),
        in_specs=[pl.BlockSpec((2, None, bt, n), lambda j, i: (0, j, i, 0)),
                  pl.BlockSpec((None, bt, n), lambda j, i: (j, i, 0))],
        out_specs=pl.BlockSpec((2, None, bt, n), lambda j, i: (0, j, i, 0)),
        out_shape=_sds((2, half, t, n), BF16),
        compiler_params=_params(("parallel", "parallel")),
    )(gu4, da)
    return out.reshape(N_DEV, t, n)


_INV_SQRT2 = 0.7071067811865476
_INV_SQRT_2PI = 0.3989422804014327


def _gelu(z):
    return 0.5 * z * (1.0 + lax.erf(z * _INV_SQRT2))


def _gelu_grad(z):
    return 0.5 * (1.0 + lax.erf(z * _INV_SQRT2)) + z * jnp.exp(-0.5 * z * z) * _INV_SQRT_2PI


def _window_mask():
    t_out = lax.broadcasted_iota(jnp.int32, (GMLP_WINDOW, GMLP_WINDOW), 0)
    s_in = lax.broadcasted_iota(jnp.int32, (GMLP_WINDOW, GMLP_WINDOW), 1)
    return (s_in // CHUNK) <= (t_out // CHUNK)


def _gmlp_recompute(z_ref, bin_ref, lng_ref, lnb_ref):
    half = N_DEV // 2
    z = z_ref[...] + bin_ref[...]
    ge = _gelu(z)
    u = ge[:half]
    v = ge[half:]
    width = half * v.shape[-1]
    mu = jnp.sum(jnp.sum(v, axis=0), axis=-1, keepdims=True) / width
    vc = v - mu
    var = jnp.sum(jnp.sum(vc * vc, axis=0), axis=-1, keepdims=True) / width
    rstd = lax.rsqrt(var + LN_EPS)
    xhat = vc * rstd
    vn = xhat * lng_ref[...] + lnb_ref[...]
    return z, u, xhat, rstd, vn


def _gmlp_mid_fwd(zpre, b_in, ln_g, ln_b, w_s, b_s):
    _, t, n = zpre.shape
    half = N_DEV // 2
    gd = half * n // GMLP_GROUPS
    per = n // gd
    w = GMLP_WINDOW

    def body(z_ref, bin_ref, lng_ref, lnb_ref, ws_ref, bs_ref, o_ref):
        _, u, _, _, vn = _gmlp_recompute(z_ref, bin_ref, lng_ref, lnb_ref)
        mask = _window_mask()
        for g in range(GMLP_GROUPS):
            sh, c0 = g // per, (g % per) * gd
            wsm = jnp.where(mask, ws_ref[g], 0.0).astype(BF16)
            s = jnp.dot(wsm, vn[sh][:, c0:c0 + gd].astype(BF16), preferred_element_type=F32) + bs_ref[g]
            o_ref[sh, :, c0:c0 + gd] = (u[sh][:, c0:c0 + gd] * s).astype(BF16)

    whole = lambda a: pl.BlockSpec(tuple(a.shape), lambda i: (0,) * a.ndim)
    return pl.pallas_call(
        body, name="gmlp_mid_fwd", grid=(t // w,),
        in_specs=[pl.BlockSpec((N_DEV, w, n), lambda i: (0, i, 0)),
                  whole(b_in), whole(ln_g), whole(ln_b), whole(w_s), whole(b_s)],
        out_specs=pl.BlockSpec((half, w, n), lambda i: (0, i, 0)),
        out_shape=_sds((half, t, n), BF16),
        compiler_params=_params(("parallel",)),
    )(zpre, b_in, ln_g, ln_b, w_s, b_s)


def _gmlp_mid_bwd(zpre, dgated, b_in, ln_g, ln_b, w_s, b_s):
    _, t, n = zpre.shape
    half = N_DEV // 2
    gd = half * n // GMLP_GROUPS
    per = n // gd
    w = GMLP_WINDOW
    width = half * n

    def body(z_ref, dg_ref, bin_ref, lng_ref, lnb_ref, ws_ref, bs_ref,
             dz_ref, dws_ref, dbs_ref, dlng_ref, dlnb_ref, dbin_ref, du_ref, dvn_ref):
        @pl.when(pl.program_id(0) == 0)
        def _():
            for r in (dws_ref, dbs_ref, dlng_ref, dlnb_ref, dbin_ref):
                r[...] = jnp.zeros_like(r)

        z, u, xhat, rstd, vn = _gmlp_recompute(z_ref, bin_ref, lng_ref, lnb_ref)
        mask = _window_mask()
        for g in range(GMLP_GROUPS):
            sh, c0 = g // per, (g % per) * gd
            wsm = jnp.where(mask, ws_ref[g], 0.0).astype(BF16)
            vg = vn[sh][:, c0:c0 + gd].astype(BF16)
            s = jnp.dot(wsm, vg, preferred_element_type=F32) + bs_ref[g]
            dgt = dg_ref[sh, :, c0:c0 + gd]
            ds = dgt * u[sh][:, c0:c0 + gd]
            du_ref[sh, :, c0:c0 + gd] = dgt * s
            dsb = ds.astype(BF16)
            dws = lax.dot_general(dsb, vg, (((1,), (1,)), ((), ())), preferred_element_type=F32)
            dws_ref[g] += jnp.where(mask, dws, 0.0)
            dbs_ref[g] += jnp.sum(ds, axis=-1, keepdims=True)
            dvn_ref[sh, :, c0:c0 + gd] = lax.dot_general(wsm, dsb, (((0,), (0,)), ((), ())),
                                                         preferred_element_type=F32)
        dvn = dvn_ref[...]
        dlng_ref[...] += jnp.sum(dvn * xhat, axis=1, keepdims=True)
        dlnb_ref[...] += jnp.sum(dvn, axis=1, keepdims=True)
        dxh = dvn * lng_ref[...]
        m1 = jnp.sum(jnp.sum(dxh, axis=0), axis=-1, keepdims=True) / width
        m2 = jnp.sum(jnp.sum(dxh * xhat, axis=0), axis=-1, keepdims=True) / width
        dv = rstd * (dxh - m1 - xhat * m2)
        gg = _gelu_grad(z)
        dzu = du_ref[...] * gg[:half]
        dzv = dv * gg[half:]
        dz_ref[:half] = dzu.astype(BF16)
        dz_ref[half:] = dzv.astype(BF16)
        dbin_ref[:half] += jnp.sum(dzu, axis=1, keepdims=True)
        dbin_ref[half:] += jnp.sum(dzv, axis=1, keepdims=True)

    whole = lambda a: pl.BlockSpec(tuple(a.shape), lambda i: (0,) * a.ndim)
    outs = [_sds((N_DEV, t, n), BF16), _sds(w_s.shape, F32), _sds(b_s.shape, F32),
            _sds(ln_g.shape, F32), _sds(ln_b.shape, F32), _sds(b_in.shape, F32)]
    return pl.pallas_call(
        body, name="gmlp_mid_bwd", grid=(t // w,),
        in_specs=[pl.BlockSpec((N_DEV, w, n), lambda i: (0, i, 0)),
                  pl.BlockSpec((half, w, n), lambda i: (0, i, 0)),
                  whole(b_in), whole(ln_g), whole(ln_b), whole(w_s), whole(b_s)],
        out_specs=[pl.BlockSpec((N_DEV, w, n), lambda i: (0, i, 0))] + [whole(o) for o in outs[1:]],
        out_shape=outs,
        scratch_shapes=[pltpu.VMEM((half, w, n), F32), pltpu.VMEM((half, w, n), F32)],
        compiler_params=_params(("arbitrary",)),
    )(zpre, dgated, b_in, ln_g, ln_b, w_s, b_s)


def _attn_scores(q_ref, kv_ref, bias_ref, h, hd, start, valid):
    per = kv_ref.shape[-1] // hd
    sh, c0 = h // per, (h % per) * hd
    qh = q_ref[:, h * hd:(h + 1) * hd]
    kb = kv_ref[sh, pl.ds(start, BAND), c0:c0 + hd]
    sc = lax.dot_general(qh, kb, (((1,), (1,)), ((), ())), preferred_element_type=F32)
    sc = sc * (hd ** -0.5) + bias_ref[h]
    sc = jnp.where(valid, sc, -jnp.inf)
    sc = sc - jnp.max(sc, axis=-1, keepdims=True)
    e = jnp.exp(sc)
    return e / jnp.sum(e, axis=-1, keepdims=True), qh, kb, (sh, c0)


def _band_valid(start):
    r = lax.broadcasted_iota(jnp.int32, (1, BAND), 1)
    return (start - LEFT_PAD + r) >= 0


def _attn_fwd(q, kvp, bias):
    t, d = q.shape
    hd = d // N_HEADS
    half = N_DEV // 2

    def body(q_ref, kv_ref, bias_ref, o_ref):
        start = pl.multiple_of(pl.program_id(0) * CHUNK, CHUNK)
        valid = _band_valid(start)
        for h in range(N_HEADS):
            p, _, _, (sh, c0) = _attn_scores(q_ref, kv_ref, bias_ref, h, hd, start, valid)
            vb = kv_ref[half + sh, pl.ds(start, BAND), c0:c0 + hd]
            o_ref[:, h * hd:(h + 1) * hd] = jnp.dot(p.astype(BF16), vb,
                                                    preferred_element_type=F32).astype(BF16)

    return pl.pallas_call(
        body, name="attn_fwd", grid=(t // CHUNK,),
        in_specs=[pl.BlockSpec((CHUNK, d), lambda i: (i, 0)), _VMEM, _VMEM],
        out_specs=pl.BlockSpec((CHUNK, d), lambda i: (i, 0)),
        out_shape=_sds((t, d), BF16),
        compiler_params=_params(("arbitrary",)),
    )(q, kvp, bias)


def _attn_bwd(q, dout, kvp, bias):
    t, d = q.shape
    hd = d // N_HEADS
    half = N_DEV // 2
    scale = hd ** -0.5

    def body(q_ref, do_ref, kv_ref, bias_ref, dq_ref, dkv_ref, dbias_ref):
        @pl.when(pl.program_id(0) == 0)
        def _():
            dkv_ref[...] = jnp.zeros_like(dkv_ref)
            dbias_ref[...] = jnp.zeros_like(dbias_ref)

        start = pl.multiple_of(pl.program_id(0) * CHUNK, CHUNK)
        valid = _band_valid(start)
        for h in range(N_HEADS):
            p, qh, kb, (sh, c0) = _attn_scores(q_ref, kv_ref, bias_ref, h, hd, start, valid)
            vb = kv_ref[half + sh, pl.ds(start, BAND), c0:c0 + hd]
            doh = do_ref[:, h * hd:(h + 1) * hd]
            dp = lax.dot_general(doh, vb, (((1,), (1,)), ((), ())), preferred_element_type=F32)
            ds = p * (dp - jnp.sum(dp * p, axis=-1, keepdims=True))
            dbias_ref[h] += ds
            dsb = (ds * scale).astype(BF16)
            dq_ref[:, h * hd:(h + 1) * hd] = jnp.dot(dsb, kb, preferred_element_type=F32).astype(BF16)
            dkv_ref[sh, pl.ds(start, BAND), c0:c0 + hd] += lax.dot_general(
                dsb, qh, (((0,), (0,)), ((), ())), preferred_element_type=F32)
            dkv_ref[half + sh, pl.ds(start, BAND), c0:c0 + hd] += lax.dot_general(
                p.astype(BF16), doh, (((0,), (0,)), ((), ())), preferred_element_type=F32)

    return pl.pallas_call(
        body, name="attn_bwd", grid=(t // CHUNK,),
        in_specs=[pl.BlockSpec((CHUNK, d), lambda i: (i, 0)), pl.BlockSpec((CHUNK, d), lambda i: (i, 0)),
                  _VMEM, _VMEM],
        out_specs=[pl.BlockSpec((CHUNK, d), lambda i: (i, 0)), _VMEM, _VMEM],
        out_shape=[_sds((t, d), BF16), _sds(kvp.shape, F32), _sds(bias.shape, F32)],
        compiler_params=_params(("arbitrary",)),
    )(q, dout, kvp, bias)


def _rel_index_onehot(t):
    r = lax.broadcasted_iota(jnp.int32, (BAND, N_REL), 0)
    i = lax.broadcasted_iota(jnp.int32, (BAND, N_REL), 1)
    idx = jnp.clip(t + LEFT_PAD - r, -(CHUNK - 1), MAX_REL) + (CHUNK - 1)
    return (idx == i).astype(BF16)


def _rel_bias_grad(dbias_t):
    def body(d_ref, o_ref):
        t = pl.program_id(0)

        @pl.when(t == 0)
        def _():
            o_ref[...] = jnp.zeros_like(o_ref)

        oh = _rel_index_onehot(t)
        dv = d_ref[...]
        hi = dv.astype(BF16)
        rest = dv - hi.astype(F32)
        mid = rest.astype(BF16)
        lo = (rest - mid.astype(F32)).astype(BF16)
        acc = jnp.dot(hi, oh, preferred_element_type=F32)
        acc += jnp.dot(mid, oh, preferred_element_type=F32)
        acc += jnp.dot(lo, oh, preferred_element_type=F32)
        o_ref[...] += acc

    return pl.pallas_call(
        body, name="rel_bias_grad", grid=(CHUNK,),
        in_specs=[pl.BlockSpec((None, N_HEADS, BAND), lambda i: (i, 0, 0))],
        out_specs=pl.BlockSpec((N_HEADS, N_REL), lambda i: (0, 0)),
        out_shape=_sds((N_HEADS, N_REL), F32),
        compiler_params=_params(("arbitrary",)),
    )(dbias_t)


def _sum_parts(parts):
    s_n, rows, c = parts.shape
    br = _row_block(rows, OPT_ROW_BLOCK)

    def body(p_ref, o_ref):
        acc = p_ref[0].astype(F32)
        for s in range(1, s_n):
            acc = acc + p_ref[s].astype(F32)
        o_ref[...] = acc

    return pl.pallas_call(
        body, name="sum_parts", grid=(rows // br,),
        in_specs=[pl.BlockSpec((s_n, br, c), lambda i: (0, i, 0))],
        out_specs=pl.BlockSpec((br, c), lambda i: (i, 0)),
        out_shape=_sds((rows, c), F32),
        compiler_params=_params(("parallel",)),
    )(parts)


def _adamw(parts, w, m, v):
    s_n, rows, c = parts.shape
    br = _row_block(rows, OPT_ROW_BLOCK)
    m_corr = 1.0 - ADAM_B1 ** ADAM_STEP
    v_corr = 1.0 - ADAM_B2 ** ADAM_STEP

    def body(p_ref, w_ref, m_ref, v_ref, g_ref, d_ref, nm_ref, nv_ref):
        g = p_ref[0].astype(F32)
        for s in range(1, s_n):
            g = g + p_ref[s].astype(F32)
        nm = ADAM_B1 * m_ref[...] + (1.0 - ADAM_B1) * g
        nv = ADAM_B2 * v_ref[...] + (1.0 - ADAM_B2) * (g * g)
        g_ref[...] = g
        nm_ref[...] = nm
        nv_ref[...] = nv
        d_ref[...] = -ADAM_LR * ((nm / m_corr) / (jnp.sqrt(nv / v_corr) + ADAM_EPS) + ADAM_WD * w_ref[...])

    tile = pl.BlockSpec((br, c), lambda i: (i, 0))
    out = _sds((rows, c), F32)
    return pl.pallas_call(
        body, name="adamw", grid=(rows // br,),
        in_specs=[pl.BlockSpec((s_n, br, c), lambda i: (0, i, 0)), tile, tile, tile],
        out_specs=[tile, tile, tile, tile],
        out_shape=[out, out, out, out],
        compiler_params=_params(("parallel",)),
    )(parts, w, m, v)


def _position():
    return tuple(lax.axis_index(a) for a in MESH_AXES)


def _linear(px, py, pc):
    return 4 * px + 2 * py + pc


def _all_gather_small(v):
    rows, lanes = v.shape

    def body(x_ref, out_ref, send_sems, recv_sems, local_sem):
        x, y, c = _position()
        me, sibling = (x, y, c), (x, y, 1 - c)
        chips = [(1 - x, y), (x, 1 - y), (1 - x, 1 - y)]

        def copy(k, block, to, src=None):
            dst = out_ref.at[_linear(*block)]
            return pltpu.make_async_remote_copy(
                src_ref=dst if src is None else src, dst_ref=dst,
                send_sem=send_sems.at[k], recv_sem=recv_sems.at[k],
                device_id=to, device_id_type=MESH_ID)

        mine = pltpu.make_async_copy(x_ref, out_ref.at[_linear(*me)], local_sem)
        mine.start()
        first = [copy(0, me, sibling, src=x_ref)]
        first += [copy(1 + j, me, (*chip, c), src=x_ref) for j, chip in enumerate(chips)]
        for cp in first:
            cp.start()
        passed = [copy(4 + j, (*chip, c), sibling) for j, chip in enumerate(chips)]
        for j, chip in enumerate(chips):
            copy(1 + j, (*chip, c), me).wait_recv()
            passed[j].start()
        copy(0, sibling, me).wait_recv()
        for j, chip in enumerate(chips):
            copy(4 + j, (*chip, 1 - c), me).wait_recv()
        for cp in first + passed:
            cp.wait_send()
        mine.wait()

    return pl.pallas_call(
        body, name="all_gather_small",
        out_shape=_sds((N_DEV, rows, lanes), v.dtype),
        in_specs=[_VMEM], out_specs=_VMEM,
        scratch_shapes=[pltpu.SemaphoreType.DMA((7,)), pltpu.SemaphoreType.DMA((7,)),
                        pltpu.SemaphoreType.DMA],
        compiler_params=pltpu.CompilerParams(vmem_limit_bytes=VMEM_LIMIT),
    )(v)


def _all_gather_weights(shards):
    n = len(shards)

    def body(*refs):
        ins, outs = refs[:n], refs[n:2 * n]
        send_sems, recv_sems, local_sems = refs[2 * n:]
        x, y, c = _position()
        me, sibling = (x, y, c), (x, y, 1 - c)
        chips = [(1 - x, y), (x, 1 - y), (1 - x, 1 - y)]

        def copy(i, k, block, to, src=None):
            dst = outs[i].at[_linear(*block)]
            return pltpu.make_async_remote_copy(
                src_ref=dst if src is None else src, dst_ref=dst,
                send_sem=send_sems.at[i, k], recv_sem=recv_sems.at[i, k],
                device_id=to, device_id_type=MESH_ID)

        mine = [pltpu.make_async_copy(ins[i], outs[i].at[_linear(*me)], local_sems.at[i]) for i in range(n)]
        for cp in mine:
            cp.start()
        first = []
        for i in range(n):
            first.append(copy(i, 0, me, sibling, src=ins[i]))
            first += [copy(i, 1 + j, me, (*chip, c), src=ins[i]) for j, chip in enumerate(chips)]
        for cp in first:
            cp.start()
        passed = []
        for i in range(n):
            for j, chip in enumerate(chips):
                copy(i, 1 + j, (*chip, c), me).wait_recv()
                fwd = copy(i, 4 + j, (*chip, c), sibling)
                fwd.start()
                passed.append(fwd)
        for i in range(n):
            copy(i, 0, sibling, me).wait_recv()
            for j, chip in enumerate(chips):
                copy(i, 4 + j, (*chip, 1 - c), me).wait_recv()
        for cp in first + passed:
            cp.wait_send()
        for cp in mine:
            cp.wait()

    return pl.pallas_call(
        body, name="all_gather_weights",
        out_shape=[_sds((N_DEV,) + s.shape, s.dtype) for s in shards],
        in_specs=[_ANY] * n, out_specs=[_ANY] * n,
        scratch_shapes=[pltpu.SemaphoreType.DMA((n, 7)), pltpu.SemaphoreType.DMA((n, 7)),
                        pltpu.SemaphoreType.DMA((n,))],
    )(*shards)


_FLIPS = [(0, 0, 1), (1, 0, 0), (0, 1, 0), (1, 1, 0), (1, 0, 1), (0, 1, 1), (1, 1, 1)]


def _exchange_partials(parts):
    n = len(parts)

    def body(*refs):
        ins, outs = refs[:n], refs[n:2 * n]
        send_sems, recv_sems, local_sems = refs[2 * n:]
        x, y, c = _position()
        me = _linear(x, y, c)
        peers = [(x if fx == 0 else 1 - x, y if fy == 0 else 1 - y, c if fc == 0 else 1 - c)
                 for fx, fy, fc in _FLIPS]

        def copy(i, k):
            peer = _linear(*peers[k])
            return pltpu.make_async_remote_copy(
                src_ref=ins[i].at[peer], dst_ref=outs[i].at[me],
                send_sem=send_sems.at[i, k], recv_sem=recv_sems.at[i, k],
                device_id=peers[k], device_id_type=MESH_ID)

        def landed(i, k):
            peer = _linear(*peers[k])
            return pltpu.make_async_remote_copy(
                src_ref=ins[i].at[peer], dst_ref=outs[i].at[peer],
                send_sem=send_sems.at[i, k], recv_sem=recv_sems.at[i, k],
                device_id=peers[k], device_id_type=MESH_ID)

        mine = [pltpu.make_async_copy(ins[i].at[me], outs[i].at[me], local_sems.at[i]) for i in range(n)]
        for cp in mine:
            cp.start()
        sent = [copy(i, k) for i in range(n) for k in range(len(_FLIPS))]
        for cp in sent:
            cp.start()
        for i in range(n):
            for k in range(len(_FLIPS)):
                landed(i, k).wait_recv()
        for cp in sent:
            cp.wait_send()
        for cp in mine:
            cp.wait()

    return pl.pallas_call(
        body, name="exchange_partials",
        out_shape=[_sds(p.shape, p.dtype) for p in parts],
        in_specs=[_ANY] * n, out_specs=[_ANY] * n,
        scratch_shapes=[pltpu.SemaphoreType.DMA((n, 7)), pltpu.SemaphoreType.DMA((n, 7)),
                        pltpu.SemaphoreType.DMA((n,))],
    )(*parts)


def _pack(arrs):
    flat = jnp.concatenate([a.reshape(-1).astype(F32) for a in arrs])
    pad = (-flat.shape[0]) % (SUBLANES * LANES)
    if pad:
        flat = jnp.concatenate([flat, jnp.zeros((pad,), F32)])
    return flat.reshape(-1, LANES)


def _unpack(packed, shapes, lead=()):
    flat = packed.reshape(lead + (-1,))
    out, off = [], 0
    for s in shapes:
        size = int(np.prod(s))
        out.append(flat[..., off:off + size].reshape(lead + tuple(s)))
        off += size
    return out


def _unshard_last(g):
    nd = g.ndim
    perm = tuple(range(1, nd - 1)) + (0, nd - 1)
    t = jnp.transpose(g, perm)
    return t.reshape(t.shape[:-2] + (N_DEV * g.shape[-1],))


def kernel(x, c, w_ada, b_ada, ln_g, ln_b, ffn_gu, ffn_down, gmlp_w_in, gmlp_b_in, gmlp_ln_g, gmlp_ln_b, gmlp_w_s, gmlp_b_s, gmlp_w_out, w_ada_kv, b_ada_kv, w_kv, attn_w_q, attn_rel_bias, attn_w_o, loss_target, m_w_ada, m_b_ada, m_ln_g, m_ln_b, m_ffn_gu, m_ffn_down, m_gmlp_w_in, m_gmlp_b_in, m_gmlp_ln_g, m_gmlp_ln_b, m_gmlp_w_s, m_gmlp_b_s, m_gmlp_w_out, m_w_ada_kv, m_b_ada_kv, m_w_kv, m_attn_w_q, m_attn_rel_bias, m_attn_w_o, v_w_ada, v_b_ada, v_ln_g, v_ln_b, v_ffn_gu, v_ffn_down, v_gmlp_w_in, v_gmlp_b_in, v_gmlp_ln_g, v_gmlp_ln_b, v_gmlp_w_s, v_gmlp_b_s, v_gmlp_w_out, v_w_ada_kv, v_b_ada_kv, v_w_kv, v_attn_w_q, v_attn_rel_bias, v_attn_w_o):
    weights = dict(w_ada=w_ada, b_ada=b_ada, ln_g=ln_g, ln_b=ln_b, ffn_gu=ffn_gu, ffn_down=ffn_down,
                   gmlp_w_in=gmlp_w_in, gmlp_b_in=gmlp_b_in, gmlp_ln_g=gmlp_ln_g, gmlp_ln_b=gmlp_ln_b,
                   gmlp_w_s=gmlp_w_s, gmlp_b_s=gmlp_b_s, gmlp_w_out=gmlp_w_out, w_ada_kv=w_ada_kv,
                   b_ada_kv=b_ada_kv, w_kv=w_kv, attn_w_q=attn_w_q, attn_rel_bias=attn_rel_bias,
                   attn_w_o=attn_w_o)
    mom1 = dict(w_ada=m_w_ada, b_ada=m_b_ada, ln_g=m_ln_g, ln_b=m_ln_b, ffn_gu=m_ffn_gu, ffn_down=m_ffn_down,
                gmlp_w_in=m_gmlp_w_in, gmlp_b_in=m_gmlp_b_in, gmlp_ln_g=m_gmlp_ln_g, gmlp_ln_b=m_gmlp_ln_b,
                gmlp_w_s=m_gmlp_w_s, gmlp_b_s=m_gmlp_b_s, gmlp_w_out=m_gmlp_w_out, w_ada_kv=m_w_ada_kv,
                b_ada_kv=m_b_ada_kv, w_kv=m_w_kv, attn_w_q=m_attn_w_q, attn_rel_bias=m_attn_rel_bias,
                attn_w_o=m_attn_w_o)
    mom2 = dict(w_ada=v_w_ada, b_ada=v_b_ada, ln_g=v_ln_g, ln_b=v_ln_b, ffn_gu=v_ffn_gu, ffn_down=v_ffn_down,
                gmlp_w_in=v_gmlp_w_in, gmlp_b_in=v_gmlp_b_in, gmlp_ln_g=v_gmlp_ln_g, gmlp_ln_b=v_gmlp_ln_b,
                gmlp_w_s=v_gmlp_w_s, gmlp_b_s=v_gmlp_b_s, gmlp_w_out=v_gmlp_w_out, w_ada_kv=v_w_ada_kv,
                b_ada_kv=v_b_ada_kv, w_kv=v_w_kv, attn_w_q=v_attn_w_q, attn_rel_bias=v_attn_rel_bias,
                attn_w_o=v_attn_w_o)
    order = list(weights)

    x = x[0]
    target = loss_target[0]
    t, d = x.shape
    n_mod = w_ada.shape[-1] * N_DEV // d
    mod_w = w_ada.shape[-1]
    kv_w = w_ada_kv.shape[-1]
    n_b = DEPTH - N_A
    me = _linear(*_position())

    l2 = DEPTH * 2
    big = dict(
        ffn_gu=ffn_gu.reshape((l2,) + ffn_gu.shape[2:]),
        ffn_down=ffn_down.reshape((l2,) + ffn_down.shape[2:]),
        gmlp_w_in=gmlp_w_in, gmlp_w_out=gmlp_w_out, w_kv=w_kv[None],
        attn_w_q=attn_w_q, attn_w_o=attn_w_o)
    big_names = list(big)
    gathered = dict(zip(big_names, _all_gather_weights([big[k].astype(BF16) for k in big_names])))
    partial = {k: lax.empty(gathered[k].shape, BF16) for k in big_names}

    c_all = _all_gather_small(_pack([c]))
    c_all = _unpack(c_all, [(d,)], lead=(N_DEV,))[0]
    c4 = _as4(c_all)
    mod_part = _matmul("ada_fwd", c4, w_ada[:, None], (DEPTH, 1, N_DEV, mod_w), F32, a_silu=True)
    kv_part = _matmul("ada_kv_fwd", c4, _as4(w_ada_kv), (1, 1, N_DEV, kv_w), F32, a_silu=True)
    small_shapes = [mod_part.shape, kv_part.shape, ln_g.shape, ln_b.shape, gmlp_b_in.shape,
                    gmlp_ln_g.shape, gmlp_ln_b.shape, attn_rel_bias.shape]
    small = _all_gather_small(_pack([mod_part, kv_part, ln_g, ln_b, gmlp_b_in, gmlp_ln_g, gmlp_ln_b,
                                     attn_rel_bias]))
    (mod_g, kvm_g, ln_g_g, ln_b_g, b_in_g, gln_g_g, gln_b_g, rel_g) = _unpack(small, small_shapes, lead=(N_DEV,))
    mod_mine = lax.dynamic_index_in_dim(mod_g[:, :, 0], me, axis=2, keepdims=False)
    mod = _unshard_last(mod_mine) + b_ada
    mod = mod.reshape(DEPTH, n_mod, 1, d)
    kvm_mine = lax.dynamic_index_in_dim(kvm_g[:, 0, 0], me, axis=1, keepdims=False)
    mkv = (_unshard_last(kvm_mine) + b_ada_kv).reshape(2, 1, d)
    ln_g_f = _unshard_last(ln_g_g)
    ln_b_f = _unshard_last(ln_b_g)
    half = N_DEV // 2
    b_in_f = jnp.transpose(b_in_g, (1, 0, 2))[:, :, None, :]
    gln_g_f = _unshard_last(gln_g_g).reshape(N_A, half, 1, -1)
    gln_b_f = _unshard_last(gln_b_g).reshape(N_A, half, 1, -1)
    rel_f = _unshard_last(rel_g)
    tq = np.arange(CHUNK)
    rb = np.arange(BAND)
    rel_idx = np.clip(tq[:, None] + LEFT_PAD - rb[None, :], -(CHUNK - 1), MAX_REL) + (CHUNK - 1)

    def shard_act(a):
        return a.reshape(a.shape[0], a.shape[2], a.shape[3])

    def ffn_fwd(h, lw):
        n = gathered["ffn_gu"].shape[-1]
        gu = _matmul("ffn_gu_fwd", _as4(h), gathered["ffn_gu"], (N_DEV, 1, t, n), F32, lb=lw)
        a = _swiglu_fwd(shard_act(gu))
        y = _matmul("ffn_down_fwd", a[:, None], gathered["ffn_down"], (1, 1, t, d), F32, lb=lw,
                    b_merge=2, reduce=True)
        return y[0, 0], (gu, a)

    def ffn_bwd(dy, h, saved, lw):
        gu, a = saved
        n = gathered["ffn_gu"].shape[-1]
        da = _matmul("ffn_down_bwd_a", _as4(dy), gathered["ffn_down"], (half, 1, t, n), F32, lb=lw,
                     b_merge=2, tb=True)
        partial["ffn_down"] = _matmul("ffn_down_bwd_w", a[:, None], _as4(dy), partial["ffn_down"].shape, BF16,
                                      ta=True, lo=lw, out_merge=2, out_buf=partial["ffn_down"])
        dgu = _swiglu_bwd(shard_act(gu), shard_act(da))
        dh = _matmul("ffn_gu_bwd_a", dgu[:, None], gathered["ffn_gu"], (1, 1, t, d), F32, lb=lw,
                     tb=True, reduce=True)
        partial["ffn_gu"] = _matmul("ffn_gu_bwd_w", _as4(h), dgu[:, None], partial["ffn_gu"].shape, BF16,
                                    ta=True, lo=lw, out_buf=partial["ffn_gu"])
        return dh[0, 0], {}

    def gmlp_params(l):
        return (b_in_f[l], gln_g_f[l], gln_b_f[l], gmlp_w_s[l], gmlp_b_s[l][:, :, None])

    def gmlp_fwd(h, l):
        n = gathered["gmlp_w_in"].shape[-1]
        zpre = _matmul("gmlp_in_fwd", _as4(h), gathered["gmlp_w_in"], (N_DEV, 1, t, n), F32, lb=l)
        gated = _gmlp_mid_fwd(shard_act(zpre), *gmlp_params(l))
        y = _matmul("gmlp_out_fwd", gated[:, None], gathered["gmlp_w_out"], (1, 1, t, d), F32, lb=l,
                    b_merge=2, reduce=True)
        return y[0, 0], (zpre, gated)

    def gmlp_bwd(dy, h, saved, l):
        zpre, gated = saved
        n = gathered["gmlp_w_in"].shape[-1]
        dgated = _matmul("gmlp_out_bwd_a", _as4(dy), gathered["gmlp_w_out"], (half, 1, t, n), F32, lb=l,
                         b_merge=2, tb=True)
        partial["gmlp_w_out"] = _matmul("gmlp_out_bwd_w", gated[:, None], _as4(dy), partial["gmlp_w_out"].shape,
                                        BF16, ta=True, lo=l, out_merge=2, out_buf=partial["gmlp_w_out"])
        dz, dws, dbs, dlng, dlnb, dbin = _gmlp_mid_bwd(shard_act(zpre), shard_act(dgated), *gmlp_params(l))
        dh = _matmul("gmlp_in_bwd_a", dz[:, None], gathered["gmlp_w_in"], (1, 1, t, d), F32, lb=l,
                     tb=True, reduce=True)
        partial["gmlp_w_in"] = _matmul("gmlp_in_bwd_w", _as4(h), dz[:, None], partial["gmlp_w_in"].shape, BF16,
                                       ta=True, lo=l, out_buf=partial["gmlp_w_in"])
        small_grads = dict(gmlp_w_s=dws, gmlp_b_s=dbs[:, :, 0], gmlp_ln_g=dlng.reshape(-1),
                           gmlp_ln_b=dlnb.reshape(-1), gmlp_b_in=dbin.reshape(-1))
        return dh[0, 0], small_grads

    def attn_fwd(h, j, kvp):
        bias = rel_f[j][:, rel_idx]
        q = _matmul("attn_q_fwd", _as4(h), gathered["attn_w_q"], (1, 1, t, d), BF16, lb=j,
                    b_merge=N_DEV, reduce=True)[0, 0]
        o = _attn_fwd(q, kvp, bias)
        y = _matmul("attn_o_fwd", _as4(o), gathered["attn_w_o"], (1, 1, t, d), F32, lb=j,
                    b_merge=N_DEV, reduce=True)
        return y[0, 0], (q, o, bias)

    def attn_bwd(dy, h, saved, j, kvp):
        q, o, bias = saved
        do = _matmul("attn_o_bwd_a", _as4(dy), gathered["attn_w_o"], (1, 1, t, d), BF16, lb=j,
                     b_merge=N_DEV, tb=True)[0, 0]
        partial["attn_w_o"] = _matmul("attn_o_bwd_w", _as4(o), _as4(dy), partial["attn_w_o"].shape, BF16,
                                      ta=True, lo=j, out_merge=N_DEV, out_buf=partial["attn_w_o"])
        dq, dkvp, dbias = _attn_bwd(q, do, kvp, bias)
        drel = _rel_bias_grad(jnp.transpose(dbias, (1, 0, 2)))
        dh = _matmul("attn_q_bwd_a", _as4(dq), gathered["attn_w_q"], (1, 1, t, d), F32, lb=j,
                     b_merge=N_DEV, tb=True)
        partial["attn_w_q"] = _matmul("attn_q_bwd_w", _as4(h), _as4(dq), partial["attn_w_q"].shape, BF16,
                                      ta=True, lo=j, out_merge=N_DEV, out_buf=partial["attn_w_q"])
        return dh[0, 0], dict(attn_rel_bias=drel, dkvp=dkvp)

    tape = []
    kvp = None
    kv_tape = None
    for l in range(DEPTH):
        for i in range(3):
            shift, scl, gate = mod[l, 3 * i], mod[l, 3 * i + 1], mod[l, 3 * i + 2]
            wgt = 1.0 if i == 1 else 0.5
            gw = wgt * (1.0 + gate)
            h = _modulate(x, scl, shift)
            if i != 1:
                y, saved = ffn_fwd(h, 2 * l + i // 2)
            elif l < N_A:
                y, saved = gmlp_fwd(h, l)
            else:
                y, saved = attn_fwd(h, l - N_A, kvp)
            x_new = _ln_res_fwd(x, y, gw, ln_g_f[l, i][None], ln_b_f[l, i][None])
            tape.append((x, h, y, gw, scl, saved))
            x = x_new
        if l == N_A - 1:
            hkv = _modulate(x, mkv[1], mkv[0])
            n = gathered["w_kv"].shape[-1]
            kv = _matmul("kv_fwd", _as4(hkv), gathered["w_kv"], (N_DEV, 1, t, n), BF16)
            kvp = jnp.pad(shard_act(kv), ((0, 0), (LEFT_PAD, 0), (0, 0)))
            kv_tape = (x, hkv)

    loss_part, dx = _loss_head(x, target)
    loss = lax.psum(loss_part[0, 0], MESH_AXES)

    d_mod = [[None] * n_mod for _ in range(DEPTH)]
    d_ln_g = [[None] * 3 for _ in range(DEPTH)]
    d_ln_b = [[None] * 3 for _ in range(DEPTH)]
    small_grads = {k: [None] * N_A for k in ("gmlp_w_s", "gmlp_b_s", "gmlp_ln_g", "gmlp_ln_b", "gmlp_b_in")}
    d_rel = [None] * n_b
    dkvp_sum = None
    d_mkv = None
    for l in reversed(range(DEPTH)):
        if l == N_A - 1:
            x_kv, hkv = kv_tape
            n = gathered["w_kv"].shape[-1]
            dkv = dkvp_sum[:, LEFT_PAD:, :].astype(BF16)[:, None]
            dhkv = _matmul("kv_bwd_a", dkv, gathered["w_kv"], (1, 1, t, d), F32, tb=True, reduce=True)[0, 0]
            partial["w_kv"] = _matmul("kv_bwd_w", _as4(hkv), dkv, partial["w_kv"].shape, BF16, ta=True,
                                      out_buf=partial["w_kv"])
            dx, ds_kv, db_kv = _mod_bwd(dx, dhkv, x_kv, mkv[1])
            d_mkv = jnp.concatenate([db_kv[0], ds_kv[0]])
        for i in reversed(range(3)):
            x_in, h, y, gw, scl, saved = tape[3 * l + i]
            wgt = 1.0 if i == 1 else 0.5
            dx_res, dy, dgw, dg, db = _ln_res_bwd(x_in, y, gw, ln_g_f[l, i][None], dx)
            d_ln_g[l][i], d_ln_b[l][i] = dg[0], db[0]
            if i != 1:
                dh, extra = ffn_bwd(dy, h, saved, 2 * l + i // 2)
            elif l < N_A:
                dh, extra = gmlp_bwd(dy, h, saved, l)
                for k, g in extra.items():
                    small_grads[k][l] = g
            else:
                dh, extra = attn_bwd(dy, h, saved, l - N_A, kvp)
                d_rel[l - N_A] = extra["attn_rel_bias"]
                dkvp_sum = extra["dkvp"] if dkvp_sum is None else _add_kv(dkvp_sum, extra["dkvp"])
            dx, dscl, dshift = _mod_bwd(dx_res, dh, x_in, scl)
            d_mod[l][3 * i], d_mod[l][3 * i + 1], d_mod[l][3 * i + 2] = dshift[0], dscl[0], wgt * dgw[0]
    grad_x = dx[None]

    d_mod_arr = jnp.stack([jnp.concatenate(r) for r in d_mod])
    small_part = dict(
        b_ada=d_mod_arr, b_ada_kv=d_mkv,
        ln_g=jnp.stack([jnp.stack(r) for r in d_ln_g]), ln_b=jnp.stack([jnp.stack(r) for r in d_ln_b]),
        gmlp_b_in=jnp.stack(small_grads["gmlp_b_in"]), gmlp_ln_g=jnp.stack(small_grads["gmlp_ln_g"]),
        gmlp_ln_b=jnp.stack(small_grads["gmlp_ln_b"]), gmlp_w_s=jnp.stack(small_grads["gmlp_w_s"]),
        gmlp_b_s=jnp.stack(small_grads["gmlp_b_s"]), attn_rel_bias=jnp.stack(d_rel))
    small_names = list(small_part)
    sp_shapes = [small_part[k].shape for k in small_names]
    sp_all = _all_gather_small(_pack([small_part[k] for k in small_names]))
    sp_sum = _sum_parts(sp_all)
    full_grads = dict(zip(small_names, _unpack(sp_sum, sp_shapes)))
    per_dev = dict(zip(small_names, _unpack(sp_all, sp_shapes, lead=(N_DEV,))))

    def my_cols(a, width):
        return lax.dynamic_slice_in_dim(a, me * width, width, axis=a.ndim - 1)

    grads = {}
    grads["b_ada"] = full_grads["b_ada"]
    grads["b_ada_kv"] = full_grads["b_ada_kv"]
    grads["gmlp_w_s"] = full_grads["gmlp_w_s"]
    grads["gmlp_b_s"] = full_grads["gmlp_b_s"]
    for k in ("ln_g", "ln_b", "gmlp_b_in", "gmlp_ln_g", "gmlp_ln_b", "attn_rel_bias"):
        grads[k] = my_cols(full_grads[k], weights[k].shape[-1])

    dmod_cols = jnp.transpose(my_cols(per_dev["b_ada"], mod_w), (1, 0, 2))[:, None]
    grads["w_ada"] = _matmul("ada_bwd_w", c4, dmod_cols, (DEPTH, 1, d, mod_w), F32, ta=True,
                             a_silu=True)[:, 0]
    dkv_cols = my_cols(per_dev["b_ada_kv"], kv_w)[None, None]
    grads["w_ada_kv"] = _matmul("ada_kv_bwd_w", c4, dkv_cols, (1, 1, d, kv_w), F32, ta=True,
                                a_silu=True)[0, 0]

    received = dict(zip(big_names, _exchange_partials([partial[k] for k in big_names])))

    delta, new_m, new_v = {}, {}, {}

    def update(k, parts):
        w = weights[k]
        cols = w.shape[-1]
        g, dl, nm, nv = _adamw(parts.reshape(parts.shape[0], -1, cols), w.reshape(-1, cols),
                               mom1[k].reshape(-1, cols), mom2[k].reshape(-1, cols))
        grads[k], delta[k], new_m[k], new_v[k] = (a.reshape(w.shape) for a in (g, dl, nm, nv))

    for k in big_names:
        update(k, received[k])
    update("w_ada", grads["w_ada"][None])
    update("w_ada_kv", grads["w_ada_kv"][None])

    tiny = [k for k in order if k not in delta]
    tiny_shapes = [weights[k].shape for k in tiny]
    packed = [_pack([src[k] for k in tiny])[None] if src is grads else _pack([src[k] for k in tiny])
              for src in (grads, weights, mom1, mom2)]
    outs = _adamw(*packed)
    for name, arr in zip(("g", "d", "m", "v"), outs):
        vals = _unpack(arr, tiny_shapes)
        for k, val in zip(tiny, vals):
            {"g": grads, "d": delta, "m": new_m, "v": new_v}[name][k] = val

    return (loss, grad_x, *[grads[k] for k in order], *[delta[k] for k in order],
            *[new_m[k] for k in order], *[new_v[k] for k in order])


def _add_kv(a, b):
    s, rows, n = a.shape
    return _add(a.reshape(s * rows, n), b.reshape(s * rows, n)).reshape(s, rows, n)
```

```python
import functools

import numpy as np
import jax
import jax.numpy as jnp
from jax import lax
from jax.experimental import pallas as pl
from jax.experimental.pallas import tpu as pltpu

F32 = jnp.float32
BF16 = jnp.bfloat16
MESH_AXES = ("x", "y", "c")
N_DEV = 8
MESH_ID = pl.DeviceIdType.MESH

DEPTH = 4
N_A = 2
CHUNK = 64
N_HEADS = 16
LEFT_CHUNKS = 8
BAND = (LEFT_CHUNKS + 1) * CHUNK
LEFT_PAD = LEFT_CHUNKS * CHUNK
MAX_REL = 4 * CHUNK
N_REL = (CHUNK - 1) + MAX_REL + 1
GMLP_WINDOW = 128
GMLP_GROUPS = 8
ALPHA = (2.0 * DEPTH) ** 0.25
LN_EPS = 1e-5
ADAM_LR = 0.001
ADAM_B1 = 0.9
ADAM_B2 = 0.999
ADAM_EPS = 1e-08
ADAM_WD = 0.01
ADAM_STEP = 10

V7X_VMEM_BYTES = 64 * 1024 * 1024
VMEM_LIMIT = V7X_VMEM_BYTES - 8 * 1024 * 1024
LANES = 128
SUBLANES = 8
MM_BLOCK = 1024
ROW_BLOCK = 256
OPT_ROW_BLOCK = 128

_ANY = pl.BlockSpec(memory_space=pl.ANY)
_VMEM = pl.BlockSpec(memory_space=pltpu.VMEM)


def _params(sem=None):
    return pltpu.CompilerParams(dimension_semantics=sem, vmem_limit_bytes=VMEM_LIMIT)


def _row_block(rows, target):
    for d in range(min(rows, target), 0, -1):
        if rows % d == 0 and (d % SUBLANES == 0 or d == rows):
            return d
    return rows


def _matmul(name, a, b, out_shape4, out_dtype, *, la=0, lb=0, lo=0, ta=False, tb=False,
            reduce=False, b_merge=1, out_merge=1, out_buf=None, a_silu=False):
    ja_n, _, a_r, a_c = a.shape
    jb_n, _, b_r, b_c = b.shape
    jo_n, _, o_r, o_c = out_shape4
    m_tot = a_c if ta else a_r
    k_a = a_r if ta else a_c
    b_rows = b_merge * b_r
    k_c = b_c if tb else b_rows
    n = b_rows if tb else b_c
    n_chunks = (jb_n // b_merge) if reduce else 1
    natural_k = reduce and ja_n == 1
    assert n == o_c, (name, n, o_c)
    assert k_a ==(k_c * n_chunks if natural_k else k_c), (name, k_a, k_c, n_chunks)
    bk = k_c if (k_c <= MM_BLOCK or (b_merge > 1 and not tb)) else MM_BLOCK
    assert k_c % bk == 0
    nkk = k_c // bk
    nk = n_chunks * nkk
    m_out = out_merge * o_r
    assert m_tot == m_out, (name, m_tot, m_out)
    bm = m_tot if (m_tot <= MM_BLOCK or out_merge > 1) else MM_BLOCK
    assert m_tot % bm == 0
    jo_blocks = jo_n // out_merge

    def a_index(j, m, k):
        kj, kk = k // nkk, k % nkk
        ja = 0 if ja_n == 1 else (kj if reduce else j)
        ke = kk + kj * nkk if natural_k else kk
        return (ja, la, ke, m) if ta else (ja, la, m, ke)

    def b_index(j, m, k):
        kj, kk = k // nkk, k % nkk
        jb = 0 if jb_n == b_merge else (kj if reduce else j)
        return (jb, lb, 0, kk) if tb else (jb, lb, kk, 0)

    def o_index(j, m, k):
        return (j, lo, 0, 0) if out_merge > 1 else (j, lo, m, 0)

    a_block = (None, None, bk, bm) if ta else (None, None, bm, bk)
    if b_merge > 1:
        b_block = (b_merge, None, b_r, bk if tb else n)
    else:
        b_block = (None, None, n, bk) if tb else (None, None, bk, n)
    o_block = (out_merge, None, o_r, n) if out_merge > 1 else (None, None, bm, n)
    dims = (((0 if ta else 1,), (1 if tb else 0,)), ((), ()))

    def body(a_ref, b_ref, *rest):
        o_ref, acc_ref = rest[-2], rest[-1]
        k = pl.program_id(2)

        @pl.when(k == 0)
        def _():
            acc_ref[...] = jnp.zeros_like(acc_ref)

        av = a_ref[...]
        if a_silu:
            af = av.astype(F32)
            av = af * jax.nn.sigmoid(af)
        bv = b_ref[...]
        if b_merge > 1:
            bv = bv.reshape(b_rows, bv.shape[-1])
        acc_ref[...] += lax.dot_general(av.astype(BF16), bv.astype(BF16), dims,
                                        preferred_element_type=F32)

        @pl.when(k == nk - 1)
        def _():
            ov = acc_ref[...].astype(out_dtype)
            if out_merge > 1:
                ov = ov.reshape(out_merge, o_r, n)
            o_ref[...] = ov

    in_specs = [pl.BlockSpec(a_block, a_index), pl.BlockSpec(b_block, b_index)]
    operands = [a, b]
    aliases = {}
    if out_buf is not None:
        assert out_buf.shape == tuple(out_shape4) and out_buf.dtype == out_dtype
        in_specs.append(_ANY)
        operands.append(out_buf)
        aliases = {2: 0}
    return pl.pallas_call(
        body, name=name,
        grid=(jo_blocks, m_tot // bm, nk),
        in_specs=in_specs,
        out_specs=pl.BlockSpec(o_block, o_index),
        out_shape=jax.ShapeDtypeStruct(tuple(out_shape4), out_dtype),
        scratch_shapes=[pltpu.VMEM((bm, n), F32)],
        input_output_aliases=aliases,
        compiler_params=_params(("parallel", "parallel", "arbitrary")),
    )(*operands)


def _as4(a):
    return a.reshape((1,) * (4 - a.ndim) + a.shape)


def _row_call(name, body, ins, outs, t, *, acc_outs=()):
    bt = _row_block(t, ROW_BLOCK)

    def spec(arr, tiled):
        if tiled:
            return pl.BlockSpec((bt,) + tuple(arr.shape[1:]), lambda i: (i,) + (0,) * (arr.ndim - 1))
        return pl.BlockSpec(tuple(arr.shape), lambda i: (0,) * arr.ndim)

    return pl.pallas_call(
        body, name=name, grid=(t // bt,),
        in_specs=[spec(a, tl) for a, tl in ins],
        out_specs=[spec(o, tl) for o, tl in outs],
        out_shape=[jax.ShapeDtypeStruct(o.shape, o.dtype) for o, _ in outs],
        compiler_params=_params(("arbitrary",) if acc_outs else ("parallel",)),
    )(*[a for a, _ in ins])


def _sds(shape, dtype):
    return jax.ShapeDtypeStruct(tuple(shape), dtype)


def _modulate(x, scl, shift):
    t, d = x.shape

    def body(x_ref, s_ref, b_ref, h_ref):
        h_ref[...] = (x_ref[...] * (1.0 + s_ref[...]) + b_ref[...]).astype(BF16)

    return _row_call("modulate", body, [(x, True), (scl, False), (shift, False)],
                     [(_sds((t, d), BF16), True)], t)[0]


def _ln_stats(r):
    mu = jnp.mean(r, axis=-1, keepdims=True)
    rc = r - mu
    var = jnp.mean(rc * rc, axis=-1, keepdims=True)
    rstd = lax.rsqrt(var + LN_EPS)
    return rc * rstd, rstd


def _ln_res_fwd(x, y, gw, g, b):
    t, d = x.shape

    def body(x_ref, y_ref, gw_ref, g_ref, b_ref, o_ref):
        r = ALPHA * x_ref[...] + gw_ref[...] * y_ref[...]
        xhat, _ = _ln_stats(r)
        o_ref[...] = xhat * g_ref[...] + b_ref[...]

    return _row_call("ln_res_fwd", body,
                     [(x, True), (y, True), (gw, False), (g, False), (b, False)],
                     [(_sds((t, d), F32), True)], t)[0]


def _ln_res_bwd(x, y, gw, g, dxn):
    t, d = x.shape

    def body(x_ref, y_ref, gw_ref, g_ref, dxn_ref, dx_ref, dy_ref, dgw_ref, dg_ref, db_ref):
        @pl.when(pl.program_id(0) == 0)
        def _():
            dgw_ref[...] = jnp.zeros_like(dgw_ref)
            dg_ref[...] = jnp.zeros_like(dg_ref)
            db_ref[...] = jnp.zeros_like(db_ref)

        yv = y_ref[...]
        gwv = gw_ref[...]
        dxn = dxn_ref[...]
        xhat, rstd = _ln_stats(ALPHA * x_ref[...] + gwv * yv)
        dxh = dxn * g_ref[...]
        m1 = jnp.mean(dxh, axis=-1, keepdims=True)
        m2 = jnp.mean(dxh * xhat, axis=-1, keepdims=True)
        dr = rstd * (dxh - m1 - xhat * m2)
        dx_ref[...] = ALPHA * dr
        dy_ref[...] = (gwv * dr).astype(BF16)
        dgw_ref[...] += jnp.sum(dr * yv, axis=0, keepdims=True)
        dg_ref[...] += jnp.sum(dxn * xhat, axis=0, keepdims=True)
        db_ref[...] += jnp.sum(dxn, axis=0, keepdims=True)

    vec = _sds((1, d), F32)
    return _row_call("ln_res_bwd", body,
                     [(x, True), (y, True), (gw, False), (g, False), (dxn, True)],
                     [(_sds((t, d), F32), True), (_sds((t, d), BF16), True),
                      (vec, False), (vec, False), (vec, False)], t, acc_outs=(2, 3, 4))


def _mod_bwd(dx_res, dh, x, scl):
    t, d = x.shape

    def body(dxr_ref, dh_ref, x_ref, s_ref, dx_ref, ds_ref, db_ref):
        @pl.when(pl.program_id(0) == 0)
        def _():
            ds_ref[...] = jnp.zeros_like(ds_ref)
            db_ref[...] = jnp.zeros_like(db_ref)

        dh = dh_ref[...]
        dx_ref[...] = dxr_ref[...] + dh * (1.0 + s_ref[...])
        ds_ref[...] += jnp.sum(dh * x_ref[...], axis=0, keepdims=True)
        db_ref[...] += jnp.sum(dh, axis=0, keepdims=True)

    vec = _sds((1, d), F32)
    return _row_call("mod_bwd", body, [(dx_res, True), (dh, True), (x, True), (scl, False)],
                     [(_sds((t, d), F32), True), (vec, False), (vec, False)], t, acc_outs=(1, 2))


def _add(a, b):
    rows = a.shape[0]

    def body(a_ref, b_ref, o_ref):
        o_ref[...] = a_ref[...] + b_ref[...]

    return _row_call("add", body, [(a, True), (b, True)], [(_sds(a.shape, a.dtype), True)], rows)[0]


def _loss_head(y, target):
    t, d = y.shape

    def body(y_ref, t_ref, l_ref, dy_ref):
        @pl.when(pl.program_id(0) == 0)
        def _():
            l_ref[...] = jnp.zeros_like(l_ref)

        err = y_ref[...] - t_ref[...]
        dy_ref[...] = err * (1.0 / d)
        part = 0.5 * jnp.sum(jnp.mean(err * err, axis=-1, keepdims=True), axis=0, keepdims=True)
        l_ref[...] += jnp.broadcast_to(part, l_ref.shape)

    return _row_call("loss_head", body, [(y, True), (target, True)],
                     [(_sds((SUBLANES, LANES), F32), False), (_sds((t, d), F32), True)], t,
                     acc_outs=(0,))


def _swiglu_fwd(gu):
    _, t, n = gu.shape
    half = N_DEV // 2
    bt = _row_block(t, ROW_BLOCK)
    gu4 = gu.reshape(2, half, t, n)

    def body(gu_ref, a_ref):
        g = gu_ref[0]
        a_ref[...] = (g * jax.nn.sigmoid(g) * gu_ref[1]).astype(BF16)

    return pl.pallas_call(
        body, name="swiglu_fwd", grid=(half, t // bt),
        in_specs=[pl.BlockSpec((2, None, bt, n), lambda j, i: (0, j, i, 0))],
        out_specs=pl.BlockSpec((None, bt, n), lambda j, i: (j, i, 0)),
        out_shape=_sds((half, t, n), BF16),
        compiler_params=_params(("parallel", "parallel")),
    )(gu4)


def _swiglu_bwd(gu, da):
    _, t, n = gu.shape
    half = N_DEV // 2
    bt = _row_block(t, ROW_BLOCK)
    gu4 = gu.reshape(2, half, t, n)

    def body(gu_ref, da_ref, d_ref):
        g = gu_ref[0]
        u = gu_ref[1]
        da = da_ref[...]
        sig = jax.nn.sigmoid(g)
        d_ref[0] = (da * u * sig * (1.0 + g * (1.0 - sig))).astype(BF16)
        d_ref[1] = (da * g * sig).astype(BF16)

    out = pl.pallas_call(
        body, name="swiglu_bwd", grid=(half, t // bt),
        in_specs=[pl.BlockSpec((2, None, bt, n), lambda j, i: (0, j, i, 0)),
                  pl.BlockSpec((None, bt, n), lambda j, i: (j, i, 0))],
        out_specs=pl.BlockSpec((2, None, bt, n), lambda j, i: (0, j, i, 0)),
        out_shape=_sds((2, half, t, n), BF16),
        compiler_params=_params(("parallel", "parallel")),
    )(gu4, da)
    return out.reshape(N_DEV, t, n)


_INV_SQRT2 = 0.7071067811865476
_INV_SQRT_2PI = 0.3989422804014327


def _gelu(z):
    return 0.5 * z * (1.0 + lax.erf(z * _INV_SQRT2))


def _gelu_grad(z):
    return 0.5 * (1.0 + lax.erf(z * _INV_SQRT2)) + z * jnp.exp(-0.5 * z * z) * _INV_SQRT_2PI


def _window_mask():
    t_out = lax.broadcasted_iota(jnp.int32, (GMLP_WINDOW, GMLP_WINDOW), 0)
    s_in = lax.broadcasted_iota(jnp.int32, (GMLP_WINDOW, GMLP_WINDOW), 1)
    return (s_in // CHUNK) <= (t_out // CHUNK)


def _gmlp_recompute(z_ref, bin_ref, lng_ref, lnb_ref):
    half = N_DEV // 2
    z = z_ref[...] + bin_ref[...]
    ge = _gelu(z)
    u = ge[:half]
    v = ge[half:]
    width = half * v.shape[-1]
    mu = jnp.sum(jnp.sum(v, axis=0), axis=-1, keepdims=True) / width
    vc = v - mu
    var = jnp.sum(jnp.sum(vc * vc, axis=0), axis=-1, keepdims=True) / width
    rstd = lax.rsqrt(var + LN_EPS)
    xhat = vc * rstd
    vn = xhat * lng_ref[...] + lnb_ref[...]
    return z, u, xhat, rstd, vn


def _gmlp_mid_fwd(zpre, b_in, ln_g, ln_b, w_s, b_s):
    _, t, n = zpre.shape
    half = N_DEV // 2
    gd = half * n // GMLP_GROUPS
    per = n // gd
    w = GMLP_WINDOW

    def body(z_ref, bin_ref, lng_ref, lnb_ref, ws_ref, bs_ref, o_ref):
        _, u, _, _, vn = _gmlp_recompute(z_ref, bin_ref, lng_ref, lnb_ref)
        mask = _window_mask()
        for g in range(GMLP_GROUPS):
            sh, c0 = g // per, (g % per) * gd
            wsm = jnp.where(mask, ws_ref[g], 0.0).astype(BF16)
            s = jnp.dot(wsm, vn[sh][:, c0:c0 + gd].astype(BF16), preferred_element_type=F32) + bs_ref[g]
            o_ref[sh, :, c0:c0 + gd] = (u[sh][:, c0:c0 + gd] * s).astype(BF16)

    whole = lambda a: pl.BlockSpec(tuple(a.shape), lambda i: (0,) * a.ndim)
    return pl.pallas_call(
        body, name="gmlp_mid_fwd", grid=(t // w,),
        in_specs=[pl.BlockSpec((N_DEV, w, n), lambda i: (0, i, 0)),
                  whole(b_in), whole(ln_g), whole(ln_b), whole(w_s), whole(b_s)],
        out_specs=pl.BlockSpec((half, w, n), lambda i: (0, i, 0)),
        out_shape=_sds((half, t, n), BF16),
        compiler_params=_params(("parallel",)),
    )(zpre, b_in, ln_g, ln_b, w_s, b_s)


def _gmlp_mid_bwd(zpre, dgated, b_in, ln_g, ln_b, w_s, b_s):
    _, t, n = zpre.shape
    half = N_DEV // 2
    gd = half * n // GMLP_GROUPS
    per = n // gd
    w = GMLP_WINDOW
    width = half * n

    def body(z_ref, dg_ref, bin_ref, lng_ref, lnb_ref, ws_ref, bs_ref,
             dz_ref, dws_ref, dbs_ref, dlng_ref, dlnb_ref, dbin_ref, du_ref, dvn_ref):
        @pl.when(pl.program_id(0) == 0)
        def _():
            for r in (dws_ref, dbs_ref, dlng_ref, dlnb_ref, dbin_ref):
                r[...] = jnp.zeros_like(r)

        z, u, xhat, rstd, vn = _gmlp_recompute(z_ref, bin_ref, lng_ref, lnb_ref)
        mask = _window_mask()
        for g in range(GMLP_GROUPS):
            sh, c0 = g // per, (g % per) * gd
            wsm = jnp.where(mask, ws_ref[g], 0.0).astype(BF16)
            vg = vn[sh][:, c0:c0 + gd].astype(BF16)
            s = jnp.dot(wsm, vg, preferred_element_type=F32) + bs_ref[g]
            dgt = dg_ref[sh, :, c0:c0 + gd]
            ds = dgt * u[sh][:, c0:c0 + gd]
            du_ref[sh, :, c0:c0 + gd] = dgt * s
            dsb = ds.astype(BF16)
            dws = lax.dot_general(dsb, vg, (((1,), (1,)), ((), ())), preferred_element_type=F32)
            dws_ref[g] += jnp.where(mask, dws, 0.0)
            dbs_ref[g] += jnp.sum(ds, axis=-1, keepdims=True)
            dvn_ref[sh, :, c0:c0 + gd] = lax.dot_general(wsm, dsb, (((0,), (0,)), ((), ())),
                                                         preferred_element_type=F32)
        dvn = dvn_ref[...]
        dlng_ref[...] += jnp.sum(dvn * xhat, axis=1, keepdims=True)
        dlnb_ref[...] += jnp.sum(dvn, axis=1, keepdims=True)
        dxh = dvn * lng_ref[...]
        m1 = jnp.sum(jnp.sum(dxh, axis=0), axis=-1, keepdims=True) / width
        m2 = jnp.sum(jnp.sum(dxh * xhat, axis=0), axis=-1, keepdims=True) / width
        dv = rstd * (dxh - m1 - xhat * m2)
        gg = _gelu_grad(z)
        dzu = du_ref[...] * gg[:half]
        dzv = dv * gg[half:]
        dz_ref[:half] = dzu.astype(BF16)
        dz_ref[half:] = dzv.astype(BF16)
        dbin_ref[:half] += jnp.sum(dzu, axis=1, keepdims=True)
        dbin_ref[half:] += jnp.sum(dzv, axis=1, keepdims=True)

    whole = lambda a: pl.BlockSpec(tuple(a.shape), lambda i: (0,) * a.ndim)
    outs = [_sds((N_DEV, t, n), BF16), _sds(w_s.shape, F32), _sds(b_s.shape, F32),
            _sds(ln_g.shape, F32), _sds(ln_b.shape, F32), _sds(b_in.shape, F32)]
    return pl.pallas_call(
        body, name="gmlp_mid_bwd", grid=(t // w,),
        in_specs=[pl.BlockSpec((N_DEV, w, n), lambda i: (0, i, 0)),
                  pl.BlockSpec((half, w, n), lambda i: (0, i, 0)),
                  whole(b_in), whole(ln_g), whole(ln_b), whole(w_s), whole(b_s)],
        out_specs=[pl.BlockSpec((N_DEV, w, n), lambda i: (0, i, 0))] + [whole(o) for o in outs[1:]],
        out_shape=outs,
        scratch_shapes=[pltpu.VMEM((half, w, n), F32), pltpu.VMEM((half, w, n), F32)],
        compiler_params=_params(("arbitrary",)),
    )(zpre, dgated, b_in, ln_g, ln_b, w_s, b_s)


def _attn_scores(q_ref, kv_ref, bias_ref, h, hd, start, valid):
    per = kv_ref.shape[-1] // hd
    sh, c0 = h // per, (h % per) * hd
    qh = q_ref[:, h * hd:(h + 1) * hd]
    kb = kv_ref[sh, pl.ds(start, BAND), c0:c0 + hd]
    sc = lax.dot_general(qh, kb, (((1,), (1,)), ((), ())), preferred_element_type=F32)
    sc = sc * (hd ** -0.5) + bias_ref[h]
    sc = jnp.where(valid, sc, -jnp.inf)
    sc = sc - jnp.max(sc, axis=-1, keepdims=True)
    e = jnp.exp(sc)
    return e / jnp.sum(e, axis=-1, keepdims=True), qh, kb, (sh, c0)


def _band_valid(start):
    r = lax.broadcasted_iota(jnp.int32, (1, BAND), 1)
    return (start - LEFT_PAD + r) >= 0


def _attn_fwd(q, kvp, bias):
    t, d = q.shape
    hd = d // N_HEADS
    half = N_DEV // 2

    def body(q_ref, kv_ref, bias_ref, o_ref):
        start = pl.multiple_of(pl.program_id(0) * CHUNK, CHUNK)
        valid = _band_valid(start)
        for h in range(N_HEADS):
            p, _, _, (sh, c0) = _attn_scores(q_ref, kv_ref, bias_ref, h, hd, start, valid)
            vb = kv_ref[half + sh, pl.ds(start, BAND), c0:c0 + hd]
            o_ref[:, h * hd:(h + 1) * hd] = jnp.dot(p.astype(BF16), vb,
                                                    preferred_element_type=F32).astype(BF16)

    return pl.pallas_call(
        body, name="attn_fwd", grid=(t // CHUNK,),
        in_specs=[pl.BlockSpec((CHUNK, d), lambda i: (i, 0)), _VMEM, _VMEM],
        out_specs=pl.BlockSpec((CHUNK, d), lambda i: (i, 0)),
        out_shape=_sds((t, d), BF16),
        compiler_params=_params(("arbitrary",)),
    )(q, kvp, bias)


def _attn_bwd(q, dout, kvp, bias):
    t, d = q.shape
    hd = d // N_HEADS
    half = N_DEV // 2
    scale = hd ** -0.5

    def body(q_ref, do_ref, kv_ref, bias_ref, dq_ref, dkv_ref, dbias_ref):
        @pl.when(pl.program_id(0) == 0)
        def _():
            dkv_ref[...] = jnp.zeros_like(dkv_ref)
            dbias_ref[...] = jnp.zeros_like(dbias_ref)

        start = pl.multiple_of(pl.program_id(0) * CHUNK, CHUNK)
        valid = _band_valid(start)
        for h in range(N_HEADS):
            p, qh, kb, (sh, c0) = _attn_scores(q_ref, kv_ref, bias_ref, h, hd, start, valid)
            vb = kv_ref[half + sh, pl.ds(start, BAND), c0:c0 + hd]
            doh = do_ref[:, h * hd:(h + 1) * hd]
            dp = lax.dot_general(doh, vb, (((1,), (1,)), ((), ())), preferred_element_type=F32)
            ds = p * (dp - jnp.sum(dp * p, axis=-1, keepdims=True))
            dbias_ref[h] += ds
            dsb = (ds * scale).astype(BF16)
            dq_ref[:, h * hd:(h + 1) * hd] = jnp.dot(dsb, kb, preferred_element_type=F32).astype(BF16)
            dkv_ref[sh, pl.ds(start, BAND), c0:c0 + hd] += lax.dot_general(
                dsb, qh, (((0,), (0,)), ((), ())), preferred_element_type=F32)
            dkv_ref[half + sh, pl.ds(start, BAND), c0:c0 + hd] += lax.dot_general(
                p.astype(BF16), doh, (((0,), (0,)), ((), ())), preferred_element_type=F32)

    return pl.pallas_call(
        body, name="attn_bwd", grid=(t // CHUNK,),
        in_specs=[pl.BlockSpec((CHUNK, d), lambda i: (i, 0)), pl.BlockSpec((CHUNK, d), lambda i: (i, 0)),
                  _VMEM, _VMEM],
        out_specs=[pl.BlockSpec((CHUNK, d), lambda i: (i, 0)), _VMEM, _VMEM],
        out_shape=[_sds((t, d), BF16), _sds(kvp.shape, F32), _sds(bias.shape, F32)],
        compiler_params=_params(("arbitrary",)),
    )(q, dout, kvp, bias)


def _rel_index_onehot(t):
    r = lax.broadcasted_iota(jnp.int32, (BAND, N_REL), 0)
    i = lax.broadcasted_iota(jnp.int32, (BAND, N_REL), 1)
    idx = jnp.clip(t + LEFT_PAD - r, -(CHUNK - 1), MAX_REL) + (CHUNK - 1)
    return (idx == i).astype(BF16)


def _rel_bias_grad(dbias_t):
    def body(d_ref, o_ref):
        t = pl.program_id(0)

        @pl.when(t == 0)
        def _():
            o_ref[...] = jnp.zeros_like(o_ref)

        oh = _rel_index_onehot(t)
        dv = d_ref[...]
        hi = dv.astype(BF16)
        rest = dv - hi.astype(F32)
        mid = rest.astype(BF16)
        lo = (rest - mid.astype(F32)).astype(BF16)
        acc = jnp.dot(hi, oh, preferred_element_type=F32)
        acc += jnp.dot(mid, oh, preferred_element_type=F32)
        acc += jnp.dot(lo, oh, preferred_element_type=F32)
        o_ref[...] += acc

    return pl.pallas_call(
        body, name="rel_bias_grad", grid=(CHUNK,),
        in_specs=[pl.BlockSpec((None, N_HEADS, BAND), lambda i: (i, 0, 0))],
        out_specs=pl.BlockSpec((N_HEADS, N_REL), lambda i: (0, 0)),
        out_shape=_sds((N_HEADS, N_REL), F32),
        compiler_params=_params(("arbitrary",)),
    )(dbias_t)


def _sum_parts(parts):
    s_n, rows, c = parts.shape
    br = _row_block(rows, OPT_ROW_BLOCK)

    def body(p_ref, o_ref):
        acc = p_ref[0].astype(F32)
        for s in range(1, s_n):
            acc = acc + p_ref[s].astype(F32)
        o_ref[...] = acc

    return pl.pallas_call(
        body, name="sum_parts", grid=(rows // br,),
        in_specs=[pl.BlockSpec((s_n, br, c), lambda i: (0, i, 0))],
        out_specs=pl.BlockSpec((br, c), lambda i: (i, 0)),
        out_shape=_sds((rows, c), F32),
        compiler_params=_params(("parallel",)),
    )(parts)


def _adamw(own, own_idx, parts, w, m, v, row0=0, bufs=None):
    _, rows, c = own.shape
    s_n = 0 if parts is None else parts.shape[0]
    total = w.shape[0]
    br = _row_block(rows, OPT_ROW_BLOCK)
    assert row0 % br == 0 and (bufs is not None or (row0 == 0 and total == rows))
    b0 = row0 // br
    m_corr = 1.0 - ADAM_B1 ** ADAM_STEP
    v_corr = 1.0 - ADAM_B2 ** ADAM_STEP

    def body(idx_ref, own_ref, *refs):
        if s_n:
            p_ref, refs = refs[0], refs[1:]
        w_ref, m_ref, v_ref = refs[:3]
        g_ref, d_ref, nm_ref, nv_ref = refs[-4:]
        g = own_ref[...].astype(F32)
        for s in range(s_n):
            g = g + p_ref[s].astype(F32)
        nm = ADAM_B1 * m_ref[...] + (1.0 - ADAM_B1) * g
        nv = ADAM_B2 * v_ref[...] + (1.0 - ADAM_B2) * (g * g)
        g_ref[...] = g
        nm_ref[...] = nm
        nv_ref[...] = nv
        d_ref[...] = -ADAM_LR * ((nm / m_corr) / (jnp.sqrt(nv / v_corr) + ADAM_EPS) + ADAM_WD * w_ref[...])

    tile = pl.BlockSpec((br, c), lambda i, idx: (i + b0, 0))
    in_specs = [pl.BlockSpec((None, br, c), lambda i, idx: (idx[0], i, 0))]
    operands = [own_idx, own]
    if s_n:
        in_specs.append(pl.BlockSpec((s_n, br, c), lambda i, idx: (0, i, 0)))
        operands.append(parts)
    in_specs += [tile, tile, tile]
    operands += [w, m, v]
    aliases = {}
    if bufs is not None:
        aliases = {len(operands) + j: j for j in range(4)}
        in_specs += [_ANY] * 4
        operands += list(bufs)
    out = _sds((total, c), F32)
    return pl.pallas_call(
        body, name="adamw",
        grid_spec=pltpu.PrefetchScalarGridSpec(
            num_scalar_prefetch=1, grid=(rows // br,), in_specs=in_specs,
            out_specs=[tile, tile, tile, tile]),
        out_shape=[out, out, out, out],
        input_output_aliases=aliases,
        compiler_params=_params(("parallel",)),
    )(*operands)


def _chip_sum(p, r1, core):
    half = N_DEV // 2
    c = p.shape[-1]
    rows = int(np.prod(p.shape[1:-1]))
    br = _row_block(rows, OPT_ROW_BLOCK)

    def body(core_ref, p_ref, r_ref, o_ref):
        o_ref[...] = (p_ref[...].astype(F32) + r_ref[...].astype(F32)).astype(BF16)

    out = pl.pallas_call(
        body, name="chip_sum",
        grid_spec=pltpu.PrefetchScalarGridSpec(
            num_scalar_prefetch=1, grid=(half, rows // br),
            in_specs=[pl.BlockSpec((None, None, br, c), lambda q, i, cr: (q, cr[0], i, 0)),
                      pl.BlockSpec((None, br, c), lambda q, i, cr: (q, i, 0))],
            out_specs=pl.BlockSpec((None, br, c), lambda q, i, cr: (q, i, 0))),
        out_shape=_sds((half, rows, c), BF16),
        compiler_params=_params(("parallel", "parallel")),
    )(core, p.reshape(half, 2, rows, c), r1.reshape(half, rows, c))
    return out.reshape((half,) + p.shape[1:])


def _position():
    return tuple(lax.axis_index(a) for a in MESH_AXES)


def _linear(px, py, pc):
    return 4 * px + 2 * py + pc


def _all_gather_small(v):
    rows, lanes = v.shape

    def body(x_ref, out_ref, send_sems, recv_sems, local_sem):
        x, y, c = _position()
        me, sibling = (x, y, c), (x, y, 1 - c)
        chips = [(1 - x, y), (x, 1 - y), (1 - x, 1 - y)]

        def copy(k, block, to, src=None):
            dst = out_ref.at[_linear(*block)]
            return pltpu.make_async_remote_copy(
                src_ref=dst if src is None else src, dst_ref=dst,
                send_sem=send_sems.at[k], recv_sem=recv_sems.at[k],
                device_id=to, device_id_type=MESH_ID)

        mine = pltpu.make_async_copy(x_ref, out_ref.at[_linear(*me)], local_sem)
        mine.start()
        first = [copy(0, me, sibling, src=x_ref)]
        first += [copy(1 + j, me, (*chip, c), src=x_ref) for j, chip in enumerate(chips)]
        for cp in first:
            cp.start()
        passed = [copy(4 + j, (*chip, c), sibling) for j, chip in enumerate(chips)]
        for j, chip in enumerate(chips):
            copy(1 + j, (*chip, c), me).wait_recv()
            passed[j].start()
        copy(0, sibling, me).wait_recv()
        for j, chip in enumerate(chips):
            copy(4 + j, (*chip, 1 - c), me).wait_recv()
        for cp in first + passed:
            cp.wait_send()
        mine.wait()

    return pl.pallas_call(
        body, name="all_gather_small",
        out_shape=_sds((N_DEV, rows, lanes), v.dtype),
        in_specs=[_VMEM], out_specs=_VMEM,
        scratch_shapes=[pltpu.SemaphoreType.DMA((7,)), pltpu.SemaphoreType.DMA((7,)),
                        pltpu.SemaphoreType.DMA],
        compiler_params=pltpu.CompilerParams(vmem_limit_bytes=VMEM_LIMIT),
    )(v)


_HBM = pl.BlockSpec(memory_space=pltpu.HBM)
_SEM = pl.BlockSpec(memory_space=pltpu.SEMAPHORE)
_EFFECT = pltpu.SideEffectType.DATAFLOW_SIDE_EFFECTING
_ALL_CHIPS = [(0, 0), (0, 1), (1, 0), (1, 1)]


def _other_chips(x, y):
    return [(1 - x, y), (x, 1 - y), (1 - x, 1 - y)]


def _in_hbm(a):
    return pltpu.with_memory_space_constraint(a, pltpu.HBM)


def _token():
    return _sds((SUBLANES, LANES), F32)


def _gather_ici_copy(ref, i, k, chip, c, block, send_sems, recv_sems):
    return pltpu.make_async_remote_copy(
        src_ref=ref.at[block], dst_ref=ref.at[block],
        send_sem=send_sems.at[3 * i + k], recv_sem=recv_sems.at[3 * i + k],
        device_id=(*chip, c), device_id_type=MESH_ID)


def _gather_ici_start(name, lands):
    n = len(lands)

    def body(*refs):
        ins, send_sems, recv_sems, token = refs[:n], refs[n], refs[n + 1], refs[-1]
        x, y, c = _position()
        me = _linear(x, y, c)
        for i in range(n):
            for k, chip in enumerate(_other_chips(x, y)):
                _gather_ici_copy(ins[i], i, k, chip, c, me, send_sems, recv_sems).start()
        token[...] = jnp.zeros_like(token)

    out = pl.pallas_call(
        body, name=name,
        out_shape=(pltpu.SemaphoreType.DMA((3 * n,)), pltpu.SemaphoreType.DMA((3 * n,)),
                   *[pltpu.HBM(a.shape, a.dtype) for a in lands], _token()),
        in_specs=[_HBM] * n, out_specs=(_SEM, _SEM, *[_HBM] * n, _VMEM),
        input_output_aliases={i: 2 + i for i in range(n)},
        compiler_params=pltpu.CompilerParams(has_side_effects=_EFFECT),
    )(*[_in_hbm(a) for a in lands])
    return out[0], out[1], list(out[2:2 + n]), out[-1]


def _gather_ici_wait(name, lands, send_sems, recv_sems, after):
    n = len(lands)

    def body(*refs):
        ins, ss, rs = refs[:n], refs[n], refs[n + 1]
        x, y, c = _position()
        me = _linear(x, y, c)
        for i in range(n):
            for k, chip in enumerate(_other_chips(x, y)):
                _gather_ici_copy(ins[i], i, k, chip, c, me, ss, rs).wait_send()
                _gather_ici_copy(ins[i], i, k, chip, c, _linear(*chip, c), ss, rs).wait_recv()

    out = pl.pallas_call(
        body, name=name,
        out_shape=[pltpu.HBM(a.shape, a.dtype) for a in lands],
        in_specs=[_HBM] * n + [_SEM, _SEM, _ANY], out_specs=[_HBM] * n,
        input_output_aliases={i: i for i in range(n)},
        compiler_params=pltpu.CompilerParams(has_side_effects=_EFFECT),
    )(*lands, send_sems, recv_sems, after)
    return list(out)


def _gather_d2d(lands):
    n = len(lands)

    def body(*refs):
        ins, outs, send_sems, recv_sems = refs[:n], refs[n:2 * n], refs[2 * n], refs[2 * n + 1]
        x, y, c = _position()

        def copy(i, q, core):
            block = _linear(*_ALL_CHIPS[q], core)
            return pltpu.make_async_remote_copy(
                src_ref=ins[i].at[block], dst_ref=outs[i].at[block],
                send_sem=send_sems.at[i, q], recv_sem=recv_sems.at[i, q],
                device_id=(x, y, 1 - c), device_id_type=MESH_ID)

        sent = [copy(i, q, c) for i in range(n) for q in range(len(_ALL_CHIPS))]
        for cp in sent:
            cp.start()
        for i in range(n):
            for q in range(len(_ALL_CHIPS)):
                copy(i, q, 1 - c).wait_recv()
        for cp in sent:
            cp.wait_send()

    return pl.pallas_call(
        body, name="gather_d2d",
        out_shape=[_sds(a.shape, a.dtype) for a in lands],
        in_specs=[_ANY] * n, out_specs=[_ANY] * n,
        input_output_aliases={i: i for i in range(n)},
        scratch_shapes=[pltpu.SemaphoreType.DMA((n, 4)), pltpu.SemaphoreType.DMA((n, 4))],
    )(*lands)


def _partials_d2d(parts):
    n = len(parts)
    half = N_DEV // 2

    def body(*refs):
        ins, outs, send_sems, recv_sems = refs[:n], refs[n:2 * n], refs[2 * n], refs[2 * n + 1]
        x, y, c = _position()

        def copy(i, q):
            return pltpu.make_async_remote_copy(
                src_ref=ins[i].at[_linear(*_ALL_CHIPS[q], 1 - c)], dst_ref=outs[i].at[q],
                send_sem=send_sems.at[i, q], recv_sem=recv_sems.at[i, q],
                device_id=(x, y, 1 - c), device_id_type=MESH_ID)

        sent = [copy(i, q) for i in range(n) for q in range(half)]
        for cp in sent:
            cp.start()
        for cp in sent:
            cp.wait_recv()
        for cp in sent:
            cp.wait_send()

    return pl.pallas_call(
        body, name="partials_d2d",
        out_shape=[_sds((half,) + p.shape[1:], p.dtype) for p in parts],
        in_specs=[_ANY] * n, out_specs=[_ANY] * n,
        scratch_shapes=[pltpu.SemaphoreType.DMA((n, half)), pltpu.SemaphoreType.DMA((n, half))],
    )(*parts)


def _partials_ici_copy(src, land, i, k, chip, c, send_sems, recv_sems):
    return pltpu.make_async_remote_copy(
        src_ref=src.at[2 * chip[0] + chip[1]], dst_ref=land.at[k],
        send_sem=send_sems.at[3 * i + k], recv_sem=recv_sems.at[3 * i + k],
        device_id=(*chip, c), device_id_type=MESH_ID)


def _partials_ici_start(name, sums, lands):
    n = len(sums)

    def body(*refs):
        srcs, dsts, send_sems, recv_sems, token = refs[:n], refs[n:2 * n], refs[2 * n], refs[2 * n + 1], refs[-1]
        x, y, c = _position()
        for i in range(n):
            for k, chip in enumerate(_other_chips(x, y)):
                _partials_ici_copy(srcs[i], dsts[i], i, k, chip, c, send_sems, recv_sems).start()
        token[...] = jnp.zeros_like(token)

    both = list(sums) + list(lands)
    out = pl.pallas_call(
        body, name=name,
        out_shape=(pltpu.SemaphoreType.DMA((3 * n,)), pltpu.SemaphoreType.DMA((3 * n,)),
                   *[pltpu.HBM(a.shape, a.dtype) for a in both], _token()),
        in_specs=[_HBM] * (2 * n), out_specs=(_SEM, _SEM, *[_HBM] * (2 * n), _VMEM),
        input_output_aliases={i: 2 + i for i in range(2 * n)},
        compiler_params=pltpu.CompilerParams(has_side_effects=_EFFECT),
    )(*[_in_hbm(a) for a in both])
    return out[0], out[1], list(out[2:2 + n]), list(out[2 + n:2 + 2 * n]), out[-1]


def _partials_ici_wait(name, sums, lands, send_sems, recv_sems, after):
    n = len(sums)

    def body(*refs):
        srcs, dsts, ss, rs = refs[:n], refs[n:2 * n], refs[2 * n], refs[2 * n + 1]
        x, y, c = _position()
        for i in range(n):
            for k, chip in enumerate(_other_chips(x, y)):
                cp = _partials_ici_copy(srcs[i], dsts[i], i, k, chip, c, ss, rs)
                cp.wait_send()
                cp.wait_recv()

    both = list(sums) + list(lands)
    out = pl.pallas_call(
        body, name=name,
        out_shape=[pltpu.HBM(a.shape, a.dtype) for a in both],
        in_specs=[_HBM] * (2 * n) + [_SEM, _SEM, _ANY], out_specs=[_HBM] * (2 * n),
        input_output_aliases={i: i for i in range(2 * n)},
        compiler_params=pltpu.CompilerParams(has_side_effects=_EFFECT),
    )(*both, send_sems, recv_sems, after)
    return list(out[:n]), list(out[n:])


def _pack(arrs):
    flat = jnp.concatenate([a.reshape(-1).astype(F32) for a in arrs])
    pad = (-flat.shape[0]) % (SUBLANES * LANES)
    if pad:
        flat = jnp.concatenate([flat, jnp.zeros((pad,), F32)])
    return flat.reshape(-1, LANES)


def _unpack(packed, shapes, lead=()):
    flat = packed.reshape(lead + (-1,))
    out, off = [], 0
    for s in shapes:
        size = int(np.prod(s))
        out.append(flat[..., off:off + size].reshape(lead + tuple(s)))
        off += size
    return out


def _unshard_last(g):
    nd = g.ndim
    perm = tuple(range(1, nd - 1)) + (0, nd - 1)
    t = jnp.transpose(g, perm)
    return t.reshape(t.shape[:-2] + (N_DEV * g.shape[-1],))


def kernel(x, c, w_ada, b_ada, ln_g, ln_b, ffn_gu, ffn_down, gmlp_w_in, gmlp_b_in, gmlp_ln_g, gmlp_ln_b, gmlp_w_s, gmlp_b_s, gmlp_w_out, w_ada_kv, b_ada_kv, w_kv, attn_w_q, attn_rel_bias, attn_w_o, loss_target, m_w_ada, m_b_ada, m_ln_g, m_ln_b, m_ffn_gu, m_ffn_down, m_gmlp_w_in, m_gmlp_b_in, m_gmlp_ln_g, m_gmlp_ln_b, m_gmlp_w_s, m_gmlp_b_s, m_gmlp_w_out, m_w_ada_kv, m_b_ada_kv, m_w_kv, m_attn_w_q, m_attn_rel_bias, m_attn_w_o, v_w_ada, v_b_ada, v_ln_g, v_ln_b, v_ffn_gu, v_ffn_down, v_gmlp_w_in, v_gmlp_b_in, v_gmlp_ln_g, v_gmlp_ln_b, v_gmlp_w_s, v_gmlp_b_s, v_gmlp_w_out, v_w_ada_kv, v_b_ada_kv, v_w_kv, v_attn_w_q, v_attn_rel_bias, v_attn_w_o):
    weights = dict(w_ada=w_ada, b_ada=b_ada, ln_g=ln_g, ln_b=ln_b, ffn_gu=ffn_gu, ffn_down=ffn_down,
                   gmlp_w_in=gmlp_w_in, gmlp_b_in=gmlp_b_in, gmlp_ln_g=gmlp_ln_g, gmlp_ln_b=gmlp_ln_b,
                   gmlp_w_s=gmlp_w_s, gmlp_b_s=gmlp_b_s, gmlp_w_out=gmlp_w_out, w_ada_kv=w_ada_kv,
                   b_ada_kv=b_ada_kv, w_kv=w_kv, attn_w_q=attn_w_q, attn_rel_bias=attn_rel_bias,
                   attn_w_o=attn_w_o)
    mom1 = dict(w_ada=m_w_ada, b_ada=m_b_ada, ln_g=m_ln_g, ln_b=m_ln_b, ffn_gu=m_ffn_gu, ffn_down=m_ffn_down,
                gmlp_w_in=m_gmlp_w_in, gmlp_b_in=m_gmlp_b_in, gmlp_ln_g=m_gmlp_ln_g, gmlp_ln_b=m_gmlp_ln_b,
                gmlp_w_s=m_gmlp_w_s, gmlp_b_s=m_gmlp_b_s, gmlp_w_out=m_gmlp_w_out, w_ada_kv=m_w_ada_kv,
                b_ada_kv=m_b_ada_kv, w_kv=m_w_kv, attn_w_q=m_attn_w_q, attn_rel_bias=m_attn_rel_bias,
                attn_w_o=m_attn_w_o)
    mom2 = dict(w_ada=v_w_ada, b_ada=v_b_ada, ln_g=v_ln_g, ln_b=v_ln_b, ffn_gu=v_ffn_gu, ffn_down=v_ffn_down,
                gmlp_w_in=v_gmlp_w_in, gmlp_b_in=v_gmlp_b_in, gmlp_ln_g=v_gmlp_ln_g, gmlp_ln_b=v_gmlp_ln_b,
                gmlp_w_s=v_gmlp_w_s, gmlp_b_s=v_gmlp_b_s, gmlp_w_out=v_gmlp_w_out, w_ada_kv=v_w_ada_kv,
                b_ada_kv=v_b_ada_kv, w_kv=v_w_kv, attn_w_q=v_attn_w_q, attn_rel_bias=v_attn_rel_bias,
                attn_w_o=v_attn_w_o)
    order = list(weights)

    x = x[0]
    target = loss_target[0]
    t, d = x.shape
    n_mod = w_ada.shape[-1] * N_DEV // d
    mod_w = w_ada.shape[-1]
    kv_w = w_ada_kv.shape[-1]
    n_b = DEPTH - N_A
    me = _linear(*_position())

    l2 = DEPTH * 2
    big = dict(
        ffn_gu=ffn_gu.reshape((l2,) + ffn_gu.shape[2:]),
        ffn_down=ffn_down.reshape((l2,) + ffn_down.shape[2:]),
        gmlp_w_in=gmlp_w_in, gmlp_w_out=gmlp_w_out, w_kv=w_kv[None],
        attn_w_q=attn_w_q, attn_w_o=attn_w_o)
    big_names = list(big)
    core = lax.axis_index("c").astype(jnp.int32).reshape(1)
    chip = (2 * lax.axis_index("x") + lax.axis_index("y")).astype(jnp.int32).reshape(1)

    fwd_groups = [
        {"ffn_gu": (0, 1), "ffn_down": (0, 1)},
        {"gmlp_w_in": (0, 1), "gmlp_w_out": (0, 1), "ffn_gu": (1, 1), "ffn_down": (1, 1)},
        {"ffn_gu": (2, 2), "ffn_down": (2, 2), "gmlp_w_in": (1, 1), "gmlp_w_out": (1, 1), "w_kv": (0, 1)},
        {"ffn_gu": (4, 2), "ffn_down": (4, 2), "attn_w_q": (0, 1), "attn_w_o": (0, 1)},
        {"ffn_gu": (6, 2), "ffn_down": (6, 2), "attn_w_q": (1, 1), "attn_w_o": (1, 1)},
    ]
    bwd_groups = []
    for l in range(DEPTH):
        g = {"ffn_gu": (2 * l, 2), "ffn_down": (2 * l, 2)}
        if l < N_A:
            g.update({"gmlp_w_in": (l, 1), "gmlp_w_out": (l, 1)})
        else:
            g.update({"attn_w_q": (l - N_A, 1), "attn_w_o": (l - N_A, 1)})
        if l == N_A - 1:
            g["w_kv"] = (0, 1)
        bwd_groups.append(g)

    def slot_of(groups, name, slot):
        for gi, g in enumerate(groups):
            if name in g and g[name][0] <= slot < g[name][0] + g[name][1]:
                return gi, slot - g[name][0]
        raise KeyError((name, slot))

    flights = []
    for gi, g in enumerate(fwd_groups):
        lands = []
        for name, (s0, cnt) in g.items():
            shard = big[name][s0:s0 + cnt].astype(BF16)
            land = lax.empty((N_DEV,) + shard.shape, BF16)
            lands.append(lax.dynamic_update_slice(land, shard[None], (me,) + (0,) * shard.ndim))
        flights.append(_gather_ici_start(f"gather_ici_start_{gi}", lands))
    start_token = sum(f[3][0, 0] for f in flights)
    gathered = [None] * len(fwd_groups)

    def land_group(gi, after):
        send_sems, recv_sems, lands, _ = flights[gi]
        lands = _gather_ici_wait(f"gather_ici_wait_{gi}", lands, send_sems, recv_sems, after)
        gathered[gi] = dict(zip(fwd_groups[gi], _gather_d2d(lands)))

    def weight(name, slot):
        gi, local = slot_of(fwd_groups, name, slot)
        return gathered[gi][name], local

    partial = [{name: lax.empty((N_DEV, cnt) + big[name].shape[1:], BF16) for name, (_, cnt) in g.items()}
               for g in bwd_groups]

    c_all = _all_gather_small(_pack([c]))
    c_all = _unpack(c_all, [(d,)], lead=(N_DEV,))[0]
    c4 = _as4(c_all)
    mod_part = _matmul("ada_fwd", c4, w_ada[:, None], (DEPTH, 1, N_DEV, mod_w), F32, a_silu=True)
    kv_part = _matmul("ada_kv_fwd", c4, _as4(w_ada_kv), (1, 1, N_DEV, kv_w), F32, a_silu=True)
    small_shapes = [mod_part.shape, kv_part.shape, ln_g.shape, ln_b.shape, gmlp_b_in.shape,
                    gmlp_ln_g.shape, gmlp_ln_b.shape, attn_rel_bias.shape]
    small = _all_gather_small(_pack([mod_part, kv_part, ln_g, ln_b, gmlp_b_in, gmlp_ln_g, gmlp_ln_b,
                                     attn_rel_bias]))
    (mod_g, kvm_g, ln_g_g, ln_b_g, b_in_g, gln_g_g, gln_b_g, rel_g) = _unpack(small, small_shapes, lead=(N_DEV,))
    mod_mine = lax.dynamic_index_in_dim(mod_g[:, :, 0], me, axis=2, keepdims=False)
    mod = _unshard_last(mod_mine) + b_ada
    mod = mod.reshape(DEPTH, n_mod, 1, d)
    kvm_mine = lax.dynamic_index_in_dim(kvm_g[:, 0, 0], me, axis=1, keepdims=False)
    mkv = (_unshard_last(kvm_mine) + b_ada_kv).reshape(2, 1, d)
    ln_g_f = _unshard_last(ln_g_g)
    ln_b_f = _unshard_last(ln_b_g)
    half = N_DEV // 2
    b_in_f = jnp.transpose(b_in_g, (1, 0, 2))[:, :, None, :]
    gln_g_f = _unshard_last(gln_g_g).reshape(N_A, half, 1, -1)
    gln_b_f = _unshard_last(gln_b_g).reshape(N_A, half, 1, -1)
    rel_f = _unshard_last(rel_g)
    tq = np.arange(CHUNK)
    rb = np.arange(BAND)
    rel_idx = np.clip(tq[:, None] + LEFT_PAD - rb[None, :], -(CHUNK - 1), MAX_REL) + (CHUNK - 1)

    def shard_act(a):
        return a.reshape(a.shape[0], a.shape[2], a.shape[3])

    def grad_into(name, slot, mm):
        gi, local = slot_of(bwd_groups, name, slot)
        partial[gi][name] = mm(partial[gi][name], local)

    def ffn_fwd(h, lw):
        w_gu, l_gu = weight("ffn_gu", lw)
        w_dn, l_dn = weight("ffn_down", lw)
        n = w_gu.shape[-1]
        gu = _matmul("ffn_gu_fwd", _as4(h), w_gu, (N_DEV, 1, t, n), F32, lb=l_gu)
        a = _swiglu_fwd(shard_act(gu))
        y = _matmul("ffn_down_fwd", a[:, None], w_dn, (1, 1, t, d), F32, lb=l_dn, b_merge=2, reduce=True)
        return y[0, 0], (gu, a)

    def ffn_bwd(dy, h, saved, lw):
        gu, a = saved
        w_gu, l_gu = weight("ffn_gu", lw)
        w_dn, l_dn = weight("ffn_down", lw)
        n = w_gu.shape[-1]
        da = _matmul("ffn_down_bwd_a", _as4(dy), w_dn, (half, 1, t, n), F32, lb=l_dn, b_merge=2, tb=True)
        grad_into("ffn_down", lw, lambda buf, lo: _matmul(
            "ffn_down_bwd_w", a[:, None], _as4(dy), buf.shape, BF16, ta=True, lo=lo, out_merge=2, out_buf=buf))
        dgu = _swiglu_bwd(shard_act(gu), shard_act(da))
        dh = _matmul("ffn_gu_bwd_a", dgu[:, None], w_gu, (1, 1, t, d), F32, lb=l_gu, tb=True, reduce=True)
        grad_into("ffn_gu", lw, lambda buf, lo: _matmul(
            "ffn_gu_bwd_w", _as4(h), dgu[:, None], buf.shape, BF16, ta=True, lo=lo, out_buf=buf))
        return dh[0, 0], {}

    def gmlp_params(l):
        return (b_in_f[l], gln_g_f[l], gln_b_f[l], gmlp_w_s[l], gmlp_b_s[l][:, :, None])

    def gmlp_fwd(h, l):
        w_in, l_in = weight("gmlp_w_in", l)
        w_out, l_out = weight("gmlp_w_out", l)
        n = w_in.shape[-1]
        zpre = _matmul("gmlp_in_fwd", _as4(h), w_in, (N_DEV, 1, t, n), F32, lb=l_in)
        gated = _gmlp_mid_fwd(shard_act(zpre), *gmlp_params(l))
        y = _matmul("gmlp_out_fwd", gated[:, None], w_out, (1, 1, t, d), F32, lb=l_out, b_merge=2, reduce=True)
        return y[0, 0], (zpre, gated)

    def gmlp_bwd(dy, h, saved, l):
        zpre, gated = saved
        w_in, l_in = weight("gmlp_w_in", l)
        w_out, l_out = weight("gmlp_w_out", l)
        n = w_in.shape[-1]
        dgated = _matmul("gmlp_out_bwd_a", _as4(dy), w_out, (half, 1, t, n), F32, lb=l_out, b_merge=2, tb=True)
        grad_into("gmlp_w_out", l, lambda buf, lo: _matmul(
            "gmlp_out_bwd_w", gated[:, None], _as4(dy), buf.shape, BF16, ta=True, lo=lo, out_merge=2, out_buf=buf))
        dz, dws, dbs, dlng, dlnb, dbin = _gmlp_mid_bwd(shard_act(zpre), shard_act(dgated), *gmlp_params(l))
        dh = _matmul("gmlp_in_bwd_a", dz[:, None], w_in, (1, 1, t, d), F32, lb=l_in, tb=True, reduce=True)
        grad_into("gmlp_w_in", l, lambda buf, lo: _matmul(
            "gmlp_in_bwd_w", _as4(h), dz[:, None], buf.shape, BF16, ta=True, lo=lo, out_buf=buf))
        small_grads = dict(gmlp_w_s=dws, gmlp_b_s=dbs[:, :, 0], gmlp_ln_g=dlng.reshape(-1),
                           gmlp_ln_b=dlnb.reshape(-1), gmlp_b_in=dbin.reshape(-1))
        return dh[0, 0], small_grads

    def attn_fwd(h, j, kvp):
        bias = rel_f[j][:, rel_idx]
        w_q, l_q = weight("attn_w_q", j)
        w_o, l_o = weight("attn_w_o", j)
        q = _matmul("attn_q_fwd", _as4(h), w_q, (1, 1, t, d), BF16, lb=l_q, b_merge=N_DEV, reduce=True)[0, 0]
        o = _attn_fwd(q, kvp, bias)
        y = _matmul("attn_o_fwd", _as4(o), w_o, (1, 1, t, d), F32, lb=l_o, b_merge=N_DEV, reduce=True)
        return y[0, 0], (q, o, bias)

    def attn_bwd(dy, h, saved, j, kvp):
        q, o, bias = saved
        w_q, l_q = weight("attn_w_q", j)
        w_o, l_o = weight("attn_w_o", j)
        do = _matmul("attn_o_bwd_a", _as4(dy), w_o, (1, 1, t, d), BF16, lb=l_o, b_merge=N_DEV, tb=True)[0, 0]
        grad_into("attn_w_o", j, lambda buf, lo: _matmul(
            "attn_o_bwd_w", _as4(o), _as4(dy), buf.shape, BF16, ta=True, lo=lo, out_merge=N_DEV, out_buf=buf))
        dq, dkvp, dbias = _attn_bwd(q, do, kvp, bias)
        drel = _rel_bias_grad(jnp.transpose(dbias, (1, 0, 2)))
        dh = _matmul("attn_q_bwd_a", _as4(dq), w_q, (1, 1, t, d), F32, lb=l_q, b_merge=N_DEV, tb=True)
        grad_into("attn_w_q", j, lambda buf, lo: _matmul(
            "attn_q_bwd_w", _as4(h), _as4(dq), buf.shape, BF16, ta=True, lo=lo, out_merge=N_DEV, out_buf=buf))
        return dh[0, 0], dict(attn_rel_bias=drel, dkvp=dkvp)

    tape = []
    kvp = None
    kv_tape = None
    first_use = {(0, 0): 0, (0, 1): 1, (1, 0): 2, (2, 0): 3, (3, 0): 4}
    mod = mod.at[0, 0].add(start_token)
    for l in range(DEPTH):
        for i in range(3):
            if (l, i) in first_use:
                land_group(first_use[l, i], x)
            shift, scl, gate = mod[l, 3 * i], mod[l, 3 * i + 1], mod[l, 3 * i + 2]
            wgt = 1.0 if i == 1 else 0.5
            gw = wgt * (1.0 + gate)
            h = _modulate(x, scl, shift)
            if i != 1:
                y, saved = ffn_fwd(h, 2 * l + i // 2)
            elif l < N_A:
                y, saved = gmlp_fwd(h, l)
            else:
                y, saved = attn_fwd(h, l - N_A, kvp)
            x_new = _ln_res_fwd(x, y, gw, ln_g_f[l, i][None], ln_b_f[l, i][None])
            tape.append((x, h, y, gw, scl, saved))
            x = x_new
        if l == N_A - 1:
            hkv = _modulate(x, mkv[1], mkv[0])
            w_kvg, l_kv = weight("w_kv", 0)
            n = w_kvg.shape[-1]
            kv = _matmul("kv_fwd", _as4(hkv), w_kvg, (N_DEV, 1, t, n), BF16, lb=l_kv)
            kvp = jnp.pad(shard_act(kv), ((0, 0), (LEFT_PAD, 0), (0, 0)))
            kv_tape = (x, hkv)

    loss_part, dx = _loss_head(x, target)
    loss = lax.psum(loss_part[0, 0], MESH_AXES)

    d_mod = [[None] * n_mod for _ in range(DEPTH)]
    d_ln_g = [[None] * 3 for _ in range(DEPTH)]
    d_ln_b = [[None] * 3 for _ in range(DEPTH)]
    small_grads = {k: [None] * N_A for k in ("gmlp_w_s", "gmlp_b_s", "gmlp_ln_g", "gmlp_ln_b", "gmlp_b_in")}
    d_rel = [None] * n_b
    dkvp_sum = None
    d_mkv = None
    reductions = [None] * DEPTH
    sent_token = None
    for l in reversed(range(DEPTH)):
        if l == N_A - 1:
            x_kv, hkv = kv_tape
            w_kvg, l_kv = weight("w_kv", 0)
            dkv = dkvp_sum[:, LEFT_PAD:, :].astype(BF16)[:, None]
            dhkv = _matmul("kv_bwd_a", dkv, w_kvg, (1, 1, t, d), F32, lb=l_kv, tb=True, reduce=True)[0, 0]
            grad_into("w_kv", 0, lambda buf, lo: _matmul(
                "kv_bwd_w", _as4(hkv), dkv, buf.shape, BF16, ta=True, lo=lo, out_buf=buf))
            dx, ds_kv, db_kv = _mod_bwd(dx, dhkv, x_kv, mkv[1])
            d_mkv = jnp.concatenate([db_kv[0], ds_kv[0]])
        for i in reversed(range(3)):
            x_in, h, y, gw, scl, saved = tape[3 * l + i]
            wgt = 1.0 if i == 1 else 0.5
            if sent_token is not None:
                gw = gw + sent_token
                sent_token = None
            dx_res, dy, dgw, dg, db = _ln_res_bwd(x_in, y, gw, ln_g_f[l, i][None], dx)
            d_ln_g[l][i], d_ln_b[l][i] = dg[0], db[0]
            if i != 1:
                dh, extra = ffn_bwd(dy, h, saved, 2 * l + i // 2)
            elif l < N_A:
                dh, extra = gmlp_bwd(dy, h, saved, l)
                for k, g in extra.items():
                    small_grads[k][l] = g
            else:
                dh, extra = attn_bwd(dy, h, saved, l - N_A, kvp)
                d_rel[l - N_A] = extra["attn_rel_bias"]
                dkvp_sum = extra["dkvp"] if dkvp_sum is None else _add_kv(dkvp_sum, extra["dkvp"])
            dx, dscl, dshift = _mod_bwd(dx_res, dh, x_in, scl)
            d_mod[l][3 * i], d_mod[l][3 * i + 1], d_mod[l][3 * i + 2] = dshift[0], dscl[0], wgt * dgw[0]
        names = list(bwd_groups[l])
        from_sibling = _partials_d2d([partial[l][k] for k in names])
        sums = [_chip_sum(partial[l][k], r1, core) for k, r1 in zip(names, from_sibling)]
        lands = [lax.empty((len(_ALL_CHIPS) - 1,) + s.shape[1:], BF16) for s in sums]
        reductions[l] = _partials_ici_start(f"partials_ici_start_{l}", sums, lands)
        sent_token = reductions[l][4][0, 0]
    grad_x = dx[None]

    d_mod_arr = jnp.stack([jnp.concatenate(r) for r in d_mod])
    small_part = dict(
        b_ada=d_mod_arr, b_ada_kv=d_mkv,
        ln_g=jnp.stack([jnp.stack(r) for r in d_ln_g]), ln_b=jnp.stack([jnp.stack(r) for r in d_ln_b]),
        gmlp_b_in=jnp.stack(small_grads["gmlp_b_in"]), gmlp_ln_g=jnp.stack(small_grads["gmlp_ln_g"]),
        gmlp_ln_b=jnp.stack(small_grads["gmlp_ln_b"]), gmlp_w_s=jnp.stack(small_grads["gmlp_w_s"]),
        gmlp_b_s=jnp.stack(small_grads["gmlp_b_s"]), attn_rel_bias=jnp.stack(d_rel))
    small_names = list(small_part)
    sp_shapes = [small_part[k].shape for k in small_names]
    sp_all = _all_gather_small(_pack([small_part[k] for k in small_names]))
    sp_sum = _sum_parts(sp_all)
    full_grads = dict(zip(small_names, _unpack(sp_sum, sp_shapes)))
    per_dev = dict(zip(small_names, _unpack(sp_all, sp_shapes, lead=(N_DEV,))))

    def my_cols(a, width):
        return lax.dynamic_slice_in_dim(a, me * width, width, axis=a.ndim - 1)

    grads = {}
    grads["b_ada"] = full_grads["b_ada"]
    grads["b_ada_kv"] = full_grads["b_ada_kv"]
    grads["gmlp_w_s"] = full_grads["gmlp_w_s"]
    grads["gmlp_b_s"] = full_grads["gmlp_b_s"]
    for k in ("ln_g", "ln_b", "gmlp_b_in", "gmlp_ln_g", "gmlp_ln_b", "attn_rel_bias"):
        grads[k] = my_cols(full_grads[k], weights[k].shape[-1])

    dmod_cols = jnp.transpose(my_cols(per_dev["b_ada"], mod_w), (1, 0, 2))[:, None]
    grads["w_ada"] = _matmul("ada_bwd_w", c4, dmod_cols, (DEPTH, 1, d, mod_w), F32, ta=True,
                             a_silu=True)[:, 0]
    dkv_cols = my_cols(per_dev["b_ada_kv"], kv_w)[None, None]
    grads["w_ada_kv"] = _matmul("ada_kv_bwd_w", c4, dkv_cols, (1, 1, d, kv_w), F32, ta=True,
                                a_silu=True)[0, 0]

    delta, new_m, new_v = {}, {}, {}
    first = jnp.zeros((1,), jnp.int32)

    def flat2(a, cols):
        return a.reshape(-1, cols)

    for k in ("w_ada", "w_ada_kv"):
        w = weights[k]
        cols = w.shape[-1]
        res = _adamw(grads[k].reshape(1, -1, cols), first, None, flat2(w, cols), flat2(mom1[k], cols),
                     flat2(mom2[k], cols))
        grads[k], delta[k], new_m[k], new_v[k] = (a.reshape(w.shape) for a in res)

    tiny = [k for k in order if k not in delta and k not in big_names]
    tiny_shapes = [weights[k].shape for k in tiny]
    tiny_out = _adamw(_pack([grads[k] for k in tiny])[None], first, None, _pack([weights[k] for k in tiny]),
                      _pack([mom1[k] for k in tiny]), _pack([mom2[k] for k in tiny]))
    for dst, arr in zip((grads, delta, new_m, new_v), tiny_out):
        for k, val in zip(tiny, _unpack(arr, tiny_shapes)):
            dst[k] = val

    bufs = {k: [lax.empty(flat2(weights[k], weights[k].shape[-1]).shape, F32) for _ in range(4)]
            for k in big_names}
    for l in reversed(range(DEPTH)):
        send_sems, recv_sems, sums, lands, _ = reductions[l]
        sums, lands = _partials_ici_wait(f"partials_ici_wait_{l}", sums, lands, send_sems, recv_sems, tiny_out[0])
        for k, own, got in zip(bwd_groups[l], sums, lands):
            cols = weights[k].shape[-1]
            slot_rows = int(np.prod(big[k].shape[1:-1]))
            bufs[k] = _adamw(own.reshape(own.shape[0], -1, cols), chip, got.reshape(got.shape[0], -1, cols),
                             flat2(weights[k], cols), flat2(mom1[k], cols), flat2(mom2[k], cols),
                             row0=bwd_groups[l][k][0] * slot_rows, bufs=bufs[k])
    for k in big_names:
        grads[k], delta[k], new_m[k], new_v[k] = (b.reshape(weights[k].shape) for b in bufs[k])

    return (loss, grad_x, *[grads[k] for k in order], *[delta[k] for k in order],
            *[new_m[k] for k in order], *[new_v[k] for k in order])


def _add_kv(a, b):
    s, rows, n = a.shape
    return _add(a.reshape(s * rows, n), b.reshape(s * rows, n)).reshape(s, rows, n)
```

```python
import functools

import numpy as np
import jax
import jax.numpy as jnp
from jax import lax
from jax.experimental import pallas as pl
from jax.experimental.pallas import tpu as pltpu

F32 = jnp.float32
BF16 = jnp.bfloat16
MESH_AXES = ("x", "y", "c")
N_DEV = 8
MESH_ID = pl.DeviceIdType.MESH

DEPTH = 4
N_A = 2
CHUNK = 64
N_HEADS = 16
LEFT_CHUNKS = 8
BAND = (LEFT_CHUNKS + 1) * CHUNK
LEFT_PAD = LEFT_CHUNKS * CHUNK
MAX_REL = 4 * CHUNK
N_REL = (CHUNK - 1) + MAX_REL + 1
GMLP_WINDOW = 128
GMLP_GROUPS = 8
ALPHA = (2.0 * DEPTH) ** 0.25
LN_EPS = 1e-5
ADAM_LR = 0.001
ADAM_B1 = 0.9
ADAM_B2 = 0.999
ADAM_EPS = 1e-08
ADAM_WD = 0.01
ADAM_STEP = 10

V7X_VMEM_BYTES = 64 * 1024 * 1024
VMEM_LIMIT = V7X_VMEM_BYTES - 8 * 1024 * 1024
LANES = 128
SUBLANES = 8
MM_BLOCK = 1024
ROW_BLOCK = 512
OPT_ROW_BLOCK = 128

_ANY = pl.BlockSpec(memory_space=pl.ANY)
_VMEM = pl.BlockSpec(memory_space=pltpu.VMEM)


def _params(sem=None):
    return pltpu.CompilerParams(dimension_semantics=sem, vmem_limit_bytes=VMEM_LIMIT)


def _row_block(rows, target):
    for d in range(min(rows, target), 0, -1):
        if rows % d == 0 and (d % SUBLANES == 0 or d == rows):
            return d
    return rows


def _matmul(name, a, b, out_shape4, out_dtype, *, la=0, lb=0, lo=0, ta=False, tb=False,
            reduce=False, b_merge=1, out_merge=1, out_buf=None, a_silu=False):
    ja_n, _, a_r, a_c = a.shape
    jb_n, _, b_r, b_c = b.shape
    jo_n, _, o_r, o_c = out_shape4
    m_tot = a_c if ta else a_r
    k_a = a_r if ta else a_c
    b_rows = b_merge * b_r
    k_c = b_c if tb else b_rows
    n = b_rows if tb else b_c
    n_chunks = (jb_n // b_merge) if reduce else 1
    natural_k = reduce and ja_n == 1
    assert n == o_c, (name, n, o_c)
    assert k_a ==(k_c * n_chunks if natural_k else k_c), (name, k_a, k_c, n_chunks)
    bk = k_c if (k_c <= MM_BLOCK or (b_merge > 1 and not tb)) else MM_BLOCK
    assert k_c % bk == 0
    nkk = k_c // bk
    nk = n_chunks * nkk
    m_out = out_merge * o_r
    assert m_tot == m_out, (name, m_tot, m_out)
    bm = m_tot if (m_tot <= MM_BLOCK or out_merge > 1) else MM_BLOCK
    assert m_tot % bm == 0
    jo_blocks = jo_n // out_merge

    def a_index(j, m, k):
        kj, kk = k // nkk, k % nkk
        ja = 0 if ja_n == 1 else (kj if reduce else j)
        ke = kk + kj * nkk if natural_k else kk
        return (ja, la, ke, m) if ta else (ja, la, m, ke)

    def b_index(j, m, k):
        kj, kk = k // nkk, k % nkk
        jb = 0 if jb_n == b_merge else (kj if reduce else j)
        return (jb, lb, 0, kk) if tb else (jb, lb, kk, 0)

    def o_index(j, m, k):
        return (j, lo, 0, 0) if out_merge > 1 else (j, lo, m, 0)

    a_block = (None, None, bk, bm) if ta else (None, None, bm, bk)
    if b_merge > 1:
        b_block = (b_merge, None, b_r, bk if tb else n)
    else:
        b_block = (None, None, n, bk) if tb else (None, None, bk, n)
    o_block = (out_merge, None, o_r, n) if out_merge > 1 else (None, None, bm, n)
    dims = (((0 if ta else 1,), (1 if tb else 0,)), ((), ()))

    def body(a_ref, b_ref, *rest):
        o_ref, acc_ref = rest[-2], rest[-1]
        k = pl.program_id(2)

        @pl.when(k == 0)
        def _():
            acc_ref[...] = jnp.zeros_like(acc_ref)

        av = a_ref[...]
        if a_silu:
            af = av.astype(F32)
            av = af * jax.nn.sigmoid(af)
        bv = b_ref[...]
        if b_merge > 1:
            bv = bv.reshape(b_rows, bv.shape[-1])
        acc_ref[...] += lax.dot_general(av.astype(BF16), bv.astype(BF16), dims,
                                        preferred_element_type=F32)

        @pl.when(k == nk - 1)
        def _():
            ov = acc_ref[...].astype(out_dtype)
            if out_merge > 1:
                ov = ov.reshape(out_merge, o_r, n)
            o_ref[...] = ov

    in_specs = [pl.BlockSpec(a_block, a_index), pl.BlockSpec(b_block, b_index)]
    operands = [a, b]
    aliases = {}
    if out_buf is not None:
        assert out_buf.shape == tuple(out_shape4) and out_buf.dtype == out_dtype
        in_specs.append(_ANY)
        operands.append(out_buf)
        aliases = {2: 0}
    return pl.pallas_call(
        body, name=name,
        grid=(jo_blocks, m_tot // bm, nk),
        in_specs=in_specs,
        out_specs=pl.BlockSpec(o_block, o_index),
        out_shape=jax.ShapeDtypeStruct(tuple(out_shape4), out_dtype),
        scratch_shapes=[pltpu.VMEM((bm, n), F32)],
        input_output_aliases=aliases,
        compiler_params=_params(("parallel", "parallel", "arbitrary")),
    )(*operands)


def _as4(a):
    return a.reshape((1,) * (4 - a.ndim) + a.shape)


def _row_call(name, body, ins, outs, t, *, acc_outs=()):
    bt = _row_block(t, ROW_BLOCK)

    def spec(arr, tiled):
        if tiled:
            return pl.BlockSpec((bt,) + tuple(arr.shape[1:]), lambda i: (i,) + (0,) * (arr.ndim - 1))
        return pl.BlockSpec(tuple(arr.shape), lambda i: (0,) * arr.ndim)

    return pl.pallas_call(
        body, name=name, grid=(t // bt,),
        in_specs=[spec(a, tl) for a, tl in ins],
        out_specs=[spec(o, tl) for o, tl in outs],
        out_shape=[jax.ShapeDtypeStruct(o.shape, o.dtype) for o, _ in outs],
        compiler_params=_params(("arbitrary",) if acc_outs else ("parallel",)),
    )(*[a for a, _ in ins])


def _sds(shape, dtype):
    return jax.ShapeDtypeStruct(tuple(shape), dtype)


def _modulate(x, scl, shift):
    t, d = x.shape

    def body(x_ref, s_ref, b_ref, h_ref):
        h_ref[...] = (x_ref[...] * (1.0 + s_ref[...]) + b_ref[...]).astype(BF16)

    return _row_call("modulate", body, [(x, True), (scl, False), (shift, False)],
                     [(_sds((t, d), BF16), True)], t)[0]


def _ln_stats(r):
    mu = jnp.mean(r, axis=-1, keepdims=True)
    rc = r - mu
    var = jnp.mean(rc * rc, axis=-1, keepdims=True)
    rstd = lax.rsqrt(var + LN_EPS)
    return rc * rstd, rstd


def _ln_res_fwd(x, y, gw, g, b):
    t, d = x.shape

    def body(x_ref, y_ref, gw_ref, g_ref, b_ref, o_ref):
        r = ALPHA * x_ref[...] + gw_ref[...] * y_ref[...]
        xhat, _ = _ln_stats(r)
        o_ref[...] = xhat * g_ref[...] + b_ref[...]

    return _row_call("ln_res_fwd", body,
                     [(x, True), (y, True), (gw, False), (g, False), (b, False)],
                     [(_sds((t, d), F32), True)], t)[0]


def _ln_res_bwd(x, y, gw, g, dxn):
    t, d = x.shape

    def body(x_ref, y_ref, gw_ref, g_ref, dxn_ref, dx_ref, dy_ref, dgw_ref, dg_ref, db_ref):
        @pl.when(pl.program_id(0) == 0)
        def _():
            dgw_ref[...] = jnp.zeros_like(dgw_ref)
            dg_ref[...] = jnp.zeros_like(dg_ref)
            db_ref[...] = jnp.zeros_like(db_ref)

        yv = y_ref[...]
        gwv = gw_ref[...]
        dxn = dxn_ref[...]
        xhat, rstd = _ln_stats(ALPHA * x_ref[...] + gwv * yv)
        dxh = dxn * g_ref[...]
        m1 = jnp.mean(dxh, axis=-1, keepdims=True)
        m2 = jnp.mean(dxh * xhat, axis=-1, keepdims=True)
        dr = rstd * (dxh - m1 - xhat * m2)
        dx_ref[...] = ALPHA * dr
        dy_ref[...] = (gwv * dr).astype(BF16)
        dgw_ref[...] += jnp.sum(dr * yv, axis=0, keepdims=True)
        dg_ref[...] += jnp.sum(dxn * xhat, axis=0, keepdims=True)
        db_ref[...] += jnp.sum(dxn, axis=0, keepdims=True)

    vec = _sds((1, d), F32)
    return _row_call("ln_res_bwd", body,
                     [(x, True), (y, True), (gw, False), (g, False), (dxn, True)],
                     [(_sds((t, d), F32), True), (_sds((t, d), BF16), True),
                      (vec, False), (vec, False), (vec, False)], t, acc_outs=(2, 3, 4))


def _mod_bwd(dx_res, dh, x, scl):
    t, d = x.shape

    def body(dxr_ref, dh_ref, x_ref, s_ref, dx_ref, ds_ref, db_ref):
        @pl.when(pl.program_id(0) == 0)
        def _():
            ds_ref[...] = jnp.zeros_like(ds_ref)
            db_ref[...] = jnp.zeros_like(db_ref)

        dh = dh_ref[...]
        dx_ref[...] = dxr_ref[...] + dh * (1.0 + s_ref[...])
        ds_ref[...] += jnp.sum(dh * x_ref[...], axis=0, keepdims=True)
        db_ref[...] += jnp.sum(dh, axis=0, keepdims=True)

    vec = _sds((1, d), F32)
    return _row_call("mod_bwd", body, [(dx_res, True), (dh, True), (x, True), (scl, False)],
                     [(_sds((t, d), F32), True), (vec, False), (vec, False)], t, acc_outs=(1, 2))


def _add(a, b):
    rows = a.shape[0]

    def body(a_ref, b_ref, o_ref):
        o_ref[...] = a_ref[...] + b_ref[...]

    return _row_call("add", body, [(a, True), (b, True)], [(_sds(a.shape, a.dtype), True)], rows)[0]


def _loss_head(y, target):
    t, d = y.shape

    def body(y_ref, t_ref, l_ref, dy_ref):
        @pl.when(pl.program_id(0) == 0)
        def _():
            l_ref[...] = jnp.zeros_like(l_ref)

        err = y_ref[...] - t_ref[...]
        dy_ref[...] = err * (1.0 / d)
        part = 0.5 * jnp.sum(jnp.mean(err * err, axis=-1, keepdims=True), axis=0, keepdims=True)
        l_ref[...] += jnp.broadcast_to(part, l_ref.shape)

    return _row_call("loss_head", body, [(y, True), (target, True)],
                     [(_sds((SUBLANES, LANES), F32), False), (_sds((t, d), F32), True)], t,
                     acc_outs=(0,))


def _swiglu_fwd(gu):
    _, t, n = gu.shape
    half = N_DEV // 2
    bt = _row_block(t, MM_BLOCK)
    gu4 = gu.reshape(2, half, t, n)

    def body(gu_ref, a_ref):
        g = gu_ref[0]
        a_ref[...] = (g * jax.nn.sigmoid(g) * gu_ref[1]).astype(BF16)

    return pl.pallas_call(
        body, name="swiglu_fwd", grid=(half, t // bt),
        in_specs=[pl.BlockSpec((2, None, bt, n), lambda j, i: (0, j, i, 0))],
        out_specs=pl.BlockSpec((None, bt, n), lambda j, i: (j, i, 0)),
        out_shape=_sds((half, t, n), BF16),
        compiler_params=_params(("parallel", "parallel")),
    )(gu4)


def _swiglu_bwd(gu, da):
    _, t, n = gu.shape
    half = N_DEV // 2
    bt = _row_block(t, MM_BLOCK)
    gu4 = gu.reshape(2, half, t, n)

    def body(gu_ref, da_ref, d_ref):
        g = gu_ref[0]
        u = gu_ref[1]
        da = da_ref[...]
        sig = jax.nn.sigmoid(g)
        d_ref[0] = (da * u * sig * (1.0 + g * (1.0 - sig))).astype(BF16)
        d_ref[1] = (da * g * sig).astype(BF16)

    out = pl.pallas_call(
        body, name="swiglu_bwd", grid=(half, t // bt),
        in_specs=[pl.BlockSpec((2, None, bt, n), lambda j, i: (0, j, i, 0)),
                  pl.BlockSpec((None, bt, n), lambda j, i: (j, i, 0))],
        out_specs=pl.BlockSpec((2, None, bt, n), lambda j, i: (0, j, i, 0)),
        out_shape=_sds((2, half, t, n), BF16),
        compiler_params=_params(("parallel", "parallel")),
    )(gu4, da)
    return out.reshape(N_DEV, t, n)


_INV_SQRT2 = 0.7071067811865476
_INV_SQRT_2PI = 0.3989422804014327


def _gelu(z):
    return 0.5 * z * (1.0 + lax.erf(z * _INV_SQRT2))


def _gelu_grad(z):
    return 0.5 * (1.0 + lax.erf(z * _INV_SQRT2)) + z * jnp.exp(-0.5 * z * z) * _INV_SQRT_2PI


def _window_mask():
    t_out = lax.broadcasted_iota(jnp.int32, (GMLP_WINDOW, GMLP_WINDOW), 0)
    s_in = lax.broadcasted_iota(jnp.int32, (GMLP_WINDOW, GMLP_WINDOW), 1)
    return (s_in // CHUNK) <= (t_out // CHUNK)


def _gmlp_recompute(z_ref, bin_ref, lng_ref, lnb_ref):
    half = N_DEV // 2
    z = z_ref[...] + bin_ref[...]
    ge = _gelu(z)
    u = ge[:half]
    v = ge[half:]
    width = half * v.shape[-1]
    mu = jnp.sum(jnp.sum(v, axis=0), axis=-1, keepdims=True) / width
    vc = v - mu
    var = jnp.sum(jnp.sum(vc * vc, axis=0), axis=-1, keepdims=True) / width
    rstd = lax.rsqrt(var + LN_EPS)
    xhat = vc * rstd
    vn = xhat * lng_ref[...] + lnb_ref[...]
    return z, u, xhat, rstd, vn


def _gmlp_mid_fwd(zpre, b_in, ln_g, ln_b, w_s, b_s):
    _, t, n = zpre.shape
    half = N_DEV // 2
    gd = half * n // GMLP_GROUPS
    per = n // gd
    w = GMLP_WINDOW

    def body(z_ref, bin_ref, lng_ref, lnb_ref, ws_ref, bs_ref, o_ref):
        _, u, _, _, vn = _gmlp_recompute(z_ref, bin_ref, lng_ref, lnb_ref)
        mask = _window_mask()
        for g in range(GMLP_GROUPS):
            sh, c0 = g // per, (g % per) * gd
            wsm = jnp.where(mask, ws_ref[g], 0.0).astype(BF16)
            s = jnp.dot(wsm, vn[sh][:, c0:c0 + gd].astype(BF16), preferred_element_type=F32) + bs_ref[g]
            o_ref[sh, :, c0:c0 + gd] = (u[sh][:, c0:c0 + gd] * s).astype(BF16)

    whole = lambda a: pl.BlockSpec(tuple(a.shape), lambda i: (0,) * a.ndim)
    return pl.pallas_call(
        body, name="gmlp_mid_fwd", grid=(t // w,),
        in_specs=[pl.BlockSpec((N_DEV, w, n), lambda i: (0, i, 0)),
                  whole(b_in), whole(ln_g), whole(ln_b), whole(w_s), whole(b_s)],
        out_specs=pl.BlockSpec((half, w, n), lambda i: (0, i, 0)),
        out_shape=_sds((half, t, n), BF16),
        compiler_params=_params(("parallel",)),
    )(zpre, b_in, ln_g, ln_b, w_s, b_s)


def _gmlp_mid_bwd(zpre, dgated, b_in, ln_g, ln_b, w_s, b_s):
    _, t, n = zpre.shape
    half = N_DEV // 2
    gd = half * n // GMLP_GROUPS
    per = n // gd
    w = GMLP_WINDOW
    width = half * n

    def body(z_ref, dg_ref, bin_ref, lng_ref, lnb_ref, ws_ref, bs_ref,
             dz_ref, dws_ref, dbs_ref, dlng_ref, dlnb_ref, dbin_ref, du_ref, dvn_ref):
        @pl.when(pl.program_id(0) == 0)
        def _():
            for r in (dws_ref, dbs_ref, dlng_ref, dlnb_ref, dbin_ref):
                r[...] = jnp.zeros_like(r)

        z, u, xhat, rstd, vn = _gmlp_recompute(z_ref, bin_ref, lng_ref, lnb_ref)
        mask = _window_mask()
        for g in range(GMLP_GROUPS):
            sh, c0 = g // per, (g % per) * gd
            wsm = jnp.where(mask, ws_ref[g], 0.0).astype(BF16)
            vg = vn[sh][:, c0:c0 + gd].astype(BF16)
            s = jnp.dot(wsm, vg, preferred_element_type=F32) + bs_ref[g]
            dgt = dg_ref[sh, :, c0:c0 + gd]
            ds = dgt * u[sh][:, c0:c0 + gd]
            du_ref[sh, :, c0:c0 + gd] = dgt * s
            dsb = ds.astype(BF16)
            dws = lax.dot_general(dsb, vg, (((1,), (1,)), ((), ())), preferred_element_type=F32)
            dws_ref[g] += jnp.where(mask, dws, 0.0)
            dbs_ref[g] += jnp.sum(ds, axis=-1, keepdims=True)
            dvn_ref[sh, :, c0:c0 + gd] = lax.dot_general(wsm, dsb, (((0,), (0,)), ((), ())),
                                                         preferred_element_type=F32)
        dvn = dvn_ref[...]
        dlng_ref[...] += jnp.sum(dvn * xhat, axis=1, keepdims=True)
        dlnb_ref[...] += jnp.sum(dvn, axis=1, keepdims=True)
        dxh = dvn * lng_ref[...]
        m1 = jnp.sum(jnp.sum(dxh, axis=0), axis=-1, keepdims=True) / width
        m2 = jnp.sum(jnp.sum(dxh * xhat, axis=0), axis=-1, keepdims=True) / width
        dv = rstd * (dxh - m1 - xhat * m2)
        gg = _gelu_grad(z)
        dzu = du_ref[...] * gg[:half]
        dzv = dv * gg[half:]
        dz_ref[:half] = dzu.astype(BF16)
        dz_ref[half:] = dzv.astype(BF16)
        dbin_ref[:half] += jnp.sum(dzu, axis=1, keepdims=True)
        dbin_ref[half:] += jnp.sum(dzv, axis=1, keepdims=True)

    whole = lambda a: pl.BlockSpec(tuple(a.shape), lambda i: (0,) * a.ndim)
    outs = [_sds((N_DEV, t, n), BF16), _sds(w_s.shape, F32), _sds(b_s.shape, F32),
            _sds(ln_g.shape, F32), _sds(ln_b.shape, F32), _sds(b_in.shape, F32)]
    return pl.pallas_call(
        body, name="gmlp_mid_bwd", grid=(t // w,),
        in_specs=[pl.BlockSpec((N_DEV, w, n), lambda i: (0, i, 0)),
                  pl.BlockSpec((half, w, n), lambda i: (0, i, 0)),
                  whole(b_in), whole(ln_g), whole(ln_b), whole(w_s), whole(b_s)],
        out_specs=[pl.BlockSpec((N_DEV, w, n), lambda i: (0, i, 0))] + [whole(o) for o in outs[1:]],
        out_shape=outs,
        scratch_shapes=[pltpu.VMEM((half, w, n), F32), pltpu.VMEM((half, w, n), F32)],
        compiler_params=_params(("arbitrary",)),
    )(zpre, dgated, b_in, ln_g, ln_b, w_s, b_s)


def _attn_scores(q_ref, kv_ref, bias_ref, h, hd, start, valid):
    per = kv_ref.shape[-1] // hd
    sh, c0 = h // per, (h % per) * hd
    qh = q_ref[:, h * hd:(h + 1) * hd]
    kb = kv_ref[sh, pl.ds(start, BAND), c0:c0 + hd]
    sc = lax.dot_general(qh, kb, (((1,), (1,)), ((), ())), preferred_element_type=F32)
    sc = sc * (hd ** -0.5) + bias_ref[h]
    sc = jnp.where(valid, sc, -jnp.inf)
    sc = sc - jnp.max(sc, axis=-1, keepdims=True)
    e = jnp.exp(sc)
    return e / jnp.sum(e, axis=-1, keepdims=True), qh, kb, (sh, c0)


def _band_valid(start):
    r = lax.broadcasted_iota(jnp.int32, (1, BAND), 1)
    return (start - LEFT_PAD + r) >= 0


def _attn_fwd(q, kvp, bias):
    t, d = q.shape
    hd = d // N_HEADS
    half = N_DEV // 2

    def body(q_ref, kv_ref, bias_ref, o_ref):
        start = pl.multiple_of(pl.program_id(0) * CHUNK, CHUNK)
        valid = _band_valid(start)
        for h in range(N_HEADS):
            p, _, _, (sh, c0) = _attn_scores(q_ref, kv_ref, bias_ref, h, hd, start, valid)
            vb = kv_ref[half + sh, pl.ds(start, BAND), c0:c0 + hd]
            o_ref[:, h * hd:(h + 1) * hd] = jnp.dot(p.astype(BF16), vb,
                                                    preferred_element_type=F32).astype(BF16)

    return pl.pallas_call(
        body, name="attn_fwd", grid=(t // CHUNK,),
        in_specs=[pl.BlockSpec((CHUNK, d), lambda i: (i, 0)), _VMEM, _VMEM],
        out_specs=pl.BlockSpec((CHUNK, d), lambda i: (i, 0)),
        out_shape=_sds((t, d), BF16),
        compiler_params=_params(("arbitrary",)),
    )(q, kvp, bias)


def _attn_bwd(q, dout, kvp, bias):
    t, d = q.shape
    hd = d // N_HEADS
    half = N_DEV // 2
    scale = hd ** -0.5

    def body(q_ref, do_ref, kv_ref, bias_ref, dq_ref, dkv_ref, dbias_ref):
        @pl.when(pl.program_id(0) == 0)
        def _():
            dkv_ref[...] = jnp.zeros_like(dkv_ref)
            dbias_ref[...] = jnp.zeros_like(dbias_ref)

        start = pl.multiple_of(pl.program_id(0) * CHUNK, CHUNK)
        valid = _band_valid(start)
        for h in range(N_HEADS):
            p, qh, kb, (sh, c0) = _attn_scores(q_ref, kv_ref, bias_ref, h, hd, start, valid)
            vb = kv_ref[half + sh, pl.ds(start, BAND), c0:c0 + hd]
            doh = do_ref[:, h * hd:(h + 1) * hd]
            dp = lax.dot_general(doh, vb, (((1,), (1,)), ((), ())), preferred_element_type=F32)
            ds = p * (dp - jnp.sum(dp * p, axis=-1, keepdims=True))
            dbias_ref[h] += ds
            dsb = (ds * scale).astype(BF16)
            dq_ref[:, h * hd:(h + 1) * hd] = jnp.dot(dsb, kb, preferred_element_type=F32).astype(BF16)
            dkv_ref[sh, pl.ds(start, BAND), c0:c0 + hd] += lax.dot_general(
                dsb, qh, (((0,), (0,)), ((), ())), preferred_element_type=F32)
            dkv_ref[half + sh, pl.ds(start, BAND), c0:c0 + hd] += lax.dot_general(
                p.astype(BF16), doh, (((0,), (0,)), ((), ())), preferred_element_type=F32)

    return pl.pallas_call(
        body, name="attn_bwd", grid=(t // CHUNK,),
        in_specs=[pl.BlockSpec((CHUNK, d), lambda i: (i, 0)), pl.BlockSpec((CHUNK, d), lambda i: (i, 0)),
                  _VMEM, _VMEM],
        out_specs=[pl.BlockSpec((CHUNK, d), lambda i: (i, 0)), _VMEM, _VMEM],
        out_shape=[_sds((t, d), BF16), _sds(kvp.shape, F32), _sds(bias.shape, F32)],
        compiler_params=_params(("arbitrary",)),
    )(q, dout, kvp, bias)


def _rel_index_onehot(t):
    r = lax.broadcasted_iota(jnp.int32, (BAND, N_REL), 0)
    i = lax.broadcasted_iota(jnp.int32, (BAND, N_REL), 1)
    idx = jnp.clip(t + LEFT_PAD - r, -(CHUNK - 1), MAX_REL) + (CHUNK - 1)
    return (idx == i).astype(BF16)


def _rel_bias_grad(dbias_t):
    def body(d_ref, o_ref):
        t = pl.program_id(0)

        @pl.when(t == 0)
        def _():
            o_ref[...] = jnp.zeros_like(o_ref)

        oh = _rel_index_onehot(t)
        dv = d_ref[...]
        hi = dv.astype(BF16)
        rest = dv - hi.astype(F32)
        mid = rest.astype(BF16)
        lo = (rest - mid.astype(F32)).astype(BF16)
        acc = jnp.dot(hi, oh, preferred_element_type=F32)
        acc += jnp.dot(mid, oh, preferred_element_type=F32)
        acc += jnp.dot(lo, oh, preferred_element_type=F32)
        o_ref[...] += acc

    return pl.pallas_call(
        body, name="rel_bias_grad", grid=(CHUNK,),
        in_specs=[pl.BlockSpec((None, N_HEADS, BAND), lambda i: (i, 0, 0))],
        out_specs=pl.BlockSpec((N_HEADS, N_REL), lambda i: (0, 0)),
        out_shape=_sds((N_HEADS, N_REL), F32),
        compiler_params=_params(("arbitrary",)),
    )(dbias_t)


def _sum_parts(parts):
    s_n, rows, c = parts.shape
    br = _row_block(rows, OPT_ROW_BLOCK)

    def body(p_ref, o_ref):
        acc = p_ref[0].astype(F32)
        for s in range(1, s_n):
            acc = acc + p_ref[s].astype(F32)
        o_ref[...] = acc

    return pl.pallas_call(
        body, name="sum_parts", grid=(rows // br,),
        in_specs=[pl.BlockSpec((s_n, br, c), lambda i: (0, i, 0))],
        out_specs=pl.BlockSpec((br, c), lambda i: (i, 0)),
        out_shape=_sds((rows, c), F32),
        compiler_params=_params(("parallel",)),
    )(parts)


def _adamw(own, own_idx, parts, w, m, v, row0=0, bufs=None):
    _, rows, c = own.shape
    s_n = 0 if parts is None else parts.shape[0]
    total = w.shape[0]
    br = _row_block(rows, OPT_ROW_BLOCK)
    assert row0 % br == 0 and (bufs is not None or (row0 == 0 and total == rows))
    b0 = row0 // br
    m_corr = 1.0 - ADAM_B1 ** ADAM_STEP
    v_corr = 1.0 - ADAM_B2 ** ADAM_STEP

    def body(idx_ref, own_ref, *refs):
        if s_n:
            p_ref, refs = refs[0], refs[1:]
        w_ref, m_ref, v_ref = refs[:3]
        g_ref, d_ref, nm_ref, nv_ref = refs[-4:]
        g = own_ref[...].astype(F32)
        for s in range(s_n):
            g = g + p_ref[s].astype(F32)
        nm = ADAM_B1 * m_ref[...] + (1.0 - ADAM_B1) * g
        nv = ADAM_B2 * v_ref[...] + (1.0 - ADAM_B2) * (g * g)
        g_ref[...] = g
        nm_ref[...] = nm
        nv_ref[...] = nv
        d_ref[...] = -ADAM_LR * ((nm / m_corr) / (jnp.sqrt(nv / v_corr) + ADAM_EPS) + ADAM_WD * w_ref[...])

    tile = pl.BlockSpec((br, c), lambda i, idx: (i + b0, 0))
    in_specs = [pl.BlockSpec((None, br, c), lambda i, idx: (idx[0], i, 0))]
    operands = [own_idx, own]
    if s_n:
        in_specs.append(pl.BlockSpec((s_n, br, c), lambda i, idx: (0, i, 0)))
        operands.append(parts)
    in_specs += [tile, tile, tile]
    operands += [w, m, v]
    aliases = {}
    if bufs is not None:
        aliases = {len(operands) + j: j for j in range(4)}
        in_specs += [_ANY] * 4
        operands += list(bufs)
    out = _sds((total, c), F32)
    return pl.pallas_call(
        body, name="adamw",
        grid_spec=pltpu.PrefetchScalarGridSpec(
            num_scalar_prefetch=1, grid=(rows // br,), in_specs=in_specs,
            out_specs=[tile, tile, tile, tile]),
        out_shape=[out, out, out, out],
        input_output_aliases=aliases,
        compiler_params=_params(("parallel",)),
    )(*operands)


def _chip_sum(p, r1, core):
    half = N_DEV // 2
    c = p.shape[-1]
    rows = int(np.prod(p.shape[1:-1]))
    br = _row_block(rows, MM_BLOCK)

    def body(core_ref, p_ref, r_ref, o_ref):
        o_ref[...] = (p_ref[...].astype(F32) + r_ref[...].astype(F32)).astype(BF16)

    out = pl.pallas_call(
        body, name="chip_sum",
        grid_spec=pltpu.PrefetchScalarGridSpec(
            num_scalar_prefetch=1, grid=(half, rows // br),
            in_specs=[pl.BlockSpec((None, None, br, c), lambda q, i, cr: (q, cr[0], i, 0)),
                      pl.BlockSpec((None, br, c), lambda q, i, cr: (q, i, 0))],
            out_specs=pl.BlockSpec((None, br, c), lambda q, i, cr: (q, i, 0))),
        out_shape=_sds((half, rows, c), BF16),
        compiler_params=_params(("parallel", "parallel")),
    )(core, p.reshape(half, 2, rows, c), r1.reshape(half, rows, c))
    return out.reshape((half,) + p.shape[1:])


def _position():
    return tuple(lax.axis_index(a) for a in MESH_AXES)


def _linear(px, py, pc):
    return 4 * px + 2 * py + pc


def _all_gather_small(v):
    rows, lanes = v.shape

    def body(x_ref, out_ref, send_sems, recv_sems, local_sem):
        x, y, c = _position()
        me, sibling = (x, y, c), (x, y, 1 - c)
        chips = [(1 - x, y), (x, 1 - y), (1 - x, 1 - y)]

        def copy(k, block, to, src=None):
            dst = out_ref.at[_linear(*block)]
            return pltpu.make_async_remote_copy(
                src_ref=dst if src is None else src, dst_ref=dst,
                send_sem=send_sems.at[k], recv_sem=recv_sems.at[k],
                device_id=to, device_id_type=MESH_ID)

        mine = pltpu.make_async_copy(x_ref, out_ref.at[_linear(*me)], local_sem)
        mine.start()
        first = [copy(0, me, sibling, src=x_ref)]
        first += [copy(1 + j, me, (*chip, c), src=x_ref) for j, chip in enumerate(chips)]
        for cp in first:
            cp.start()
        passed = [copy(4 + j, (*chip, c), sibling) for j, chip in enumerate(chips)]
        for j, chip in enumerate(chips):
            copy(1 + j, (*chip, c), me).wait_recv()
            passed[j].start()
        copy(0, sibling, me).wait_recv()
        for j, chip in enumerate(chips):
            copy(4 + j, (*chip, 1 - c), me).wait_recv()
        for cp in first + passed:
            cp.wait_send()
        mine.wait()

    return pl.pallas_call(
        body, name="all_gather_small",
        out_shape=_sds((N_DEV, rows, lanes), v.dtype),
        in_specs=[_VMEM], out_specs=_VMEM,
        scratch_shapes=[pltpu.SemaphoreType.DMA((7,)), pltpu.SemaphoreType.DMA((7,)),
                        pltpu.SemaphoreType.DMA],
        compiler_params=pltpu.CompilerParams(vmem_limit_bytes=VMEM_LIMIT),
    )(v)


_HBM = pl.BlockSpec(memory_space=pltpu.HBM)
_SEM = pl.BlockSpec(memory_space=pltpu.SEMAPHORE)
_EFFECT = pltpu.SideEffectType.DATAFLOW_SIDE_EFFECTING
_ALL_CHIPS = [(0, 0), (0, 1), (1, 0), (1, 1)]


def _other_chips(x, y):
    return [(1 - x, y), (x, 1 - y), (1 - x, 1 - y)]


def _in_hbm(a):
    return pltpu.with_memory_space_constraint(a, pltpu.HBM)


def _token():
    return _sds((SUBLANES, LANES), F32)


def _gather_ici_copy(ref, i, k, chip, c, block, send_sems, recv_sems):
    return pltpu.make_async_remote_copy(
        src_ref=ref.at[block], dst_ref=ref.at[block],
        send_sem=send_sems.at[3 * i + k], recv_sem=recv_sems.at[3 * i + k],
        device_id=(*chip, c), device_id_type=MESH_ID)


def _gather_ici_start(name, lands):
    n = len(lands)

    def body(*refs):
        ins, send_sems, recv_sems, token = refs[:n], refs[n], refs[n + 1], refs[-1]
        x, y, c = _position()
        me = _linear(x, y, c)
        for i in range(n):
            for k, chip in enumerate(_other_chips(x, y)):
                _gather_ici_copy(ins[i], i, k, chip, c, me, send_sems, recv_sems).start()
        token[...] = jnp.zeros_like(token)

    out = pl.pallas_call(
        body, name=name,
        out_shape=(pltpu.SemaphoreType.DMA((3 * n,)), pltpu.SemaphoreType.DMA((3 * n,)),
                   *[pltpu.HBM(a.shape, a.dtype) for a in lands], _token()),
        in_specs=[_HBM] * n, out_specs=(_SEM, _SEM, *[_HBM] * n, _VMEM),
        input_output_aliases={i: 2 + i for i in range(n)},
        compiler_params=pltpu.CompilerParams(has_side_effects=_EFFECT),
    )(*[_in_hbm(a) for a in lands])
    return out[0], out[1], list(out[2:2 + n]), out[-1]


def _gather_ici_wait(name, lands, send_sems, recv_sems, after):
    n = len(lands)

    def body(*refs):
        ins, ss, rs = refs[:n], refs[n], refs[n + 1]
        x, y, c = _position()
        me = _linear(x, y, c)
        for i in range(n):
            for k, chip in enumerate(_other_chips(x, y)):
                _gather_ici_copy(ins[i], i, k, chip, c, me, ss, rs).wait_send()
                _gather_ici_copy(ins[i], i, k, chip, c, _linear(*chip, c), ss, rs).wait_recv()

    out = pl.pallas_call(
        body, name=name,
        out_shape=[pltpu.HBM(a.shape, a.dtype) for a in lands],
        in_specs=[_HBM] * n + [_SEM, _SEM, _ANY], out_specs=[_HBM] * n,
        input_output_aliases={i: i for i in range(n)},
        compiler_params=pltpu.CompilerParams(has_side_effects=_EFFECT),
    )(*lands, send_sems, recv_sems, after)
    return list(out)


def _gather_d2d(lands):
    n = len(lands)

    def body(*refs):
        ins, outs, send_sems, recv_sems = refs[:n], refs[n:2 * n], refs[2 * n], refs[2 * n + 1]
        x, y, c = _position()

        def copy(i, q, core):
            block = _linear(*_ALL_CHIPS[q], core)
            return pltpu.make_async_remote_copy(
                src_ref=ins[i].at[block], dst_ref=outs[i].at[block],
                send_sem=send_sems.at[i, q], recv_sem=recv_sems.at[i, q],
                device_id=(x, y, 1 - c), device_id_type=MESH_ID)

        sent = [copy(i, q, c) for i in range(n) for q in range(len(_ALL_CHIPS))]
        for cp in sent:
            cp.start()
        for i in range(n):
            for q in range(len(_ALL_CHIPS)):
                copy(i, q, 1 - c).wait_recv()
        for cp in sent:
            cp.wait_send()

    return pl.pallas_call(
        body, name="gather_d2d",
        out_shape=[_sds(a.shape, a.dtype) for a in lands],
        in_specs=[_ANY] * n, out_specs=[_ANY] * n,
        input_output_aliases={i: i for i in range(n)},
        scratch_shapes=[pltpu.SemaphoreType.DMA((n, 4)), pltpu.SemaphoreType.DMA((n, 4))],
    )(*lands)


def _partials_d2d(parts):
    n = len(parts)
    half = N_DEV // 2

    def body(*refs):
        ins, outs, send_sems, recv_sems = refs[:n], refs[n:2 * n], refs[2 * n], refs[2 * n + 1]
        x, y, c = _position()

        def copy(i, q):
            return pltpu.make_async_remote_copy(
                src_ref=ins[i].at[_linear(*_ALL_CHIPS[q], 1 - c)], dst_ref=outs[i].at[q],
                send_sem=send_sems.at[i, q], recv_sem=recv_sems.at[i, q],
                device_id=(x, y, 1 - c), device_id_type=MESH_ID)

        sent = [copy(i, q) for i in range(n) for q in range(half)]
        for cp in sent:
            cp.start()
        for cp in sent:
            cp.wait_recv()
        for cp in sent:
            cp.wait_send()

    return pl.pallas_call(
        body, name="partials_d2d",
        out_shape=[_sds((half,) + p.shape[1:], p.dtype) for p in parts],
        in_specs=[_ANY] * n, out_specs=[_ANY] * n,
        scratch_shapes=[pltpu.SemaphoreType.DMA((n, half)), pltpu.SemaphoreType.DMA((n, half))],
    )(*parts)


def _partials_ici_copy(src, land, i, k, chip, c, send_sems, recv_sems):
    return pltpu.make_async_remote_copy(
        src_ref=src.at[2 * chip[0] + chip[1]], dst_ref=land.at[k],
        send_sem=send_sems.at[3 * i + k], recv_sem=recv_sems.at[3 * i + k],
        device_id=(*chip, c), device_id_type=MESH_ID)


def _partials_ici_start(name, sums, lands):
    n = len(sums)

    def body(*refs):
        srcs, dsts, send_sems, recv_sems, token = refs[:n], refs[n:2 * n], refs[2 * n], refs[2 * n + 1], refs[-1]
        x, y, c = _position()
        for i in range(n):
            for k, chip in enumerate(_other_chips(x, y)):
                _partials_ici_copy(srcs[i], dsts[i], i, k, chip, c, send_sems, recv_sems).start()
        token[...] = jnp.zeros_like(token)

    both = list(sums) + list(lands)
    out = pl.pallas_call(
        body, name=name,
        out_shape=(pltpu.SemaphoreType.DMA((3 * n,)), pltpu.SemaphoreType.DMA((3 * n,)),
                   *[pltpu.HBM(a.shape, a.dtype) for a in both], _token()),
        in_specs=[_HBM] * (2 * n), out_specs=(_SEM, _SEM, *[_HBM] * (2 * n), _VMEM),
        input_output_aliases={i: 2 + i for i in range(2 * n)},
        compiler_params=pltpu.CompilerParams(has_side_effects=_EFFECT),
    )(*[_in_hbm(a) for a in both])
    return out[0], out[1], list(out[2:2 + n]), list(out[2 + n:2 + 2 * n]), out[-1]


def _partials_ici_wait(name, sums, lands, send_sems, recv_sems, after):
    n = len(sums)

    def body(*refs):
        srcs, dsts, ss, rs = refs[:n], refs[n:2 * n], refs[2 * n], refs[2 * n + 1]
        x, y, c = _position()
        for i in range(n):
            for k, chip in enumerate(_other_chips(x, y)):
                cp = _partials_ici_copy(srcs[i], dsts[i], i, k, chip, c, ss, rs)
                cp.wait_send()
                cp.wait_recv()

    both = list(sums) + list(lands)
    out = pl.pallas_call(
        body, name=name,
        out_shape=[pltpu.HBM(a.shape, a.dtype) for a in both],
        in_specs=[_HBM] * (2 * n) + [_SEM, _SEM, _ANY], out_specs=[_HBM] * (2 * n),
        input_output_aliases={i: i for i in range(2 * n)},
        compiler_params=pltpu.CompilerParams(has_side_effects=_EFFECT),
    )(*both, send_sems, recv_sems, after)
    return list(out[:n]), list(out[n:])


def _pack(arrs):
    flat = jnp.concatenate([a.reshape(-1).astype(F32) for a in arrs])
    block = OPT_ROW_BLOCK if flat.shape[0] > OPT_ROW_BLOCK * LANES else SUBLANES
    pad = (-flat.shape[0]) % (block * LANES)
    if pad:
        flat = jnp.concatenate([flat, jnp.zeros((pad,), F32)])
    return flat.reshape(-1, LANES)


def _unpack(packed, shapes, lead=()):
    flat = packed.reshape(lead + (-1,))
    out, off = [], 0
    for s in shapes:
        size = int(np.prod(s))
        out.append(flat[..., off:off + size].reshape(lead + tuple(s)))
        off += size
    return out


def _unshard_last(g):
    nd = g.ndim
    perm = tuple(range(1, nd - 1)) + (0, nd - 1)
    t = jnp.transpose(g, perm)
    return t.reshape(t.shape[:-2] + (N_DEV * g.shape[-1],))


def kernel(x, c, w_ada, b_ada, ln_g, ln_b, ffn_gu, ffn_down, gmlp_w_in, gmlp_b_in, gmlp_ln_g, gmlp_ln_b, gmlp_w_s, gmlp_b_s, gmlp_w_out, w_ada_kv, b_ada_kv, w_kv, attn_w_q, attn_rel_bias, attn_w_o, loss_target, m_w_ada, m_b_ada, m_ln_g, m_ln_b, m_ffn_gu, m_ffn_down, m_gmlp_w_in, m_gmlp_b_in, m_gmlp_ln_g, m_gmlp_ln_b, m_gmlp_w_s, m_gmlp_b_s, m_gmlp_w_out, m_w_ada_kv, m_b_ada_kv, m_w_kv, m_attn_w_q, m_attn_rel_bias, m_attn_w_o, v_w_ada, v_b_ada, v_ln_g, v_ln_b, v_ffn_gu, v_ffn_down, v_gmlp_w_in, v_gmlp_b_in, v_gmlp_ln_g, v_gmlp_ln_b, v_gmlp_w_s, v_gmlp_b_s, v_gmlp_w_out, v_w_ada_kv, v_b_ada_kv, v_w_kv, v_attn_w_q, v_attn_rel_bias, v_attn_w_o):
    weights = dict(w_ada=w_ada, b_ada=b_ada, ln_g=ln_g, ln_b=ln_b, ffn_gu=ffn_gu, ffn_down=ffn_down,
                   gmlp_w_in=gmlp_w_in, gmlp_b_in=gmlp_b_in, gmlp_ln_g=gmlp_ln_g, gmlp_ln_b=gmlp_ln_b,
                   gmlp_w_s=gmlp_w_s, gmlp_b_s=gmlp_b_s, gmlp_w_out=gmlp_w_out, w_ada_kv=w_ada_kv,
                   b_ada_kv=b_ada_kv, w_kv=w_kv, attn_w_q=attn_w_q, attn_rel_bias=attn_rel_bias,
                   attn_w_o=attn_w_o)
    mom1 = dict(w_ada=m_w_ada, b_ada=m_b_ada, ln_g=m_ln_g, ln_b=m_ln_b, ffn_gu=m_ffn_gu, ffn_down=m_ffn_down,
                gmlp_w_in=m_gmlp_w_in, gmlp_b_in=m_gmlp_b_in, gmlp_ln_g=m_gmlp_ln_g, gmlp_ln_b=m_gmlp_ln_b,
                gmlp_w_s=m_gmlp_w_s, gmlp_b_s=m_gmlp_b_s, gmlp_w_out=m_gmlp_w_out, w_ada_kv=m_w_ada_kv,
                b_ada_kv=m_b_ada_kv, w_kv=m_w_kv, attn_w_q=m_attn_w_q, attn_rel_bias=m_attn_rel_bias,
                attn_w_o=m_attn_w_o)
    mom2 = dict(w_ada=v_w_ada, b_ada=v_b_ada, ln_g=v_ln_g, ln_b=v_ln_b, ffn_gu=v_ffn_gu, ffn_down=v_ffn_down,
                gmlp_w_in=v_gmlp_w_in, gmlp_b_in=v_gmlp_b_in, gmlp_ln_g=v_gmlp_ln_g, gmlp_ln_b=v_gmlp_ln_b,
                gmlp_w_s=v_gmlp_w_s, gmlp_b_s=v_gmlp_b_s, gmlp_w_out=v_gmlp_w_out, w_ada_kv=v_w_ada_kv,
                b_ada_kv=v_b_ada_kv, w_kv=v_w_kv, attn_w_q=v_attn_w_q, attn_rel_bias=v_attn_rel_bias,
                attn_w_o=v_attn_w_o)
    order = list(weights)

    x = x[0]
    target = loss_target[0]
    t, d = x.shape
    n_mod = w_ada.shape[-1] * N_DEV // d
    mod_w = w_ada.shape[-1]
    kv_w = w_ada_kv.shape[-1]
    n_b = DEPTH - N_A
    me = _linear(*_position())

    l2 = DEPTH * 2
    big = dict(
        ffn_gu=ffn_gu.reshape((l2,) + ffn_gu.shape[2:]),
        ffn_down=ffn_down.reshape((l2,) + ffn_down.shape[2:]),
        gmlp_w_in=gmlp_w_in, gmlp_w_out=gmlp_w_out, w_kv=w_kv[None],
        attn_w_q=attn_w_q, attn_w_o=attn_w_o)
    big_names = list(big)
    core = lax.axis_index("c").astype(jnp.int32).reshape(1)
    chip = (2 * lax.axis_index("x") + lax.axis_index("y")).astype(jnp.int32).reshape(1)

    fwd_groups = [
        {"ffn_gu": (0, 1), "ffn_down": (0, 1)},
        {"gmlp_w_in": (0, 1), "gmlp_w_out": (0, 1), "ffn_gu": (1, 1), "ffn_down": (1, 1)},
        {"ffn_gu": (2, 2), "ffn_down": (2, 2), "gmlp_w_in": (1, 1), "gmlp_w_out": (1, 1), "w_kv": (0, 1)},
        {"ffn_gu": (4, 2), "ffn_down": (4, 2), "attn_w_q": (0, 1), "attn_w_o": (0, 1)},
        {"ffn_gu": (6, 2), "ffn_down": (6, 2), "attn_w_q": (1, 1), "attn_w_o": (1, 1)},
    ]
    bwd_groups = []
    for l in range(DEPTH):
        g = {"ffn_gu": (2 * l, 2), "ffn_down": (2 * l, 2)}
        if l < N_A:
            g.update({"gmlp_w_in": (l, 1), "gmlp_w_out": (l, 1)})
        else:
            g.update({"attn_w_q": (l - N_A, 1), "attn_w_o": (l - N_A, 1)})
        if l == N_A - 1:
            g["w_kv"] = (0, 1)
        bwd_groups.append(g)

    def slot_of(groups, name, slot):
        for gi, g in enumerate(groups):
            if name in g and g[name][0] <= slot < g[name][0] + g[name][1]:
                return gi, slot - g[name][0]
        raise KeyError((name, slot))

    flights = []
    for gi, g in enumerate(fwd_groups):
        lands = []
        for name, (s0, cnt) in g.items():
            shard = big[name][s0:s0 + cnt].astype(BF16)
            land = lax.empty((N_DEV,) + shard.shape, BF16)
            lands.append(lax.dynamic_update_slice(land, shard[None], (me,) + (0,) * shard.ndim))
        flights.append(_gather_ici_start(f"gather_ici_start_{gi}", lands))
    start_token = sum(f[3][0, 0] for f in flights)
    gathered = [None] * len(fwd_groups)

    def land_group(gi, after):
        send_sems, recv_sems, lands, _ = flights[gi]
        lands = _gather_ici_wait(f"gather_ici_wait_{gi}", lands, send_sems, recv_sems, after)
        gathered[gi] = dict(zip(fwd_groups[gi], _gather_d2d(lands)))

    def weight(name, slot):
        gi, local = slot_of(fwd_groups, name, slot)
        return gathered[gi][name], local

    partial = [{name: lax.empty((N_DEV, cnt) + big[name].shape[1:], BF16) for name, (_, cnt) in g.items()}
               for g in bwd_groups]

    c_all = _all_gather_small(_pack([c]))
    c_all = _unpack(c_all, [(d,)], lead=(N_DEV,))[0]
    c4 = _as4(c_all)
    mod_part = _matmul("ada_fwd", c4, w_ada[:, None], (DEPTH, 1, N_DEV, mod_w), F32, a_silu=True)
    kv_part = _matmul("ada_kv_fwd", c4, _as4(w_ada_kv), (1, 1, N_DEV, kv_w), F32, a_silu=True)
    small_shapes = [mod_part.shape, kv_part.shape, ln_g.shape, ln_b.shape, gmlp_b_in.shape,
                    gmlp_ln_g.shape, gmlp_ln_b.shape, attn_rel_bias.shape]
    small = _all_gather_small(_pack([mod_part, kv_part, ln_g, ln_b, gmlp_b_in, gmlp_ln_g, gmlp_ln_b,
                                     attn_rel_bias]))
    (mod_g, kvm_g, ln_g_g, ln_b_g, b_in_g, gln_g_g, gln_b_g, rel_g) = _unpack(small, small_shapes, lead=(N_DEV,))
    mod_mine = lax.dynamic_index_in_dim(mod_g[:, :, 0], me, axis=2, keepdims=False)
    mod = _unshard_last(mod_mine) + b_ada
    mod = mod.reshape(DEPTH, n_mod, 1, d)
    kvm_mine = lax.dynamic_index_in_dim(kvm_g[:, 0, 0], me, axis=1, keepdims=False)
    mkv = (_unshard_last(kvm_mine) + b_ada_kv).reshape(2, 1, d)
    ln_g_f = _unshard_last(ln_g_g)
    ln_b_f = _unshard_last(ln_b_g)
    half = N_DEV // 2
    b_in_f = jnp.transpose(b_in_g, (1, 0, 2))[:, :, None, :]
    gln_g_f = _unshard_last(gln_g_g).reshape(N_A, half, 1, -1)
    gln_b_f = _unshard_last(gln_b_g).reshape(N_A, half, 1, -1)
    rel_f = _unshard_last(rel_g)
    tq = np.arange(CHUNK)
    rb = np.arange(BAND)
    rel_idx = np.clip(tq[:, None] + LEFT_PAD - rb[None, :], -(CHUNK - 1), MAX_REL) + (CHUNK - 1)

    def shard_act(a):
        return a.reshape(a.shape[0], a.shape[2], a.shape[3])

    def grad_into(name, slot, mm):
        gi, local = slot_of(bwd_groups, name, slot)
        partial[gi][name] = mm(partial[gi][name], local)

    def ffn_fwd(h, lw):
        w_gu, l_gu = weight("ffn_gu", lw)
        w_dn, l_dn = weight("ffn_down", lw)
        n = w_gu.shape[-1]
        gu = _matmul("ffn_gu_fwd", _as4(h), w_gu, (N_DEV, 1, t, n), F32, lb=l_gu)
        a = _swiglu_fwd(shard_act(gu))
        y = _matmul("ffn_down_fwd", a[:, None], w_dn, (1, 1, t, d), F32, lb=l_dn, b_merge=2, reduce=True)
        return y[0, 0], (gu, a)

    def ffn_bwd(dy, h, saved, lw):
        gu, a = saved
        w_gu, l_gu = weight("ffn_gu", lw)
        w_dn, l_dn = weight("ffn_down", lw)
        n = w_gu.shape[-1]
        da = _matmul("ffn_down_bwd_a", _as4(dy), w_dn, (half, 1, t, n), F32, lb=l_dn, b_merge=2, tb=True)
        grad_into("ffn_down", lw, lambda buf, lo: _matmul(
            "ffn_down_bwd_w", a[:, None], _as4(dy), buf.shape, BF16, ta=True, lo=lo, out_merge=2, out_buf=buf))
        dgu = _swiglu_bwd(shard_act(gu), shard_act(da))
        dh = _matmul("ffn_gu_bwd_a", dgu[:, None], w_gu, (1, 1, t, d), F32, lb=l_gu, tb=True, reduce=True)
        grad_into("ffn_gu", lw, lambda buf, lo: _matmul(
            "ffn_gu_bwd_w", _as4(h), dgu[:, None], buf.shape, BF16, ta=True, lo=lo, out_buf=buf))
        return dh[0, 0], {}

    def gmlp_params(l):
        return (b_in_f[l], gln_g_f[l], gln_b_f[l], gmlp_w_s[l], gmlp_b_s[l][:, :, None])

    def gmlp_fwd(h, l):
        w_in, l_in = weight("gmlp_w_in", l)
        w_out, l_out = weight("gmlp_w_out", l)
        n = w_in.shape[-1]
        zpre = _matmul("gmlp_in_fwd", _as4(h), w_in, (N_DEV, 1, t, n), F32, lb=l_in)
        gated = _gmlp_mid_fwd(shard_act(zpre), *gmlp_params(l))
        y = _matmul("gmlp_out_fwd", gated[:, None], w_out, (1, 1, t, d), F32, lb=l_out, b_merge=2, reduce=True)
        return y[0, 0], (zpre, gated)

    def gmlp_bwd(dy, h, saved, l):
        zpre, gated = saved
        w_in, l_in = weight("gmlp_w_in", l)
        w_out, l_out = weight("gmlp_w_out", l)
        n = w_in.shape[-1]
        dgated = _matmul("gmlp_out_bwd_a", _as4(dy), w_out, (half, 1, t, n), F32, lb=l_out, b_merge=2, tb=True)
        grad_into("gmlp_w_out", l, lambda buf, lo: _matmul(
            "gmlp_out_bwd_w", gated[:, None], _as4(dy), buf.shape, BF16, ta=True, lo=lo, out_merge=2, out_buf=buf))
        dz, dws, dbs, dlng, dlnb, dbin = _gmlp_mid_bwd(shard_act(zpre), shard_act(dgated), *gmlp_params(l))
        dh = _matmul("gmlp_in_bwd_a", dz[:, None], w_in, (1, 1, t, d), F32, lb=l_in, tb=True, reduce=True)
        grad_into("gmlp_w_in", l, lambda buf, lo: _matmul(
            "gmlp_in_bwd_w", _as4(h), dz[:, None], buf.shape, BF16, ta=True, lo=lo, out_buf=buf))
        small_grads = dict(gmlp_w_s=dws, gmlp_b_s=dbs[:, :, 0], gmlp_ln_g=dlng.reshape(-1),
                           gmlp_ln_b=dlnb.reshape(-1), gmlp_b_in=dbin.reshape(-1))
        return dh[0, 0], small_grads

    def attn_fwd(h, j, kvp):
        bias = rel_f[j][:, rel_idx]
        w_q, l_q = weight("attn_w_q", j)
        w_o, l_o = weight("attn_w_o", j)
        q = _matmul("attn_q_fwd", _as4(h), w_q, (1, 1, t, d), BF16, lb=l_q, b_merge=N_DEV, reduce=True)[0, 0]
        o = _attn_fwd(q, kvp, bias)
        y = _matmul("attn_o_fwd", _as4(o), w_o, (1, 1, t, d), F32, lb=l_o, b_merge=N_DEV, reduce=True)
        return y[0, 0], (q, o, bias)

    def attn_bwd(dy, h, saved, j, kvp):
        q, o, bias = saved
        w_q, l_q = weight("attn_w_q", j)
        w_o, l_o = weight("attn_w_o", j)
        do = _matmul("attn_o_bwd_a", _as4(dy), w_o, (1, 1, t, d), BF16, lb=l_o, b_merge=N_DEV, tb=True)[0, 0]
        grad_into("attn_w_o", j, lambda buf, lo: _matmul(
            "attn_o_bwd_w", _as4(o), _as4(dy), buf.shape, BF16, ta=True, lo=lo, out_merge=N_DEV, out_buf=buf))
        dq, dkvp, dbias = _attn_bwd(q, do, kvp, bias)
        drel = _rel_bias_grad(jnp.transpose(dbias, (1, 0, 2)))
        dh = _matmul("attn_q_bwd_a", _as4(dq), w_q, (1, 1, t, d), F32, lb=l_q, b_merge=N_DEV, tb=True)
        grad_into("attn_w_q", j, lambda buf, lo: _matmul(
            "attn_q_bwd_w", _as4(h), _as4(dq), buf.shape, BF16, ta=True, lo=lo, out_merge=N_DEV, out_buf=buf))
        return dh[0, 0], dict(attn_rel_bias=drel, dkvp=dkvp)

    tape = []
    kvp = None
    kv_tape = None
    first_use = {(0, 0): 0, (0, 1): 1, (1, 0): 2, (2, 0): 3, (3, 0): 4}
    mod = mod.at[0, 0].add(start_token)
    for l in range(DEPTH):
        for i in range(3):
            if (l, i) in first_use:
                land_group(first_use[l, i], x)
            shift, scl, gate = mod[l, 3 * i], mod[l, 3 * i + 1], mod[l, 3 * i + 2]
            wgt = 1.0 if i == 1 else 0.5
            gw = wgt * (1.0 + gate)
            h = _modulate(x, scl, shift)
            if i != 1:
                y, saved = ffn_fwd(h, 2 * l + i // 2)
            elif l < N_A:
                y, saved = gmlp_fwd(h, l)
            else:
                y, saved = attn_fwd(h, l - N_A, kvp)
            x_new = _ln_res_fwd(x, y, gw, ln_g_f[l, i][None], ln_b_f[l, i][None])
            tape.append((x, h, y, gw, scl, saved))
            x = x_new
        if l == N_A - 1:
            hkv = _modulate(x, mkv[1], mkv[0])
            w_kvg, l_kv = weight("w_kv", 0)
            n = w_kvg.shape[-1]
            kv = _matmul("kv_fwd", _as4(hkv), w_kvg, (N_DEV, 1, t, n), BF16, lb=l_kv)
            kvp = jnp.pad(shard_act(kv), ((0, 0), (LEFT_PAD, 0), (0, 0)))
            kv_tape = (x, hkv)

    loss_part, dx = _loss_head(x, target)
    loss = lax.psum(loss_part[0, 0], MESH_AXES)

    d_mod = [[None] * n_mod for _ in range(DEPTH)]
    d_ln_g = [[None] * 3 for _ in range(DEPTH)]
    d_ln_b = [[None] * 3 for _ in range(DEPTH)]
    small_grads = {k: [None] * N_A for k in ("gmlp_w_s", "gmlp_b_s", "gmlp_ln_g", "gmlp_ln_b", "gmlp_b_in")}
    d_rel = [None] * n_b
    dkvp_sum = None
    d_mkv = None
    reductions = [None] * DEPTH
    sent_token = None
    for l in reversed(range(DEPTH)):
        if l == N_A - 1:
            x_kv, hkv = kv_tape
            w_kvg, l_kv = weight("w_kv", 0)
            dkv = dkvp_sum[:, LEFT_PAD:, :].astype(BF16)[:, None]
            dhkv = _matmul("kv_bwd_a", dkv, w_kvg, (1, 1, t, d), F32, lb=l_kv, tb=True, reduce=True)[0, 0]
            grad_into("w_kv", 0, lambda buf, lo: _matmul(
                "kv_bwd_w", _as4(hkv), dkv, buf.shape, BF16, ta=True, lo=lo, out_buf=buf))
            dx, ds_kv, db_kv = _mod_bwd(dx, dhkv, x_kv, mkv[1])
            d_mkv = jnp.concatenate([db_kv[0], ds_kv[0]])
        for i in reversed(range(3)):
            x_in, h, y, gw, scl, saved = tape[3 * l + i]
            wgt = 1.0 if i == 1 else 0.5
            if sent_token is not None:
                gw = gw + sent_token
                sent_token = None
            dx_res, dy, dgw, dg, db = _ln_res_bwd(x_in, y, gw, ln_g_f[l, i][None], dx)
            d_ln_g[l][i], d_ln_b[l][i] = dg[0], db[0]
            if i != 1:
                dh, extra = ffn_bwd(dy, h, saved, 2 * l + i // 2)
            elif l < N_A:
                dh, extra = gmlp_bwd(dy, h, saved, l)
                for k, g in extra.items():
                    small_grads[k][l] = g
            else:
                dh, extra = attn_bwd(dy, h, saved, l - N_A, kvp)
                d_rel[l - N_A] = extra["attn_rel_bias"]
                dkvp_sum = extra["dkvp"] if dkvp_sum is None else _add_kv(dkvp_sum, extra["dkvp"])
            dx, dscl, dshift = _mod_bwd(dx_res, dh, x_in, scl)
            d_mod[l][3 * i], d_mod[l][3 * i + 1], d_mod[l][3 * i + 2] = dshift[0], dscl[0], wgt * dgw[0]
        names = list(bwd_groups[l])
        from_sibling = _partials_d2d([partial[l][k] for k in names])
        sums = [_chip_sum(partial[l][k], r1, core) for k, r1 in zip(names, from_sibling)]
        lands = [lax.empty((len(_ALL_CHIPS) - 1,) + s.shape[1:], BF16) for s in sums]
        reductions[l] = _partials_ici_start(f"partials_ici_start_{l}", sums, lands)
        sent_token = reductions[l][4][0, 0]
    grad_x = dx[None]

    d_mod_arr = jnp.stack([jnp.concatenate(r) for r in d_mod])
    small_part = dict(
        b_ada=d_mod_arr, b_ada_kv=d_mkv,
        ln_g=jnp.stack([jnp.stack(r) for r in d_ln_g]), ln_b=jnp.stack([jnp.stack(r) for r in d_ln_b]),
        gmlp_b_in=jnp.stack(small_grads["gmlp_b_in"]), gmlp_ln_g=jnp.stack(small_grads["gmlp_ln_g"]),
        gmlp_ln_b=jnp.stack(small_grads["gmlp_ln_b"]), gmlp_w_s=jnp.stack(small_grads["gmlp_w_s"]),
        gmlp_b_s=jnp.stack(small_grads["gmlp_b_s"]), attn_rel_bias=jnp.stack(d_rel))
    small_names = list(small_part)
    sp_shapes = [small_part[k].shape for k in small_names]
    sp_all = _all_gather_small(_pack([small_part[k] for k in small_names]))
    sp_sum = _sum_parts(sp_all)
    full_grads = dict(zip(small_names, _unpack(sp_sum, sp_shapes)))
    per_dev = dict(zip(small_names, _unpack(sp_all, sp_shapes, lead=(N_DEV,))))

    def my_cols(a, width):
        return lax.dynamic_slice_in_dim(a, me * width, width, axis=a.ndim - 1)

    grads = {}
    grads["b_ada"] = full_grads["b_ada"]
    grads["b_ada_kv"] = full_grads["b_ada_kv"]
    grads["gmlp_w_s"] = full_grads["gmlp_w_s"]
    grads["gmlp_b_s"] = full_grads["gmlp_b_s"]
    for k in ("ln_g", "ln_b", "gmlp_b_in", "gmlp_ln_g", "gmlp_ln_b", "attn_rel_bias"):
        grads[k] = my_cols(full_grads[k], weights[k].shape[-1])

    dmod_cols = jnp.transpose(my_cols(per_dev["b_ada"], mod_w), (1, 0, 2))[:, None]
    grads["w_ada"] = _matmul("ada_bwd_w", c4, dmod_cols, (DEPTH, 1, d, mod_w), F32, ta=True,
                             a_silu=True)[:, 0]
    dkv_cols = my_cols(per_dev["b_ada_kv"], kv_w)[None, None]
    grads["w_ada_kv"] = _matmul("ada_kv_bwd_w", c4, dkv_cols, (1, 1, d, kv_w), F32, ta=True,
                                a_silu=True)[0, 0]

    delta, new_m, new_v = {}, {}, {}
    first = jnp.zeros((1,), jnp.int32)

    def flat2(a, cols):
        return a.reshape(-1, cols)

    for k in ("w_ada", "w_ada_kv"):
        w = weights[k]
        cols = w.shape[-1]
        res = _adamw(grads[k].reshape(1, -1, cols), first, None, flat2(w, cols), flat2(mom1[k], cols),
                     flat2(mom2[k], cols))
        grads[k], delta[k], new_m[k], new_v[k] = (a.reshape(w.shape) for a in res)

    tiny = [k for k in order if k not in delta and k not in big_names]
    tiny_shapes = [weights[k].shape for k in tiny]
    tiny_out = _adamw(_pack([grads[k] for k in tiny])[None], first, None, _pack([weights[k] for k in tiny]),
                      _pack([mom1[k] for k in tiny]), _pack([mom2[k] for k in tiny]))
    for dst, arr in zip((grads, delta, new_m, new_v), tiny_out):
        for k, val in zip(tiny, _unpack(arr, tiny_shapes)):
            dst[k] = val

    bufs = {k: [lax.empty(flat2(weights[k], weights[k].shape[-1]).shape, F32) for _ in range(4)]
            for k in big_names}
    done = tiny_out[0]
    for l in reversed(range(DEPTH)):
        send_sems, recv_sems, sums, lands, _ = reductions[l]
        sums, lands = _partials_ici_wait(f"partials_ici_wait_{l}", sums, lands, send_sems, recv_sems, done)
        for k, own, got in zip(bwd_groups[l], sums, lands):
            cols = weights[k].shape[-1]
            slot_rows = int(np.prod(big[k].shape[1:-1]))
            bufs[k] = _adamw(own.reshape(own.shape[0], -1, cols), chip, got.reshape(got.shape[0], -1, cols),
                             flat2(weights[k], cols), flat2(mom1[k], cols), flat2(mom2[k], cols),
                             row0=bwd_groups[l][k][0] * slot_rows, bufs=bufs[k])
            done = bufs[k][0][:SUBLANES, :LANES]
    for k in big_names:
        grads[k], delta[k], new_m[k], new_v[k] = (b.reshape(weights[k].shape) for b in bufs[k])

    return (loss, grad_x, *[grads[k] for k in order], *[delta[k] for k in order],
            *[new_m[k] for k in order], *[new_v[k] for k in order])


def _add_kv(a, b):
    s, rows, n = a.shape
    return _add(a.reshape(s * rows, n), b.reshape(s * rows, n)).reshape(s, rows, n)
```

```python
import functools

import numpy as np
import jax
import jax.numpy as jnp
from jax import lax
from jax.experimental import pallas as pl
from jax.experimental.pallas import tpu as pltpu

F32 = jnp.float32
BF16 = jnp.bfloat16
MESH_AXES = ("x", "y", "c")
N_DEV = 8
MESH_ID = pl.DeviceIdType.MESH

DEPTH = 4
N_A = 2
CHUNK = 64
N_HEADS = 16
LEFT_CHUNKS = 8
BAND = (LEFT_CHUNKS + 1) * CHUNK
LEFT_PAD = LEFT_CHUNKS * CHUNK
MAX_REL = 4 * CHUNK
N_REL = (CHUNK - 1) + MAX_REL + 1
GMLP_WINDOW = 128
GMLP_GROUPS = 8
ALPHA = (2.0 * DEPTH) ** 0.25
LN_EPS = 1e-5
ADAM_LR = 0.001
ADAM_B1 = 0.9
ADAM_B2 = 0.999
ADAM_EPS = 1e-08
ADAM_WD = 0.01
ADAM_STEP = 10

V7X_VMEM_BYTES = 64 * 1024 * 1024
VMEM_LIMIT = V7X_VMEM_BYTES - 8 * 1024 * 1024
LANES = 128
SUBLANES = 8
MM_BLOCK = 1024
ROW_BLOCK = 512
OPT_ROW_BLOCK = 128

_ANY = pl.BlockSpec(memory_space=pl.ANY)
_VMEM = pl.BlockSpec(memory_space=pltpu.VMEM)


def _params(sem=None):
    return pltpu.CompilerParams(dimension_semantics=sem, vmem_limit_bytes=VMEM_LIMIT)


def _row_block(rows, target):
    for d in range(min(rows, target), 0, -1):
        if rows % d == 0 and (d % SUBLANES == 0 or d == rows):
            return d
    return rows


def _matmul(name, a, b, out_shape4, out_dtype, *, la=0, lb=0, lo=0, ta=False, tb=False,
            reduce=False, b_merge=1, out_merge=1, out_buf=None, a_silu=False):
    ja_n, _, a_r, a_c = a.shape
    jb_n, _, b_r, b_c = b.shape
    jo_n, _, o_r, o_c = out_shape4
    m_tot = a_c if ta else a_r
    k_a = a_r if ta else a_c
    b_rows = b_merge * b_r
    k_c = b_c if tb else b_rows
    n = b_rows if tb else b_c
    n_chunks = (jb_n // b_merge) if reduce else 1
    natural_k = reduce and ja_n == 1
    assert n == o_c, (name, n, o_c)
    assert k_a ==(k_c * n_chunks if natural_k else k_c), (name, k_a, k_c, n_chunks)
    bk = k_c if (k_c <= MM_BLOCK or (b_merge > 1 and not tb)) else MM_BLOCK
    assert k_c % bk == 0
    nkk = k_c // bk
    nk = n_chunks * nkk
    m_out = out_merge * o_r
    assert m_tot == m_out, (name, m_tot, m_out)
    bm = m_tot if (m_tot <= MM_BLOCK or out_merge > 1) else MM_BLOCK
    assert m_tot % bm == 0
    jo_blocks = jo_n // out_merge

    def a_index(j, m, k):
        kj, kk = k // nkk, k % nkk
        ja = 0 if ja_n == 1 else (kj if reduce else j)
        ke = kk + kj * nkk if natural_k else kk
        return (ja, la, ke, m) if ta else (ja, la, m, ke)

    def b_index(j, m, k):
        kj, kk = k // nkk, k % nkk
        jb = 0 if jb_n == b_merge else (kj if reduce else j)
        return (jb, lb, 0, kk) if tb else (jb, lb, kk, 0)

    def o_index(j, m, k):
        return (j, lo, 0, 0) if out_merge > 1 else (j, lo, m, 0)

    a_block = (None, None, bk, bm) if ta else (None, None, bm, bk)
    if b_merge > 1:
        b_block = (b_merge, None, b_r, bk if tb else n)
    else:
        b_block = (None, None, n, bk) if tb else (None, None, bk, n)
    o_block = (out_merge, None, o_r, n) if out_merge > 1 else (None, None, bm, n)
    dims = (((0 if ta else 1,), (1 if tb else 0,)), ((), ()))

    def body(a_ref, b_ref, *rest):
        o_ref, acc_ref = rest[-2], rest[-1]
        k = pl.program_id(2)

        @pl.when(k == 0)
        def _():
            acc_ref[...] = jnp.zeros_like(acc_ref)

        av = a_ref[...]
        if a_silu:
            af = av.astype(F32)
            av = af * jax.nn.sigmoid(af)
        bv = b_ref[...]
        if b_merge > 1:
            bv = bv.reshape(b_rows, bv.shape[-1])
        acc_ref[...] += lax.dot_general(av.astype(BF16), bv.astype(BF16), dims,
                                        preferred_element_type=F32)

        @pl.when(k == nk - 1)
        def _():
            ov = acc_ref[...].astype(out_dtype)
            if out_merge > 1:
                ov = ov.reshape(out_merge, o_r, n)
            o_ref[...] = ov

    in_specs = [pl.BlockSpec(a_block, a_index), pl.BlockSpec(b_block, b_index)]
    operands = [a, b]
    aliases = {}
    if out_buf is not None:
        assert out_buf.shape == tuple(out_shape4) and out_buf.dtype == out_dtype
        in_specs.append(_ANY)
        operands.append(out_buf)
        aliases = {2: 0}
    return pl.pallas_call(
        body, name=name,
        grid=(jo_blocks, m_tot // bm, nk),
        in_specs=in_specs,
        out_specs=pl.BlockSpec(o_block, o_index),
        out_shape=jax.ShapeDtypeStruct(tuple(out_shape4), out_dtype),
        scratch_shapes=[pltpu.VMEM((bm, n), F32)],
        input_output_aliases=aliases,
        compiler_params=_params(("parallel", "parallel", "arbitrary")),
    )(*operands)


def _as4(a):
    return a.reshape((1,) * (4 - a.ndim) + a.shape)


def _row_call(name, body, ins, outs, t, *, acc_outs=()):
    bt = _row_block(t, ROW_BLOCK)

    def spec(arr, tiled):
        if tiled:
            return pl.BlockSpec((bt,) + tuple(arr.shape[1:]), lambda i: (i,) + (0,) * (arr.ndim - 1))
        return pl.BlockSpec(tuple(arr.shape), lambda i: (0,) * arr.ndim)

    return pl.pallas_call(
        body, name=name, grid=(t // bt,),
        in_specs=[spec(a, tl) for a, tl in ins],
        out_specs=[spec(o, tl) for o, tl in outs],
        out_shape=[jax.ShapeDtypeStruct(o.shape, o.dtype) for o, _ in outs],
        compiler_params=_params(("arbitrary",) if acc_outs else ("parallel",)),
    )(*[a for a, _ in ins])


def _sds(shape, dtype):
    return jax.ShapeDtypeStruct(tuple(shape), dtype)


def _modulate(x, scl, shift):
    t, d = x.shape

    def body(x_ref, s_ref, b_ref, h_ref):
        h_ref[...] = (x_ref[...] * (1.0 + s_ref[...]) + b_ref[...]).astype(BF16)

    return _row_call("modulate", body, [(x, True), (scl, False), (shift, False)],
                     [(_sds((t, d), BF16), True)], t)[0]


def _ln_stats(r):
    mu = jnp.mean(r, axis=-1, keepdims=True)
    rc = r - mu
    var = jnp.mean(rc * rc, axis=-1, keepdims=True)
    rstd = lax.rsqrt(var + LN_EPS)
    return rc * rstd, rstd


def _ln_res_fwd(x, y, gw, g, b):
    t, d = x.shape

    def body(x_ref, y_ref, gw_ref, g_ref, b_ref, o_ref):
        r = ALPHA * x_ref[...] + gw_ref[...] * y_ref[...]
        xhat, _ = _ln_stats(r)
        o_ref[...] = xhat * g_ref[...] + b_ref[...]

    return _row_call("ln_res_fwd", body,
                     [(x, True), (y, True), (gw, False), (g, False), (b, False)],
                     [(_sds((t, d), F32), True)], t)[0]


def _ln_res_bwd(x, y, gw, g, dxn):
    t, d = x.shape

    def body(x_ref, y_ref, gw_ref, g_ref, dxn_ref, dx_ref, dy_ref, dgw_ref, dg_ref, db_ref):
        @pl.when(pl.program_id(0) == 0)
        def _():
            dgw_ref[...] = jnp.zeros_like(dgw_ref)
            dg_ref[...] = jnp.zeros_like(dg_ref)
            db_ref[...] = jnp.zeros_like(db_ref)

        yv = y_ref[...]
        gwv = gw_ref[...]
        dxn = dxn_ref[...]
        xhat, rstd = _ln_stats(ALPHA * x_ref[...] + gwv * yv)
        dxh = dxn * g_ref[...]
        m1 = jnp.mean(dxh, axis=-1, keepdims=True)
        m2 = jnp.mean(dxh * xhat, axis=-1, keepdims=True)
        dr = rstd * (dxh - m1 - xhat * m2)
        dx_ref[...] = ALPHA * dr
        dy_ref[...] = (gwv * dr).astype(BF16)
        dgw_ref[...] += jnp.sum(dr * yv, axis=0, keepdims=True)
        dg_ref[...] += jnp.sum(dxn * xhat, axis=0, keepdims=True)
        db_ref[...] += jnp.sum(dxn, axis=0, keepdims=True)

    vec = _sds((1, d), F32)
    return _row_call("ln_res_bwd", body,
                     [(x, True), (y, True), (gw, False), (g, False), (dxn, True)],
                     [(_sds((t, d), F32), True), (_sds((t, d), BF16), True),
                      (vec, False), (vec, False), (vec, False)], t, acc_outs=(2, 3, 4))


def _mod_bwd(dx_res, dh, x, scl):
    t, d = x.shape

    def body(dxr_ref, dh_ref, x_ref, s_ref, dx_ref, ds_ref, db_ref):
        @pl.when(pl.program_id(0) == 0)
        def _():
            ds_ref[...] = jnp.zeros_like(ds_ref)
            db_ref[...] = jnp.zeros_like(db_ref)

        dh = dh_ref[...]
        dx_ref[...] = dxr_ref[...] + dh * (1.0 + s_ref[...])
        ds_ref[...] += jnp.sum(dh * x_ref[...], axis=0, keepdims=True)
        db_ref[...] += jnp.sum(dh, axis=0, keepdims=True)

    vec = _sds((1, d), F32)
    return _row_call("mod_bwd", body, [(dx_res, True), (dh, True), (x, True), (scl, False)],
                     [(_sds((t, d), F32), True), (vec, False), (vec, False)], t, acc_outs=(1, 2))


def _loss_head(y, target):
    t, d = y.shape

    def body(y_ref, t_ref, l_ref, dy_ref):
        @pl.when(pl.program_id(0) == 0)
        def _():
            l_ref[...] = jnp.zeros_like(l_ref)

        err = y_ref[...] - t_ref[...]
        dy_ref[...] = err * (1.0 / d)
        part = 0.5 * jnp.sum(jnp.mean(err * err, axis=-1, keepdims=True), axis=0, keepdims=True)
        l_ref[...] += jnp.broadcast_to(part, l_ref.shape)

    return _row_call("loss_head", body, [(y, True), (target, True)],
                     [(_sds((SUBLANES, LANES), F32), False), (_sds((t, d), F32), True)], t,
                     acc_outs=(0,))


def _swiglu_fwd(gu):
    _, t, n = gu.shape
    half = N_DEV // 2
    bt = _row_block(t, MM_BLOCK)
    gu4 = gu.reshape(2, half, t, n)

    def body(gu_ref, a_ref):
        g = gu_ref[0]
        a_ref[...] = (g * jax.nn.sigmoid(g) * gu_ref[1]).astype(BF16)

    return pl.pallas_call(
        body, name="swiglu_fwd", grid=(half, t // bt),
        in_specs=[pl.BlockSpec((2, None, bt, n), lambda j, i: (0, j, i, 0))],
        out_specs=pl.BlockSpec((None, bt, n), lambda j, i: (j, i, 0)),
        out_shape=_sds((half, t, n), BF16),
        compiler_params=_params(("parallel", "parallel")),
    )(gu4)


def _swiglu_bwd(gu, da):
    _, t, n = gu.shape
    half = N_DEV // 2
    bt = _row_block(t, MM_BLOCK)
    gu4 = gu.reshape(2, half, t, n)

    def body(gu_ref, da_ref, d_ref):
        g = gu_ref[0]
        u = gu_ref[1]
        da = da_ref[...]
        sig = jax.nn.sigmoid(g)
        d_ref[0] = (da * u * sig * (1.0 + g * (1.0 - sig))).astype(BF16)
        d_ref[1] = (da * g * sig).astype(BF16)

    out = pl.pallas_call(
        body, name="swiglu_bwd", grid=(half, t // bt),
        in_specs=[pl.BlockSpec((2, None, bt, n), lambda j, i: (0, j, i, 0)),
                  pl.BlockSpec((None, bt, n), lambda j, i: (j, i, 0))],
        out_specs=pl.BlockSpec((2, None, bt, n), lambda j, i: (0, j, i, 0)),
        out_shape=_sds((2, half, t, n), BF16),
        compiler_params=_params(("parallel", "parallel")),
    )(gu4, da)
    return out.reshape(N_DEV, t, n)


_INV_SQRT2 = 0.7071067811865476
_INV_SQRT_2PI = 0.3989422804014327


def _gelu(z):
    return 0.5 * z * (1.0 + lax.erf(z * _INV_SQRT2))


def _gelu_grad(z):
    return 0.5 * (1.0 + lax.erf(z * _INV_SQRT2)) + z * jnp.exp(-0.5 * z * z) * _INV_SQRT_2PI


def _window_mask():
    t_out = lax.broadcasted_iota(jnp.int32, (GMLP_WINDOW, GMLP_WINDOW), 0)
    s_in = lax.broadcasted_iota(jnp.int32, (GMLP_WINDOW, GMLP_WINDOW), 1)
    return (s_in // CHUNK) <= (t_out // CHUNK)


def _gmlp_recompute(z_ref, bin_ref, lng_ref, lnb_ref):
    half = N_DEV // 2
    z = z_ref[...] + bin_ref[...]
    ge = _gelu(z)
    u = ge[:half]
    v = ge[half:]
    width = half * v.shape[-1]
    mu = jnp.sum(jnp.sum(v, axis=0), axis=-1, keepdims=True) / width
    vc = v - mu
    var = jnp.sum(jnp.sum(vc * vc, axis=0), axis=-1, keepdims=True) / width
    rstd = lax.rsqrt(var + LN_EPS)
    xhat = vc * rstd
    vn = xhat * lng_ref[...] + lnb_ref[...]
    return z, u, xhat, rstd, vn


def _gmlp_mid_fwd(zpre, b_in, ln_g, ln_b, w_s, b_s):
    _, t, n = zpre.shape
    half = N_DEV // 2
    gd = half * n // GMLP_GROUPS
    per = n // gd
    w = GMLP_WINDOW

    def body(z_ref, bin_ref, lng_ref, lnb_ref, ws_ref, bs_ref, o_ref):
        _, u, _, _, vn = _gmlp_recompute(z_ref, bin_ref, lng_ref, lnb_ref)
        mask = _window_mask()
        for g in range(GMLP_GROUPS):
            sh, c0 = g // per, (g % per) * gd
            wsm = jnp.where(mask, ws_ref[g], 0.0).astype(BF16)
            s = jnp.dot(wsm, vn[sh][:, c0:c0 + gd].astype(BF16), preferred_element_type=F32) + bs_ref[g]
            o_ref[sh, :, c0:c0 + gd] = (u[sh][:, c0:c0 + gd] * s).astype(BF16)

    whole = lambda a: pl.BlockSpec(tuple(a.shape), lambda i: (0,) * a.ndim)
    return pl.pallas_call(
        body, name="gmlp_mid_fwd", grid=(t // w,),
        in_specs=[pl.BlockSpec((N_DEV, w, n), lambda i: (0, i, 0)),
                  whole(b_in), whole(ln_g), whole(ln_b), whole(w_s), whole(b_s)],
        out_specs=pl.BlockSpec((half, w, n), lambda i: (0, i, 0)),
        out_shape=_sds((half, t, n), BF16),
        compiler_params=_params(("parallel",)),
    )(zpre, b_in, ln_g, ln_b, w_s, b_s)


def _gmlp_mid_bwd(zpre, dgated, b_in, ln_g, ln_b, w_s, b_s):
    _, t, n = zpre.shape
    half = N_DEV // 2
    gd = half * n // GMLP_GROUPS
    per = n // gd
    w = GMLP_WINDOW
    width = half * n

    def body(z_ref, dg_ref, bin_ref, lng_ref, lnb_ref, ws_ref, bs_ref,
             dz_ref, dws_ref, dbs_ref, dlng_ref, dlnb_ref, dbin_ref, du_ref, dvn_ref):
        @pl.when(pl.program_id(0) == 0)
        def _():
            for r in (dws_ref, dbs_ref, dlng_ref, dlnb_ref, dbin_ref):
                r[...] = jnp.zeros_like(r)

        z, u, xhat, rstd, vn = _gmlp_recompute(z_ref, bin_ref, lng_ref, lnb_ref)
        mask = _window_mask()
        for g in range(GMLP_GROUPS):
            sh, c0 = g // per, (g % per) * gd
            wsm = jnp.where(mask, ws_ref[g], 0.0).astype(BF16)
            vg = vn[sh][:, c0:c0 + gd].astype(BF16)
            s = jnp.dot(wsm, vg, preferred_element_type=F32) + bs_ref[g]
            dgt = dg_ref[sh, :, c0:c0 + gd]
            ds = dgt * u[sh][:, c0:c0 + gd]
            du_ref[sh, :, c0:c0 + gd] = dgt * s
            dsb = ds.astype(BF16)
            dws = lax.dot_general(dsb, vg, (((1,), (1,)), ((), ())), preferred_element_type=F32)
            dws_ref[g] += jnp.where(mask, dws, 0.0)
            dbs_ref[g] += jnp.sum(ds, axis=-1, keepdims=True)
            dvn_ref[sh, :, c0:c0 + gd] = lax.dot_general(wsm, dsb, (((0,), (0,)), ((), ())),
                                                         preferred_element_type=F32)
        dvn = dvn_ref[...]
        dlng_ref[...] += jnp.sum(dvn * xhat, axis=1, keepdims=True)
        dlnb_ref[...] += jnp.sum(dvn, axis=1, keepdims=True)
        dxh = dvn * lng_ref[...]
        m1 = jnp.sum(jnp.sum(dxh, axis=0), axis=-1, keepdims=True) / width
        m2 = jnp.sum(jnp.sum(dxh * xhat, axis=0), axis=-1, keepdims=True) / width
        dv = rstd * (dxh - m1 - xhat * m2)
        gg = _gelu_grad(z)
        dzu = du_ref[...] * gg[:half]
        dzv = dv * gg[half:]
        dz_ref[:half] = dzu.astype(BF16)
        dz_ref[half:] = dzv.astype(BF16)
        dbin_ref[:half] += jnp.sum(dzu, axis=1, keepdims=True)
        dbin_ref[half:] += jnp.sum(dzv, axis=1, keepdims=True)

    whole = lambda a: pl.BlockSpec(tuple(a.shape), lambda i: (0,) * a.ndim)
    outs = [_sds((N_DEV, t, n), BF16), _sds(w_s.shape, F32), _sds(b_s.shape, F32),
            _sds(ln_g.shape, F32), _sds(ln_b.shape, F32), _sds(b_in.shape, F32)]
    return pl.pallas_call(
        body, name="gmlp_mid_bwd", grid=(t // w,),
        in_specs=[pl.BlockSpec((N_DEV, w, n), lambda i: (0, i, 0)),
                  pl.BlockSpec((half, w, n), lambda i: (0, i, 0)),
                  whole(b_in), whole(ln_g), whole(ln_b), whole(w_s), whole(b_s)],
        out_specs=[pl.BlockSpec((N_DEV, w, n), lambda i: (0, i, 0))] + [whole(o) for o in outs[1:]],
        out_shape=outs,
        scratch_shapes=[pltpu.VMEM((half, w, n), F32), pltpu.VMEM((half, w, n), F32)],
        compiler_params=_params(("arbitrary",)),
    )(zpre, dgated, b_in, ln_g, ln_b, w_s, b_s)


ATTN_CHUNKS = 4
ATTN_ROWS = ATTN_CHUNKS * CHUNK
ATTN_WINDOW = ATTN_ROWS + LEFT_PAD
ATTN_DIAGS = 1024
ATTN_ROLL = ATTN_DIAGS - (ATTN_ROWS - 1)


def _rel_vector(rel):
    j = np.arange(ATTN_DIAGS)
    idx = np.clip(ATTN_WINDOW - 1 - j, -(CHUNK - 1), MAX_REL) + (CHUNK - 1)
    return rel[:, idx]


def _attn_bias_mask(rel_ref, bm_ref):
    tt = lax.broadcasted_iota(jnp.int32, (ATTN_ROWS, ATTN_WINDOW), 0) // CHUNK
    rr = lax.broadcasted_iota(jnp.int32, (ATTN_ROWS, ATTN_WINDOW), 1) // CHUNK
    band = (rr >= tt) & (rr <= tt + LEFT_CHUNKS)
    for j in range(bm_ref.shape[0]):
        vec = jnp.broadcast_to(rel_ref[j:j + 1, :], (ATTN_ROWS, ATTN_DIAGS))
        toeplitz = pltpu.roll(vec, ATTN_ROLL, 1, stride=1, stride_axis=0)[:, :ATTN_WINDOW]
        bm_ref[j] = jnp.where(band, toeplitz, -jnp.inf)


def _attn_probs(q_ref, k_ref, bm_ref, j, hd, start, valid):
    qh = q_ref[:, j * hd:(j + 1) * hd]
    kb = k_ref[pl.ds(start, ATTN_WINDOW), j * hd:(j + 1) * hd]
    sc = lax.dot_general(qh, kb, (((1,), (1,)), ((), ())), preferred_element_type=F32)
    sc = sc * (hd ** -0.5) + bm_ref[j]
    sc = jnp.where(valid, sc, -jnp.inf)
    sc = sc - jnp.max(sc, axis=-1, keepdims=True)
    e = jnp.exp(sc)
    return e / jnp.sum(e, axis=-1, keepdims=True), qh, kb


def _window_valid(start):
    r = lax.broadcasted_iota(jnp.int32, (1, ATTN_WINDOW), 1)
    return (start + r) >= LEFT_PAD


def _attn_fwd(q, kvp, rel_vec):
    t, d = q.shape
    hd = d // N_HEADS
    half = N_DEV // 2
    n = kvp.shape[-1]
    per = n // hd
    rows = kvp.shape[1]

    def body(q_ref, k_ref, v_ref, rel_ref, o_ref, bm_ref):
        @pl.when(pl.program_id(1) == 0)
        def _():
            _attn_bias_mask(rel_ref, bm_ref)

        start = pl.multiple_of(pl.program_id(1) * ATTN_ROWS, ATTN_ROWS)
        valid = _window_valid(start)
        for j in range(per):
            p, _, _ = _attn_probs(q_ref, k_ref, bm_ref, j, hd, start, valid)
            vb = v_ref[pl.ds(start, ATTN_WINDOW), j * hd:(j + 1) * hd]
            o_ref[:, j * hd:(j + 1) * hd] = jnp.dot(p.astype(BF16), vb, preferred_element_type=F32).astype(BF16)

    return pl.pallas_call(
        body, name="attn_fwd", grid=(half, t // ATTN_ROWS),
        in_specs=[pl.BlockSpec((ATTN_ROWS, n), lambda g, i: (i, g)),
                  pl.BlockSpec((None, rows, n), lambda g, i: (g, 0, 0)),
                  pl.BlockSpec((None, rows, n), lambda g, i: (half + g, 0, 0)),
                  pl.BlockSpec((None, per, ATTN_DIAGS), lambda g, i: (g, 0, 0))],
        out_specs=pl.BlockSpec((ATTN_ROWS, n), lambda g, i: (i, g)),
        out_shape=_sds((t, d), BF16),
        scratch_shapes=[pltpu.VMEM((per, ATTN_ROWS, ATTN_WINDOW), F32)],
        compiler_params=_params(("arbitrary", "arbitrary")),
    )(q, kvp, kvp, rel_vec.reshape(half, per, ATTN_DIAGS))


def _attn_bwd(q, dout, kvp, rel_vec, dk_in=None, dv_in=None):
    t, d = q.shape
    hd = d // N_HEADS
    half = N_DEV // 2
    n = kvp.shape[-1]
    per = n // hd
    rows = kvp.shape[1]
    scale = hd ** -0.5
    carry = dk_in is not None

    def body(q_ref, do_ref, k_ref, v_ref, rel_ref, *rest):
        dq_ref, dk_ref, dv_ref, dsc_ref, bm_ref = rest[-5:]

        @pl.when(pl.program_id(1) == 0)
        def _():
            _attn_bias_mask(rel_ref, bm_ref)
            dk_ref[...] = rest[0][...] if carry else jnp.zeros_like(dk_ref)
            dv_ref[...] = rest[1][...] if carry else jnp.zeros_like(dv_ref)
            dsc_ref[...] = jnp.zeros_like(dsc_ref)

        start = pl.multiple_of(pl.program_id(1) * ATTN_ROWS, ATTN_ROWS)
        valid = _window_valid(start)
        for j in range(per):
            cols = slice(j * hd, (j + 1) * hd)
            p, qh, kb = _attn_probs(q_ref, k_ref, bm_ref, j, hd, start, valid)
            vb = v_ref[pl.ds(start, ATTN_WINDOW), cols]
            doh = do_ref[:, cols]
            dp = lax.dot_general(doh, vb, (((1,), (1,)), ((), ())), preferred_element_type=F32)
            ds = p * (dp - jnp.sum(dp * p, axis=-1, keepdims=True))
            dsc_ref[j] += ds
            dsb = (ds * scale).astype(BF16)
            dq_ref[:, cols] = jnp.dot(dsb, kb, preferred_element_type=F32).astype(BF16)
            dk_ref[pl.ds(start, ATTN_WINDOW), cols] += lax.dot_general(
                dsb, qh, (((0,), (0,)), ((), ())), preferred_element_type=F32)
            dv_ref[pl.ds(start, ATTN_WINDOW), cols] += lax.dot_general(
                p.astype(BF16), doh, (((0,), (0,)), ((), ())), preferred_element_type=F32)

    tile = pl.BlockSpec((ATTN_ROWS, n), lambda g, i: (i, g))
    shard = pl.BlockSpec((None, rows, n), lambda g, i: (g, 0, 0))
    in_specs = [tile, tile, shard, pl.BlockSpec((None, rows, n), lambda g, i: (half + g, 0, 0)),
                pl.BlockSpec((None, per, ATTN_DIAGS), lambda g, i: (g, 0, 0))]
    operands = [q, dout, kvp, kvp, rel_vec.reshape(half, per, ATTN_DIAGS)]
    if carry:
        in_specs += [shard, shard]
        operands += [dk_in, dv_in]
    acc = _sds((half, rows, n), F32)
    return pl.pallas_call(
        body, name="attn_bwd", grid=(half, t // ATTN_ROWS),
        in_specs=in_specs,
        out_specs=[tile, shard, shard, pl.BlockSpec((per, ATTN_ROWS, ATTN_WINDOW), lambda g, i: (g, 0, 0))],
        out_shape=[_sds((t, d), BF16), acc, acc, _sds((N_HEADS, ATTN_ROWS, ATTN_WINDOW), F32)],
        scratch_shapes=[pltpu.VMEM((per, ATTN_ROWS, ATTN_WINDOW), F32)],
        compiler_params=_params(("arbitrary", "arbitrary")),
    )(*operands)


ATTN_SKEW = ATTN_DIAGS + 1
ATTN_SKEW_LANES = -(-ATTN_SKEW // LANES) * LANES


def _skew_diagonals(dsc):
    h = dsc.shape[0]
    wide = jnp.pad(dsc, ((0, 0), (0, 0), (0, ATTN_DIAGS - ATTN_WINDOW))).reshape(h, ATTN_ROWS * ATTN_DIAGS)
    wide = jnp.pad(wide, ((0, 0), (0, ATTN_ROWS))).reshape(h, ATTN_ROWS, ATTN_SKEW)
    return jnp.pad(wide, ((0, 0), (0, 0), (0, ATTN_SKEW_LANES - ATTN_SKEW)))


def _rel_bias_grad(skewed):
    heads = skewed.shape[0]
    hb = SUBLANES

    def body(d_ref, o_ref):
        col = lax.broadcasted_iota(jnp.int32, (ATTN_SKEW_LANES, N_REL), 0)
        bucket = lax.broadcasted_iota(jnp.int32, (ATTN_SKEW_LANES, N_REL), 1)
        diag = jnp.where(col < ATTN_WINDOW, col, col - ATTN_SKEW)
        idx = jnp.clip(LEFT_PAD - diag, -(CHUNK - 1), MAX_REL) + (CHUNK - 1)
        oh = ((idx == bucket) & (col < ATTN_SKEW)).astype(BF16)
        dv = jnp.sum(d_ref[...], axis=1)
        hi = dv.astype(BF16)
        rest = dv - hi.astype(F32)
        mid = rest.astype(BF16)
        lo = (rest - mid.astype(F32)).astype(BF16)
        acc = jnp.dot(hi, oh, preferred_element_type=F32)
        acc += jnp.dot(mid, oh, preferred_element_type=F32)
        acc += jnp.dot(lo, oh, preferred_element_type=F32)
        o_ref[...] = acc

    return pl.pallas_call(
        body, name="rel_bias_grad", grid=(heads // hb,),
        in_specs=[pl.BlockSpec((hb, ATTN_ROWS, ATTN_SKEW_LANES), lambda i: (i, 0, 0))],
        out_specs=pl.BlockSpec((hb, N_REL), lambda i: (i, 0)),
        out_shape=_sds((heads, N_REL), F32),
        compiler_params=_params(("parallel",)),
    )(skewed)


def _sum_parts(parts):
    s_n, rows, c = parts.shape
    br = _row_block(rows, OPT_ROW_BLOCK)

    def body(p_ref, o_ref):
        acc = p_ref[0].astype(F32)
        for s in range(1, s_n):
            acc = acc + p_ref[s].astype(F32)
        o_ref[...] = acc

    return pl.pallas_call(
        body, name="sum_parts", grid=(rows // br,),
        in_specs=[pl.BlockSpec((s_n, br, c), lambda i: (0, i, 0))],
        out_specs=pl.BlockSpec((br, c), lambda i: (i, 0)),
        out_shape=_sds((rows, c), F32),
        compiler_params=_params(("parallel",)),
    )(parts)


def _adamw(own, own_idx, parts, w, m, v, row0=0, bufs=None):
    _, rows, c = own.shape
    s_n = 0 if parts is None else parts.shape[0]
    total = w.shape[0]
    br = _row_block(rows, OPT_ROW_BLOCK)
    assert row0 % br == 0 and (bufs is not None or (row0 == 0 and total == rows))
    b0 = row0 // br
    m_corr = 1.0 - ADAM_B1 ** ADAM_STEP
    v_corr = 1.0 - ADAM_B2 ** ADAM_STEP

    def body(idx_ref, own_ref, *refs):
        if s_n:
            p_ref, refs = refs[0], refs[1:]
        w_ref, m_ref, v_ref = refs[:3]
        g_ref, d_ref, nm_ref, nv_ref = refs[-4:]
        g = own_ref[...].astype(F32)
        for s in range(s_n):
            g = g + p_ref[s].astype(F32)
        nm = ADAM_B1 * m_ref[...] + (1.0 - ADAM_B1) * g
        nv = ADAM_B2 * v_ref[...] + (1.0 - ADAM_B2) * (g * g)
        g_ref[...] = g
        nm_ref[...] = nm
        nv_ref[...] = nv
        d_ref[...] = -ADAM_LR * ((nm / m_corr) / (jnp.sqrt(nv / v_corr) + ADAM_EPS) + ADAM_WD * w_ref[...])

    tile = pl.BlockSpec((br, c), lambda i, idx: (i + b0, 0))
    in_specs = [pl.BlockSpec((None, br, c), lambda i, idx: (idx[0], i, 0))]
    operands = [own_idx, own]
    if s_n:
        in_specs.append(pl.BlockSpec((s_n, br, c), lambda i, idx: (0, i, 0)))
        operands.append(parts)
    in_specs += [tile, tile, tile]
    operands += [w, m, v]
    aliases = {}
    if bufs is not None:
        aliases = {len(operands) + j: j for j in range(4)}
        in_specs += [_ANY] * 4
        operands += list(bufs)
    out = _sds((total, c), F32)
    return pl.pallas_call(
        body, name="adamw",
        grid_spec=pltpu.PrefetchScalarGridSpec(
            num_scalar_prefetch=1, grid=(rows // br,), in_specs=in_specs,
            out_specs=[tile, tile, tile, tile]),
        out_shape=[out, out, out, out],
        input_output_aliases=aliases,
        compiler_params=_params(("parallel",)),
    )(*operands)


def _chip_sum(p, r1, core):
    half = N_DEV // 2
    c = p.shape[-1]
    rows = int(np.prod(p.shape[1:-1]))
    br = _row_block(rows, MM_BLOCK)

    def body(core_ref, p_ref, r_ref, o_ref):
        o_ref[...] = (p_ref[...].astype(F32) + r_ref[...].astype(F32)).astype(BF16)

    out = pl.pallas_call(
        body, name="chip_sum",
        grid_spec=pltpu.PrefetchScalarGridSpec(
            num_scalar_prefetch=1, grid=(half, rows // br),
            in_specs=[pl.BlockSpec((None, None, br, c), lambda q, i, cr: (q, cr[0], i, 0)),
                      pl.BlockSpec((None, br, c), lambda q, i, cr: (q, i, 0))],
            out_specs=pl.BlockSpec((None, br, c), lambda q, i, cr: (q, i, 0))),
        out_shape=_sds((half, rows, c), BF16),
        compiler_params=_params(("parallel", "parallel")),
    )(core, p.reshape(half, 2, rows, c), r1.reshape(half, rows, c))
    return out.reshape((half,) + p.shape[1:])


def _position():
    return tuple(lax.axis_index(a) for a in MESH_AXES)


def _linear(px, py, pc):
    return 4 * px + 2 * py + pc


def _all_gather_small(v):
    rows, lanes = v.shape

    def body(x_ref, out_ref, send_sems, recv_sems, local_sem):
        x, y, c = _position()
        me, sibling = (x, y, c), (x, y, 1 - c)
        chips = [(1 - x, y), (x, 1 - y), (1 - x, 1 - y)]

        def copy(k, block, to, src=None):
            dst = out_ref.at[_linear(*block)]
            return pltpu.make_async_remote_copy(
                src_ref=dst if src is None else src, dst_ref=dst,
                send_sem=send_sems.at[k], recv_sem=recv_sems.at[k],
                device_id=to, device_id_type=MESH_ID)

        mine = pltpu.make_async_copy(x_ref, out_ref.at[_linear(*me)], local_sem)
        mine.start()
        first = [copy(0, me, sibling, src=x_ref)]
        first += [copy(1 + j, me, (*chip, c), src=x_ref) for j, chip in enumerate(chips)]
        for cp in first:
            cp.start()
        passed = [copy(4 + j, (*chip, c), sibling) for j, chip in enumerate(chips)]
        for j, chip in enumerate(chips):
            copy(1 + j, (*chip, c), me).wait_recv()
            passed[j].start()
        copy(0, sibling, me).wait_recv()
        for j, chip in enumerate(chips):
            copy(4 + j, (*chip, 1 - c), me).wait_recv()
        for cp in first + passed:
            cp.wait_send()
        mine.wait()

    return pl.pallas_call(
        body, name="all_gather_small",
        out_shape=_sds((N_DEV, rows, lanes), v.dtype),
        in_specs=[_VMEM], out_specs=_VMEM,
        scratch_shapes=[pltpu.SemaphoreType.DMA((7,)), pltpu.SemaphoreType.DMA((7,)),
                        pltpu.SemaphoreType.DMA],
        compiler_params=pltpu.CompilerParams(vmem_limit_bytes=VMEM_LIMIT),
    )(v)


_HBM = pl.BlockSpec(memory_space=pltpu.HBM)
_SEM = pl.BlockSpec(memory_space=pltpu.SEMAPHORE)
_EFFECT = pltpu.SideEffectType.DATAFLOW_SIDE_EFFECTING
_ALL_CHIPS = [(0, 0), (0, 1), (1, 0), (1, 1)]


def _other_chips(x, y):
    return [(1 - x, y), (x, 1 - y), (1 - x, 1 - y)]


def _in_hbm(a):
    return pltpu.with_memory_space_constraint(a, pltpu.HBM)


def _token():
    return _sds((SUBLANES, LANES), F32)


def _gather_ici_copy(ref, i, k, chip, c, block, send_sems, recv_sems):
    return pltpu.make_async_remote_copy(
        src_ref=ref.at[block], dst_ref=ref.at[block],
        send_sem=send_sems.at[3 * i + k], recv_sem=recv_sems.at[3 * i + k],
        device_id=(*chip, c), device_id_type=MESH_ID)


def _gather_ici_start(name, lands):
    n = len(lands)

    def body(*refs):
        ins, send_sems, recv_sems, token = refs[:n], refs[n], refs[n + 1], refs[-1]
        x, y, c = _position()
        me = _linear(x, y, c)
        for i in range(n):
            for k, chip in enumerate(_other_chips(x, y)):
                _gather_ici_copy(ins[i], i, k, chip, c, me, send_sems, recv_sems).start()
        token[...] = jnp.zeros_like(token)

    out = pl.pallas_call(
        body, name=name,
        out_shape=(pltpu.SemaphoreType.DMA((3 * n,)), pltpu.SemaphoreType.DMA((3 * n,)),
                   *[pltpu.HBM(a.shape, a.dtype) for a in lands], _token()),
        in_specs=[_HBM] * n, out_specs=(_SEM, _SEM, *[_HBM] * n, _VMEM),
        input_output_aliases={i: 2 + i for i in range(n)},
        compiler_params=pltpu.CompilerParams(has_side_effects=_EFFECT),
    )(*[_in_hbm(a) for a in lands])
    return out[0], out[1], list(out[2:2 + n]), out[-1]


def _gather_ici_wait(name, lands, send_sems, recv_sems, after):
    n = len(lands)

    def body(*refs):
        ins, ss, rs = refs[:n], refs[n], refs[n + 1]
        x, y, c = _position()
        me = _linear(x, y, c)
        for i in range(n):
            for k, chip in enumerate(_other_chips(x, y)):
                _gather_ici_copy(ins[i], i, k, chip, c, me, ss, rs).wait_send()
                _gather_ici_copy(ins[i], i, k, chip, c, _linear(*chip, c), ss, rs).wait_recv()

    out = pl.pallas_call(
        body, name=name,
        out_shape=[pltpu.HBM(a.shape, a.dtype) for a in lands],
        in_specs=[_HBM] * n + [_SEM, _SEM, _ANY], out_specs=[_HBM] * n,
        input_output_aliases={i: i for i in range(n)},
        compiler_params=pltpu.CompilerParams(has_side_effects=_EFFECT),
    )(*lands, send_sems, recv_sems, after)
    return list(out)


def _gather_d2d(lands):
    n = len(lands)

    def body(*refs):
        ins, outs, send_sems, recv_sems = refs[:n], refs[n:2 * n], refs[2 * n], refs[2 * n + 1]
        x, y, c = _position()

        def copy(i, q, core):
            block = _linear(*_ALL_CHIPS[q], core)
            return pltpu.make_async_remote_copy(
                src_ref=ins[i].at[block], dst_ref=outs[i].at[block],
                send_sem=send_sems.at[i, q], recv_sem=recv_sems.at[i, q],
                device_id=(x, y, 1 - c), device_id_type=MESH_ID)

        sent = [copy(i, q, c) for i in range(n) for q in range(len(_ALL_CHIPS))]
        for cp in sent:
            cp.start()
        for i in range(n):
            for q in range(len(_ALL_CHIPS)):
                copy(i, q, 1 - c).wait_recv()
        for cp in sent:
            cp.wait_send()

    return pl.pallas_call(
        body, name="gather_d2d",
        out_shape=[_sds(a.shape, a.dtype) for a in lands],
        in_specs=[_ANY] * n, out_specs=[_ANY] * n,
        input_output_aliases={i: i for i in range(n)},
        scratch_shapes=[pltpu.SemaphoreType.DMA((n, 4)), pltpu.SemaphoreType.DMA((n, 4))],
    )(*lands)


def _partials_d2d(parts):
    n = len(parts)
    half = N_DEV // 2

    def body(*refs):
        ins, outs, send_sems, recv_sems = refs[:n], refs[n:2 * n], refs[2 * n], refs[2 * n + 1]
        x, y, c = _position()

        def copy(i, q):
            return pltpu.make_async_remote_copy(
                src_ref=ins[i].at[_linear(*_ALL_CHIPS[q], 1 - c)], dst_ref=outs[i].at[q],
                send_sem=send_sems.at[i, q], recv_sem=recv_sems.at[i, q],
                device_id=(x, y, 1 - c), device_id_type=MESH_ID)

        sent = [copy(i, q) for i in range(n) for q in range(half)]
        for cp in sent:
            cp.start()
        for cp in sent:
            cp.wait_recv()
        for cp in sent:
            cp.wait_send()

    return pl.pallas_call(
        body, name="partials_d2d",
        out_shape=[_sds((half,) + p.shape[1:], p.dtype) for p in parts],
        in_specs=[_ANY] * n, out_specs=[_ANY] * n,
        scratch_shapes=[pltpu.SemaphoreType.DMA((n, half)), pltpu.SemaphoreType.DMA((n, half))],
    )(*parts)


def _partials_ici_copy(src, land, i, k, chip, c, send_sems, recv_sems):
    return pltpu.make_async_remote_copy(
        src_ref=src.at[2 * chip[0] + chip[1]], dst_ref=land.at[k],
        send_sem=send_sems.at[3 * i + k], recv_sem=recv_sems.at[3 * i + k],
        device_id=(*chip, c), device_id_type=MESH_ID)


def _partials_ici_start(name, sums, lands):
    n = len(sums)

    def body(*refs):
        srcs, dsts, send_sems, recv_sems, token = refs[:n], refs[n:2 * n], refs[2 * n], refs[2 * n + 1], refs[-1]
        x, y, c = _position()
        for i in range(n):
            for k, chip in enumerate(_other_chips(x, y)):
                _partials_ici_copy(srcs[i], dsts[i], i, k, chip, c, send_sems, recv_sems).start()
        token[...] = jnp.zeros_like(token)

    both = list(sums) + list(lands)
    out = pl.pallas_call(
        body, name=name,
        out_shape=(pltpu.SemaphoreType.DMA((3 * n,)), pltpu.SemaphoreType.DMA((3 * n,)),
                   *[pltpu.HBM(a.shape, a.dtype) for a in both], _token()),
        in_specs=[_HBM] * (2 * n), out_specs=(_SEM, _SEM, *[_HBM] * (2 * n), _VMEM),
        input_output_aliases={i: 2 + i for i in range(2 * n)},
        compiler_params=pltpu.CompilerParams(has_side_effects=_EFFECT),
    )(*[_in_hbm(a) for a in both])
    return out[0], out[1], list(out[2:2 + n]), list(out[2 + n:2 + 2 * n]), out[-1]


def _partials_ici_wait(name, sums, lands, send_sems, recv_sems, after):
    n = len(sums)

    def body(*refs):
        srcs, dsts, ss, rs = refs[:n], refs[n:2 * n], refs[2 * n], refs[2 * n + 1]
        x, y, c = _position()
        for i in range(n):
            for k, chip in enumerate(_other_chips(x, y)):
                cp = _partials_ici_copy(srcs[i], dsts[i], i, k, chip, c, ss, rs)
                cp.wait_send()
                cp.wait_recv()

    both = list(sums) + list(lands)
    out = pl.pallas_call(
        body, name=name,
        out_shape=[pltpu.HBM(a.shape, a.dtype) for a in both],
        in_specs=[_HBM] * (2 * n) + [_SEM, _SEM, _ANY], out_specs=[_HBM] * (2 * n),
        input_output_aliases={i: i for i in range(2 * n)},
        compiler_params=pltpu.CompilerParams(has_side_effects=_EFFECT),
    )(*both, send_sems, recv_sems, after)
    return list(out[:n]), list(out[n:])


def _pack(arrs):
    flat = jnp.concatenate([a.reshape(-1).astype(F32) for a in arrs])
    block = OPT_ROW_BLOCK if flat.shape[0] > OPT_ROW_BLOCK * LANES else SUBLANES
    pad = (-flat.shape[0]) % (block * LANES)
    if pad:
        flat = jnp.concatenate([flat, jnp.zeros((pad,), F32)])
    return flat.reshape(-1, LANES)


def _unpack(packed, shapes, lead=()):
    flat = packed.reshape(lead + (-1,))
    out, off = [], 0
    for s in shapes:
        size = int(np.prod(s))
        out.append(flat[..., off:off + size].reshape(lead + tuple(s)))
        off += size
    return out


def _unshard_last(g):
    nd = g.ndim
    perm = tuple(range(1, nd - 1)) + (0, nd - 1)
    t = jnp.transpose(g, perm)
    return t.reshape(t.shape[:-2] + (N_DEV * g.shape[-1],))


def kernel(x, c, w_ada, b_ada, ln_g, ln_b, ffn_gu, ffn_down, gmlp_w_in, gmlp_b_in, gmlp_ln_g, gmlp_ln_b, gmlp_w_s, gmlp_b_s, gmlp_w_out, w_ada_kv, b_ada_kv, w_kv, attn_w_q, attn_rel_bias, attn_w_o, loss_target, m_w_ada, m_b_ada, m_ln_g, m_ln_b, m_ffn_gu, m_ffn_down, m_gmlp_w_in, m_gmlp_b_in, m_gmlp_ln_g, m_gmlp_ln_b, m_gmlp_w_s, m_gmlp_b_s, m_gmlp_w_out, m_w_ada_kv, m_b_ada_kv, m_w_kv, m_attn_w_q, m_attn_rel_bias, m_attn_w_o, v_w_ada, v_b_ada, v_ln_g, v_ln_b, v_ffn_gu, v_ffn_down, v_gmlp_w_in, v_gmlp_b_in, v_gmlp_ln_g, v_gmlp_ln_b, v_gmlp_w_s, v_gmlp_b_s, v_gmlp_w_out, v_w_ada_kv, v_b_ada_kv, v_w_kv, v_attn_w_q, v_attn_rel_bias, v_attn_w_o):
    weights = dict(w_ada=w_ada, b_ada=b_ada, ln_g=ln_g, ln_b=ln_b, ffn_gu=ffn_gu, ffn_down=ffn_down,
                   gmlp_w_in=gmlp_w_in, gmlp_b_in=gmlp_b_in, gmlp_ln_g=gmlp_ln_g, gmlp_ln_b=gmlp_ln_b,
                   gmlp_w_s=gmlp_w_s, gmlp_b_s=gmlp_b_s, gmlp_w_out=gmlp_w_out, w_ada_kv=w_ada_kv,
                   b_ada_kv=b_ada_kv, w_kv=w_kv, attn_w_q=attn_w_q, attn_rel_bias=attn_rel_bias,
                   attn_w_o=attn_w_o)
    mom1 = dict(w_ada=m_w_ada, b_ada=m_b_ada, ln_g=m_ln_g, ln_b=m_ln_b, ffn_gu=m_ffn_gu, ffn_down=m_ffn_down,
                gmlp_w_in=m_gmlp_w_in, gmlp_b_in=m_gmlp_b_in, gmlp_ln_g=m_gmlp_ln_g, gmlp_ln_b=m_gmlp_ln_b,
                gmlp_w_s=m_gmlp_w_s, gmlp_b_s=m_gmlp_b_s, gmlp_w_out=m_gmlp_w_out, w_ada_kv=m_w_ada_kv,
                b_ada_kv=m_b_ada_kv, w_kv=m_w_kv, attn_w_q=m_attn_w_q, attn_rel_bias=m_attn_rel_bias,
                attn_w_o=m_attn_w_o)
    mom2 = dict(w_ada=v_w_ada, b_ada=v_b_ada, ln_g=v_ln_g, ln_b=v_ln_b, ffn_gu=v_ffn_gu, ffn_down=v_ffn_down,
                gmlp_w_in=v_gmlp_w_in, gmlp_b_in=v_gmlp_b_in, gmlp_ln_g=v_gmlp_ln_g, gmlp_ln_b=v_gmlp_ln_b,
                gmlp_w_s=v_gmlp_w_s, gmlp_b_s=v_gmlp_b_s, gmlp_w_out=v_gmlp_w_out, w_ada_kv=v_w_ada_kv,
                b_ada_kv=v_b_ada_kv, w_kv=v_w_kv, attn_w_q=v_attn_w_q, attn_rel_bias=v_attn_rel_bias,
                attn_w_o=v_attn_w_o)
    order = list(weights)

    x = x[0]
    target = loss_target[0]
    t, d = x.shape
    n_mod = w_ada.shape[-1] * N_DEV // d
    mod_w = w_ada.shape[-1]
    kv_w = w_ada_kv.shape[-1]
    n_b = DEPTH - N_A
    me = _linear(*_position())

    l2 = DEPTH * 2
    big = dict(
        ffn_gu=ffn_gu.reshape((l2,) + ffn_gu.shape[2:]),
        ffn_down=ffn_down.reshape((l2,) + ffn_down.shape[2:]),
        gmlp_w_in=gmlp_w_in, gmlp_w_out=gmlp_w_out, w_kv=w_kv[None],
        attn_w_q=attn_w_q, attn_w_o=attn_w_o)
    big_names = list(big)
    core = lax.axis_index("c").astype(jnp.int32).reshape(1)
    chip = (2 * lax.axis_index("x") + lax.axis_index("y")).astype(jnp.int32).reshape(1)

    fwd_groups = [
        {"ffn_gu": (0, 1), "ffn_down": (0, 1)},
        {"gmlp_w_in": (0, 1), "gmlp_w_out": (0, 1), "ffn_gu": (1, 1), "ffn_down": (1, 1)},
        {"ffn_gu": (2, 2), "ffn_down": (2, 2), "gmlp_w_in": (1, 1), "gmlp_w_out": (1, 1), "w_kv": (0, 1)},
        {"ffn_gu": (4, 2), "ffn_down": (4, 2), "attn_w_q": (0, 1), "attn_w_o": (0, 1)},
        {"ffn_gu": (6, 2), "ffn_down": (6, 2), "attn_w_q": (1, 1), "attn_w_o": (1, 1)},
    ]
    bwd_groups = []
    for l in range(DEPTH):
        g = {"ffn_gu": (2 * l, 2), "ffn_down": (2 * l, 2)}
        if l < N_A:
            g.update({"gmlp_w_in": (l, 1), "gmlp_w_out": (l, 1)})
        else:
            g.update({"attn_w_q": (l - N_A, 1), "attn_w_o": (l - N_A, 1)})
        if l == N_A - 1:
            g["w_kv"] = (0, 1)
        bwd_groups.append(g)

    def slot_of(groups, name, slot):
        for gi, g in enumerate(groups):
            if name in g and g[name][0] <= slot < g[name][0] + g[name][1]:
                return gi, slot - g[name][0]
        raise KeyError((name, slot))

    flights = []
    for gi, g in enumerate(fwd_groups):
        lands = []
        for name, (s0, cnt) in g.items():
            shard = big[name][s0:s0 + cnt].astype(BF16)
            land = lax.empty((N_DEV,) + shard.shape, BF16)
            lands.append(lax.dynamic_update_slice(land, shard[None], (me,) + (0,) * shard.ndim))
        flights.append(_gather_ici_start(f"gather_ici_start_{gi}", lands))
    start_token = sum(f[3][0, 0] for f in flights)
    gathered = [None] * len(fwd_groups)

    def land_group(gi, after):
        send_sems, recv_sems, lands, _ = flights[gi]
        lands = _gather_ici_wait(f"gather_ici_wait_{gi}", lands, send_sems, recv_sems, after)
        gathered[gi] = dict(zip(fwd_groups[gi], _gather_d2d(lands)))

    def weight(name, slot):
        gi, local = slot_of(fwd_groups, name, slot)
        return gathered[gi][name], local

    partial = [{name: lax.empty((N_DEV, cnt) + big[name].shape[1:], BF16) for name, (_, cnt) in g.items()}
               for g in bwd_groups]

    c_all = _all_gather_small(_pack([c]))
    c_all = _unpack(c_all, [(d,)], lead=(N_DEV,))[0]
    c4 = _as4(c_all)
    mod_part = _matmul("ada_fwd", c4, w_ada[:, None], (DEPTH, 1, N_DEV, mod_w), F32, a_silu=True)
    kv_part = _matmul("ada_kv_fwd", c4, _as4(w_ada_kv), (1, 1, N_DEV, kv_w), F32, a_silu=True)
    small_shapes = [mod_part.shape, kv_part.shape, ln_g.shape, ln_b.shape, gmlp_b_in.shape,
                    gmlp_ln_g.shape, gmlp_ln_b.shape, attn_rel_bias.shape]
    small = _all_gather_small(_pack([mod_part, kv_part, ln_g, ln_b, gmlp_b_in, gmlp_ln_g, gmlp_ln_b,
                                     attn_rel_bias]))
    (mod_g, kvm_g, ln_g_g, ln_b_g, b_in_g, gln_g_g, gln_b_g, rel_g) = _unpack(small, small_shapes, lead=(N_DEV,))
    mod_mine = lax.dynamic_index_in_dim(mod_g[:, :, 0], me, axis=2, keepdims=False)
    mod = _unshard_last(mod_mine) + b_ada
    mod = mod.reshape(DEPTH, n_mod, 1, d)
    kvm_mine = lax.dynamic_index_in_dim(kvm_g[:, 0, 0], me, axis=1, keepdims=False)
    mkv = (_unshard_last(kvm_mine) + b_ada_kv).reshape(2, 1, d)
    ln_g_f = _unshard_last(ln_g_g)
    ln_b_f = _unshard_last(ln_b_g)
    half = N_DEV // 2
    b_in_f = jnp.transpose(b_in_g, (1, 0, 2))[:, :, None, :]
    gln_g_f = _unshard_last(gln_g_g).reshape(N_A, half, 1, -1)
    gln_b_f = _unshard_last(gln_b_g).reshape(N_A, half, 1, -1)
    rel_f = _unshard_last(rel_g)

    def shard_act(a):
        return a.reshape(a.shape[0], a.shape[2], a.shape[3])

    def grad_into(name, slot, mm):
        gi, local = slot_of(bwd_groups, name, slot)
        partial[gi][name] = mm(partial[gi][name], local)

    def ffn_fwd(h, lw):
        w_gu, l_gu = weight("ffn_gu", lw)
        w_dn, l_dn = weight("ffn_down", lw)
        n = w_gu.shape[-1]
        gu = _matmul("ffn_gu_fwd", _as4(h), w_gu, (N_DEV, 1, t, n), F32, lb=l_gu)
        a = _swiglu_fwd(shard_act(gu))
        y = _matmul("ffn_down_fwd", a[:, None], w_dn, (1, 1, t, d), F32, lb=l_dn, b_merge=2, reduce=True)
        return y[0, 0], (gu, a)

    def ffn_bwd(dy, h, saved, lw):
        gu, a = saved
        w_gu, l_gu = weight("ffn_gu", lw)
        w_dn, l_dn = weight("ffn_down", lw)
        n = w_gu.shape[-1]
        da = _matmul("ffn_down_bwd_a", _as4(dy), w_dn, (half, 1, t, n), F32, lb=l_dn, b_merge=2, tb=True)
        grad_into("ffn_down", lw, lambda buf, lo: _matmul(
            "ffn_down_bwd_w", a[:, None], _as4(dy), buf.shape, BF16, ta=True, lo=lo, out_merge=2, out_buf=buf))
        dgu = _swiglu_bwd(shard_act(gu), shard_act(da))
        dh = _matmul("ffn_gu_bwd_a", dgu[:, None], w_gu, (1, 1, t, d), F32, lb=l_gu, tb=True, reduce=True)
        grad_into("ffn_gu", lw, lambda buf, lo: _matmul(
            "ffn_gu_bwd_w", _as4(h), dgu[:, None], buf.shape, BF16, ta=True, lo=lo, out_buf=buf))
        return dh[0, 0], {}

    def gmlp_params(l):
        return (b_in_f[l], gln_g_f[l], gln_b_f[l], gmlp_w_s[l], gmlp_b_s[l][:, :, None])

    def gmlp_fwd(h, l):
        w_in, l_in = weight("gmlp_w_in", l)
        w_out, l_out = weight("gmlp_w_out", l)
        n = w_in.shape[-1]
        zpre = _matmul("gmlp_in_fwd", _as4(h), w_in, (N_DEV, 1, t, n), F32, lb=l_in)
        gated = _gmlp_mid_fwd(shard_act(zpre), *gmlp_params(l))
        y = _matmul("gmlp_out_fwd", gated[:, None], w_out, (1, 1, t, d), F32, lb=l_out, b_merge=2, reduce=True)
        return y[0, 0], (zpre, gated)

    def gmlp_bwd(dy, h, saved, l):
        zpre, gated = saved
        w_in, l_in = weight("gmlp_w_in", l)
        w_out, l_out = weight("gmlp_w_out", l)
        n = w_in.shape[-1]
        dgated = _matmul("gmlp_out_bwd_a", _as4(dy), w_out, (half, 1, t, n), F32, lb=l_out, b_merge=2, tb=True)
        grad_into("gmlp_w_out", l, lambda buf, lo: _matmul(
            "gmlp_out_bwd_w", gated[:, None], _as4(dy), buf.shape, BF16, ta=True, lo=lo, out_merge=2, out_buf=buf))
        dz, dws, dbs, dlng, dlnb, dbin = _gmlp_mid_bwd(shard_act(zpre), shard_act(dgated), *gmlp_params(l))
        dh = _matmul("gmlp_in_bwd_a", dz[:, None], w_in, (1, 1, t, d), F32, lb=l_in, tb=True, reduce=True)
        grad_into("gmlp_w_in", l, lambda buf, lo: _matmul(
            "gmlp_in_bwd_w", _as4(h), dz[:, None], buf.shape, BF16, ta=True, lo=lo, out_buf=buf))
        small_grads = dict(gmlp_w_s=dws, gmlp_b_s=dbs[:, :, 0], gmlp_ln_g=dlng.reshape(-1),
                           gmlp_ln_b=dlnb.reshape(-1), gmlp_b_in=dbin.reshape(-1))
        return dh[0, 0], small_grads

    def attn_fwd(h, j, kvp):
        rel_vec = _rel_vector(rel_f[j])
        w_q, l_q = weight("attn_w_q", j)
        w_o, l_o = weight("attn_w_o", j)
        q = _matmul("attn_q_fwd", _as4(h), w_q, (1, 1, t, d), BF16, lb=l_q, b_merge=N_DEV, reduce=True)[0, 0]
        o = _attn_fwd(q, kvp, rel_vec)
        y = _matmul("attn_o_fwd", _as4(o), w_o, (1, 1, t, d), F32, lb=l_o, b_merge=N_DEV, reduce=True)
        return y[0, 0], (q, o, rel_vec)

    def attn_bwd(dy, h, saved, j, kvp, dkv_acc):
        q, o, rel_vec = saved
        w_q, l_q = weight("attn_w_q", j)
        w_o, l_o = weight("attn_w_o", j)
        do = _matmul("attn_o_bwd_a", _as4(dy), w_o, (1, 1, t, d), BF16, lb=l_o, b_merge=N_DEV, tb=True)[0, 0]
        grad_into("attn_w_o", j, lambda buf, lo: _matmul(
            "attn_o_bwd_w", _as4(o), _as4(dy), buf.shape, BF16, ta=True, lo=lo, out_merge=N_DEV, out_buf=buf))
        dq, dk, dv, dsc = _attn_bwd(q, do, kvp, rel_vec, *dkv_acc)
        drel = _rel_bias_grad(_skew_diagonals(dsc))
        dh = _matmul("attn_q_bwd_a", _as4(dq), w_q, (1, 1, t, d), F32, lb=l_q, b_merge=N_DEV, tb=True)
        grad_into("attn_w_q", j, lambda buf, lo: _matmul(
            "attn_q_bwd_w", _as4(h), _as4(dq), buf.shape, BF16, ta=True, lo=lo, out_merge=N_DEV, out_buf=buf))
        return dh[0, 0], dict(attn_rel_bias=drel, dkv=(dk, dv))

    tape = []
    kvp = None
    kv_tape = None
    first_use = {(0, 0): 0, (0, 1): 1, (1, 0): 2, (2, 0): 3, (3, 0): 4}
    mod = mod.at[0, 0].add(start_token)
    for l in range(DEPTH):
        for i in range(3):
            if (l, i) in first_use:
                land_group(first_use[l, i], x)
            shift, scl, gate = mod[l, 3 * i], mod[l, 3 * i + 1], mod[l, 3 * i + 2]
            wgt = 1.0 if i == 1 else 0.5
            gw = wgt * (1.0 + gate)
            h = _modulate(x, scl, shift)
            if i != 1:
                y, saved = ffn_fwd(h, 2 * l + i // 2)
            elif l < N_A:
                y, saved = gmlp_fwd(h, l)
            else:
                y, saved = attn_fwd(h, l - N_A, kvp)
            x_new = _ln_res_fwd(x, y, gw, ln_g_f[l, i][None], ln_b_f[l, i][None])
            tape.append((x, h, y, gw, scl, saved))
            x = x_new
        if l == N_A - 1:
            hkv = _modulate(x, mkv[1], mkv[0])
            w_kvg, l_kv = weight("w_kv", 0)
            n = w_kvg.shape[-1]
            kv = _matmul("kv_fwd", _as4(hkv), w_kvg, (N_DEV, 1, t, n), BF16, lb=l_kv)
            kvp = jnp.pad(shard_act(kv), ((0, 0), (LEFT_PAD, 0), (0, 0)))
            kv_tape = (x, hkv)

    loss_part, dx = _loss_head(x, target)
    loss = lax.psum(loss_part[0, 0], MESH_AXES)

    d_mod = [[None] * n_mod for _ in range(DEPTH)]
    d_ln_g = [[None] * 3 for _ in range(DEPTH)]
    d_ln_b = [[None] * 3 for _ in range(DEPTH)]
    small_grads = {k: [None] * N_A for k in ("gmlp_w_s", "gmlp_b_s", "gmlp_ln_g", "gmlp_ln_b", "gmlp_b_in")}
    d_rel = [None] * n_b
    dkv_acc = ()
    d_mkv = None
    reductions = [None] * DEPTH
    sent_token = None
    for l in reversed(range(DEPTH)):
        if l == N_A - 1:
            x_kv, hkv = kv_tape
            w_kvg, l_kv = weight("w_kv", 0)
            dkv = jnp.concatenate(dkv_acc)[:, LEFT_PAD:, :].astype(BF16)[:, None]
            dhkv = _matmul("kv_bwd_a", dkv, w_kvg, (1, 1, t, d), F32, lb=l_kv, tb=True, reduce=True)[0, 0]
            grad_into("w_kv", 0, lambda buf, lo: _matmul(
                "kv_bwd_w", _as4(hkv), dkv, buf.shape, BF16, ta=True, lo=lo, out_buf=buf))
            dx, ds_kv, db_kv = _mod_bwd(dx, dhkv, x_kv, mkv[1])
            d_mkv = jnp.concatenate([db_kv[0], ds_kv[0]])
        for i in reversed(range(3)):
            x_in, h, y, gw, scl, saved = tape[3 * l + i]
            wgt = 1.0 if i == 1 else 0.5
            if sent_token is not None:
                gw = gw + sent_token
                sent_token = None
            dx_res, dy, dgw, dg, db = _ln_res_bwd(x_in, y, gw, ln_g_f[l, i][None], dx)
            d_ln_g[l][i], d_ln_b[l][i] = dg[0], db[0]
            if i != 1:
                dh, extra = ffn_bwd(dy, h, saved, 2 * l + i // 2)
            elif l < N_A:
                dh, extra = gmlp_bwd(dy, h, saved, l)
                for k, g in extra.items():
                    small_grads[k][l] = g
            else:
                dh, extra = attn_bwd(dy, h, saved, l - N_A, kvp, dkv_acc)
                d_rel[l - N_A] = extra["attn_rel_bias"]
                dkv_acc = extra["dkv"]
            dx, dscl, dshift = _mod_bwd(dx_res, dh, x_in, scl)
            d_mod[l][3 * i], d_mod[l][3 * i + 1], d_mod[l][3 * i + 2] = dshift[0], dscl[0], wgt * dgw[0]
        names = list(bwd_groups[l])
        from_sibling = _partials_d2d([partial[l][k] for k in names])
        sums = [_chip_sum(partial[l][k], r1, core) for k, r1 in zip(names, from_sibling)]
        lands = [lax.empty((len(_ALL_CHIPS) - 1,) + s.shape[1:], BF16) for s in sums]
        reductions[l] = _partials_ici_start(f"partials_ici_start_{l}", sums, lands)
        sent_token = reductions[l][4][0, 0]
    grad_x = dx[None]

    d_mod_arr = jnp.stack([jnp.concatenate(r) for r in d_mod])
    small_part = dict(
        b_ada=d_mod_arr + sent_token, b_ada_kv=d_mkv,
        ln_g=jnp.stack([jnp.stack(r) for r in d_ln_g]), ln_b=jnp.stack([jnp.stack(r) for r in d_ln_b]),
        gmlp_b_in=jnp.stack(small_grads["gmlp_b_in"]), gmlp_ln_g=jnp.stack(small_grads["gmlp_ln_g"]),
        gmlp_ln_b=jnp.stack(small_grads["gmlp_ln_b"]), gmlp_w_s=jnp.stack(small_grads["gmlp_w_s"]),
        gmlp_b_s=jnp.stack(small_grads["gmlp_b_s"]), attn_rel_bias=jnp.stack(d_rel))
    small_names = list(small_part)
    sp_shapes = [small_part[k].shape for k in small_names]
    sp_all = _all_gather_small(_pack([small_part[k] for k in small_names]))
    sp_sum = _sum_parts(sp_all)
    full_grads = dict(zip(small_names, _unpack(sp_sum, sp_shapes)))
    per_dev = dict(zip(small_names, _unpack(sp_all, sp_shapes, lead=(N_DEV,))))

    def my_cols(a, width):
        return lax.dynamic_slice_in_dim(a, me * width, width, axis=a.ndim - 1)

    grads = {}
    grads["b_ada"] = full_grads["b_ada"]
    grads["b_ada_kv"] = full_grads["b_ada_kv"]
    grads["gmlp_w_s"] = full_grads["gmlp_w_s"]
    grads["gmlp_b_s"] = full_grads["gmlp_b_s"]
    for k in ("ln_g", "ln_b", "gmlp_b_in", "gmlp_ln_g", "gmlp_ln_b", "attn_rel_bias"):
        grads[k] = my_cols(full_grads[k], weights[k].shape[-1])

    dmod_cols = jnp.transpose(my_cols(per_dev["b_ada"], mod_w), (1, 0, 2))[:, None]
    grads["w_ada"] = _matmul("ada_bwd_w", c4, dmod_cols, (DEPTH, 1, d, mod_w), F32, ta=True,
                             a_silu=True)[:, 0]
    dkv_cols = my_cols(per_dev["b_ada_kv"], kv_w)[None, None]
    grads["w_ada_kv"] = _matmul("ada_kv_bwd_w", c4, dkv_cols, (1, 1, d, kv_w), F32, ta=True,
                                a_silu=True)[0, 0]

    delta, new_m, new_v = {}, {}, {}
    first = jnp.zeros((1,), jnp.int32)

    def flat2(a, cols):
        return a.reshape(-1, cols)

    for k in ("w_ada", "w_ada_kv"):
        w = weights[k]
        cols = w.shape[-1]
        res = _adamw(grads[k].reshape(1, -1, cols), first, None, flat2(w, cols), flat2(mom1[k], cols),
                     flat2(mom2[k], cols))
        grads[k], delta[k], new_m[k], new_v[k] = (a.reshape(w.shape) for a in res)

    tiny = [k for k in order if k not in delta and k not in big_names]
    tiny_shapes = [weights[k].shape for k in tiny]
    tiny_out = _adamw(_pack([grads[k] for k in tiny])[None], first, None, _pack([weights[k] for k in tiny]),
                      _pack([mom1[k] for k in tiny]), _pack([mom2[k] for k in tiny]))
    for dst, arr in zip((grads, delta, new_m, new_v), tiny_out):
        for k, val in zip(tiny, _unpack(arr, tiny_shapes)):
            dst[k] = val

    bufs = {k: [lax.empty(flat2(weights[k], weights[k].shape[-1]).shape, F32) for _ in range(4)]
            for k in big_names}
    done = tiny_out[0]
    for l in reversed(range(DEPTH)):
        send_sems, recv_sems, sums, lands, _ = reductions[l]
        sums, lands = _partials_ici_wait(f"partials_ici_wait_{l}", sums, lands, send_sems, recv_sems, done)
        for k, own, got in zip(bwd_groups[l], sums, lands):
            cols = weights[k].shape[-1]
            slot_rows = int(np.prod(big[k].shape[1:-1]))
            bufs[k] = _adamw(own.reshape(own.shape[0], -1, cols), chip, got.reshape(got.shape[0], -1, cols),
                             flat2(weights[k], cols), flat2(mom1[k], cols), flat2(mom2[k], cols),
                             row0=bwd_groups[l][k][0] * slot_rows, bufs=bufs[k])
            done = bufs[k][0][:SUBLANES, :LANES]
    for k in big_names:
        grads[k], delta[k], new_m[k], new_v[k] = (b.reshape(weights[k].shape) for b in bufs[k])

    return (loss, grad_x, *[grads[k] for k in order], *[delta[k] for k in order],
            *[new_m[k] for k in order], *[new_v[k] for k in order])
```

```python
import functools

import numpy as np
import jax
import jax.numpy as jnp
from jax import lax
from jax.experimental import pallas as pl
from jax.experimental.pallas import tpu as pltpu

F32 = jnp.float32
BF16 = jnp.bfloat16
MESH_AXES = ("x", "y", "c")
N_DEV = 8
MESH_ID = pl.DeviceIdType.MESH

DEPTH = 4
N_A = 2
CHUNK = 64
N_HEADS = 16
LEFT_CHUNKS = 8
BAND = (LEFT_CHUNKS + 1) * CHUNK
LEFT_PAD = LEFT_CHUNKS * CHUNK
MAX_REL = 4 * CHUNK
N_REL = (CHUNK - 1) + MAX_REL + 1
GMLP_WINDOW = 128
GMLP_GROUPS = 8
ALPHA = (2.0 * DEPTH) ** 0.25
LN_EPS = 1e-5
ADAM_LR = 0.001
ADAM_B1 = 0.9
ADAM_B2 = 0.999
ADAM_EPS = 1e-08
ADAM_WD = 0.01
ADAM_STEP = 10

V7X_VMEM_BYTES = 64 * 1024 * 1024
VMEM_LIMIT = V7X_VMEM_BYTES - 8 * 1024 * 1024
LANES = 128
SUBLANES = 8
MM_BLOCK = 2048
BIG_ROW_BLOCK = 1024
ROW_BLOCK = 512
OPT_ROW_BLOCK = 256

_ANY = pl.BlockSpec(memory_space=pl.ANY)
_VMEM = pl.BlockSpec(memory_space=pltpu.VMEM)


def _params(sem=None):
    return pltpu.CompilerParams(dimension_semantics=sem, vmem_limit_bytes=VMEM_LIMIT)


def _row_block(rows, target):
    for d in range(min(rows, target), 0, -1):
        if rows % d == 0 and (d % SUBLANES == 0 or d == rows):
            return d
    return rows


def _matmul(name, a, b, out_shape4, out_dtype, *, la=0, lb=0, lo=0, ta=False, tb=False,
            reduce=False, b_merge=1, out_merge=1, out_buf=None, a_silu=False):
    ja_n, _, a_r, a_c = a.shape
    jb_n, _, b_r, b_c = b.shape
    jo_n, _, o_r, o_c = out_shape4
    m_tot = a_c if ta else a_r
    k_a = a_r if ta else a_c
    b_rows = b_merge * b_r
    k_c = b_c if tb else b_rows
    n = b_rows if tb else b_c
    n_chunks = (jb_n // b_merge) if reduce else 1
    natural_k = reduce and ja_n == 1
    assert n == o_c, (name, n, o_c)
    assert k_a ==(k_c * n_chunks if natural_k else k_c), (name, k_a, k_c, n_chunks)
    bk = k_c if (k_c <= MM_BLOCK or (b_merge > 1 and not tb)) else MM_BLOCK
    assert k_c % bk == 0
    nkk = k_c // bk
    nk = n_chunks * nkk
    m_out = out_merge * o_r
    assert m_tot == m_out, (name, m_tot, m_out)
    bm = m_tot if (m_tot <= MM_BLOCK or out_merge > 1) else MM_BLOCK
    assert m_tot % bm == 0
    jo_blocks = jo_n // out_merge

    def a_index(j, m, k):
        kj, kk = k // nkk, k % nkk
        ja = 0 if ja_n == 1 else (kj if reduce else j)
        ke = kk + kj * nkk if natural_k else kk
        return (ja, la, ke, m) if ta else (ja, la, m, ke)

    def b_index(j, m, k):
        kj, kk = k // nkk, k % nkk
        jb = 0 if jb_n == b_merge else (kj if reduce else j)
        return (jb, lb, 0, kk) if tb else (jb, lb, kk, 0)

    def o_index(j, m, k):
        return (j, lo, 0, 0) if out_merge > 1 else (j, lo, m, 0)

    a_block = (None, None, bk, bm) if ta else (None, None, bm, bk)
    if b_merge > 1:
        b_block = (b_merge, None, b_r, bk if tb else n)
    else:
        b_block = (None, None, n, bk) if tb else (None, None, bk, n)
    o_block = (out_merge, None, o_r, n) if out_merge > 1 else (None, None, bm, n)
    dims = (((0 if ta else 1,), (1 if tb else 0,)), ((), ()))

    in_place = nk > 1 and out_dtype == F32 and out_merge == 1
    use_acc = nk > 1 and not in_place

    def body(a_ref, b_ref, *rest):
        o_ref = rest[-2] if use_acc else rest[-1]
        k = pl.program_id(2)
        av = a_ref[...]
        if a_silu:
            af = av.astype(F32)
            av = af * jax.nn.sigmoid(af)
        bv = b_ref[...]
        if b_merge > 1:
            bv = bv.reshape(b_rows, bv.shape[-1])
        prod = lax.dot_general(av.astype(BF16), bv.astype(BF16), dims, preferred_element_type=F32)

        def emit(val):
            val = val.astype(out_dtype)
            o_ref[...] = val.reshape(out_merge, o_r, n) if out_merge > 1 else val

        if nk == 1:
            emit(prod)
            return
        acc_ref = o_ref if in_place else rest[-1]

        @pl.when(k == 0)
        def _():
            acc_ref[...] = prod

        @pl.when(k > 0)
        def _():
            acc_ref[...] += prod

        if use_acc:
            @pl.when(k == nk - 1)
            def _():
                emit(acc_ref[...])

    in_specs = [pl.BlockSpec(a_block, a_index), pl.BlockSpec(b_block, b_index)]
    operands = [a, b]
    aliases = {}
    if out_buf is not None:
        assert out_buf.shape == tuple(out_shape4) and out_buf.dtype == out_dtype
        in_specs.append(_ANY)
        operands.append(out_buf)
        aliases = {2: 0}
    return pl.pallas_call(
        body, name=name,
        grid=(jo_blocks, m_tot // bm, nk),
        in_specs=in_specs,
        out_specs=pl.BlockSpec(o_block, o_index),
        out_shape=jax.ShapeDtypeStruct(tuple(out_shape4), out_dtype),
        scratch_shapes=[pltpu.VMEM((bm, n), F32)] if use_acc else [],
        input_output_aliases=aliases,
        compiler_params=_params(("parallel", "parallel", "arbitrary")),
    )(*operands)


def _as4(a):
    return a.reshape((1,) * (4 - a.ndim) + a.shape)


def _row_call(name, body, ins, outs, t, *, acc_outs=()):
    bt = _row_block(t, ROW_BLOCK)

    def spec(arr, tiled):
        if tiled:
            return pl.BlockSpec((bt,) + tuple(arr.shape[1:]), lambda i: (i,) + (0,) * (arr.ndim - 1))
        return pl.BlockSpec(tuple(arr.shape), lambda i: (0,) * arr.ndim)

    return pl.pallas_call(
        body, name=name, grid=(t // bt,),
        in_specs=[spec(a, tl) for a, tl in ins],
        out_specs=[spec(o, tl) for o, tl in outs],
        out_shape=[jax.ShapeDtypeStruct(o.shape, o.dtype) for o, _ in outs],
        compiler_params=_params(("arbitrary",) if acc_outs else ("parallel",)),
    )(*[a for a, _ in ins])


def _sds(shape, dtype):
    return jax.ShapeDtypeStruct(tuple(shape), dtype)


def _modulate(x, scl, shift):
    t, d = x.shape

    def body(x_ref, s_ref, b_ref, h_ref):
        h_ref[...] = (x_ref[...] * (1.0 + s_ref[...]) + b_ref[...]).astype(BF16)

    return _row_call("modulate", body, [(x, True), (scl, False), (shift, False)],
                     [(_sds((t, d), BF16), True)], t)[0]


def _ln_stats(r):
    mu = jnp.mean(r, axis=-1, keepdims=True)
    rc = r - mu
    var = jnp.mean(rc * rc, axis=-1, keepdims=True)
    rstd = lax.rsqrt(var + LN_EPS)
    return rc * rstd, rstd


def _ln_res_fwd(x, y, gw, g, b):
    t, d = x.shape

    def body(x_ref, y_ref, gw_ref, g_ref, b_ref, o_ref):
        r = ALPHA * x_ref[...] + gw_ref[...] * y_ref[...]
        xhat, _ = _ln_stats(r)
        o_ref[...] = xhat * g_ref[...] + b_ref[...]

    return _row_call("ln_res_fwd", body,
                     [(x, True), (y, True), (gw, False), (g, False), (b, False)],
                     [(_sds((t, d), F32), True)], t)[0]


def _ln_res_bwd(x, y, gw, g, dxn):
    t, d = x.shape

    def body(x_ref, y_ref, gw_ref, g_ref, dxn_ref, dx_ref, dy_ref, dgw_ref, dg_ref, db_ref):
        @pl.when(pl.program_id(0) == 0)
        def _():
            dgw_ref[...] = jnp.zeros_like(dgw_ref)
            dg_ref[...] = jnp.zeros_like(dg_ref)
            db_ref[...] = jnp.zeros_like(db_ref)

        yv = y_ref[...]
        gwv = gw_ref[...]
        dxn = dxn_ref[...]
        xhat, rstd = _ln_stats(ALPHA * x_ref[...] + gwv * yv)
        dxh = dxn * g_ref[...]
        m1 = jnp.mean(dxh, axis=-1, keepdims=True)
        m2 = jnp.mean(dxh * xhat, axis=-1, keepdims=True)
        dr = rstd * (dxh - m1 - xhat * m2)
        dx_ref[...] = ALPHA * dr
        dy_ref[...] = (gwv * dr).astype(BF16)
        dgw_ref[...] += jnp.sum(dr * yv, axis=0, keepdims=True)
        dg_ref[...] += jnp.sum(dxn * xhat, axis=0, keepdims=True)
        db_ref[...] += jnp.sum(dxn, axis=0, keepdims=True)

    vec = _sds((1, d), F32)
    return _row_call("ln_res_bwd", body,
                     [(x, True), (y, True), (gw, False), (g, False), (dxn, True)],
                     [(_sds((t, d), F32), True), (_sds((t, d), BF16), True),
                      (vec, False), (vec, False), (vec, False)], t, acc_outs=(2, 3, 4))


def _mod_bwd(dx_res, dh, x, scl):
    t, d = x.shape

    def body(dxr_ref, dh_ref, x_ref, s_ref, dx_ref, ds_ref, db_ref):
        @pl.when(pl.program_id(0) == 0)
        def _():
            ds_ref[...] = jnp.zeros_like(ds_ref)
            db_ref[...] = jnp.zeros_like(db_ref)

        dh = dh_ref[...]
        dx_ref[...] = dxr_ref[...] + dh * (1.0 + s_ref[...])
        ds_ref[...] += jnp.sum(dh * x_ref[...], axis=0, keepdims=True)
        db_ref[...] += jnp.sum(dh, axis=0, keepdims=True)

    vec = _sds((1, d), F32)
    return _row_call("mod_bwd", body, [(dx_res, True), (dh, True), (x, True), (scl, False)],
                     [(_sds((t, d), F32), True), (vec, False), (vec, False)], t, acc_outs=(1, 2))


def _loss_head(y, target):
    t, d = y.shape

    def body(y_ref, t_ref, l_ref, dy_ref):
        @pl.when(pl.program_id(0) == 0)
        def _():
            l_ref[...] = jnp.zeros_like(l_ref)

        err = y_ref[...] - t_ref[...]
        dy_ref[...] = err * (1.0 / d)
        part = 0.5 * jnp.sum(jnp.mean(err * err, axis=-1, keepdims=True), axis=0, keepdims=True)
        l_ref[...] += jnp.broadcast_to(part, l_ref.shape)

    return _row_call("loss_head", body, [(y, True), (target, True)],
                     [(_sds((SUBLANES, LANES), F32), False), (_sds((t, d), F32), True)], t,
                     acc_outs=(0,))


def _swiglu_fwd(gu):
    _, t, n = gu.shape
    half = N_DEV // 2
    bt = _row_block(t, BIG_ROW_BLOCK)
    gu4 = gu.reshape(2, half, t, n)

    def body(gu_ref, a_ref):
        g = gu_ref[0]
        a_ref[...] = (g * jax.nn.sigmoid(g) * gu_ref[1]).astype(BF16)

    return pl.pallas_call(
        body, name="swiglu_fwd", grid=(half, t // bt),
        in_specs=[pl.BlockSpec((2, None, bt, n), lambda j, i: (0, j, i, 0))],
        out_specs=pl.BlockSpec((None, bt, n), lambda j, i: (j, i, 0)),
        out_shape=_sds((half, t, n), BF16),
        compiler_params=_params(("parallel", "parallel")),
    )(gu4)


def _swiglu_bwd(gu, da):
    _, t, n = gu.shape
    half = N_DEV // 2
    bt = _row_block(t, BIG_ROW_BLOCK)
    gu4 = gu.reshape(2, half, t, n)

    def body(gu_ref, da_ref, d_ref):
        g = gu_ref[0]
        u = gu_ref[1]
        da = da_ref[...]
        sig = jax.nn.sigmoid(g)
        d_ref[0] = (da * u * sig * (1.0 + g * (1.0 - sig))).astype(BF16)
        d_ref[1] = (da * g * sig).astype(BF16)

    out = pl.pallas_call(
        body, name="swiglu_bwd", grid=(half, t // bt),
        in_specs=[pl.BlockSpec((2, None, bt, n), lambda j, i: (0, j, i, 0)),
                  pl.BlockSpec((None, bt, n), lambda j, i: (j, i, 0))],
        out_specs=pl.BlockSpec((2, None, bt, n), lambda j, i: (0, j, i, 0)),
        out_shape=_sds((2, half, t, n), BF16),
        compiler_params=_params(("parallel", "parallel")),
    )(gu4, da)
    return out.reshape(N_DEV, t, n)


_INV_SQRT2 = 0.7071067811865476
_INV_SQRT_2PI = 0.3989422804014327


def _gelu(z):
    return 0.5 * z * (1.0 + lax.erf(z * _INV_SQRT2))


def _gelu_grad(z):
    return 0.5 * (1.0 + lax.erf(z * _INV_SQRT2)) + z * jnp.exp(-0.5 * z * z) * _INV_SQRT_2PI


def _window_mask():
    t_out = lax.broadcasted_iota(jnp.int32, (GMLP_WINDOW, GMLP_WINDOW), 0)
    s_in = lax.broadcasted_iota(jnp.int32, (GMLP_WINDOW, GMLP_WINDOW), 1)
    return (s_in // CHUNK) <= (t_out // CHUNK)


def _gmlp_recompute(z_ref, bin_ref, lng_ref, lnb_ref):
    half = N_DEV // 2
    z = z_ref[...] + bin_ref[...]
    ge = _gelu(z)
    u = ge[:half]
    v = ge[half:]
    width = half * v.shape[-1]
    mu = jnp.sum(jnp.sum(v, axis=0), axis=-1, keepdims=True) / width
    vc = v - mu
    var = jnp.sum(jnp.sum(vc * vc, axis=0), axis=-1, keepdims=True) / width
    rstd = lax.rsqrt(var + LN_EPS)
    xhat = vc * rstd
    vn = xhat * lng_ref[...] + lnb_ref[...]
    return z, u, xhat, rstd, vn


def _gmlp_mid_fwd(zpre, b_in, ln_g, ln_b, w_s, b_s):
    _, t, n = zpre.shape
    half = N_DEV // 2
    gd = half * n // GMLP_GROUPS
    per = n // gd
    w = GMLP_WINDOW

    def body(z_ref, bin_ref, lng_ref, lnb_ref, ws_ref, bs_ref, o_ref):
        _, u, _, _, vn = _gmlp_recompute(z_ref, bin_ref, lng_ref, lnb_ref)
        mask = _window_mask()
        for g in range(GMLP_GROUPS):
            sh, c0 = g // per, (g % per) * gd
            wsm = jnp.where(mask, ws_ref[g], 0.0).astype(BF16)
            s = jnp.dot(wsm, vn[sh][:, c0:c0 + gd].astype(BF16), preferred_element_type=F32) + bs_ref[g]
            o_ref[sh, :, c0:c0 + gd] = (u[sh][:, c0:c0 + gd] * s).astype(BF16)

    whole = lambda a: pl.BlockSpec(tuple(a.shape), lambda i: (0,) * a.ndim)
    return pl.pallas_call(
        body, name="gmlp_mid_fwd", grid=(t // w,),
        in_specs=[pl.BlockSpec((N_DEV, w, n), lambda i: (0, i, 0)),
                  whole(b_in), whole(ln_g), whole(ln_b), whole(w_s), whole(b_s)],
        out_specs=pl.BlockSpec((half, w, n), lambda i: (0, i, 0)),
        out_shape=_sds((half, t, n), BF16),
        compiler_params=_params(("parallel",)),
    )(zpre, b_in, ln_g, ln_b, w_s, b_s)


def _gmlp_mid_bwd(zpre, dgated, b_in, ln_g, ln_b, w_s, b_s):
    _, t, n = zpre.shape
    half = N_DEV // 2
    gd = half * n // GMLP_GROUPS
    per = n // gd
    w = GMLP_WINDOW
    width = half * n

    def body(z_ref, dg_ref, bin_ref, lng_ref, lnb_ref, ws_ref, bs_ref,
             dz_ref, dws_ref, dbs_ref, dlng_ref, dlnb_ref, dbin_ref, du_ref, dvn_ref):
        @pl.when(pl.program_id(0) == 0)
        def _():
            for r in (dws_ref, dbs_ref, dlng_ref, dlnb_ref, dbin_ref):
                r[...] = jnp.zeros_like(r)

        z, u, xhat, rstd, vn = _gmlp_recompute(z_ref, bin_ref, lng_ref, lnb_ref)
        mask = _window_mask()
        for g in range(GMLP_GROUPS):
            sh, c0 = g // per, (g % per) * gd
            wsm = jnp.where(mask, ws_ref[g], 0.0).astype(BF16)
            vg = vn[sh][:, c0:c0 + gd].astype(BF16)
            s = jnp.dot(wsm, vg, preferred_element_type=F32) + bs_ref[g]
            dgt = dg_ref[sh, :, c0:c0 + gd]
            ds = dgt * u[sh][:, c0:c0 + gd]
            du_ref[sh, :, c0:c0 + gd] = dgt * s
            dsb = ds.astype(BF16)
            dws = lax.dot_general(dsb, vg, (((1,), (1,)), ((), ())), preferred_element_type=F32)
            dws_ref[g] += jnp.where(mask, dws, 0.0)
            dbs_ref[g] += jnp.sum(ds, axis=-1, keepdims=True)
            dvn_ref[sh, :, c0:c0 + gd] = lax.dot_general(wsm, dsb, (((0,), (0,)), ((), ())),
                                                         preferred_element_type=F32)
        dvn = dvn_ref[...]
        dlng_ref[...] += jnp.sum(dvn * xhat, axis=1, keepdims=True)
        dlnb_ref[...] += jnp.sum(dvn, axis=1, keepdims=True)
        dxh = dvn * lng_ref[...]
        m1 = jnp.sum(jnp.sum(dxh, axis=0), axis=-1, keepdims=True) / width
        m2 = jnp.sum(jnp.sum(dxh * xhat, axis=0), axis=-1, keepdims=True) / width
        dv = rstd * (dxh - m1 - xhat * m2)
        gg = _gelu_grad(z)
        dzu = du_ref[...] * gg[:half]
        dzv = dv * gg[half:]
        dz_ref[:half] = dzu.astype(BF16)
        dz_ref[half:] = dzv.astype(BF16)
        dbin_ref[:half] += jnp.sum(dzu, axis=1, keepdims=True)
        dbin_ref[half:] += jnp.sum(dzv, axis=1, keepdims=True)

    whole = lambda a: pl.BlockSpec(tuple(a.shape), lambda i: (0,) * a.ndim)
    outs = [_sds((N_DEV, t, n), BF16), _sds(w_s.shape, F32), _sds(b_s.shape, F32),
            _sds(ln_g.shape, F32), _sds(ln_b.shape, F32), _sds(b_in.shape, F32)]
    return pl.pallas_call(
        body, name="gmlp_mid_bwd", grid=(t // w,),
        in_specs=[pl.BlockSpec((N_DEV, w, n), lambda i: (0, i, 0)),
                  pl.BlockSpec((half, w, n), lambda i: (0, i, 0)),
                  whole(b_in), whole(ln_g), whole(ln_b), whole(w_s), whole(b_s)],
        out_specs=[pl.BlockSpec((N_DEV, w, n), lambda i: (0, i, 0))] + [whole(o) for o in outs[1:]],
        out_shape=outs,
        scratch_shapes=[pltpu.VMEM((half, w, n), F32), pltpu.VMEM((half, w, n), F32)],
        compiler_params=_params(("arbitrary",)),
    )(zpre, dgated, b_in, ln_g, ln_b, w_s, b_s)


ATTN_CHUNKS = 4
ATTN_ROWS = ATTN_CHUNKS * CHUNK
ATTN_WINDOW = ATTN_ROWS + LEFT_PAD
ATTN_DIAGS = 1024
ATTN_ROLL = ATTN_DIAGS - (ATTN_ROWS - 1)


def _rel_vector(rel):
    j = np.arange(ATTN_DIAGS)
    idx = np.clip(ATTN_WINDOW - 1 - j, -(CHUNK - 1), MAX_REL) + (CHUNK - 1)
    return rel[:, idx]


def _attn_bias_mask(rel_ref, bm_ref):
    tt = lax.broadcasted_iota(jnp.int32, (ATTN_ROWS, ATTN_WINDOW), 0) // CHUNK
    rr = lax.broadcasted_iota(jnp.int32, (ATTN_ROWS, ATTN_WINDOW), 1) // CHUNK
    band = (rr >= tt) & (rr <= tt + LEFT_CHUNKS)
    for j in range(bm_ref.shape[0]):
        vec = jnp.broadcast_to(rel_ref[j:j + 1, :], (ATTN_ROWS, ATTN_DIAGS))
        toeplitz = pltpu.roll(vec, ATTN_ROLL, 1, stride=1, stride_axis=0)[:, :ATTN_WINDOW]
        bm_ref[j] = jnp.where(band, toeplitz, -jnp.inf)


def _attn_probs(q_ref, k_ref, bm_ref, j, hd, start, valid):
    qh = q_ref[:, j * hd:(j + 1) * hd]
    kb = k_ref[pl.ds(start, ATTN_WINDOW), j * hd:(j + 1) * hd]
    sc = lax.dot_general(qh, kb, (((1,), (1,)), ((), ())), preferred_element_type=F32)
    sc = sc * (hd ** -0.5) + bm_ref[j]
    sc = jnp.where(valid, sc, -jnp.inf)
    sc = sc - jnp.max(sc, axis=-1, keepdims=True)
    e = jnp.exp(sc)
    return e / jnp.sum(e, axis=-1, keepdims=True), qh, kb


def _window_valid(start):
    r = lax.broadcasted_iota(jnp.int32, (1, ATTN_WINDOW), 1)
    return (start + r) >= LEFT_PAD


def _attn_fwd(q, kvp, rel_vec):
    t, d = q.shape
    hd = d // N_HEADS
    half = N_DEV // 2
    n = kvp.shape[-1]
    per = n // hd
    rows = kvp.shape[1]

    def body(q_ref, k_ref, v_ref, rel_ref, o_ref, bm_ref):
        @pl.when(pl.program_id(1) == 0)
        def _():
            _attn_bias_mask(rel_ref, bm_ref)

        start = pl.multiple_of(pl.program_id(1) * ATTN_ROWS, ATTN_ROWS)
        valid = _window_valid(start)
        for j in range(per):
            p, _, _ = _attn_probs(q_ref, k_ref, bm_ref, j, hd, start, valid)
            vb = v_ref[pl.ds(start, ATTN_WINDOW), j * hd:(j + 1) * hd]
            o_ref[:, j * hd:(j + 1) * hd] = jnp.dot(p.astype(BF16), vb, preferred_element_type=F32).astype(BF16)

    return pl.pallas_call(
        body, name="attn_fwd", grid=(half, t // ATTN_ROWS),
        in_specs=[pl.BlockSpec((ATTN_ROWS, n), lambda g, i: (i, g)),
                  pl.BlockSpec((None, rows, n), lambda g, i: (g, 0, 0)),
                  pl.BlockSpec((None, rows, n), lambda g, i: (half + g, 0, 0)),
                  pl.BlockSpec((None, per, ATTN_DIAGS), lambda g, i: (g, 0, 0))],
        out_specs=pl.BlockSpec((ATTN_ROWS, n), lambda g, i: (i, g)),
        out_shape=_sds((t, d), BF16),
        scratch_shapes=[pltpu.VMEM((per, ATTN_ROWS, ATTN_WINDOW), F32)],
        compiler_params=_params(("arbitrary", "arbitrary")),
    )(q, kvp, kvp, rel_vec.reshape(half, per, ATTN_DIAGS))


def _attn_bwd(q, dout, kvp, rel_vec, dk_in=None, dv_in=None):
    t, d = q.shape
    hd = d // N_HEADS
    half = N_DEV // 2
    n = kvp.shape[-1]
    per = n // hd
    rows = kvp.shape[1]
    scale = hd ** -0.5
    carry = dk_in is not None

    def body(q_ref, do_ref, k_ref, v_ref, rel_ref, *rest):
        dq_ref, dk_ref, dv_ref, dsc_ref, bm_ref = rest[-5:]

        @pl.when(pl.program_id(1) == 0)
        def _():
            _attn_bias_mask(rel_ref, bm_ref)
            dk_ref[...] = rest[0][...] if carry else jnp.zeros_like(dk_ref)
            dv_ref[...] = rest[1][...] if carry else jnp.zeros_like(dv_ref)
            dsc_ref[...] = jnp.zeros_like(dsc_ref)

        start = pl.multiple_of(pl.program_id(1) * ATTN_ROWS, ATTN_ROWS)
        valid = _window_valid(start)
        for j in range(per):
            cols = slice(j * hd, (j + 1) * hd)
            p, qh, kb = _attn_probs(q_ref, k_ref, bm_ref, j, hd, start, valid)
            vb = v_ref[pl.ds(start, ATTN_WINDOW), cols]
            doh = do_ref[:, cols]
            dp = lax.dot_general(doh, vb, (((1,), (1,)), ((), ())), preferred_element_type=F32)
            ds = p * (dp - jnp.sum(dp * p, axis=-1, keepdims=True))
            dsc_ref[j] += sum(ds[a * CHUNK:(a + 1) * CHUNK, a * CHUNK:a * CHUNK + BAND]
                              for a in range(ATTN_CHUNKS))
            dsb = (ds * scale).astype(BF16)
            dq_ref[:, cols] = jnp.dot(dsb, kb, preferred_element_type=F32).astype(BF16)
            dk_ref[pl.ds(start, ATTN_WINDOW), cols] += lax.dot_general(
                dsb, qh, (((0,), (0,)), ((), ())), preferred_element_type=F32)
            dv_ref[pl.ds(start, ATTN_WINDOW), cols] += lax.dot_general(
                p.astype(BF16), doh, (((0,), (0,)), ((), ())), preferred_element_type=F32)

    tile = pl.BlockSpec((ATTN_ROWS, n), lambda g, i: (i, g))
    shard = pl.BlockSpec((None, rows, n), lambda g, i: (g, 0, 0))
    in_specs = [tile, tile, shard, pl.BlockSpec((None, rows, n), lambda g, i: (half + g, 0, 0)),
                pl.BlockSpec((None, per, ATTN_DIAGS), lambda g, i: (g, 0, 0))]
    operands = [q, dout, kvp, kvp, rel_vec.reshape(half, per, ATTN_DIAGS)]
    if carry:
        in_specs += [shard, shard]
        operands += [dk_in, dv_in]
    acc = _sds((half, rows, n), F32)
    return pl.pallas_call(
        body, name="attn_bwd", grid=(half, t // ATTN_ROWS),
        in_specs=in_specs,
        out_specs=[tile, shard, shard, pl.BlockSpec((per, CHUNK, BAND), lambda g, i: (g, 0, 0))],
        out_shape=[_sds((t, d), BF16), acc, acc, _sds((N_HEADS, CHUNK, BAND), F32)],
        scratch_shapes=[pltpu.VMEM((per, ATTN_ROWS, ATTN_WINDOW), F32)],
        compiler_params=_params(("arbitrary", "arbitrary")),
    )(*operands)


SKEW_PITCH = 640
SKEW = SKEW_PITCH + 1
SKEW_LANES = -(-SKEW // LANES) * LANES


def _skew_diagonals(dsc):
    h = dsc.shape[0]
    wide = jnp.pad(dsc, ((0, 0), (0, 0), (0, SKEW_PITCH - BAND))).reshape(h, CHUNK * SKEW_PITCH)
    wide = jnp.pad(wide, ((0, 0), (0, CHUNK))).reshape(h, CHUNK, SKEW)
    return jnp.pad(wide, ((0, 0), (0, 0), (0, SKEW_LANES - SKEW)))


def _rel_bias_grad(skewed):
    heads = skewed.shape[0]
    hb = SUBLANES

    def body(d_ref, o_ref):
        col = lax.broadcasted_iota(jnp.int32, (SKEW_LANES, N_REL), 0)
        bucket = lax.broadcasted_iota(jnp.int32, (SKEW_LANES, N_REL), 1)
        diag = jnp.where(col < BAND, col, col - SKEW)
        idx = jnp.clip(LEFT_PAD - diag, -(CHUNK - 1), MAX_REL) + (CHUNK - 1)
        oh = ((idx == bucket) & (col < SKEW)).astype(BF16)
        dv = jnp.sum(d_ref[...], axis=1)
        hi = dv.astype(BF16)
        rest = dv - hi.astype(F32)
        mid = rest.astype(BF16)
        lo = (rest - mid.astype(F32)).astype(BF16)
        acc = jnp.dot(hi, oh, preferred_element_type=F32)
        acc += jnp.dot(mid, oh, preferred_element_type=F32)
        acc += jnp.dot(lo, oh, preferred_element_type=F32)
        o_ref[...] = acc

    return pl.pallas_call(
        body, name="rel_bias_grad", grid=(heads // hb,),
        in_specs=[pl.BlockSpec((hb, CHUNK, SKEW_LANES), lambda i: (i, 0, 0))],
        out_specs=pl.BlockSpec((hb, N_REL), lambda i: (i, 0)),
        out_shape=_sds((heads, N_REL), F32),
        compiler_params=_params(("parallel",)),
    )(skewed)


def _sum_parts(parts):
    s_n, rows, c = parts.shape
    br = _row_block(rows, OPT_ROW_BLOCK)

    def body(p_ref, o_ref):
        acc = p_ref[0].astype(F32)
        for s in range(1, s_n):
            acc = acc + p_ref[s].astype(F32)
        o_ref[...] = acc

    return pl.pallas_call(
        body, name="sum_parts", grid=(rows // br,),
        in_specs=[pl.BlockSpec((s_n, br, c), lambda i: (0, i, 0))],
        out_specs=pl.BlockSpec((br, c), lambda i: (i, 0)),
        out_shape=_sds((rows, c), F32),
        compiler_params=_params(("parallel",)),
    )(parts)


def _adamw(own, own_idx, parts, w, m, v, row0=0, bufs=None):
    _, rows, c = own.shape
    s_n = 0 if parts is None else parts.shape[0]
    total = w.shape[0]
    br = _row_block(rows, OPT_ROW_BLOCK)
    assert row0 % br == 0 and (bufs is not None or (row0 == 0 and total == rows))
    b0 = row0 // br
    m_corr = 1.0 - ADAM_B1 ** ADAM_STEP
    v_corr = 1.0 - ADAM_B2 ** ADAM_STEP

    def body(idx_ref, own_ref, *refs):
        if s_n:
            p_ref, refs = refs[0], refs[1:]
        w_ref, m_ref, v_ref = refs[:3]
        g_ref, d_ref, nm_ref, nv_ref = refs[-4:]
        g = own_ref[...].astype(F32)
        for s in range(s_n):
            g = g + p_ref[s].astype(F32)
        nm = ADAM_B1 * m_ref[...] + (1.0 - ADAM_B1) * g
        nv = ADAM_B2 * v_ref[...] + (1.0 - ADAM_B2) * (g * g)
        g_ref[...] = g
        nm_ref[...] = nm
        nv_ref[...] = nv
        d_ref[...] = -ADAM_LR * ((nm / m_corr) / (jnp.sqrt(nv / v_corr) + ADAM_EPS) + ADAM_WD * w_ref[...])

    tile = pl.BlockSpec((br, c), lambda i, idx: (i + b0, 0))
    in_specs = [pl.BlockSpec((None, br, c), lambda i, idx: (idx[0], i, 0))]
    operands = [own_idx, own]
    if s_n:
        in_specs.append(pl.BlockSpec((s_n, br, c), lambda i, idx: (0, i, 0)))
        operands.append(parts)
    in_specs += [tile, tile, tile]
    operands += [w, m, v]
    aliases = {}
    if bufs is not None:
        aliases = {len(operands) + j: j for j in range(4)}
        in_specs += [_ANY] * 4
        operands += list(bufs)
    out = _sds((total, c), F32)
    return pl.pallas_call(
        body, name="adamw",
        grid_spec=pltpu.PrefetchScalarGridSpec(
            num_scalar_prefetch=1, grid=(rows // br,), in_specs=in_specs,
            out_specs=[tile, tile, tile, tile]),
        out_shape=[out, out, out, out],
        input_output_aliases=aliases,
        compiler_params=_params(("parallel",)),
    )(*operands)


def _chip_sum(p, r1, core):
    half = N_DEV // 2
    c = p.shape[-1]
    rows = int(np.prod(p.shape[1:-1]))
    br = _row_block(rows, BIG_ROW_BLOCK)

    def body(core_ref, p_ref, r_ref, o_ref):
        o_ref[...] = (p_ref[...].astype(F32) + r_ref[...].astype(F32)).astype(BF16)

    out = pl.pallas_call(
        body, name="chip_sum",
        grid_spec=pltpu.PrefetchScalarGridSpec(
            num_scalar_prefetch=1, grid=(half, rows // br),
            in_specs=[pl.BlockSpec((None, None, br, c), lambda q, i, cr: (q, cr[0], i, 0)),
                      pl.BlockSpec((None, br, c), lambda q, i, cr: (q, i, 0))],
            out_specs=pl.BlockSpec((None, br, c), lambda q, i, cr: (q, i, 0))),
        out_shape=_sds((half, rows, c), BF16),
        compiler_params=_params(("parallel", "parallel")),
    )(core, p.reshape(half, 2, rows, c), r1.reshape(half, rows, c))
    return out.reshape((half,) + p.shape[1:])


def _position():
    return tuple(lax.axis_index(a) for a in MESH_AXES)


def _linear(px, py, pc):
    return 4 * px + 2 * py + pc


def _all_gather_small(v):
    rows, lanes = v.shape

    def body(x_ref, out_ref, send_sems, recv_sems, local_sem):
        x, y, c = _position()
        me, sibling = (x, y, c), (x, y, 1 - c)
        chips = [(1 - x, y), (x, 1 - y), (1 - x, 1 - y)]

        def copy(k, block, to, src=None):
            dst = out_ref.at[_linear(*block)]
            return pltpu.make_async_remote_copy(
                src_ref=dst if src is None else src, dst_ref=dst,
                send_sem=send_sems.at[k], recv_sem=recv_sems.at[k],
                device_id=to, device_id_type=MESH_ID)

        mine = pltpu.make_async_copy(x_ref, out_ref.at[_linear(*me)], local_sem)
        mine.start()
        first = [copy(0, me, sibling, src=x_ref)]
        first += [copy(1 + j, me, (*chip, c), src=x_ref) for j, chip in enumerate(chips)]
        for cp in first:
            cp.start()
        passed = [copy(4 + j, (*chip, c), sibling) for j, chip in enumerate(chips)]
        for j, chip in enumerate(chips):
            copy(1 + j, (*chip, c), me).wait_recv()
            passed[j].start()
        copy(0, sibling, me).wait_recv()
        for j, chip in enumerate(chips):
            copy(4 + j, (*chip, 1 - c), me).wait_recv()
        for cp in first + passed:
            cp.wait_send()
        mine.wait()

    return pl.pallas_call(
        body, name="all_gather_small",
        out_shape=_sds((N_DEV, rows, lanes), v.dtype),
        in_specs=[_VMEM], out_specs=_VMEM,
        scratch_shapes=[pltpu.SemaphoreType.DMA((7,)), pltpu.SemaphoreType.DMA((7,)),
                        pltpu.SemaphoreType.DMA],
        compiler_params=pltpu.CompilerParams(vmem_limit_bytes=VMEM_LIMIT),
    )(v)


_HBM = pl.BlockSpec(memory_space=pltpu.HBM)
_SEM = pl.BlockSpec(memory_space=pltpu.SEMAPHORE)
_EFFECT = pltpu.SideEffectType.DATAFLOW_SIDE_EFFECTING
_ALL_CHIPS = [(0, 0), (0, 1), (1, 0), (1, 1)]


def _other_chips(x, y):
    return [(1 - x, y), (x, 1 - y), (1 - x, 1 - y)]


def _in_hbm(a):
    return pltpu.with_memory_space_constraint(a, pltpu.HBM)


def _token():
    return _sds((SUBLANES, LANES), F32)


def _gather_ici_copy(ref, i, k, chip, c, block, send_sems, recv_sems):
    return pltpu.make_async_remote_copy(
        src_ref=ref.at[block], dst_ref=ref.at[block],
        send_sem=send_sems.at[3 * i + k], recv_sem=recv_sems.at[3 * i + k],
        device_id=(*chip, c), device_id_type=MESH_ID)


def _gather_ici_start(name, lands):
    n = len(lands)

    def body(*refs):
        ins, send_sems, recv_sems, token = refs[:n], refs[n], refs[n + 1], refs[-1]
        x, y, c = _position()
        me = _linear(x, y, c)
        for i in range(n):
            for k, chip in enumerate(_other_chips(x, y)):
                _gather_ici_copy(ins[i], i, k, chip, c, me, send_sems, recv_sems).start()
        token[...] = jnp.zeros_like(token)

    out = pl.pallas_call(
        body, name=name,
        out_shape=(pltpu.SemaphoreType.DMA((3 * n,)), pltpu.SemaphoreType.DMA((3 * n,)),
                   *[pltpu.HBM(a.shape, a.dtype) for a in lands], _token()),
        in_specs=[_HBM] * n, out_specs=(_SEM, _SEM, *[_HBM] * n, _VMEM),
        input_output_aliases={i: 2 + i for i in range(n)},
        compiler_params=pltpu.CompilerParams(has_side_effects=_EFFECT),
    )(*[_in_hbm(a) for a in lands])
    return out[0], out[1], list(out[2:2 + n]), out[-1]


def _gather_ici_wait(name, lands, send_sems, recv_sems, after):
    n = len(lands)

    def body(*refs):
        ins, ss, rs = refs[:n], refs[n], refs[n + 1]
        x, y, c = _position()
        me = _linear(x, y, c)
        for i in range(n):
            for k, chip in enumerate(_other_chips(x, y)):
                _gather_ici_copy(ins[i], i, k, chip, c, me, ss, rs).wait_send()
                _gather_ici_copy(ins[i], i, k, chip, c, _linear(*chip, c), ss, rs).wait_recv()

    out = pl.pallas_call(
        body, name=name,
        out_shape=[pltpu.HBM(a.shape, a.dtype) for a in lands],
        in_specs=[_HBM] * n + [_SEM, _SEM, _ANY], out_specs=[_HBM] * n,
        input_output_aliases={i: i for i in range(n)},
        compiler_params=pltpu.CompilerParams(has_side_effects=_EFFECT),
    )(*lands, send_sems, recv_sems, after)
    return list(out)


def _gather_d2d(lands):
    n = len(lands)

    def body(*refs):
        ins, outs, send_sems, recv_sems = refs[:n], refs[n:2 * n], refs[2 * n], refs[2 * n + 1]
        x, y, c = _position()

        def copy(i, q, core):
            block = _linear(*_ALL_CHIPS[q], core)
            return pltpu.make_async_remote_copy(
                src_ref=ins[i].at[block], dst_ref=outs[i].at[block],
                send_sem=send_sems.at[i, q], recv_sem=recv_sems.at[i, q],
                device_id=(x, y, 1 - c), device_id_type=MESH_ID)

        sent = [copy(i, q, c) for i in range(n) for q in range(len(_ALL_CHIPS))]
        for cp in sent:
            cp.start()
        for i in range(n):
            for q in range(len(_ALL_CHIPS)):
                copy(i, q, 1 - c).wait_recv()
        for cp in sent:
            cp.wait_send()

    return pl.pallas_call(
        body, name="gather_d2d",
        out_shape=[_sds(a.shape, a.dtype) for a in lands],
        in_specs=[_ANY] * n, out_specs=[_ANY] * n,
        input_output_aliases={i: i for i in range(n)},
        scratch_shapes=[pltpu.SemaphoreType.DMA((n, 4)), pltpu.SemaphoreType.DMA((n, 4))],
    )(*lands)


def _partials_d2d(parts):
    n = len(parts)
    half = N_DEV // 2

    def body(*refs):
        ins, outs, send_sems, recv_sems = refs[:n], refs[n:2 * n], refs[2 * n], refs[2 * n + 1]
        x, y, c = _position()

        def copy(i, q):
            return pltpu.make_async_remote_copy(
                src_ref=ins[i].at[_linear(*_ALL_CHIPS[q], 1 - c)], dst_ref=outs[i].at[q],
                send_sem=send_sems.at[i, q], recv_sem=recv_sems.at[i, q],
                device_id=(x, y, 1 - c), device_id_type=MESH_ID)

        sent = [copy(i, q) for i in range(n) for q in range(half)]
        for cp in sent:
            cp.start()
        for cp in sent:
            cp.wait_recv()
        for cp in sent:
            cp.wait_send()

    return pl.pallas_call(
        body, name="partials_d2d",
        out_shape=[_sds((half,) + p.shape[1:], p.dtype) for p in parts],
        in_specs=[_ANY] * n, out_specs=[_ANY] * n,
        scratch_shapes=[pltpu.SemaphoreType.DMA((n, half)), pltpu.SemaphoreType.DMA((n, half))],
    )(*parts)


def _partials_ici_copy(src, land, i, k, chip, c, send_sems, recv_sems):
    return pltpu.make_async_remote_copy(
        src_ref=src.at[2 * chip[0] + chip[1]], dst_ref=land.at[k],
        send_sem=send_sems.at[3 * i + k], recv_sem=recv_sems.at[3 * i + k],
        device_id=(*chip, c), device_id_type=MESH_ID)


def _partials_ici_start(name, sums, lands):
    n = len(sums)

    def body(*refs):
        srcs, dsts, send_sems, recv_sems, token = refs[:n], refs[n:2 * n], refs[2 * n], refs[2 * n + 1], refs[-1]
        x, y, c = _position()
        for i in range(n):
            for k, chip in enumerate(_other_chips(x, y)):
                _partials_ici_copy(srcs[i], dsts[i], i, k, chip, c, send_sems, recv_sems).start()
        token[...] = jnp.zeros_like(token)

    both = list(sums) + list(lands)
    out = pl.pallas_call(
        body, name=name,
        out_shape=(pltpu.SemaphoreType.DMA((3 * n,)), pltpu.SemaphoreType.DMA((3 * n,)),
                   *[pltpu.HBM(a.shape, a.dtype) for a in both], _token()),
        in_specs=[_HBM] * (2 * n), out_specs=(_SEM, _SEM, *[_HBM] * (2 * n), _VMEM),
        input_output_aliases={i: 2 + i for i in range(2 * n)},
        compiler_params=pltpu.CompilerParams(has_side_effects=_EFFECT),
    )(*[_in_hbm(a) for a in both])
    return out[0], out[1], list(out[2:2 + n]), list(out[2 + n:2 + 2 * n]), out[-1]


def _partials_ici_wait(name, sums, lands, send_sems, recv_sems, after):
    n = len(sums)

    def body(*refs):
        srcs, dsts, ss, rs = refs[:n], refs[n:2 * n], refs[2 * n], refs[2 * n + 1]
        x, y, c = _position()
        for i in range(n):
            for k, chip in enumerate(_other_chips(x, y)):
                cp = _partials_ici_copy(srcs[i], dsts[i], i, k, chip, c, ss, rs)
                cp.wait_send()
                cp.wait_recv()

    both = list(sums) + list(lands)
    out = pl.pallas_call(
        body, name=name,
        out_shape=[pltpu.HBM(a.shape, a.dtype) for a in both],
        in_specs=[_HBM] * (2 * n) + [_SEM, _SEM, _ANY], out_specs=[_HBM] * (2 * n),
        input_output_aliases={i: i for i in range(2 * n)},
        compiler_params=pltpu.CompilerParams(has_side_effects=_EFFECT),
    )(*both, send_sems, recv_sems, after)
    return list(out[:n]), list(out[n:])


def _pack(arrs):
    flat = jnp.concatenate([a.reshape(-1).astype(F32) for a in arrs])
    block = OPT_ROW_BLOCK if flat.shape[0] > OPT_ROW_BLOCK * LANES else SUBLANES
    pad = (-flat.shape[0]) % (block * LANES)
    if pad:
        flat = jnp.concatenate([flat, jnp.zeros((pad,), F32)])
    return flat.reshape(-1, LANES)


def _unpack(packed, shapes, lead=()):
    flat = packed.reshape(lead + (-1,))
    out, off = [], 0
    for s in shapes:
        size = int(np.prod(s))
        out.append(flat[..., off:off + size].reshape(lead + tuple(s)))
        off += size
    return out


def _unshard_last(g):
    nd = g.ndim
    perm = tuple(range(1, nd - 1)) + (0, nd - 1)
    t = jnp.transpose(g, perm)
    return t.reshape(t.shape[:-2] + (N_DEV * g.shape[-1],))


def kernel(x, c, w_ada, b_ada, ln_g, ln_b, ffn_gu, ffn_down, gmlp_w_in, gmlp_b_in, gmlp_ln_g, gmlp_ln_b, gmlp_w_s, gmlp_b_s, gmlp_w_out, w_ada_kv, b_ada_kv, w_kv, attn_w_q, attn_rel_bias, attn_w_o, loss_target, m_w_ada, m_b_ada, m_ln_g, m_ln_b, m_ffn_gu, m_ffn_down, m_gmlp_w_in, m_gmlp_b_in, m_gmlp_ln_g, m_gmlp_ln_b, m_gmlp_w_s, m_gmlp_b_s, m_gmlp_w_out, m_w_ada_kv, m_b_ada_kv, m_w_kv, m_attn_w_q, m_attn_rel_bias, m_attn_w_o, v_w_ada, v_b_ada, v_ln_g, v_ln_b, v_ffn_gu, v_ffn_down, v_gmlp_w_in, v_gmlp_b_in, v_gmlp_ln_g, v_gmlp_ln_b, v_gmlp_w_s, v_gmlp_b_s, v_gmlp_w_out, v_w_ada_kv, v_b_ada_kv, v_w_kv, v_attn_w_q, v_attn_rel_bias, v_attn_w_o):
    weights = dict(w_ada=w_ada, b_ada=b_ada, ln_g=ln_g, ln_b=ln_b, ffn_gu=ffn_gu, ffn_down=ffn_down,
                   gmlp_w_in=gmlp_w_in, gmlp_b_in=gmlp_b_in, gmlp_ln_g=gmlp_ln_g, gmlp_ln_b=gmlp_ln_b,
                   gmlp_w_s=gmlp_w_s, gmlp_b_s=gmlp_b_s, gmlp_w_out=gmlp_w_out, w_ada_kv=w_ada_kv,
                   b_ada_kv=b_ada_kv, w_kv=w_kv, attn_w_q=attn_w_q, attn_rel_bias=attn_rel_bias,
                   attn_w_o=attn_w_o)
    mom1 = dict(w_ada=m_w_ada, b_ada=m_b_ada, ln_g=m_ln_g, ln_b=m_ln_b, ffn_gu=m_ffn_gu, ffn_down=m_ffn_down,
                gmlp_w_in=m_gmlp_w_in, gmlp_b_in=m_gmlp_b_in, gmlp_ln_g=m_gmlp_ln_g, gmlp_ln_b=m_gmlp_ln_b,
                gmlp_w_s=m_gmlp_w_s, gmlp_b_s=m_gmlp_b_s, gmlp_w_out=m_gmlp_w_out, w_ada_kv=m_w_ada_kv,
                b_ada_kv=m_b_ada_kv, w_kv=m_w_kv, attn_w_q=m_attn_w_q, attn_rel_bias=m_attn_rel_bias,
                attn_w_o=m_attn_w_o)
    mom2 = dict(w_ada=v_w_ada, b_ada=v_b_ada, ln_g=v_ln_g, ln_b=v_ln_b, ffn_gu=v_ffn_gu, ffn_down=v_ffn_down,
                gmlp_w_in=v_gmlp_w_in, gmlp_b_in=v_gmlp_b_in, gmlp_ln_g=v_gmlp_ln_g, gmlp_ln_b=v_gmlp_ln_b,
                gmlp_w_s=v_gmlp_w_s, gmlp_b_s=v_gmlp_b_s, gmlp_w_out=v_gmlp_w_out, w_ada_kv=v_w_ada_kv,
                b_ada_kv=v_b_ada_kv, w_kv=v_w_kv, attn_w_q=v_attn_w_q, attn_rel_bias=v_attn_rel_bias,
                attn_w_o=v_attn_w_o)
    order = list(weights)

    x = x[0]
    target = loss_target[0]
    t, d = x.shape
    n_mod = w_ada.shape[-1] * N_DEV // d
    mod_w = w_ada.shape[-1]
    kv_w = w_ada_kv.shape[-1]
    n_b = DEPTH - N_A
    me = _linear(*_position())

    l2 = DEPTH * 2
    big = dict(
        ffn_gu=ffn_gu.reshape((l2,) + ffn_gu.shape[2:]),
        ffn_down=ffn_down.reshape((l2,) + ffn_down.shape[2:]),
        gmlp_w_in=gmlp_w_in, gmlp_w_out=gmlp_w_out, w_kv=w_kv[None],
        attn_w_q=attn_w_q, attn_w_o=attn_w_o)
    big_names = list(big)
    core = lax.axis_index("c").astype(jnp.int32).reshape(1)
    chip = (2 * lax.axis_index("x") + lax.axis_index("y")).astype(jnp.int32).reshape(1)

    fwd_groups = [
        {"ffn_gu": (0, 1), "ffn_down": (0, 1)},
        {"gmlp_w_in": (0, 1), "gmlp_w_out": (0, 1), "ffn_gu": (1, 1), "ffn_down": (1, 1)},
        {"ffn_gu": (2, 2), "ffn_down": (2, 2), "gmlp_w_in": (1, 1), "gmlp_w_out": (1, 1), "w_kv": (0, 1)},
        {"ffn_gu": (4, 2), "ffn_down": (4, 2), "attn_w_q": (0, 1), "attn_w_o": (0, 1)},
        {"ffn_gu": (6, 2), "ffn_down": (6, 2), "attn_w_q": (1, 1), "attn_w_o": (1, 1)},
    ]
    bwd_groups = []
    for l in range(DEPTH):
        g = {"ffn_gu": (2 * l, 2), "ffn_down": (2 * l, 2)}
        if l < N_A:
            g.update({"gmlp_w_in": (l, 1), "gmlp_w_out": (l, 1)})
        else:
            g.update({"attn_w_q": (l - N_A, 1), "attn_w_o": (l - N_A, 1)})
        if l == N_A - 1:
            g["w_kv"] = (0, 1)
        bwd_groups.append(g)

    def slot_of(groups, name, slot):
        for gi, g in enumerate(groups):
            if name in g and g[name][0] <= slot < g[name][0] + g[name][1]:
                return gi, slot - g[name][0]
        raise KeyError((name, slot))

    flights = []
    for gi, g in enumerate(fwd_groups):
        lands = []
        for name, (s0, cnt) in g.items():
            shard = big[name][s0:s0 + cnt].astype(BF16)
            land = lax.empty((N_DEV,) + shard.shape, BF16)
            lands.append(lax.dynamic_update_slice(land, shard[None], (me,) + (0,) * shard.ndim))
        flights.append(_gather_ici_start(f"gather_ici_start_{gi}", lands))
    start_token = sum(f[3][0, 0] for f in flights)
    gathered = [None] * len(fwd_groups)

    def land_group(gi, after):
        send_sems, recv_sems, lands, _ = flights[gi]
        lands = _gather_ici_wait(f"gather_ici_wait_{gi}", lands, send_sems, recv_sems, after)
        gathered[gi] = dict(zip(fwd_groups[gi], _gather_d2d(lands)))

    def weight(name, slot):
        gi, local = slot_of(fwd_groups, name, slot)
        return gathered[gi][name], local

    partial = [{name: lax.empty((N_DEV, cnt) + big[name].shape[1:], BF16) for name, (_, cnt) in g.items()}
               for g in bwd_groups]

    c_all = _all_gather_small(_pack([c]))
    c_all = _unpack(c_all, [(d,)], lead=(N_DEV,))[0]
    c4 = _as4(c_all)
    mod_part = _matmul("ada_fwd", c4, w_ada[:, None], (DEPTH, 1, N_DEV, mod_w), F32, a_silu=True)
    kv_part = _matmul("ada_kv_fwd", c4, _as4(w_ada_kv), (1, 1, N_DEV, kv_w), F32, a_silu=True)
    small_shapes = [mod_part.shape, kv_part.shape, ln_g.shape, ln_b.shape, gmlp_b_in.shape,
                    gmlp_ln_g.shape, gmlp_ln_b.shape, attn_rel_bias.shape]
    small = _all_gather_small(_pack([mod_part, kv_part, ln_g, ln_b, gmlp_b_in, gmlp_ln_g, gmlp_ln_b,
                                     attn_rel_bias]))
    (mod_g, kvm_g, ln_g_g, ln_b_g, b_in_g, gln_g_g, gln_b_g, rel_g) = _unpack(small, small_shapes, lead=(N_DEV,))
    mod_mine = lax.dynamic_index_in_dim(mod_g[:, :, 0], me, axis=2, keepdims=False)
    mod = _unshard_last(mod_mine) + b_ada
    mod = mod.reshape(DEPTH, n_mod, 1, d)
    kvm_mine = lax.dynamic_index_in_dim(kvm_g[:, 0, 0], me, axis=1, keepdims=False)
    mkv = (_unshard_last(kvm_mine) + b_ada_kv).reshape(2, 1, d)
    ln_g_f = _unshard_last(ln_g_g)
    ln_b_f = _unshard_last(ln_b_g)
    half = N_DEV // 2
    b_in_f = jnp.transpose(b_in_g, (1, 0, 2))[:, :, None, :]
    gln_g_f = _unshard_last(gln_g_g).reshape(N_A, half, 1, -1)
    gln_b_f = _unshard_last(gln_b_g).reshape(N_A, half, 1, -1)
    rel_f = _unshard_last(rel_g)

    def shard_act(a):
        return a.reshape(a.shape[0], a.shape[2], a.shape[3])

    def grad_into(name, slot, mm):
        gi, local = slot_of(bwd_groups, name, slot)
        partial[gi][name] = mm(partial[gi][name], local)

    def ffn_fwd(h, lw):
        w_gu, l_gu = weight("ffn_gu", lw)
        w_dn, l_dn = weight("ffn_down", lw)
        n = w_gu.shape[-1]
        gu = _matmul("ffn_gu_fwd", _as4(h), w_gu, (N_DEV, 1, t, n), F32, lb=l_gu)
        a = _swiglu_fwd(shard_act(gu))
        y = _matmul("ffn_down_fwd", a[:, None], w_dn, (1, 1, t, d), F32, lb=l_dn, b_merge=2, reduce=True)
        return y[0, 0], (gu, a)

    def ffn_bwd(dy, h, saved, lw):
        gu, a = saved
        w_gu, l_gu = weight("ffn_gu", lw)
        w_dn, l_dn = weight("ffn_down", lw)
        n = w_gu.shape[-1]
        da = _matmul("ffn_down_bwd_a", _as4(dy), w_dn, (half, 1, t, n), F32, lb=l_dn, b_merge=2, tb=True)
        grad_into("ffn_down", lw, lambda buf, lo: _matmul(
            "ffn_down_bwd_w", a[:, None], _as4(dy), buf.shape, BF16, ta=True, lo=lo, out_merge=2, out_buf=buf))
        dgu = _swiglu_bwd(shard_act(gu), shard_act(da))
        dh = _matmul("ffn_gu_bwd_a", dgu[:, None], w_gu, (1, 1, t, d), F32, lb=l_gu, tb=True, reduce=True)
        grad_into("ffn_gu", lw, lambda buf, lo: _matmul(
            "ffn_gu_bwd_w", _as4(h), dgu[:, None], buf.shape, BF16, ta=True, lo=lo, out_buf=buf))
        return dh[0, 0], {}

    def gmlp_params(l):
        return (b_in_f[l], gln_g_f[l], gln_b_f[l], gmlp_w_s[l], gmlp_b_s[l][:, :, None])

    def gmlp_fwd(h, l):
        w_in, l_in = weight("gmlp_w_in", l)
        w_out, l_out = weight("gmlp_w_out", l)
        n = w_in.shape[-1]
        zpre = _matmul("gmlp_in_fwd", _as4(h), w_in, (N_DEV, 1, t, n), F32, lb=l_in)
        gated = _gmlp_mid_fwd(shard_act(zpre), *gmlp_params(l))
        y = _matmul("gmlp_out_fwd", gated[:, None], w_out, (1, 1, t, d), F32, lb=l_out, b_merge=2, reduce=True)
        return y[0, 0], (zpre, gated)

    def gmlp_bwd(dy, h, saved, l):
        zpre, gated = saved
        w_in, l_in = weight("gmlp_w_in", l)
        w_out, l_out = weight("gmlp_w_out", l)
        n = w_in.shape[-1]
        dgated = _matmul("gmlp_out_bwd_a", _as4(dy), w_out, (half, 1, t, n), F32, lb=l_out, b_merge=2, tb=True)
        grad_into("gmlp_w_out", l, lambda buf, lo: _matmul(
            "gmlp_out_bwd_w", gated[:, None], _as4(dy), buf.shape, BF16, ta=True, lo=lo, out_merge=2, out_buf=buf))
        dz, dws, dbs, dlng, dlnb, dbin = _gmlp_mid_bwd(shard_act(zpre), shard_act(dgated), *gmlp_params(l))
        dh = _matmul("gmlp_in_bwd_a", dz[:, None], w_in, (1, 1, t, d), F32, lb=l_in, tb=True, reduce=True)
        grad_into("gmlp_w_in", l, lambda buf, lo: _matmul(
            "gmlp_in_bwd_w", _as4(h), dz[:, None], buf.shape, BF16, ta=True, lo=lo, out_buf=buf))
        small_grads = dict(gmlp_w_s=dws, gmlp_b_s=dbs[:, :, 0], gmlp_ln_g=dlng.reshape(-1),
                           gmlp_ln_b=dlnb.reshape(-1), gmlp_b_in=dbin.reshape(-1))
        return dh[0, 0], small_grads

    def attn_fwd(h, j, kvp):
        rel_vec = _rel_vector(rel_f[j])
        w_q, l_q = weight("attn_w_q", j)
        w_o, l_o = weight("attn_w_o", j)
        q = _matmul("attn_q_fwd", _as4(h), w_q, (1, 1, t, d), BF16, lb=l_q, b_merge=N_DEV, reduce=True)[0, 0]
        o = _attn_fwd(q, kvp, rel_vec)
        y = _matmul("attn_o_fwd", _as4(o), w_o, (1, 1, t, d), F32, lb=l_o, b_merge=N_DEV, reduce=True)
        return y[0, 0], (q, o, rel_vec)

    def attn_bwd(dy, h, saved, j, kvp, dkv_acc):
        q, o, rel_vec = saved
        w_q, l_q = weight("attn_w_q", j)
        w_o, l_o = weight("attn_w_o", j)
        do = _matmul("attn_o_bwd_a", _as4(dy), w_o, (1, 1, t, d), BF16, lb=l_o, b_merge=N_DEV, tb=True)[0, 0]
        grad_into("attn_w_o", j, lambda buf, lo: _matmul(
            "attn_o_bwd_w", _as4(o), _as4(dy), buf.shape, BF16, ta=True, lo=lo, out_merge=N_DEV, out_buf=buf))
        dq, dk, dv, dsc = _attn_bwd(q, do, kvp, rel_vec, *dkv_acc)
        drel = _rel_bias_grad(_skew_diagonals(dsc))
        dh = _matmul("attn_q_bwd_a", _as4(dq), w_q, (1, 1, t, d), F32, lb=l_q, b_merge=N_DEV, tb=True)
        grad_into("attn_w_q", j, lambda buf, lo: _matmul(
            "attn_q_bwd_w", _as4(h), _as4(dq), buf.shape, BF16, ta=True, lo=lo, out_merge=N_DEV, out_buf=buf))
        return dh[0, 0], dict(attn_rel_bias=drel, dkv=(dk, dv))

    tape = []
    kvp = None
    kv_tape = None
    first_use = {(0, 0): 0, (0, 1): 1, (1, 0): 2, (2, 0): 3, (3, 0): 4}
    mod = mod.at[0, 0].add(start_token)
    for l in range(DEPTH):
        for i in range(3):
            if (l, i) in first_use:
                land_group(first_use[l, i], x)
            shift, scl, gate = mod[l, 3 * i], mod[l, 3 * i + 1], mod[l, 3 * i + 2]
            wgt = 1.0 if i == 1 else 0.5
            gw = wgt * (1.0 + gate)
            h = _modulate(x, scl, shift)
            if i != 1:
                y, saved = ffn_fwd(h, 2 * l + i // 2)
            elif l < N_A:
                y, saved = gmlp_fwd(h, l)
            else:
                y, saved = attn_fwd(h, l - N_A, kvp)
            x_new = _ln_res_fwd(x, y, gw, ln_g_f[l, i][None], ln_b_f[l, i][None])
            tape.append((x, h, y, gw, scl, saved))
            x = x_new
        if l == N_A - 1:
            hkv = _modulate(x, mkv[1], mkv[0])
            w_kvg, l_kv = weight("w_kv", 0)
            n = w_kvg.shape[-1]
            kv = _matmul("kv_fwd", _as4(hkv), w_kvg, (N_DEV, 1, t, n), BF16, lb=l_kv)
            kvp = jnp.pad(shard_act(kv), ((0, 0), (LEFT_PAD, 0), (0, 0)))
            kv_tape = (x, hkv)

    loss_part, dx = _loss_head(x, target)
    loss = lax.psum(loss_part[0, 0], MESH_AXES)

    d_mod = [[None] * n_mod for _ in range(DEPTH)]
    d_ln_g = [[None] * 3 for _ in range(DEPTH)]
    d_ln_b = [[None] * 3 for _ in range(DEPTH)]
    small_grads = {k: [None] * N_A for k in ("gmlp_w_s", "gmlp_b_s", "gmlp_ln_g", "gmlp_ln_b", "gmlp_b_in")}
    d_rel = [None] * n_b
    dkv_acc = ()
    d_mkv = None
    reductions = [None] * DEPTH
    sent_token = None
    for l in reversed(range(DEPTH)):
        if l == N_A - 1:
            x_kv, hkv = kv_tape
            w_kvg, l_kv = weight("w_kv", 0)
            dkv = jnp.concatenate(dkv_acc)[:, LEFT_PAD:, :].astype(BF16)[:, None]
            dhkv = _matmul("kv_bwd_a", dkv, w_kvg, (1, 1, t, d), F32, lb=l_kv, tb=True, reduce=True)[0, 0]
            grad_into("w_kv", 0, lambda buf, lo: _matmul(
                "kv_bwd_w", _as4(hkv), dkv, buf.shape, BF16, ta=True, lo=lo, out_buf=buf))
            dx, ds_kv, db_kv = _mod_bwd(dx, dhkv, x_kv, mkv[1])
            d_mkv = jnp.concatenate([db_kv[0], ds_kv[0]])
        for i in reversed(range(3)):
            x_in, h, y, gw, scl, saved = tape[3 * l + i]
            wgt = 1.0 if i == 1 else 0.5
            if sent_token is not None:
                gw = gw + sent_token
                sent_token = None
            dx_res, dy, dgw, dg, db = _ln_res_bwd(x_in, y, gw, ln_g_f[l, i][None], dx)
            d_ln_g[l][i], d_ln_b[l][i] = dg[0], db[0]
            if i != 1:
                dh, extra = ffn_bwd(dy, h, saved, 2 * l + i // 2)
            elif l < N_A:
                dh, extra = gmlp_bwd(dy, h, saved, l)
                for k, g in extra.items():
                    small_grads[k][l] = g
            else:
                dh, extra = attn_bwd(dy, h, saved, l - N_A, kvp, dkv_acc)
                d_rel[l - N_A] = extra["attn_rel_bias"]
                dkv_acc = extra["dkv"]
            dx, dscl, dshift = _mod_bwd(dx_res, dh, x_in, scl)
            d_mod[l][3 * i], d_mod[l][3 * i + 1], d_mod[l][3 * i + 2] = dshift[0], dscl[0], wgt * dgw[0]
        names = list(bwd_groups[l])
        from_sibling = _partials_d2d([partial[l][k] for k in names])
        sums = [_chip_sum(partial[l][k], r1, core) for k, r1 in zip(names, from_sibling)]
        lands = [lax.empty((len(_ALL_CHIPS) - 1,) + s.shape[1:], BF16) for s in sums]
        reductions[l] = _partials_ici_start(f"partials_ici_start_{l}", sums, lands)
        sent_token = reductions[l][4][0, 0]
    grad_x = dx[None]

    d_mod_arr = jnp.stack([jnp.concatenate(r) for r in d_mod])
    small_part = dict(
        b_ada=d_mod_arr + sent_token, b_ada_kv=d_mkv,
        ln_g=jnp.stack([jnp.stack(r) for r in d_ln_g]), ln_b=jnp.stack([jnp.stack(r) for r in d_ln_b]),
        gmlp_b_in=jnp.stack(small_grads["gmlp_b_in"]), gmlp_ln_g=jnp.stack(small_grads["gmlp_ln_g"]),
        gmlp_ln_b=jnp.stack(small_grads["gmlp_ln_b"]), gmlp_w_s=jnp.stack(small_grads["gmlp_w_s"]),
        gmlp_b_s=jnp.stack(small_grads["gmlp_b_s"]), attn_rel_bias=jnp.stack(d_rel))
    small_names = list(small_part)
    sp_shapes = [small_part[k].shape for k in small_names]
    sp_all = _all_gather_small(_pack([small_part[k] for k in small_names]))
    sp_sum = _sum_parts(sp_all)
    full_grads = dict(zip(small_names, _unpack(sp_sum, sp_shapes)))
    per_dev = dict(zip(small_names, _unpack(sp_all, sp_shapes, lead=(N_DEV,))))

    def my_cols(a, width):
        return lax.dynamic_slice_in_dim(a, me * width, width, axis=a.ndim - 1)

    grads = {}
    grads["b_ada"] = full_grads["b_ada"]
    grads["b_ada_kv"] = full_grads["b_ada_kv"]
    grads["gmlp_w_s"] = full_grads["gmlp_w_s"]
    grads["gmlp_b_s"] = full_grads["gmlp_b_s"]
    for k in ("ln_g", "ln_b", "gmlp_b_in", "gmlp_ln_g", "gmlp_ln_b", "attn_rel_bias"):
        grads[k] = my_cols(full_grads[k], weights[k].shape[-1])

    dmod_cols = jnp.transpose(my_cols(per_dev["b_ada"], mod_w), (1, 0, 2))[:, None]
    grads["w_ada"] = _matmul("ada_bwd_w", c4, dmod_cols, (DEPTH, 1, d, mod_w), F32, ta=True,
                             a_silu=True)[:, 0]
    dkv_cols = my_cols(per_dev["b_ada_kv"], kv_w)[None, None]
    grads["w_ada_kv"] = _matmul("ada_kv_bwd_w", c4, dkv_cols, (1, 1, d, kv_w), F32, ta=True,
                                a_silu=True)[0, 0]

    delta, new_m, new_v = {}, {}, {}
    first = jnp.zeros((1,), jnp.int32)

    def flat2(a, cols):
        return a.reshape(-1, cols)

    for k in ("w_ada", "w_ada_kv"):
        w = weights[k]
        cols = w.shape[-1]
        res = _adamw(grads[k].reshape(1, -1, cols), first, None, flat2(w, cols), flat2(mom1[k], cols),
                     flat2(mom2[k], cols))
        grads[k], delta[k], new_m[k], new_v[k] = (a.reshape(w.shape) for a in res)

    tiny = [k for k in order if k not in delta and k not in big_names]
    tiny_shapes = [weights[k].shape for k in tiny]
    tiny_out = _adamw(_pack([grads[k] for k in tiny])[None], first, None, _pack([weights[k] for k in tiny]),
                      _pack([mom1[k] for k in tiny]), _pack([mom2[k] for k in tiny]))
    for dst, arr in zip((grads, delta, new_m, new_v), tiny_out):
        for k, val in zip(tiny, _unpack(arr, tiny_shapes)):
            dst[k] = val

    bufs = {k: [lax.empty(flat2(weights[k], weights[k].shape[-1]).shape, F32) for _ in range(4)]
            for k in big_names}
    done = tiny_out[0]
    for l in reversed(range(DEPTH)):
        send_sems, recv_sems, sums, lands, _ = reductions[l]
        sums, lands = _partials_ici_wait(f"partials_ici_wait_{l}", sums, lands, send_sems, recv_sems, done)
        for k, own, got in zip(bwd_groups[l], sums, lands):
            cols = weights[k].shape[-1]
            slot_rows = int(np.prod(big[k].shape[1:-1]))
            bufs[k] = _adamw(own.reshape(own.shape[0], -1, cols), chip, got.reshape(got.shape[0], -1, cols),
                             flat2(weights[k], cols), flat2(mom1[k], cols), flat2(mom2[k], cols),
                             row0=bwd_groups[l][k][0] * slot_rows, bufs=bufs[k])
            done = bufs[k][0][:SUBLANES, :LANES]
    for k in big_names:
        grads[k], delta[k], new_m[k], new_v[k] = (b.reshape(weights[k].shape) for b in bufs[k])

    return (loss, grad_x, *[grads[k] for k in order], *[delta[k] for k in order],
            *[new_m[k] for k in order], *[new_v[k] for k in order])
```

```python
import functools

import numpy as np
import jax
import jax.numpy as jnp
from jax import lax
from jax.experimental import pallas as pl
from jax.experimental.pallas import tpu as pltpu

F32 = jnp.float32
BF16 = jnp.bfloat16
MESH_AXES = ("x", "y", "c")
N_DEV = 8
MESH_ID = pl.DeviceIdType.MESH

DEPTH = 4
N_A = 2
CHUNK = 64
N_HEADS = 16
LEFT_CHUNKS = 8
BAND = (LEFT_CHUNKS + 1) * CHUNK
LEFT_PAD = LEFT_CHUNKS * CHUNK
MAX_REL = 4 * CHUNK
N_REL = (CHUNK - 1) + MAX_REL + 1
GMLP_WINDOW = 128
GMLP_GROUPS = 8
ALPHA = (2.0 * DEPTH) ** 0.25
LN_EPS = 1e-5
ADAM_LR = 0.001
ADAM_B1 = 0.9
ADAM_B2 = 0.999
ADAM_EPS = 1e-08
ADAM_WD = 0.01
ADAM_STEP = 10

V7X_VMEM_BYTES = 64 * 1024 * 1024
VMEM_LIMIT = V7X_VMEM_BYTES - 8 * 1024 * 1024
LANES = 128
SUBLANES = 8
MM_BLOCK = 2048
BIG_ROW_BLOCK = 1024
ROW_BLOCK = 512
OPT_ROW_BLOCK = 256

_ANY = pl.BlockSpec(memory_space=pl.ANY)
_VMEM = pl.BlockSpec(memory_space=pltpu.VMEM)


def _params(sem=None):
    return pltpu.CompilerParams(dimension_semantics=sem, vmem_limit_bytes=VMEM_LIMIT)


def _row_block(rows, target):
    for d in range(min(rows, target), 0, -1):
        if rows % d == 0 and (d % SUBLANES == 0 or d == rows):
            return d
    return rows


def _matmul(name, a, b, out_shape4, out_dtype, *, la=0, lb=0, lo=0, ta=False, tb=False,
            reduce=False, b_merge=1, out_merge=1, out_buf=None, a_silu=False):
    ja_n, _, a_r, a_c = a.shape
    jb_n, _, b_r, b_c = b.shape
    jo_n, _, o_r, o_c = out_shape4
    m_tot = a_c if ta else a_r
    k_a = a_r if ta else a_c
    b_rows = b_merge * b_r
    k_c = b_c if tb else b_rows
    n = b_rows if tb else b_c
    n_chunks = (jb_n // b_merge) if reduce else 1
    natural_k = reduce and ja_n == 1
    assert n == o_c, (name, n, o_c)
    assert k_a ==(k_c * n_chunks if natural_k else k_c), (name, k_a, k_c, n_chunks)
    bk = k_c if (k_c <= MM_BLOCK or (b_merge > 1 and not tb)) else MM_BLOCK
    assert k_c % bk == 0
    nkk = k_c // bk
    nk = n_chunks * nkk
    m_out = out_merge * o_r
    assert m_tot == m_out, (name, m_tot, m_out)
    bm = m_tot if (m_tot <= MM_BLOCK or out_merge > 1) else MM_BLOCK
    assert m_tot % bm == 0
    jo_blocks = jo_n // out_merge

    def a_index(j, m, k):
        kj, kk = k // nkk, k % nkk
        ja = 0 if ja_n == 1 else (kj if reduce else j)
        ke = kk + kj * nkk if natural_k else kk
        return (ja, la, ke, m) if ta else (ja, la, m, ke)

    def b_index(j, m, k):
        kj, kk = k // nkk, k % nkk
        jb = 0 if jb_n == b_merge else (kj if reduce else j)
        return (jb, lb, 0, kk) if tb else (jb, lb, kk, 0)

    def o_index(j, m, k):
        return (j, lo, 0, 0) if out_merge > 1 else (j, lo, m, 0)

    a_block = (None, None, bk, bm) if ta else (None, None, bm, bk)
    if b_merge > 1:
        b_block = (b_merge, None, b_r, bk if tb else n)
    else:
        b_block = (None, None, n, bk) if tb else (None, None, bk, n)
    o_block = (out_merge, None, o_r, n) if out_merge > 1 else (None, None, bm, n)
    dims = (((0 if ta else 1,), (1 if tb else 0,)), ((), ()))

    in_place = nk > 1 and out_dtype == F32 and out_merge == 1
    use_acc = nk > 1 and not in_place

    def body(a_ref, b_ref, *rest):
        o_ref = rest[-2] if use_acc else rest[-1]
        k = pl.program_id(2)
        av = a_ref[...]
        if a_silu:
            af = av.astype(F32)
            av = af * jax.nn.sigmoid(af)
        bv = b_ref[...]
        if b_merge > 1:
            bv = bv.reshape(b_rows, bv.shape[-1])
        prod = lax.dot_general(av.astype(BF16), bv.astype(BF16), dims, preferred_element_type=F32)

        def emit(val):
            val = val.astype(out_dtype)
            o_ref[...] = val.reshape(out_merge, o_r, n) if out_merge > 1 else val

        if nk == 1:
            emit(prod)
            return
        acc_ref = o_ref if in_place else rest[-1]

        @pl.when(k == 0)
        def _():
            acc_ref[...] = prod

        @pl.when(k > 0)
        def _():
            acc_ref[...] += prod

        if use_acc:
            @pl.when(k == nk - 1)
            def _():
                emit(acc_ref[...])

    in_specs = [pl.BlockSpec(a_block, a_index), pl.BlockSpec(b_block, b_index)]
    operands = [a, b]
    aliases = {}
    if out_buf is not None:
        assert out_buf.shape == tuple(out_shape4) and out_buf.dtype == out_dtype
        in_specs.append(_ANY)
        operands.append(out_buf)
        aliases = {2: 0}
    return pl.pallas_call(
        body, name=name,
        grid=(jo_blocks, m_tot // bm, nk),
        in_specs=in_specs,
        out_specs=pl.BlockSpec(o_block, o_index),
        out_shape=jax.ShapeDtypeStruct(tuple(out_shape4), out_dtype),
        scratch_shapes=[pltpu.VMEM((bm, n), F32)] if use_acc else [],
        input_output_aliases=aliases,
        compiler_params=_params(("parallel", "parallel", "arbitrary")),
    )(*operands)


def _as4(a):
    return a.reshape((1,) * (4 - a.ndim) + a.shape)


def _row_call(name, body, ins, outs, t, *, acc_outs=()):
    bt = _row_block(t, ROW_BLOCK)

    def spec(arr, tiled):
        if tiled:
            return pl.BlockSpec((bt,) + tuple(arr.shape[1:]), lambda i: (i,) + (0,) * (arr.ndim - 1))
        return pl.BlockSpec(tuple(arr.shape), lambda i: (0,) * arr.ndim)

    return pl.pallas_call(
        body, name=name, grid=(t // bt,),
        in_specs=[spec(a, tl) for a, tl in ins],
        out_specs=[spec(o, tl) for o, tl in outs],
        out_shape=[jax.ShapeDtypeStruct(o.shape, o.dtype) for o, _ in outs],
        compiler_params=_params(("arbitrary",) if acc_outs else ("parallel",)),
    )(*[a for a, _ in ins])


def _sds(shape, dtype):
    return jax.ShapeDtypeStruct(tuple(shape), dtype)


def _modulate(x, scl, shift):
    t, d = x.shape

    def body(x_ref, s_ref, b_ref, h_ref):
        h_ref[...] = (x_ref[...] * (1.0 + s_ref[...]) + b_ref[...]).astype(BF16)

    return _row_call("modulate", body, [(x, True), (scl, False), (shift, False)],
                     [(_sds((t, d), BF16), True)], t)[0]


def _ln_stats(r):
    mu = jnp.mean(r, axis=-1, keepdims=True)
    rc = r - mu
    var = jnp.mean(rc * rc, axis=-1, keepdims=True)
    rstd = lax.rsqrt(var + LN_EPS)
    return rc * rstd, rstd


def _ln_res_fwd(x, y, gw, g, b):
    t, d = x.shape

    def body(x_ref, y_ref, gw_ref, g_ref, b_ref, o_ref):
        r = ALPHA * x_ref[...] + gw_ref[...] * y_ref[...]
        xhat, _ = _ln_stats(r)
        o_ref[...] = xhat * g_ref[...] + b_ref[...]

    return _row_call("ln_res_fwd", body,
                     [(x, True), (y, True), (gw, False), (g, False), (b, False)],
                     [(_sds((t, d), F32), True)], t)[0]


def _ln_res_bwd(x, y, gw, g, dxn):
    t, d = x.shape

    def body(x_ref, y_ref, gw_ref, g_ref, dxn_ref, dx_ref, dy_ref, dgw_ref, dg_ref, db_ref):
        @pl.when(pl.program_id(0) == 0)
        def _():
            dgw_ref[...] = jnp.zeros_like(dgw_ref)
            dg_ref[...] = jnp.zeros_like(dg_ref)
            db_ref[...] = jnp.zeros_like(db_ref)

        yv = y_ref[...]
        gwv = gw_ref[...]
        dxn = dxn_ref[...]
        xhat, rstd = _ln_stats(ALPHA * x_ref[...] + gwv * yv)
        dxh = dxn * g_ref[...]
        m1 = jnp.mean(dxh, axis=-1, keepdims=True)
        m2 = jnp.mean(dxh * xhat, axis=-1, keepdims=True)
        dr = rstd * (dxh - m1 - xhat * m2)
        dx_ref[...] = ALPHA * dr
        dy_ref[...] = (gwv * dr).astype(BF16)
        dgw_ref[...] += jnp.sum(dr * yv, axis=0, keepdims=True)
        dg_ref[...] += jnp.sum(dxn * xhat, axis=0, keepdims=True)
        db_ref[...] += jnp.sum(dxn, axis=0, keepdims=True)

    vec = _sds((1, d), F32)
    return _row_call("ln_res_bwd", body,
                     [(x, True), (y, True), (gw, False), (g, False), (dxn, True)],
                     [(_sds((t, d), F32), True), (_sds((t, d), BF16), True),
                      (vec, False), (vec, False), (vec, False)], t, acc_outs=(2, 3, 4))


def _mod_bwd(dx_res, dh, x, scl):
    t, d = x.shape

    def body(dxr_ref, dh_ref, x_ref, s_ref, dx_ref, ds_ref, db_ref):
        @pl.when(pl.program_id(0) == 0)
        def _():
            ds_ref[...] = jnp.zeros_like(ds_ref)
            db_ref[...] = jnp.zeros_like(db_ref)

        dh = dh_ref[...]
        dx_ref[...] = dxr_ref[...] + dh * (1.0 + s_ref[...])
        ds_ref[...] += jnp.sum(dh * x_ref[...], axis=0, keepdims=True)
        db_ref[...] += jnp.sum(dh, axis=0, keepdims=True)

    vec = _sds((1, d), F32)
    return _row_call("mod_bwd", body, [(dx_res, True), (dh, True), (x, True), (scl, False)],
                     [(_sds((t, d), F32), True), (vec, False), (vec, False)], t, acc_outs=(1, 2))


def _loss_head(y, target):
    t, d = y.shape

    def body(y_ref, t_ref, l_ref, dy_ref):
        @pl.when(pl.program_id(0) == 0)
        def _():
            l_ref[...] = jnp.zeros_like(l_ref)

        err = y_ref[...] - t_ref[...]
        dy_ref[...] = err * (1.0 / d)
        part = 0.5 * jnp.sum(jnp.mean(err * err, axis=-1, keepdims=True), axis=0, keepdims=True)
        l_ref[...] += jnp.broadcast_to(part, l_ref.shape)

    return _row_call("loss_head", body, [(y, True), (target, True)],
                     [(_sds((SUBLANES, LANES), F32), False), (_sds((t, d), F32), True)], t,
                     acc_outs=(0,))


def _swiglu_fwd(gu):
    _, t, n = gu.shape
    half = N_DEV // 2
    bt = _row_block(t, BIG_ROW_BLOCK)
    gu4 = gu.reshape(2, half, t, n)

    def body(gu_ref, a_ref):
        g = gu_ref[0]
        a_ref[...] = (g * jax.nn.sigmoid(g) * gu_ref[1]).astype(BF16)

    return pl.pallas_call(
        body, name="swiglu_fwd", grid=(half, t // bt),
        in_specs=[pl.BlockSpec((2, None, bt, n), lambda j, i: (0, j, i, 0))],
        out_specs=pl.BlockSpec((None, bt, n), lambda j, i: (j, i, 0)),
        out_shape=_sds((half, t, n), BF16),
        compiler_params=_params(("parallel", "parallel")),
    )(gu4)


def _swiglu_bwd(gu, da):
    _, t, n = gu.shape
    half = N_DEV // 2
    bt = _row_block(t, BIG_ROW_BLOCK)
    gu4 = gu.reshape(2, half, t, n)

    def body(gu_ref, da_ref, d_ref):
        g = gu_ref[0]
        u = gu_ref[1]
        da = da_ref[...]
        sig = jax.nn.sigmoid(g)
        d_ref[0] = (da * u * sig * (1.0 + g * (1.0 - sig))).astype(BF16)
        d_ref[1] = (da * g * sig).astype(BF16)

    out = pl.pallas_call(
        body, name="swiglu_bwd", grid=(half, t // bt),
        in_specs=[pl.BlockSpec((2, None, bt, n), lambda j, i: (0, j, i, 0)),
                  pl.BlockSpec((None, bt, n), lambda j, i: (j, i, 0))],
        out_specs=pl.BlockSpec((2, None, bt, n), lambda j, i: (0, j, i, 0)),
        out_shape=_sds((2, half, t, n), BF16),
        compiler_params=_params(("parallel", "parallel")),
    )(gu4, da)
    return out.reshape(N_DEV, t, n)


_INV_SQRT2 = 0.7071067811865476
_INV_SQRT_2PI = 0.3989422804014327


def _gelu(z):
    return 0.5 * z * (1.0 + lax.erf(z * _INV_SQRT2))


def _gelu_grad(z):
    return 0.5 * (1.0 + lax.erf(z * _INV_SQRT2)) + z * jnp.exp(-0.5 * z * z) * _INV_SQRT_2PI


def _window_mask():
    t_out = lax.broadcasted_iota(jnp.int32, (GMLP_WINDOW, GMLP_WINDOW), 0)
    s_in = lax.broadcasted_iota(jnp.int32, (GMLP_WINDOW, GMLP_WINDOW), 1)
    return (s_in // CHUNK) <= (t_out // CHUNK)


def _gmlp_recompute(z_ref, bin_ref, lng_ref, lnb_ref):
    half = N_DEV // 2
    z = z_ref[...] + bin_ref[...]
    ge = _gelu(z)
    u = ge[:half]
    v = ge[half:]
    width = half * v.shape[-1]
    mu = jnp.sum(jnp.sum(v, axis=0), axis=-1, keepdims=True) / width
    vc = v - mu
    var = jnp.sum(jnp.sum(vc * vc, axis=0), axis=-1, keepdims=True) / width
    rstd = lax.rsqrt(var + LN_EPS)
    xhat = vc * rstd
    vn = xhat * lng_ref[...] + lnb_ref[...]
    return z, u, xhat, rstd, vn


def _gmlp_mid_fwd(zpre, b_in, ln_g, ln_b, w_s, b_s):
    _, t, n = zpre.shape
    half = N_DEV // 2
    gd = half * n // GMLP_GROUPS
    per = n // gd
    w = GMLP_WINDOW

    def body(z_ref, bin_ref, lng_ref, lnb_ref, ws_ref, bs_ref, o_ref):
        _, u, _, _, vn = _gmlp_recompute(z_ref, bin_ref, lng_ref, lnb_ref)
        mask = _window_mask()
        for g in range(GMLP_GROUPS):
            sh, c0 = g // per, (g % per) * gd
            wsm = jnp.where(mask, ws_ref[g], 0.0).astype(BF16)
            s = jnp.dot(wsm, vn[sh][:, c0:c0 + gd].astype(BF16), preferred_element_type=F32) + bs_ref[g]
            o_ref[sh, :, c0:c0 + gd] = (u[sh][:, c0:c0 + gd] * s).astype(BF16)

    whole = lambda a: pl.BlockSpec(tuple(a.shape), lambda i: (0,) * a.ndim)
    return pl.pallas_call(
        body, name="gmlp_mid_fwd", grid=(t // w,),
        in_specs=[pl.BlockSpec((N_DEV, w, n), lambda i: (0, i, 0)),
                  whole(b_in), whole(ln_g), whole(ln_b), whole(w_s), whole(b_s)],
        out_specs=pl.BlockSpec((half, w, n), lambda i: (0, i, 0)),
        out_shape=_sds((half, t, n), BF16),
        compiler_params=_params(("parallel",)),
    )(zpre, b_in, ln_g, ln_b, w_s, b_s)


def _gmlp_mid_bwd(zpre, dgated, b_in, ln_g, ln_b, w_s, b_s):
    _, t, n = zpre.shape
    half = N_DEV // 2
    gd = half * n // GMLP_GROUPS
    per = n // gd
    w = GMLP_WINDOW
    width = half * n

    def body(z_ref, dg_ref, bin_ref, lng_ref, lnb_ref, ws_ref, bs_ref,
             dz_ref, dws_ref, dbs_ref, dlng_ref, dlnb_ref, dbin_ref, du_ref, dvn_ref):
        @pl.when(pl.program_id(0) == 0)
        def _():
            for r in (dws_ref, dbs_ref, dlng_ref, dlnb_ref, dbin_ref):
                r[...] = jnp.zeros_like(r)

        z, u, xhat, rstd, vn = _gmlp_recompute(z_ref, bin_ref, lng_ref, lnb_ref)
        mask = _window_mask()
        for g in range(GMLP_GROUPS):
            sh, c0 = g // per, (g % per) * gd
            wsm = jnp.where(mask, ws_ref[g], 0.0).astype(BF16)
            vg = vn[sh][:, c0:c0 + gd].astype(BF16)
            s = jnp.dot(wsm, vg, preferred_element_type=F32) + bs_ref[g]
            dgt = dg_ref[sh, :, c0:c0 + gd]
            ds = dgt * u[sh][:, c0:c0 + gd]
            du_ref[sh, :, c0:c0 + gd] = dgt * s
            dsb = ds.astype(BF16)
            dws = lax.dot_general(dsb, vg, (((1,), (1,)), ((), ())), preferred_element_type=F32)
            dws_ref[g] += jnp.where(mask, dws, 0.0)
            dbs_ref[g] += jnp.sum(ds, axis=-1, keepdims=True)
            dvn_ref[sh, :, c0:c0 + gd] = lax.dot_general(wsm, dsb, (((0,), (0,)), ((), ())),
                                                         preferred_element_type=F32)
        dvn = dvn_ref[...]
        dlng_ref[...] += jnp.sum(dvn * xhat, axis=1, keepdims=True)
        dlnb_ref[...] += jnp.sum(dvn, axis=1, keepdims=True)
        dxh = dvn * lng_ref[...]
        m1 = jnp.sum(jnp.sum(dxh, axis=0), axis=-1, keepdims=True) / width
        m2 = jnp.sum(jnp.sum(dxh * xhat, axis=0), axis=-1, keepdims=True) / width
        dv = rstd * (dxh - m1 - xhat * m2)
        gg = _gelu_grad(z)
        dzu = du_ref[...] * gg[:half]
        dzv = dv * gg[half:]
        dz_ref[:half] = dzu.astype(BF16)
        dz_ref[half:] = dzv.astype(BF16)
        dbin_ref[:half] += jnp.sum(dzu, axis=1, keepdims=True)
        dbin_ref[half:] += jnp.sum(dzv, axis=1, keepdims=True)

    whole = lambda a: pl.BlockSpec(tuple(a.shape), lambda i: (0,) * a.ndim)
    outs = [_sds((N_DEV, t, n), BF16), _sds(w_s.shape, F32), _sds(b_s.shape, F32),
            _sds(ln_g.shape, F32), _sds(ln_b.shape, F32), _sds(b_in.shape, F32)]
    return pl.pallas_call(
        body, name="gmlp_mid_bwd", grid=(t // w,),
        in_specs=[pl.BlockSpec((N_DEV, w, n), lambda i: (0, i, 0)),
                  pl.BlockSpec((half, w, n), lambda i: (0, i, 0)),
                  whole(b_in), whole(ln_g), whole(ln_b), whole(w_s), whole(b_s)],
        out_specs=[pl.BlockSpec((N_DEV, w, n), lambda i: (0, i, 0))] + [whole(o) for o in outs[1:]],
        out_shape=outs,
        scratch_shapes=[pltpu.VMEM((half, w, n), F32), pltpu.VMEM((half, w, n), F32)],
        compiler_params=_params(("arbitrary",)),
    )(zpre, dgated, b_in, ln_g, ln_b, w_s, b_s)


ATTN_CHUNKS = 4
ATTN_ROWS = ATTN_CHUNKS * CHUNK
ATTN_WINDOW = ATTN_ROWS + LEFT_PAD
ATTN_DIAGS = 1024
ATTN_ROLL = ATTN_DIAGS - (ATTN_ROWS - 1)


def _rel_vector(rel):
    j = np.arange(ATTN_DIAGS)
    idx = np.clip(ATTN_WINDOW - 1 - j, -(CHUNK - 1), MAX_REL) + (CHUNK - 1)
    return rel[:, idx]


def _attn_bias_mask(rel_ref, bm_ref):
    tt = lax.broadcasted_iota(jnp.int32, (ATTN_ROWS, ATTN_WINDOW), 0) // CHUNK
    rr = lax.broadcasted_iota(jnp.int32, (ATTN_ROWS, ATTN_WINDOW), 1) // CHUNK
    band = (rr >= tt) & (rr <= tt + LEFT_CHUNKS)
    for j in range(bm_ref.shape[0]):
        vec = jnp.broadcast_to(rel_ref[j:j + 1, :], (ATTN_ROWS, ATTN_DIAGS))
        toeplitz = pltpu.roll(vec, ATTN_ROLL, 1, stride=1, stride_axis=0)[:, :ATTN_WINDOW]
        bm_ref[j] = jnp.where(band, toeplitz, -jnp.inf)


def _attn_probs(q_ref, k_ref, bm_ref, j, hd, start, valid):
    qh = q_ref[:, j * hd:(j + 1) * hd]
    kb = k_ref[pl.ds(start, ATTN_WINDOW), j * hd:(j + 1) * hd]
    sc = lax.dot_general(qh, kb, (((1,), (1,)), ((), ())), preferred_element_type=F32)
    sc = sc * (hd ** -0.5) + bm_ref[j]
    sc = jnp.where(valid, sc, -jnp.inf)
    sc = sc - jnp.max(sc, axis=-1, keepdims=True)
    e = jnp.exp(sc)
    return e / jnp.sum(e, axis=-1, keepdims=True), qh, kb


def _window_valid(start):
    r = lax.broadcasted_iota(jnp.int32, (1, ATTN_WINDOW), 1)
    return (start + r) >= LEFT_PAD


def _attn_fwd(q, kvp, rel_vec):
    t, d = q.shape
    hd = d // N_HEADS
    half = N_DEV // 2
    n = kvp.shape[-1]
    per = n // hd
    rows = kvp.shape[1]

    def body(q_ref, k_ref, v_ref, rel_ref, o_ref, bm_ref):
        @pl.when(pl.program_id(1) == 0)
        def _():
            _attn_bias_mask(rel_ref, bm_ref)

        start = pl.multiple_of(pl.program_id(1) * ATTN_ROWS, ATTN_ROWS)
        valid = _window_valid(start)
        for j in range(per):
            p, _, _ = _attn_probs(q_ref, k_ref, bm_ref, j, hd, start, valid)
            vb = v_ref[pl.ds(start, ATTN_WINDOW), j * hd:(j + 1) * hd]
            o_ref[:, j * hd:(j + 1) * hd] = jnp.dot(p.astype(BF16), vb, preferred_element_type=F32).astype(BF16)

    return pl.pallas_call(
        body, name="attn_fwd", grid=(half, t // ATTN_ROWS),
        in_specs=[pl.BlockSpec((ATTN_ROWS, n), lambda g, i: (i, g)),
                  pl.BlockSpec((None, rows, n), lambda g, i: (g, 0, 0)),
                  pl.BlockSpec((None, rows, n), lambda g, i: (half + g, 0, 0)),
                  pl.BlockSpec((None, per, ATTN_DIAGS), lambda g, i: (g, 0, 0))],
        out_specs=pl.BlockSpec((ATTN_ROWS, n), lambda g, i: (i, g)),
        out_shape=_sds((t, d), BF16),
        scratch_shapes=[pltpu.VMEM((per, ATTN_ROWS, ATTN_WINDOW), F32)],
        compiler_params=_params(("arbitrary", "arbitrary")),
    )(q, kvp, kvp, rel_vec.reshape(half, per, ATTN_DIAGS))


def _attn_bwd(q, dout, kvp, rel_vec, dk_in=None, dv_in=None):
    t, d = q.shape
    hd = d // N_HEADS
    half = N_DEV // 2
    n = kvp.shape[-1]
    per = n // hd
    rows = kvp.shape[1]
    scale = hd ** -0.5
    carry = dk_in is not None

    def body(q_ref, do_ref, k_ref, v_ref, rel_ref, *rest):
        dq_ref, dk_ref, dv_ref, dsc_ref, bm_ref = rest[-5:]

        @pl.when(pl.program_id(1) == 0)
        def _():
            _attn_bias_mask(rel_ref, bm_ref)
            dk_ref[...] = rest[0][...] if carry else jnp.zeros_like(dk_ref)
            dv_ref[...] = rest[1][...] if carry else jnp.zeros_like(dv_ref)
            dsc_ref[...] = jnp.zeros_like(dsc_ref)

        start = pl.multiple_of(pl.program_id(1) * ATTN_ROWS, ATTN_ROWS)
        valid = _window_valid(start)
        for j in range(per):
            cols = slice(j * hd, (j + 1) * hd)
            p, qh, kb = _attn_probs(q_ref, k_ref, bm_ref, j, hd, start, valid)
            vb = v_ref[pl.ds(start, ATTN_WINDOW), cols]
            doh = do_ref[:, cols]
            dp = lax.dot_general(doh, vb, (((1,), (1,)), ((), ())), preferred_element_type=F32)
            ds = p * (dp - jnp.sum(dp * p, axis=-1, keepdims=True))
            dsc_ref[j] += sum(ds[a * CHUNK:(a + 1) * CHUNK, a * CHUNK:a * CHUNK + BAND]
                              for a in range(ATTN_CHUNKS))
            dsb = (ds * scale).astype(BF16)
            dq_ref[:, cols] = jnp.dot(dsb, kb, preferred_element_type=F32).astype(BF16)
            dk_ref[pl.ds(start, ATTN_WINDOW), cols] += lax.dot_general(
                dsb, qh, (((0,), (0,)), ((), ())), preferred_element_type=F32)
            dv_ref[pl.ds(start, ATTN_WINDOW), cols] += lax.dot_general(
                p.astype(BF16), doh, (((0,), (0,)), ((), ())), preferred_element_type=F32)

    tile = pl.BlockSpec((ATTN_ROWS, n), lambda g, i: (i, g))
    shard = pl.BlockSpec((None, rows, n), lambda g, i: (g, 0, 0))
    in_specs = [tile, tile, shard, pl.BlockSpec((None, rows, n), lambda g, i: (half + g, 0, 0)),
                pl.BlockSpec((None, per, ATTN_DIAGS), lambda g, i: (g, 0, 0))]
    operands = [q, dout, kvp, kvp, rel_vec.reshape(half, per, ATTN_DIAGS)]
    if carry:
        in_specs += [shard, shard]
        operands += [dk_in, dv_in]
    acc = _sds((half, rows, n), F32)
    return pl.pallas_call(
        body, name="attn_bwd", grid=(half, t // ATTN_ROWS),
        in_specs=in_specs,
        out_specs=[tile, shard, shard, pl.BlockSpec((per, CHUNK, BAND), lambda g, i: (g, 0, 0))],
        out_shape=[_sds((t, d), BF16), acc, acc, _sds((N_HEADS, CHUNK, BAND), F32)],
        scratch_shapes=[pltpu.VMEM((per, ATTN_ROWS, ATTN_WINDOW), F32)],
        compiler_params=_params(("arbitrary", "arbitrary")),
    )(*operands)


SKEW_PITCH = 640
SKEW = SKEW_PITCH + 1
SKEW_LANES = -(-SKEW // LANES) * LANES


def _skew_diagonals(dsc):
    h = dsc.shape[0]
    wide = jnp.pad(dsc, ((0, 0), (0, 0), (0, SKEW_PITCH - BAND))).reshape(h, CHUNK * SKEW_PITCH)
    wide = jnp.pad(wide, ((0, 0), (0, CHUNK))).reshape(h, CHUNK, SKEW)
    return jnp.pad(wide, ((0, 0), (0, 0), (0, SKEW_LANES - SKEW)))


def _rel_bias_grad(skewed):
    heads = skewed.shape[0]
    hb = SUBLANES

    def body(d_ref, o_ref):
        col = lax.broadcasted_iota(jnp.int32, (SKEW_LANES, N_REL), 0)
        bucket = lax.broadcasted_iota(jnp.int32, (SKEW_LANES, N_REL), 1)
        diag = jnp.where(col < BAND, col, col - SKEW)
        idx = jnp.clip(LEFT_PAD - diag, -(CHUNK - 1), MAX_REL) + (CHUNK - 1)
        oh = ((idx == bucket) & (col < SKEW)).astype(BF16)
        dv = jnp.sum(d_ref[...], axis=1)
        hi = dv.astype(BF16)
        rest = dv - hi.astype(F32)
        mid = rest.astype(BF16)
        lo = (rest - mid.astype(F32)).astype(BF16)
        acc = jnp.dot(hi, oh, preferred_element_type=F32)
        acc += jnp.dot(mid, oh, preferred_element_type=F32)
        acc += jnp.dot(lo, oh, preferred_element_type=F32)
        o_ref[...] = acc

    return pl.pallas_call(
        body, name="rel_bias_grad", grid=(heads // hb,),
        in_specs=[pl.BlockSpec((hb, CHUNK, SKEW_LANES), lambda i: (i, 0, 0))],
        out_specs=pl.BlockSpec((hb, N_REL), lambda i: (i, 0)),
        out_shape=_sds((heads, N_REL), F32),
        compiler_params=_params(("parallel",)),
    )(skewed)


def _sum_parts(parts):
    s_n, rows, c = parts.shape
    br = _row_block(rows, OPT_ROW_BLOCK)

    def body(p_ref, o_ref):
        acc = p_ref[0].astype(F32)
        for s in range(1, s_n):
            acc = acc + p_ref[s].astype(F32)
        o_ref[...] = acc

    return pl.pallas_call(
        body, name="sum_parts", grid=(rows // br,),
        in_specs=[pl.BlockSpec((s_n, br, c), lambda i: (0, i, 0))],
        out_specs=pl.BlockSpec((br, c), lambda i: (i, 0)),
        out_shape=_sds((rows, c), F32),
        compiler_params=_params(("parallel",)),
    )(parts)


def _adamw(own, own_idx, parts, w, m, v, row0=0, bufs=None):
    _, rows, c = own.shape
    s_n = 0 if parts is None else parts.shape[0]
    total = w.shape[0]
    br = _row_block(rows, OPT_ROW_BLOCK)
    assert row0 % br == 0 and (bufs is not None or (row0 == 0 and total == rows))
    b0 = row0 // br
    m_corr = 1.0 - ADAM_B1 ** ADAM_STEP
    v_corr = 1.0 - ADAM_B2 ** ADAM_STEP

    def body(idx_ref, own_ref, *refs):
        if s_n:
            p_ref, refs = refs[0], refs[1:]
        w_ref, m_ref, v_ref = refs[:3]
        g_ref, d_ref, nm_ref, nv_ref = refs[-4:]
        g = own_ref[...].astype(F32)
        for s in range(s_n):
            g = g + p_ref[s].astype(F32)
        nm = ADAM_B1 * m_ref[...] + (1.0 - ADAM_B1) * g
        nv = ADAM_B2 * v_ref[...] + (1.0 - ADAM_B2) * (g * g)
        g_ref[...] = g
        nm_ref[...] = nm
        nv_ref[...] = nv
        d_ref[...] = -ADAM_LR * ((nm / m_corr) / (jnp.sqrt(nv / v_corr) + ADAM_EPS) + ADAM_WD * w_ref[...])

    tile = pl.BlockSpec((br, c), lambda i, idx: (i + b0, 0))
    in_specs = [pl.BlockSpec((None, br, c), lambda i, idx: (idx[0], i, 0))]
    operands = [own_idx, own]
    if s_n:
        in_specs.append(pl.BlockSpec((s_n, br, c), lambda i, idx: (0, i, 0)))
        operands.append(parts)
    in_specs += [tile, tile, tile]
    operands += [w, m, v]
    aliases = {}
    if bufs is not None:
        aliases = {len(operands) + j: j for j in range(4)}
        in_specs += [_ANY] * 4
        operands += list(bufs)
    out = _sds((total, c), F32)
    return pl.pallas_call(
        body, name="adamw",
        grid_spec=pltpu.PrefetchScalarGridSpec(
            num_scalar_prefetch=1, grid=(rows // br,), in_specs=in_specs,
            out_specs=[tile, tile, tile, tile]),
        out_shape=[out, out, out, out],
        input_output_aliases=aliases,
        compiler_params=_params(("parallel",)),
    )(*operands)


def _chip_sum(p, r1, core):
    half = N_DEV // 2
    c = p.shape[-1]
    rows = int(np.prod(p.shape[1:-1]))
    br = _row_block(rows, BIG_ROW_BLOCK)

    def body(core_ref, p_ref, r_ref, o_ref):
        o_ref[...] = (p_ref[...].astype(F32) + r_ref[...].astype(F32)).astype(BF16)

    out = pl.pallas_call(
        body, name="chip_sum",
        grid_spec=pltpu.PrefetchScalarGridSpec(
            num_scalar_prefetch=1, grid=(half, rows // br),
            in_specs=[pl.BlockSpec((None, None, br, c), lambda q, i, cr: (q, cr[0], i, 0)),
                      pl.BlockSpec((None, br, c), lambda q, i, cr: (q, i, 0))],
            out_specs=pl.BlockSpec((None, br, c), lambda q, i, cr: (q, i, 0))),
        out_shape=_sds((half, rows, c), BF16),
        compiler_params=_params(("parallel", "parallel")),
    )(core, p.reshape(half, 2, rows, c), r1.reshape(half, rows, c))
    return out.reshape((half,) + p.shape[1:])


def _position():
    return tuple(lax.axis_index(a) for a in MESH_AXES)


def _linear(px, py, pc):
    return 4 * px + 2 * py + pc


def _all_gather_small(v):
    rows, lanes = v.shape

    def body(x_ref, out_ref, send_sems, recv_sems, local_sem):
        x, y, c = _position()
        me, sibling = (x, y, c), (x, y, 1 - c)
        chips = [(1 - x, y), (x, 1 - y), (1 - x, 1 - y)]

        def copy(k, block, to, src=None):
            dst = out_ref.at[_linear(*block)]
            return pltpu.make_async_remote_copy(
                src_ref=dst if src is None else src, dst_ref=dst,
                send_sem=send_sems.at[k], recv_sem=recv_sems.at[k],
                device_id=to, device_id_type=MESH_ID)

        mine = pltpu.make_async_copy(x_ref, out_ref.at[_linear(*me)], local_sem)
        mine.start()
        first = [copy(0, me, sibling, src=x_ref)]
        first += [copy(1 + j, me, (*chip, c), src=x_ref) for j, chip in enumerate(chips)]
        for cp in first:
            cp.start()
        passed = [copy(4 + j, (*chip, c), sibling) for j, chip in enumerate(chips)]
        for j, chip in enumerate(chips):
            copy(1 + j, (*chip, c), me).wait_recv()
            passed[j].start()
        copy(0, sibling, me).wait_recv()
        for j, chip in enumerate(chips):
            copy(4 + j, (*chip, 1 - c), me).wait_recv()
        for cp in first + passed:
            cp.wait_send()
        mine.wait()

    return pl.pallas_call(
        body, name="all_gather_small",
        out_shape=_sds((N_DEV, rows, lanes), v.dtype),
        in_specs=[_VMEM], out_specs=_VMEM,
        scratch_shapes=[pltpu.SemaphoreType.DMA((7,)), pltpu.SemaphoreType.DMA((7,)),
                        pltpu.SemaphoreType.DMA],
        compiler_params=pltpu.CompilerParams(vmem_limit_bytes=VMEM_LIMIT),
    )(v)


_HBM = pl.BlockSpec(memory_space=pltpu.HBM)
_SEM = pl.BlockSpec(memory_space=pltpu.SEMAPHORE)
_EFFECT = pltpu.SideEffectType.DATAFLOW_SIDE_EFFECTING
_ALL_CHIPS = [(0, 0), (0, 1), (1, 0), (1, 1)]


def _other_chips(x, y):
    return [(1 - x, y), (x, 1 - y), (1 - x, 1 - y)]


def _in_hbm(a):
    return pltpu.with_memory_space_constraint(a, pltpu.HBM)


def _token():
    return _sds((SUBLANES, LANES), F32)


def _gather_ici_copy(ref, i, k, chip, c, block, send_sems, recv_sems):
    return pltpu.make_async_remote_copy(
        src_ref=ref.at[block], dst_ref=ref.at[block],
        send_sem=send_sems.at[3 * i + k], recv_sem=recv_sems.at[3 * i + k],
        device_id=(*chip, c), device_id_type=MESH_ID)


def _gather_ici_start(name, lands):
    n = len(lands)

    def body(*refs):
        ins, send_sems, recv_sems, token = refs[:n], refs[n], refs[n + 1], refs[-1]
        x, y, c = _position()
        me = _linear(x, y, c)
        for i in range(n):
            for k, chip in enumerate(_other_chips(x, y)):
                _gather_ici_copy(ins[i], i, k, chip, c, me, send_sems, recv_sems).start()
        token[...] = jnp.zeros_like(token)

    out = pl.pallas_call(
        body, name=name,
        out_shape=(pltpu.SemaphoreType.DMA((3 * n,)), pltpu.SemaphoreType.DMA((3 * n,)),
                   *[pltpu.HBM(a.shape, a.dtype) for a in lands], _token()),
        in_specs=[_HBM] * n, out_specs=(_SEM, _SEM, *[_HBM] * n, _VMEM),
        input_output_aliases={i: 2 + i for i in range(n)},
        compiler_params=pltpu.CompilerParams(has_side_effects=_EFFECT),
    )(*[_in_hbm(a) for a in lands])
    return out[0], out[1], list(out[2:2 + n]), out[-1]


def _gather_ici_wait(name, lands, send_sems, recv_sems, after):
    n = len(lands)

    def body(*refs):
        ins, ss, rs = refs[:n], refs[n], refs[n + 1]
        x, y, c = _position()
        me = _linear(x, y, c)
        for i in range(n):
            for k, chip in enumerate(_other_chips(x, y)):
                _gather_ici_copy(ins[i], i, k, chip, c, me, ss, rs).wait_send()
                _gather_ici_copy(ins[i], i, k, chip, c, _linear(*chip, c), ss, rs).wait_recv()

    out = pl.pallas_call(
        body, name=name,
        out_shape=[pltpu.HBM(a.shape, a.dtype) for a in lands],
        in_specs=[_HBM] * n + [_SEM, _SEM, _ANY], out_specs=[_HBM] * n,
        input_output_aliases={i: i for i in range(n)},
        compiler_params=pltpu.CompilerParams(has_side_effects=_EFFECT),
    )(*lands, send_sems, recv_sems, after)
    return list(out)


def _gather_d2d(lands):
    n = len(lands)

    def body(*refs):
        ins, outs, send_sems, recv_sems = refs[:n], refs[n:2 * n], refs[2 * n], refs[2 * n + 1]
        x, y, c = _position()

        def copy(i, q, core):
            block = _linear(*_ALL_CHIPS[q], core)
            return pltpu.make_async_remote_copy(
                src_ref=ins[i].at[block], dst_ref=outs[i].at[block],
                send_sem=send_sems.at[i, q], recv_sem=recv_sems.at[i, q],
                device_id=(x, y, 1 - c), device_id_type=MESH_ID)

        sent = [copy(i, q, c) for i in range(n) for q in range(len(_ALL_CHIPS))]
        for cp in sent:
            cp.start()
        for i in range(n):
            for q in range(len(_ALL_CHIPS)):
                copy(i, q, 1 - c).wait_recv()
        for cp in sent:
            cp.wait_send()

    return pl.pallas_call(
        body, name="gather_d2d",
        out_shape=[_sds(a.shape, a.dtype) for a in lands],
        in_specs=[_ANY] * n, out_specs=[_ANY] * n,
        input_output_aliases={i: i for i in range(n)},
        scratch_shapes=[pltpu.SemaphoreType.DMA((n, 4)), pltpu.SemaphoreType.DMA((n, 4))],
    )(*lands)


def _partials_d2d(parts):
    n = len(parts)
    half = N_DEV // 2

    def body(*refs):
        ins, outs, send_sems, recv_sems = refs[:n], refs[n:2 * n], refs[2 * n], refs[2 * n + 1]
        x, y, c = _position()

        def copy(i, q):
            return pltpu.make_async_remote_copy(
                src_ref=ins[i].at[_linear(*_ALL_CHIPS[q], 1 - c)], dst_ref=outs[i].at[q],
                send_sem=send_sems.at[i, q], recv_sem=recv_sems.at[i, q],
                device_id=(x, y, 1 - c), device_id_type=MESH_ID)

        sent = [copy(i, q) for i in range(n) for q in range(half)]
        for cp in sent:
            cp.start()
        for cp in sent:
            cp.wait_recv()
        for cp in sent:
            cp.wait_send()

    return pl.pallas_call(
        body, name="partials_d2d",
        out_shape=[_sds((half,) + p.shape[1:], p.dtype) for p in parts],
        in_specs=[_ANY] * n, out_specs=[_ANY] * n,
        scratch_shapes=[pltpu.SemaphoreType.DMA((n, half)), pltpu.SemaphoreType.DMA((n, half))],
    )(*parts)


def _partials_peers(x, y, c, direct):
    chips = _other_chips(x, y)
    if not direct:
        return [((*ch, c), 2 * ch[0] + ch[1]) for ch in chips]
    peers = [(x, y, 1 - c)] + [(*ch, c) for ch in chips] + [(*ch, 1 - c) for ch in chips]
    return [(p, _linear(*p)) for p in peers]


def _partials_copies(srcs, lands, send_sems, recv_sems, direct):
    x, y, c = _position()
    peers = _partials_peers(x, y, c, direct)
    return [pltpu.make_async_remote_copy(
        src_ref=srcs[i].at[block], dst_ref=lands[i].at[k],
        send_sem=send_sems.at[len(peers) * i + k], recv_sem=recv_sems.at[len(peers) * i + k],
        device_id=peer, device_id_type=MESH_ID)
        for i in range(len(srcs)) for k, (peer, block) in enumerate(peers)]


def _partials_send_start(name, srcs, lands, direct, after=None):
    n = len(srcs)
    n_sem = n * (N_DEV - 1 if direct else len(_ALL_CHIPS) - 1)

    def body(*refs):
        _, send_sems, recv_sems = refs[:2 * n], refs[-2 * n - 3], refs[-2 * n - 2]
        for cp in _partials_copies(refs[:n], refs[n:2 * n], send_sems, recv_sems, direct):
            cp.start()
        refs[-1][...] = jnp.zeros_like(refs[-1])

    both = list(srcs) + list(lands)
    extra = [] if after is None else [after]
    out = pl.pallas_call(
        body, name=name,
        out_shape=(pltpu.SemaphoreType.DMA((n_sem,)), pltpu.SemaphoreType.DMA((n_sem,)),
                   *[pltpu.HBM(a.shape, a.dtype) for a in both], _token()),
        in_specs=[_HBM] * (2 * n) + [_ANY] * len(extra), out_specs=(_SEM, _SEM, *[_HBM] * (2 * n), _VMEM),
        input_output_aliases={i: 2 + i for i in range(2 * n)},
        compiler_params=pltpu.CompilerParams(has_side_effects=_EFFECT),
    )(*[_in_hbm(a) for a in both], *extra)
    return out[0], out[1], list(out[2:2 + n]), list(out[2 + n:2 + 2 * n]), out[-1]


def _partials_send_wait(name, srcs, lands, send_sems, recv_sems, direct, after):
    n = len(srcs)

    def body(*refs):
        for cp in _partials_copies(refs[:n], refs[n:2 * n], refs[2 * n], refs[2 * n + 1], direct):
            cp.wait_send()
            cp.wait_recv()

    both = list(srcs) + list(lands)
    out = pl.pallas_call(
        body, name=name,
        out_shape=[pltpu.HBM(a.shape, a.dtype) for a in both],
        in_specs=[_HBM] * (2 * n) + [_SEM, _SEM, _ANY], out_specs=[_HBM] * (2 * n),
        input_output_aliases={i: i for i in range(2 * n)},
        compiler_params=pltpu.CompilerParams(has_side_effects=_EFFECT),
    )(*both, send_sems, recv_sems, after)
    return list(out[:n]), list(out[n:])


def _pack(arrs):
    flat = jnp.concatenate([a.reshape(-1).astype(F32) for a in arrs])
    block = OPT_ROW_BLOCK if flat.shape[0] > OPT_ROW_BLOCK * LANES else SUBLANES
    pad = (-flat.shape[0]) % (block * LANES)
    if pad:
        flat = jnp.concatenate([flat, jnp.zeros((pad,), F32)])
    return flat.reshape(-1, LANES)


def _unpack(packed, shapes, lead=()):
    flat = packed.reshape(lead + (-1,))
    out, off = [], 0
    for s in shapes:
        size = int(np.prod(s))
        out.append(flat[..., off:off + size].reshape(lead + tuple(s)))
        off += size
    return out


def _unshard_last(g):
    nd = g.ndim
    perm = tuple(range(1, nd - 1)) + (0, nd - 1)
    t = jnp.transpose(g, perm)
    return t.reshape(t.shape[:-2] + (N_DEV * g.shape[-1],))


def kernel(x, c, w_ada, b_ada, ln_g, ln_b, ffn_gu, ffn_down, gmlp_w_in, gmlp_b_in, gmlp_ln_g, gmlp_ln_b, gmlp_w_s, gmlp_b_s, gmlp_w_out, w_ada_kv, b_ada_kv, w_kv, attn_w_q, attn_rel_bias, attn_w_o, loss_target, m_w_ada, m_b_ada, m_ln_g, m_ln_b, m_ffn_gu, m_ffn_down, m_gmlp_w_in, m_gmlp_b_in, m_gmlp_ln_g, m_gmlp_ln_b, m_gmlp_w_s, m_gmlp_b_s, m_gmlp_w_out, m_w_ada_kv, m_b_ada_kv, m_w_kv, m_attn_w_q, m_attn_rel_bias, m_attn_w_o, v_w_ada, v_b_ada, v_ln_g, v_ln_b, v_ffn_gu, v_ffn_down, v_gmlp_w_in, v_gmlp_b_in, v_gmlp_ln_g, v_gmlp_ln_b, v_gmlp_w_s, v_gmlp_b_s, v_gmlp_w_out, v_w_ada_kv, v_b_ada_kv, v_w_kv, v_attn_w_q, v_attn_rel_bias, v_attn_w_o):
    weights = dict(w_ada=w_ada, b_ada=b_ada, ln_g=ln_g, ln_b=ln_b, ffn_gu=ffn_gu, ffn_down=ffn_down,
                   gmlp_w_in=gmlp_w_in, gmlp_b_in=gmlp_b_in, gmlp_ln_g=gmlp_ln_g, gmlp_ln_b=gmlp_ln_b,
                   gmlp_w_s=gmlp_w_s, gmlp_b_s=gmlp_b_s, gmlp_w_out=gmlp_w_out, w_ada_kv=w_ada_kv,
                   b_ada_kv=b_ada_kv, w_kv=w_kv, attn_w_q=attn_w_q, attn_rel_bias=attn_rel_bias,
                   attn_w_o=attn_w_o)
    mom1 = dict(w_ada=m_w_ada, b_ada=m_b_ada, ln_g=m_ln_g, ln_b=m_ln_b, ffn_gu=m_ffn_gu, ffn_down=m_ffn_down,
                gmlp_w_in=m_gmlp_w_in, gmlp_b_in=m_gmlp_b_in, gmlp_ln_g=m_gmlp_ln_g, gmlp_ln_b=m_gmlp_ln_b,
                gmlp_w_s=m_gmlp_w_s, gmlp_b_s=m_gmlp_b_s, gmlp_w_out=m_gmlp_w_out, w_ada_kv=m_w_ada_kv,
                b_ada_kv=m_b_ada_kv, w_kv=m_w_kv, attn_w_q=m_attn_w_q, attn_rel_bias=m_attn_rel_bias,
                attn_w_o=m_attn_w_o)
    mom2 = dict(w_ada=v_w_ada, b_ada=v_b_ada, ln_g=v_ln_g, ln_b=v_ln_b, ffn_gu=v_ffn_gu, ffn_down=v_ffn_down,
                gmlp_w_in=v_gmlp_w_in, gmlp_b_in=v_gmlp_b_in, gmlp_ln_g=v_gmlp_ln_g, gmlp_ln_b=v_gmlp_ln_b,
                gmlp_w_s=v_gmlp_w_s, gmlp_b_s=v_gmlp_b_s, gmlp_w_out=v_gmlp_w_out, w_ada_kv=v_w_ada_kv,
                b_ada_kv=v_b_ada_kv, w_kv=v_w_kv, attn_w_q=v_attn_w_q, attn_rel_bias=v_attn_rel_bias,
                attn_w_o=v_attn_w_o)
    order = list(weights)

    x = x[0]
    target = loss_target[0]
    t, d = x.shape
    n_mod = w_ada.shape[-1] * N_DEV // d
    mod_w = w_ada.shape[-1]
    kv_w = w_ada_kv.shape[-1]
    n_b = DEPTH - N_A
    me = _linear(*_position())

    l2 = DEPTH * 2
    big = dict(
        ffn_gu=ffn_gu.reshape((l2,) + ffn_gu.shape[2:]),
        ffn_down=ffn_down.reshape((l2,) + ffn_down.shape[2:]),
        gmlp_w_in=gmlp_w_in, gmlp_w_out=gmlp_w_out, w_kv=w_kv[None],
        attn_w_q=attn_w_q, attn_w_o=attn_w_o)
    big_names = list(big)
    core = lax.axis_index("c").astype(jnp.int32).reshape(1)
    chip = (2 * lax.axis_index("x") + lax.axis_index("y")).astype(jnp.int32).reshape(1)

    fwd_groups = [
        {"ffn_gu": (0, 1), "ffn_down": (0, 1)},
        {"gmlp_w_in": (0, 1), "gmlp_w_out": (0, 1), "ffn_gu": (1, 1), "ffn_down": (1, 1)},
        {"ffn_gu": (2, 2), "ffn_down": (2, 2), "gmlp_w_in": (1, 1), "gmlp_w_out": (1, 1), "w_kv": (0, 1)},
        {"ffn_gu": (4, 2), "ffn_down": (4, 2), "attn_w_q": (0, 1), "attn_w_o": (0, 1)},
        {"ffn_gu": (6, 2), "ffn_down": (6, 2), "attn_w_q": (1, 1), "attn_w_o": (1, 1)},
    ]
    bwd_groups = []
    for l in range(DEPTH):
        g = {"ffn_gu": (2 * l, 2), "ffn_down": (2 * l, 2)}
        if l < N_A:
            g.update({"gmlp_w_in": (l, 1), "gmlp_w_out": (l, 1)})
        else:
            g.update({"attn_w_q": (l - N_A, 1), "attn_w_o": (l - N_A, 1)})
        if l == N_A - 1:
            g["w_kv"] = (0, 1)
        bwd_groups.append(g)

    def slot_of(groups, name, slot):
        for gi, g in enumerate(groups):
            if name in g and g[name][0] <= slot < g[name][0] + g[name][1]:
                return gi, slot - g[name][0]
        raise KeyError((name, slot))

    flights = []
    for gi, g in enumerate(fwd_groups):
        lands = []
        for name, (s0, cnt) in g.items():
            shard = big[name][s0:s0 + cnt].astype(BF16)
            land = lax.empty((N_DEV,) + shard.shape, BF16)
            lands.append(lax.dynamic_update_slice(land, shard[None], (me,) + (0,) * shard.ndim))
        flights.append(_gather_ici_start(f"gather_ici_start_{gi}", lands))
    start_token = sum(f[3][0, 0] for f in flights)
    gathered = [None] * len(fwd_groups)

    def land_group(gi, after):
        send_sems, recv_sems, lands, _ = flights[gi]
        lands = _gather_ici_wait(f"gather_ici_wait_{gi}", lands, send_sems, recv_sems, after)
        gathered[gi] = dict(zip(fwd_groups[gi], _gather_d2d(lands)))

    def weight(name, slot):
        gi, local = slot_of(fwd_groups, name, slot)
        return gathered[gi][name], local

    partial = [{name: lax.empty((N_DEV, cnt) + big[name].shape[1:], BF16) for name, (_, cnt) in g.items()}
               for g in bwd_groups]

    c_all = _all_gather_small(_pack([c]))
    c_all = _unpack(c_all, [(d,)], lead=(N_DEV,))[0]
    c4 = _as4(c_all)
    mod_part = _matmul("ada_fwd", c4, w_ada[:, None], (DEPTH, 1, N_DEV, mod_w), F32, a_silu=True)
    kv_part = _matmul("ada_kv_fwd", c4, _as4(w_ada_kv), (1, 1, N_DEV, kv_w), F32, a_silu=True)
    small_shapes = [mod_part.shape, kv_part.shape, ln_g.shape, ln_b.shape, gmlp_b_in.shape,
                    gmlp_ln_g.shape, gmlp_ln_b.shape, attn_rel_bias.shape]
    small = _all_gather_small(_pack([mod_part, kv_part, ln_g, ln_b, gmlp_b_in, gmlp_ln_g, gmlp_ln_b,
                                     attn_rel_bias]))
    (mod_g, kvm_g, ln_g_g, ln_b_g, b_in_g, gln_g_g, gln_b_g, rel_g) = _unpack(small, small_shapes, lead=(N_DEV,))
    mod_mine = lax.dynamic_index_in_dim(mod_g[:, :, 0], me, axis=2, keepdims=False)
    mod = _unshard_last(mod_mine) + b_ada
    mod = mod.reshape(DEPTH, n_mod, 1, d)
    kvm_mine = lax.dynamic_index_in_dim(kvm_g[:, 0, 0], me, axis=1, keepdims=False)
    mkv = (_unshard_last(kvm_mine) + b_ada_kv).reshape(2, 1, d)
    ln_g_f = _unshard_last(ln_g_g)
    ln_b_f = _unshard_last(ln_b_g)
    half = N_DEV // 2
    b_in_f = jnp.transpose(b_in_g, (1, 0, 2))[:, :, None, :]
    gln_g_f = _unshard_last(gln_g_g).reshape(N_A, half, 1, -1)
    gln_b_f = _unshard_last(gln_b_g).reshape(N_A, half, 1, -1)
    rel_f = _unshard_last(rel_g)

    def shard_act(a):
        return a.reshape(a.shape[0], a.shape[2], a.shape[3])

    def grad_into(name, slot, mm):
        gi, local = slot_of(bwd_groups, name, slot)
        partial[gi][name] = mm(partial[gi][name], local)

    def ffn_fwd(h, lw):
        w_gu, l_gu = weight("ffn_gu", lw)
        w_dn, l_dn = weight("ffn_down", lw)
        n = w_gu.shape[-1]
        gu = _matmul("ffn_gu_fwd", _as4(h), w_gu, (N_DEV, 1, t, n), F32, lb=l_gu)
        a = _swiglu_fwd(shard_act(gu))
        y = _matmul("ffn_down_fwd", a[:, None], w_dn, (1, 1, t, d), F32, lb=l_dn, b_merge=2, reduce=True)
        return y[0, 0], (gu, a)

    def ffn_bwd(dy, h, saved, lw):
        gu, a = saved
        w_gu, l_gu = weight("ffn_gu", lw)
        w_dn, l_dn = weight("ffn_down", lw)
        n = w_gu.shape[-1]
        da = _matmul("ffn_down_bwd_a", _as4(dy), w_dn, (half, 1, t, n), F32, lb=l_dn, b_merge=2, tb=True)
        grad_into("ffn_down", lw, lambda buf, lo: _matmul(
            "ffn_down_bwd_w", a[:, None], _as4(dy), buf.shape, BF16, ta=True, lo=lo, out_merge=2, out_buf=buf))
        dgu = _swiglu_bwd(shard_act(gu), shard_act(da))
        dh = _matmul("ffn_gu_bwd_a", dgu[:, None], w_gu, (1, 1, t, d), F32, lb=l_gu, tb=True, reduce=True)
        grad_into("ffn_gu", lw, lambda buf, lo: _matmul(
            "ffn_gu_bwd_w", _as4(h), dgu[:, None], buf.shape, BF16, ta=True, lo=lo, out_buf=buf))
        return dh[0, 0], {}

    def gmlp_params(l):
        return (b_in_f[l], gln_g_f[l], gln_b_f[l], gmlp_w_s[l], gmlp_b_s[l][:, :, None])

    def gmlp_fwd(h, l):
        w_in, l_in = weight("gmlp_w_in", l)
        w_out, l_out = weight("gmlp_w_out", l)
        n = w_in.shape[-1]
        zpre = _matmul("gmlp_in_fwd", _as4(h), w_in, (N_DEV, 1, t, n), F32, lb=l_in)
        gated = _gmlp_mid_fwd(shard_act(zpre), *gmlp_params(l))
        y = _matmul("gmlp_out_fwd", gated[:, None], w_out, (1, 1, t, d), F32, lb=l_out, b_merge=2, reduce=True)
        return y[0, 0], (zpre, gated)

    def gmlp_bwd(dy, h, saved, l):
        zpre, gated = saved
        w_in, l_in = weight("gmlp_w_in", l)
        w_out, l_out = weight("gmlp_w_out", l)
        n = w_in.shape[-1]
        dgated = _matmul("gmlp_out_bwd_a", _as4(dy), w_out, (half, 1, t, n), F32, lb=l_out, b_merge=2, tb=True)
        grad_into("gmlp_w_out", l, lambda buf, lo: _matmul(
            "gmlp_out_bwd_w", gated[:, None], _as4(dy), buf.shape, BF16, ta=True, lo=lo, out_merge=2, out_buf=buf))
        dz, dws, dbs, dlng, dlnb, dbin = _gmlp_mid_bwd(shard_act(zpre), shard_act(dgated), *gmlp_params(l))
        dh = _matmul("gmlp_in_bwd_a", dz[:, None], w_in, (1, 1, t, d), F32, lb=l_in, tb=True, reduce=True)
        grad_into("gmlp_w_in", l, lambda buf, lo: _matmul(
            "gmlp_in_bwd_w", _as4(h), dz[:, None], buf.shape, BF16, ta=True, lo=lo, out_buf=buf))
        small_grads = dict(gmlp_w_s=dws, gmlp_b_s=dbs[:, :, 0], gmlp_ln_g=dlng.reshape(-1),
                           gmlp_ln_b=dlnb.reshape(-1), gmlp_b_in=dbin.reshape(-1))
        return dh[0, 0], small_grads

    def attn_fwd(h, j, kvp):
        rel_vec = _rel_vector(rel_f[j])
        w_q, l_q = weight("attn_w_q", j)
        w_o, l_o = weight("attn_w_o", j)
        q = _matmul("attn_q_fwd", _as4(h), w_q, (1, 1, t, d), BF16, lb=l_q, b_merge=N_DEV, reduce=True)[0, 0]
        o = _attn_fwd(q, kvp, rel_vec)
        y = _matmul("attn_o_fwd", _as4(o), w_o, (1, 1, t, d), F32, lb=l_o, b_merge=N_DEV, reduce=True)
        return y[0, 0], (q, o, rel_vec)

    def attn_bwd(dy, h, saved, j, kvp, dkv_acc):
        q, o, rel_vec = saved
        w_q, l_q = weight("attn_w_q", j)
        w_o, l_o = weight("attn_w_o", j)
        do = _matmul("attn_o_bwd_a", _as4(dy), w_o, (1, 1, t, d), BF16, lb=l_o, b_merge=N_DEV, tb=True)[0, 0]
        grad_into("attn_w_o", j, lambda buf, lo: _matmul(
            "attn_o_bwd_w", _as4(o), _as4(dy), buf.shape, BF16, ta=True, lo=lo, out_merge=N_DEV, out_buf=buf))
        dq, dk, dv, dsc = _attn_bwd(q, do, kvp, rel_vec, *dkv_acc)
        drel = _rel_bias_grad(_skew_diagonals(dsc))
        dh = _matmul("attn_q_bwd_a", _as4(dq), w_q, (1, 1, t, d), F32, lb=l_q, b_merge=N_DEV, tb=True)
        grad_into("attn_w_q", j, lambda buf, lo: _matmul(
            "attn_q_bwd_w", _as4(h), _as4(dq), buf.shape, BF16, ta=True, lo=lo, out_merge=N_DEV, out_buf=buf))
        return dh[0, 0], dict(attn_rel_bias=drel, dkv=(dk, dv))

    tape = []
    kvp = None
    kv_tape = None
    first_use = {(0, 0): 0, (0, 1): 1, (1, 0): 2, (2, 0): 3, (3, 0): 4}
    mod = mod.at[0, 0].add(start_token)
    for l in range(DEPTH):
        for i in range(3):
            if (l, i) in first_use:
                land_group(first_use[l, i], x)
            shift, scl, gate = mod[l, 3 * i], mod[l, 3 * i + 1], mod[l, 3 * i + 2]
            wgt = 1.0 if i == 1 else 0.5
            gw = wgt * (1.0 + gate)
            h = _modulate(x, scl, shift)
            if i != 1:
                y, saved = ffn_fwd(h, 2 * l + i // 2)
            elif l < N_A:
                y, saved = gmlp_fwd(h, l)
            else:
                y, saved = attn_fwd(h, l - N_A, kvp)
            x_new = _ln_res_fwd(x, y, gw, ln_g_f[l, i][None], ln_b_f[l, i][None])
            tape.append((x, h, y, gw, scl, saved))
            x = x_new
        if l == N_A - 1:
            hkv = _modulate(x, mkv[1], mkv[0])
            w_kvg, l_kv = weight("w_kv", 0)
            n = w_kvg.shape[-1]
            kv = _matmul("kv_fwd", _as4(hkv), w_kvg, (N_DEV, 1, t, n), BF16, lb=l_kv)
            kvp = jnp.pad(shard_act(kv), ((0, 0), (LEFT_PAD, 0), (0, 0)))
            kv_tape = (x, hkv)

    loss_part, dx = _loss_head(x, target)
    loss = lax.psum(loss_part[0, 0], MESH_AXES)

    d_mod = [[None] * n_mod for _ in range(DEPTH)]
    d_ln_g = [[None] * 3 for _ in range(DEPTH)]
    d_ln_b = [[None] * 3 for _ in range(DEPTH)]
    small_grads = {k: [None] * N_A for k in ("gmlp_w_s", "gmlp_b_s", "gmlp_ln_g", "gmlp_ln_b", "gmlp_b_in")}
    d_rel = [None] * n_b
    dkv_acc = ()
    d_mkv = None
    reductions = [None] * DEPTH
    sent_token = None
    for l in reversed(range(DEPTH)):
        if l == N_A - 1:
            x_kv, hkv = kv_tape
            w_kvg, l_kv = weight("w_kv", 0)
            dkv = jnp.concatenate(dkv_acc)[:, LEFT_PAD:, :].astype(BF16)[:, None]
            dhkv = _matmul("kv_bwd_a", dkv, w_kvg, (1, 1, t, d), F32, lb=l_kv, tb=True, reduce=True)[0, 0]
            grad_into("w_kv", 0, lambda buf, lo: _matmul(
                "kv_bwd_w", _as4(hkv), dkv, buf.shape, BF16, ta=True, lo=lo, out_buf=buf))
            dx, ds_kv, db_kv = _mod_bwd(dx, dhkv, x_kv, mkv[1])
            d_mkv = jnp.concatenate([db_kv[0], ds_kv[0]])
        for i in reversed(range(3)):
            x_in, h, y, gw, scl, saved = tape[3 * l + i]
            wgt = 1.0 if i == 1 else 0.5
            if sent_token is not None:
                gw = gw + sent_token
                sent_token = None
            dx_res, dy, dgw, dg, db = _ln_res_bwd(x_in, y, gw, ln_g_f[l, i][None], dx)
            d_ln_g[l][i], d_ln_b[l][i] = dg[0], db[0]
            if i != 1:
                dh, extra = ffn_bwd(dy, h, saved, 2 * l + i // 2)
            elif l < N_A:
                dh, extra = gmlp_bwd(dy, h, saved, l)
                for k, g in extra.items():
                    small_grads[k][l] = g
            else:
                dh, extra = attn_bwd(dy, h, saved, l - N_A, kvp, dkv_acc)
                d_rel[l - N_A] = extra["attn_rel_bias"]
                dkv_acc = extra["dkv"]
            dx, dscl, dshift = _mod_bwd(dx_res, dh, x_in, scl)
            d_mod[l][3 * i], d_mod[l][3 * i + 1], d_mod[l][3 * i + 2] = dshift[0], dscl[0], wgt * dgw[0]
        if l > 0:
            srcs = [partial[l][k] for k in bwd_groups[l]]
            lands = [lax.empty((N_DEV - 1,) + s.shape[1:], BF16) for s in srcs]
            reductions[l] = _partials_send_start(f"partials_send_start_{l}", srcs, lands, True)
            sent_token = reductions[l][4][0, 0]
    grad_x = dx[None]

    d_mod_arr = jnp.stack([jnp.concatenate(r) for r in d_mod])
    small_part = dict(
        b_ada=d_mod_arr, b_ada_kv=d_mkv,
        ln_g=jnp.stack([jnp.stack(r) for r in d_ln_g]), ln_b=jnp.stack([jnp.stack(r) for r in d_ln_b]),
        gmlp_b_in=jnp.stack(small_grads["gmlp_b_in"]), gmlp_ln_g=jnp.stack(small_grads["gmlp_ln_g"]),
        gmlp_ln_b=jnp.stack(small_grads["gmlp_ln_b"]), gmlp_w_s=jnp.stack(small_grads["gmlp_w_s"]),
        gmlp_b_s=jnp.stack(small_grads["gmlp_b_s"]), attn_rel_bias=jnp.stack(d_rel))
    small_names = list(small_part)
    sp_shapes = [small_part[k].shape for k in small_names]
    sp_all = _all_gather_small(_pack([small_part[k] for k in small_names]))

    from_sibling = _partials_d2d([partial[0][k] for k in bwd_groups[0]])
    sums = [_chip_sum(partial[0][k], r1, core) for k, r1 in zip(bwd_groups[0], from_sibling)]
    lands = [lax.empty((len(_ALL_CHIPS) - 1,) + s.shape[1:], BF16) for s in sums]
    reductions[0] = _partials_send_start("partials_send_start_0", sums, lands, False, after=sp_all)
    sent_token = reductions[0][4][0, 0]
    c4 = c4 + sent_token

    sp_sum = _sum_parts(sp_all)
    full_grads = dict(zip(small_names, _unpack(sp_sum, sp_shapes)))
    per_dev = dict(zip(small_names, _unpack(sp_all, sp_shapes, lead=(N_DEV,))))

    def my_cols(a, width):
        return lax.dynamic_slice_in_dim(a, me * width, width, axis=a.ndim - 1)

    grads = {}
    grads["b_ada"] = full_grads["b_ada"]
    grads["b_ada_kv"] = full_grads["b_ada_kv"]
    grads["gmlp_w_s"] = full_grads["gmlp_w_s"]
    grads["gmlp_b_s"] = full_grads["gmlp_b_s"]
    for k in ("ln_g", "ln_b", "gmlp_b_in", "gmlp_ln_g", "gmlp_ln_b", "attn_rel_bias"):
        grads[k] = my_cols(full_grads[k], weights[k].shape[-1])

    dmod_cols = jnp.transpose(my_cols(per_dev["b_ada"], mod_w), (1, 0, 2))[:, None]
    grads["w_ada"] = _matmul("ada_bwd_w", c4, dmod_cols, (DEPTH, 1, d, mod_w), F32, ta=True,
                             a_silu=True)[:, 0]
    dkv_cols = my_cols(per_dev["b_ada_kv"], kv_w)[None, None]
    grads["w_ada_kv"] = _matmul("ada_kv_bwd_w", c4, dkv_cols, (1, 1, d, kv_w), F32, ta=True,
                                a_silu=True)[0, 0]

    delta, new_m, new_v = {}, {}, {}
    first = jnp.zeros((1,), jnp.int32)

    def flat2(a, cols):
        return a.reshape(-1, cols)

    for k in ("w_ada", "w_ada_kv"):
        w = weights[k]
        cols = w.shape[-1]
        res = _adamw(grads[k].reshape(1, -1, cols), first, None, flat2(w, cols), flat2(mom1[k], cols),
                     flat2(mom2[k], cols))
        grads[k], delta[k], new_m[k], new_v[k] = (a.reshape(w.shape) for a in res)

    tiny = [k for k in order if k not in delta and k not in big_names]
    tiny_shapes = [weights[k].shape for k in tiny]
    tiny_out = _adamw((_pack([grads[k] for k in tiny]) + sent_token)[None], first, None,
                      _pack([weights[k] for k in tiny]), _pack([mom1[k] for k in tiny]),
                      _pack([mom2[k] for k in tiny]))
    for dst, arr in zip((grads, delta, new_m, new_v), tiny_out):
        for k, val in zip(tiny, _unpack(arr, tiny_shapes)):
            dst[k] = val

    bufs = {k: [lax.empty(flat2(weights[k], weights[k].shape[-1]).shape, F32) for _ in range(4)]
            for k in big_names}
    done = tiny_out[0]
    me_idx = me.astype(jnp.int32).reshape(1)
    for l in reversed(range(DEPTH)):
        send_sems, recv_sems, srcs, lands, _ = reductions[l]
        srcs, lands = _partials_send_wait(f"partials_send_wait_{l}", srcs, lands, send_sems, recv_sems, l > 0, done)
        for k, own, got in zip(bwd_groups[l], srcs, lands):
            cols = weights[k].shape[-1]
            slot_rows = int(np.prod(big[k].shape[1:-1]))
            bufs[k] = _adamw(own.reshape(own.shape[0], -1, cols), me_idx if l > 0 else chip,
                             got.reshape(got.shape[0], -1, cols),
                             flat2(weights[k], cols), flat2(mom1[k], cols), flat2(mom2[k], cols),
                             row0=bwd_groups[l][k][0] * slot_rows, bufs=bufs[k])
            done = bufs[k][0][:SUBLANES, :LANES]
    for k in big_names:
        grads[k], delta[k], new_m[k], new_v[k] = (b.reshape(weights[k].shape) for b in bufs[k])

    return (loss, grad_x, *[grads[k] for k in order], *[delta[k] for k in order],
            *[new_m[k] for k in order], *[new_v[k] for k in order])
```

```python
import functools

import numpy as np
import jax
import jax.numpy as jnp
from jax import lax
from jax.experimental import pallas as pl
from jax.experimental.pallas import tpu as pltpu

F32 = jnp.float32
BF16 = jnp.bfloat16
MESH_AXES = ("x", "y", "c")
N_DEV = 8
MESH_ID = pl.DeviceIdType.MESH

DEPTH = 4
N_A = 2
CHUNK = 64
N_HEADS = 16
LEFT_CHUNKS = 8
BAND = (LEFT_CHUNKS + 1) * CHUNK
LEFT_PAD = LEFT_CHUNKS * CHUNK
MAX_REL = 4 * CHUNK
N_REL = (CHUNK - 1) + MAX_REL + 1
GMLP_WINDOW = 128
GMLP_GROUPS = 8
ALPHA = (2.0 * DEPTH) ** 0.25
LN_EPS = 1e-5
ADAM_LR = 0.001
ADAM_B1 = 0.9
ADAM_B2 = 0.999
ADAM_EPS = 1e-08
ADAM_WD = 0.01
ADAM_STEP = 10

V7X_VMEM_BYTES = 64 * 1024 * 1024
VMEM_LIMIT = V7X_VMEM_BYTES - 8 * 1024 * 1024
LANES = 128
SUBLANES = 8
MM_BLOCK = 2048
BIG_ROW_BLOCK = 1024
ROW_BLOCK = 512
OPT_ROW_BLOCK = 256

_ANY = pl.BlockSpec(memory_space=pl.ANY)
_VMEM = pl.BlockSpec(memory_space=pltpu.VMEM)


def _params(sem=None):
    return pltpu.CompilerParams(dimension_semantics=sem, vmem_limit_bytes=VMEM_LIMIT)


def _row_block(rows, target):
    for d in range(min(rows, target), 0, -1):
        if rows % d == 0 and (d % SUBLANES == 0 or d == rows):
            return d
    return rows


def _matmul(name, a, b, out_shape4, out_dtype, *, la=0, lb=0, lo=0, ta=False, tb=False,
            reduce=False, b_merge=1, out_merge=1, out_buf=None, a_silu=False):
    ja_n, _, a_r, a_c = a.shape
    jb_n, _, b_r, b_c = b.shape
    jo_n, _, o_r, o_c = out_shape4
    m_tot = a_c if ta else a_r
    k_a = a_r if ta else a_c
    b_rows = b_merge * b_r
    k_c = b_c if tb else b_rows
    n = b_rows if tb else b_c
    n_chunks = (jb_n // b_merge) if reduce else 1
    natural_k = reduce and ja_n == 1
    assert n == o_c, (name, n, o_c)
    assert k_a ==(k_c * n_chunks if natural_k else k_c), (name, k_a, k_c, n_chunks)
    bk = k_c if (k_c <= MM_BLOCK or (b_merge > 1 and not tb)) else MM_BLOCK
    assert k_c % bk == 0
    nkk = k_c // bk
    nk = n_chunks * nkk
    m_out = out_merge * o_r
    assert m_tot == m_out, (name, m_tot, m_out)
    bm = m_tot if (m_tot <= MM_BLOCK or out_merge > 1) else MM_BLOCK
    assert m_tot % bm == 0
    jo_blocks = jo_n // out_merge

    def a_index(j, m, k):
        kj, kk = k // nkk, k % nkk
        ja = 0 if ja_n == 1 else (kj if reduce else j)
        ke = kk + kj * nkk if natural_k else kk
        return (ja, la, ke, m) if ta else (ja, la, m, ke)

    def b_index(j, m, k):
        kj, kk = k // nkk, k % nkk
        jb = 0 if jb_n == b_merge else (kj if reduce else j)
        return (jb, lb, 0, kk) if tb else (jb, lb, kk, 0)

    def o_index(j, m, k):
        return (j, lo, 0, 0) if out_merge > 1 else (j, lo, m, 0)

    a_block = (None, None, bk, bm) if ta else (None, None, bm, bk)
    if b_merge > 1:
        b_block = (b_merge, None, b_r, bk if tb else n)
    else:
        b_block = (None, None, n, bk) if tb else (None, None, bk, n)
    o_block = (out_merge, None, o_r, n) if out_merge > 1 else (None, None, bm, n)
    dims = (((0 if ta else 1,), (1 if tb else 0,)), ((), ()))

    in_place = nk > 1 and out_dtype == F32 and out_merge == 1
    use_acc = nk > 1 and not in_place

    def body(a_ref, b_ref, *rest):
        o_ref = rest[-2] if use_acc else rest[-1]
        k = pl.program_id(2)
        av = a_ref[...]
        if a_silu:
            af = av.astype(F32)
            av = af * jax.nn.sigmoid(af)
        bv = b_ref[...]
        if b_merge > 1:
            bv = bv.reshape(b_rows, bv.shape[-1])
        prod = lax.dot_general(av.astype(BF16), bv.astype(BF16), dims, preferred_element_type=F32)

        def emit(val):
            val = val.astype(out_dtype)
            o_ref[...] = val.reshape(out_merge, o_r, n) if out_merge > 1 else val

        if nk == 1:
            emit(prod)
            return
        acc_ref = o_ref if in_place else rest[-1]

        @pl.when(k == 0)
        def _():
            acc_ref[...] = prod

        @pl.when(k > 0)
        def _():
            acc_ref[...] += prod

        if use_acc:
            @pl.when(k == nk - 1)
            def _():
                emit(acc_ref[...])

    in_specs = [pl.BlockSpec(a_block, a_index), pl.BlockSpec(b_block, b_index)]
    operands = [a, b]
    aliases = {}
    if out_buf is not None:
        assert out_buf.shape == tuple(out_shape4) and out_buf.dtype == out_dtype
        in_specs.append(_ANY)
        operands.append(out_buf)
        aliases = {2: 0}
    return pl.pallas_call(
        body, name=name,
        grid=(jo_blocks, m_tot // bm, nk),
        in_specs=in_specs,
        out_specs=pl.BlockSpec(o_block, o_index),
        out_shape=jax.ShapeDtypeStruct(tuple(out_shape4), out_dtype),
        scratch_shapes=[pltpu.VMEM((bm, n), F32)] if use_acc else [],
        input_output_aliases=aliases,
        compiler_params=_params(("parallel", "parallel", "arbitrary")),
    )(*operands)


def _as4(a):
    return a.reshape((1,) * (4 - a.ndim) + a.shape)


def _row_call(name, body, ins, outs, t, *, acc_outs=()):
    bt = _row_block(t, ROW_BLOCK)

    def spec(arr, tiled):
        if tiled:
            return pl.BlockSpec((bt,) + tuple(arr.shape[1:]), lambda i: (i,) + (0,) * (arr.ndim - 1))
        return pl.BlockSpec(tuple(arr.shape), lambda i: (0,) * arr.ndim)

    return pl.pallas_call(
        body, name=name, grid=(t // bt,),
        in_specs=[spec(a, tl) for a, tl in ins],
        out_specs=[spec(o, tl) for o, tl in outs],
        out_shape=[jax.ShapeDtypeStruct(o.shape, o.dtype) for o, _ in outs],
        compiler_params=_params(("arbitrary",) if acc_outs else ("parallel",)),
    )(*[a for a, _ in ins])


def _sds(shape, dtype):
    return jax.ShapeDtypeStruct(tuple(shape), dtype)


def _modulate(x, scl, shift):
    t, d = x.shape

    def body(x_ref, s_ref, b_ref, h_ref):
        h_ref[...] = (x_ref[...] * (1.0 + s_ref[...]) + b_ref[...]).astype(BF16)

    return _row_call("modulate", body, [(x, True), (scl, False), (shift, False)],
                     [(_sds((t, d), BF16), True)], t)[0]


def _ln_stats(r):
    mu = jnp.mean(r, axis=-1, keepdims=True)
    rc = r - mu
    var = jnp.mean(rc * rc, axis=-1, keepdims=True)
    rstd = lax.rsqrt(var + LN_EPS)
    return rc * rstd, rstd


def _ln_res_fwd(x, y, gw, g, b, mods=()):
    t, d = x.shape
    n_mod = len(mods)

    def body(x_ref, y_ref, gw_ref, g_ref, b_ref, *rest):
        mod_refs, o_ref, h_refs = rest[:2 * n_mod], rest[2 * n_mod], rest[2 * n_mod + 1:]
        r = ALPHA * x_ref[...] + gw_ref[...] * y_ref[...]
        xhat, _ = _ln_stats(r)
        xn = xhat * g_ref[...] + b_ref[...]
        o_ref[...] = xn
        for k in range(n_mod):
            h_refs[k][...] = (xn * (1.0 + mod_refs[2 * k][...]) + mod_refs[2 * k + 1][...]).astype(BF16)

    vecs = [(v, False) for pair in mods for v in pair]
    return _row_call("ln_res_fwd", body,
                     [(x, True), (y, True), (gw, False), (g, False), (b, False)] + vecs,
                     [(_sds((t, d), F32), True)] + [(_sds((t, d), BF16), True)] * n_mod, t)


def _ln_res_bwd(x, y, gw, g, b, dx_base, pairs=()):
    t, d = x.shape
    n_pair = len(pairs)

    def body(x_ref, y_ref, gw_ref, g_ref, b_ref, dxb_ref, *rest):
        pair_refs, outs = rest[:2 * n_pair], rest[2 * n_pair:]
        dx_ref, dy_ref = outs[0], outs[1]
        sums = outs[2:]

        @pl.when(pl.program_id(0) == 0)
        def _():
            for r in sums:
                r[...] = jnp.zeros_like(r)

        yv = y_ref[...]
        gwv = gw_ref[...]
        gv = g_ref[...]
        xhat, rstd = _ln_stats(ALPHA * x_ref[...] + gwv * yv)
        dxn = dxb_ref[...]
        if n_pair:
            xn = xhat * gv + b_ref[...]
            for k in range(n_pair):
                dh = pair_refs[2 * k][...]
                dxn = dxn + dh * (1.0 + pair_refs[2 * k + 1][...])
                sums[3 + 2 * k][...] += jnp.sum(dh * xn, axis=0, keepdims=True)
                sums[4 + 2 * k][...] += jnp.sum(dh, axis=0, keepdims=True)
        dxh = dxn * gv
        m1 = jnp.mean(dxh, axis=-1, keepdims=True)
        m2 = jnp.mean(dxh * xhat, axis=-1, keepdims=True)
        dr = rstd * (dxh - m1 - xhat * m2)
        dx_ref[...] = ALPHA * dr
        dy_ref[...] = (gwv * dr).astype(BF16)
        sums[0][...] += jnp.sum(dr * yv, axis=0, keepdims=True)
        sums[1][...] += jnp.sum(dxn * xhat, axis=0, keepdims=True)
        sums[2][...] += jnp.sum(dxn, axis=0, keepdims=True)

    vec = _sds((1, d), F32)
    n_sum = 3 + 2 * n_pair
    ins = [(x, True), (y, True), (gw, False), (g, False), (b, False), (dx_base, True)]
    for dh, scl in pairs:
        ins += [(dh, True), (scl, False)]
    return _row_call("ln_res_bwd", body, ins,
                     [(_sds((t, d), F32), True), (_sds((t, d), BF16), True)] + [(vec, False)] * n_sum, t,
                     acc_outs=tuple(range(2, 2 + n_sum)))


def _mod_bwd(dx_res, dh, x, scl):
    t, d = x.shape

    def body(dxr_ref, dh_ref, x_ref, s_ref, dx_ref, ds_ref, db_ref):
        @pl.when(pl.program_id(0) == 0)
        def _():
            ds_ref[...] = jnp.zeros_like(ds_ref)
            db_ref[...] = jnp.zeros_like(db_ref)

        dh = dh_ref[...]
        dx_ref[...] = dxr_ref[...] + dh * (1.0 + s_ref[...])
        ds_ref[...] += jnp.sum(dh * x_ref[...], axis=0, keepdims=True)
        db_ref[...] += jnp.sum(dh, axis=0, keepdims=True)

    vec = _sds((1, d), F32)
    return _row_call("mod_bwd", body, [(dx_res, True), (dh, True), (x, True), (scl, False)],
                     [(_sds((t, d), F32), True), (vec, False), (vec, False)], t, acc_outs=(1, 2))


def _loss_head(y, target):
    t, d = y.shape

    def body(y_ref, t_ref, l_ref, dy_ref):
        @pl.when(pl.program_id(0) == 0)
        def _():
            l_ref[...] = jnp.zeros_like(l_ref)

        err = y_ref[...] - t_ref[...]
        dy_ref[...] = err * (1.0 / d)
        part = 0.5 * jnp.sum(jnp.mean(err * err, axis=-1, keepdims=True), axis=0, keepdims=True)
        l_ref[...] += jnp.broadcast_to(part, l_ref.shape)

    return _row_call("loss_head", body, [(y, True), (target, True)],
                     [(_sds((SUBLANES, LANES), F32), False), (_sds((t, d), F32), True)], t,
                     acc_outs=(0,))


def _swiglu_fwd(gu):
    _, t, n = gu.shape
    half = N_DEV // 2
    bt = _row_block(t, BIG_ROW_BLOCK)
    gu4 = gu.reshape(2, half, t, n)

    def body(gu_ref, a_ref):
        g = gu_ref[0]
        a_ref[...] = (g * jax.nn.sigmoid(g) * gu_ref[1]).astype(BF16)

    return pl.pallas_call(
        body, name="swiglu_fwd", grid=(half, t // bt),
        in_specs=[pl.BlockSpec((2, None, bt, n), lambda j, i: (0, j, i, 0))],
        out_specs=pl.BlockSpec((None, bt, n), lambda j, i: (j, i, 0)),
        out_shape=_sds((half, t, n), BF16),
        compiler_params=_params(("parallel", "parallel")),
    )(gu4)


def _swiglu_bwd(gu, da):
    _, t, n = gu.shape
    half = N_DEV // 2
    bt = _row_block(t, BIG_ROW_BLOCK)
    gu4 = gu.reshape(2, half, t, n)

    def body(gu_ref, da_ref, d_ref):
        g = gu_ref[0]
        u = gu_ref[1]
        da = da_ref[...]
        sig = jax.nn.sigmoid(g)
        d_ref[0] = (da * u * sig * (1.0 + g * (1.0 - sig))).astype(BF16)
        d_ref[1] = (da * g * sig).astype(BF16)

    out = pl.pallas_call(
        body, name="swiglu_bwd", grid=(half, t // bt),
        in_specs=[pl.BlockSpec((2, None, bt, n), lambda j, i: (0, j, i, 0)),
                  pl.BlockSpec((None, bt, n), lambda j, i: (j, i, 0))],
        out_specs=pl.BlockSpec((2, None, bt, n), lambda j, i: (0, j, i, 0)),
        out_shape=_sds((2, half, t, n), BF16),
        compiler_params=_params(("parallel", "parallel")),
    )(gu4, da)
    return out.reshape(N_DEV, t, n)


_INV_SQRT2 = 0.7071067811865476
_INV_SQRT_2PI = 0.3989422804014327


def _gelu(z):
    return 0.5 * z * (1.0 + lax.erf(z * _INV_SQRT2))


def _gelu_grad(z):
    return 0.5 * (1.0 + lax.erf(z * _INV_SQRT2)) + z * jnp.exp(-0.5 * z * z) * _INV_SQRT_2PI


def _window_mask():
    t_out = lax.broadcasted_iota(jnp.int32, (GMLP_WINDOW, GMLP_WINDOW), 0)
    s_in = lax.broadcasted_iota(jnp.int32, (GMLP_WINDOW, GMLP_WINDOW), 1)
    return (s_in // CHUNK) <= (t_out // CHUNK)


def _gmlp_recompute(z_ref, bin_ref, lng_ref, lnb_ref):
    half = N_DEV // 2
    z = z_ref[...] + bin_ref[...]
    ge = _gelu(z)
    u = ge[:half]
    v = ge[half:]
    width = half * v.shape[-1]
    mu = jnp.sum(jnp.sum(v, axis=0), axis=-1, keepdims=True) / width
    vc = v - mu
    var = jnp.sum(jnp.sum(vc * vc, axis=0), axis=-1, keepdims=True) / width
    rstd = lax.rsqrt(var + LN_EPS)
    xhat = vc * rstd
    vn = xhat * lng_ref[...] + lnb_ref[...]
    return z, u, xhat, rstd, vn


def _gmlp_mid_fwd(zpre, b_in, ln_g, ln_b, w_s, b_s):
    _, t, n = zpre.shape
    half = N_DEV // 2
    gd = half * n // GMLP_GROUPS
    per = n // gd
    w = GMLP_WINDOW

    def body(z_ref, bin_ref, lng_ref, lnb_ref, ws_ref, bs_ref, o_ref):
        _, u, _, _, vn = _gmlp_recompute(z_ref, bin_ref, lng_ref, lnb_ref)
        mask = _window_mask()
        for g in range(GMLP_GROUPS):
            sh, c0 = g // per, (g % per) * gd
            wsm = jnp.where(mask, ws_ref[g], 0.0).astype(BF16)
            s = jnp.dot(wsm, vn[sh][:, c0:c0 + gd].astype(BF16), preferred_element_type=F32) + bs_ref[g]
            o_ref[sh, :, c0:c0 + gd] = (u[sh][:, c0:c0 + gd] * s).astype(BF16)

    whole = lambda a: pl.BlockSpec(tuple(a.shape), lambda i: (0,) * a.ndim)
    return pl.pallas_call(
        body, name="gmlp_mid_fwd", grid=(t // w,),
        in_specs=[pl.BlockSpec((N_DEV, w, n), lambda i: (0, i, 0)),
                  whole(b_in), whole(ln_g), whole(ln_b), whole(w_s), whole(b_s)],
        out_specs=pl.BlockSpec((half, w, n), lambda i: (0, i, 0)),
        out_shape=_sds((half, t, n), BF16),
        compiler_params=_params(("parallel",)),
    )(zpre, b_in, ln_g, ln_b, w_s, b_s)


def _gmlp_mid_bwd(zpre, dgated, b_in, ln_g, ln_b, w_s, b_s):
    _, t, n = zpre.shape
    half = N_DEV // 2
    gd = half * n // GMLP_GROUPS
    per = n // gd
    w = GMLP_WINDOW
    width = half * n

    def body(z_ref, dg_ref, bin_ref, lng_ref, lnb_ref, ws_ref, bs_ref,
             dz_ref, dws_ref, dbs_ref, dlng_ref, dlnb_ref, dbin_ref, du_ref, dvn_ref):
        @pl.when(pl.program_id(0) == 0)
        def _():
            for r in (dws_ref, dbs_ref, dlng_ref, dlnb_ref, dbin_ref):
                r[...] = jnp.zeros_like(r)

        z, u, xhat, rstd, vn = _gmlp_recompute(z_ref, bin_ref, lng_ref, lnb_ref)
        mask = _window_mask()
        for g in range(GMLP_GROUPS):
            sh, c0 = g // per, (g % per) * gd
            wsm = jnp.where(mask, ws_ref[g], 0.0).astype(BF16)
            vg = vn[sh][:, c0:c0 + gd].astype(BF16)
            s = jnp.dot(wsm, vg, preferred_element_type=F32) + bs_ref[g]
            dgt = dg_ref[sh, :, c0:c0 + gd]
            ds = dgt * u[sh][:, c0:c0 + gd]
            du_ref[sh, :, c0:c0 + gd] = dgt * s
            dsb = ds.astype(BF16)
            dws = lax.dot_general(dsb, vg, (((1,), (1,)), ((), ())), preferred_element_type=F32)
            dws_ref[g] += jnp.where(mask, dws, 0.0)
            dbs_ref[g] += jnp.sum(ds, axis=-1, keepdims=True)
            dvn_ref[sh, :, c0:c0 + gd] = lax.dot_general(wsm, dsb, (((0,), (0,)), ((), ())),
                                                         preferred_element_type=F32)
        dvn = dvn_ref[...]
        dlng_ref[...] += jnp.sum(dvn * xhat, axis=1, keepdims=True)
        dlnb_ref[...] += jnp.sum(dvn, axis=1, keepdims=True)
        dxh = dvn * lng_ref[...]
        m1 = jnp.sum(jnp.sum(dxh, axis=0), axis=-1, keepdims=True) / width
        m2 = jnp.sum(jnp.sum(dxh * xhat, axis=0), axis=-1, keepdims=True) / width
        dv = rstd * (dxh - m1 - xhat * m2)
        gg = _gelu_grad(z)
        dzu = du_ref[...] * gg[:half]
        dzv = dv * gg[half:]
        dz_ref[:half] = dzu.astype(BF16)
        dz_ref[half:] = dzv.astype(BF16)
        dbin_ref[:half] += jnp.sum(dzu, axis=1, keepdims=True)
        dbin_ref[half:] += jnp.sum(dzv, axis=1, keepdims=True)

    whole = lambda a: pl.BlockSpec(tuple(a.shape), lambda i: (0,) * a.ndim)
    outs = [_sds((N_DEV, t, n), BF16), _sds(w_s.shape, F32), _sds(b_s.shape, F32),
            _sds(ln_g.shape, F32), _sds(ln_b.shape, F32), _sds(b_in.shape, F32)]
    return pl.pallas_call(
        body, name="gmlp_mid_bwd", grid=(t // w,),
        in_specs=[pl.BlockSpec((N_DEV, w, n), lambda i: (0, i, 0)),
                  pl.BlockSpec((half, w, n), lambda i: (0, i, 0)),
                  whole(b_in), whole(ln_g), whole(ln_b), whole(w_s), whole(b_s)],
        out_specs=[pl.BlockSpec((N_DEV, w, n), lambda i: (0, i, 0))] + [whole(o) for o in outs[1:]],
        out_shape=outs,
        scratch_shapes=[pltpu.VMEM((half, w, n), F32), pltpu.VMEM((half, w, n), F32)],
        compiler_params=_params(("arbitrary",)),
    )(zpre, dgated, b_in, ln_g, ln_b, w_s, b_s)


ATTN_CHUNKS = 4
ATTN_ROWS = ATTN_CHUNKS * CHUNK
ATTN_WINDOW = ATTN_ROWS + LEFT_PAD
ATTN_DIAGS = 1024
ATTN_ROLL = ATTN_DIAGS - (ATTN_ROWS - 1)


def _rel_vector(rel):
    j = np.arange(ATTN_DIAGS)
    idx = np.clip(ATTN_WINDOW - 1 - j, -(CHUNK - 1), MAX_REL) + (CHUNK - 1)
    return rel[:, idx]


def _attn_bias_mask(rel_ref, bm_ref):
    tt = lax.broadcasted_iota(jnp.int32, (ATTN_ROWS, ATTN_WINDOW), 0) // CHUNK
    rr = lax.broadcasted_iota(jnp.int32, (ATTN_ROWS, ATTN_WINDOW), 1) // CHUNK
    band = (rr >= tt) & (rr <= tt + LEFT_CHUNKS)
    for j in range(bm_ref.shape[0]):
        vec = jnp.broadcast_to(rel_ref[j:j + 1, :], (ATTN_ROWS, ATTN_DIAGS))
        toeplitz = pltpu.roll(vec, ATTN_ROLL, 1, stride=1, stride_axis=0)[:, :ATTN_WINDOW]
        bm_ref[j] = jnp.where(band, toeplitz, -jnp.inf)


def _attn_probs(q_ref, k_ref, bm_ref, j, hd, start, valid):
    qh = q_ref[:, j * hd:(j + 1) * hd]
    kb = k_ref[pl.ds(start, ATTN_WINDOW), j * hd:(j + 1) * hd]
    sc = lax.dot_general(qh, kb, (((1,), (1,)), ((), ())), preferred_element_type=F32)
    sc = sc * (hd ** -0.5) + bm_ref[j]
    sc = jnp.where(valid, sc, -jnp.inf)
    sc = sc - jnp.max(sc, axis=-1, keepdims=True)
    e = jnp.exp(sc)
    return e / jnp.sum(e, axis=-1, keepdims=True), qh, kb


def _window_valid(start):
    r = lax.broadcasted_iota(jnp.int32, (1, ATTN_WINDOW), 1)
    return (start + r) >= LEFT_PAD


def _attn_fwd(q, kvp, rel_vec):
    t, d = q.shape
    hd = d // N_HEADS
    half = N_DEV // 2
    n = kvp.shape[-1]
    per = n // hd
    rows = kvp.shape[1]

    def body(q_ref, k_ref, v_ref, rel_ref, o_ref, bm_ref):
        @pl.when(pl.program_id(1) == 0)
        def _():
            _attn_bias_mask(rel_ref, bm_ref)

        start = pl.multiple_of(pl.program_id(1) * ATTN_ROWS, ATTN_ROWS)
        valid = _window_valid(start)
        for j in range(per):
            p, _, _ = _attn_probs(q_ref, k_ref, bm_ref, j, hd, start, valid)
            vb = v_ref[pl.ds(start, ATTN_WINDOW), j * hd:(j + 1) * hd]
            o_ref[:, j * hd:(j + 1) * hd] = jnp.dot(p.astype(BF16), vb, preferred_element_type=F32).astype(BF16)

    return pl.pallas_call(
        body, name="attn_fwd", grid=(half, t // ATTN_ROWS),
        in_specs=[pl.BlockSpec((ATTN_ROWS, n), lambda g, i: (i, g)),
                  pl.BlockSpec((None, rows, n), lambda g, i: (g, 0, 0)),
                  pl.BlockSpec((None, rows, n), lambda g, i: (half + g, 0, 0)),
                  pl.BlockSpec((None, per, ATTN_DIAGS), lambda g, i: (g, 0, 0))],
        out_specs=pl.BlockSpec((ATTN_ROWS, n), lambda g, i: (i, g)),
        out_shape=_sds((t, d), BF16),
        scratch_shapes=[pltpu.VMEM((per, ATTN_ROWS, ATTN_WINDOW), F32)],
        compiler_params=_params(("arbitrary", "arbitrary")),
    )(q, kvp, kvp, rel_vec.reshape(half, per, ATTN_DIAGS))


def _attn_bwd(q, dout, kvp, rel_vec, dk_in=None, dv_in=None):
    t, d = q.shape
    hd = d // N_HEADS
    half = N_DEV // 2
    n = kvp.shape[-1]
    per = n // hd
    rows = kvp.shape[1]
    scale = hd ** -0.5
    carry = dk_in is not None

    def body(q_ref, do_ref, k_ref, v_ref, rel_ref, *rest):
        dq_ref, dk_ref, dv_ref, dsc_ref, bm_ref = rest[-5:]

        @pl.when(pl.program_id(1) == 0)
        def _():
            _attn_bias_mask(rel_ref, bm_ref)
            dk_ref[...] = rest[0][...] if carry else jnp.zeros_like(dk_ref)
            dv_ref[...] = rest[1][...] if carry else jnp.zeros_like(dv_ref)
            dsc_ref[...] = jnp.zeros_like(dsc_ref)

        start = pl.multiple_of(pl.program_id(1) * ATTN_ROWS, ATTN_ROWS)
        valid = _window_valid(start)
        for j in range(per):
            cols = slice(j * hd, (j + 1) * hd)
            p, qh, kb = _attn_probs(q_ref, k_ref, bm_ref, j, hd, start, valid)
            vb = v_ref[pl.ds(start, ATTN_WINDOW), cols]
            doh = do_ref[:, cols]
            dp = lax.dot_general(doh, vb, (((1,), (1,)), ((), ())), preferred_element_type=F32)
            ds = p * (dp - jnp.sum(dp * p, axis=-1, keepdims=True))
            dsc_ref[j] += sum(ds[a * CHUNK:(a + 1) * CHUNK, a * CHUNK:a * CHUNK + BAND]
                              for a in range(ATTN_CHUNKS))
            dsb = (ds * scale).astype(BF16)
            dq_ref[:, cols] = jnp.dot(dsb, kb, preferred_element_type=F32).astype(BF16)
            dk_ref[pl.ds(start, ATTN_WINDOW), cols] += lax.dot_general(
                dsb, qh, (((0,), (0,)), ((), ())), preferred_element_type=F32)
            dv_ref[pl.ds(start, ATTN_WINDOW), cols] += lax.dot_general(
                p.astype(BF16), doh, (((0,), (0,)), ((), ())), preferred_element_type=F32)

    tile = pl.BlockSpec((ATTN_ROWS, n), lambda g, i: (i, g))
    shard = pl.BlockSpec((None, rows, n), lambda g, i: (g, 0, 0))
    in_specs = [tile, tile, shard, pl.BlockSpec((None, rows, n), lambda g, i: (half + g, 0, 0)),
                pl.BlockSpec((None, per, ATTN_DIAGS), lambda g, i: (g, 0, 0))]
    operands = [q, dout, kvp, kvp, rel_vec.reshape(half, per, ATTN_DIAGS)]
    if carry:
        in_specs += [shard, shard]
        operands += [dk_in, dv_in]
    acc = _sds((half, rows, n), F32)
    return pl.pallas_call(
        body, name="attn_bwd", grid=(half, t // ATTN_ROWS),
        in_specs=in_specs,
        out_specs=[tile, shard, shard, pl.BlockSpec((per, CHUNK, BAND), lambda g, i: (g, 0, 0))],
        out_shape=[_sds((t, d), BF16), acc, acc, _sds((N_HEADS, CHUNK, BAND), F32)],
        scratch_shapes=[pltpu.VMEM((per, ATTN_ROWS, ATTN_WINDOW), F32)],
        compiler_params=_params(("arbitrary", "arbitrary")),
    )(*operands)


SKEW_PITCH = 640
SKEW = SKEW_PITCH + 1
SKEW_LANES = -(-SKEW // LANES) * LANES


def _skew_diagonals(dsc):
    h = dsc.shape[0]
    wide = jnp.pad(dsc, ((0, 0), (0, 0), (0, SKEW_PITCH - BAND))).reshape(h, CHUNK * SKEW_PITCH)
    wide = jnp.pad(wide, ((0, 0), (0, CHUNK))).reshape(h, CHUNK, SKEW)
    return jnp.pad(wide, ((0, 0), (0, 0), (0, SKEW_LANES - SKEW)))


def _rel_bias_grad(skewed):
    heads = skewed.shape[0]
    hb = SUBLANES

    def body(d_ref, o_ref):
        col = lax.broadcasted_iota(jnp.int32, (SKEW_LANES, N_REL), 0)
        bucket = lax.broadcasted_iota(jnp.int32, (SKEW_LANES, N_REL), 1)
        diag = jnp.where(col < BAND, col, col - SKEW)
        idx = jnp.clip(LEFT_PAD - diag, -(CHUNK - 1), MAX_REL) + (CHUNK - 1)
        oh = ((idx == bucket) & (col < SKEW)).astype(BF16)
        dv = jnp.sum(d_ref[...], axis=1)
        hi = dv.astype(BF16)
        rest = dv - hi.astype(F32)
        mid = rest.astype(BF16)
        lo = (rest - mid.astype(F32)).astype(BF16)
        acc = jnp.dot(hi, oh, preferred_element_type=F32)
        acc += jnp.dot(mid, oh, preferred_element_type=F32)
        acc += jnp.dot(lo, oh, preferred_element_type=F32)
        o_ref[...] = acc

    return pl.pallas_call(
        body, name="rel_bias_grad", grid=(heads // hb,),
        in_specs=[pl.BlockSpec((hb, CHUNK, SKEW_LANES), lambda i: (i, 0, 0))],
        out_specs=pl.BlockSpec((hb, N_REL), lambda i: (i, 0)),
        out_shape=_sds((heads, N_REL), F32),
        compiler_params=_params(("parallel",)),
    )(skewed)


def _sum_parts(parts):
    s_n, rows, c = parts.shape
    br = _row_block(rows, OPT_ROW_BLOCK)

    def body(p_ref, o_ref):
        acc = p_ref[0].astype(F32)
        for s in range(1, s_n):
            acc = acc + p_ref[s].astype(F32)
        o_ref[...] = acc

    return pl.pallas_call(
        body, name="sum_parts", grid=(rows // br,),
        in_specs=[pl.BlockSpec((s_n, br, c), lambda i: (0, i, 0))],
        out_specs=pl.BlockSpec((br, c), lambda i: (i, 0)),
        out_shape=_sds((rows, c), F32),
        compiler_params=_params(("parallel",)),
    )(parts)


def _adamw(own, own_idx, parts, w, m, v, row0=0, bufs=None, after=None):
    _, rows, c = own.shape
    s_n = 0 if parts is None else parts.shape[0]
    total = w.shape[0]
    br = _row_block(rows, OPT_ROW_BLOCK)
    assert row0 % br == 0 and (bufs is not None or (row0 == 0 and total == rows))
    b0 = row0 // br
    m_corr = 1.0 - ADAM_B1 ** ADAM_STEP
    v_corr = 1.0 - ADAM_B2 ** ADAM_STEP

    def body(idx_ref, own_ref, *refs):
        if s_n:
            p_ref, refs = refs[0], refs[1:]
        w_ref, m_ref, v_ref = refs[:3]
        g_ref, d_ref, nm_ref, nv_ref = refs[-4:]
        g = own_ref[...].astype(F32)
        for s in range(s_n):
            g = g + p_ref[s].astype(F32)
        nm = ADAM_B1 * m_ref[...] + (1.0 - ADAM_B1) * g
        nv = ADAM_B2 * v_ref[...] + (1.0 - ADAM_B2) * (g * g)
        g_ref[...] = g
        nm_ref[...] = nm
        nv_ref[...] = nv
        d_ref[...] = -ADAM_LR * ((nm / m_corr) / (jnp.sqrt(nv / v_corr) + ADAM_EPS) + ADAM_WD * w_ref[...])

    tile = pl.BlockSpec((br, c), lambda i, idx: (i + b0, 0))
    in_specs = [pl.BlockSpec((None, br, c), lambda i, idx: (idx[0], i, 0))]
    operands = [own_idx, own]
    if s_n:
        in_specs.append(pl.BlockSpec((s_n, br, c), lambda i, idx: (0, i, 0)))
        operands.append(parts)
    in_specs += [tile, tile, tile]
    operands += [w, m, v]
    aliases = {}
    if bufs is not None:
        aliases = {len(operands) + j: j for j in range(4)}
        in_specs += [_ANY] * 4
        operands += list(bufs)
    if after is not None:
        in_specs.append(_ANY)
        operands.append(after)
    out = _sds((total, c), F32)
    return pl.pallas_call(
        body, name="adamw",
        grid_spec=pltpu.PrefetchScalarGridSpec(
            num_scalar_prefetch=1, grid=(rows // br,), in_specs=in_specs,
            out_specs=[tile, tile, tile, tile]),
        out_shape=[out, out, out, out],
        input_output_aliases=aliases,
        compiler_params=_params(("parallel",)),
    )(*operands)


def _chip_sum(p, r1, core):
    half = N_DEV // 2
    c = p.shape[-1]
    rows = int(np.prod(p.shape[1:-1]))
    br = _row_block(rows, BIG_ROW_BLOCK)

    def body(core_ref, p_ref, r_ref, o_ref):
        o_ref[...] = (p_ref[...].astype(F32) + r_ref[...].astype(F32)).astype(BF16)

    out = pl.pallas_call(
        body, name="chip_sum",
        grid_spec=pltpu.PrefetchScalarGridSpec(
            num_scalar_prefetch=1, grid=(half, rows // br),
            in_specs=[pl.BlockSpec((None, None, br, c), lambda q, i, cr: (q, cr[0], i, 0)),
                      pl.BlockSpec((None, br, c), lambda q, i, cr: (q, i, 0))],
            out_specs=pl.BlockSpec((None, br, c), lambda q, i, cr: (q, i, 0))),
        out_shape=_sds((half, rows, c), BF16),
        compiler_params=_params(("parallel", "parallel")),
    )(core, p.reshape(half, 2, rows, c), r1.reshape(half, rows, c))
    return out.reshape((half,) + p.shape[1:])


def _position():
    return tuple(lax.axis_index(a) for a in MESH_AXES)


def _linear(px, py, pc):
    return 4 * px + 2 * py + pc


def _all_gather_small(v):
    rows, lanes = v.shape

    def body(x_ref, out_ref, send_sems, recv_sems, local_sem):
        x, y, c = _position()
        me, sibling = (x, y, c), (x, y, 1 - c)
        chips = [(1 - x, y), (x, 1 - y), (1 - x, 1 - y)]

        def copy(k, block, to, src=None):
            dst = out_ref.at[_linear(*block)]
            return pltpu.make_async_remote_copy(
                src_ref=dst if src is None else src, dst_ref=dst,
                send_sem=send_sems.at[k], recv_sem=recv_sems.at[k],
                device_id=to, device_id_type=MESH_ID)

        mine = pltpu.make_async_copy(x_ref, out_ref.at[_linear(*me)], local_sem)
        mine.start()
        first = [copy(0, me, sibling, src=x_ref)]
        first += [copy(1 + j, me, (*chip, c), src=x_ref) for j, chip in enumerate(chips)]
        for cp in first:
            cp.start()
        passed = [copy(4 + j, (*chip, c), sibling) for j, chip in enumerate(chips)]
        for j, chip in enumerate(chips):
            copy(1 + j, (*chip, c), me).wait_recv()
            passed[j].start()
        copy(0, sibling, me).wait_recv()
        for j, chip in enumerate(chips):
            copy(4 + j, (*chip, 1 - c), me).wait_recv()
        for cp in first + passed:
            cp.wait_send()
        mine.wait()

    return pl.pallas_call(
        body, name="all_gather_small",
        out_shape=_sds((N_DEV, rows, lanes), v.dtype),
        in_specs=[_VMEM], out_specs=_VMEM,
        scratch_shapes=[pltpu.SemaphoreType.DMA((7,)), pltpu.SemaphoreType.DMA((7,)),
                        pltpu.SemaphoreType.DMA],
        compiler_params=pltpu.CompilerParams(vmem_limit_bytes=VMEM_LIMIT),
    )(v)


_HBM = pl.BlockSpec(memory_space=pltpu.HBM)
_SEM = pl.BlockSpec(memory_space=pltpu.SEMAPHORE)
_EFFECT = pltpu.SideEffectType.DATAFLOW_SIDE_EFFECTING
_ALL_CHIPS = [(0, 0), (0, 1), (1, 0), (1, 1)]


def _other_chips(x, y):
    return [(1 - x, y), (x, 1 - y), (1 - x, 1 - y)]


def _in_hbm(a):
    return pltpu.with_memory_space_constraint(a, pltpu.HBM)


def _token():
    return _sds((SUBLANES, LANES), F32)


def _gather_ici_copy(ref, i, k, chip, c, block, send_sems, recv_sems):
    return pltpu.make_async_remote_copy(
        src_ref=ref.at[block], dst_ref=ref.at[block],
        send_sem=send_sems.at[3 * i + k], recv_sem=recv_sems.at[3 * i + k],
        device_id=(*chip, c), device_id_type=MESH_ID)


def _gather_ici_start(name, lands):
    n = len(lands)

    def body(*refs):
        ins, send_sems, recv_sems, token = refs[:n], refs[n], refs[n + 1], refs[-1]
        x, y, c = _position()
        me = _linear(x, y, c)
        for i in range(n):
            for k, chip in enumerate(_other_chips(x, y)):
                _gather_ici_copy(ins[i], i, k, chip, c, me, send_sems, recv_sems).start()
        token[...] = jnp.zeros_like(token)

    out = pl.pallas_call(
        body, name=name,
        out_shape=(pltpu.SemaphoreType.DMA((3 * n,)), pltpu.SemaphoreType.DMA((3 * n,)),
                   *[pltpu.HBM(a.shape, a.dtype) for a in lands], _token()),
        in_specs=[_HBM] * n, out_specs=(_SEM, _SEM, *[_HBM] * n, _VMEM),
        input_output_aliases={i: 2 + i for i in range(n)},
        compiler_params=pltpu.CompilerParams(has_side_effects=_EFFECT),
    )(*[_in_hbm(a) for a in lands])
    return out[0], out[1], list(out[2:2 + n]), out[-1]


def _gather_ici_wait(name, lands, send_sems, recv_sems, after):
    n = len(lands)

    def body(*refs):
        ins, ss, rs = refs[:n], refs[n], refs[n + 1]
        x, y, c = _position()
        me = _linear(x, y, c)
        for i in range(n):
            for k, chip in enumerate(_other_chips(x, y)):
                _gather_ici_copy(ins[i], i, k, chip, c, me, ss, rs).wait_send()
                _gather_ici_copy(ins[i], i, k, chip, c, _linear(*chip, c), ss, rs).wait_recv()

    out = pl.pallas_call(
        body, name=name,
        out_shape=[pltpu.HBM(a.shape, a.dtype) for a in lands],
        in_specs=[_HBM] * n + [_SEM, _SEM, _ANY], out_specs=[_HBM] * n,
        input_output_aliases={i: i for i in range(n)},
        compiler_params=pltpu.CompilerParams(has_side_effects=_EFFECT),
    )(*lands, send_sems, recv_sems, after)
    return list(out)


def _gather_d2d(lands):
    n = len(lands)

    def body(*refs):
        ins, outs, send_sems, recv_sems = refs[:n], refs[n:2 * n], refs[2 * n], refs[2 * n + 1]
        x, y, c = _position()

        def copy(i, q, core):
            block = _linear(*_ALL_CHIPS[q], core)
            return pltpu.make_async_remote_copy(
                src_ref=ins[i].at[block], dst_ref=outs[i].at[block],
                send_sem=send_sems.at[i, q], recv_sem=recv_sems.at[i, q],
                device_id=(x, y, 1 - c), device_id_type=MESH_ID)

        sent = [copy(i, q, c) for i in range(n) for q in range(len(_ALL_CHIPS))]
        for cp in sent:
            cp.start()
        for i in range(n):
            for q in range(len(_ALL_CHIPS)):
                copy(i, q, 1 - c).wait_recv()
        for cp in sent:
            cp.wait_send()

    return pl.pallas_call(
        body, name="gather_d2d",
        out_shape=[_sds(a.shape, a.dtype) for a in lands],
        in_specs=[_ANY] * n, out_specs=[_ANY] * n,
        input_output_aliases={i: i for i in range(n)},
        scratch_shapes=[pltpu.SemaphoreType.DMA((n, 4)), pltpu.SemaphoreType.DMA((n, 4))],
    )(*lands)


def _partials_d2d(parts):
    n = len(parts)
    half = N_DEV // 2

    def body(*refs):
        ins, outs, send_sems, recv_sems = refs[:n], refs[n:2 * n], refs[2 * n], refs[2 * n + 1]
        x, y, c = _position()

        def copy(i, q):
            return pltpu.make_async_remote_copy(
                src_ref=ins[i].at[_linear(*_ALL_CHIPS[q], 1 - c)], dst_ref=outs[i].at[q],
                send_sem=send_sems.at[i, q], recv_sem=recv_sems.at[i, q],
                device_id=(x, y, 1 - c), device_id_type=MESH_ID)

        sent = [copy(i, q) for i in range(n) for q in range(half)]
        for cp in sent:
            cp.start()
        for cp in sent:
            cp.wait_recv()
        for cp in sent:
            cp.wait_send()

    return pl.pallas_call(
        body, name="partials_d2d",
        out_shape=[_sds((half,) + p.shape[1:], p.dtype) for p in parts],
        in_specs=[_ANY] * n, out_specs=[_ANY] * n,
        scratch_shapes=[pltpu.SemaphoreType.DMA((n, half)), pltpu.SemaphoreType.DMA((n, half))],
    )(*parts)


def _partials_peers(x, y, c, direct):
    chips = _other_chips(x, y)
    if not direct:
        return [((*ch, c), 2 * ch[0] + ch[1]) for ch in chips]
    peers = [(x, y, 1 - c)] + [(*ch, c) for ch in chips] + [(*ch, 1 - c) for ch in chips]
    return [(p, _linear(*p)) for p in peers]


def _partials_copies(srcs, lands, send_sems, recv_sems, direct):
    x, y, c = _position()
    peers = _partials_peers(x, y, c, direct)
    return [pltpu.make_async_remote_copy(
        src_ref=srcs[i].at[block], dst_ref=lands[i].at[k],
        send_sem=send_sems.at[len(peers) * i + k], recv_sem=recv_sems.at[len(peers) * i + k],
        device_id=peer, device_id_type=MESH_ID)
        for i in range(len(srcs)) for k, (peer, block) in enumerate(peers)]


def _partials_send_start(name, srcs, lands, direct, after=None):
    n = len(srcs)
    n_sem = n * (N_DEV - 1 if direct else len(_ALL_CHIPS) - 1)

    def body(*refs):
        _, send_sems, recv_sems = refs[:2 * n], refs[-2 * n - 3], refs[-2 * n - 2]
        for cp in _partials_copies(refs[:n], refs[n:2 * n], send_sems, recv_sems, direct):
            cp.start()
        refs[-1][...] = jnp.zeros_like(refs[-1])

    both = list(srcs) + list(lands)
    extra = [] if after is None else [after]
    out = pl.pallas_call(
        body, name=name,
        out_shape=(pltpu.SemaphoreType.DMA((n_sem,)), pltpu.SemaphoreType.DMA((n_sem,)),
                   *[pltpu.HBM(a.shape, a.dtype) for a in both], _token()),
        in_specs=[_HBM] * (2 * n) + [_ANY] * len(extra), out_specs=(_SEM, _SEM, *[_HBM] * (2 * n), _VMEM),
        input_output_aliases={i: 2 + i for i in range(2 * n)},
        compiler_params=pltpu.CompilerParams(has_side_effects=_EFFECT),
    )(*[_in_hbm(a) for a in both], *extra)
    return out[0], out[1], list(out[2:2 + n]), list(out[2 + n:2 + 2 * n]), out[-1]


def _partials_send_wait(name, srcs, lands, send_sems, recv_sems, direct, after):
    n = len(srcs)

    def body(*refs):
        for cp in _partials_copies(refs[:n], refs[n:2 * n], refs[2 * n], refs[2 * n + 1], direct):
            cp.wait_send()
            cp.wait_recv()

    both = list(srcs) + list(lands)
    out = pl.pallas_call(
        body, name=name,
        out_shape=[pltpu.HBM(a.shape, a.dtype) for a in both],
        in_specs=[_HBM] * (2 * n) + [_SEM, _SEM, _ANY], out_specs=[_HBM] * (2 * n),
        input_output_aliases={i: i for i in range(2 * n)},
        compiler_params=pltpu.CompilerParams(has_side_effects=_EFFECT),
    )(*both, send_sems, recv_sems, after)
    return list(out[:n]), list(out[n:])


def _pack(arrs):
    flat = jnp.concatenate([a.reshape(-1).astype(F32) for a in arrs])
    block = OPT_ROW_BLOCK if flat.shape[0] > OPT_ROW_BLOCK * LANES else SUBLANES
    pad = (-flat.shape[0]) % (block * LANES)
    if pad:
        flat = jnp.concatenate([flat, jnp.zeros((pad,), F32)])
    return flat.reshape(-1, LANES)


def _unpack(packed, shapes, lead=()):
    flat = packed.reshape(lead + (-1,))
    out, off = [], 0
    for s in shapes:
        size = int(np.prod(s))
        out.append(flat[..., off:off + size].reshape(lead + tuple(s)))
        off += size
    return out


def _unshard_last(g):
    nd = g.ndim
    perm = tuple(range(1, nd - 1)) + (0, nd - 1)
    t = jnp.transpose(g, perm)
    return t.reshape(t.shape[:-2] + (N_DEV * g.shape[-1],))


def kernel(x, c, w_ada, b_ada, ln_g, ln_b, ffn_gu, ffn_down, gmlp_w_in, gmlp_b_in, gmlp_ln_g, gmlp_ln_b, gmlp_w_s, gmlp_b_s, gmlp_w_out, w_ada_kv, b_ada_kv, w_kv, attn_w_q, attn_rel_bias, attn_w_o, loss_target, m_w_ada, m_b_ada, m_ln_g, m_ln_b, m_ffn_gu, m_ffn_down, m_gmlp_w_in, m_gmlp_b_in, m_gmlp_ln_g, m_gmlp_ln_b, m_gmlp_w_s, m_gmlp_b_s, m_gmlp_w_out, m_w_ada_kv, m_b_ada_kv, m_w_kv, m_attn_w_q, m_attn_rel_bias, m_attn_w_o, v_w_ada, v_b_ada, v_ln_g, v_ln_b, v_ffn_gu, v_ffn_down, v_gmlp_w_in, v_gmlp_b_in, v_gmlp_ln_g, v_gmlp_ln_b, v_gmlp_w_s, v_gmlp_b_s, v_gmlp_w_out, v_w_ada_kv, v_b_ada_kv, v_w_kv, v_attn_w_q, v_attn_rel_bias, v_attn_w_o):
    weights = dict(w_ada=w_ada, b_ada=b_ada, ln_g=ln_g, ln_b=ln_b, ffn_gu=ffn_gu, ffn_down=ffn_down,
                   gmlp_w_in=gmlp_w_in, gmlp_b_in=gmlp_b_in, gmlp_ln_g=gmlp_ln_g, gmlp_ln_b=gmlp_ln_b,
                   gmlp_w_s=gmlp_w_s, gmlp_b_s=gmlp_b_s, gmlp_w_out=gmlp_w_out, w_ada_kv=w_ada_kv,
                   b_ada_kv=b_ada_kv, w_kv=w_kv, attn_w_q=attn_w_q, attn_rel_bias=attn_rel_bias,
                   attn_w_o=attn_w_o)
    mom1 = dict(w_ada=m_w_ada, b_ada=m_b_ada, ln_g=m_ln_g, ln_b=m_ln_b, ffn_gu=m_ffn_gu, ffn_down=m_ffn_down,
                gmlp_w_in=m_gmlp_w_in, gmlp_b_in=m_gmlp_b_in, gmlp_ln_g=m_gmlp_ln_g, gmlp_ln_b=m_gmlp_ln_b,
                gmlp_w_s=m_gmlp_w_s, gmlp_b_s=m_gmlp_b_s, gmlp_w_out=m_gmlp_w_out, w_ada_kv=m_w_ada_kv,
                b_ada_kv=m_b_ada_kv, w_kv=m_w_kv, attn_w_q=m_attn_w_q, attn_rel_bias=m_attn_rel_bias,
                attn_w_o=m_attn_w_o)
    mom2 = dict(w_ada=v_w_ada, b_ada=v_b_ada, ln_g=v_ln_g, ln_b=v_ln_b, ffn_gu=v_ffn_gu, ffn_down=v_ffn_down,
                gmlp_w_in=v_gmlp_w_in, gmlp_b_in=v_gmlp_b_in, gmlp_ln_g=v_gmlp_ln_g, gmlp_ln_b=v_gmlp_ln_b,
                gmlp_w_s=v_gmlp_w_s, gmlp_b_s=v_gmlp_b_s, gmlp_w_out=v_gmlp_w_out, w_ada_kv=v_w_ada_kv,
                b_ada_kv=v_b_ada_kv, w_kv=v_w_kv, attn_w_q=v_attn_w_q, attn_rel_bias=v_attn_rel_bias,
                attn_w_o=v_attn_w_o)
    order = list(weights)

    x = x[0]
    target = loss_target[0]
    t, d = x.shape
    n_mod = w_ada.shape[-1] * N_DEV // d
    mod_w = w_ada.shape[-1]
    kv_w = w_ada_kv.shape[-1]
    n_b = DEPTH - N_A
    me = _linear(*_position())

    l2 = DEPTH * 2
    big = dict(
        ffn_gu=ffn_gu.reshape((l2,) + ffn_gu.shape[2:]),
        ffn_down=ffn_down.reshape((l2,) + ffn_down.shape[2:]),
        gmlp_w_in=gmlp_w_in, gmlp_w_out=gmlp_w_out, w_kv=w_kv[None],
        attn_w_q=attn_w_q, attn_w_o=attn_w_o)
    big_names = list(big)
    core = lax.axis_index("c").astype(jnp.int32).reshape(1)
    chip = (2 * lax.axis_index("x") + lax.axis_index("y")).astype(jnp.int32).reshape(1)

    fwd_groups = [
        {"ffn_gu": (0, 1), "ffn_down": (0, 1)},
        {"gmlp_w_in": (0, 1), "gmlp_w_out": (0, 1), "ffn_gu": (1, 1), "ffn_down": (1, 1)},
        {"ffn_gu": (2, 2), "ffn_down": (2, 2), "gmlp_w_in": (1, 1), "gmlp_w_out": (1, 1), "w_kv": (0, 1)},
        {"ffn_gu": (4, 2), "ffn_down": (4, 2), "attn_w_q": (0, 1), "attn_w_o": (0, 1)},
        {"ffn_gu": (6, 2), "ffn_down": (6, 2), "attn_w_q": (1, 1), "attn_w_o": (1, 1)},
    ]
    bwd_groups = []
    for l in range(DEPTH):
        g = {"ffn_gu": (2 * l, 2), "ffn_down": (2 * l, 2)}
        if l < N_A:
            g.update({"gmlp_w_in": (l, 1), "gmlp_w_out": (l, 1)})
        else:
            g.update({"attn_w_q": (l - N_A, 1), "attn_w_o": (l - N_A, 1)})
        if l == N_A - 1:
            g["w_kv"] = (0, 1)
        bwd_groups.append(g)

    def slot_of(groups, name, slot):
        for gi, g in enumerate(groups):
            if name in g and g[name][0] <= slot < g[name][0] + g[name][1]:
                return gi, slot - g[name][0]
        raise KeyError((name, slot))

    flights = []
    for gi, g in enumerate(fwd_groups):
        lands = []
        for name, (s0, cnt) in g.items():
            shard = big[name][s0:s0 + cnt].astype(BF16)
            land = lax.empty((N_DEV,) + shard.shape, BF16)
            lands.append(lax.dynamic_update_slice(land, shard[None], (me,) + (0,) * shard.ndim))
        flights.append(_gather_ici_start(f"gather_ici_start_{gi}", lands))
    start_token = sum(f[3][0, 0] for f in flights)
    gathered = [None] * len(fwd_groups)

    def land_group(gi, after):
        send_sems, recv_sems, lands, _ = flights[gi]
        lands = _gather_ici_wait(f"gather_ici_wait_{gi}", lands, send_sems, recv_sems, after)
        gathered[gi] = dict(zip(fwd_groups[gi], _gather_d2d(lands)))

    def weight(name, slot):
        gi, local = slot_of(fwd_groups, name, slot)
        return gathered[gi][name], local

    partial = [{name: lax.empty((N_DEV, cnt) + big[name].shape[1:], BF16) for name, (_, cnt) in g.items()}
               for g in bwd_groups]

    c_all = _all_gather_small(_pack([c]) + start_token)
    c_all = _unpack(c_all, [(d,)], lead=(N_DEV,))[0]
    c4 = _as4(c_all)
    mod_part = _matmul("ada_fwd", c4, w_ada[:, None], (DEPTH, 1, N_DEV, mod_w), F32, a_silu=True)
    kv_part = _matmul("ada_kv_fwd", c4, _as4(w_ada_kv), (1, 1, N_DEV, kv_w), F32, a_silu=True)
    small_shapes = [mod_part.shape, kv_part.shape, ln_g.shape, ln_b.shape, gmlp_b_in.shape,
                    gmlp_ln_g.shape, gmlp_ln_b.shape, attn_rel_bias.shape]
    small = _all_gather_small(_pack([mod_part, kv_part, ln_g, ln_b, gmlp_b_in, gmlp_ln_g, gmlp_ln_b,
                                     attn_rel_bias]))
    (mod_g, kvm_g, ln_g_g, ln_b_g, b_in_g, gln_g_g, gln_b_g, rel_g) = _unpack(small, small_shapes, lead=(N_DEV,))
    mod_mine = lax.dynamic_index_in_dim(mod_g[:, :, 0], me, axis=2, keepdims=False)
    mod = _unshard_last(mod_mine) + b_ada
    mod = mod.reshape(DEPTH, n_mod, 1, d)
    kvm_mine = lax.dynamic_index_in_dim(kvm_g[:, 0, 0], me, axis=1, keepdims=False)
    mkv = (_unshard_last(kvm_mine) + b_ada_kv).reshape(2, 1, d)
    ln_g_f = _unshard_last(ln_g_g)
    ln_b_f = _unshard_last(ln_b_g)
    half = N_DEV // 2
    b_in_f = jnp.transpose(b_in_g, (1, 0, 2))[:, :, None, :]
    gln_g_f = _unshard_last(gln_g_g).reshape(N_A, half, 1, -1)
    gln_b_f = _unshard_last(gln_b_g).reshape(N_A, half, 1, -1)
    rel_f = _unshard_last(rel_g)

    def shard_act(a):
        return a.reshape(a.shape[0], a.shape[2], a.shape[3])

    def grad_into(name, slot, mm):
        gi, local = slot_of(bwd_groups, name, slot)
        partial[gi][name] = mm(partial[gi][name], local)

    def ffn_fwd(h, lw):
        w_gu, l_gu = weight("ffn_gu", lw)
        w_dn, l_dn = weight("ffn_down", lw)
        n = w_gu.shape[-1]
        gu = _matmul("ffn_gu_fwd", _as4(h), w_gu, (N_DEV, 1, t, n), F32, lb=l_gu)
        a = _swiglu_fwd(shard_act(gu))
        y = _matmul("ffn_down_fwd", a[:, None], w_dn, (1, 1, t, d), F32, lb=l_dn, b_merge=2, reduce=True)
        return y[0, 0], (gu, a)

    def ffn_bwd(dy, h, saved, lw):
        gu, a = saved
        w_gu, l_gu = weight("ffn_gu", lw)
        w_dn, l_dn = weight("ffn_down", lw)
        n = w_gu.shape[-1]
        da = _matmul("ffn_down_bwd_a", _as4(dy), w_dn, (half, 1, t, n), F32, lb=l_dn, b_merge=2, tb=True)
        grad_into("ffn_down", lw, lambda buf, lo: _matmul(
            "ffn_down_bwd_w", a[:, None], _as4(dy), buf.shape, BF16, ta=True, lo=lo, out_merge=2, out_buf=buf))
        dgu = _swiglu_bwd(shard_act(gu), shard_act(da))
        dh = _matmul("ffn_gu_bwd_a", dgu[:, None], w_gu, (1, 1, t, d), F32, lb=l_gu, tb=True, reduce=True)
        grad_into("ffn_gu", lw, lambda buf, lo: _matmul(
            "ffn_gu_bwd_w", _as4(h), dgu[:, None], buf.shape, BF16, ta=True, lo=lo, out_buf=buf))
        return dh[0, 0], {}

    def gmlp_params(l):
        return (b_in_f[l], gln_g_f[l], gln_b_f[l], gmlp_w_s[l], gmlp_b_s[l][:, :, None])

    def gmlp_fwd(h, l):
        w_in, l_in = weight("gmlp_w_in", l)
        w_out, l_out = weight("gmlp_w_out", l)
        n = w_in.shape[-1]
        zpre = _matmul("gmlp_in_fwd", _as4(h), w_in, (N_DEV, 1, t, n), F32, lb=l_in)
        gated = _gmlp_mid_fwd(shard_act(zpre), *gmlp_params(l))
        y = _matmul("gmlp_out_fwd", gated[:, None], w_out, (1, 1, t, d), F32, lb=l_out, b_merge=2, reduce=True)
        return y[0, 0], (zpre, gated)

    def gmlp_bwd(dy, h, saved, l):
        zpre, gated = saved
        w_in, l_in = weight("gmlp_w_in", l)
        w_out, l_out = weight("gmlp_w_out", l)
        n = w_in.shape[-1]
        dgated = _matmul("gmlp_out_bwd_a", _as4(dy), w_out, (half, 1, t, n), F32, lb=l_out, b_merge=2, tb=True)
        grad_into("gmlp_w_out", l, lambda buf, lo: _matmul(
            "gmlp_out_bwd_w", gated[:, None], _as4(dy), buf.shape, BF16, ta=True, lo=lo, out_merge=2, out_buf=buf))
        dz, dws, dbs, dlng, dlnb, dbin = _gmlp_mid_bwd(shard_act(zpre), shard_act(dgated), *gmlp_params(l))
        dh = _matmul("gmlp_in_bwd_a", dz[:, None], w_in, (1, 1, t, d), F32, lb=l_in, tb=True, reduce=True)
        grad_into("gmlp_w_in", l, lambda buf, lo: _matmul(
            "gmlp_in_bwd_w", _as4(h), dz[:, None], buf.shape, BF16, ta=True, lo=lo, out_buf=buf))
        small_grads = dict(gmlp_w_s=dws, gmlp_b_s=dbs[:, :, 0], gmlp_ln_g=dlng.reshape(-1),
                           gmlp_ln_b=dlnb.reshape(-1), gmlp_b_in=dbin.reshape(-1))
        return dh[0, 0], small_grads

    def attn_fwd(h, j, kvp):
        rel_vec = _rel_vector(rel_f[j])
        w_q, l_q = weight("attn_w_q", j)
        w_o, l_o = weight("attn_w_o", j)
        q = _matmul("attn_q_fwd", _as4(h), w_q, (1, 1, t, d), BF16, lb=l_q, b_merge=N_DEV, reduce=True)[0, 0]
        o = _attn_fwd(q, kvp, rel_vec)
        y = _matmul("attn_o_fwd", _as4(o), w_o, (1, 1, t, d), F32, lb=l_o, b_merge=N_DEV, reduce=True)
        return y[0, 0], (q, o, rel_vec)

    def attn_bwd(dy, h, saved, j, kvp, dkv_acc):
        q, o, rel_vec = saved
        w_q, l_q = weight("attn_w_q", j)
        w_o, l_o = weight("attn_w_o", j)
        do = _matmul("attn_o_bwd_a", _as4(dy), w_o, (1, 1, t, d), BF16, lb=l_o, b_merge=N_DEV, tb=True)[0, 0]
        grad_into("attn_w_o", j, lambda buf, lo: _matmul(
            "attn_o_bwd_w", _as4(o), _as4(dy), buf.shape, BF16, ta=True, lo=lo, out_merge=N_DEV, out_buf=buf))
        dq, dk, dv, dsc = _attn_bwd(q, do, kvp, rel_vec, *dkv_acc)
        drel = _rel_bias_grad(_skew_diagonals(dsc))
        dh = _matmul("attn_q_bwd_a", _as4(dq), w_q, (1, 1, t, d), F32, lb=l_q, b_merge=N_DEV, tb=True)
        grad_into("attn_w_q", j, lambda buf, lo: _matmul(
            "attn_q_bwd_w", _as4(h), _as4(dq), buf.shape, BF16, ta=True, lo=lo, out_merge=N_DEV, out_buf=buf))
        return dh[0, 0], dict(attn_rel_bias=drel, dkv=(dk, dv))

    tape = []
    kvp = None
    kv_tape = None
    first_use = {(0, 0): 0, (0, 1): 1, (1, 0): 2, (2, 0): 3, (3, 0): 4}
    h = _modulate(x, mod[0, 1], mod[0, 0])
    for l in range(DEPTH):
        for i in range(3):
            if (l, i) in first_use:
                land_group(first_use[l, i], x)
            scl, gate = mod[l, 3 * i + 1], mod[l, 3 * i + 2]
            wgt = 1.0 if i == 1 else 0.5
            gw = wgt * (1.0 + gate)
            if i != 1:
                y, saved = ffn_fwd(h, 2 * l + i // 2)
            elif l < N_A:
                y, saved = gmlp_fwd(h, l)
            else:
                y, saved = attn_fwd(h, l - N_A, kvp)
            nl, ni = (l, i + 1) if i < 2 else (l + 1, 0)
            readers = [(mod[nl, 3 * ni + 1], mod[nl, 3 * ni])] if nl < DEPTH else []
            shared_kv = (l, i) == (N_A - 1, 2)
            if shared_kv:
                readers.append((mkv[1], mkv[0]))
            outs = _ln_res_fwd(x, y, gw, ln_g_f[l, i][None], ln_b_f[l, i][None], readers)
            tape.append((x, h, y, gw, scl, saved))
            x = outs[0]
            h = outs[1] if nl < DEPTH else None
            if shared_kv:
                hkv = outs[-1]
                w_kvg, l_kv = weight("w_kv", 0)
                n = w_kvg.shape[-1]
                kv = _matmul("kv_fwd", _as4(hkv), w_kvg, (N_DEV, 1, t, n), BF16, lb=l_kv)
                kvp = jnp.pad(shard_act(kv), ((0, 0), (LEFT_PAD, 0), (0, 0)))
                kv_tape = hkv

    loss_part, dx = _loss_head(x, target)
    loss = lax.psum(loss_part[0, 0], MESH_AXES)

    d_mod = [[None] * n_mod for _ in range(DEPTH)]
    d_ln_g = [[None] * 3 for _ in range(DEPTH)]
    d_ln_b = [[None] * 3 for _ in range(DEPTH)]
    small_grads = {k: [None] * N_A for k in ("gmlp_w_s", "gmlp_b_s", "gmlp_ln_g", "gmlp_ln_b", "gmlp_b_in")}
    d_rel = [None] * n_b
    dkv_acc = ()
    d_mkv = None
    reductions = [None] * DEPTH
    sent_token = None
    readers = []
    for l in reversed(range(DEPTH)):
        if l == N_A - 1:
            hkv = kv_tape
            w_kvg, l_kv = weight("w_kv", 0)
            dkv = jnp.concatenate(dkv_acc)[:, LEFT_PAD:, :].astype(BF16)[:, None]
            dhkv = _matmul("kv_bwd_a", dkv, w_kvg, (1, 1, t, d), F32, lb=l_kv, tb=True, reduce=True)[0, 0]
            grad_into("w_kv", 0, lambda buf, lo: _matmul(
                "kv_bwd_w", _as4(hkv), dkv, buf.shape, BF16, ta=True, lo=lo, out_buf=buf))
            readers.append((dhkv, mkv[1], None))
        for i in reversed(range(3)):
            x_in, h, y, gw, scl, saved = tape[3 * l + i]
            wgt = 1.0 if i == 1 else 0.5
            if sent_token is not None:
                gw = gw + sent_token
                sent_token = None
            res = _ln_res_bwd(x_in, y, gw, ln_g_f[l, i][None], ln_b_f[l, i][None], dx,
                              [(r[0], r[1]) for r in readers])
            dx_res, dy, dgw, dg, db = res[:5]
            for k, (_, _, slot) in enumerate(readers):
                dscl_k, dshift_k = res[5 + 2 * k][0], res[6 + 2 * k][0]
                if slot is None:
                    d_mkv = jnp.concatenate([dshift_k, dscl_k])
                else:
                    d_mod[slot[0]][slot[1]], d_mod[slot[0]][slot[1] + 1] = dshift_k, dscl_k
            d_ln_g[l][i], d_ln_b[l][i] = dg[0], db[0]
            if i != 1:
                dh, extra = ffn_bwd(dy, h, saved, 2 * l + i // 2)
            elif l < N_A:
                dh, extra = gmlp_bwd(dy, h, saved, l)
                for k, g in extra.items():
                    small_grads[k][l] = g
            else:
                dh, extra = attn_bwd(dy, h, saved, l - N_A, kvp, dkv_acc)
                d_rel[l - N_A] = extra["attn_rel_bias"]
                dkv_acc = extra["dkv"]
            d_mod[l][3 * i + 2] = wgt * dgw[0]
            dx = dx_res
            readers = [(dh, scl, (l, 3 * i))]
        if l > 0:
            srcs = [partial[l][k] for k in bwd_groups[l]]
            lands = [lax.empty((N_DEV - 1,) + s.shape[1:], BF16) for s in srcs]
            reductions[l] = _partials_send_start(f"partials_send_start_{l}", srcs, lands, True)
            sent_token = reductions[l][4][0, 0]
    (dh, scl, _), = readers
    dx, dscl, dshift = _mod_bwd(dx, dh, tape[0][0], scl)
    d_mod[0][0], d_mod[0][1] = dshift[0], dscl[0]
    grad_x = dx[None]

    d_mod_arr = jnp.stack([jnp.concatenate(r) for r in d_mod])
    small_part = dict(
        b_ada=d_mod_arr, b_ada_kv=d_mkv,
        ln_g=jnp.stack([jnp.stack(r) for r in d_ln_g]), ln_b=jnp.stack([jnp.stack(r) for r in d_ln_b]),
        gmlp_b_in=jnp.stack(small_grads["gmlp_b_in"]), gmlp_ln_g=jnp.stack(small_grads["gmlp_ln_g"]),
        gmlp_ln_b=jnp.stack(small_grads["gmlp_ln_b"]), gmlp_w_s=jnp.stack(small_grads["gmlp_w_s"]),
        gmlp_b_s=jnp.stack(small_grads["gmlp_b_s"]), attn_rel_bias=jnp.stack(d_rel))
    small_names = list(small_part)
    sp_shapes = [small_part[k].shape for k in small_names]
    sp_all = _all_gather_small(_pack([small_part[k] for k in small_names]))

    from_sibling = _partials_d2d([partial[0][k] for k in bwd_groups[0]])
    sums = [_chip_sum(partial[0][k], r1, core) for k, r1 in zip(bwd_groups[0], from_sibling)]
    lands = [lax.empty((len(_ALL_CHIPS) - 1,) + s.shape[1:], BF16) for s in sums]
    reductions[0] = _partials_send_start("partials_send_start_0", sums, lands, False, after=sp_all)
    sent_token = reductions[0][4][0, 0]
    c4 = c4 + sent_token

    sp_sum = _sum_parts(sp_all)
    full_grads = dict(zip(small_names, _unpack(sp_sum, sp_shapes)))
    per_dev = dict(zip(small_names, _unpack(sp_all, sp_shapes, lead=(N_DEV,))))

    def my_cols(a, width):
        return lax.dynamic_slice_in_dim(a, me * width, width, axis=a.ndim - 1)

    grads = {}
    grads["b_ada"] = full_grads["b_ada"]
    grads["b_ada_kv"] = full_grads["b_ada_kv"]
    grads["gmlp_w_s"] = full_grads["gmlp_w_s"]
    grads["gmlp_b_s"] = full_grads["gmlp_b_s"]
    for k in ("ln_g", "ln_b", "gmlp_b_in", "gmlp_ln_g", "gmlp_ln_b", "attn_rel_bias"):
        grads[k] = my_cols(full_grads[k], weights[k].shape[-1])

    dmod_cols = jnp.transpose(my_cols(per_dev["b_ada"], mod_w), (1, 0, 2))[:, None]
    grads["w_ada"] = _matmul("ada_bwd_w", c4, dmod_cols, (DEPTH, 1, d, mod_w), F32, ta=True,
                             a_silu=True)[:, 0]
    dkv_cols = my_cols(per_dev["b_ada_kv"], kv_w)[None, None]
    grads["w_ada_kv"] = _matmul("ada_kv_bwd_w", c4, dkv_cols, (1, 1, d, kv_w), F32, ta=True,
                                a_silu=True)[0, 0]

    delta, new_m, new_v = {}, {}, {}
    first = jnp.zeros((1,), jnp.int32)

    def flat2(a, cols):
        return a.reshape(-1, cols)

    done = None
    for k in ("w_ada", "w_ada_kv"):
        w = weights[k]
        cols = w.shape[-1]
        res = _adamw(grads[k].reshape(1, -1, cols), first, None, flat2(w, cols), flat2(mom1[k], cols),
                     flat2(mom2[k], cols), after=done)
        grads[k], delta[k], new_m[k], new_v[k] = (a.reshape(w.shape) for a in res)
        done = res[0][:SUBLANES, :LANES]

    tiny = [k for k in order if k not in delta and k not in big_names]
    tiny_shapes = [weights[k].shape for k in tiny]
    tiny_out = _adamw((_pack([grads[k] for k in tiny]) + sent_token)[None], first, None,
                      _pack([weights[k] for k in tiny]), _pack([mom1[k] for k in tiny]),
                      _pack([mom2[k] for k in tiny]), after=done)
    for dst, arr in zip((grads, delta, new_m, new_v), tiny_out):
        for k, val in zip(tiny, _unpack(arr, tiny_shapes)):
            dst[k] = val

    bufs = {k: [lax.empty(flat2(weights[k], weights[k].shape[-1]).shape, F32) for _ in range(4)]
            for k in big_names}
    done = tiny_out[0]
    me_idx = me.astype(jnp.int32).reshape(1)
    for l in reversed(range(DEPTH)):
        send_sems, recv_sems, srcs, lands, _ = reductions[l]
        srcs, lands = _partials_send_wait(f"partials_send_wait_{l}", srcs, lands, send_sems, recv_sems, l > 0, done)
        for k, own, got in zip(bwd_groups[l], srcs, lands):
            cols = weights[k].shape[-1]
            slot_rows = int(np.prod(big[k].shape[1:-1]))
            bufs[k] = _adamw(own.reshape(own.shape[0], -1, cols), me_idx if l > 0 else chip,
                             got.reshape(got.shape[0], -1, cols),
                             flat2(weights[k], cols), flat2(mom1[k], cols), flat2(mom2[k], cols),
                             row0=bwd_groups[l][k][0] * slot_rows, bufs=bufs[k], after=done)
            done = bufs[k][0][:SUBLANES, :LANES]
    for k in big_names:
        grads[k], delta[k], new_m[k], new_v[k] = (b.reshape(weights[k].shape) for b in bufs[k])

    return (loss, grad_x, *[grads[k] for k in order], *[delta[k] for k in order],
            *[new_m[k] for k in order], *[new_v[k] for k in order])
```

```python
import functools

import numpy as np
import jax
import jax.numpy as jnp
from jax import lax
from jax.experimental import pallas as pl
from jax.experimental.pallas import tpu as pltpu

F32 = jnp.float32
BF16 = jnp.bfloat16
MESH_AXES = ("x", "y", "c")
N_DEV = 8
MESH_ID = pl.DeviceIdType.MESH

DEPTH = 4
N_A = 2
CHUNK = 64
N_HEADS = 16
LEFT_CHUNKS = 8
BAND = (LEFT_CHUNKS + 1) * CHUNK
LEFT_PAD = LEFT_CHUNKS * CHUNK
MAX_REL = 4 * CHUNK
N_REL = (CHUNK - 1) + MAX_REL + 1
GMLP_WINDOW = 128
GMLP_GROUPS = 8
ALPHA = (2.0 * DEPTH) ** 0.25
LN_EPS = 1e-5
ADAM_LR = 0.001
ADAM_B1 = 0.9
ADAM_B2 = 0.999
ADAM_EPS = 1e-08
ADAM_WD = 0.01
ADAM_STEP = 10

V7X_VMEM_BYTES = 64 * 1024 * 1024
VMEM_LIMIT = V7X_VMEM_BYTES - 8 * 1024 * 1024
LANES = 128
SUBLANES = 8
MM_BLOCK = 2048
BIG_ROW_BLOCK = 1024
ROW_BLOCK = 512
OPT_ROW_BLOCK = 256

_ANY = pl.BlockSpec(memory_space=pl.ANY)
_VMEM = pl.BlockSpec(memory_space=pltpu.VMEM)


def _params(sem=None):
    return pltpu.CompilerParams(dimension_semantics=sem, vmem_limit_bytes=VMEM_LIMIT)


def _row_block(rows, target):
    for d in range(min(rows, target), 0, -1):
        if rows % d == 0 and (d % SUBLANES == 0 or d == rows):
            return d
    return rows


def _matmul(name, a, b, out_shape4, out_dtype, *, la=0, lb=0, lo=0, ta=False, tb=False,
            reduce=False, b_merge=1, out_merge=1, out_buf=None, a_silu=False):
    ja_n, _, a_r, a_c = a.shape
    jb_n, _, b_r, b_c = b.shape
    jo_n, _, o_r, o_c = out_shape4
    m_tot = a_c if ta else a_r
    k_a = a_r if ta else a_c
    b_rows = b_merge * b_r
    k_c = b_c if tb else b_rows
    n = b_rows if tb else b_c
    n_chunks = (jb_n // b_merge) if reduce else 1
    natural_k = reduce and ja_n == 1
    assert n == o_c, (name, n, o_c)
    assert k_a ==(k_c * n_chunks if natural_k else k_c), (name, k_a, k_c, n_chunks)
    bk = k_c if (k_c <= MM_BLOCK or (b_merge > 1 and not tb)) else MM_BLOCK
    assert k_c % bk == 0
    nkk = k_c // bk
    nk = n_chunks * nkk
    m_out = out_merge * o_r
    assert m_tot == m_out, (name, m_tot, m_out)
    bm = m_tot if (m_tot <= MM_BLOCK or out_merge > 1) else MM_BLOCK
    assert m_tot % bm == 0
    jo_blocks = jo_n // out_merge

    def a_index(j, m, k):
        kj, kk = k // nkk, k % nkk
        ja = 0 if ja_n == 1 else (kj if reduce else j)
        ke = kk + kj * nkk if natural_k else kk
        return (ja, la, ke, m) if ta else (ja, la, m, ke)

    def b_index(j, m, k):
        kj, kk = k // nkk, k % nkk
        jb = 0 if jb_n == b_merge else (kj if reduce else j)
        return (jb, lb, 0, kk) if tb else (jb, lb, kk, 0)

    def o_index(j, m, k):
        return (j, lo, 0, 0) if out_merge > 1 else (j, lo, m, 0)

    a_block = (None, None, bk, bm) if ta else (None, None, bm, bk)
    if b_merge > 1:
        b_block = (b_merge, None, b_r, bk if tb else n)
    else:
        b_block = (None, None, n, bk) if tb else (None, None, bk, n)
    o_block = (out_merge, None, o_r, n) if out_merge > 1 else (None, None, bm, n)
    dims = (((0 if ta else 1,), (1 if tb else 0,)), ((), ()))

    in_place = nk > 1 and out_dtype == F32 and out_merge == 1
    use_acc = nk > 1 and not in_place

    def body(a_ref, b_ref, *rest):
        o_ref = rest[-2] if use_acc else rest[-1]
        k = pl.program_id(2)
        av = a_ref[...]
        if a_silu:
            af = av.astype(F32)
            av = af * jax.nn.sigmoid(af)
        bv = b_ref[...]
        if b_merge > 1:
            bv = bv.reshape(b_rows, bv.shape[-1])
        prod = lax.dot_general(av.astype(BF16), bv.astype(BF16), dims, preferred_element_type=F32)

        def emit(val):
            val = val.astype(out_dtype)
            o_ref[...] = val.reshape(out_merge, o_r, n) if out_merge > 1 else val

        if nk == 1:
            emit(prod)
            return
        acc_ref = o_ref if in_place else rest[-1]

        @pl.when(k == 0)
        def _():
            acc_ref[...] = prod

        @pl.when(k > 0)
        def _():
            acc_ref[...] += prod

        if use_acc:
            @pl.when(k == nk - 1)
            def _():
                emit(acc_ref[...])

    in_specs = [pl.BlockSpec(a_block, a_index), pl.BlockSpec(b_block, b_index)]
    operands = [a, b]
    aliases = {}
    if out_buf is not None:
        assert out_buf.shape == tuple(out_shape4) and out_buf.dtype == out_dtype
        in_specs.append(_ANY)
        operands.append(out_buf)
        aliases = {2: 0}
    return pl.pallas_call(
        body, name=name,
        grid=(jo_blocks, m_tot // bm, nk),
        in_specs=in_specs,
        out_specs=pl.BlockSpec(o_block, o_index),
        out_shape=jax.ShapeDtypeStruct(tuple(out_shape4), out_dtype),
        scratch_shapes=[pltpu.VMEM((bm, n), F32)] if use_acc else [],
        input_output_aliases=aliases,
        compiler_params=_params(("parallel", "parallel", "arbitrary")),
    )(*operands)


def _as4(a):
    return a.reshape((1,) * (4 - a.ndim) + a.shape)


def _row_call(name, body, ins, outs, t, *, acc_outs=()):
    bt = _row_block(t, ROW_BLOCK)

    def spec(arr, tiled):
        if tiled:
            return pl.BlockSpec((bt,) + tuple(arr.shape[1:]), lambda i: (i,) + (0,) * (arr.ndim - 1))
        return pl.BlockSpec(tuple(arr.shape), lambda i: (0,) * arr.ndim)

    return pl.pallas_call(
        body, name=name, grid=(t // bt,),
        in_specs=[spec(a, tl) for a, tl in ins],
        out_specs=[spec(o, tl) for o, tl in outs],
        out_shape=[jax.ShapeDtypeStruct(o.shape, o.dtype) for o, _ in outs],
        compiler_params=_params(("arbitrary",) if acc_outs else ("parallel",)),
    )(*[a for a, _ in ins])


def _sds(shape, dtype):
    return jax.ShapeDtypeStruct(tuple(shape), dtype)


def _modulate(x, scl, shift):
    t, d = x.shape

    def body(x_ref, s_ref, b_ref, h_ref):
        h_ref[...] = (x_ref[...] * (1.0 + s_ref[...]) + b_ref[...]).astype(BF16)

    return _row_call("modulate", body, [(x, True), (scl, False), (shift, False)],
                     [(_sds((t, d), BF16), True)], t)[0]


def _ln_stats(r):
    mu = jnp.mean(r, axis=-1, keepdims=True)
    rc = r - mu
    var = jnp.mean(rc * rc, axis=-1, keepdims=True)
    rstd = lax.rsqrt(var + LN_EPS)
    return rc * rstd, rstd


def _ln_res_fwd(x, y, gw, g, b, mods=()):
    t, d = x.shape
    n_mod = len(mods)

    def body(x_ref, y_ref, gw_ref, g_ref, b_ref, *rest):
        mod_refs, o_ref, h_refs = rest[:2 * n_mod], rest[2 * n_mod], rest[2 * n_mod + 1:]
        r = ALPHA * x_ref[...] + gw_ref[...] * y_ref[...]
        xhat, _ = _ln_stats(r)
        xn = xhat * g_ref[...] + b_ref[...]
        o_ref[...] = xn
        for k in range(n_mod):
            h_refs[k][...] = (xn * (1.0 + mod_refs[2 * k][...]) + mod_refs[2 * k + 1][...]).astype(BF16)

    vecs = [(v, False) for pair in mods for v in pair]
    return _row_call("ln_res_fwd", body,
                     [(x, True), (y, True), (gw, False), (g, False), (b, False)] + vecs,
                     [(_sds((t, d), F32), True)] + [(_sds((t, d), BF16), True)] * n_mod, t)


def _ln_res_bwd(x, y, gw, g, b, dx_base, pairs=()):
    t, d = x.shape
    n_pair = len(pairs)

    def body(x_ref, y_ref, gw_ref, g_ref, b_ref, dxb_ref, *rest):
        pair_refs, outs = rest[:2 * n_pair], rest[2 * n_pair:]
        dx_ref, dy_ref = outs[0], outs[1]
        sums = outs[2:]

        @pl.when(pl.program_id(0) == 0)
        def _():
            for r in sums:
                r[...] = jnp.zeros_like(r)

        yv = y_ref[...]
        gwv = gw_ref[...]
        gv = g_ref[...]
        xhat, rstd = _ln_stats(ALPHA * x_ref[...] + gwv * yv)
        dxn = dxb_ref[...]
        if n_pair:
            xn = xhat * gv + b_ref[...]
            for k in range(n_pair):
                dh = pair_refs[2 * k][...]
                dxn = dxn + dh * (1.0 + pair_refs[2 * k + 1][...])
                sums[3 + 2 * k][...] += jnp.sum(dh * xn, axis=0, keepdims=True)
                sums[4 + 2 * k][...] += jnp.sum(dh, axis=0, keepdims=True)
        dxh = dxn * gv
        m1 = jnp.mean(dxh, axis=-1, keepdims=True)
        m2 = jnp.mean(dxh * xhat, axis=-1, keepdims=True)
        dr = rstd * (dxh - m1 - xhat * m2)
        dx_ref[...] = ALPHA * dr
        dy_ref[...] = (gwv * dr).astype(BF16)
        sums[0][...] += jnp.sum(dr * yv, axis=0, keepdims=True)
        sums[1][...] += jnp.sum(dxn * xhat, axis=0, keepdims=True)
        sums[2][...] += jnp.sum(dxn, axis=0, keepdims=True)

    vec = _sds((1, d), F32)
    n_sum = 3 + 2 * n_pair
    ins = [(x, True), (y, True), (gw, False), (g, False), (b, False), (dx_base, True)]
    for dh, scl in pairs:
        ins += [(dh, True), (scl, False)]
    return _row_call("ln_res_bwd", body, ins,
                     [(_sds((t, d), F32), True), (_sds((t, d), BF16), True)] + [(vec, False)] * n_sum, t,
                     acc_outs=tuple(range(2, 2 + n_sum)))


def _mod_bwd(dx_res, dh, x, scl):
    t, d = x.shape

    def body(dxr_ref, dh_ref, x_ref, s_ref, dx_ref, ds_ref, db_ref):
        @pl.when(pl.program_id(0) == 0)
        def _():
            ds_ref[...] = jnp.zeros_like(ds_ref)
            db_ref[...] = jnp.zeros_like(db_ref)

        dh = dh_ref[...]
        dx_ref[...] = dxr_ref[...] + dh * (1.0 + s_ref[...])
        ds_ref[...] += jnp.sum(dh * x_ref[...], axis=0, keepdims=True)
        db_ref[...] += jnp.sum(dh, axis=0, keepdims=True)

    vec = _sds((1, d), F32)
    return _row_call("mod_bwd", body, [(dx_res, True), (dh, True), (x, True), (scl, False)],
                     [(_sds((t, d), F32), True), (vec, False), (vec, False)], t, acc_outs=(1, 2))


def _loss_head(y, target):
    t, d = y.shape

    def body(y_ref, t_ref, l_ref, dy_ref):
        @pl.when(pl.program_id(0) == 0)
        def _():
            l_ref[...] = jnp.zeros_like(l_ref)

        err = y_ref[...] - t_ref[...]
        dy_ref[...] = err * (1.0 / d)
        part = 0.5 * jnp.sum(jnp.mean(err * err, axis=-1, keepdims=True), axis=0, keepdims=True)
        l_ref[...] += jnp.broadcast_to(part, l_ref.shape)

    return _row_call("loss_head", body, [(y, True), (target, True)],
                     [(_sds((SUBLANES, LANES), F32), False), (_sds((t, d), F32), True)], t,
                     acc_outs=(0,))


def _swiglu_fwd(gu):
    _, t, n = gu.shape
    half = N_DEV // 2
    bt = _row_block(t, BIG_ROW_BLOCK)
    gu4 = gu.reshape(2, half, t, n)

    def body(gu_ref, a_ref):
        g = gu_ref[0]
        a_ref[...] = (g * jax.nn.sigmoid(g) * gu_ref[1]).astype(BF16)

    return pl.pallas_call(
        body, name="swiglu_fwd", grid=(half, t // bt),
        in_specs=[pl.BlockSpec((2, None, bt, n), lambda j, i: (0, j, i, 0))],
        out_specs=pl.BlockSpec((None, bt, n), lambda j, i: (j, i, 0)),
        out_shape=_sds((half, t, n), BF16),
        compiler_params=_params(("parallel", "parallel")),
    )(gu4)


def _swiglu_bwd(gu, da):
    _, t, n = gu.shape
    half = N_DEV // 2
    bt = _row_block(t, BIG_ROW_BLOCK)
    gu4 = gu.reshape(2, half, t, n)

    def body(gu_ref, da_ref, d_ref):
        g = gu_ref[0]
        u = gu_ref[1]
        da = da_ref[...]
        sig = jax.nn.sigmoid(g)
        d_ref[0] = (da * u * sig * (1.0 + g * (1.0 - sig))).astype(BF16)
        d_ref[1] = (da * g * sig).astype(BF16)

    out = pl.pallas_call(
        body, name="swiglu_bwd", grid=(half, t // bt),
        in_specs=[pl.BlockSpec((2, None, bt, n), lambda j, i: (0, j, i, 0)),
                  pl.BlockSpec((None, bt, n), lambda j, i: (j, i, 0))],
        out_specs=pl.BlockSpec((2, None, bt, n), lambda j, i: (0, j, i, 0)),
        out_shape=_sds((2, half, t, n), BF16),
        compiler_params=_params(("parallel", "parallel")),
    )(gu4, da)
    return out.reshape(N_DEV, t, n)


_INV_SQRT2 = 0.7071067811865476
_INV_SQRT_2PI = 0.3989422804014327


def _gelu(z):
    return 0.5 * z * (1.0 + lax.erf(z * _INV_SQRT2))


def _gelu_grad(z):
    return 0.5 * (1.0 + lax.erf(z * _INV_SQRT2)) + z * jnp.exp(-0.5 * z * z) * _INV_SQRT_2PI


def _window_mask():
    t_out = lax.broadcasted_iota(jnp.int32, (GMLP_WINDOW, GMLP_WINDOW), 0)
    s_in = lax.broadcasted_iota(jnp.int32, (GMLP_WINDOW, GMLP_WINDOW), 1)
    return (s_in // CHUNK) <= (t_out // CHUNK)


def _gmlp_recompute(z_ref, bin_ref, lng_ref, lnb_ref):
    half = N_DEV // 2
    z = z_ref[...] + bin_ref[...]
    ge = _gelu(z)
    u = ge[:half]
    v = ge[half:]
    width = half * v.shape[-1]
    mu = jnp.sum(jnp.sum(v, axis=0), axis=-1, keepdims=True) / width
    vc = v - mu
    var = jnp.sum(jnp.sum(vc * vc, axis=0), axis=-1, keepdims=True) / width
    rstd = lax.rsqrt(var + LN_EPS)
    xhat = vc * rstd
    vn = xhat * lng_ref[...] + lnb_ref[...]
    return z, u, xhat, rstd, vn


def _gmlp_mid_fwd(zpre, b_in, ln_g, ln_b, w_s, b_s):
    _, t, n = zpre.shape
    half = N_DEV // 2
    gd = half * n // GMLP_GROUPS
    per = n // gd
    w = GMLP_WINDOW

    def body(z_ref, bin_ref, lng_ref, lnb_ref, ws_ref, bs_ref, o_ref):
        _, u, _, _, vn = _gmlp_recompute(z_ref, bin_ref, lng_ref, lnb_ref)
        mask = _window_mask()
        for g in range(GMLP_GROUPS):
            sh, c0 = g // per, (g % per) * gd
            wsm = jnp.where(mask, ws_ref[g], 0.0).astype(BF16)
            s = jnp.dot(wsm, vn[sh][:, c0:c0 + gd].astype(BF16), preferred_element_type=F32) + bs_ref[g]
            o_ref[sh, :, c0:c0 + gd] = (u[sh][:, c0:c0 + gd] * s).astype(BF16)

    whole = lambda a: pl.BlockSpec(tuple(a.shape), lambda i: (0,) * a.ndim)
    return pl.pallas_call(
        body, name="gmlp_mid_fwd", grid=(t // w,),
        in_specs=[pl.BlockSpec((N_DEV, w, n), lambda i: (0, i, 0)),
                  whole(b_in), whole(ln_g), whole(ln_b), whole(w_s), whole(b_s)],
        out_specs=pl.BlockSpec((half, w, n), lambda i: (0, i, 0)),
        out_shape=_sds((half, t, n), BF16),
        compiler_params=_params(("parallel",)),
    )(zpre, b_in, ln_g, ln_b, w_s, b_s)


def _gmlp_mid_bwd(zpre, dgated, b_in, ln_g, ln_b, w_s, b_s):
    _, t, n = zpre.shape
    half = N_DEV // 2
    gd = half * n // GMLP_GROUPS
    per = n // gd
    w = GMLP_WINDOW
    width = half * n

    def body(z_ref, dg_ref, bin_ref, lng_ref, lnb_ref, ws_ref, bs_ref,
             dz_ref, dws_ref, dbs_ref, dlng_ref, dlnb_ref, dbin_ref, du_ref, dvn_ref):
        @pl.when(pl.program_id(0) == 0)
        def _():
            for r in (dws_ref, dbs_ref, dlng_ref, dlnb_ref, dbin_ref):
                r[...] = jnp.zeros_like(r)

        z, u, xhat, rstd, vn = _gmlp_recompute(z_ref, bin_ref, lng_ref, lnb_ref)
        mask = _window_mask()
        for g in range(GMLP_GROUPS):
            sh, c0 = g // per, (g % per) * gd
            wsm = jnp.where(mask, ws_ref[g], 0.0).astype(BF16)
            vg = vn[sh][:, c0:c0 + gd].astype(BF16)
            s = jnp.dot(wsm, vg, preferred_element_type=F32) + bs_ref[g]
            dgt = dg_ref[sh, :, c0:c0 + gd]
            ds = dgt * u[sh][:, c0:c0 + gd]
            du_ref[sh, :, c0:c0 + gd] = dgt * s
            dsb = ds.astype(BF16)
            dws = lax.dot_general(dsb, vg, (((1,), (1,)), ((), ())), preferred_element_type=F32)
            dws_ref[g] += jnp.where(mask, dws, 0.0)
            dbs_ref[g] += jnp.sum(ds, axis=-1, keepdims=True)
            dvn_ref[sh, :, c0:c0 + gd] = lax.dot_general(wsm, dsb, (((0,), (0,)), ((), ())),
                                                         preferred_element_type=F32)
        dvn = dvn_ref[...]
        dlng_ref[...] += jnp.sum(dvn * xhat, axis=1, keepdims=True)
        dlnb_ref[...] += jnp.sum(dvn, axis=1, keepdims=True)
        dxh = dvn * lng_ref[...]
        m1 = jnp.sum(jnp.sum(dxh, axis=0), axis=-1, keepdims=True) / width
        m2 = jnp.sum(jnp.sum(dxh * xhat, axis=0), axis=-1, keepdims=True) / width
        dv = rstd * (dxh - m1 - xhat * m2)
        gg = _gelu_grad(z)
        dzu = du_ref[...] * gg[:half]
        dzv = dv * gg[half:]
        dz_ref[:half] = dzu.astype(BF16)
        dz_ref[half:] = dzv.astype(BF16)
        dbin_ref[:half] += jnp.sum(dzu, axis=1, keepdims=True)
        dbin_ref[half:] += jnp.sum(dzv, axis=1, keepdims=True)

    whole = lambda a: pl.BlockSpec(tuple(a.shape), lambda i: (0,) * a.ndim)
    outs = [_sds((N_DEV, t, n), BF16), _sds(w_s.shape, F32), _sds(b_s.shape, F32),
            _sds(ln_g.shape, F32), _sds(ln_b.shape, F32), _sds(b_in.shape, F32)]
    return pl.pallas_call(
        body, name="gmlp_mid_bwd", grid=(t // w,),
        in_specs=[pl.BlockSpec((N_DEV, w, n), lambda i: (0, i, 0)),
                  pl.BlockSpec((half, w, n), lambda i: (0, i, 0)),
                  whole(b_in), whole(ln_g), whole(ln_b), whole(w_s), whole(b_s)],
        out_specs=[pl.BlockSpec((N_DEV, w, n), lambda i: (0, i, 0))] + [whole(o) for o in outs[1:]],
        out_shape=outs,
        scratch_shapes=[pltpu.VMEM((half, w, n), F32), pltpu.VMEM((half, w, n), F32)],
        compiler_params=_params(("arbitrary",)),
    )(zpre, dgated, b_in, ln_g, ln_b, w_s, b_s)


ATTN_CHUNKS = 4
ATTN_ROWS = ATTN_CHUNKS * CHUNK
ATTN_WINDOW = ATTN_ROWS + LEFT_PAD
ATTN_DIAGS = 1024
ATTN_ROLL = ATTN_DIAGS - (ATTN_ROWS - 1)


def _rel_vector(rel):
    j = np.arange(ATTN_DIAGS)
    idx = np.clip(ATTN_WINDOW - 1 - j, -(CHUNK - 1), MAX_REL) + (CHUNK - 1)
    return rel[:, idx]


def _attn_bias_mask(rel_ref, bm_ref):
    tt = lax.broadcasted_iota(jnp.int32, (ATTN_ROWS, ATTN_WINDOW), 0) // CHUNK
    rr = lax.broadcasted_iota(jnp.int32, (ATTN_ROWS, ATTN_WINDOW), 1) // CHUNK
    band = (rr >= tt) & (rr <= tt + LEFT_CHUNKS)
    for j in range(bm_ref.shape[0]):
        vec = jnp.broadcast_to(rel_ref[j:j + 1, :], (ATTN_ROWS, ATTN_DIAGS))
        toeplitz = pltpu.roll(vec, ATTN_ROLL, 1, stride=1, stride_axis=0)[:, :ATTN_WINDOW]
        bm_ref[j] = jnp.where(band, toeplitz, -jnp.inf)


def _attn_probs(q_ref, k_ref, bm_ref, j, hd, start, valid):
    qh = q_ref[:, j * hd:(j + 1) * hd]
    kb = k_ref[pl.ds(start, ATTN_WINDOW), j * hd:(j + 1) * hd]
    sc = lax.dot_general(qh, kb, (((1,), (1,)), ((), ())), preferred_element_type=F32)
    sc = sc * (hd ** -0.5) + bm_ref[j]
    sc = jnp.where(valid, sc, -jnp.inf)
    sc = sc - jnp.max(sc, axis=-1, keepdims=True)
    e = jnp.exp(sc)
    return e / jnp.sum(e, axis=-1, keepdims=True), qh, kb


def _window_valid(start):
    r = lax.broadcasted_iota(jnp.int32, (1, ATTN_WINDOW), 1)
    return (start + r) >= LEFT_PAD


def _attn_fwd(q, kvp, rel_vec):
    t, d = q.shape
    hd = d // N_HEADS
    half = N_DEV // 2
    n = kvp.shape[-1]
    per = n // hd
    rows = kvp.shape[1]

    def body(q_ref, k_ref, v_ref, rel_ref, o_ref, bm_ref):
        @pl.when(pl.program_id(1) == 0)
        def _():
            _attn_bias_mask(rel_ref, bm_ref)

        start = pl.multiple_of(pl.program_id(1) * ATTN_ROWS, ATTN_ROWS)
        valid = _window_valid(start)
        for j in range(per):
            p, _, _ = _attn_probs(q_ref, k_ref, bm_ref, j, hd, start, valid)
            vb = v_ref[pl.ds(start, ATTN_WINDOW), j * hd:(j + 1) * hd]
            o_ref[:, j * hd:(j + 1) * hd] = jnp.dot(p.astype(BF16), vb, preferred_element_type=F32).astype(BF16)

    return pl.pallas_call(
        body, name="attn_fwd", grid=(half, t // ATTN_ROWS),
        in_specs=[pl.BlockSpec((ATTN_ROWS, n), lambda g, i: (i, g)),
                  pl.BlockSpec((None, rows, n), lambda g, i: (g, 0, 0)),
                  pl.BlockSpec((None, rows, n), lambda g, i: (half + g, 0, 0)),
                  pl.BlockSpec((None, per, ATTN_DIAGS), lambda g, i: (g, 0, 0))],
        out_specs=pl.BlockSpec((ATTN_ROWS, n), lambda g, i: (i, g)),
        out_shape=_sds((t, d), BF16),
        scratch_shapes=[pltpu.VMEM((per, ATTN_ROWS, ATTN_WINDOW), F32)],
        compiler_params=_params(("arbitrary", "arbitrary")),
    )(q, kvp, kvp, rel_vec.reshape(half, per, ATTN_DIAGS))


def _attn_bwd(q, dout, kvp, rel_vec, dk_in=None, dv_in=None):
    t, d = q.shape
    hd = d // N_HEADS
    half = N_DEV // 2
    n = kvp.shape[-1]
    per = n // hd
    rows = kvp.shape[1]
    scale = hd ** -0.5
    carry = dk_in is not None

    def body(q_ref, do_ref, k_ref, v_ref, rel_ref, *rest):
        dq_ref, dk_ref, dv_ref, dsc_ref, bm_ref = rest[-5:]

        @pl.when(pl.program_id(1) == 0)
        def _():
            _attn_bias_mask(rel_ref, bm_ref)
            dk_ref[...] = rest[0][...] if carry else jnp.zeros_like(dk_ref)
            dv_ref[...] = rest[1][...] if carry else jnp.zeros_like(dv_ref)
            dsc_ref[...] = jnp.zeros_like(dsc_ref)

        start = pl.multiple_of(pl.program_id(1) * ATTN_ROWS, ATTN_ROWS)
        valid = _window_valid(start)
        for j in range(per):
            cols = slice(j * hd, (j + 1) * hd)
            p, qh, kb = _attn_probs(q_ref, k_ref, bm_ref, j, hd, start, valid)
            vb = v_ref[pl.ds(start, ATTN_WINDOW), cols]
            doh = do_ref[:, cols]
            dp = lax.dot_general(doh, vb, (((1,), (1,)), ((), ())), preferred_element_type=F32)
            ds = p * (dp - jnp.sum(dp * p, axis=-1, keepdims=True))
            dsc_ref[j] += sum(ds[a * CHUNK:(a + 1) * CHUNK, a * CHUNK:a * CHUNK + BAND]
                              for a in range(ATTN_CHUNKS))
            dsb = (ds * scale).astype(BF16)
            dq_ref[:, cols] = jnp.dot(dsb, kb, preferred_element_type=F32).astype(BF16)
            dk_ref[pl.ds(start, ATTN_WINDOW), cols] += lax.dot_general(
                dsb, qh, (((0,), (0,)), ((), ())), preferred_element_type=F32)
            dv_ref[pl.ds(start, ATTN_WINDOW), cols] += lax.dot_general(
                p.astype(BF16), doh, (((0,), (0,)), ((), ())), preferred_element_type=F32)

    tile = pl.BlockSpec((ATTN_ROWS, n), lambda g, i: (i, g))
    shard = pl.BlockSpec((None, rows, n), lambda g, i: (g, 0, 0))
    in_specs = [tile, tile, shard, pl.BlockSpec((None, rows, n), lambda g, i: (half + g, 0, 0)),
                pl.BlockSpec((None, per, ATTN_DIAGS), lambda g, i: (g, 0, 0))]
    operands = [q, dout, kvp, kvp, rel_vec.reshape(half, per, ATTN_DIAGS)]
    if carry:
        in_specs += [shard, shard]
        operands += [dk_in, dv_in]
    acc = _sds((half, rows, n), F32)
    return pl.pallas_call(
        body, name="attn_bwd", grid=(half, t // ATTN_ROWS),
        in_specs=in_specs,
        out_specs=[tile, shard, shard, pl.BlockSpec((per, CHUNK, BAND), lambda g, i: (g, 0, 0))],
        out_shape=[_sds((t, d), BF16), acc, acc, _sds((N_HEADS, CHUNK, BAND), F32)],
        scratch_shapes=[pltpu.VMEM((per, ATTN_ROWS, ATTN_WINDOW), F32)],
        compiler_params=_params(("arbitrary", "arbitrary")),
    )(*operands)


SKEW_PITCH = 640
SKEW = SKEW_PITCH + 1
SKEW_LANES = -(-SKEW // LANES) * LANES


def _skew_diagonals(dsc):
    h = dsc.shape[0]
    wide = jnp.pad(dsc, ((0, 0), (0, 0), (0, SKEW_PITCH - BAND))).reshape(h, CHUNK * SKEW_PITCH)
    wide = jnp.pad(wide, ((0, 0), (0, CHUNK))).reshape(h, CHUNK, SKEW)
    return jnp.pad(wide, ((0, 0), (0, 0), (0, SKEW_LANES - SKEW)))


def _rel_bias_grad(skewed):
    heads = skewed.shape[0]
    hb = SUBLANES

    def body(d_ref, o_ref):
        col = lax.broadcasted_iota(jnp.int32, (SKEW_LANES, N_REL), 0)
        bucket = lax.broadcasted_iota(jnp.int32, (SKEW_LANES, N_REL), 1)
        diag = jnp.where(col < BAND, col, col - SKEW)
        idx = jnp.clip(LEFT_PAD - diag, -(CHUNK - 1), MAX_REL) + (CHUNK - 1)
        oh = ((idx == bucket) & (col < SKEW)).astype(BF16)
        dv = jnp.sum(d_ref[...], axis=1)
        hi = dv.astype(BF16)
        rest = dv - hi.astype(F32)
        mid = rest.astype(BF16)
        lo = (rest - mid.astype(F32)).astype(BF16)
        acc = jnp.dot(hi, oh, preferred_element_type=F32)
        acc += jnp.dot(mid, oh, preferred_element_type=F32)
        acc += jnp.dot(lo, oh, preferred_element_type=F32)
        o_ref[...] = acc

    return pl.pallas_call(
        body, name="rel_bias_grad", grid=(heads // hb,),
        in_specs=[pl.BlockSpec((hb, CHUNK, SKEW_LANES), lambda i: (i, 0, 0))],
        out_specs=pl.BlockSpec((hb, N_REL), lambda i: (i, 0)),
        out_shape=_sds((heads, N_REL), F32),
        compiler_params=_params(("parallel",)),
    )(skewed)


def _sum_parts(parts):
    s_n, rows, c = parts.shape
    br = _row_block(rows, OPT_ROW_BLOCK)

    def body(p_ref, o_ref):
        acc = p_ref[0].astype(F32)
        for s in range(1, s_n):
            acc = acc + p_ref[s].astype(F32)
        o_ref[...] = acc

    return pl.pallas_call(
        body, name="sum_parts", grid=(rows // br,),
        in_specs=[pl.BlockSpec((s_n, br, c), lambda i: (0, i, 0))],
        out_specs=pl.BlockSpec((br, c), lambda i: (i, 0)),
        out_shape=_sds((rows, c), F32),
        compiler_params=_params(("parallel",)),
    )(parts)


def _adamw(own, own_idx, parts, w, m, v, row0=0, bufs=None, after=None):
    _, rows, c = own.shape
    s_n = 0 if parts is None else parts.shape[0]
    total = w.shape[0]
    br = _row_block(rows, OPT_ROW_BLOCK)
    assert row0 % br == 0 and (bufs is not None or (row0 == 0 and total == rows))
    b0 = row0 // br
    m_corr = 1.0 - ADAM_B1 ** ADAM_STEP
    v_corr = 1.0 - ADAM_B2 ** ADAM_STEP

    def body(idx_ref, own_ref, *refs):
        if s_n:
            p_ref, refs = refs[0], refs[1:]
        w_ref, m_ref, v_ref = refs[:3]
        g_ref, d_ref, nm_ref, nv_ref = refs[-4:]
        g = own_ref[...].astype(F32)
        for s in range(s_n):
            g = g + p_ref[s].astype(F32)
        nm = ADAM_B1 * m_ref[...] + (1.0 - ADAM_B1) * g
        nv = ADAM_B2 * v_ref[...] + (1.0 - ADAM_B2) * (g * g)
        g_ref[...] = g
        nm_ref[...] = nm
        nv_ref[...] = nv
        d_ref[...] = -ADAM_LR * ((nm / m_corr) / (jnp.sqrt(nv / v_corr) + ADAM_EPS) + ADAM_WD * w_ref[...])

    tile = pl.BlockSpec((br, c), lambda i, idx: (i + b0, 0))
    in_specs = [pl.BlockSpec((None, br, c), lambda i, idx: (idx[0], i, 0))]
    operands = [own_idx, own]
    if s_n:
        in_specs.append(pl.BlockSpec((s_n, br, c), lambda i, idx: (0, i, 0)))
        operands.append(parts)
    in_specs += [tile, tile, tile]
    operands += [w, m, v]
    aliases = {}
    if bufs is not None:
        aliases = {len(operands) + j: j for j in range(4)}
        in_specs += [_ANY] * 4
        operands += list(bufs)
    if after is not None:
        in_specs.append(_ANY)
        operands.append(after)
    out = _sds((total, c), F32)
    return pl.pallas_call(
        body, name="adamw",
        grid_spec=pltpu.PrefetchScalarGridSpec(
            num_scalar_prefetch=1, grid=(rows // br,), in_specs=in_specs,
            out_specs=[tile, tile, tile, tile]),
        out_shape=[out, out, out, out],
        input_output_aliases=aliases,
        compiler_params=_params(("parallel",)),
    )(*operands)


def _chip_sum(p, r1, core):
    half = N_DEV // 2
    c = p.shape[-1]
    rows = int(np.prod(p.shape[1:-1]))
    br = _row_block(rows, BIG_ROW_BLOCK)

    def body(core_ref, p_ref, r_ref, o_ref):
        o_ref[...] = (p_ref[...].astype(F32) + r_ref[...].astype(F32)).astype(BF16)

    out = pl.pallas_call(
        body, name="chip_sum",
        grid_spec=pltpu.PrefetchScalarGridSpec(
            num_scalar_prefetch=1, grid=(half, rows // br),
            in_specs=[pl.BlockSpec((None, None, br, c), lambda q, i, cr: (q, cr[0], i, 0)),
                      pl.BlockSpec((None, br, c), lambda q, i, cr: (q, i, 0))],
            out_specs=pl.BlockSpec((None, br, c), lambda q, i, cr: (q, i, 0))),
        out_shape=_sds((half, rows, c), BF16),
        compiler_params=_params(("parallel", "parallel")),
    )(core, p.reshape(half, 2, rows, c), r1.reshape(half, rows, c))
    return out.reshape((half,) + p.shape[1:])


def _position():
    return tuple(lax.axis_index(a) for a in MESH_AXES)


def _linear(px, py, pc):
    return 4 * px + 2 * py + pc


def _all_gather_small(v, after=()):
    rows, lanes = v.shape

    def body(x_ref, *rest):
        out_ref, send_sems, recv_sems, local_sem = rest[-4:]
        x, y, c = _position()
        me, sibling = (x, y, c), (x, y, 1 - c)
        chips = [(1 - x, y), (x, 1 - y), (1 - x, 1 - y)]

        def copy(k, block, to, src=None):
            dst = out_ref.at[_linear(*block)]
            return pltpu.make_async_remote_copy(
                src_ref=dst if src is None else src, dst_ref=dst,
                send_sem=send_sems.at[k], recv_sem=recv_sems.at[k],
                device_id=to, device_id_type=MESH_ID)

        mine = pltpu.make_async_copy(x_ref, out_ref.at[_linear(*me)], local_sem)
        mine.start()
        first = [copy(0, me, sibling, src=x_ref)]
        first += [copy(1 + j, me, (*chip, c), src=x_ref) for j, chip in enumerate(chips)]
        for cp in first:
            cp.start()
        passed = [copy(4 + j, (*chip, c), sibling) for j, chip in enumerate(chips)]
        for j, chip in enumerate(chips):
            copy(1 + j, (*chip, c), me).wait_recv()
            passed[j].start()
        copy(0, sibling, me).wait_recv()
        for j, chip in enumerate(chips):
            copy(4 + j, (*chip, 1 - c), me).wait_recv()
        for cp in first + passed:
            cp.wait_send()
        mine.wait()

    return pl.pallas_call(
        body, name="all_gather_small",
        out_shape=_sds((N_DEV, rows, lanes), v.dtype),
        in_specs=[_VMEM] + [_ANY] * len(after), out_specs=_VMEM,
        scratch_shapes=[pltpu.SemaphoreType.DMA((7,)), pltpu.SemaphoreType.DMA((7,)),
                        pltpu.SemaphoreType.DMA],
        compiler_params=pltpu.CompilerParams(vmem_limit_bytes=VMEM_LIMIT),
    )(v, *after)


_HBM = pl.BlockSpec(memory_space=pltpu.HBM)
_SEM = pl.BlockSpec(memory_space=pltpu.SEMAPHORE)
_EFFECT = pltpu.SideEffectType.DATAFLOW_SIDE_EFFECTING
_ALL_CHIPS = [(0, 0), (0, 1), (1, 0), (1, 1)]


def _other_chips(x, y):
    return [(1 - x, y), (x, 1 - y), (1 - x, 1 - y)]


def _in_hbm(a):
    return pltpu.with_memory_space_constraint(a, pltpu.HBM)


def _token():
    return _sds((SUBLANES, LANES), F32)


def _gather_ici_copy(ref, i, k, chip, c, block, send_sems, recv_sems):
    return pltpu.make_async_remote_copy(
        src_ref=ref.at[block], dst_ref=ref.at[block],
        send_sem=send_sems.at[3 * i + k], recv_sem=recv_sems.at[3 * i + k],
        device_id=(*chip, c), device_id_type=MESH_ID)


def _gather_ici_start(name, lands):
    n = len(lands)

    def body(*refs):
        ins, send_sems, recv_sems, token = refs[:n], refs[n], refs[n + 1], refs[-1]
        x, y, c = _position()
        me = _linear(x, y, c)
        for i in range(n):
            for k, chip in enumerate(_other_chips(x, y)):
                _gather_ici_copy(ins[i], i, k, chip, c, me, send_sems, recv_sems).start()
        token[...] = jnp.zeros_like(token)

    out = pl.pallas_call(
        body, name=name,
        out_shape=(pltpu.SemaphoreType.DMA((3 * n,)), pltpu.SemaphoreType.DMA((3 * n,)),
                   *[pltpu.HBM(a.shape, a.dtype) for a in lands], _token()),
        in_specs=[_HBM] * n, out_specs=(_SEM, _SEM, *[_HBM] * n, _VMEM),
        input_output_aliases={i: 2 + i for i in range(n)},
        compiler_params=pltpu.CompilerParams(has_side_effects=_EFFECT),
    )(*[_in_hbm(a) for a in lands])
    return out[0], out[1], list(out[2:2 + n]), out[-1]


def _gather_ici_wait(name, lands, send_sems, recv_sems, after):
    n = len(lands)

    def body(*refs):
        ins, ss, rs = refs[:n], refs[n], refs[n + 1]
        x, y, c = _position()
        me = _linear(x, y, c)
        for i in range(n):
            for k, chip in enumerate(_other_chips(x, y)):
                _gather_ici_copy(ins[i], i, k, chip, c, me, ss, rs).wait_send()
                _gather_ici_copy(ins[i], i, k, chip, c, _linear(*chip, c), ss, rs).wait_recv()

    out = pl.pallas_call(
        body, name=name,
        out_shape=[pltpu.HBM(a.shape, a.dtype) for a in lands],
        in_specs=[_HBM] * n + [_SEM, _SEM, _ANY], out_specs=[_HBM] * n,
        input_output_aliases={i: i for i in range(n)},
        compiler_params=pltpu.CompilerParams(has_side_effects=_EFFECT),
    )(*lands, send_sems, recv_sems, after)
    return list(out)


def _gather_d2d(lands):
    n = len(lands)

    def body(*refs):
        ins, outs, send_sems, recv_sems = refs[:n], refs[n:2 * n], refs[2 * n], refs[2 * n + 1]
        x, y, c = _position()

        def copy(i, q, core):
            block = _linear(*_ALL_CHIPS[q], core)
            return pltpu.make_async_remote_copy(
                src_ref=ins[i].at[block], dst_ref=outs[i].at[block],
                send_sem=send_sems.at[i, q], recv_sem=recv_sems.at[i, q],
                device_id=(x, y, 1 - c), device_id_type=MESH_ID)

        sent = [copy(i, q, c) for i in range(n) for q in range(len(_ALL_CHIPS))]
        for cp in sent:
            cp.start()
        for i in range(n):
            for q in range(len(_ALL_CHIPS)):
                copy(i, q, 1 - c).wait_recv()
        for cp in sent:
            cp.wait_send()

    return pl.pallas_call(
        body, name="gather_d2d",
        out_shape=[_sds(a.shape, a.dtype) for a in lands],
        in_specs=[_ANY] * n, out_specs=[_ANY] * n,
        input_output_aliases={i: i for i in range(n)},
        scratch_shapes=[pltpu.SemaphoreType.DMA((n, 4)), pltpu.SemaphoreType.DMA((n, 4))],
    )(*lands)


def _partials_d2d(parts):
    n = len(parts)
    half = N_DEV // 2

    def body(*refs):
        ins, outs, send_sems, recv_sems = refs[:n], refs[n:2 * n], refs[2 * n], refs[2 * n + 1]
        x, y, c = _position()

        def copy(i, q):
            return pltpu.make_async_remote_copy(
                src_ref=ins[i].at[_linear(*_ALL_CHIPS[q], 1 - c)], dst_ref=outs[i].at[q],
                send_sem=send_sems.at[i, q], recv_sem=recv_sems.at[i, q],
                device_id=(x, y, 1 - c), device_id_type=MESH_ID)

        sent = [copy(i, q) for i in range(n) for q in range(half)]
        for cp in sent:
            cp.start()
        for cp in sent:
            cp.wait_recv()
        for cp in sent:
            cp.wait_send()

    return pl.pallas_call(
        body, name="partials_d2d",
        out_shape=[_sds((half,) + p.shape[1:], p.dtype) for p in parts],
        in_specs=[_ANY] * n, out_specs=[_ANY] * n,
        scratch_shapes=[pltpu.SemaphoreType.DMA((n, half)), pltpu.SemaphoreType.DMA((n, half))],
    )(*parts)


def _partials_peers(x, y, c, direct):
    chips = _other_chips(x, y)
    if not direct:
        return [((*ch, c), 2 * ch[0] + ch[1]) for ch in chips]
    peers = [(x, y, 1 - c)] + [(*ch, c) for ch in chips] + [(*ch, 1 - c) for ch in chips]
    return [(p, _linear(*p)) for p in peers]


def _partials_copies(srcs, lands, send_sems, recv_sems, direct):
    x, y, c = _position()
    peers = _partials_peers(x, y, c, direct)
    return [pltpu.make_async_remote_copy(
        src_ref=srcs[i].at[block], dst_ref=lands[i].at[k],
        send_sem=send_sems.at[len(peers) * i + k], recv_sem=recv_sems.at[len(peers) * i + k],
        device_id=peer, device_id_type=MESH_ID)
        for i in range(len(srcs)) for k, (peer, block) in enumerate(peers)]


def _partials_send_start(name, srcs, lands, direct, after=None):
    n = len(srcs)
    n_sem = n * (N_DEV - 1 if direct else len(_ALL_CHIPS) - 1)

    def body(*refs):
        _, send_sems, recv_sems = refs[:2 * n], refs[-2 * n - 3], refs[-2 * n - 2]
        for cp in _partials_copies(refs[:n], refs[n:2 * n], send_sems, recv_sems, direct):
            cp.start()
        refs[-1][...] = jnp.zeros_like(refs[-1])

    both = list(srcs) + list(lands)
    extra = [] if after is None else [after]
    out = pl.pallas_call(
        body, name=name,
        out_shape=(pltpu.SemaphoreType.DMA((n_sem,)), pltpu.SemaphoreType.DMA((n_sem,)),
                   *[pltpu.HBM(a.shape, a.dtype) for a in both], _token()),
        in_specs=[_HBM] * (2 * n) + [_ANY] * len(extra), out_specs=(_SEM, _SEM, *[_HBM] * (2 * n), _VMEM),
        input_output_aliases={i: 2 + i for i in range(2 * n)},
        compiler_params=pltpu.CompilerParams(has_side_effects=_EFFECT),
    )(*[_in_hbm(a) for a in both], *extra)
    return out[0], out[1], list(out[2:2 + n]), list(out[2 + n:2 + 2 * n]), out[-1]


def _partials_send_wait(name, srcs, lands, send_sems, recv_sems, direct, after):
    n = len(srcs)

    def body(*refs):
        for cp in _partials_copies(refs[:n], refs[n:2 * n], refs[2 * n], refs[2 * n + 1], direct):
            cp.wait_send()
            cp.wait_recv()

    both = list(srcs) + list(lands)
    out = pl.pallas_call(
        body, name=name,
        out_shape=[pltpu.HBM(a.shape, a.dtype) for a in both],
        in_specs=[_HBM] * (2 * n) + [_SEM, _SEM, _ANY], out_specs=[_HBM] * (2 * n),
        input_output_aliases={i: i for i in range(2 * n)},
        compiler_params=pltpu.CompilerParams(has_side_effects=_EFFECT),
    )(*both, send_sems, recv_sems, after)
    return list(out[:n]), list(out[n:])


def _pack(arrs):
    flat = jnp.concatenate([a.reshape(-1).astype(F32) for a in arrs])
    block = OPT_ROW_BLOCK if flat.shape[0] > OPT_ROW_BLOCK * LANES else SUBLANES
    pad = (-flat.shape[0]) % (block * LANES)
    if pad:
        flat = jnp.concatenate([flat, jnp.zeros((pad,), F32)])
    return flat.reshape(-1, LANES)


def _unpack(packed, shapes, lead=()):
    flat = packed.reshape(lead + (-1,))
    out, off = [], 0
    for s in shapes:
        size = int(np.prod(s))
        out.append(flat[..., off:off + size].reshape(lead + tuple(s)))
        off += size
    return out


def _unshard_last(g):
    nd = g.ndim
    perm = tuple(range(1, nd - 1)) + (0, nd - 1)
    t = jnp.transpose(g, perm)
    return t.reshape(t.shape[:-2] + (N_DEV * g.shape[-1],))


def kernel(x, c, w_ada, b_ada, ln_g, ln_b, ffn_gu, ffn_down, gmlp_w_in, gmlp_b_in, gmlp_ln_g, gmlp_ln_b, gmlp_w_s, gmlp_b_s, gmlp_w_out, w_ada_kv, b_ada_kv, w_kv, attn_w_q, attn_rel_bias, attn_w_o, loss_target, m_w_ada, m_b_ada, m_ln_g, m_ln_b, m_ffn_gu, m_ffn_down, m_gmlp_w_in, m_gmlp_b_in, m_gmlp_ln_g, m_gmlp_ln_b, m_gmlp_w_s, m_gmlp_b_s, m_gmlp_w_out, m_w_ada_kv, m_b_ada_kv, m_w_kv, m_attn_w_q, m_attn_rel_bias, m_attn_w_o, v_w_ada, v_b_ada, v_ln_g, v_ln_b, v_ffn_gu, v_ffn_down, v_gmlp_w_in, v_gmlp_b_in, v_gmlp_ln_g, v_gmlp_ln_b, v_gmlp_w_s, v_gmlp_b_s, v_gmlp_w_out, v_w_ada_kv, v_b_ada_kv, v_w_kv, v_attn_w_q, v_attn_rel_bias, v_attn_w_o):
    weights = dict(w_ada=w_ada, b_ada=b_ada, ln_g=ln_g, ln_b=ln_b, ffn_gu=ffn_gu, ffn_down=ffn_down,
                   gmlp_w_in=gmlp_w_in, gmlp_b_in=gmlp_b_in, gmlp_ln_g=gmlp_ln_g, gmlp_ln_b=gmlp_ln_b,
                   gmlp_w_s=gmlp_w_s, gmlp_b_s=gmlp_b_s, gmlp_w_out=gmlp_w_out, w_ada_kv=w_ada_kv,
                   b_ada_kv=b_ada_kv, w_kv=w_kv, attn_w_q=attn_w_q, attn_rel_bias=attn_rel_bias,
                   attn_w_o=attn_w_o)
    mom1 = dict(w_ada=m_w_ada, b_ada=m_b_ada, ln_g=m_ln_g, ln_b=m_ln_b, ffn_gu=m_ffn_gu, ffn_down=m_ffn_down,
                gmlp_w_in=m_gmlp_w_in, gmlp_b_in=m_gmlp_b_in, gmlp_ln_g=m_gmlp_ln_g, gmlp_ln_b=m_gmlp_ln_b,
                gmlp_w_s=m_gmlp_w_s, gmlp_b_s=m_gmlp_b_s, gmlp_w_out=m_gmlp_w_out, w_ada_kv=m_w_ada_kv,
                b_ada_kv=m_b_ada_kv, w_kv=m_w_kv, attn_w_q=m_attn_w_q, attn_rel_bias=m_attn_rel_bias,
                attn_w_o=m_attn_w_o)
    mom2 = dict(w_ada=v_w_ada, b_ada=v_b_ada, ln_g=v_ln_g, ln_b=v_ln_b, ffn_gu=v_ffn_gu, ffn_down=v_ffn_down,
                gmlp_w_in=v_gmlp_w_in, gmlp_b_in=v_gmlp_b_in, gmlp_ln_g=v_gmlp_ln_g, gmlp_ln_b=v_gmlp_ln_b,
                gmlp_w_s=v_gmlp_w_s, gmlp_b_s=v_gmlp_b_s, gmlp_w_out=v_gmlp_w_out, w_ada_kv=v_w_ada_kv,
                b_ada_kv=v_b_ada_kv, w_kv=v_w_kv, attn_w_q=v_attn_w_q, attn_rel_bias=v_attn_rel_bias,
                attn_w_o=v_attn_w_o)
    order = list(weights)

    x = x[0]
    target = loss_target[0]
    t, d = x.shape
    n_mod = w_ada.shape[-1] * N_DEV // d
    mod_w = w_ada.shape[-1]
    kv_w = w_ada_kv.shape[-1]
    n_b = DEPTH - N_A
    me = _linear(*_position())

    l2 = DEPTH * 2
    big = dict(
        ffn_gu=ffn_gu.reshape((l2,) + ffn_gu.shape[2:]),
        ffn_down=ffn_down.reshape((l2,) + ffn_down.shape[2:]),
        gmlp_w_in=gmlp_w_in, gmlp_w_out=gmlp_w_out, w_kv=w_kv[None],
        attn_w_q=attn_w_q, attn_w_o=attn_w_o)
    big_names = list(big)
    core = lax.axis_index("c").astype(jnp.int32).reshape(1)
    chip = (2 * lax.axis_index("x") + lax.axis_index("y")).astype(jnp.int32).reshape(1)

    fwd_groups = [
        {"ffn_gu": (0, 1), "ffn_down": (0, 1)},
        {"gmlp_w_in": (0, 1), "gmlp_w_out": (0, 1), "ffn_gu": (1, 1), "ffn_down": (1, 1)},
        {"ffn_gu": (2, 2), "ffn_down": (2, 2), "gmlp_w_in": (1, 1), "gmlp_w_out": (1, 1), "w_kv": (0, 1)},
        {"ffn_gu": (4, 2), "ffn_down": (4, 2), "attn_w_q": (0, 1), "attn_w_o": (0, 1)},
        {"ffn_gu": (6, 2), "ffn_down": (6, 2), "attn_w_q": (1, 1), "attn_w_o": (1, 1)},
    ]
    bwd_groups = []
    for l in range(DEPTH):
        g = {"ffn_gu": (2 * l, 2), "ffn_down": (2 * l, 2)}
        if l < N_A:
            g.update({"gmlp_w_in": (l, 1), "gmlp_w_out": (l, 1)})
        else:
            g.update({"attn_w_q": (l - N_A, 1), "attn_w_o": (l - N_A, 1)})
        if l == N_A - 1:
            g["w_kv"] = (0, 1)
        bwd_groups.append(g)

    def slot_of(groups, name, slot):
        for gi, g in enumerate(groups):
            if name in g and g[name][0] <= slot < g[name][0] + g[name][1]:
                return gi, slot - g[name][0]
        raise KeyError((name, slot))

    flights = []
    for gi, g in enumerate(fwd_groups):
        lands = []
        for name, (s0, cnt) in g.items():
            shard = big[name][s0:s0 + cnt].astype(BF16)
            land = lax.empty((N_DEV,) + shard.shape, BF16)
            lands.append(lax.dynamic_update_slice(land, shard[None], (me,) + (0,) * shard.ndim))
        flights.append(_gather_ici_start(f"gather_ici_start_{gi}", lands))
    start_token = sum(f[3][0, 0] for f in flights)
    gathered = [None] * len(fwd_groups)

    def land_group(gi, after):
        send_sems, recv_sems, lands, _ = flights[gi]
        lands = _gather_ici_wait(f"gather_ici_wait_{gi}", lands, send_sems, recv_sems, after)
        gathered[gi] = dict(zip(fwd_groups[gi], _gather_d2d(lands)))

    def weight(name, slot):
        gi, local = slot_of(fwd_groups, name, slot)
        return gathered[gi][name], local

    partial = [{name: lax.empty((N_DEV, cnt) + big[name].shape[1:], BF16) for name, (_, cnt) in g.items()}
               for g in bwd_groups]

    c_all = _all_gather_small(_pack([c]))
    c_all = _unpack(c_all, [(d,)], lead=(N_DEV,))[0]
    c4 = _as4(c_all)
    mod_part = _matmul("ada_fwd", c4, w_ada[:, None], (DEPTH, 1, N_DEV, mod_w), F32, a_silu=True)
    kv_part = _matmul("ada_kv_fwd", c4, _as4(w_ada_kv), (1, 1, N_DEV, kv_w), F32, a_silu=True)
    small_shapes = [mod_part.shape, kv_part.shape, ln_g.shape, ln_b.shape, gmlp_b_in.shape,
                    gmlp_ln_g.shape, gmlp_ln_b.shape, attn_rel_bias.shape]
    small = _all_gather_small(_pack([mod_part, kv_part, ln_g, ln_b, gmlp_b_in, gmlp_ln_g, gmlp_ln_b,
                                     attn_rel_bias]))
    (mod_g, kvm_g, ln_g_g, ln_b_g, b_in_g, gln_g_g, gln_b_g, rel_g) = _unpack(small, small_shapes, lead=(N_DEV,))
    mod_mine = lax.dynamic_index_in_dim(mod_g[:, :, 0], me, axis=2, keepdims=False)
    mod = _unshard_last(mod_mine) + b_ada
    mod = mod.reshape(DEPTH, n_mod, 1, d)
    kvm_mine = lax.dynamic_index_in_dim(kvm_g[:, 0, 0], me, axis=1, keepdims=False)
    mkv = (_unshard_last(kvm_mine) + b_ada_kv).reshape(2, 1, d)
    ln_g_f = _unshard_last(ln_g_g)
    ln_b_f = _unshard_last(ln_b_g)
    half = N_DEV // 2
    b_in_f = jnp.transpose(b_in_g, (1, 0, 2))[:, :, None, :]
    gln_g_f = _unshard_last(gln_g_g).reshape(N_A, half, 1, -1)
    gln_b_f = _unshard_last(gln_b_g).reshape(N_A, half, 1, -1)
    rel_f = _unshard_last(rel_g)

    def shard_act(a):
        return a.reshape(a.shape[0], a.shape[2], a.shape[3])

    def grad_into(name, slot, mm):
        gi, local = slot_of(bwd_groups, name, slot)
        partial[gi][name] = mm(partial[gi][name], local)

    def ffn_fwd(h, lw):
        w_gu, l_gu = weight("ffn_gu", lw)
        w_dn, l_dn = weight("ffn_down", lw)
        n = w_gu.shape[-1]
        gu = _matmul("ffn_gu_fwd", _as4(h), w_gu, (N_DEV, 1, t, n), F32, lb=l_gu)
        a = _swiglu_fwd(shard_act(gu))
        y = _matmul("ffn_down_fwd", a[:, None], w_dn, (1, 1, t, d), F32, lb=l_dn, b_merge=2, reduce=True)
        return y[0, 0], (gu, a)

    def ffn_bwd(dy, h, saved, lw):
        gu, a = saved
        w_gu, l_gu = weight("ffn_gu", lw)
        w_dn, l_dn = weight("ffn_down", lw)
        n = w_gu.shape[-1]
        da = _matmul("ffn_down_bwd_a", _as4(dy), w_dn, (half, 1, t, n), F32, lb=l_dn, b_merge=2, tb=True)
        grad_into("ffn_down", lw, lambda buf, lo: _matmul(
            "ffn_down_bwd_w", a[:, None], _as4(dy), buf.shape, BF16, ta=True, lo=lo, out_merge=2, out_buf=buf))
        dgu = _swiglu_bwd(shard_act(gu), shard_act(da))
        dh = _matmul("ffn_gu_bwd_a", dgu[:, None], w_gu, (1, 1, t, d), F32, lb=l_gu, tb=True, reduce=True)
        grad_into("ffn_gu", lw, lambda buf, lo: _matmul(
            "ffn_gu_bwd_w", _as4(h), dgu[:, None], buf.shape, BF16, ta=True, lo=lo, out_buf=buf))
        return dh[0, 0], {}

    def gmlp_params(l):
        return (b_in_f[l], gln_g_f[l], gln_b_f[l], gmlp_w_s[l], gmlp_b_s[l][:, :, None])

    def gmlp_fwd(h, l):
        w_in, l_in = weight("gmlp_w_in", l)
        w_out, l_out = weight("gmlp_w_out", l)
        n = w_in.shape[-1]
        zpre = _matmul("gmlp_in_fwd", _as4(h), w_in, (N_DEV, 1, t, n), F32, lb=l_in)
        gated = _gmlp_mid_fwd(shard_act(zpre), *gmlp_params(l))
        y = _matmul("gmlp_out_fwd", gated[:, None], w_out, (1, 1, t, d), F32, lb=l_out, b_merge=2, reduce=True)
        return y[0, 0], (zpre, gated)

    def gmlp_bwd(dy, h, saved, l):
        zpre, gated = saved
        w_in, l_in = weight("gmlp_w_in", l)
        w_out, l_out = weight("gmlp_w_out", l)
        n = w_in.shape[-1]
        dgated = _matmul("gmlp_out_bwd_a", _as4(dy), w_out, (half, 1, t, n), F32, lb=l_out, b_merge=2, tb=True)
        grad_into("gmlp_w_out", l, lambda buf, lo: _matmul(
            "gmlp_out_bwd_w", gated[:, None], _as4(dy), buf.shape, BF16, ta=True, lo=lo, out_merge=2, out_buf=buf))
        dz, dws, dbs, dlng, dlnb, dbin = _gmlp_mid_bwd(shard_act(zpre), shard_act(dgated), *gmlp_params(l))
        dh = _matmul("gmlp_in_bwd_a", dz[:, None], w_in, (1, 1, t, d), F32, lb=l_in, tb=True, reduce=True)
        grad_into("gmlp_w_in", l, lambda buf, lo: _matmul(
            "gmlp_in_bwd_w", _as4(h), dz[:, None], buf.shape, BF16, ta=True, lo=lo, out_buf=buf))
        small_grads = dict(gmlp_w_s=dws, gmlp_b_s=dbs[:, :, 0], gmlp_ln_g=dlng.reshape(-1),
                           gmlp_ln_b=dlnb.reshape(-1), gmlp_b_in=dbin.reshape(-1))
        return dh[0, 0], small_grads

    def attn_fwd(h, j, kvp):
        rel_vec = _rel_vector(rel_f[j])
        w_q, l_q = weight("attn_w_q", j)
        w_o, l_o = weight("attn_w_o", j)
        q = _matmul("attn_q_fwd", _as4(h), w_q, (1, 1, t, d), BF16, lb=l_q, b_merge=N_DEV, reduce=True)[0, 0]
        o = _attn_fwd(q, kvp, rel_vec)
        y = _matmul("attn_o_fwd", _as4(o), w_o, (1, 1, t, d), F32, lb=l_o, b_merge=N_DEV, reduce=True)
        return y[0, 0], (q, o, rel_vec)

    def attn_bwd(dy, h, saved, j, kvp, dkv_acc):
        q, o, rel_vec = saved
        w_q, l_q = weight("attn_w_q", j)
        w_o, l_o = weight("attn_w_o", j)
        do = _matmul("attn_o_bwd_a", _as4(dy), w_o, (1, 1, t, d), BF16, lb=l_o, b_merge=N_DEV, tb=True)[0, 0]
        grad_into("attn_w_o", j, lambda buf, lo: _matmul(
            "attn_o_bwd_w", _as4(o), _as4(dy), buf.shape, BF16, ta=True, lo=lo, out_merge=N_DEV, out_buf=buf))
        dq, dk, dv, dsc = _attn_bwd(q, do, kvp, rel_vec, *dkv_acc)
        drel = _rel_bias_grad(_skew_diagonals(dsc))
        dh = _matmul("attn_q_bwd_a", _as4(dq), w_q, (1, 1, t, d), F32, lb=l_q, b_merge=N_DEV, tb=True)
        grad_into("attn_w_q", j, lambda buf, lo: _matmul(
            "attn_q_bwd_w", _as4(h), _as4(dq), buf.shape, BF16, ta=True, lo=lo, out_merge=N_DEV, out_buf=buf))
        return dh[0, 0], dict(attn_rel_bias=drel, dkv=(dk, dv))

    tape = []
    kvp = None
    kv_tape = None
    first_use = {(0, 0): 0, (0, 1): 1, (1, 0): 2, (2, 0): 3, (3, 0): 4}
    h = _modulate(x, mod[0, 1], mod[0, 0] + start_token)
    for l in range(DEPTH):
        for i in range(3):
            if (l, i) in first_use:
                land_group(first_use[l, i], x)
            scl, gate = mod[l, 3 * i + 1], mod[l, 3 * i + 2]
            wgt = 1.0 if i == 1 else 0.5
            gw = wgt * (1.0 + gate)
            if i != 1:
                y, saved = ffn_fwd(h, 2 * l + i // 2)
            elif l < N_A:
                y, saved = gmlp_fwd(h, l)
            else:
                y, saved = attn_fwd(h, l - N_A, kvp)
            nl, ni = (l, i + 1) if i < 2 else (l + 1, 0)
            readers = [(mod[nl, 3 * ni + 1], mod[nl, 3 * ni])] if nl < DEPTH else []
            shared_kv = (l, i) == (N_A - 1, 2)
            if shared_kv:
                readers.append((mkv[1], mkv[0]))
            outs = _ln_res_fwd(x, y, gw, ln_g_f[l, i][None], ln_b_f[l, i][None], readers)
            tape.append((x, h, y, gw, scl, saved))
            x = outs[0]
            h = outs[1] if nl < DEPTH else None
            if shared_kv:
                hkv = outs[-1]
                w_kvg, l_kv = weight("w_kv", 0)
                n = w_kvg.shape[-1]
                kv = _matmul("kv_fwd", _as4(hkv), w_kvg, (N_DEV, 1, t, n), BF16, lb=l_kv)
                kvp = jnp.pad(shard_act(kv), ((0, 0), (LEFT_PAD, 0), (0, 0)))
                kv_tape = hkv

    loss_part, dx = _loss_head(x, target)
    loss = lax.psum(loss_part[0, 0], MESH_AXES)

    d_mod = [[None] * n_mod for _ in range(DEPTH)]
    d_ln_g = [[None] * 3 for _ in range(DEPTH)]
    d_ln_b = [[None] * 3 for _ in range(DEPTH)]
    small_grads = {k: [None] * N_A for k in ("gmlp_w_s", "gmlp_b_s", "gmlp_ln_g", "gmlp_ln_b", "gmlp_b_in")}
    d_rel = [None] * n_b
    dkv_acc = ()
    d_mkv = None
    reductions = [None] * DEPTH
    sent_token = None
    readers = []
    for l in reversed(range(DEPTH)):
        if l == N_A - 1:
            hkv = kv_tape
            w_kvg, l_kv = weight("w_kv", 0)
            dkv = jnp.concatenate(dkv_acc)[:, LEFT_PAD:, :].astype(BF16)[:, None]
            dhkv = _matmul("kv_bwd_a", dkv, w_kvg, (1, 1, t, d), F32, lb=l_kv, tb=True, reduce=True)[0, 0]
            grad_into("w_kv", 0, lambda buf, lo: _matmul(
                "kv_bwd_w", _as4(hkv), dkv, buf.shape, BF16, ta=True, lo=lo, out_buf=buf))
            readers.append((dhkv, mkv[1], None))
        for i in reversed(range(3)):
            x_in, h, y, gw, scl, saved = tape[3 * l + i]
            wgt = 1.0 if i == 1 else 0.5
            if sent_token is not None:
                gw = gw + sent_token
                sent_token = None
            res = _ln_res_bwd(x_in, y, gw, ln_g_f[l, i][None], ln_b_f[l, i][None], dx,
                              [(r[0], r[1]) for r in readers])
            dx_res, dy, dgw, dg, db = res[:5]
            for k, (_, _, slot) in enumerate(readers):
                dscl_k, dshift_k = res[5 + 2 * k][0], res[6 + 2 * k][0]
                if slot is None:
                    d_mkv = jnp.concatenate([dshift_k, dscl_k])
                else:
                    d_mod[slot[0]][slot[1]], d_mod[slot[0]][slot[1] + 1] = dshift_k, dscl_k
            d_ln_g[l][i], d_ln_b[l][i] = dg[0], db[0]
            if i != 1:
                dh, extra = ffn_bwd(dy, h, saved, 2 * l + i // 2)
            elif l < N_A:
                dh, extra = gmlp_bwd(dy, h, saved, l)
                for k, g in extra.items():
                    small_grads[k][l] = g
            else:
                dh, extra = attn_bwd(dy, h, saved, l - N_A, kvp, dkv_acc)
                d_rel[l - N_A] = extra["attn_rel_bias"]
                dkv_acc = extra["dkv"]
            d_mod[l][3 * i + 2] = wgt * dgw[0]
            dx = dx_res
            readers = [(dh, scl, (l, 3 * i))]
        if l > 0:
            srcs = [partial[l][k] for k in bwd_groups[l]]
            lands = [lax.empty((N_DEV - 1,) + s.shape[1:], BF16) for s in srcs]
            reductions[l] = _partials_send_start(f"partials_send_start_{l}", srcs, lands, True)
            sent_token = reductions[l][4][0, 0]
    (dh, scl, _), = readers
    dx, dscl, dshift = _mod_bwd(dx, dh, tape[0][0], scl)
    d_mod[0][0], d_mod[0][1] = dshift[0], dscl[0]
    grad_x = dx[None]

    d_mod_arr = jnp.stack([jnp.concatenate(r) for r in d_mod])
    small_part = dict(
        b_ada=d_mod_arr, b_ada_kv=d_mkv,
        ln_g=jnp.stack([jnp.stack(r) for r in d_ln_g]), ln_b=jnp.stack([jnp.stack(r) for r in d_ln_b]),
        gmlp_b_in=jnp.stack(small_grads["gmlp_b_in"]), gmlp_ln_g=jnp.stack(small_grads["gmlp_ln_g"]),
        gmlp_ln_b=jnp.stack(small_grads["gmlp_ln_b"]), gmlp_w_s=jnp.stack(small_grads["gmlp_w_s"]),
        gmlp_b_s=jnp.stack(small_grads["gmlp_b_s"]), attn_rel_bias=jnp.stack(d_rel))
    small_names = list(small_part)
    sp_shapes = [small_part[k].shape for k in small_names]
    sp_all = _all_gather_small(_pack([small_part[k] for k in small_names]),
                               after=[partial[0][k] for k in bwd_groups[0]])

    from_sibling = _partials_d2d([partial[0][k] for k in bwd_groups[0]])
    sums = [_chip_sum(partial[0][k], r1, core) for k, r1 in zip(bwd_groups[0], from_sibling)]
    lands = [lax.empty((len(_ALL_CHIPS) - 1,) + s.shape[1:], BF16) for s in sums]
    reductions[0] = _partials_send_start("partials_send_start_0", sums, lands, False, after=sp_all)
    sent_token = reductions[0][4][0, 0]
    c4 = c4 + sent_token

    sp_sum = _sum_parts(sp_all)
    full_grads = dict(zip(small_names, _unpack(sp_sum, sp_shapes)))
    per_dev = dict(zip(small_names, _unpack(sp_all, sp_shapes, lead=(N_DEV,))))

    def my_cols(a, width):
        return lax.dynamic_slice_in_dim(a, me * width, width, axis=a.ndim - 1)

    grads = {}
    grads["b_ada"] = full_grads["b_ada"]
    grads["b_ada_kv"] = full_grads["b_ada_kv"]
    grads["gmlp_w_s"] = full_grads["gmlp_w_s"]
    grads["gmlp_b_s"] = full_grads["gmlp_b_s"]
    for k in ("ln_g", "ln_b", "gmlp_b_in", "gmlp_ln_g", "gmlp_ln_b", "attn_rel_bias"):
        grads[k] = my_cols(full_grads[k], weights[k].shape[-1])

    dmod_cols = jnp.transpose(my_cols(per_dev["b_ada"], mod_w), (1, 0, 2))[:, None]
    grads["w_ada"] = _matmul("ada_bwd_w", c4, dmod_cols, (DEPTH, 1, d, mod_w), F32, ta=True,
                             a_silu=True)[:, 0]
    dkv_cols = my_cols(per_dev["b_ada_kv"], kv_w)[None, None]
    grads["w_ada_kv"] = _matmul("ada_kv_bwd_w", c4, dkv_cols, (1, 1, d, kv_w), F32, ta=True,
                                a_silu=True)[0, 0]

    delta, new_m, new_v = {}, {}, {}
    first = jnp.zeros((1,), jnp.int32)

    def flat2(a, cols):
        return a.reshape(-1, cols)

    done = None
    for k in ("w_ada", "w_ada_kv"):
        w = weights[k]
        cols = w.shape[-1]
        res = _adamw(grads[k].reshape(1, -1, cols), first, None, flat2(w, cols), flat2(mom1[k], cols),
                     flat2(mom2[k], cols), after=done)
        grads[k], delta[k], new_m[k], new_v[k] = (a.reshape(w.shape) for a in res)
        done = res[0][:SUBLANES, :LANES]

    tiny = [k for k in order if k not in delta and k not in big_names]
    tiny_shapes = [weights[k].shape for k in tiny]
    tiny_out = _adamw((_pack([grads[k] for k in tiny]) + sent_token)[None], first, None,
                      _pack([weights[k] for k in tiny]), _pack([mom1[k] for k in tiny]),
                      _pack([mom2[k] for k in tiny]), after=done)
    for dst, arr in zip((grads, delta, new_m, new_v), tiny_out):
        for k, val in zip(tiny, _unpack(arr, tiny_shapes)):
            dst[k] = val

    bufs = {k: [lax.empty(flat2(weights[k], weights[k].shape[-1]).shape, F32) for _ in range(4)]
            for k in big_names}
    done = tiny_out[0]
    me_idx = me.astype(jnp.int32).reshape(1)
    for l in reversed(range(DEPTH)):
        send_sems, recv_sems, srcs, lands, _ = reductions[l]
        srcs, lands = _partials_send_wait(f"partials_send_wait_{l}", srcs, lands, send_sems, recv_sems, l > 0, done)
        for k, own, got in zip(bwd_groups[l], srcs, lands):
            cols = weights[k].shape[-1]
            slot_rows = int(np.prod(big[k].shape[1:-1]))
            bufs[k] = _adamw(own.reshape(own.shape[0], -1, cols), me_idx if l > 0 else chip,
                             got.reshape(got.shape[0], -1, cols),
                             flat2(weights[k], cols), flat2(mom1[k], cols), flat2(mom2[k], cols),
                             row0=bwd_groups[l][k][0] * slot_rows, bufs=bufs[k], after=done)
            done = bufs[k][0][:SUBLANES, :LANES]
    for k in big_names:
        grads[k], delta[k], new_m[k], new_v[k] = (b.reshape(weights[k].shape) for b in bufs[k])

    return (loss, grad_x, *[grads[k] for k in order], *[delta[k] for k in order],
            *[new_m[k] for k in order], *[new_v[k] for k in order])
```

```python
import functools

import numpy as np
import jax
import jax.numpy as jnp
from jax import lax
from jax.experimental import pallas as pl
from jax.experimental.pallas import tpu as pltpu

F32 = jnp.float32
BF16 = jnp.bfloat16
MESH_AXES = ("x", "y", "c")
N_DEV = 8
MESH_ID = pl.DeviceIdType.MESH

DEPTH = 4
N_A = 2
CHUNK = 64
N_HEADS = 16
LEFT_CHUNKS = 8
BAND = (LEFT_CHUNKS + 1) * CHUNK
LEFT_PAD = LEFT_CHUNKS * CHUNK
MAX_REL = 4 * CHUNK
N_REL = (CHUNK - 1) + MAX_REL + 1
GMLP_WINDOW = 128
GMLP_GROUPS = 8
ALPHA = (2.0 * DEPTH) ** 0.25
LN_EPS = 1e-5
ADAM_LR = 0.001
ADAM_B1 = 0.9
ADAM_B2 = 0.999
ADAM_EPS = 1e-08
ADAM_WD = 0.01
ADAM_STEP = 10

V7X_VMEM_BYTES = 64 * 1024 * 1024
VMEM_LIMIT = V7X_VMEM_BYTES - 8 * 1024 * 1024
LANES = 128
SUBLANES = 8
MM_BLOCK = 2048
BIG_ROW_BLOCK = 1024
ROW_BLOCK = 512
OPT_ROW_BLOCK = 256

_ANY = pl.BlockSpec(memory_space=pl.ANY)
_VMEM = pl.BlockSpec(memory_space=pltpu.VMEM)


def _params(sem=None):
    return pltpu.CompilerParams(dimension_semantics=sem, vmem_limit_bytes=VMEM_LIMIT)


def _row_block(rows, target):
    for d in range(min(rows, target), 0, -1):
        if rows % d == 0 and (d % SUBLANES == 0 or d == rows):
            return d
    return rows


def _matmul(name, a, b, out_shape4, out_dtype, *, la=0, lb=0, lo=0, ta=False, tb=False,
            reduce=False, b_merge=1, out_merge=1, out_buf=None, a_silu=False):
    ja_n, _, a_r, a_c = a.shape
    jb_n, _, b_r, b_c = b.shape
    jo_n, _, o_r, o_c = out_shape4
    m_tot = a_c if ta else a_r
    k_a = a_r if ta else a_c
    b_rows = b_merge * b_r
    k_c = b_c if tb else b_rows
    n = b_rows if tb else b_c
    n_chunks = (jb_n // b_merge) if reduce else 1
    natural_k = reduce and ja_n == 1
    assert n == o_c, (name, n, o_c)
    assert k_a ==(k_c * n_chunks if natural_k else k_c), (name, k_a, k_c, n_chunks)
    bk = k_c if (k_c <= MM_BLOCK or (b_merge > 1 and not tb)) else MM_BLOCK
    assert k_c % bk == 0
    nkk = k_c // bk
    nk = n_chunks * nkk
    m_out = out_merge * o_r
    assert m_tot == m_out, (name, m_tot, m_out)
    bm = m_tot if (m_tot <= MM_BLOCK or out_merge > 1) else MM_BLOCK
    assert m_tot % bm == 0
    jo_blocks = jo_n // out_merge

    def a_index(j, m, k):
        kj, kk = k // nkk, k % nkk
        ja = 0 if ja_n == 1 else (kj if reduce else j)
        ke = kk + kj * nkk if natural_k else kk
        return (ja, la, ke, m) if ta else (ja, la, m, ke)

    def b_index(j, m, k):
        kj, kk = k // nkk, k % nkk
        jb = 0 if jb_n == b_merge else (kj if reduce else j)
        return (jb, lb, 0, kk) if tb else (jb, lb, kk, 0)

    def o_index(j, m, k):
        return (j, lo, 0, 0) if out_merge > 1 else (j, lo, m, 0)

    a_block = (None, None, bk, bm) if ta else (None, None, bm, bk)
    if b_merge > 1:
        b_block = (b_merge, None, b_r, bk if tb else n)
    else:
        b_block = (None, None, n, bk) if tb else (None, None, bk, n)
    o_block = (out_merge, None, o_r, n) if out_merge > 1 else (None, None, bm, n)
    dims = (((0 if ta else 1,), (1 if tb else 0,)), ((), ()))

    in_place = nk > 1 and out_dtype == F32 and out_merge == 1
    use_acc = nk > 1 and not in_place

    def body(a_ref, b_ref, *rest):
        o_ref = rest[-2] if use_acc else rest[-1]
        k = pl.program_id(2)
        av = a_ref[...]
        if a_silu:
            af = av.astype(F32)
            av = af * jax.nn.sigmoid(af)
        bv = b_ref[...]
        if b_merge > 1:
            bv = bv.reshape(b_rows, bv.shape[-1])
        prod = lax.dot_general(av.astype(BF16), bv.astype(BF16), dims, preferred_element_type=F32)

        def emit(val):
            val = val.astype(out_dtype)
            o_ref[...] = val.reshape(out_merge, o_r, n) if out_merge > 1 else val

        if nk == 1:
            emit(prod)
            return
        acc_ref = o_ref if in_place else rest[-1]

        @pl.when(k == 0)
        def _():
            acc_ref[...] = prod

        @pl.when(k > 0)
        def _():
            acc_ref[...] += prod

        if use_acc:
            @pl.when(k == nk - 1)
            def _():
                emit(acc_ref[...])

    in_specs = [pl.BlockSpec(a_block, a_index), pl.BlockSpec(b_block, b_index)]
    operands = [a, b]
    aliases = {}
    if out_buf is not None:
        assert out_buf.shape == tuple(out_shape4) and out_buf.dtype == out_dtype
        in_specs.append(_ANY)
        operands.append(out_buf)
        aliases = {2: 0}
    return pl.pallas_call(
        body, name=name,
        grid=(jo_blocks, m_tot // bm, nk),
        in_specs=in_specs,
        out_specs=pl.BlockSpec(o_block, o_index),
        out_shape=jax.ShapeDtypeStruct(tuple(out_shape4), out_dtype),
        scratch_shapes=[pltpu.VMEM((bm, n), F32)] if use_acc else [],
        input_output_aliases=aliases,
        compiler_params=_params(("parallel", "parallel", "arbitrary")),
    )(*operands)


def _as4(a):
    return a.reshape((1,) * (4 - a.ndim) + a.shape)


def _row_call(name, body, ins, outs, t, *, acc_outs=()):
    bt = _row_block(t, ROW_BLOCK)

    def spec(arr, tiled):
        if tiled:
            return pl.BlockSpec((bt,) + tuple(arr.shape[1:]), lambda i: (i,) + (0,) * (arr.ndim - 1))
        return pl.BlockSpec(tuple(arr.shape), lambda i: (0,) * arr.ndim)

    return pl.pallas_call(
        body, name=name, grid=(t // bt,),
        in_specs=[spec(a, tl) for a, tl in ins],
        out_specs=[spec(o, tl) for o, tl in outs],
        out_shape=[jax.ShapeDtypeStruct(o.shape, o.dtype) for o, _ in outs],
        compiler_params=_params(("arbitrary",) if acc_outs else ("parallel",)),
    )(*[a for a, _ in ins])


def _sds(shape, dtype):
    return jax.ShapeDtypeStruct(tuple(shape), dtype)


def _modulate(x, scl, shift):
    t, d = x.shape

    def body(x_ref, s_ref, b_ref, h_ref):
        h_ref[...] = (x_ref[...] * (1.0 + s_ref[...]) + b_ref[...]).astype(BF16)

    return _row_call("modulate", body, [(x, True), (scl, False), (shift, False)],
                     [(_sds((t, d), BF16), True)], t)[0]


def _ln_stats(r):
    mu = jnp.mean(r, axis=-1, keepdims=True)
    rc = r - mu
    var = jnp.mean(rc * rc, axis=-1, keepdims=True)
    rstd = lax.rsqrt(var + LN_EPS)
    return rc * rstd, rstd


def _ln_res_fwd(x, y, gw, g, b, mods=()):
    t, d = x.shape
    n_mod = len(mods)

    def body(x_ref, y_ref, gw_ref, g_ref, b_ref, *rest):
        mod_refs, o_ref, h_refs = rest[:2 * n_mod], rest[2 * n_mod], rest[2 * n_mod + 1:]
        r = ALPHA * x_ref[...] + gw_ref[...] * y_ref[...]
        xhat, _ = _ln_stats(r)
        xn = xhat * g_ref[...] + b_ref[...]
        o_ref[...] = xn
        for k in range(n_mod):
            h_refs[k][...] = (xn * (1.0 + mod_refs[2 * k][...]) + mod_refs[2 * k + 1][...]).astype(BF16)

    vecs = [(v, False) for pair in mods for v in pair]
    return _row_call("ln_res_fwd", body,
                     [(x, True), (y, True), (gw, False), (g, False), (b, False)] + vecs,
                     [(_sds((t, d), F32), True)] + [(_sds((t, d), BF16), True)] * n_mod, t)


def _ln_res_bwd(x, y, gw, g, b, dx_base, pairs=()):
    t, d = x.shape
    n_pair = len(pairs)

    def body(x_ref, y_ref, gw_ref, g_ref, b_ref, dxb_ref, *rest):
        pair_refs, outs = rest[:2 * n_pair], rest[2 * n_pair:]
        dx_ref, dy_ref = outs[0], outs[1]
        sums = outs[2:]

        @pl.when(pl.program_id(0) == 0)
        def _():
            for r in sums:
                r[...] = jnp.zeros_like(r)

        yv = y_ref[...]
        gwv = gw_ref[...]
        gv = g_ref[...]
        xhat, rstd = _ln_stats(ALPHA * x_ref[...] + gwv * yv)
        dxn = dxb_ref[...]
        if n_pair:
            xn = xhat * gv + b_ref[...]
            for k in range(n_pair):
                dh = pair_refs[2 * k][...]
                dxn = dxn + dh * (1.0 + pair_refs[2 * k + 1][...])
                sums[3 + 2 * k][...] += jnp.sum(dh * xn, axis=0, keepdims=True)
                sums[4 + 2 * k][...] += jnp.sum(dh, axis=0, keepdims=True)
        dxh = dxn * gv
        m1 = jnp.mean(dxh, axis=-1, keepdims=True)
        m2 = jnp.mean(dxh * xhat, axis=-1, keepdims=True)
        dr = rstd * (dxh - m1 - xhat * m2)
        dx_ref[...] = ALPHA * dr
        dy_ref[...] = (gwv * dr).astype(BF16)
        sums[0][...] += jnp.sum(dr * yv, axis=0, keepdims=True)
        sums[1][...] += jnp.sum(dxn * xhat, axis=0, keepdims=True)
        sums[2][...] += jnp.sum(dxn, axis=0, keepdims=True)

    vec = _sds((1, d), F32)
    n_sum = 3 + 2 * n_pair
    ins = [(x, True), (y, True), (gw, False), (g, False), (b, False), (dx_base, True)]
    for dh, scl in pairs:
        ins += [(dh, True), (scl, False)]
    return _row_call("ln_res_bwd", body, ins,
                     [(_sds((t, d), F32), True), (_sds((t, d), BF16), True)] + [(vec, False)] * n_sum, t,
                     acc_outs=tuple(range(2, 2 + n_sum)))


def _mod_bwd(dx_res, dh, x, scl):
    t, d = x.shape

    def body(dxr_ref, dh_ref, x_ref, s_ref, dx_ref, ds_ref, db_ref):
        @pl.when(pl.program_id(0) == 0)
        def _():
            ds_ref[...] = jnp.zeros_like(ds_ref)
            db_ref[...] = jnp.zeros_like(db_ref)

        dh = dh_ref[...]
        dx_ref[...] = dxr_ref[...] + dh * (1.0 + s_ref[...])
        ds_ref[...] += jnp.sum(dh * x_ref[...], axis=0, keepdims=True)
        db_ref[...] += jnp.sum(dh, axis=0, keepdims=True)

    vec = _sds((1, d), F32)
    return _row_call("mod_bwd", body, [(dx_res, True), (dh, True), (x, True), (scl, False)],
                     [(_sds((t, d), F32), True), (vec, False), (vec, False)], t, acc_outs=(1, 2))


def _loss_head(y, target):
    t, d = y.shape

    def body(y_ref, t_ref, l_ref, dy_ref):
        @pl.when(pl.program_id(0) == 0)
        def _():
            l_ref[...] = jnp.zeros_like(l_ref)

        err = y_ref[...] - t_ref[...]
        dy_ref[...] = err * (1.0 / d)
        part = 0.5 * jnp.sum(jnp.mean(err * err, axis=-1, keepdims=True), axis=0, keepdims=True)
        l_ref[...] += jnp.broadcast_to(part, l_ref.shape)

    return _row_call("loss_head", body, [(y, True), (target, True)],
                     [(_sds((SUBLANES, LANES), F32), False), (_sds((t, d), F32), True)], t,
                     acc_outs=(0,))


def _ffn_up_fwd(h, w_gu, lb):
    t, d = h.shape
    n = w_gu.shape[-1]
    half = N_DEV // 2
    bt = _row_block(t, BIG_ROW_BLOCK)

    def body(h_ref, wg_ref, wu_ref, gu_ref, a_ref):
        hv = h_ref[...]
        g = jnp.dot(hv, wg_ref[...], preferred_element_type=F32)
        u = jnp.dot(hv, wu_ref[...], preferred_element_type=F32)
        gu_ref[0] = g
        gu_ref[1] = u
        a_ref[...] = (g * jax.nn.sigmoid(g) * u).astype(BF16)

    return pl.pallas_call(
        body, name="ffn_up_fwd", grid=(half, t // bt),
        in_specs=[pl.BlockSpec((bt, d), lambda j, i: (i, 0)),
                  pl.BlockSpec((None, None, d, n), lambda j, i: (j, lb, 0, 0)),
                  pl.BlockSpec((None, None, d, n), lambda j, i: (half + j, lb, 0, 0))],
        out_specs=[pl.BlockSpec((2, None, bt, n), lambda j, i: (0, j, i, 0)),
                   pl.BlockSpec((None, bt, n), lambda j, i: (j, i, 0))],
        out_shape=[_sds((2, half, t, n), F32), _sds((half, t, n), BF16)],
        compiler_params=_params(("parallel", "parallel")),
    )(h, w_gu, w_gu)


def _ffn_down_bwd_a(dy, w_down, lb, gu):
    t, d = dy.shape
    _, half, _, n = gu.shape
    r = w_down.shape[2]
    bt = _row_block(t, BIG_ROW_BLOCK)

    def body(dy_ref, w_ref, gu_ref, d_ref):
        da = lax.dot_general(dy_ref[...], w_ref[...].reshape(2 * r, d), (((1,), (1,)), ((), ())),
                             preferred_element_type=F32)
        g = gu_ref[0]
        u = gu_ref[1]
        sig = jax.nn.sigmoid(g)
        d_ref[0] = (da * u * sig * (1.0 + g * (1.0 - sig))).astype(BF16)
        d_ref[1] = (da * g * sig).astype(BF16)

    return pl.pallas_call(
        body, name="ffn_down_bwd_a", grid=(half, t // bt),
        in_specs=[pl.BlockSpec((bt, d), lambda j, i: (i, 0)),
                  pl.BlockSpec((2, None, r, d), lambda j, i: (j, lb, 0, 0)),
                  pl.BlockSpec((2, None, bt, n), lambda j, i: (0, j, i, 0))],
        out_specs=pl.BlockSpec((2, None, bt, n), lambda j, i: (0, j, i, 0)),
        out_shape=_sds((2, half, t, n), BF16),
        compiler_params=_params(("parallel", "parallel")),
    )(dy, w_down, gu)


_INV_SQRT2 = 0.7071067811865476
_INV_SQRT_2PI = 0.3989422804014327


def _gelu(z):
    return 0.5 * z * (1.0 + lax.erf(z * _INV_SQRT2))


def _gelu_grad(z):
    return 0.5 * (1.0 + lax.erf(z * _INV_SQRT2)) + z * jnp.exp(-0.5 * z * z) * _INV_SQRT_2PI


def _window_mask():
    t_out = lax.broadcasted_iota(jnp.int32, (GMLP_WINDOW, GMLP_WINDOW), 0)
    s_in = lax.broadcasted_iota(jnp.int32, (GMLP_WINDOW, GMLP_WINDOW), 1)
    return (s_in // CHUNK) <= (t_out // CHUNK)


def _gmlp_recompute(z_ref, bin_ref, lng_ref, lnb_ref):
    half = N_DEV // 2
    z = z_ref[...] + bin_ref[...]
    ge = _gelu(z)
    u = ge[:half]
    v = ge[half:]
    width = half * v.shape[-1]
    mu = jnp.sum(jnp.sum(v, axis=0), axis=-1, keepdims=True) / width
    vc = v - mu
    var = jnp.sum(jnp.sum(vc * vc, axis=0), axis=-1, keepdims=True) / width
    rstd = lax.rsqrt(var + LN_EPS)
    xhat = vc * rstd
    vn = xhat * lng_ref[...] + lnb_ref[...]
    return z, u, xhat, rstd, vn


def _gmlp_mid_fwd(zpre, b_in, ln_g, ln_b, w_s, b_s):
    _, t, n = zpre.shape
    half = N_DEV // 2
    gd = half * n // GMLP_GROUPS
    per = n // gd
    w = GMLP_WINDOW

    def body(z_ref, bin_ref, lng_ref, lnb_ref, ws_ref, bs_ref, o_ref):
        _, u, _, _, vn = _gmlp_recompute(z_ref, bin_ref, lng_ref, lnb_ref)
        mask = _window_mask()
        for g in range(GMLP_GROUPS):
            sh, c0 = g // per, (g % per) * gd
            wsm = jnp.where(mask, ws_ref[g], 0.0).astype(BF16)
            s = jnp.dot(wsm, vn[sh][:, c0:c0 + gd].astype(BF16), preferred_element_type=F32) + bs_ref[g]
            o_ref[sh, :, c0:c0 + gd] = (u[sh][:, c0:c0 + gd] * s).astype(BF16)

    whole = lambda a: pl.BlockSpec(tuple(a.shape), lambda i: (0,) * a.ndim)
    return pl.pallas_call(
        body, name="gmlp_mid_fwd", grid=(t // w,),
        in_specs=[pl.BlockSpec((N_DEV, w, n), lambda i: (0, i, 0)),
                  whole(b_in), whole(ln_g), whole(ln_b), whole(w_s), whole(b_s)],
        out_specs=pl.BlockSpec((half, w, n), lambda i: (0, i, 0)),
        out_shape=_sds((half, t, n), BF16),
        compiler_params=_params(("parallel",)),
    )(zpre, b_in, ln_g, ln_b, w_s, b_s)


def _gmlp_mid_bwd(zpre, dgated, b_in, ln_g, ln_b, w_s, b_s):
    _, t, n = zpre.shape
    half = N_DEV // 2
    gd = half * n // GMLP_GROUPS
    per = n // gd
    w = GMLP_WINDOW
    width = half * n

    def body(z_ref, dg_ref, bin_ref, lng_ref, lnb_ref, ws_ref, bs_ref,
             dz_ref, dws_ref, dbs_ref, dlng_ref, dlnb_ref, dbin_ref, du_ref, dvn_ref):
        @pl.when(pl.program_id(0) == 0)
        def _():
            for r in (dws_ref, dbs_ref, dlng_ref, dlnb_ref, dbin_ref):
                r[...] = jnp.zeros_like(r)

        z, u, xhat, rstd, vn = _gmlp_recompute(z_ref, bin_ref, lng_ref, lnb_ref)
        mask = _window_mask()
        for g in range(GMLP_GROUPS):
            sh, c0 = g // per, (g % per) * gd
            wsm = jnp.where(mask, ws_ref[g], 0.0).astype(BF16)
            vg = vn[sh][:, c0:c0 + gd].astype(BF16)
            s = jnp.dot(wsm, vg, preferred_element_type=F32) + bs_ref[g]
            dgt = dg_ref[sh, :, c0:c0 + gd]
            ds = dgt * u[sh][:, c0:c0 + gd]
            du_ref[sh, :, c0:c0 + gd] = dgt * s
            dsb = ds.astype(BF16)
            dws = lax.dot_general(dsb, vg, (((1,), (1,)), ((), ())), preferred_element_type=F32)
            dws_ref[g] += jnp.where(mask, dws, 0.0)
            dbs_ref[g] += jnp.sum(ds, axis=-1, keepdims=True)
            dvn_ref[sh, :, c0:c0 + gd] = lax.dot_general(wsm, dsb, (((0,), (0,)), ((), ())),
                                                         preferred_element_type=F32)
        dvn = dvn_ref[...]
        dlng_ref[...] += jnp.sum(dvn * xhat, axis=1, keepdims=True)
        dlnb_ref[...] += jnp.sum(dvn, axis=1, keepdims=True)
        dxh = dvn * lng_ref[...]
        m1 = jnp.sum(jnp.sum(dxh, axis=0), axis=-1, keepdims=True) / width
        m2 = jnp.sum(jnp.sum(dxh * xhat, axis=0), axis=-1, keepdims=True) / width
        dv = rstd * (dxh - m1 - xhat * m2)
        gg = _gelu_grad(z)
        dzu = du_ref[...] * gg[:half]
        dzv = dv * gg[half:]
        dz_ref[:half] = dzu.astype(BF16)
        dz_ref[half:] = dzv.astype(BF16)
        dbin_ref[:half] += jnp.sum(dzu, axis=1, keepdims=True)
        dbin_ref[half:] += jnp.sum(dzv, axis=1, keepdims=True)

    whole = lambda a: pl.BlockSpec(tuple(a.shape), lambda i: (0,) * a.ndim)
    outs = [_sds((N_DEV, t, n), BF16), _sds(w_s.shape, F32), _sds(b_s.shape, F32),
            _sds(ln_g.shape, F32), _sds(ln_b.shape, F32), _sds(b_in.shape, F32)]
    return pl.pallas_call(
        body, name="gmlp_mid_bwd", grid=(t // w,),
        in_specs=[pl.BlockSpec((N_DEV, w, n), lambda i: (0, i, 0)),
                  pl.BlockSpec((half, w, n), lambda i: (0, i, 0)),
                  whole(b_in), whole(ln_g), whole(ln_b), whole(w_s), whole(b_s)],
        out_specs=[pl.BlockSpec((N_DEV, w, n), lambda i: (0, i, 0))] + [whole(o) for o in outs[1:]],
        out_shape=outs,
        scratch_shapes=[pltpu.VMEM((half, w, n), F32), pltpu.VMEM((half, w, n), F32)],
        compiler_params=_params(("arbitrary",)),
    )(zpre, dgated, b_in, ln_g, ln_b, w_s, b_s)


ATTN_CHUNKS = 4
ATTN_ROWS = ATTN_CHUNKS * CHUNK
ATTN_WINDOW = ATTN_ROWS + LEFT_PAD
ATTN_DIAGS = 1024
ATTN_ROLL = ATTN_DIAGS - (ATTN_ROWS - 1)


def _rel_vector(rel):
    j = np.arange(ATTN_DIAGS)
    idx = np.clip(ATTN_WINDOW - 1 - j, -(CHUNK - 1), MAX_REL) + (CHUNK - 1)
    return rel[:, idx]


def _attn_bias_mask(rel_ref, bm_ref):
    tt = lax.broadcasted_iota(jnp.int32, (ATTN_ROWS, ATTN_WINDOW), 0) // CHUNK
    rr = lax.broadcasted_iota(jnp.int32, (ATTN_ROWS, ATTN_WINDOW), 1) // CHUNK
    band = (rr >= tt) & (rr <= tt + LEFT_CHUNKS)
    for j in range(bm_ref.shape[0]):
        vec = jnp.broadcast_to(rel_ref[j:j + 1, :], (ATTN_ROWS, ATTN_DIAGS))
        toeplitz = pltpu.roll(vec, ATTN_ROLL, 1, stride=1, stride_axis=0)[:, :ATTN_WINDOW]
        bm_ref[j] = jnp.where(band, toeplitz, -jnp.inf)


def _attn_probs(q_ref, k_ref, bm_ref, j, hd, start, valid):
    qh = q_ref[:, j * hd:(j + 1) * hd]
    kb = k_ref[pl.ds(start, ATTN_WINDOW), j * hd:(j + 1) * hd]
    sc = lax.dot_general(qh, kb, (((1,), (1,)), ((), ())), preferred_element_type=F32)
    sc = sc * (hd ** -0.5) + bm_ref[j]
    sc = jnp.where(valid, sc, -jnp.inf)
    sc = sc - jnp.max(sc, axis=-1, keepdims=True)
    e = jnp.exp(sc)
    return e / jnp.sum(e, axis=-1, keepdims=True), qh, kb


def _window_valid(start):
    r = lax.broadcasted_iota(jnp.int32, (1, ATTN_WINDOW), 1)
    return (start + r) >= LEFT_PAD


def _attn_fwd(q, kvp, rel_vec):
    t, d = q.shape
    hd = d // N_HEADS
    half = N_DEV // 2
    n = kvp.shape[-1]
    per = n // hd
    rows = kvp.shape[1]

    def body(q_ref, k_ref, v_ref, rel_ref, o_ref, bm_ref):
        @pl.when(pl.program_id(1) == 0)
        def _():
            _attn_bias_mask(rel_ref, bm_ref)

        start = pl.multiple_of(pl.program_id(1) * ATTN_ROWS, ATTN_ROWS)
        valid = _window_valid(start)
        for j in range(per):
            p, _, _ = _attn_probs(q_ref, k_ref, bm_ref, j, hd, start, valid)
            vb = v_ref[pl.ds(start, ATTN_WINDOW), j * hd:(j + 1) * hd]
            o_ref[:, j * hd:(j + 1) * hd] = jnp.dot(p.astype(BF16), vb, preferred_element_type=F32).astype(BF16)

    return pl.pallas_call(
        body, name="attn_fwd", grid=(half, t // ATTN_ROWS),
        in_specs=[pl.BlockSpec((ATTN_ROWS, n), lambda g, i: (i, g)),
                  pl.BlockSpec((None, rows, n), lambda g, i: (g, 0, 0)),
                  pl.BlockSpec((None, rows, n), lambda g, i: (half + g, 0, 0)),
                  pl.BlockSpec((None, per, ATTN_DIAGS), lambda g, i: (g, 0, 0))],
        out_specs=pl.BlockSpec((ATTN_ROWS, n), lambda g, i: (i, g)),
        out_shape=_sds((t, d), BF16),
        scratch_shapes=[pltpu.VMEM((per, ATTN_ROWS, ATTN_WINDOW), F32)],
        compiler_params=_params(("arbitrary", "arbitrary")),
    )(q, kvp, kvp, rel_vec.reshape(half, per, ATTN_DIAGS))


def _attn_bwd(q, dout, kvp, rel_vec, dk_in=None, dv_in=None):
    t, d = q.shape
    hd = d // N_HEADS
    half = N_DEV // 2
    n = kvp.shape[-1]
    per = n // hd
    rows = kvp.shape[1]
    scale = hd ** -0.5
    carry = dk_in is not None

    def body(q_ref, do_ref, k_ref, v_ref, rel_ref, *rest):
        dq_ref, dk_ref, dv_ref, dsc_ref, bm_ref = rest[-5:]

        @pl.when(pl.program_id(1) == 0)
        def _():
            _attn_bias_mask(rel_ref, bm_ref)
            dk_ref[...] = rest[0][...] if carry else jnp.zeros_like(dk_ref)
            dv_ref[...] = rest[1][...] if carry else jnp.zeros_like(dv_ref)
            dsc_ref[...] = jnp.zeros_like(dsc_ref)

        start = pl.multiple_of(pl.program_id(1) * ATTN_ROWS, ATTN_ROWS)
        valid = _window_valid(start)
        for j in range(per):
            cols = slice(j * hd, (j + 1) * hd)
            p, qh, kb = _attn_probs(q_ref, k_ref, bm_ref, j, hd, start, valid)
            vb = v_ref[pl.ds(start, ATTN_WINDOW), cols]
            doh = do_ref[:, cols]
            dp = lax.dot_general(doh, vb, (((1,), (1,)), ((), ())), preferred_element_type=F32)
            ds = p * (dp - jnp.sum(dp * p, axis=-1, keepdims=True))
            dsc_ref[j] += sum(ds[a * CHUNK:(a + 1) * CHUNK, a * CHUNK:a * CHUNK + BAND]
                              for a in range(ATTN_CHUNKS))
            dsb = (ds * scale).astype(BF16)
            dq_ref[:, cols] = jnp.dot(dsb, kb, preferred_element_type=F32).astype(BF16)
            dk_ref[pl.ds(start, ATTN_WINDOW), cols] += lax.dot_general(
                dsb, qh, (((0,), (0,)), ((), ())), preferred_element_type=F32)
            dv_ref[pl.ds(start, ATTN_WINDOW), cols] += lax.dot_general(
                p.astype(BF16), doh, (((0,), (0,)), ((), ())), preferred_element_type=F32)

    tile = pl.BlockSpec((ATTN_ROWS, n), lambda g, i: (i, g))
    shard = pl.BlockSpec((None, rows, n), lambda g, i: (g, 0, 0))
    in_specs = [tile, tile, shard, pl.BlockSpec((None, rows, n), lambda g, i: (half + g, 0, 0)),
                pl.BlockSpec((None, per, ATTN_DIAGS), lambda g, i: (g, 0, 0))]
    operands = [q, dout, kvp, kvp, rel_vec.reshape(half, per, ATTN_DIAGS)]
    if carry:
        in_specs += [shard, shard]
        operands += [dk_in, dv_in]
    acc = _sds((half, rows, n), F32)
    return pl.pallas_call(
        body, name="attn_bwd", grid=(half, t // ATTN_ROWS),
        in_specs=in_specs,
        out_specs=[tile, shard, shard, pl.BlockSpec((per, CHUNK, BAND), lambda g, i: (g, 0, 0))],
        out_shape=[_sds((t, d), BF16), acc, acc, _sds((N_HEADS, CHUNK, BAND), F32)],
        scratch_shapes=[pltpu.VMEM((per, ATTN_ROWS, ATTN_WINDOW), F32)],
        compiler_params=_params(("arbitrary", "arbitrary")),
    )(*operands)


SKEW_PITCH = 640
SKEW = SKEW_PITCH + 1
SKEW_LANES = -(-SKEW // LANES) * LANES


def _skew_diagonals(dsc):
    h = dsc.shape[0]
    wide = jnp.pad(dsc, ((0, 0), (0, 0), (0, SKEW_PITCH - BAND))).reshape(h, CHUNK * SKEW_PITCH)
    wide = jnp.pad(wide, ((0, 0), (0, CHUNK))).reshape(h, CHUNK, SKEW)
    return jnp.pad(wide, ((0, 0), (0, 0), (0, SKEW_LANES - SKEW)))


def _rel_bias_grad(skewed):
    heads = skewed.shape[0]
    hb = SUBLANES

    def body(d_ref, o_ref):
        col = lax.broadcasted_iota(jnp.int32, (SKEW_LANES, N_REL), 0)
        bucket = lax.broadcasted_iota(jnp.int32, (SKEW_LANES, N_REL), 1)
        diag = jnp.where(col < BAND, col, col - SKEW)
        idx = jnp.clip(LEFT_PAD - diag, -(CHUNK - 1), MAX_REL) + (CHUNK - 1)
        oh = ((idx == bucket) & (col < SKEW)).astype(BF16)
        dv = jnp.sum(d_ref[...], axis=1)
        hi = dv.astype(BF16)
        rest = dv - hi.astype(F32)
        mid = rest.astype(BF16)
        lo = (rest - mid.astype(F32)).astype(BF16)
        acc = jnp.dot(hi, oh, preferred_element_type=F32)
        acc += jnp.dot(mid, oh, preferred_element_type=F32)
        acc += jnp.dot(lo, oh, preferred_element_type=F32)
        o_ref[...] = acc

    return pl.pallas_call(
        body, name="rel_bias_grad", grid=(heads // hb,),
        in_specs=[pl.BlockSpec((hb, CHUNK, SKEW_LANES), lambda i: (i, 0, 0))],
        out_specs=pl.BlockSpec((hb, N_REL), lambda i: (i, 0)),
        out_shape=_sds((heads, N_REL), F32),
        compiler_params=_params(("parallel",)),
    )(skewed)


def _sum_parts(parts):
    s_n, rows, c = parts.shape
    br = _row_block(rows, OPT_ROW_BLOCK)

    def body(p_ref, o_ref):
        acc = p_ref[0].astype(F32)
        for s in range(1, s_n):
            acc = acc + p_ref[s].astype(F32)
        o_ref[...] = acc

    return pl.pallas_call(
        body, name="sum_parts", grid=(rows // br,),
        in_specs=[pl.BlockSpec((s_n, br, c), lambda i: (0, i, 0))],
        out_specs=pl.BlockSpec((br, c), lambda i: (i, 0)),
        out_shape=_sds((rows, c), F32),
        compiler_params=_params(("parallel",)),
    )(parts)


def _adamw(own, own_idx, parts, w, m, v, row0=0, bufs=None, after=None):
    _, rows, c = own.shape
    s_n = 0 if parts is None else parts.shape[0]
    total = w.shape[0]
    br = _row_block(rows, OPT_ROW_BLOCK)
    assert row0 % br == 0 and (bufs is not None or (row0 == 0 and total == rows))
    b0 = row0 // br
    m_corr = 1.0 - ADAM_B1 ** ADAM_STEP
    v_corr = 1.0 - ADAM_B2 ** ADAM_STEP

    def body(idx_ref, own_ref, *refs):
        if s_n:
            p_ref, refs = refs[0], refs[1:]
        w_ref, m_ref, v_ref = refs[:3]
        g_ref, d_ref, nm_ref, nv_ref = refs[-4:]
        g = own_ref[...].astype(F32)
        for s in range(s_n):
            g = g + p_ref[s].astype(F32)
        nm = ADAM_B1 * m_ref[...] + (1.0 - ADAM_B1) * g
        nv = ADAM_B2 * v_ref[...] + (1.0 - ADAM_B2) * (g * g)
        g_ref[...] = g
        nm_ref[...] = nm
        nv_ref[...] = nv
        d_ref[...] = -ADAM_LR * ((nm / m_corr) / (jnp.sqrt(nv / v_corr) + ADAM_EPS) + ADAM_WD * w_ref[...])

    tile = pl.BlockSpec((br, c), lambda i, idx: (i + b0, 0))
    in_specs = [pl.BlockSpec((None, br, c), lambda i, idx: (idx[0], i, 0))]
    operands = [own_idx, own]
    if s_n:
        in_specs.append(pl.BlockSpec((s_n, br, c), lambda i, idx: (0, i, 0)))
        operands.append(parts)
    in_specs += [tile, tile, tile]
    operands += [w, m, v]
    aliases = {}
    if bufs is not None:
        aliases = {len(operands) + j: j for j in range(4)}
        in_specs += [_ANY] * 4
        operands += list(bufs)
    if after is not None:
        in_specs.append(_ANY)
        operands.append(after)
    out = _sds((total, c), F32)
    return pl.pallas_call(
        body, name="adamw",
        grid_spec=pltpu.PrefetchScalarGridSpec(
            num_scalar_prefetch=1, grid=(rows // br,), in_specs=in_specs,
            out_specs=[tile, tile, tile, tile]),
        out_shape=[out, out, out, out],
        input_output_aliases=aliases,
        compiler_params=_params(("parallel",)),
    )(*operands)


def _chip_sum(p, r1, core):
    half = N_DEV // 2
    c = p.shape[-1]
    rows = int(np.prod(p.shape[1:-1]))
    br = _row_block(rows, BIG_ROW_BLOCK)

    def body(core_ref, p_ref, r_ref, o_ref):
        o_ref[...] = (p_ref[...].astype(F32) + r_ref[...].astype(F32)).astype(BF16)

    out = pl.pallas_call(
        body, name="chip_sum",
        grid_spec=pltpu.PrefetchScalarGridSpec(
            num_scalar_prefetch=1, grid=(half, rows // br),
            in_specs=[pl.BlockSpec((None, None, br, c), lambda q, i, cr: (q, cr[0], i, 0)),
                      pl.BlockSpec((None, br, c), lambda q, i, cr: (q, i, 0))],
            out_specs=pl.BlockSpec((None, br, c), lambda q, i, cr: (q, i, 0))),
        out_shape=_sds((half, rows, c), BF16),
        compiler_params=_params(("parallel", "parallel")),
    )(core, p.reshape(half, 2, rows, c), r1.reshape(half, rows, c))
    return out.reshape((half,) + p.shape[1:])


def _position():
    return tuple(lax.axis_index(a) for a in MESH_AXES)


def _linear(px, py, pc):
    return 4 * px + 2 * py + pc


def _all_gather_small(v, after=()):
    rows, lanes = v.shape

    def body(x_ref, *rest):
        out_ref, send_sems, recv_sems, local_sem = rest[-4:]
        x, y, c = _position()
        me, sibling = (x, y, c), (x, y, 1 - c)
        chips = [(1 - x, y), (x, 1 - y), (1 - x, 1 - y)]

        def copy(k, block, to, src=None):
            dst = out_ref.at[_linear(*block)]
            return pltpu.make_async_remote_copy(
                src_ref=dst if src is None else src, dst_ref=dst,
                send_sem=send_sems.at[k], recv_sem=recv_sems.at[k],
                device_id=to, device_id_type=MESH_ID)

        mine = pltpu.make_async_copy(x_ref, out_ref.at[_linear(*me)], local_sem)
        mine.start()
        first = [copy(0, me, sibling, src=x_ref)]
        first += [copy(1 + j, me, (*chip, c), src=x_ref) for j, chip in enumerate(chips)]
        for cp in first:
            cp.start()
        passed = [copy(4 + j, (*chip, c), sibling) for j, chip in enumerate(chips)]
        for j, chip in enumerate(chips):
            copy(1 + j, (*chip, c), me).wait_recv()
            passed[j].start()
        copy(0, sibling, me).wait_recv()
        for j, chip in enumerate(chips):
            copy(4 + j, (*chip, 1 - c), me).wait_recv()
        for cp in first + passed:
            cp.wait_send()
        mine.wait()

    return pl.pallas_call(
        body, name="all_gather_small",
        out_shape=_sds((N_DEV, rows, lanes), v.dtype),
        in_specs=[_VMEM] + [_ANY] * len(after), out_specs=_VMEM,
        scratch_shapes=[pltpu.SemaphoreType.DMA((7,)), pltpu.SemaphoreType.DMA((7,)),
                        pltpu.SemaphoreType.DMA],
        compiler_params=pltpu.CompilerParams(vmem_limit_bytes=VMEM_LIMIT),
    )(v, *after)


_HBM = pl.BlockSpec(memory_space=pltpu.HBM)
_SEM = pl.BlockSpec(memory_space=pltpu.SEMAPHORE)
_EFFECT = pltpu.SideEffectType.DATAFLOW_SIDE_EFFECTING
_ALL_CHIPS = [(0, 0), (0, 1), (1, 0), (1, 1)]


def _other_chips(x, y):
    return [(1 - x, y), (x, 1 - y), (1 - x, 1 - y)]


def _in_hbm(a):
    return pltpu.with_memory_space_constraint(a, pltpu.HBM)


def _token():
    return _sds((SUBLANES, LANES), F32)


def _gather_ici_copy(ref, i, k, chip, c, block, send_sems, recv_sems):
    return pltpu.make_async_remote_copy(
        src_ref=ref.at[block], dst_ref=ref.at[block],
        send_sem=send_sems.at[3 * i + k], recv_sem=recv_sems.at[3 * i + k],
        device_id=(*chip, c), device_id_type=MESH_ID)


def _gather_ici_start(name, lands):
    n = len(lands)

    def body(*refs):
        ins, send_sems, recv_sems, token = refs[:n], refs[n], refs[n + 1], refs[-1]
        x, y, c = _position()
        me = _linear(x, y, c)
        for i in range(n):
            for k, chip in enumerate(_other_chips(x, y)):
                _gather_ici_copy(ins[i], i, k, chip, c, me, send_sems, recv_sems).start()
        token[...] = jnp.zeros_like(token)

    out = pl.pallas_call(
        body, name=name,
        out_shape=(pltpu.SemaphoreType.DMA((3 * n,)), pltpu.SemaphoreType.DMA((3 * n,)),
                   *[pltpu.HBM(a.shape, a.dtype) for a in lands], _token()),
        in_specs=[_HBM] * n, out_specs=(_SEM, _SEM, *[_HBM] * n, _VMEM),
        input_output_aliases={i: 2 + i for i in range(n)},
        compiler_params=pltpu.CompilerParams(has_side_effects=_EFFECT),
    )(*[_in_hbm(a) for a in lands])
    return out[0], out[1], list(out[2:2 + n]), out[-1]


def _gather_ici_wait(name, lands, send_sems, recv_sems, after):
    n = len(lands)

    def body(*refs):
        ins, ss, rs = refs[:n], refs[n], refs[n + 1]
        x, y, c = _position()
        me = _linear(x, y, c)
        for i in range(n):
            for k, chip in enumerate(_other_chips(x, y)):
                _gather_ici_copy(ins[i], i, k, chip, c, me, ss, rs).wait_send()
                _gather_ici_copy(ins[i], i, k, chip, c, _linear(*chip, c), ss, rs).wait_recv()

    out = pl.pallas_call(
        body, name=name,
        out_shape=[pltpu.HBM(a.shape, a.dtype) for a in lands],
        in_specs=[_HBM] * n + [_SEM, _SEM, _ANY], out_specs=[_HBM] * n,
        input_output_aliases={i: i for i in range(n)},
        compiler_params=pltpu.CompilerParams(has_side_effects=_EFFECT),
    )(*lands, send_sems, recv_sems, after)
    return list(out)


def _gather_d2d(lands):
    n = len(lands)

    def body(*refs):
        ins, outs, send_sems, recv_sems = refs[:n], refs[n:2 * n], refs[2 * n], refs[2 * n + 1]
        x, y, c = _position()

        def copy(i, q, core):
            block = _linear(*_ALL_CHIPS[q], core)
            return pltpu.make_async_remote_copy(
                src_ref=ins[i].at[block], dst_ref=outs[i].at[block],
                send_sem=send_sems.at[i, q], recv_sem=recv_sems.at[i, q],
                device_id=(x, y, 1 - c), device_id_type=MESH_ID)

        sent = [copy(i, q, c) for i in range(n) for q in range(len(_ALL_CHIPS))]
        for cp in sent:
            cp.start()
        for i in range(n):
            for q in range(len(_ALL_CHIPS)):
                copy(i, q, 1 - c).wait_recv()
        for cp in sent:
            cp.wait_send()

    return pl.pallas_call(
        body, name="gather_d2d",
        out_shape=[_sds(a.shape, a.dtype) for a in lands],
        in_specs=[_ANY] * n, out_specs=[_ANY] * n,
        input_output_aliases={i: i for i in range(n)},
        scratch_shapes=[pltpu.SemaphoreType.DMA((n, 4)), pltpu.SemaphoreType.DMA((n, 4))],
    )(*lands)


def _partials_d2d(parts):
    n = len(parts)
    half = N_DEV // 2

    def body(*refs):
        ins, outs, send_sems, recv_sems = refs[:n], refs[n:2 * n], refs[2 * n], refs[2 * n + 1]
        x, y, c = _position()

        def copy(i, q):
            return pltpu.make_async_remote_copy(
                src_ref=ins[i].at[_linear(*_ALL_CHIPS[q], 1 - c)], dst_ref=outs[i].at[q],
                send_sem=send_sems.at[i, q], recv_sem=recv_sems.at[i, q],
                device_id=(x, y, 1 - c), device_id_type=MESH_ID)

        sent = [copy(i, q) for i in range(n) for q in range(half)]
        for cp in sent:
            cp.start()
        for cp in sent:
            cp.wait_recv()
        for cp in sent:
            cp.wait_send()

    return pl.pallas_call(
        body, name="partials_d2d",
        out_shape=[_sds((half,) + p.shape[1:], p.dtype) for p in parts],
        in_specs=[_ANY] * n, out_specs=[_ANY] * n,
        scratch_shapes=[pltpu.SemaphoreType.DMA((n, half)), pltpu.SemaphoreType.DMA((n, half))],
    )(*parts)


def _partials_peers(x, y, c, direct):
    chips = _other_chips(x, y)
    if not direct:
        return [((*ch, c), 2 * ch[0] + ch[1]) for ch in chips]
    peers = [(x, y, 1 - c)] + [(*ch, c) for ch in chips] + [(*ch, 1 - c) for ch in chips]
    return [(p, _linear(*p)) for p in peers]


def _partials_copies(srcs, lands, send_sems, recv_sems, direct):
    x, y, c = _position()
    peers = _partials_peers(x, y, c, direct)
    return [pltpu.make_async_remote_copy(
        src_ref=srcs[i].at[block], dst_ref=lands[i].at[k],
        send_sem=send_sems.at[len(peers) * i + k], recv_sem=recv_sems.at[len(peers) * i + k],
        device_id=peer, device_id_type=MESH_ID)
        for i in range(len(srcs)) for k, (peer, block) in enumerate(peers)]


def _partials_send_start(name, srcs, lands, direct, after=None):
    n = len(srcs)
    n_sem = n * (N_DEV - 1 if direct else len(_ALL_CHIPS) - 1)

    def body(*refs):
        _, send_sems, recv_sems = refs[:2 * n], refs[-2 * n - 3], refs[-2 * n - 2]
        for cp in _partials_copies(refs[:n], refs[n:2 * n], send_sems, recv_sems, direct):
            cp.start()
        refs[-1][...] = jnp.zeros_like(refs[-1])

    both = list(srcs) + list(lands)
    extra = [] if after is None else [after]
    out = pl.pallas_call(
        body, name=name,
        out_shape=(pltpu.SemaphoreType.DMA((n_sem,)), pltpu.SemaphoreType.DMA((n_sem,)),
                   *[pltpu.HBM(a.shape, a.dtype) for a in both], _token()),
        in_specs=[_HBM] * (2 * n) + [_ANY] * len(extra), out_specs=(_SEM, _SEM, *[_HBM] * (2 * n), _VMEM),
        input_output_aliases={i: 2 + i for i in range(2 * n)},
        compiler_params=pltpu.CompilerParams(has_side_effects=_EFFECT),
    )(*[_in_hbm(a) for a in both], *extra)
    return out[0], out[1], list(out[2:2 + n]), list(out[2 + n:2 + 2 * n]), out[-1]


def _partials_send_wait(name, srcs, lands, send_sems, recv_sems, direct, after):
    n = len(srcs)

    def body(*refs):
        for cp in _partials_copies(refs[:n], refs[n:2 * n], refs[2 * n], refs[2 * n + 1], direct):
            cp.wait_send()
            cp.wait_recv()

    both = list(srcs) + list(lands)
    out = pl.pallas_call(
        body, name=name,
        out_shape=[pltpu.HBM(a.shape, a.dtype) for a in both],
        in_specs=[_HBM] * (2 * n) + [_SEM, _SEM, _ANY], out_specs=[_HBM] * (2 * n),
        input_output_aliases={i: i for i in range(2 * n)},
        compiler_params=pltpu.CompilerParams(has_side_effects=_EFFECT),
    )(*both, send_sems, recv_sems, after)
    return list(out[:n]), list(out[n:])


def _pack(arrs):
    flat = jnp.concatenate([a.reshape(-1).astype(F32) for a in arrs])
    block = OPT_ROW_BLOCK if flat.shape[0] > OPT_ROW_BLOCK * LANES else SUBLANES
    pad = (-flat.shape[0]) % (block * LANES)
    if pad:
        flat = jnp.concatenate([flat, jnp.zeros((pad,), F32)])
    return flat.reshape(-1, LANES)


def _unpack(packed, shapes, lead=()):
    flat = packed.reshape(lead + (-1,))
    out, off = [], 0
    for s in shapes:
        size = int(np.prod(s))
        out.append(flat[..., off:off + size].reshape(lead + tuple(s)))
        off += size
    return out


def _unshard_last(g):
    nd = g.ndim
    perm = tuple(range(1, nd - 1)) + (0, nd - 1)
    t = jnp.transpose(g, perm)
    return t.reshape(t.shape[:-2] + (N_DEV * g.shape[-1],))


def kernel(x, c, w_ada, b_ada, ln_g, ln_b, ffn_gu, ffn_down, gmlp_w_in, gmlp_b_in, gmlp_ln_g, gmlp_ln_b, gmlp_w_s, gmlp_b_s, gmlp_w_out, w_ada_kv, b_ada_kv, w_kv, attn_w_q, attn_rel_bias, attn_w_o, loss_target, m_w_ada, m_b_ada, m_ln_g, m_ln_b, m_ffn_gu, m_ffn_down, m_gmlp_w_in, m_gmlp_b_in, m_gmlp_ln_g, m_gmlp_ln_b, m_gmlp_w_s, m_gmlp_b_s, m_gmlp_w_out, m_w_ada_kv, m_b_ada_kv, m_w_kv, m_attn_w_q, m_attn_rel_bias, m_attn_w_o, v_w_ada, v_b_ada, v_ln_g, v_ln_b, v_ffn_gu, v_ffn_down, v_gmlp_w_in, v_gmlp_b_in, v_gmlp_ln_g, v_gmlp_ln_b, v_gmlp_w_s, v_gmlp_b_s, v_gmlp_w_out, v_w_ada_kv, v_b_ada_kv, v_w_kv, v_attn_w_q, v_attn_rel_bias, v_attn_w_o):
    weights = dict(w_ada=w_ada, b_ada=b_ada, ln_g=ln_g, ln_b=ln_b, ffn_gu=ffn_gu, ffn_down=ffn_down,
                   gmlp_w_in=gmlp_w_in, gmlp_b_in=gmlp_b_in, gmlp_ln_g=gmlp_ln_g, gmlp_ln_b=gmlp_ln_b,
                   gmlp_w_s=gmlp_w_s, gmlp_b_s=gmlp_b_s, gmlp_w_out=gmlp_w_out, w_ada_kv=w_ada_kv,
                   b_ada_kv=b_ada_kv, w_kv=w_kv, attn_w_q=attn_w_q, attn_rel_bias=attn_rel_bias,
                   attn_w_o=attn_w_o)
    mom1 = dict(w_ada=m_w_ada, b_ada=m_b_ada, ln_g=m_ln_g, ln_b=m_ln_b, ffn_gu=m_ffn_gu, ffn_down=m_ffn_down,
                gmlp_w_in=m_gmlp_w_in, gmlp_b_in=m_gmlp_b_in, gmlp_ln_g=m_gmlp_ln_g, gmlp_ln_b=m_gmlp_ln_b,
                gmlp_w_s=m_gmlp_w_s, gmlp_b_s=m_gmlp_b_s, gmlp_w_out=m_gmlp_w_out, w_ada_kv=m_w_ada_kv,
                b_ada_kv=m_b_ada_kv, w_kv=m_w_kv, attn_w_q=m_attn_w_q, attn_rel_bias=m_attn_rel_bias,
                attn_w_o=m_attn_w_o)
    mom2 = dict(w_ada=v_w_ada, b_ada=v_b_ada, ln_g=v_ln_g, ln_b=v_ln_b, ffn_gu=v_ffn_gu, ffn_down=v_ffn_down,
                gmlp_w_in=v_gmlp_w_in, gmlp_b_in=v_gmlp_b_in, gmlp_ln_g=v_gmlp_ln_g, gmlp_ln_b=v_gmlp_ln_b,
                gmlp_w_s=v_gmlp_w_s, gmlp_b_s=v_gmlp_b_s, gmlp_w_out=v_gmlp_w_out, w_ada_kv=v_w_ada_kv,
                b_ada_kv=v_b_ada_kv, w_kv=v_w_kv, attn_w_q=v_attn_w_q, attn_rel_bias=v_attn_rel_bias,
                attn_w_o=v_attn_w_o)
    order = list(weights)

    x = x[0]
    target = loss_target[0]
    t, d = x.shape
    n_mod = w_ada.shape[-1] * N_DEV // d
    mod_w = w_ada.shape[-1]
    kv_w = w_ada_kv.shape[-1]
    n_b = DEPTH - N_A
    me = _linear(*_position())

    l2 = DEPTH * 2
    big = dict(
        ffn_gu=ffn_gu.reshape((l2,) + ffn_gu.shape[2:]),
        ffn_down=ffn_down.reshape((l2,) + ffn_down.shape[2:]),
        gmlp_w_in=gmlp_w_in, gmlp_w_out=gmlp_w_out, w_kv=w_kv[None],
        attn_w_q=attn_w_q, attn_w_o=attn_w_o)
    big_names = list(big)
    core = lax.axis_index("c").astype(jnp.int32).reshape(1)
    chip = (2 * lax.axis_index("x") + lax.axis_index("y")).astype(jnp.int32).reshape(1)

    fwd_groups = [
        {"ffn_gu": (0, 1), "ffn_down": (0, 1)},
        {"gmlp_w_in": (0, 1), "gmlp_w_out": (0, 1), "ffn_gu": (1, 1), "ffn_down": (1, 1)},
        {"ffn_gu": (2, 2), "ffn_down": (2, 2), "gmlp_w_in": (1, 1), "gmlp_w_out": (1, 1), "w_kv": (0, 1)},
        {"ffn_gu": (4, 2), "ffn_down": (4, 2), "attn_w_q": (0, 1), "attn_w_o": (0, 1)},
        {"ffn_gu": (6, 2), "ffn_down": (6, 2), "attn_w_q": (1, 1), "attn_w_o": (1, 1)},
    ]
    bwd_groups = []
    for l in range(DEPTH):
        g = {"ffn_gu": (2 * l, 2), "ffn_down": (2 * l, 2)}
        if l < N_A:
            g.update({"gmlp_w_in": (l, 1), "gmlp_w_out": (l, 1)})
        else:
            g.update({"attn_w_q": (l - N_A, 1), "attn_w_o": (l - N_A, 1)})
        if l == N_A - 1:
            g["w_kv"] = (0, 1)
        bwd_groups.append(g)

    def slot_of(groups, name, slot):
        for gi, g in enumerate(groups):
            if name in g and g[name][0] <= slot < g[name][0] + g[name][1]:
                return gi, slot - g[name][0]
        raise KeyError((name, slot))

    flights = []
    for gi, g in enumerate(fwd_groups):
        lands = []
        for name, (s0, cnt) in g.items():
            shard = big[name][s0:s0 + cnt].astype(BF16)
            land = lax.empty((N_DEV,) + shard.shape, BF16)
            lands.append(lax.dynamic_update_slice(land, shard[None], (me,) + (0,) * shard.ndim))
        flights.append(_gather_ici_start(f"gather_ici_start_{gi}", lands))
    start_token = sum(f[3][0, 0] for f in flights)
    gathered = [None] * len(fwd_groups)

    def land_group(gi, after):
        send_sems, recv_sems, lands, _ = flights[gi]
        lands = _gather_ici_wait(f"gather_ici_wait_{gi}", lands, send_sems, recv_sems, after)
        gathered[gi] = dict(zip(fwd_groups[gi], _gather_d2d(lands)))

    def weight(name, slot):
        gi, local = slot_of(fwd_groups, name, slot)
        return gathered[gi][name], local

    swapped = ("ffn_gu",)

    def grad_shape(name):
        s = big[name].shape[1:]
        return s[:-2] + (s[-1], s[-2]) if name in swapped else s

    partial = [{name: lax.empty((N_DEV, cnt) + grad_shape(name), BF16) for name, (_, cnt) in g.items()}
               for g in bwd_groups]

    c_all = _all_gather_small(_pack([c]))
    c_all = _unpack(c_all, [(d,)], lead=(N_DEV,))[0]
    c4 = _as4(c_all)
    mod_part = _matmul("ada_fwd", c4, w_ada[:, None], (DEPTH, 1, N_DEV, mod_w), F32, a_silu=True)
    kv_part = _matmul("ada_kv_fwd", c4, _as4(w_ada_kv), (1, 1, N_DEV, kv_w), F32, a_silu=True)
    small_shapes = [mod_part.shape, kv_part.shape, ln_g.shape, ln_b.shape, gmlp_b_in.shape,
                    gmlp_ln_g.shape, gmlp_ln_b.shape, attn_rel_bias.shape]
    small = _all_gather_small(_pack([mod_part, kv_part, ln_g, ln_b, gmlp_b_in, gmlp_ln_g, gmlp_ln_b,
                                     attn_rel_bias]))
    (mod_g, kvm_g, ln_g_g, ln_b_g, b_in_g, gln_g_g, gln_b_g, rel_g) = _unpack(small, small_shapes, lead=(N_DEV,))
    mod_mine = lax.dynamic_index_in_dim(mod_g[:, :, 0], me, axis=2, keepdims=False)
    mod = _unshard_last(mod_mine) + b_ada
    mod = mod.reshape(DEPTH, n_mod, 1, d)
    kvm_mine = lax.dynamic_index_in_dim(kvm_g[:, 0, 0], me, axis=1, keepdims=False)
    mkv = (_unshard_last(kvm_mine) + b_ada_kv).reshape(2, 1, d)
    ln_g_f = _unshard_last(ln_g_g)
    ln_b_f = _unshard_last(ln_b_g)
    half = N_DEV // 2
    b_in_f = jnp.transpose(b_in_g, (1, 0, 2))[:, :, None, :]
    gln_g_f = _unshard_last(gln_g_g).reshape(N_A, half, 1, -1)
    gln_b_f = _unshard_last(gln_b_g).reshape(N_A, half, 1, -1)
    rel_f = _unshard_last(rel_g)

    def shard_act(a):
        return a.reshape(a.shape[0], a.shape[2], a.shape[3])

    def grad_into(name, slot, mm):
        gi, local = slot_of(bwd_groups, name, slot)
        partial[gi][name] = mm(partial[gi][name], local)

    def ffn_fwd(h, lw):
        w_gu, l_gu = weight("ffn_gu", lw)
        w_dn, l_dn = weight("ffn_down", lw)
        gu, a = _ffn_up_fwd(h, w_gu, l_gu)
        y = _matmul("ffn_down_fwd", a[:, None], w_dn, (1, 1, t, d), F32, lb=l_dn, b_merge=2, reduce=True)
        return y[0, 0], (gu, a)

    def ffn_bwd(dy, h, saved, lw):
        gu, a = saved
        w_gu, l_gu = weight("ffn_gu", lw)
        w_dn, l_dn = weight("ffn_down", lw)
        dgu = _ffn_down_bwd_a(dy, w_dn, l_dn, gu).reshape((N_DEV,) + gu.shape[2:])
        grad_into("ffn_down", lw, lambda buf, lo: _matmul(
            "ffn_down_bwd_w", a[:, None], _as4(dy), buf.shape, BF16, ta=True, lo=lo, out_merge=2, out_buf=buf))
        dh = _matmul("ffn_gu_bwd_a", dgu[:, None], w_gu, (1, 1, t, d), F32, lb=l_gu, tb=True, reduce=True)
        grad_into("ffn_gu", lw, lambda buf, lo: _matmul(
            "ffn_gu_bwd_w", dgu[:, None], _as4(h), buf.shape, BF16, ta=True, lo=lo, out_buf=buf))
        return dh[0, 0], {}

    def gmlp_params(l):
        return (b_in_f[l], gln_g_f[l], gln_b_f[l], gmlp_w_s[l], gmlp_b_s[l][:, :, None])

    def gmlp_fwd(h, l):
        w_in, l_in = weight("gmlp_w_in", l)
        w_out, l_out = weight("gmlp_w_out", l)
        n = w_in.shape[-1]
        zpre = _matmul("gmlp_in_fwd", _as4(h), w_in, (N_DEV, 1, t, n), F32, lb=l_in)
        gated = _gmlp_mid_fwd(shard_act(zpre), *gmlp_params(l))
        y = _matmul("gmlp_out_fwd", gated[:, None], w_out, (1, 1, t, d), F32, lb=l_out, b_merge=2, reduce=True)
        return y[0, 0], (zpre, gated)

    def gmlp_bwd(dy, h, saved, l):
        zpre, gated = saved
        w_in, l_in = weight("gmlp_w_in", l)
        w_out, l_out = weight("gmlp_w_out", l)
        n = w_in.shape[-1]
        dgated = _matmul("gmlp_out_bwd_a", _as4(dy), w_out, (half, 1, t, n), F32, lb=l_out, b_merge=2, tb=True)
        grad_into("gmlp_w_out", l, lambda buf, lo: _matmul(
            "gmlp_out_bwd_w", gated[:, None], _as4(dy), buf.shape, BF16, ta=True, lo=lo, out_merge=2, out_buf=buf))
        dz, dws, dbs, dlng, dlnb, dbin = _gmlp_mid_bwd(shard_act(zpre), shard_act(dgated), *gmlp_params(l))
        dh = _matmul("gmlp_in_bwd_a", dz[:, None], w_in, (1, 1, t, d), F32, lb=l_in, tb=True, reduce=True)
        grad_into("gmlp_w_in", l, lambda buf, lo: _matmul(
            "gmlp_in_bwd_w", _as4(h), dz[:, None], buf.shape, BF16, ta=True, lo=lo, out_buf=buf))
        small_grads = dict(gmlp_w_s=dws, gmlp_b_s=dbs[:, :, 0], gmlp_ln_g=dlng.reshape(-1),
                           gmlp_ln_b=dlnb.reshape(-1), gmlp_b_in=dbin.reshape(-1))
        return dh[0, 0], small_grads

    def attn_fwd(h, j, kvp):
        rel_vec = _rel_vector(rel_f[j])
        w_q, l_q = weight("attn_w_q", j)
        w_o, l_o = weight("attn_w_o", j)
        q = _matmul("attn_q_fwd", _as4(h), w_q, (1, 1, t, d), BF16, lb=l_q, b_merge=N_DEV, reduce=True)[0, 0]
        o = _attn_fwd(q, kvp, rel_vec)
        y = _matmul("attn_o_fwd", _as4(o), w_o, (1, 1, t, d), F32, lb=l_o, b_merge=N_DEV, reduce=True)
        return y[0, 0], (q, o, rel_vec)

    def attn_bwd(dy, h, saved, j, kvp, dkv_acc):
        q, o, rel_vec = saved
        w_q, l_q = weight("attn_w_q", j)
        w_o, l_o = weight("attn_w_o", j)
        do = _matmul("attn_o_bwd_a", _as4(dy), w_o, (1, 1, t, d), BF16, lb=l_o, b_merge=N_DEV, tb=True)[0, 0]
        grad_into("attn_w_o", j, lambda buf, lo: _matmul(
            "attn_o_bwd_w", _as4(o), _as4(dy), buf.shape, BF16, ta=True, lo=lo, out_merge=N_DEV, out_buf=buf))
        dq, dk, dv, dsc = _attn_bwd(q, do, kvp, rel_vec, *dkv_acc)
        drel = _rel_bias_grad(_skew_diagonals(dsc))
        dh = _matmul("attn_q_bwd_a", _as4(dq), w_q, (1, 1, t, d), F32, lb=l_q, b_merge=N_DEV, tb=True)
        grad_into("attn_w_q", j, lambda buf, lo: _matmul(
            "attn_q_bwd_w", _as4(h), _as4(dq), buf.shape, BF16, ta=True, lo=lo, out_merge=N_DEV, out_buf=buf))
        return dh[0, 0], dict(attn_rel_bias=drel, dkv=(dk, dv))

    tape = []
    kvp = None
    kv_tape = None
    first_use = {(0, 0): 0, (0, 1): 1, (1, 0): 2, (2, 0): 3, (3, 0): 4}
    h = _modulate(x, mod[0, 1], mod[0, 0] + start_token)
    for l in range(DEPTH):
        for i in range(3):
            if (l, i) in first_use:
                land_group(first_use[l, i], x)
            scl, gate = mod[l, 3 * i + 1], mod[l, 3 * i + 2]
            wgt = 1.0 if i == 1 else 0.5
            gw = wgt * (1.0 + gate)
            if i != 1:
                y, saved = ffn_fwd(h, 2 * l + i // 2)
            elif l < N_A:
                y, saved = gmlp_fwd(h, l)
            else:
                y, saved = attn_fwd(h, l - N_A, kvp)
            nl, ni = (l, i + 1) if i < 2 else (l + 1, 0)
            readers = [(mod[nl, 3 * ni + 1], mod[nl, 3 * ni])] if nl < DEPTH else []
            shared_kv = (l, i) == (N_A - 1, 2)
            if shared_kv:
                readers.append((mkv[1], mkv[0]))
            outs = _ln_res_fwd(x, y, gw, ln_g_f[l, i][None], ln_b_f[l, i][None], readers)
            tape.append((x, h, y, gw, scl, saved))
            x = outs[0]
            h = outs[1] if nl < DEPTH else None
            if shared_kv:
                hkv = outs[-1]
                w_kvg, l_kv = weight("w_kv", 0)
                n = w_kvg.shape[-1]
                kv = _matmul("kv_fwd", _as4(hkv), w_kvg, (N_DEV, 1, t, n), BF16, lb=l_kv)
                kvp = jnp.pad(shard_act(kv), ((0, 0), (LEFT_PAD, 0), (0, 0)))
                kv_tape = hkv

    loss_part, dx = _loss_head(x, target)
    loss = lax.psum(loss_part[0, 0], MESH_AXES)

    d_mod = [[None] * n_mod for _ in range(DEPTH)]
    d_ln_g = [[None] * 3 for _ in range(DEPTH)]
    d_ln_b = [[None] * 3 for _ in range(DEPTH)]
    small_grads = {k: [None] * N_A for k in ("gmlp_w_s", "gmlp_b_s", "gmlp_ln_g", "gmlp_ln_b", "gmlp_b_in")}
    d_rel = [None] * n_b
    dkv_acc = ()
    d_mkv = None
    reductions = [None] * DEPTH
    sent_token = None
    readers = []
    for l in reversed(range(DEPTH)):
        if l == N_A - 1:
            hkv = kv_tape
            w_kvg, l_kv = weight("w_kv", 0)
            dkv = jnp.concatenate(dkv_acc)[:, LEFT_PAD:, :].astype(BF16)[:, None]
            dhkv = _matmul("kv_bwd_a", dkv, w_kvg, (1, 1, t, d), F32, lb=l_kv, tb=True, reduce=True)[0, 0]
            grad_into("w_kv", 0, lambda buf, lo: _matmul(
                "kv_bwd_w", _as4(hkv), dkv, buf.shape, BF16, ta=True, lo=lo, out_buf=buf))
            readers.append((dhkv, mkv[1], None))
        for i in reversed(range(3)):
            x_in, h, y, gw, scl, saved = tape[3 * l + i]
            wgt = 1.0 if i == 1 else 0.5
            if sent_token is not None:
                gw = gw + sent_token
                sent_token = None
            res = _ln_res_bwd(x_in, y, gw, ln_g_f[l, i][None], ln_b_f[l, i][None], dx,
                              [(r[0], r[1]) for r in readers])
            dx_res, dy, dgw, dg, db = res[:5]
            for k, (_, _, slot) in enumerate(readers):
                dscl_k, dshift_k = res[5 + 2 * k][0], res[6 + 2 * k][0]
                if slot is None:
                    d_mkv = jnp.concatenate([dshift_k, dscl_k])
                else:
                    d_mod[slot[0]][slot[1]], d_mod[slot[0]][slot[1] + 1] = dshift_k, dscl_k
            d_ln_g[l][i], d_ln_b[l][i] = dg[0], db[0]
            if i != 1:
                dh, extra = ffn_bwd(dy, h, saved, 2 * l + i // 2)
            elif l < N_A:
                dh, extra = gmlp_bwd(dy, h, saved, l)
                for k, g in extra.items():
                    small_grads[k][l] = g
            else:
                dh, extra = attn_bwd(dy, h, saved, l - N_A, kvp, dkv_acc)
                d_rel[l - N_A] = extra["attn_rel_bias"]
                dkv_acc = extra["dkv"]
            d_mod[l][3 * i + 2] = wgt * dgw[0]
            dx = dx_res
            readers = [(dh, scl, (l, 3 * i))]
        if l > 0:
            srcs = [partial[l][k] for k in bwd_groups[l]]
            lands = [lax.empty((N_DEV - 1,) + s.shape[1:], BF16) for s in srcs]
            reductions[l] = _partials_send_start(f"partials_send_start_{l}", srcs, lands, True)
            sent_token = reductions[l][4][0, 0]
    (dh, scl, _), = readers
    dx, dscl, dshift = _mod_bwd(dx, dh, tape[0][0], scl)
    d_mod[0][0], d_mod[0][1] = dshift[0], dscl[0]
    grad_x = dx[None]

    d_mod_arr = jnp.stack([jnp.concatenate(r) for r in d_mod])
    small_part = dict(
        b_ada=d_mod_arr, b_ada_kv=d_mkv,
        ln_g=jnp.stack([jnp.stack(r) for r in d_ln_g]), ln_b=jnp.stack([jnp.stack(r) for r in d_ln_b]),
        gmlp_b_in=jnp.stack(small_grads["gmlp_b_in"]), gmlp_ln_g=jnp.stack(small_grads["gmlp_ln_g"]),
        gmlp_ln_b=jnp.stack(small_grads["gmlp_ln_b"]), gmlp_w_s=jnp.stack(small_grads["gmlp_w_s"]),
        gmlp_b_s=jnp.stack(small_grads["gmlp_b_s"]), attn_rel_bias=jnp.stack(d_rel))
    small_names = list(small_part)
    sp_shapes = [small_part[k].shape for k in small_names]
    sp_all = _all_gather_small(_pack([small_part[k] for k in small_names]),
                               after=[partial[0][k] for k in bwd_groups[0]])

    from_sibling = _partials_d2d([partial[0][k] for k in bwd_groups[0]])
    sums = [_chip_sum(partial[0][k], r1, core) for k, r1 in zip(bwd_groups[0], from_sibling)]
    lands = [lax.empty((len(_ALL_CHIPS) - 1,) + s.shape[1:], BF16) for s in sums]
    reductions[0] = _partials_send_start("partials_send_start_0", sums, lands, False, after=sp_all)
    sent_token = reductions[0][4][0, 0]
    c4 = c4 + sent_token

    sp_sum = _sum_parts(sp_all)
    full_grads = dict(zip(small_names, _unpack(sp_sum, sp_shapes)))
    per_dev = dict(zip(small_names, _unpack(sp_all, sp_shapes, lead=(N_DEV,))))

    def my_cols(a, width):
        return lax.dynamic_slice_in_dim(a, me * width, width, axis=a.ndim - 1)

    grads = {}
    grads["b_ada"] = full_grads["b_ada"]
    grads["b_ada_kv"] = full_grads["b_ada_kv"]
    grads["gmlp_w_s"] = full_grads["gmlp_w_s"]
    grads["gmlp_b_s"] = full_grads["gmlp_b_s"]
    for k in ("ln_g", "ln_b", "gmlp_b_in", "gmlp_ln_g", "gmlp_ln_b", "attn_rel_bias"):
        grads[k] = my_cols(full_grads[k], weights[k].shape[-1])

    dmod_cols = jnp.transpose(my_cols(per_dev["b_ada"], mod_w), (1, 0, 2))[:, None]
    grads["w_ada"] = _matmul("ada_bwd_w", c4, dmod_cols, (DEPTH, 1, d, mod_w), F32, ta=True,
                             a_silu=True)[:, 0]
    dkv_cols = my_cols(per_dev["b_ada_kv"], kv_w)[None, None]
    grads["w_ada_kv"] = _matmul("ada_kv_bwd_w", c4, dkv_cols, (1, 1, d, kv_w), F32, ta=True,
                                a_silu=True)[0, 0]

    delta, new_m, new_v = {}, {}, {}
    first = jnp.zeros((1,), jnp.int32)

    def flat2(a, cols):
        return a.reshape(-1, cols)

    done = None
    for k in ("w_ada", "w_ada_kv"):
        w = weights[k]
        cols = w.shape[-1]
        res = _adamw(grads[k].reshape(1, -1, cols), first, None, flat2(w, cols), flat2(mom1[k], cols),
                     flat2(mom2[k], cols), after=done)
        grads[k], delta[k], new_m[k], new_v[k] = (a.reshape(w.shape) for a in res)
        done = res[0][:SUBLANES, :LANES]

    tiny = [k for k in order if k not in delta and k not in big_names]
    tiny_shapes = [weights[k].shape for k in tiny]
    tiny_out = _adamw((_pack([grads[k] for k in tiny]) + sent_token)[None], first, None,
                      _pack([weights[k] for k in tiny]), _pack([mom1[k] for k in tiny]),
                      _pack([mom2[k] for k in tiny]), after=done)
    for dst, arr in zip((grads, delta, new_m, new_v), tiny_out):
        for k, val in zip(tiny, _unpack(arr, tiny_shapes)):
            dst[k] = val

    def opt_view(k, a):
        a = jnp.swapaxes(a, -1, -2) if k in swapped else a
        return a.reshape(-1, a.shape[-1])

    def opt_unview(k, a):
        s = weights[k].shape
        return jnp.swapaxes(a.reshape(s[:-2] + (s[-1], s[-2])), -1, -2) if k in swapped else a.reshape(s)

    bufs = {k: [lax.empty(opt_view(k, weights[k]).shape, F32) for _ in range(4)] for k in big_names}
    done = tiny_out[0]
    me_idx = me.astype(jnp.int32).reshape(1)
    for l in reversed(range(DEPTH)):
        send_sems, recv_sems, srcs, lands, _ = reductions[l]
        srcs, lands = _partials_send_wait(f"partials_send_wait_{l}", srcs, lands, send_sems, recv_sems, l > 0, done)
        for k, own, got in zip(bwd_groups[l], srcs, lands):
            cols = own.shape[-1]
            slot_rows = int(np.prod(own.shape[2:-1]))
            bufs[k] = _adamw(own.reshape(own.shape[0], -1, cols), me_idx if l > 0 else chip,
                             got.reshape(got.shape[0], -1, cols),
                             opt_view(k, weights[k]), opt_view(k, mom1[k]), opt_view(k, mom2[k]),
                             row0=bwd_groups[l][k][0] * slot_rows, bufs=bufs[k], after=done)
            done = bufs[k][0][:SUBLANES, :LANES]
    for k in big_names:
        grads[k], delta[k], new_m[k], new_v[k] = (opt_unview(k, b) for b in bufs[k])

    return (loss, grad_x, *[grads[k] for k in order], *[delta[k] for k in order],
            *[new_m[k] for k in order], *[new_v[k] for k in order])
```

```python
import functools

import numpy as np
import jax
import jax.numpy as jnp
from jax import lax
from jax.experimental import pallas as pl
from jax.experimental.pallas import tpu as pltpu

F32 = jnp.float32
BF16 = jnp.bfloat16
MESH_AXES = ("x", "y", "c")
N_DEV = 8
MESH_ID = pl.DeviceIdType.MESH

DEPTH = 4
N_A = 2
CHUNK = 64
N_HEADS = 16
LEFT_CHUNKS = 8
BAND = (LEFT_CHUNKS + 1) * CHUNK
LEFT_PAD = LEFT_CHUNKS * CHUNK
MAX_REL = 4 * CHUNK
N_REL = (CHUNK - 1) + MAX_REL + 1
GMLP_WINDOW = 128
GMLP_GROUPS = 8
ALPHA = (2.0 * DEPTH) ** 0.25
LN_EPS = 1e-5
ADAM_LR = 0.001
ADAM_B1 = 0.9
ADAM_B2 = 0.999
ADAM_EPS = 1e-08
ADAM_WD = 0.01
ADAM_STEP = 10

V7X_VMEM_BYTES = 64 * 1024 * 1024
VMEM_LIMIT = V7X_VMEM_BYTES - 8 * 1024 * 1024
LANES = 128
SUBLANES = 8
MM_BLOCK = 2048
BIG_ROW_BLOCK = 1024
ROW_BLOCK = 512
OPT_ROW_BLOCK = 256

_ANY = pl.BlockSpec(memory_space=pl.ANY)
_VMEM = pl.BlockSpec(memory_space=pltpu.VMEM)


def _params(sem=None):
    return pltpu.CompilerParams(dimension_semantics=sem, vmem_limit_bytes=VMEM_LIMIT)


def _row_block(rows, target):
    for d in range(min(rows, target), 0, -1):
        if rows % d == 0 and (d % SUBLANES == 0 or d == rows):
            return d
    return rows


def _matmul(name, a, b, out_shape4, out_dtype, *, la=0, lb=0, lo=0, ta=False, tb=False,
            reduce=False, b_merge=1, out_merge=1, out_buf=None, a_silu=False):
    ja_n, _, a_r, a_c = a.shape
    jb_n, _, b_r, b_c = b.shape
    jo_n, _, o_r, o_c = out_shape4
    m_tot = a_c if ta else a_r
    k_a = a_r if ta else a_c
    b_rows = b_merge * b_r
    k_c = b_c if tb else b_rows
    n = b_rows if tb else b_c
    n_chunks = (jb_n // b_merge) if reduce else 1
    natural_k = reduce and ja_n == 1
    assert n == o_c, (name, n, o_c)
    assert k_a ==(k_c * n_chunks if natural_k else k_c), (name, k_a, k_c, n_chunks)
    bk = k_c if (k_c <= MM_BLOCK or (b_merge > 1 and not tb)) else MM_BLOCK
    assert k_c % bk == 0
    nkk = k_c // bk
    kg = 2 if (reduce and ja_n > 1 and nkk == 1 and not ta and n_chunks % 2 == 0) else 1
    nk = n_chunks * nkk // kg
    m_out = out_merge * o_r
    assert m_tot == m_out, (name, m_tot, m_out)
    bm = m_tot if (m_tot <= MM_BLOCK or out_merge > 1) else MM_BLOCK
    assert m_tot % bm == 0
    jo_blocks = jo_n // out_merge

    def a_index(j, m, k):
        kj, kk = k // nkk, k % nkk
        ja = 0 if ja_n == 1 else (kj if reduce else j)
        ke = kk + kj * nkk if natural_k else kk
        return (ja, la, ke, m) if ta else (ja, la, m, ke)

    def b_index(j, m, k):
        kj, kk = k // nkk, k % nkk
        jb = 0 if jb_n == b_merge else (kj if reduce else j)
        return (jb, lb, 0, kk) if tb else (jb, lb, kk, 0)

    def o_index(j, m, k):
        return (j, lo, 0, 0) if out_merge > 1 else (j, lo, m, 0)

    a_block = (None, None, bk, bm) if ta else (None if kg == 1 else kg, None, bm, bk)
    if b_merge > 1:
        b_block = (kg * b_merge, None, b_r, bk if tb else n)
    else:
        b_block = (None if kg == 1 else kg, None) + ((n, bk) if tb else (bk, n))
    o_block = (out_merge, None, o_r, n) if out_merge > 1 else (None, None, bm, n)
    dims = (((0 if ta else 1,), (1 if tb else 0,)), ((), ()))

    in_place = nk > 1 and out_dtype == F32 and out_merge == 1
    use_acc = nk > 1 and not in_place

    def body(a_ref, b_ref, *rest):
        o_ref = rest[-2] if use_acc else rest[-1]
        k = pl.program_id(2)
        av = a_ref[...]
        if a_silu:
            af = av.astype(F32)
            av = af * jax.nn.sigmoid(af)
        bv = b_ref[...]
        if kg > 1:
            bv = bv.reshape(kg, -1, bv.shape[-1])
            prod = sum(lax.dot_general(av[g].astype(BF16), bv[g].astype(BF16), dims, preferred_element_type=F32)
                       for g in range(kg))
        else:
            if b_merge > 1:
                bv = bv.reshape(b_rows, bv.shape[-1])
            prod = lax.dot_general(av.astype(BF16), bv.astype(BF16), dims, preferred_element_type=F32)

        def emit(val):
            val = val.astype(out_dtype)
            o_ref[...] = val.reshape(out_merge, o_r, n) if out_merge > 1 else val

        if nk == 1:
            emit(prod)
            return
        acc_ref = o_ref if in_place else rest[-1]

        @pl.when(k == 0)
        def _():
            acc_ref[...] = prod

        @pl.when(k > 0)
        def _():
            acc_ref[...] += prod

        if use_acc:
            @pl.when(k == nk - 1)
            def _():
                emit(acc_ref[...])

    in_specs = [pl.BlockSpec(a_block, a_index), pl.BlockSpec(b_block, b_index)]
    operands = [a, b]
    aliases = {}
    if out_buf is not None:
        assert out_buf.shape == tuple(out_shape4) and out_buf.dtype == out_dtype
        in_specs.append(_ANY)
        operands.append(out_buf)
        aliases = {2: 0}
    return pl.pallas_call(
        body, name=name,
        grid=(jo_blocks, m_tot // bm, nk),
        in_specs=in_specs,
        out_specs=pl.BlockSpec(o_block, o_index),
        out_shape=jax.ShapeDtypeStruct(tuple(out_shape4), out_dtype),
        scratch_shapes=[pltpu.VMEM((bm, n), F32)] if use_acc else [],
        input_output_aliases=aliases,
        compiler_params=_params(("parallel", "parallel", "arbitrary")),
    )(*operands)


def _as4(a):
    return a.reshape((1,) * (4 - a.ndim) + a.shape)


def _row_call(name, body, ins, outs, t, *, acc_outs=()):
    bt = _row_block(t, ROW_BLOCK)

    def spec(arr, tiled):
        if tiled:
            return pl.BlockSpec((bt,) + tuple(arr.shape[1:]), lambda i: (i,) + (0,) * (arr.ndim - 1))
        return pl.BlockSpec(tuple(arr.shape), lambda i: (0,) * arr.ndim)

    return pl.pallas_call(
        body, name=name, grid=(t // bt,),
        in_specs=[spec(a, tl) for a, tl in ins],
        out_specs=[spec(o, tl) for o, tl in outs],
        out_shape=[jax.ShapeDtypeStruct(o.shape, o.dtype) for o, _ in outs],
        compiler_params=_params(("arbitrary",) if acc_outs else ("parallel",)),
    )(*[a for a, _ in ins])


def _sds(shape, dtype):
    return jax.ShapeDtypeStruct(tuple(shape), dtype)


def _modulate(x, scl, shift):
    t, d = x.shape

    def body(x_ref, s_ref, b_ref, h_ref):
        h_ref[...] = (x_ref[...] * (1.0 + s_ref[...]) + b_ref[...]).astype(BF16)

    return _row_call("modulate", body, [(x, True), (scl, False), (shift, False)],
                     [(_sds((t, d), BF16), True)], t)[0]


def _ln_stats(r):
    mu = jnp.mean(r, axis=-1, keepdims=True)
    rc = r - mu
    var = jnp.mean(rc * rc, axis=-1, keepdims=True)
    rstd = lax.rsqrt(var + LN_EPS)
    return rc * rstd, rstd


def _ln_res_fwd(x, y, gw, g, b, mods=()):
    t, d = x.shape
    n_mod = len(mods)

    def body(x_ref, y_ref, gw_ref, g_ref, b_ref, *rest):
        mod_refs, o_ref, h_refs = rest[:2 * n_mod], rest[2 * n_mod], rest[2 * n_mod + 1:]
        r = ALPHA * x_ref[...] + gw_ref[...] * y_ref[...]
        xhat, _ = _ln_stats(r)
        xn = xhat * g_ref[...] + b_ref[...]
        o_ref[...] = xn
        for k in range(n_mod):
            h_refs[k][...] = (xn * (1.0 + mod_refs[2 * k][...]) + mod_refs[2 * k + 1][...]).astype(BF16)

    vecs = [(v, False) for pair in mods for v in pair]
    return _row_call("ln_res_fwd", body,
                     [(x, True), (y, True), (gw, False), (g, False), (b, False)] + vecs,
                     [(_sds((t, d), F32), True)] + [(_sds((t, d), BF16), True)] * n_mod, t)


def _ln_res_bwd(x, y, gw, g, b, dx_base, pairs=()):
    t, d = x.shape
    n_pair = len(pairs)

    def body(x_ref, y_ref, gw_ref, g_ref, b_ref, dxb_ref, *rest):
        pair_refs, outs = rest[:2 * n_pair], rest[2 * n_pair:]
        dx_ref, dy_ref = outs[0], outs[1]
        sums = outs[2:]

        @pl.when(pl.program_id(0) == 0)
        def _():
            for r in sums:
                r[...] = jnp.zeros_like(r)

        yv = y_ref[...]
        gwv = gw_ref[...]
        gv = g_ref[...]
        xhat, rstd = _ln_stats(ALPHA * x_ref[...] + gwv * yv)
        dxn = dxb_ref[...]
        if n_pair:
            xn = xhat * gv + b_ref[...]
            for k in range(n_pair):
                dh = pair_refs[2 * k][...]
                dxn = dxn + dh * (1.0 + pair_refs[2 * k + 1][...])
                sums[3 + 2 * k][...] += jnp.sum(dh * xn, axis=0, keepdims=True)
                sums[4 + 2 * k][...] += jnp.sum(dh, axis=0, keepdims=True)
        dxh = dxn * gv
        m1 = jnp.mean(dxh, axis=-1, keepdims=True)
        m2 = jnp.mean(dxh * xhat, axis=-1, keepdims=True)
        dr = rstd * (dxh - m1 - xhat * m2)
        dx_ref[...] = ALPHA * dr
        dy_ref[...] = (gwv * dr).astype(BF16)
        sums[0][...] += jnp.sum(dr * yv, axis=0, keepdims=True)
        sums[1][...] += jnp.sum(dxn * xhat, axis=0, keepdims=True)
        sums[2][...] += jnp.sum(dxn, axis=0, keepdims=True)

    vec = _sds((1, d), F32)
    n_sum = 3 + 2 * n_pair
    ins = [(x, True), (y, True), (gw, False), (g, False), (b, False), (dx_base, True)]
    for dh, scl in pairs:
        ins += [(dh, True), (scl, False)]
    return _row_call("ln_res_bwd", body, ins,
                     [(_sds((t, d), F32), True), (_sds((t, d), BF16), True)] + [(vec, False)] * n_sum, t,
                     acc_outs=tuple(range(2, 2 + n_sum)))


def _mod_bwd(dx_res, dh, x, scl):
    t, d = x.shape

    def body(dxr_ref, dh_ref, x_ref, s_ref, dx_ref, ds_ref, db_ref):
        @pl.when(pl.program_id(0) == 0)
        def _():
            ds_ref[...] = jnp.zeros_like(ds_ref)
            db_ref[...] = jnp.zeros_like(db_ref)

        dh = dh_ref[...]
        dx_ref[...] = dxr_ref[...] + dh * (1.0 + s_ref[...])
        ds_ref[...] += jnp.sum(dh * x_ref[...], axis=0, keepdims=True)
        db_ref[...] += jnp.sum(dh, axis=0, keepdims=True)

    vec = _sds((1, d), F32)
    return _row_call("mod_bwd", body, [(dx_res, True), (dh, True), (x, True), (scl, False)],
                     [(_sds((t, d), F32), True), (vec, False), (vec, False)], t, acc_outs=(1, 2))


def _loss_head(y, target):
    t, d = y.shape

    def body(y_ref, t_ref, l_ref, dy_ref):
        @pl.when(pl.program_id(0) == 0)
        def _():
            l_ref[...] = jnp.zeros_like(l_ref)

        err = y_ref[...] - t_ref[...]
        dy_ref[...] = err * (1.0 / d)
        part = 0.5 * jnp.sum(jnp.mean(err * err, axis=-1, keepdims=True), axis=0, keepdims=True)
        l_ref[...] += jnp.broadcast_to(part, l_ref.shape)

    return _row_call("loss_head", body, [(y, True), (target, True)],
                     [(_sds((SUBLANES, LANES), F32), False), (_sds((t, d), F32), True)], t,
                     acc_outs=(0,))


def _ffn_up_fwd(h, w_gu, lb):
    t, d = h.shape
    n = w_gu.shape[-1]
    half = N_DEV // 2
    bt = _row_block(t, BIG_ROW_BLOCK)

    def body(h_ref, wg_ref, wu_ref, gu_ref, a_ref):
        hv = h_ref[...]
        g = jnp.dot(hv, wg_ref[...], preferred_element_type=F32)
        u = jnp.dot(hv, wu_ref[...], preferred_element_type=F32)
        gu_ref[0] = g
        gu_ref[1] = u
        a_ref[...] = (g * jax.nn.sigmoid(g) * u).astype(BF16)

    return pl.pallas_call(
        body, name="ffn_up_fwd", grid=(half, t // bt),
        in_specs=[pl.BlockSpec((bt, d), lambda j, i: (i, 0)),
                  pl.BlockSpec((None, None, d, n), lambda j, i: (j, lb, 0, 0)),
                  pl.BlockSpec((None, None, d, n), lambda j, i: (half + j, lb, 0, 0))],
        out_specs=[pl.BlockSpec((2, None, bt, n), lambda j, i: (0, j, i, 0)),
                   pl.BlockSpec((None, bt, n), lambda j, i: (j, i, 0))],
        out_shape=[_sds((2, half, t, n), F32), _sds((half, t, n), BF16)],
        compiler_params=_params(("parallel", "parallel")),
    )(h, w_gu, w_gu)


def _ffn_down_bwd_a(dy, w_down, lb, gu):
    t, d = dy.shape
    _, half, _, n = gu.shape
    r = w_down.shape[2]
    bt = _row_block(t, BIG_ROW_BLOCK)

    def body(dy_ref, w_ref, gu_ref, d_ref):
        da = lax.dot_general(dy_ref[...], w_ref[...].reshape(2 * r, d), (((1,), (1,)), ((), ())),
                             preferred_element_type=F32)
        g = gu_ref[0]
        u = gu_ref[1]
        sig = jax.nn.sigmoid(g)
        d_ref[0] = (da * u * sig * (1.0 + g * (1.0 - sig))).astype(BF16)
        d_ref[1] = (da * g * sig).astype(BF16)

    return pl.pallas_call(
        body, name="ffn_down_bwd_a", grid=(half, t // bt),
        in_specs=[pl.BlockSpec((bt, d), lambda j, i: (i, 0)),
                  pl.BlockSpec((2, None, r, d), lambda j, i: (j, lb, 0, 0)),
                  pl.BlockSpec((2, None, bt, n), lambda j, i: (0, j, i, 0))],
        out_specs=pl.BlockSpec((2, None, bt, n), lambda j, i: (0, j, i, 0)),
        out_shape=_sds((2, half, t, n), BF16),
        compiler_params=_params(("parallel", "parallel")),
    )(dy, w_down, gu)


_INV_SQRT2 = 0.7071067811865476
_INV_SQRT_2PI = 0.3989422804014327


def _gelu(z):
    return 0.5 * z * (1.0 + lax.erf(z * _INV_SQRT2))


def _gelu_grad(z):
    return 0.5 * (1.0 + lax.erf(z * _INV_SQRT2)) + z * jnp.exp(-0.5 * z * z) * _INV_SQRT_2PI


def _window_mask():
    t_out = lax.broadcasted_iota(jnp.int32, (GMLP_WINDOW, GMLP_WINDOW), 0)
    s_in = lax.broadcasted_iota(jnp.int32, (GMLP_WINDOW, GMLP_WINDOW), 1)
    return (s_in // CHUNK) <= (t_out // CHUNK)


def _gmlp_recompute(z_ref, bin_ref, lng_ref, lnb_ref):
    half = N_DEV // 2
    z = z_ref[...] + bin_ref[...]
    ge = _gelu(z)
    u = ge[:half]
    v = ge[half:]
    width = half * v.shape[-1]
    mu = jnp.sum(jnp.sum(v, axis=0), axis=-1, keepdims=True) / width
    vc = v - mu
    var = jnp.sum(jnp.sum(vc * vc, axis=0), axis=-1, keepdims=True) / width
    rstd = lax.rsqrt(var + LN_EPS)
    xhat = vc * rstd
    vn = xhat * lng_ref[...] + lnb_ref[...]
    return z, u, xhat, rstd, vn


def _gmlp_mid_fwd(zpre, b_in, ln_g, ln_b, w_s, b_s):
    _, t, n = zpre.shape
    half = N_DEV // 2
    gd = half * n // GMLP_GROUPS
    per = n // gd
    w = GMLP_WINDOW

    def body(z_ref, bin_ref, lng_ref, lnb_ref, ws_ref, bs_ref, o_ref):
        _, u, _, _, vn = _gmlp_recompute(z_ref, bin_ref, lng_ref, lnb_ref)
        mask = _window_mask()
        for g in range(GMLP_GROUPS):
            sh, c0 = g // per, (g % per) * gd
            wsm = jnp.where(mask, ws_ref[g], 0.0).astype(BF16)
            s = jnp.dot(wsm, vn[sh][:, c0:c0 + gd].astype(BF16), preferred_element_type=F32) + bs_ref[g]
            o_ref[sh, :, c0:c0 + gd] = (u[sh][:, c0:c0 + gd] * s).astype(BF16)

    whole = lambda a: pl.BlockSpec(tuple(a.shape), lambda i: (0,) * a.ndim)
    return pl.pallas_call(
        body, name="gmlp_mid_fwd", grid=(t // w,),
        in_specs=[pl.BlockSpec((N_DEV, w, n), lambda i: (0, i, 0)),
                  whole(b_in), whole(ln_g), whole(ln_b), whole(w_s), whole(b_s)],
        out_specs=pl.BlockSpec((half, w, n), lambda i: (0, i, 0)),
        out_shape=_sds((half, t, n), BF16),
        compiler_params=_params(("parallel",)),
    )(zpre, b_in, ln_g, ln_b, w_s, b_s)


def _gmlp_mid_bwd(zpre, dgated, b_in, ln_g, ln_b, w_s, b_s):
    _, t, n = zpre.shape
    half = N_DEV // 2
    gd = half * n // GMLP_GROUPS
    per = n // gd
    w = GMLP_WINDOW
    width = half * n

    def body(z_ref, dg_ref, bin_ref, lng_ref, lnb_ref, ws_ref, bs_ref,
             dz_ref, dws_ref, dbs_ref, dlng_ref, dlnb_ref, dbin_ref, du_ref, dvn_ref):
        @pl.when(pl.program_id(0) == 0)
        def _():
            for r in (dws_ref, dbs_ref, dlng_ref, dlnb_ref, dbin_ref):
                r[...] = jnp.zeros_like(r)

        z, u, xhat, rstd, vn = _gmlp_recompute(z_ref, bin_ref, lng_ref, lnb_ref)
        mask = _window_mask()
        for g in range(GMLP_GROUPS):
            sh, c0 = g // per, (g % per) * gd
            wsm = jnp.where(mask, ws_ref[g], 0.0).astype(BF16)
            vg = vn[sh][:, c0:c0 + gd].astype(BF16)
            s = jnp.dot(wsm, vg, preferred_element_type=F32) + bs_ref[g]
            dgt = dg_ref[sh, :, c0:c0 + gd]
            ds = dgt * u[sh][:, c0:c0 + gd]
            du_ref[sh, :, c0:c0 + gd] = dgt * s
            dsb = ds.astype(BF16)
            dws = lax.dot_general(dsb, vg, (((1,), (1,)), ((), ())), preferred_element_type=F32)
            dws_ref[g] += jnp.where(mask, dws, 0.0)
            dbs_ref[g] += jnp.sum(ds, axis=-1, keepdims=True)
            dvn_ref[sh, :, c0:c0 + gd] = lax.dot_general(wsm, dsb, (((0,), (0,)), ((), ())),
                                                         preferred_element_type=F32)
        dvn = dvn_ref[...]
        dlng_ref[...] += jnp.sum(dvn * xhat, axis=1, keepdims=True)
        dlnb_ref[...] += jnp.sum(dvn, axis=1, keepdims=True)
        dxh = dvn * lng_ref[...]
        m1 = jnp.sum(jnp.sum(dxh, axis=0), axis=-1, keepdims=True) / width
        m2 = jnp.sum(jnp.sum(dxh * xhat, axis=0), axis=-1, keepdims=True) / width
        dv = rstd * (dxh - m1 - xhat * m2)
        gg = _gelu_grad(z)
        dzu = du_ref[...] * gg[:half]
        dzv = dv * gg[half:]
        dz_ref[:half] = dzu.astype(BF16)
        dz_ref[half:] = dzv.astype(BF16)
        dbin_ref[:half] += jnp.sum(dzu, axis=1, keepdims=True)
        dbin_ref[half:] += jnp.sum(dzv, axis=1, keepdims=True)

    whole = lambda a: pl.BlockSpec(tuple(a.shape), lambda i: (0,) * a.ndim)
    outs = [_sds((N_DEV, t, n), BF16), _sds(w_s.shape, F32), _sds(b_s.shape, F32),
            _sds(ln_g.shape, F32), _sds(ln_b.shape, F32), _sds(b_in.shape, F32)]
    return pl.pallas_call(
        body, name="gmlp_mid_bwd", grid=(t // w,),
        in_specs=[pl.BlockSpec((N_DEV, w, n), lambda i: (0, i, 0)),
                  pl.BlockSpec((half, w, n), lambda i: (0, i, 0)),
                  whole(b_in), whole(ln_g), whole(ln_b), whole(w_s), whole(b_s)],
        out_specs=[pl.BlockSpec((N_DEV, w, n), lambda i: (0, i, 0))] + [whole(o) for o in outs[1:]],
        out_shape=outs,
        scratch_shapes=[pltpu.VMEM((half, w, n), F32), pltpu.VMEM((half, w, n), F32)],
        compiler_params=_params(("arbitrary",)),
    )(zpre, dgated, b_in, ln_g, ln_b, w_s, b_s)


ATTN_CHUNKS = 4
ATTN_ROWS = ATTN_CHUNKS * CHUNK
ATTN_WINDOW = ATTN_ROWS + LEFT_PAD
ATTN_DIAGS = 1024
ATTN_ROLL = ATTN_DIAGS - (ATTN_ROWS - 1)


def _rel_vector(rel):
    j = np.arange(ATTN_DIAGS)
    idx = np.clip(ATTN_WINDOW - 1 - j, -(CHUNK - 1), MAX_REL) + (CHUNK - 1)
    return rel[:, idx]


def _attn_bias_mask(rel_ref, bm_ref):
    tt = lax.broadcasted_iota(jnp.int32, (ATTN_ROWS, ATTN_WINDOW), 0) // CHUNK
    rr = lax.broadcasted_iota(jnp.int32, (ATTN_ROWS, ATTN_WINDOW), 1) // CHUNK
    band = (rr >= tt) & (rr <= tt + LEFT_CHUNKS)
    for j in range(bm_ref.shape[0]):
        vec = jnp.broadcast_to(rel_ref[j:j + 1, :], (ATTN_ROWS, ATTN_DIAGS))
        toeplitz = pltpu.roll(vec, ATTN_ROLL, 1, stride=1, stride_axis=0)[:, :ATTN_WINDOW]
        bm_ref[j] = jnp.where(band, toeplitz, -jnp.inf)


def _attn_probs(q_ref, k_ref, bm_ref, j, hd, start, valid):
    qh = q_ref[:, j * hd:(j + 1) * hd]
    kb = k_ref[pl.ds(start, ATTN_WINDOW), j * hd:(j + 1) * hd]
    sc = lax.dot_general(qh, kb, (((1,), (1,)), ((), ())), preferred_element_type=F32)
    sc = sc * (hd ** -0.5) + bm_ref[j]
    sc = jnp.where(valid, sc, -jnp.inf)
    sc = sc - jnp.max(sc, axis=-1, keepdims=True)
    e = jnp.exp(sc)
    return e / jnp.sum(e, axis=-1, keepdims=True), qh, kb


def _window_valid(start):
    r = lax.broadcasted_iota(jnp.int32, (1, ATTN_WINDOW), 1)
    return (start + r) >= LEFT_PAD


def _attn_fwd(q, kvp, rel_vec):
    t, d = q.shape
    hd = d // N_HEADS
    half = N_DEV // 2
    n = kvp.shape[-1]
    per = n // hd
    rows = kvp.shape[1]

    def body(q_ref, k_ref, v_ref, rel_ref, o_ref, bm_ref):
        @pl.when(pl.program_id(1) == 0)
        def _():
            _attn_bias_mask(rel_ref, bm_ref)

        start = pl.multiple_of(pl.program_id(1) * ATTN_ROWS, ATTN_ROWS)
        valid = _window_valid(start)
        for j in range(per):
            p, _, _ = _attn_probs(q_ref, k_ref, bm_ref, j, hd, start, valid)
            vb = v_ref[pl.ds(start, ATTN_WINDOW), j * hd:(j + 1) * hd]
            o_ref[:, j * hd:(j + 1) * hd] = jnp.dot(p.astype(BF16), vb, preferred_element_type=F32).astype(BF16)

    return pl.pallas_call(
        body, name="attn_fwd", grid=(half, t // ATTN_ROWS),
        in_specs=[pl.BlockSpec((ATTN_ROWS, n), lambda g, i: (i, g)),
                  pl.BlockSpec((None, rows, n), lambda g, i: (g, 0, 0)),
                  pl.BlockSpec((None, rows, n), lambda g, i: (half + g, 0, 0)),
                  pl.BlockSpec((None, per, ATTN_DIAGS), lambda g, i: (g, 0, 0))],
        out_specs=pl.BlockSpec((ATTN_ROWS, n), lambda g, i: (i, g)),
        out_shape=_sds((t, d), BF16),
        scratch_shapes=[pltpu.VMEM((per, ATTN_ROWS, ATTN_WINDOW), F32)],
        compiler_params=_params(("arbitrary", "arbitrary")),
    )(q, kvp, kvp, rel_vec.reshape(half, per, ATTN_DIAGS))


def _attn_bwd(q, dout, kvp, rel_vec, dk_in=None, dv_in=None):
    t, d = q.shape
    hd = d // N_HEADS
    half = N_DEV // 2
    n = kvp.shape[-1]
    per = n // hd
    rows = kvp.shape[1]
    scale = hd ** -0.5
    carry = dk_in is not None

    def body(q_ref, do_ref, k_ref, v_ref, rel_ref, *rest):
        dq_ref, dk_ref, dv_ref, dsc_ref, bm_ref = rest[-5:]

        @pl.when(pl.program_id(1) == 0)
        def _():
            _attn_bias_mask(rel_ref, bm_ref)
            dk_ref[...] = rest[0][...] if carry else jnp.zeros_like(dk_ref)
            dv_ref[...] = rest[1][...] if carry else jnp.zeros_like(dv_ref)
            dsc_ref[...] = jnp.zeros_like(dsc_ref)

        start = pl.multiple_of(pl.program_id(1) * ATTN_ROWS, ATTN_ROWS)
        valid = _window_valid(start)
        for j in range(per):
            cols = slice(j * hd, (j + 1) * hd)
            p, qh, kb = _attn_probs(q_ref, k_ref, bm_ref, j, hd, start, valid)
            vb = v_ref[pl.ds(start, ATTN_WINDOW), cols]
            doh = do_ref[:, cols]
            dp = lax.dot_general(doh, vb, (((1,), (1,)), ((), ())), preferred_element_type=F32)
            ds = p * (dp - jnp.sum(dp * p, axis=-1, keepdims=True))
            dsc_ref[j] += sum(ds[a * CHUNK:(a + 1) * CHUNK, a * CHUNK:a * CHUNK + BAND]
                              for a in range(ATTN_CHUNKS))
            dsb = (ds * scale).astype(BF16)
            dq_ref[:, cols] = jnp.dot(dsb, kb, preferred_element_type=F32).astype(BF16)
            dk_ref[pl.ds(start, ATTN_WINDOW), cols] += lax.dot_general(
                dsb, qh, (((0,), (0,)), ((), ())), preferred_element_type=F32)
            dv_ref[pl.ds(start, ATTN_WINDOW), cols] += lax.dot_general(
                p.astype(BF16), doh, (((0,), (0,)), ((), ())), preferred_element_type=F32)

    tile = pl.BlockSpec((ATTN_ROWS, n), lambda g, i: (i, g))
    shard = pl.BlockSpec((None, rows, n), lambda g, i: (g, 0, 0))
    in_specs = [tile, tile, shard, pl.BlockSpec((None, rows, n), lambda g, i: (half + g, 0, 0)),
                pl.BlockSpec((None, per, ATTN_DIAGS), lambda g, i: (g, 0, 0))]
    operands = [q, dout, kvp, kvp, rel_vec.reshape(half, per, ATTN_DIAGS)]
    if carry:
        in_specs += [shard, shard]
        operands += [dk_in, dv_in]
    acc = _sds((half, rows, n), F32)
    return pl.pallas_call(
        body, name="attn_bwd", grid=(half, t // ATTN_ROWS),
        in_specs=in_specs,
        out_specs=[tile, shard, shard, pl.BlockSpec((per, CHUNK, BAND), lambda g, i: (g, 0, 0))],
        out_shape=[_sds((t, d), BF16), acc, acc, _sds((N_HEADS, CHUNK, BAND), F32)],
        scratch_shapes=[pltpu.VMEM((per, ATTN_ROWS, ATTN_WINDOW), F32)],
        compiler_params=_params(("arbitrary", "arbitrary")),
    )(*operands)


SKEW_PITCH = 640
SKEW = SKEW_PITCH + 1
SKEW_LANES = -(-SKEW // LANES) * LANES


def _skew_diagonals(dsc):
    h = dsc.shape[0]
    wide = jnp.pad(dsc, ((0, 0), (0, 0), (0, SKEW_PITCH - BAND))).reshape(h, CHUNK * SKEW_PITCH)
    wide = jnp.pad(wide, ((0, 0), (0, CHUNK))).reshape(h, CHUNK, SKEW)
    return jnp.pad(wide, ((0, 0), (0, 0), (0, SKEW_LANES - SKEW)))


def _rel_bias_grad(skewed):
    heads = skewed.shape[0]
    hb = SUBLANES

    def body(d_ref, o_ref):
        col = lax.broadcasted_iota(jnp.int32, (SKEW_LANES, N_REL), 0)
        bucket = lax.broadcasted_iota(jnp.int32, (SKEW_LANES, N_REL), 1)
        diag = jnp.where(col < BAND, col, col - SKEW)
        idx = jnp.clip(LEFT_PAD - diag, -(CHUNK - 1), MAX_REL) + (CHUNK - 1)
        oh = ((idx == bucket) & (col < SKEW)).astype(BF16)
        dv = jnp.sum(d_ref[...], axis=1)
        hi = dv.astype(BF16)
        rest = dv - hi.astype(F32)
        mid = rest.astype(BF16)
        lo = (rest - mid.astype(F32)).astype(BF16)
        acc = jnp.dot(hi, oh, preferred_element_type=F32)
        acc += jnp.dot(mid, oh, preferred_element_type=F32)
        acc += jnp.dot(lo, oh, preferred_element_type=F32)
        o_ref[...] = acc

    return pl.pallas_call(
        body, name="rel_bias_grad", grid=(heads // hb,),
        in_specs=[pl.BlockSpec((hb, CHUNK, SKEW_LANES), lambda i: (i, 0, 0))],
        out_specs=pl.BlockSpec((hb, N_REL), lambda i: (i, 0)),
        out_shape=_sds((heads, N_REL), F32),
        compiler_params=_params(("parallel",)),
    )(skewed)


def _sum_parts(parts):
    s_n, rows, c = parts.shape
    br = _row_block(rows, OPT_ROW_BLOCK)

    def body(p_ref, o_ref):
        acc = p_ref[0].astype(F32)
        for s in range(1, s_n):
            acc = acc + p_ref[s].astype(F32)
        o_ref[...] = acc

    return pl.pallas_call(
        body, name="sum_parts", grid=(rows // br,),
        in_specs=[pl.BlockSpec((s_n, br, c), lambda i: (0, i, 0))],
        out_specs=pl.BlockSpec((br, c), lambda i: (i, 0)),
        out_shape=_sds((rows, c), F32),
        compiler_params=_params(("parallel",)),
    )(parts)


def _adamw(own, own_idx, parts, w, m, v, row0=0, bufs=None, after=None):
    _, rows, c = own.shape
    s_n = 0 if parts is None else parts.shape[0]
    total = w.shape[0]
    br = _row_block(rows, OPT_ROW_BLOCK)
    assert row0 % br == 0 and (bufs is not None or (row0 == 0 and total == rows))
    b0 = row0 // br
    m_corr = 1.0 - ADAM_B1 ** ADAM_STEP
    v_corr = 1.0 - ADAM_B2 ** ADAM_STEP

    def body(idx_ref, own_ref, *refs):
        if s_n:
            p_ref, refs = refs[0], refs[1:]
        w_ref, m_ref, v_ref = refs[:3]
        g_ref, d_ref, nm_ref, nv_ref = refs[-4:]
        g = own_ref[...].astype(F32)
        for s in range(s_n):
            g = g + p_ref[s].astype(F32)
        nm = ADAM_B1 * m_ref[...] + (1.0 - ADAM_B1) * g
        nv = ADAM_B2 * v_ref[...] + (1.0 - ADAM_B2) * (g * g)
        g_ref[...] = g
        nm_ref[...] = nm
        nv_ref[...] = nv
        d_ref[...] = -ADAM_LR * ((nm / m_corr) / (jnp.sqrt(nv / v_corr) + ADAM_EPS) + ADAM_WD * w_ref[...])

    tile = pl.BlockSpec((br, c), lambda i, idx: (i + b0, 0))
    in_specs = [pl.BlockSpec((None, br, c), lambda i, idx: (idx[0], i, 0))]
    operands = [own_idx, own]
    if s_n:
        in_specs.append(pl.BlockSpec((s_n, br, c), lambda i, idx: (0, i, 0)))
        operands.append(parts)
    in_specs += [tile, tile, tile]
    operands += [w, m, v]
    aliases = {}
    if bufs is not None:
        aliases = {len(operands) + j: j for j in range(4)}
        in_specs += [_ANY] * 4
        operands += list(bufs)
    if after is not None:
        in_specs.append(_ANY)
        operands.append(after)
    out = _sds((total, c), F32)
    return pl.pallas_call(
        body, name="adamw",
        grid_spec=pltpu.PrefetchScalarGridSpec(
            num_scalar_prefetch=1, grid=(rows // br,), in_specs=in_specs,
            out_specs=[tile, tile, tile, tile]),
        out_shape=[out, out, out, out],
        input_output_aliases=aliases,
        compiler_params=_params(("parallel",)),
    )(*operands)


def _chip_sum(p, r1, core):
    half = N_DEV // 2
    c = p.shape[-1]
    rows = int(np.prod(p.shape[1:-1]))
    br = _row_block(rows, BIG_ROW_BLOCK)

    def body(core_ref, p_ref, r_ref, o_ref):
        o_ref[...] = (p_ref[...].astype(F32) + r_ref[...].astype(F32)).astype(BF16)

    out = pl.pallas_call(
        body, name="chip_sum",
        grid_spec=pltpu.PrefetchScalarGridSpec(
            num_scalar_prefetch=1, grid=(half, rows // br),
            in_specs=[pl.BlockSpec((None, None, br, c), lambda q, i, cr: (q, cr[0], i, 0)),
                      pl.BlockSpec((None, br, c), lambda q, i, cr: (q, i, 0))],
            out_specs=pl.BlockSpec((None, br, c), lambda q, i, cr: (q, i, 0))),
        out_shape=_sds((half, rows, c), BF16),
        compiler_params=_params(("parallel", "parallel")),
    )(core, p.reshape(half, 2, rows, c), r1.reshape(half, rows, c))
    return out.reshape((half,) + p.shape[1:])


def _position():
    return tuple(lax.axis_index(a) for a in MESH_AXES)


def _linear(px, py, pc):
    return 4 * px + 2 * py + pc


def _all_gather_small(v, after=()):
    rows, lanes = v.shape

    def body(x_ref, *rest):
        out_ref, send_sems, recv_sems, local_sem = rest[-4:]
        x, y, c = _position()
        me, sibling = (x, y, c), (x, y, 1 - c)
        chips = [(1 - x, y), (x, 1 - y), (1 - x, 1 - y)]

        def copy(k, block, to, src=None):
            dst = out_ref.at[_linear(*block)]
            return pltpu.make_async_remote_copy(
                src_ref=dst if src is None else src, dst_ref=dst,
                send_sem=send_sems.at[k], recv_sem=recv_sems.at[k],
                device_id=to, device_id_type=MESH_ID)

        mine = pltpu.make_async_copy(x_ref, out_ref.at[_linear(*me)], local_sem)
        mine.start()
        first = [copy(0, me, sibling, src=x_ref)]
        first += [copy(1 + j, me, (*chip, c), src=x_ref) for j, chip in enumerate(chips)]
        for cp in first:
            cp.start()
        passed = [copy(4 + j, (*chip, c), sibling) for j, chip in enumerate(chips)]
        for j, chip in enumerate(chips):
            copy(1 + j, (*chip, c), me).wait_recv()
            passed[j].start()
        copy(0, sibling, me).wait_recv()
        for j, chip in enumerate(chips):
            copy(4 + j, (*chip, 1 - c), me).wait_recv()
        for cp in first + passed:
            cp.wait_send()
        mine.wait()

    return pl.pallas_call(
        body, name="all_gather_small",
        out_shape=_sds((N_DEV, rows, lanes), v.dtype),
        in_specs=[_VMEM] + [_ANY] * len(after), out_specs=_VMEM,
        scratch_shapes=[pltpu.SemaphoreType.DMA((7,)), pltpu.SemaphoreType.DMA((7,)),
                        pltpu.SemaphoreType.DMA],
        compiler_params=pltpu.CompilerParams(vmem_limit_bytes=VMEM_LIMIT),
    )(v, *after)


_HBM = pl.BlockSpec(memory_space=pltpu.HBM)
_SEM = pl.BlockSpec(memory_space=pltpu.SEMAPHORE)
_EFFECT = pltpu.SideEffectType.DATAFLOW_SIDE_EFFECTING
_ALL_CHIPS = [(0, 0), (0, 1), (1, 0), (1, 1)]


def _other_chips(x, y):
    return [(1 - x, y), (x, 1 - y), (1 - x, 1 - y)]


def _in_hbm(a):
    return pltpu.with_memory_space_constraint(a, pltpu.HBM)


def _token():
    return _sds((SUBLANES, LANES), F32)


def _gather_ici_copy(ref, i, k, chip, c, block, send_sems, recv_sems):
    return pltpu.make_async_remote_copy(
        src_ref=ref.at[block], dst_ref=ref.at[block],
        send_sem=send_sems.at[3 * i + k], recv_sem=recv_sems.at[3 * i + k],
        device_id=(*chip, c), device_id_type=MESH_ID)


def _gather_ici_start(name, lands):
    n = len(lands)

    def body(*refs):
        ins, send_sems, recv_sems, token = refs[:n], refs[n], refs[n + 1], refs[-1]
        x, y, c = _position()
        me = _linear(x, y, c)
        for i in range(n):
            for k, chip in enumerate(_other_chips(x, y)):
                _gather_ici_copy(ins[i], i, k, chip, c, me, send_sems, recv_sems).start()
        token[...] = jnp.zeros_like(token)

    out = pl.pallas_call(
        body, name=name,
        out_shape=(pltpu.SemaphoreType.DMA((3 * n,)), pltpu.SemaphoreType.DMA((3 * n,)),
                   *[pltpu.HBM(a.shape, a.dtype) for a in lands], _token()),
        in_specs=[_HBM] * n, out_specs=(_SEM, _SEM, *[_HBM] * n, _VMEM),
        input_output_aliases={i: 2 + i for i in range(n)},
        compiler_params=pltpu.CompilerParams(has_side_effects=_EFFECT),
    )(*[_in_hbm(a) for a in lands])
    return out[0], out[1], list(out[2:2 + n]), out[-1]


def _gather_ici_wait(name, lands, send_sems, recv_sems, after):
    n = len(lands)

    def body(*refs):
        ins, ss, rs = refs[:n], refs[n], refs[n + 1]
        x, y, c = _position()
        me = _linear(x, y, c)
        for i in range(n):
            for k, chip in enumerate(_other_chips(x, y)):
                _gather_ici_copy(ins[i], i, k, chip, c, me, ss, rs).wait_send()
                _gather_ici_copy(ins[i], i, k, chip, c, _linear(*chip, c), ss, rs).wait_recv()

    out = pl.pallas_call(
        body, name=name,
        out_shape=[pltpu.HBM(a.shape, a.dtype) for a in lands],
        in_specs=[_HBM] * n + [_SEM, _SEM, _ANY], out_specs=[_HBM] * n,
        input_output_aliases={i: i for i in range(n)},
        compiler_params=pltpu.CompilerParams(has_side_effects=_EFFECT),
    )(*lands, send_sems, recv_sems, after)
    return list(out)


def _gather_d2d(lands):
    n = len(lands)

    def body(*refs):
        ins, outs, send_sems, recv_sems = refs[:n], refs[n:2 * n], refs[2 * n], refs[2 * n + 1]
        x, y, c = _position()

        def copy(i, q, core):
            block = _linear(*_ALL_CHIPS[q], core)
            return pltpu.make_async_remote_copy(
                src_ref=ins[i].at[block], dst_ref=outs[i].at[block],
                send_sem=send_sems.at[i, q], recv_sem=recv_sems.at[i, q],
                device_id=(x, y, 1 - c), device_id_type=MESH_ID)

        sent = [copy(i, q, c) for i in range(n) for q in range(len(_ALL_CHIPS))]
        for cp in sent:
            cp.start()
        for i in range(n):
            for q in range(len(_ALL_CHIPS)):
                copy(i, q, 1 - c).wait_recv()
        for cp in sent:
            cp.wait_send()

    return pl.pallas_call(
        body, name="gather_d2d",
        out_shape=[_sds(a.shape, a.dtype) for a in lands],
        in_specs=[_ANY] * n, out_specs=[_ANY] * n,
        input_output_aliases={i: i for i in range(n)},
        scratch_shapes=[pltpu.SemaphoreType.DMA((n, 4)), pltpu.SemaphoreType.DMA((n, 4))],
    )(*lands)


def _partials_d2d(parts):
    n = len(parts)
    half = N_DEV // 2

    def body(*refs):
        ins, outs, send_sems, recv_sems = refs[:n], refs[n:2 * n], refs[2 * n], refs[2 * n + 1]
        x, y, c = _position()

        def copy(i, q):
            return pltpu.make_async_remote_copy(
                src_ref=ins[i].at[_linear(*_ALL_CHIPS[q], 1 - c)], dst_ref=outs[i].at[q],
                send_sem=send_sems.at[i, q], recv_sem=recv_sems.at[i, q],
                device_id=(x, y, 1 - c), device_id_type=MESH_ID)

        sent = [copy(i, q) for i in range(n) for q in range(half)]
        for cp in sent:
            cp.start()
        for cp in sent:
            cp.wait_recv()
        for cp in sent:
            cp.wait_send()

    return pl.pallas_call(
        body, name="partials_d2d",
        out_shape=[_sds((half,) + p.shape[1:], p.dtype) for p in parts],
        in_specs=[_ANY] * n, out_specs=[_ANY] * n,
        scratch_shapes=[pltpu.SemaphoreType.DMA((n, half)), pltpu.SemaphoreType.DMA((n, half))],
    )(*parts)


def _partials_peers(x, y, c, direct):
    chips = _other_chips(x, y)
    if not direct:
        return [((*ch, c), 2 * ch[0] + ch[1]) for ch in chips]
    peers = [(x, y, 1 - c)] + [(*ch, c) for ch in chips] + [(*ch, 1 - c) for ch in chips]
    return [(p, _linear(*p)) for p in peers]


def _partials_copies(srcs, lands, send_sems, recv_sems, direct):
    x, y, c = _position()
    peers = _partials_peers(x, y, c, direct)
    return [pltpu.make_async_remote_copy(
        src_ref=srcs[i].at[block], dst_ref=lands[i].at[k],
        send_sem=send_sems.at[len(peers) * i + k], recv_sem=recv_sems.at[len(peers) * i + k],
        device_id=peer, device_id_type=MESH_ID)
        for i in range(len(srcs)) for k, (peer, block) in enumerate(peers)]


def _partials_send_start(name, srcs, lands, direct, after=None):
    n = len(srcs)
    n_sem = n * (N_DEV - 1 if direct else len(_ALL_CHIPS) - 1)

    def body(*refs):
        _, send_sems, recv_sems = refs[:2 * n], refs[-2 * n - 3], refs[-2 * n - 2]
        for cp in _partials_copies(refs[:n], refs[n:2 * n], send_sems, recv_sems, direct):
            cp.start()
        refs[-1][...] = jnp.zeros_like(refs[-1])

    both = list(srcs) + list(lands)
    extra = [] if after is None else [after]
    out = pl.pallas_call(
        body, name=name,
        out_shape=(pltpu.SemaphoreType.DMA((n_sem,)), pltpu.SemaphoreType.DMA((n_sem,)),
                   *[pltpu.HBM(a.shape, a.dtype) for a in both], _token()),
        in_specs=[_HBM] * (2 * n) + [_ANY] * len(extra), out_specs=(_SEM, _SEM, *[_HBM] * (2 * n), _VMEM),
        input_output_aliases={i: 2 + i for i in range(2 * n)},
        compiler_params=pltpu.CompilerParams(has_side_effects=_EFFECT),
    )(*[_in_hbm(a) for a in both], *extra)
    return out[0], out[1], list(out[2:2 + n]), list(out[2 + n:2 + 2 * n]), out[-1]


def _partials_send_wait(name, srcs, lands, send_sems, recv_sems, direct, after):
    n = len(srcs)

    def body(*refs):
        for cp in _partials_copies(refs[:n], refs[n:2 * n], refs[2 * n], refs[2 * n + 1], direct):
            cp.wait_send()
            cp.wait_recv()

    both = list(srcs) + list(lands)
    out = pl.pallas_call(
        body, name=name,
        out_shape=[pltpu.HBM(a.shape, a.dtype) for a in both],
        in_specs=[_HBM] * (2 * n) + [_SEM, _SEM, _ANY], out_specs=[_HBM] * (2 * n),
        input_output_aliases={i: i for i in range(2 * n)},
        compiler_params=pltpu.CompilerParams(has_side_effects=_EFFECT),
    )(*both, send_sems, recv_sems, after)
    return list(out[:n]), list(out[n:])


def _pack(arrs):
    flat = jnp.concatenate([a.reshape(-1).astype(F32) for a in arrs])
    block = OPT_ROW_BLOCK if flat.shape[0] > OPT_ROW_BLOCK * LANES else SUBLANES
    pad = (-flat.shape[0]) % (block * LANES)
    if pad:
        flat = jnp.concatenate([flat, jnp.zeros((pad,), F32)])
    return flat.reshape(-1, LANES)


def _unpack(packed, shapes, lead=()):
    flat = packed.reshape(lead + (-1,))
    out, off = [], 0
    for s in shapes:
        size = int(np.prod(s))
        out.append(flat[..., off:off + size].reshape(lead + tuple(s)))
        off += size
    return out


def _unshard_last(g):
    nd = g.ndim
    perm = tuple(range(1, nd - 1)) + (0, nd - 1)
    t = jnp.transpose(g, perm)
    return t.reshape(t.shape[:-2] + (N_DEV * g.shape[-1],))


def kernel(x, c, w_ada, b_ada, ln_g, ln_b, ffn_gu, ffn_down, gmlp_w_in, gmlp_b_in, gmlp_ln_g, gmlp_ln_b, gmlp_w_s, gmlp_b_s, gmlp_w_out, w_ada_kv, b_ada_kv, w_kv, attn_w_q, attn_rel_bias, attn_w_o, loss_target, m_w_ada, m_b_ada, m_ln_g, m_ln_b, m_ffn_gu, m_ffn_down, m_gmlp_w_in, m_gmlp_b_in, m_gmlp_ln_g, m_gmlp_ln_b, m_gmlp_w_s, m_gmlp_b_s, m_gmlp_w_out, m_w_ada_kv, m_b_ada_kv, m_w_kv, m_attn_w_q, m_attn_rel_bias, m_attn_w_o, v_w_ada, v_b_ada, v_ln_g, v_ln_b, v_ffn_gu, v_ffn_down, v_gmlp_w_in, v_gmlp_b_in, v_gmlp_ln_g, v_gmlp_ln_b, v_gmlp_w_s, v_gmlp_b_s, v_gmlp_w_out, v_w_ada_kv, v_b_ada_kv, v_w_kv, v_attn_w_q, v_attn_rel_bias, v_attn_w_o):
    weights = dict(w_ada=w_ada, b_ada=b_ada, ln_g=ln_g, ln_b=ln_b, ffn_gu=ffn_gu, ffn_down=ffn_down,
                   gmlp_w_in=gmlp_w_in, gmlp_b_in=gmlp_b_in, gmlp_ln_g=gmlp_ln_g, gmlp_ln_b=gmlp_ln_b,
                   gmlp_w_s=gmlp_w_s, gmlp_b_s=gmlp_b_s, gmlp_w_out=gmlp_w_out, w_ada_kv=w_ada_kv,
                   b_ada_kv=b_ada_kv, w_kv=w_kv, attn_w_q=attn_w_q, attn_rel_bias=attn_rel_bias,
                   attn_w_o=attn_w_o)
    mom1 = dict(w_ada=m_w_ada, b_ada=m_b_ada, ln_g=m_ln_g, ln_b=m_ln_b, ffn_gu=m_ffn_gu, ffn_down=m_ffn_down,
                gmlp_w_in=m_gmlp_w_in, gmlp_b_in=m_gmlp_b_in, gmlp_ln_g=m_gmlp_ln_g, gmlp_ln_b=m_gmlp_ln_b,
                gmlp_w_s=m_gmlp_w_s, gmlp_b_s=m_gmlp_b_s, gmlp_w_out=m_gmlp_w_out, w_ada_kv=m_w_ada_kv,
                b_ada_kv=m_b_ada_kv, w_kv=m_w_kv, attn_w_q=m_attn_w_q, attn_rel_bias=m_attn_rel_bias,
                attn_w_o=m_attn_w_o)
    mom2 = dict(w_ada=v_w_ada, b_ada=v_b_ada, ln_g=v_ln_g, ln_b=v_ln_b, ffn_gu=v_ffn_gu, ffn_down=v_ffn_down,
                gmlp_w_in=v_gmlp_w_in, gmlp_b_in=v_gmlp_b_in, gmlp_ln_g=v_gmlp_ln_g, gmlp_ln_b=v_gmlp_ln_b,
                gmlp_w_s=v_gmlp_w_s, gmlp_b_s=v_gmlp_b_s, gmlp_w_out=v_gmlp_w_out, w_ada_kv=v_w_ada_kv,
                b_ada_kv=v_b_ada_kv, w_kv=v_w_kv, attn_w_q=v_attn_w_q, attn_rel_bias=v_attn_rel_bias,
                attn_w_o=v_attn_w_o)
    order = list(weights)

    x = x[0]
    target = loss_target[0]
    t, d = x.shape
    n_mod = w_ada.shape[-1] * N_DEV // d
    mod_w = w_ada.shape[-1]
    kv_w = w_ada_kv.shape[-1]
    n_b = DEPTH - N_A
    me = _linear(*_position())

    l2 = DEPTH * 2
    big = dict(
        ffn_gu=ffn_gu.reshape((l2,) + ffn_gu.shape[2:]),
        ffn_down=ffn_down.reshape((l2,) + ffn_down.shape[2:]),
        gmlp_w_in=gmlp_w_in, gmlp_w_out=gmlp_w_out, w_kv=w_kv[None],
        attn_w_q=attn_w_q, attn_w_o=attn_w_o)
    big_names = list(big)
    core = lax.axis_index("c").astype(jnp.int32).reshape(1)
    chip = (2 * lax.axis_index("x") + lax.axis_index("y")).astype(jnp.int32).reshape(1)

    fwd_groups = [
        {"ffn_gu": (0, 1), "ffn_down": (0, 1)},
        {"gmlp_w_in": (0, 1), "gmlp_w_out": (0, 1)},
        {"ffn_gu": (1, 1), "ffn_down": (1, 1)},
        {"ffn_gu": (2, 1), "ffn_down": (2, 1)},
        {"gmlp_w_in": (1, 1), "gmlp_w_out": (1, 1)},
        {"ffn_gu": (3, 1), "ffn_down": (3, 1), "w_kv": (0, 1)},
        {"ffn_gu": (4, 2), "ffn_down": (4, 2), "attn_w_q": (0, 1), "attn_w_o": (0, 1)},
        {"ffn_gu": (6, 2), "ffn_down": (6, 2), "attn_w_q": (1, 1), "attn_w_o": (1, 1)},
    ]
    bwd_groups = []
    for l in range(DEPTH):
        g = {"ffn_gu": (2 * l, 2), "ffn_down": (2 * l, 2)}
        if l < N_A:
            g.update({"gmlp_w_in": (l, 1), "gmlp_w_out": (l, 1)})
        else:
            g.update({"attn_w_q": (l - N_A, 1), "attn_w_o": (l - N_A, 1)})
        if l == N_A - 1:
            g["w_kv"] = (0, 1)
        bwd_groups.append(g)

    def slot_of(groups, name, slot):
        for gi, g in enumerate(groups):
            if name in g and g[name][0] <= slot < g[name][0] + g[name][1]:
                return gi, slot - g[name][0]
        raise KeyError((name, slot))

    flights = []
    for gi, g in enumerate(fwd_groups):
        lands = []
        for name, (s0, cnt) in g.items():
            shard = big[name][s0:s0 + cnt].astype(BF16)
            land = lax.empty((N_DEV,) + shard.shape, BF16)
            lands.append(lax.dynamic_update_slice(land, shard[None], (me,) + (0,) * shard.ndim))
        flights.append(_gather_ici_start(f"gather_ici_start_{gi}", lands))
    start_token = sum(f[3][0, 0] for f in flights)
    gathered = [None] * len(fwd_groups)

    def land_group(gi, after):
        send_sems, recv_sems, lands, _ = flights[gi]
        lands = _gather_ici_wait(f"gather_ici_wait_{gi}", lands, send_sems, recv_sems, after)
        gathered[gi] = dict(zip(fwd_groups[gi], _gather_d2d(lands)))

    def weight(name, slot):
        gi, local = slot_of(fwd_groups, name, slot)
        return gathered[gi][name], local

    swapped = ("ffn_gu",)

    def grad_shape(name):
        s = big[name].shape[1:]
        return s[:-2] + (s[-1], s[-2]) if name in swapped else s

    partial = [{name: lax.empty((N_DEV, cnt) + grad_shape(name), BF16) for name, (_, cnt) in g.items()}
               for g in bwd_groups]

    c_all = _all_gather_small(_pack([c]))
    c_all = _unpack(c_all, [(d,)], lead=(N_DEV,))[0]
    c4 = _as4(c_all)
    mod_part = _matmul("ada_fwd", c4, w_ada[:, None], (DEPTH, 1, N_DEV, mod_w), F32, a_silu=True)
    kv_part = _matmul("ada_kv_fwd", c4, _as4(w_ada_kv), (1, 1, N_DEV, kv_w), F32, a_silu=True)
    small_shapes = [mod_part.shape, kv_part.shape, ln_g.shape, ln_b.shape, gmlp_b_in.shape,
                    gmlp_ln_g.shape, gmlp_ln_b.shape, attn_rel_bias.shape]
    small = _all_gather_small(_pack([mod_part, kv_part, ln_g, ln_b, gmlp_b_in, gmlp_ln_g, gmlp_ln_b,
                                     attn_rel_bias]))
    (mod_g, kvm_g, ln_g_g, ln_b_g, b_in_g, gln_g_g, gln_b_g, rel_g) = _unpack(small, small_shapes, lead=(N_DEV,))
    mod_mine = lax.dynamic_index_in_dim(mod_g[:, :, 0], me, axis=2, keepdims=False)
    mod = _unshard_last(mod_mine) + b_ada
    mod = mod.reshape(DEPTH, n_mod, 1, d)
    kvm_mine = lax.dynamic_index_in_dim(kvm_g[:, 0, 0], me, axis=1, keepdims=False)
    mkv = (_unshard_last(kvm_mine) + b_ada_kv).reshape(2, 1, d)
    ln_g_f = _unshard_last(ln_g_g)
    ln_b_f = _unshard_last(ln_b_g)
    half = N_DEV // 2
    b_in_f = jnp.transpose(b_in_g, (1, 0, 2))[:, :, None, :]
    gln_g_f = _unshard_last(gln_g_g).reshape(N_A, half, 1, -1)
    gln_b_f = _unshard_last(gln_b_g).reshape(N_A, half, 1, -1)
    rel_f = _unshard_last(rel_g)

    def shard_act(a):
        return a.reshape(a.shape[0], a.shape[2], a.shape[3])

    def grad_into(name, slot, mm):
        gi, local = slot_of(bwd_groups, name, slot)
        partial[gi][name] = mm(partial[gi][name], local)

    def ffn_fwd(h, lw):
        w_gu, l_gu = weight("ffn_gu", lw)
        w_dn, l_dn = weight("ffn_down", lw)
        gu, a = _ffn_up_fwd(h, w_gu, l_gu)
        y = _matmul("ffn_down_fwd", a[:, None], w_dn, (1, 1, t, d), F32, lb=l_dn, b_merge=2, reduce=True)
        return y[0, 0], (gu, a)

    def ffn_bwd(dy, h, saved, lw):
        gu, a = saved
        w_gu, l_gu = weight("ffn_gu", lw)
        w_dn, l_dn = weight("ffn_down", lw)
        dgu = _ffn_down_bwd_a(dy, w_dn, l_dn, gu).reshape((N_DEV,) + gu.shape[2:])
        grad_into("ffn_down", lw, lambda buf, lo: _matmul(
            "ffn_down_bwd_w", a[:, None], _as4(dy), buf.shape, BF16, ta=True, lo=lo, out_merge=2, out_buf=buf))
        dh = _matmul("ffn_gu_bwd_a", dgu[:, None], w_gu, (1, 1, t, d), F32, lb=l_gu, tb=True, reduce=True)
        grad_into("ffn_gu", lw, lambda buf, lo: _matmul(
            "ffn_gu_bwd_w", dgu[:, None], _as4(h), buf.shape, BF16, ta=True, lo=lo, out_buf=buf))
        return dh[0, 0], {}

    def gmlp_params(l):
        return (b_in_f[l], gln_g_f[l], gln_b_f[l], gmlp_w_s[l], gmlp_b_s[l][:, :, None])

    def gmlp_fwd(h, l):
        w_in, l_in = weight("gmlp_w_in", l)
        w_out, l_out = weight("gmlp_w_out", l)
        n = w_in.shape[-1]
        zpre = _matmul("gmlp_in_fwd", _as4(h), w_in, (N_DEV, 1, t, n), F32, lb=l_in)
        gated = _gmlp_mid_fwd(shard_act(zpre), *gmlp_params(l))
        y = _matmul("gmlp_out_fwd", gated[:, None], w_out, (1, 1, t, d), F32, lb=l_out, b_merge=2, reduce=True)
        return y[0, 0], (zpre, gated)

    def gmlp_bwd(dy, h, saved, l):
        zpre, gated = saved
        w_in, l_in = weight("gmlp_w_in", l)
        w_out, l_out = weight("gmlp_w_out", l)
        n = w_in.shape[-1]
        dgated = _matmul("gmlp_out_bwd_a", _as4(dy), w_out, (half, 1, t, n), F32, lb=l_out, b_merge=2, tb=True)
        grad_into("gmlp_w_out", l, lambda buf, lo: _matmul(
            "gmlp_out_bwd_w", gated[:, None], _as4(dy), buf.shape, BF16, ta=True, lo=lo, out_merge=2, out_buf=buf))
        dz, dws, dbs, dlng, dlnb, dbin = _gmlp_mid_bwd(shard_act(zpre), shard_act(dgated), *gmlp_params(l))
        dh = _matmul("gmlp_in_bwd_a", dz[:, None], w_in, (1, 1, t, d), F32, lb=l_in, tb=True, reduce=True)
        grad_into("gmlp_w_in", l, lambda buf, lo: _matmul(
            "gmlp_in_bwd_w", _as4(h), dz[:, None], buf.shape, BF16, ta=True, lo=lo, out_buf=buf))
        small_grads = dict(gmlp_w_s=dws, gmlp_b_s=dbs[:, :, 0], gmlp_ln_g=dlng.reshape(-1),
                           gmlp_ln_b=dlnb.reshape(-1), gmlp_b_in=dbin.reshape(-1))
        return dh[0, 0], small_grads

    def attn_fwd(h, j, kvp):
        rel_vec = _rel_vector(rel_f[j])
        w_q, l_q = weight("attn_w_q", j)
        w_o, l_o = weight("attn_w_o", j)
        q = _matmul("attn_q_fwd", _as4(h), w_q, (1, 1, t, d), BF16, lb=l_q, b_merge=N_DEV, reduce=True)[0, 0]
        o = _attn_fwd(q, kvp, rel_vec)
        y = _matmul("attn_o_fwd", _as4(o), w_o, (1, 1, t, d), F32, lb=l_o, b_merge=N_DEV, reduce=True)
        return y[0, 0], (q, o, rel_vec)

    def attn_bwd(dy, h, saved, j, kvp, dkv_acc):
        q, o, rel_vec = saved
        w_q, l_q = weight("attn_w_q", j)
        w_o, l_o = weight("attn_w_o", j)
        do = _matmul("attn_o_bwd_a", _as4(dy), w_o, (1, 1, t, d), BF16, lb=l_o, b_merge=N_DEV, tb=True)[0, 0]
        grad_into("attn_w_o", j, lambda buf, lo: _matmul(
            "attn_o_bwd_w", _as4(o), _as4(dy), buf.shape, BF16, ta=True, lo=lo, out_merge=N_DEV, out_buf=buf))
        dq, dk, dv, dsc = _attn_bwd(q, do, kvp, rel_vec, *dkv_acc)
        drel = _rel_bias_grad(_skew_diagonals(dsc))
        dh = _matmul("attn_q_bwd_a", _as4(dq), w_q, (1, 1, t, d), F32, lb=l_q, b_merge=N_DEV, tb=True)
        grad_into("attn_w_q", j, lambda buf, lo: _matmul(
            "attn_q_bwd_w", _as4(h), _as4(dq), buf.shape, BF16, ta=True, lo=lo, out_merge=N_DEV, out_buf=buf))
        return dh[0, 0], dict(attn_rel_bias=drel, dkv=(dk, dv))

    tape = []
    kvp = None
    kv_tape = None
    first_use = {(0, 0): 0, (0, 1): 1, (0, 2): 2, (1, 0): 3, (1, 1): 4, (1, 2): 5, (2, 0): 6, (3, 0): 7}
    h = _modulate(x, mod[0, 1], mod[0, 0] + start_token)
    for l in range(DEPTH):
        for i in range(3):
            if (l, i) in first_use:
                land_group(first_use[l, i], x)
            scl, gate = mod[l, 3 * i + 1], mod[l, 3 * i + 2]
            wgt = 1.0 if i == 1 else 0.5
            gw = wgt * (1.0 + gate)
            if i != 1:
                y, saved = ffn_fwd(h, 2 * l + i // 2)
            elif l < N_A:
                y, saved = gmlp_fwd(h, l)
            else:
                y, saved = attn_fwd(h, l - N_A, kvp)
            nl, ni = (l, i + 1) if i < 2 else (l + 1, 0)
            readers = [(mod[nl, 3 * ni + 1], mod[nl, 3 * ni])] if nl < DEPTH else []
            shared_kv = (l, i) == (N_A - 1, 2)
            if shared_kv:
                readers.append((mkv[1], mkv[0]))
            outs = _ln_res_fwd(x, y, gw, ln_g_f[l, i][None], ln_b_f[l, i][None], readers)
            tape.append((x, h, y, gw, scl, saved))
            x = outs[0]
            h = outs[1] if nl < DEPTH else None
            if shared_kv:
                hkv = outs[-1]
                w_kvg, l_kv = weight("w_kv", 0)
                n = w_kvg.shape[-1]
                kv = _matmul("kv_fwd", _as4(hkv), w_kvg, (N_DEV, 1, t, n), BF16, lb=l_kv)
                kvp = jnp.pad(shard_act(kv), ((0, 0), (LEFT_PAD, 0), (0, 0)))
                kv_tape = hkv

    loss_part, dx = _loss_head(x, target)
    loss = lax.psum(loss_part[0, 0], MESH_AXES)

    d_mod = [[None] * n_mod for _ in range(DEPTH)]
    d_ln_g = [[None] * 3 for _ in range(DEPTH)]
    d_ln_b = [[None] * 3 for _ in range(DEPTH)]
    small_grads = {k: [None] * N_A for k in ("gmlp_w_s", "gmlp_b_s", "gmlp_ln_g", "gmlp_ln_b", "gmlp_b_in")}
    d_rel = [None] * n_b
    dkv_acc = ()
    d_mkv = None
    reductions = [None] * DEPTH
    sent_token = None
    readers = []
    for l in reversed(range(DEPTH)):
        if l == N_A - 1:
            hkv = kv_tape
            w_kvg, l_kv = weight("w_kv", 0)
            dkv = jnp.concatenate(dkv_acc)[:, LEFT_PAD:, :].astype(BF16)[:, None]
            dhkv = _matmul("kv_bwd_a", dkv, w_kvg, (1, 1, t, d), F32, lb=l_kv, tb=True, reduce=True)[0, 0]
            grad_into("w_kv", 0, lambda buf, lo: _matmul(
                "kv_bwd_w", _as4(hkv), dkv, buf.shape, BF16, ta=True, lo=lo, out_buf=buf))
            readers.append((dhkv, mkv[1], None))
        for i in reversed(range(3)):
            x_in, h, y, gw, scl, saved = tape[3 * l + i]
            wgt = 1.0 if i == 1 else 0.5
            if sent_token is not None:
                gw = gw + sent_token
                sent_token = None
            res = _ln_res_bwd(x_in, y, gw, ln_g_f[l, i][None], ln_b_f[l, i][None], dx,
                              [(r[0], r[1]) for r in readers])
            dx_res, dy, dgw, dg, db = res[:5]
            for k, (_, _, slot) in enumerate(readers):
                dscl_k, dshift_k = res[5 + 2 * k][0], res[6 + 2 * k][0]
                if slot is None:
                    d_mkv = jnp.concatenate([dshift_k, dscl_k])
                else:
                    d_mod[slot[0]][slot[1]], d_mod[slot[0]][slot[1] + 1] = dshift_k, dscl_k
            d_ln_g[l][i], d_ln_b[l][i] = dg[0], db[0]
            if i != 1:
                dh, extra = ffn_bwd(dy, h, saved, 2 * l + i // 2)
            elif l < N_A:
                dh, extra = gmlp_bwd(dy, h, saved, l)
                for k, g in extra.items():
                    small_grads[k][l] = g
            else:
                dh, extra = attn_bwd(dy, h, saved, l - N_A, kvp, dkv_acc)
                d_rel[l - N_A] = extra["attn_rel_bias"]
                dkv_acc = extra["dkv"]
            d_mod[l][3 * i + 2] = wgt * dgw[0]
            dx = dx_res
            readers = [(dh, scl, (l, 3 * i))]
        if l > 0:
            srcs = [partial[l][k] for k in bwd_groups[l]]
            lands = [lax.empty((N_DEV - 1,) + s.shape[1:], BF16) for s in srcs]
            reductions[l] = _partials_send_start(f"partials_send_start_{l}", srcs, lands, True)
            sent_token = reductions[l][4][0, 0]
    (dh, scl, _), = readers
    dx, dscl, dshift = _mod_bwd(dx, dh, tape[0][0], scl)
    d_mod[0][0], d_mod[0][1] = dshift[0], dscl[0]
    grad_x = dx[None]

    d_mod_arr = jnp.stack([jnp.concatenate(r) for r in d_mod])
    small_part = dict(
        b_ada=d_mod_arr, b_ada_kv=d_mkv,
        ln_g=jnp.stack([jnp.stack(r) for r in d_ln_g]), ln_b=jnp.stack([jnp.stack(r) for r in d_ln_b]),
        gmlp_b_in=jnp.stack(small_grads["gmlp_b_in"]), gmlp_ln_g=jnp.stack(small_grads["gmlp_ln_g"]),
        gmlp_ln_b=jnp.stack(small_grads["gmlp_ln_b"]), gmlp_w_s=jnp.stack(small_grads["gmlp_w_s"]),
        gmlp_b_s=jnp.stack(small_grads["gmlp_b_s"]), attn_rel_bias=jnp.stack(d_rel))
    small_names = list(small_part)
    sp_shapes = [small_part[k].shape for k in small_names]
    sp_all = _all_gather_small(_pack([small_part[k] for k in small_names]),
                               after=[partial[0][k] for k in bwd_groups[0]])

    from_sibling = _partials_d2d([partial[0][k] for k in bwd_groups[0]])
    sums = [_chip_sum(partial[0][k], r1, core) for k, r1 in zip(bwd_groups[0], from_sibling)]
    lands = [lax.empty((len(_ALL_CHIPS) - 1,) + s.shape[1:], BF16) for s in sums]
    reductions[0] = _partials_send_start("partials_send_start_0", sums, lands, False, after=sp_all)
    sent_token = reductions[0][4][0, 0]
    c4 = c4 + sent_token

    sp_sum = _sum_parts(sp_all)
    full_grads = dict(zip(small_names, _unpack(sp_sum, sp_shapes)))
    per_dev = dict(zip(small_names, _unpack(sp_all, sp_shapes, lead=(N_DEV,))))

    def my_cols(a, width):
        return lax.dynamic_slice_in_dim(a, me * width, width, axis=a.ndim - 1)

    grads = {}
    grads["b_ada"] = full_grads["b_ada"]
    grads["b_ada_kv"] = full_grads["b_ada_kv"]
    grads["gmlp_w_s"] = full_grads["gmlp_w_s"]
    grads["gmlp_b_s"] = full_grads["gmlp_b_s"]
    for k in ("ln_g", "ln_b", "gmlp_b_in", "gmlp_ln_g", "gmlp_ln_b", "attn_rel_bias"):
        grads[k] = my_cols(full_grads[k], weights[k].shape[-1])

    dmod_cols = jnp.transpose(my_cols(per_dev["b_ada"], mod_w), (1, 0, 2))[:, None]
    grads["w_ada"] = _matmul("ada_bwd_w", c4, dmod_cols, (DEPTH, 1, d, mod_w), F32, ta=True,
                             a_silu=True)[:, 0]
    dkv_cols = my_cols(per_dev["b_ada_kv"], kv_w)[None, None]
    grads["w_ada_kv"] = _matmul("ada_kv_bwd_w", c4, dkv_cols, (1, 1, d, kv_w), F32, ta=True,
                                a_silu=True)[0, 0]

    delta, new_m, new_v = {}, {}, {}
    first = jnp.zeros((1,), jnp.int32)

    def flat2(a, cols):
        return a.reshape(-1, cols)

    done = None
    for k in ("w_ada", "w_ada_kv"):
        w = weights[k]
        cols = w.shape[-1]
        res = _adamw(grads[k].reshape(1, -1, cols), first, None, flat2(w, cols), flat2(mom1[k], cols),
                     flat2(mom2[k], cols), after=done)
        grads[k], delta[k], new_m[k], new_v[k] = (a.reshape(w.shape) for a in res)
        done = res[0][:SUBLANES, :LANES]

    tiny = [k for k in order if k not in delta and k not in big_names]
    tiny_shapes = [weights[k].shape for k in tiny]
    tiny_out = _adamw((_pack([grads[k] for k in tiny]) + sent_token)[None], first, None,
                      _pack([weights[k] for k in tiny]), _pack([mom1[k] for k in tiny]),
                      _pack([mom2[k] for k in tiny]), after=done)
    for dst, arr in zip((grads, delta, new_m, new_v), tiny_out):
        for k, val in zip(tiny, _unpack(arr, tiny_shapes)):
            dst[k] = val

    def opt_view(k, a):
        a = jnp.swapaxes(a, -1, -2) if k in swapped else a
        return a.reshape(-1, a.shape[-1])

    def opt_unview(k, a):
        s = weights[k].shape
        return jnp.swapaxes(a.reshape(s[:-2] + (s[-1], s[-2])), -1, -2) if k in swapped else a.reshape(s)

    bufs = {k: [lax.empty(opt_view(k, weights[k]).shape, F32) for _ in range(4)] for k in big_names}
    done = tiny_out[0]
    me_idx = me.astype(jnp.int32).reshape(1)
    for l in reversed(range(DEPTH)):
        send_sems, recv_sems, srcs, lands, _ = reductions[l]
        srcs, lands = _partials_send_wait(f"partials_send_wait_{l}", srcs, lands, send_sems, recv_sems, l > 0, done)
        for k, own, got in zip(bwd_groups[l], srcs, lands):
            cols = own.shape[-1]
            slot_rows = int(np.prod(own.shape[2:-1]))
            bufs[k] = _adamw(own.reshape(own.shape[0], -1, cols), me_idx if l > 0 else chip,
                             got.reshape(got.shape[0], -1, cols),
                             opt_view(k, weights[k]), opt_view(k, mom1[k]), opt_view(k, mom2[k]),
                             row0=bwd_groups[l][k][0] * slot_rows, bufs=bufs[k], after=done)
            done = bufs[k][0][:SUBLANES, :LANES]
    for k in big_names:
        grads[k], delta[k], new_m[k], new_v[k] = (opt_unview(k, b) for b in bufs[k])

    return (loss, grad_x, *[grads[k] for k in order], *[delta[k] for k in order],
            *[new_m[k] for k in order], *[new_v[k] for k in order])
```

```python
import functools

import numpy as np
import jax
import jax.numpy as jnp
from jax import lax
from jax.experimental import pallas as pl
from jax.experimental.pallas import tpu as pltpu

F32 = jnp.float32
BF16 = jnp.bfloat16
MESH_AXES = ("x", "y", "c")
N_DEV = 8
MESH_ID = pl.DeviceIdType.MESH

DEPTH = 4
N_A = 2
CHUNK = 64
N_HEADS = 16
LEFT_CHUNKS = 8
BAND = (LEFT_CHUNKS + 1) * CHUNK
LEFT_PAD = LEFT_CHUNKS * CHUNK
MAX_REL = 4 * CHUNK
N_REL = (CHUNK - 1) + MAX_REL + 1
GMLP_WINDOW = 128
GMLP_GROUPS = 8
ALPHA = (2.0 * DEPTH) ** 0.25
LN_EPS = 1e-5
ADAM_LR = 0.001
ADAM_B1 = 0.9
ADAM_B2 = 0.999
ADAM_EPS = 1e-08
ADAM_WD = 0.01
ADAM_STEP = 10

V7X_VMEM_BYTES = 64 * 1024 * 1024
VMEM_LIMIT = V7X_VMEM_BYTES - 8 * 1024 * 1024
LANES = 128
SUBLANES = 8
MM_BLOCK = 2048
BIG_ROW_BLOCK = 1024
ROW_BLOCK = 512
OPT_ROW_BLOCK = 256

_ANY = pl.BlockSpec(memory_space=pl.ANY)
_VMEM = pl.BlockSpec(memory_space=pltpu.VMEM)


def _params(sem=None):
    return pltpu.CompilerParams(dimension_semantics=sem, vmem_limit_bytes=VMEM_LIMIT)


def _row_block(rows, target):
    for d in range(min(rows, target), 0, -1):
        if rows % d == 0 and (d % SUBLANES == 0 or d == rows):
            return d
    return rows


def _matmul(name, a, b, out_shape4, out_dtype, *, la=0, lb=0, lo=0, ta=False, tb=False,
            reduce=False, b_merge=1, out_merge=1, out_buf=None, a_silu=False):
    ja_n, _, a_r, a_c = a.shape
    jb_n, _, b_r, b_c = b.shape
    jo_n, _, o_r, o_c = out_shape4
    m_tot = a_c if ta else a_r
    k_a = a_r if ta else a_c
    b_rows = b_merge * b_r
    k_c = b_c if tb else b_rows
    n = b_rows if tb else b_c
    n_chunks = (jb_n // b_merge) if reduce else 1
    natural_k = reduce and ja_n == 1
    assert n == o_c, (name, n, o_c)
    assert k_a ==(k_c * n_chunks if natural_k else k_c), (name, k_a, k_c, n_chunks)
    bk = k_c if (k_c <= MM_BLOCK or (b_merge > 1 and not tb)) else MM_BLOCK
    assert k_c % bk == 0
    nkk = k_c // bk
    kg = 2 if (reduce and ja_n > 1 and nkk == 1 and not ta and n_chunks % 2 == 0) else 1
    nk = n_chunks * nkk // kg
    m_out = out_merge * o_r
    assert m_tot == m_out, (name, m_tot, m_out)
    bm = m_tot if (m_tot <= MM_BLOCK or out_merge > 1) else MM_BLOCK
    assert m_tot % bm == 0
    jo_blocks = jo_n // out_merge

    def a_index(j, m, k):
        kj, kk = k // nkk, k % nkk
        ja = 0 if ja_n == 1 else (kj if reduce else j)
        ke = kk + kj * nkk if natural_k else kk
        return (ja, la, ke, m) if ta else (ja, la, m, ke)

    def b_index(j, m, k):
        kj, kk = k // nkk, k % nkk
        jb = 0 if jb_n == b_merge else (kj if reduce else j)
        return (jb, lb, 0, kk) if tb else (jb, lb, kk, 0)

    def o_index(j, m, k):
        return (j, lo, 0, 0) if out_merge > 1 else (j, lo, m, 0)

    a_block = (None, None, bk, bm) if ta else (None if kg == 1 else kg, None, bm, bk)
    if b_merge > 1:
        b_block = (kg * b_merge, None, b_r, bk if tb else n)
    else:
        b_block = (None if kg == 1 else kg, None) + ((n, bk) if tb else (bk, n))
    o_block = (out_merge, None, o_r, n) if out_merge > 1 else (None, None, bm, n)
    dims = (((0 if ta else 1,), (1 if tb else 0,)), ((), ()))

    in_place = nk > 1 and out_dtype == F32 and out_merge == 1
    use_acc = nk > 1 and not in_place

    def body(a_ref, b_ref, *rest):
        o_ref = rest[-2] if use_acc else rest[-1]
        k = pl.program_id(2)
        av = a_ref[...]
        if a_silu:
            af = av.astype(F32)
            av = af * jax.nn.sigmoid(af)
        bv = b_ref[...]
        if kg > 1:
            bv = bv.reshape(kg, -1, bv.shape[-1])
            prod = sum(lax.dot_general(av[g].astype(BF16), bv[g].astype(BF16), dims, preferred_element_type=F32)
                       for g in range(kg))
        else:
            if b_merge > 1:
                bv = bv.reshape(b_rows, bv.shape[-1])
            prod = lax.dot_general(av.astype(BF16), bv.astype(BF16), dims, preferred_element_type=F32)

        def emit(val):
            val = val.astype(out_dtype)
            o_ref[...] = val.reshape(out_merge, o_r, n) if out_merge > 1 else val

        if nk == 1:
            emit(prod)
            return
        acc_ref = o_ref if in_place else rest[-1]

        @pl.when(k == 0)
        def _():
            acc_ref[...] = prod

        @pl.when(k > 0)
        def _():
            acc_ref[...] += prod

        if use_acc:
            @pl.when(k == nk - 1)
            def _():
                emit(acc_ref[...])

    in_specs = [pl.BlockSpec(a_block, a_index), pl.BlockSpec(b_block, b_index)]
    operands = [a, b]
    aliases = {}
    if out_buf is not None:
        assert out_buf.shape == tuple(out_shape4) and out_buf.dtype == out_dtype
        in_specs.append(_ANY)
        operands.append(out_buf)
        aliases = {2: 0}
    return pl.pallas_call(
        body, name=name,
        grid=(jo_blocks, m_tot // bm, nk),
        in_specs=in_specs,
        out_specs=pl.BlockSpec(o_block, o_index),
        out_shape=jax.ShapeDtypeStruct(tuple(out_shape4), out_dtype),
        scratch_shapes=[pltpu.VMEM((bm, n), F32)] if use_acc else [],
        input_output_aliases=aliases,
        compiler_params=_params(("parallel", "parallel", "arbitrary")),
    )(*operands)


def _as4(a):
    return a.reshape((1,) * (4 - a.ndim) + a.shape)


def _row_call(name, body, ins, outs, t, *, acc_outs=()):
    bt = _row_block(t, ROW_BLOCK)

    def spec(arr, tiled):
        if tiled:
            return pl.BlockSpec((bt,) + tuple(arr.shape[1:]), lambda i: (i,) + (0,) * (arr.ndim - 1))
        return pl.BlockSpec(tuple(arr.shape), lambda i: (0,) * arr.ndim)

    return pl.pallas_call(
        body, name=name, grid=(t // bt,),
        in_specs=[spec(a, tl) for a, tl in ins],
        out_specs=[spec(o, tl) for o, tl in outs],
        out_shape=[jax.ShapeDtypeStruct(o.shape, o.dtype) for o, _ in outs],
        compiler_params=_params(("arbitrary",) if acc_outs else ("parallel",)),
    )(*[a for a, _ in ins])


def _sds(shape, dtype):
    return jax.ShapeDtypeStruct(tuple(shape), dtype)


def _modulate(x, scl, shift):
    t, d = x.shape

    def body(x_ref, s_ref, b_ref, h_ref):
        h_ref[...] = (x_ref[...] * (1.0 + s_ref[...]) + b_ref[...]).astype(BF16)

    return _row_call("modulate", body, [(x, True), (scl, False), (shift, False)],
                     [(_sds((t, d), BF16), True)], t)[0]


def _ln_stats(r):
    mu = jnp.mean(r, axis=-1, keepdims=True)
    rc = r - mu
    var = jnp.mean(rc * rc, axis=-1, keepdims=True)
    rstd = lax.rsqrt(var + LN_EPS)
    return rc * rstd, rstd


def _ln_res_fwd(x, y, gw, g, b, mods=()):
    t, d = x.shape
    n_mod = len(mods)

    def body(x_ref, y_ref, gw_ref, g_ref, b_ref, *rest):
        mod_refs, o_ref, h_refs = rest[:2 * n_mod], rest[2 * n_mod], rest[2 * n_mod + 1:]
        r = ALPHA * x_ref[...] + gw_ref[...] * y_ref[...]
        xhat, _ = _ln_stats(r)
        xn = xhat * g_ref[...] + b_ref[...]
        o_ref[...] = xn
        for k in range(n_mod):
            h_refs[k][...] = (xn * (1.0 + mod_refs[2 * k][...]) + mod_refs[2 * k + 1][...]).astype(BF16)

    vecs = [(v, False) for pair in mods for v in pair]
    return _row_call("ln_res_fwd", body,
                     [(x, True), (y, True), (gw, False), (g, False), (b, False)] + vecs,
                     [(_sds((t, d), F32), True)] + [(_sds((t, d), BF16), True)] * n_mod, t)


def _ln_res_bwd(x, y, gw, g, b, dx_base, pairs=()):
    t, d = x.shape
    n_pair = len(pairs)

    def body(x_ref, y_ref, gw_ref, g_ref, b_ref, dxb_ref, *rest):
        pair_refs, outs = rest[:2 * n_pair], rest[2 * n_pair:]
        dx_ref, dy_ref = outs[0], outs[1]
        sums = outs[2:]

        @pl.when(pl.program_id(0) == 0)
        def _():
            for r in sums:
                r[...] = jnp.zeros_like(r)

        yv = y_ref[...]
        gwv = gw_ref[...]
        gv = g_ref[...]
        xhat, rstd = _ln_stats(ALPHA * x_ref[...] + gwv * yv)
        dxn = dxb_ref[...]
        if n_pair:
            xn = xhat * gv + b_ref[...]
            for k in range(n_pair):
                dh = pair_refs[2 * k][...]
                dxn = dxn + dh * (1.0 + pair_refs[2 * k + 1][...])
                sums[3 + 2 * k][...] += jnp.sum(dh * xn, axis=0, keepdims=True)
                sums[4 + 2 * k][...] += jnp.sum(dh, axis=0, keepdims=True)
        dxh = dxn * gv
        m1 = jnp.mean(dxh, axis=-1, keepdims=True)
        m2 = jnp.mean(dxh * xhat, axis=-1, keepdims=True)
        dr = rstd * (dxh - m1 - xhat * m2)
        dx_ref[...] = ALPHA * dr
        dy_ref[...] = (gwv * dr).astype(BF16)
        sums[0][...] += jnp.sum(dr * yv, axis=0, keepdims=True)
        sums[1][...] += jnp.sum(dxn * xhat, axis=0, keepdims=True)
        sums[2][...] += jnp.sum(dxn, axis=0, keepdims=True)

    vec = _sds((1, d), F32)
    n_sum = 3 + 2 * n_pair
    ins = [(x, True), (y, True), (gw, False), (g, False), (b, False), (dx_base, True)]
    for dh, scl in pairs:
        ins += [(dh, True), (scl, False)]
    return _row_call("ln_res_bwd", body, ins,
                     [(_sds((t, d), F32), True), (_sds((t, d), BF16), True)] + [(vec, False)] * n_sum, t,
                     acc_outs=tuple(range(2, 2 + n_sum)))


def _mod_bwd(dx_res, dh, x, scl):
    t, d = x.shape

    def body(dxr_ref, dh_ref, x_ref, s_ref, dx_ref, ds_ref, db_ref):
        @pl.when(pl.program_id(0) == 0)
        def _():
            ds_ref[...] = jnp.zeros_like(ds_ref)
            db_ref[...] = jnp.zeros_like(db_ref)

        dh = dh_ref[...]
        dx_ref[...] = dxr_ref[...] + dh * (1.0 + s_ref[...])
        ds_ref[...] += jnp.sum(dh * x_ref[...], axis=0, keepdims=True)
        db_ref[...] += jnp.sum(dh, axis=0, keepdims=True)

    vec = _sds((1, d), F32)
    return _row_call("mod_bwd", body, [(dx_res, True), (dh, True), (x, True), (scl, False)],
                     [(_sds((t, d), F32), True), (vec, False), (vec, False)], t, acc_outs=(1, 2))


def _loss_head(y, target):
    t, d = y.shape

    def body(y_ref, t_ref, l_ref, dy_ref):
        @pl.when(pl.program_id(0) == 0)
        def _():
            l_ref[...] = jnp.zeros_like(l_ref)

        err = y_ref[...] - t_ref[...]
        dy_ref[...] = err * (1.0 / d)
        part = 0.5 * jnp.sum(jnp.mean(err * err, axis=-1, keepdims=True), axis=0, keepdims=True)
        l_ref[...] += jnp.broadcast_to(part, l_ref.shape)

    return _row_call("loss_head", body, [(y, True), (target, True)],
                     [(_sds((SUBLANES, LANES), F32), False), (_sds((t, d), F32), True)], t,
                     acc_outs=(0,))


def _sigmoid(x):
    return 0.5 * jnp.tanh(0.5 * x) + 0.5


def _ffn_up_fwd(h, w_gu, lb):
    t, d = h.shape
    n = w_gu.shape[-1]
    half = N_DEV // 2
    bt = _row_block(t, BIG_ROW_BLOCK)

    def body(h_ref, wg_ref, wu_ref, gu_ref, a_ref):
        hv = h_ref[...]
        g = jnp.dot(hv, wg_ref[...], preferred_element_type=F32)
        u = jnp.dot(hv, wu_ref[...], preferred_element_type=F32)
        gu_ref[0] = g
        gu_ref[1] = u
        a_ref[...] = (g * _sigmoid(g) * u).astype(BF16)

    return pl.pallas_call(
        body, name="ffn_up_fwd", grid=(half, t // bt),
        in_specs=[pl.BlockSpec((bt, d), lambda j, i: (i, 0)),
                  pl.BlockSpec((None, None, d, n), lambda j, i: (j, lb, 0, 0)),
                  pl.BlockSpec((None, None, d, n), lambda j, i: (half + j, lb, 0, 0))],
        out_specs=[pl.BlockSpec((2, None, bt, n), lambda j, i: (0, j, i, 0)),
                   pl.BlockSpec((None, bt, n), lambda j, i: (j, i, 0))],
        out_shape=[_sds((2, half, t, n), F32), _sds((half, t, n), BF16)],
        compiler_params=_params(("parallel", "parallel")),
    )(h, w_gu, w_gu)


def _ffn_down_bwd_a(dy, w_down, lb, gu):
    t, d = dy.shape
    _, half, _, n = gu.shape
    r = w_down.shape[2]
    bt = _row_block(t, BIG_ROW_BLOCK)

    def body(dy_ref, w_ref, gu_ref, d_ref):
        da = lax.dot_general(dy_ref[...], w_ref[...].reshape(2 * r, d), (((1,), (1,)), ((), ())),
                             preferred_element_type=F32)
        g = gu_ref[0]
        u = gu_ref[1]
        sig = _sigmoid(g)
        das = da * sig
        d_ref[0] = (das * u * (1.0 + g * (1.0 - sig))).astype(BF16)
        d_ref[1] = (das * g).astype(BF16)

    return pl.pallas_call(
        body, name="ffn_down_bwd_a", grid=(half, t // bt),
        in_specs=[pl.BlockSpec((bt, d), lambda j, i: (i, 0)),
                  pl.BlockSpec((2, None, r, d), lambda j, i: (j, lb, 0, 0)),
                  pl.BlockSpec((2, None, bt, n), lambda j, i: (0, j, i, 0))],
        out_specs=pl.BlockSpec((2, None, bt, n), lambda j, i: (0, j, i, 0)),
        out_shape=_sds((2, half, t, n), BF16),
        compiler_params=_params(("parallel", "parallel")),
    )(dy, w_down, gu)


_INV_SQRT2 = 0.7071067811865476
_INV_SQRT_2PI = 0.3989422804014327


def _gelu(z):
    return 0.5 * z * (1.0 + lax.erf(z * _INV_SQRT2))


def _gelu_grad(z):
    return 0.5 * (1.0 + lax.erf(z * _INV_SQRT2)) + z * jnp.exp(-0.5 * z * z) * _INV_SQRT_2PI


def _window_mask():
    t_out = lax.broadcasted_iota(jnp.int32, (GMLP_WINDOW, GMLP_WINDOW), 0)
    s_in = lax.broadcasted_iota(jnp.int32, (GMLP_WINDOW, GMLP_WINDOW), 1)
    return (s_in // CHUNK) <= (t_out // CHUNK)


def _gmlp_recompute(z_ref, bin_ref, lng_ref, lnb_ref):
    half = N_DEV // 2
    z = z_ref[...] + bin_ref[...]
    ge = _gelu(z)
    u = ge[:half]
    v = ge[half:]
    width = half * v.shape[-1]
    mu = jnp.sum(jnp.sum(v, axis=0), axis=-1, keepdims=True) / width
    vc = v - mu
    var = jnp.sum(jnp.sum(vc * vc, axis=0), axis=-1, keepdims=True) / width
    rstd = lax.rsqrt(var + LN_EPS)
    xhat = vc * rstd
    vn = xhat * lng_ref[...] + lnb_ref[...]
    return z, u, xhat, rstd, vn


def _gmlp_mid_fwd(zpre, b_in, ln_g, ln_b, w_s, b_s):
    _, t, n = zpre.shape
    half = N_DEV // 2
    gd = half * n // GMLP_GROUPS
    per = n // gd
    w = GMLP_WINDOW

    def body(z_ref, bin_ref, lng_ref, lnb_ref, ws_ref, bs_ref, o_ref):
        _, u, _, _, vn = _gmlp_recompute(z_ref, bin_ref, lng_ref, lnb_ref)
        mask = _window_mask()
        for g in range(GMLP_GROUPS):
            sh, c0 = g // per, (g % per) * gd
            wsm = jnp.where(mask, ws_ref[g], 0.0).astype(BF16)
            s = jnp.dot(wsm, vn[sh][:, c0:c0 + gd].astype(BF16), preferred_element_type=F32) + bs_ref[g]
            o_ref[sh, :, c0:c0 + gd] = (u[sh][:, c0:c0 + gd] * s).astype(BF16)

    whole = lambda a: pl.BlockSpec(tuple(a.shape), lambda i: (0,) * a.ndim)
    return pl.pallas_call(
        body, name="gmlp_mid_fwd", grid=(t // w,),
        in_specs=[pl.BlockSpec((N_DEV, w, n), lambda i: (0, i, 0)),
                  whole(b_in), whole(ln_g), whole(ln_b), whole(w_s), whole(b_s)],
        out_specs=pl.BlockSpec((half, w, n), lambda i: (0, i, 0)),
        out_shape=_sds((half, t, n), BF16),
        compiler_params=_params(("parallel",)),
    )(zpre, b_in, ln_g, ln_b, w_s, b_s)


def _gmlp_mid_bwd(zpre, dgated, b_in, ln_g, ln_b, w_s, b_s):
    _, t, n = zpre.shape
    half = N_DEV // 2
    gd = half * n // GMLP_GROUPS
    per = n // gd
    w = GMLP_WINDOW
    width = half * n

    def body(z_ref, dg_ref, bin_ref, lng_ref, lnb_ref, ws_ref, bs_ref,
             dz_ref, dws_ref, dbs_ref, dlng_ref, dlnb_ref, dbin_ref, du_ref, dvn_ref):
        @pl.when(pl.program_id(0) == 0)
        def _():
            for r in (dws_ref, dbs_ref, dlng_ref, dlnb_ref, dbin_ref):
                r[...] = jnp.zeros_like(r)

        z, u, xhat, rstd, vn = _gmlp_recompute(z_ref, bin_ref, lng_ref, lnb_ref)
        mask = _window_mask()
        for g in range(GMLP_GROUPS):
            sh, c0 = g // per, (g % per) * gd
            wsm = jnp.where(mask, ws_ref[g], 0.0).astype(BF16)
            vg = vn[sh][:, c0:c0 + gd].astype(BF16)
            s = jnp.dot(wsm, vg, preferred_element_type=F32) + bs_ref[g]
            dgt = dg_ref[sh, :, c0:c0 + gd]
            ds = dgt * u[sh][:, c0:c0 + gd]
            du_ref[sh, :, c0:c0 + gd] = dgt * s
            dsb = ds.astype(BF16)
            dws = lax.dot_general(dsb, vg, (((1,), (1,)), ((), ())), preferred_element_type=F32)
            dws_ref[g] += jnp.where(mask, dws, 0.0)
            dbs_ref[g] += jnp.sum(ds, axis=-1, keepdims=True)
            dvn_ref[sh, :, c0:c0 + gd] = lax.dot_general(wsm, dsb, (((0,), (0,)), ((), ())),
                                                         preferred_element_type=F32)
        dvn = dvn_ref[...]
        dlng_ref[...] += jnp.sum(dvn * xhat, axis=1, keepdims=True)
        dlnb_ref[...] += jnp.sum(dvn, axis=1, keepdims=True)
        dxh = dvn * lng_ref[...]
        m1 = jnp.sum(jnp.sum(dxh, axis=0), axis=-1, keepdims=True) / width
        m2 = jnp.sum(jnp.sum(dxh * xhat, axis=0), axis=-1, keepdims=True) / width
        dv = rstd * (dxh - m1 - xhat * m2)
        gg = _gelu_grad(z)
        dzu = du_ref[...] * gg[:half]
        dzv = dv * gg[half:]
        dz_ref[:half] = dzu.astype(BF16)
        dz_ref[half:] = dzv.astype(BF16)
        dbin_ref[:half] += jnp.sum(dzu, axis=1, keepdims=True)
        dbin_ref[half:] += jnp.sum(dzv, axis=1, keepdims=True)

    whole = lambda a: pl.BlockSpec(tuple(a.shape), lambda i: (0,) * a.ndim)
    outs = [_sds((N_DEV, t, n), BF16), _sds(w_s.shape, F32), _sds(b_s.shape, F32),
            _sds(ln_g.shape, F32), _sds(ln_b.shape, F32), _sds(b_in.shape, F32)]
    return pl.pallas_call(
        body, name="gmlp_mid_bwd", grid=(t // w,),
        in_specs=[pl.BlockSpec((N_DEV, w, n), lambda i: (0, i, 0)),
                  pl.BlockSpec((half, w, n), lambda i: (0, i, 0)),
                  whole(b_in), whole(ln_g), whole(ln_b), whole(w_s), whole(b_s)],
        out_specs=[pl.BlockSpec((N_DEV, w, n), lambda i: (0, i, 0))] + [whole(o) for o in outs[1:]],
        out_shape=outs,
        scratch_shapes=[pltpu.VMEM((half, w, n), F32), pltpu.VMEM((half, w, n), F32)],
        compiler_params=_params(("arbitrary",)),
    )(zpre, dgated, b_in, ln_g, ln_b, w_s, b_s)


ATTN_CHUNKS = 4
ATTN_ROWS = ATTN_CHUNKS * CHUNK
ATTN_WINDOW = ATTN_ROWS + LEFT_PAD
ATTN_DIAGS = 1024
ATTN_ROLL = ATTN_DIAGS - (ATTN_ROWS - 1)


def _rel_vector(rel):
    j = np.arange(ATTN_DIAGS)
    idx = np.clip(ATTN_WINDOW - 1 - j, -(CHUNK - 1), MAX_REL) + (CHUNK - 1)
    return rel[:, idx]


def _attn_bias_mask(rel_ref, bm_ref):
    tt = lax.broadcasted_iota(jnp.int32, (ATTN_ROWS, ATTN_WINDOW), 0) // CHUNK
    rr = lax.broadcasted_iota(jnp.int32, (ATTN_ROWS, ATTN_WINDOW), 1) // CHUNK
    band = (rr >= tt) & (rr <= tt + LEFT_CHUNKS)
    for j in range(bm_ref.shape[0]):
        vec = jnp.broadcast_to(rel_ref[j:j + 1, :], (ATTN_ROWS, ATTN_DIAGS))
        toeplitz = pltpu.roll(vec, ATTN_ROLL, 1, stride=1, stride_axis=0)[:, :ATTN_WINDOW]
        bm_ref[j] = jnp.where(band, toeplitz, -jnp.inf)


def _attn_probs(q_ref, k_ref, bm_ref, j, hd, start, valid):
    qh = q_ref[:, j * hd:(j + 1) * hd]
    kb = k_ref[pl.ds(start, ATTN_WINDOW), j * hd:(j + 1) * hd]
    sc = lax.dot_general(qh, kb, (((1,), (1,)), ((), ())), preferred_element_type=F32)
    sc = sc * (hd ** -0.5) + bm_ref[j]
    sc = jnp.where(valid, sc, -jnp.inf)
    sc = sc - jnp.max(sc, axis=-1, keepdims=True)
    e = jnp.exp(sc)
    return e / jnp.sum(e, axis=-1, keepdims=True), qh, kb


def _window_valid(start):
    r = lax.broadcasted_iota(jnp.int32, (1, ATTN_WINDOW), 1)
    return (start + r) >= LEFT_PAD


def _attn_fwd(q, kvp, rel_vec):
    t, d = q.shape
    hd = d // N_HEADS
    half = N_DEV // 2
    n = kvp.shape[-1]
    per = n // hd
    rows = kvp.shape[1]

    def body(q_ref, k_ref, v_ref, rel_ref, o_ref, bm_ref):
        @pl.when(pl.program_id(1) == 0)
        def _():
            _attn_bias_mask(rel_ref, bm_ref)

        start = pl.multiple_of(pl.program_id(1) * ATTN_ROWS, ATTN_ROWS)
        valid = _window_valid(start)
        for j in range(per):
            p, _, _ = _attn_probs(q_ref, k_ref, bm_ref, j, hd, start, valid)
            vb = v_ref[pl.ds(start, ATTN_WINDOW), j * hd:(j + 1) * hd]
            o_ref[:, j * hd:(j + 1) * hd] = jnp.dot(p.astype(BF16), vb, preferred_element_type=F32).astype(BF16)

    return pl.pallas_call(
        body, name="attn_fwd", grid=(half, t // ATTN_ROWS),
        in_specs=[pl.BlockSpec((ATTN_ROWS, n), lambda g, i: (i, g)),
                  pl.BlockSpec((None, rows, n), lambda g, i: (g, 0, 0)),
                  pl.BlockSpec((None, rows, n), lambda g, i: (half + g, 0, 0)),
                  pl.BlockSpec((None, per, ATTN_DIAGS), lambda g, i: (g, 0, 0))],
        out_specs=pl.BlockSpec((ATTN_ROWS, n), lambda g, i: (i, g)),
        out_shape=_sds((t, d), BF16),
        scratch_shapes=[pltpu.VMEM((per, ATTN_ROWS, ATTN_WINDOW), F32)],
        compiler_params=_params(("arbitrary", "arbitrary")),
    )(q, kvp, kvp, rel_vec.reshape(half, per, ATTN_DIAGS))


def _attn_bwd(q, dout, kvp, rel_vec, dk_in=None, dv_in=None):
    t, d = q.shape
    hd = d // N_HEADS
    half = N_DEV // 2
    n = kvp.shape[-1]
    per = n // hd
    rows = kvp.shape[1]
    scale = hd ** -0.5
    carry = dk_in is not None

    def body(q_ref, do_ref, k_ref, v_ref, rel_ref, *rest):
        dq_ref, dk_ref, dv_ref, dsc_ref, bm_ref = rest[-5:]

        @pl.when(pl.program_id(1) == 0)
        def _():
            _attn_bias_mask(rel_ref, bm_ref)
            dk_ref[...] = rest[0][...] if carry else jnp.zeros_like(dk_ref)
            dv_ref[...] = rest[1][...] if carry else jnp.zeros_like(dv_ref)
            dsc_ref[...] = jnp.zeros_like(dsc_ref)

        start = pl.multiple_of(pl.program_id(1) * ATTN_ROWS, ATTN_ROWS)
        valid = _window_valid(start)
        for j in range(per):
            cols = slice(j * hd, (j + 1) * hd)
            p, qh, kb = _attn_probs(q_ref, k_ref, bm_ref, j, hd, start, valid)
            vb = v_ref[pl.ds(start, ATTN_WINDOW), cols]
            doh = do_ref[:, cols]
            dp = lax.dot_general(doh, vb, (((1,), (1,)), ((), ())), preferred_element_type=F32)
            ds = p * (dp - jnp.sum(dp * p, axis=-1, keepdims=True))
            dsc_ref[j] += sum(ds[a * CHUNK:(a + 1) * CHUNK, a * CHUNK:a * CHUNK + BAND]
                              for a in range(ATTN_CHUNKS))
            dsb = (ds * scale).astype(BF16)
            dq_ref[:, cols] = jnp.dot(dsb, kb, preferred_element_type=F32).astype(BF16)
            dk_ref[pl.ds(start, ATTN_WINDOW), cols] += lax.dot_general(
                dsb, qh, (((0,), (0,)), ((), ())), preferred_element_type=F32)
            dv_ref[pl.ds(start, ATTN_WINDOW), cols] += lax.dot_general(
                p.astype(BF16), doh, (((0,), (0,)), ((), ())), preferred_element_type=F32)

    tile = pl.BlockSpec((ATTN_ROWS, n), lambda g, i: (i, g))
    shard = pl.BlockSpec((None, rows, n), lambda g, i: (g, 0, 0))
    in_specs = [tile, tile, shard, pl.BlockSpec((None, rows, n), lambda g, i: (half + g, 0, 0)),
                pl.BlockSpec((None, per, ATTN_DIAGS), lambda g, i: (g, 0, 0))]
    operands = [q, dout, kvp, kvp, rel_vec.reshape(half, per, ATTN_DIAGS)]
    if carry:
        in_specs += [shard, shard]
        operands += [dk_in, dv_in]
    acc = _sds((half, rows, n), F32)
    return pl.pallas_call(
        body, name="attn_bwd", grid=(half, t // ATTN_ROWS),
        in_specs=in_specs,
        out_specs=[tile, shard, shard, pl.BlockSpec((per, CHUNK, BAND), lambda g, i: (g, 0, 0))],
        out_shape=[_sds((t, d), BF16), acc, acc, _sds((N_HEADS, CHUNK, BAND), F32)],
        scratch_shapes=[pltpu.VMEM((per, ATTN_ROWS, ATTN_WINDOW), F32)],
        compiler_params=_params(("arbitrary", "arbitrary")),
    )(*operands)


SKEW_PITCH = 640
SKEW = SKEW_PITCH + 1
SKEW_LANES = -(-SKEW // LANES) * LANES


def _skew_diagonals(dsc):
    h = dsc.shape[0]
    wide = jnp.pad(dsc, ((0, 0), (0, 0), (0, SKEW_PITCH - BAND))).reshape(h, CHUNK * SKEW_PITCH)
    wide = jnp.pad(wide, ((0, 0), (0, CHUNK))).reshape(h, CHUNK, SKEW)
    return jnp.pad(wide, ((0, 0), (0, 0), (0, SKEW_LANES - SKEW)))


def _rel_bias_grad(skewed):
    heads = skewed.shape[0]
    hb = SUBLANES

    def body(d_ref, o_ref):
        col = lax.broadcasted_iota(jnp.int32, (SKEW_LANES, N_REL), 0)
        bucket = lax.broadcasted_iota(jnp.int32, (SKEW_LANES, N_REL), 1)
        diag = jnp.where(col < BAND, col, col - SKEW)
        idx = jnp.clip(LEFT_PAD - diag, -(CHUNK - 1), MAX_REL) + (CHUNK - 1)
        oh = ((idx == bucket) & (col < SKEW)).astype(BF16)
        dv = jnp.sum(d_ref[...], axis=1)
        hi = dv.astype(BF16)
        rest = dv - hi.astype(F32)
        mid = rest.astype(BF16)
        lo = (rest - mid.astype(F32)).astype(BF16)
        acc = jnp.dot(hi, oh, preferred_element_type=F32)
        acc += jnp.dot(mid, oh, preferred_element_type=F32)
        acc += jnp.dot(lo, oh, preferred_element_type=F32)
        o_ref[...] = acc

    return pl.pallas_call(
        body, name="rel_bias_grad", grid=(heads // hb,),
        in_specs=[pl.BlockSpec((hb, CHUNK, SKEW_LANES), lambda i: (i, 0, 0))],
        out_specs=pl.BlockSpec((hb, N_REL), lambda i: (i, 0)),
        out_shape=_sds((heads, N_REL), F32),
        compiler_params=_params(("parallel",)),
    )(skewed)


def _sum_parts(parts):
    s_n, rows, c = parts.shape
    br = _row_block(rows, OPT_ROW_BLOCK)

    def body(p_ref, o_ref):
        acc = p_ref[0].astype(F32)
        for s in range(1, s_n):
            acc = acc + p_ref[s].astype(F32)
        o_ref[...] = acc

    return pl.pallas_call(
        body, name="sum_parts", grid=(rows // br,),
        in_specs=[pl.BlockSpec((s_n, br, c), lambda i: (0, i, 0))],
        out_specs=pl.BlockSpec((br, c), lambda i: (i, 0)),
        out_shape=_sds((rows, c), F32),
        compiler_params=_params(("parallel",)),
    )(parts)


def _adamw(own, own_idx, parts, w, m, v, row0=0, bufs=None, after=None):
    _, rows, c = own.shape
    s_n = 0 if parts is None else parts.shape[0]
    total = w.shape[0]
    br = _row_block(rows, OPT_ROW_BLOCK)
    assert row0 % br == 0 and (bufs is not None or (row0 == 0 and total == rows))
    b0 = row0 // br
    m_corr = 1.0 - ADAM_B1 ** ADAM_STEP
    v_corr = 1.0 - ADAM_B2 ** ADAM_STEP

    def body(idx_ref, own_ref, *refs):
        if s_n:
            p_ref, refs = refs[0], refs[1:]
        w_ref, m_ref, v_ref = refs[:3]
        g_ref, d_ref, nm_ref, nv_ref = refs[-4:]
        g = own_ref[...].astype(F32)
        for s in range(s_n):
            g = g + p_ref[s].astype(F32)
        nm = ADAM_B1 * m_ref[...] + (1.0 - ADAM_B1) * g
        nv = ADAM_B2 * v_ref[...] + (1.0 - ADAM_B2) * (g * g)
        g_ref[...] = g
        nm_ref[...] = nm
        nv_ref[...] = nv
        d_ref[...] = -ADAM_LR * ((nm / m_corr) / (jnp.sqrt(nv / v_corr) + ADAM_EPS) + ADAM_WD * w_ref[...])

    tile = pl.BlockSpec((br, c), lambda i, idx: (i + b0, 0))
    in_specs = [pl.BlockSpec((None, br, c), lambda i, idx: (idx[0], i, 0))]
    operands = [own_idx, own]
    if s_n:
        in_specs.append(pl.BlockSpec((s_n, br, c), lambda i, idx: (0, i, 0)))
        operands.append(parts)
    in_specs += [tile, tile, tile]
    operands += [w, m, v]
    aliases = {}
    if bufs is not None:
        aliases = {len(operands) + j: j for j in range(4)}
        in_specs += [_ANY] * 4
        operands += list(bufs)
    if after is not None:
        in_specs.append(_ANY)
        operands.append(after)
    out = _sds((total, c), F32)
    return pl.pallas_call(
        body, name="adamw",
        grid_spec=pltpu.PrefetchScalarGridSpec(
            num_scalar_prefetch=1, grid=(rows // br,), in_specs=in_specs,
            out_specs=[tile, tile, tile, tile]),
        out_shape=[out, out, out, out],
        input_output_aliases=aliases,
        compiler_params=_params(("parallel",)),
    )(*operands)


def _chip_sum(p, r1, core):
    half = N_DEV // 2
    c = p.shape[-1]
    rows = int(np.prod(p.shape[1:-1]))
    br = _row_block(rows, BIG_ROW_BLOCK)

    def body(core_ref, p_ref, r_ref, o_ref):
        o_ref[...] = (p_ref[...].astype(F32) + r_ref[...].astype(F32)).astype(BF16)

    out = pl.pallas_call(
        body, name="chip_sum",
        grid_spec=pltpu.PrefetchScalarGridSpec(
            num_scalar_prefetch=1, grid=(half, rows // br),
            in_specs=[pl.BlockSpec((None, None, br, c), lambda q, i, cr: (q, cr[0], i, 0)),
                      pl.BlockSpec((None, br, c), lambda q, i, cr: (q, i, 0))],
            out_specs=pl.BlockSpec((None, br, c), lambda q, i, cr: (q, i, 0))),
        out_shape=_sds((half, rows, c), BF16),
        compiler_params=_params(("parallel", "parallel")),
    )(core, p.reshape(half, 2, rows, c), r1.reshape(half, rows, c))
    return out.reshape((half,) + p.shape[1:])


def _position():
    return tuple(lax.axis_index(a) for a in MESH_AXES)


def _linear(px, py, pc):
    return 4 * px + 2 * py + pc


def _all_gather_small(v, after=()):
    rows, lanes = v.shape

    def body(x_ref, *rest):
        out_ref, send_sems, recv_sems, local_sem = rest[-4:]
        x, y, c = _position()
        me, sibling = (x, y, c), (x, y, 1 - c)
        chips = [(1 - x, y), (x, 1 - y), (1 - x, 1 - y)]

        def copy(k, block, to, src=None):
            dst = out_ref.at[_linear(*block)]
            return pltpu.make_async_remote_copy(
                src_ref=dst if src is None else src, dst_ref=dst,
                send_sem=send_sems.at[k], recv_sem=recv_sems.at[k],
                device_id=to, device_id_type=MESH_ID)

        mine = pltpu.make_async_copy(x_ref, out_ref.at[_linear(*me)], local_sem)
        mine.start()
        first = [copy(0, me, sibling, src=x_ref)]
        first += [copy(1 + j, me, (*chip, c), src=x_ref) for j, chip in enumerate(chips)]
        for cp in first:
            cp.start()
        passed = [copy(4 + j, (*chip, c), sibling) for j, chip in enumerate(chips)]
        for j, chip in enumerate(chips):
            copy(1 + j, (*chip, c), me).wait_recv()
            passed[j].start()
        copy(0, sibling, me).wait_recv()
        for j, chip in enumerate(chips):
            copy(4 + j, (*chip, 1 - c), me).wait_recv()
        for cp in first + passed:
            cp.wait_send()
        mine.wait()

    return pl.pallas_call(
        body, name="all_gather_small",
        out_shape=_sds((N_DEV, rows, lanes), v.dtype),
        in_specs=[_VMEM] + [_ANY] * len(after), out_specs=_VMEM,
        scratch_shapes=[pltpu.SemaphoreType.DMA((7,)), pltpu.SemaphoreType.DMA((7,)),
                        pltpu.SemaphoreType.DMA],
        compiler_params=pltpu.CompilerParams(vmem_limit_bytes=VMEM_LIMIT),
    )(v, *after)


_HBM = pl.BlockSpec(memory_space=pltpu.HBM)
_SEM = pl.BlockSpec(memory_space=pltpu.SEMAPHORE)
_EFFECT = pltpu.SideEffectType.DATAFLOW_SIDE_EFFECTING
_ALL_CHIPS = [(0, 0), (0, 1), (1, 0), (1, 1)]


def _other_chips(x, y):
    return [(1 - x, y), (x, 1 - y), (1 - x, 1 - y)]


def _in_hbm(a):
    return pltpu.with_memory_space_constraint(a, pltpu.HBM)


def _token():
    return _sds((SUBLANES, LANES), F32)


def _gather_ici_copy(ref, i, k, chip, c, block, send_sems, recv_sems):
    return pltpu.make_async_remote_copy(
        src_ref=ref.at[block], dst_ref=ref.at[block],
        send_sem=send_sems.at[3 * i + k], recv_sem=recv_sems.at[3 * i + k],
        device_id=(*chip, c), device_id_type=MESH_ID)


def _gather_ici_start(name, lands, after=None):
    n = len(lands)
    extra = [] if after is None else [after]

    def body(*refs):
        ins, send_sems, recv_sems, token = refs[:n], refs[-n - 3], refs[-n - 2], refs[-1]
        x, y, c = _position()
        me = _linear(x, y, c)
        for i in range(n):
            for k, chip in enumerate(_other_chips(x, y)):
                _gather_ici_copy(ins[i], i, k, chip, c, me, send_sems, recv_sems).start()
        token[...] = jnp.zeros_like(token)

    out = pl.pallas_call(
        body, name=name,
        out_shape=(pltpu.SemaphoreType.DMA((3 * n,)), pltpu.SemaphoreType.DMA((3 * n,)),
                   *[pltpu.HBM(a.shape, a.dtype) for a in lands], _token()),
        in_specs=[_HBM] * n + [_ANY] * len(extra), out_specs=(_SEM, _SEM, *[_HBM] * n, _VMEM),
        input_output_aliases={i: 2 + i for i in range(n)},
        compiler_params=pltpu.CompilerParams(has_side_effects=_EFFECT),
    )(*[_in_hbm(a) for a in lands], *extra)
    return out[0], out[1], list(out[2:2 + n]), out[-1]


def _gather_ici_wait(name, lands, send_sems, recv_sems, after):
    n = len(lands)

    def body(*refs):
        ins, ss, rs = refs[:n], refs[n], refs[n + 1]
        x, y, c = _position()
        me = _linear(x, y, c)
        for i in range(n):
            for k, chip in enumerate(_other_chips(x, y)):
                _gather_ici_copy(ins[i], i, k, chip, c, me, ss, rs).wait_send()
                _gather_ici_copy(ins[i], i, k, chip, c, _linear(*chip, c), ss, rs).wait_recv()

    out = pl.pallas_call(
        body, name=name,
        out_shape=[pltpu.HBM(a.shape, a.dtype) for a in lands],
        in_specs=[_HBM] * n + [_SEM, _SEM, _ANY], out_specs=[_HBM] * n,
        input_output_aliases={i: i for i in range(n)},
        compiler_params=pltpu.CompilerParams(has_side_effects=_EFFECT),
    )(*lands, send_sems, recv_sems, after)
    return list(out)


def _gather_d2d(lands):
    n = len(lands)

    def body(*refs):
        ins, outs, send_sems, recv_sems = refs[:n], refs[n:2 * n], refs[2 * n], refs[2 * n + 1]
        x, y, c = _position()

        def copy(i, q, core):
            block = _linear(*_ALL_CHIPS[q], core)
            return pltpu.make_async_remote_copy(
                src_ref=ins[i].at[block], dst_ref=outs[i].at[block],
                send_sem=send_sems.at[i, q], recv_sem=recv_sems.at[i, q],
                device_id=(x, y, 1 - c), device_id_type=MESH_ID)

        sent = [copy(i, q, c) for i in range(n) for q in range(len(_ALL_CHIPS))]
        for cp in sent:
            cp.start()
        for i in range(n):
            for q in range(len(_ALL_CHIPS)):
                copy(i, q, 1 - c).wait_recv()
        for cp in sent:
            cp.wait_send()

    return pl.pallas_call(
        body, name="gather_d2d",
        out_shape=[_sds(a.shape, a.dtype) for a in lands],
        in_specs=[_ANY] * n, out_specs=[_ANY] * n,
        input_output_aliases={i: i for i in range(n)},
        scratch_shapes=[pltpu.SemaphoreType.DMA((n, 4)), pltpu.SemaphoreType.DMA((n, 4))],
    )(*lands)


def _partials_d2d(parts):
    n = len(parts)
    half = N_DEV // 2

    def body(*refs):
        ins, outs, send_sems, recv_sems = refs[:n], refs[n:2 * n], refs[2 * n], refs[2 * n + 1]
        x, y, c = _position()

        def copy(i, q):
            return pltpu.make_async_remote_copy(
                src_ref=ins[i].at[_linear(*_ALL_CHIPS[q], 1 - c)], dst_ref=outs[i].at[q],
                send_sem=send_sems.at[i, q], recv_sem=recv_sems.at[i, q],
                device_id=(x, y, 1 - c), device_id_type=MESH_ID)

        sent = [copy(i, q) for i in range(n) for q in range(half)]
        for cp in sent:
            cp.start()
        for cp in sent:
            cp.wait_recv()
        for cp in sent:
            cp.wait_send()

    return pl.pallas_call(
        body, name="partials_d2d",
        out_shape=[_sds((half,) + p.shape[1:], p.dtype) for p in parts],
        in_specs=[_ANY] * n, out_specs=[_ANY] * n,
        scratch_shapes=[pltpu.SemaphoreType.DMA((n, half)), pltpu.SemaphoreType.DMA((n, half))],
    )(*parts)


def _partials_peers(x, y, c, direct):
    chips = _other_chips(x, y)
    if not direct:
        return [((*ch, c), 2 * ch[0] + ch[1]) for ch in chips]
    peers = [(x, y, 1 - c)] + [(*ch, c) for ch in chips] + [(*ch, 1 - c) for ch in chips]
    return [(p, _linear(*p)) for p in peers]


def _partials_copies(srcs, lands, send_sems, recv_sems, direct):
    x, y, c = _position()
    peers = _partials_peers(x, y, c, direct)
    return [pltpu.make_async_remote_copy(
        src_ref=srcs[i].at[block], dst_ref=lands[i].at[k],
        send_sem=send_sems.at[len(peers) * i + k], recv_sem=recv_sems.at[len(peers) * i + k],
        device_id=peer, device_id_type=MESH_ID)
        for i in range(len(srcs)) for k, (peer, block) in enumerate(peers)]


def _partials_send_start(name, srcs, lands, direct, after=None):
    n = len(srcs)
    n_sem = n * (N_DEV - 1 if direct else len(_ALL_CHIPS) - 1)

    def body(*refs):
        _, send_sems, recv_sems = refs[:2 * n], refs[-2 * n - 3], refs[-2 * n - 2]
        for cp in _partials_copies(refs[:n], refs[n:2 * n], send_sems, recv_sems, direct):
            cp.start()
        refs[-1][...] = jnp.zeros_like(refs[-1])

    both = list(srcs) + list(lands)
    extra = [] if after is None else [after]
    out = pl.pallas_call(
        body, name=name,
        out_shape=(pltpu.SemaphoreType.DMA((n_sem,)), pltpu.SemaphoreType.DMA((n_sem,)),
                   *[pltpu.HBM(a.shape, a.dtype) for a in both], _token()),
        in_specs=[_HBM] * (2 * n) + [_ANY] * len(extra), out_specs=(_SEM, _SEM, *[_HBM] * (2 * n), _VMEM),
        input_output_aliases={i: 2 + i for i in range(2 * n)},
        compiler_params=pltpu.CompilerParams(has_side_effects=_EFFECT),
    )(*[_in_hbm(a) for a in both], *extra)
    return out[0], out[1], list(out[2:2 + n]), list(out[2 + n:2 + 2 * n]), out[-1]


def _partials_send_wait(name, srcs, lands, send_sems, recv_sems, direct, after):
    n = len(srcs)

    def body(*refs):
        for cp in _partials_copies(refs[:n], refs[n:2 * n], refs[2 * n], refs[2 * n + 1], direct):
            cp.wait_send()
            cp.wait_recv()

    both = list(srcs) + list(lands)
    out = pl.pallas_call(
        body, name=name,
        out_shape=[pltpu.HBM(a.shape, a.dtype) for a in both],
        in_specs=[_HBM] * (2 * n) + [_SEM, _SEM, _ANY], out_specs=[_HBM] * (2 * n),
        input_output_aliases={i: i for i in range(2 * n)},
        compiler_params=pltpu.CompilerParams(has_side_effects=_EFFECT),
    )(*both, send_sems, recv_sems, after)
    return list(out[:n]), list(out[n:])


def _pack(arrs):
    flat = jnp.concatenate([a.reshape(-1).astype(F32) for a in arrs])
    block = OPT_ROW_BLOCK if flat.shape[0] > OPT_ROW_BLOCK * LANES else SUBLANES
    pad = (-flat.shape[0]) % (block * LANES)
    if pad:
        flat = jnp.concatenate([flat, jnp.zeros((pad,), F32)])
    return flat.reshape(-1, LANES)


def _unpack(packed, shapes, lead=()):
    flat = packed.reshape(lead + (-1,))
    out, off = [], 0
    for s in shapes:
        size = int(np.prod(s))
        out.append(flat[..., off:off + size].reshape(lead + tuple(s)))
        off += size
    return out


def _unshard_last(g):
    nd = g.ndim
    perm = tuple(range(1, nd - 1)) + (0, nd - 1)
    t = jnp.transpose(g, perm)
    return t.reshape(t.shape[:-2] + (N_DEV * g.shape[-1],))


def kernel(x, c, w_ada, b_ada, ln_g, ln_b, ffn_gu, ffn_down, gmlp_w_in, gmlp_b_in, gmlp_ln_g, gmlp_ln_b, gmlp_w_s, gmlp_b_s, gmlp_w_out, w_ada_kv, b_ada_kv, w_kv, attn_w_q, attn_rel_bias, attn_w_o, loss_target, m_w_ada, m_b_ada, m_ln_g, m_ln_b, m_ffn_gu, m_ffn_down, m_gmlp_w_in, m_gmlp_b_in, m_gmlp_ln_g, m_gmlp_ln_b, m_gmlp_w_s, m_gmlp_b_s, m_gmlp_w_out, m_w_ada_kv, m_b_ada_kv, m_w_kv, m_attn_w_q, m_attn_rel_bias, m_attn_w_o, v_w_ada, v_b_ada, v_ln_g, v_ln_b, v_ffn_gu, v_ffn_down, v_gmlp_w_in, v_gmlp_b_in, v_gmlp_ln_g, v_gmlp_ln_b, v_gmlp_w_s, v_gmlp_b_s, v_gmlp_w_out, v_w_ada_kv, v_b_ada_kv, v_w_kv, v_attn_w_q, v_attn_rel_bias, v_attn_w_o):
    weights = dict(w_ada=w_ada, b_ada=b_ada, ln_g=ln_g, ln_b=ln_b, ffn_gu=ffn_gu, ffn_down=ffn_down,
                   gmlp_w_in=gmlp_w_in, gmlp_b_in=gmlp_b_in, gmlp_ln_g=gmlp_ln_g, gmlp_ln_b=gmlp_ln_b,
                   gmlp_w_s=gmlp_w_s, gmlp_b_s=gmlp_b_s, gmlp_w_out=gmlp_w_out, w_ada_kv=w_ada_kv,
                   b_ada_kv=b_ada_kv, w_kv=w_kv, attn_w_q=attn_w_q, attn_rel_bias=attn_rel_bias,
                   attn_w_o=attn_w_o)
    mom1 = dict(w_ada=m_w_ada, b_ada=m_b_ada, ln_g=m_ln_g, ln_b=m_ln_b, ffn_gu=m_ffn_gu, ffn_down=m_ffn_down,
                gmlp_w_in=m_gmlp_w_in, gmlp_b_in=m_gmlp_b_in, gmlp_ln_g=m_gmlp_ln_g, gmlp_ln_b=m_gmlp_ln_b,
                gmlp_w_s=m_gmlp_w_s, gmlp_b_s=m_gmlp_b_s, gmlp_w_out=m_gmlp_w_out, w_ada_kv=m_w_ada_kv,
                b_ada_kv=m_b_ada_kv, w_kv=m_w_kv, attn_w_q=m_attn_w_q, attn_rel_bias=m_attn_rel_bias,
                attn_w_o=m_attn_w_o)
    mom2 = dict(w_ada=v_w_ada, b_ada=v_b_ada, ln_g=v_ln_g, ln_b=v_ln_b, ffn_gu=v_ffn_gu, ffn_down=v_ffn_down,
                gmlp_w_in=v_gmlp_w_in, gmlp_b_in=v_gmlp_b_in, gmlp_ln_g=v_gmlp_ln_g, gmlp_ln_b=v_gmlp_ln_b,
                gmlp_w_s=v_gmlp_w_s, gmlp_b_s=v_gmlp_b_s, gmlp_w_out=v_gmlp_w_out, w_ada_kv=v_w_ada_kv,
                b_ada_kv=v_b_ada_kv, w_kv=v_w_kv, attn_w_q=v_attn_w_q, attn_rel_bias=v_attn_rel_bias,
                attn_w_o=v_attn_w_o)
    order = list(weights)

    x = x[0]
    target = loss_target[0]
    t, d = x.shape
    n_mod = w_ada.shape[-1] * N_DEV // d
    mod_w = w_ada.shape[-1]
    kv_w = w_ada_kv.shape[-1]
    n_b = DEPTH - N_A
    me = _linear(*_position())

    l2 = DEPTH * 2
    big = dict(
        ffn_gu=ffn_gu.reshape((l2,) + ffn_gu.shape[2:]),
        ffn_down=ffn_down.reshape((l2,) + ffn_down.shape[2:]),
        gmlp_w_in=gmlp_w_in, gmlp_w_out=gmlp_w_out, w_kv=w_kv[None],
        attn_w_q=attn_w_q, attn_w_o=attn_w_o)
    big_names = list(big)
    core = lax.axis_index("c").astype(jnp.int32).reshape(1)
    chip = (2 * lax.axis_index("x") + lax.axis_index("y")).astype(jnp.int32).reshape(1)

    fwd_groups = [
        {"ffn_gu": (0, 1), "ffn_down": (0, 1)},
        {"gmlp_w_in": (0, 1), "gmlp_w_out": (0, 1)},
        {"ffn_gu": (1, 1), "ffn_down": (1, 1)},
        {"ffn_gu": (2, 1), "ffn_down": (2, 1)},
        {"gmlp_w_in": (1, 1), "gmlp_w_out": (1, 1)},
        {"ffn_gu": (3, 1), "ffn_down": (3, 1), "w_kv": (0, 1)},
        {"ffn_gu": (4, 1), "ffn_down": (4, 1)},
        {"attn_w_q": (0, 1), "attn_w_o": (0, 1)},
        {"ffn_gu": (5, 1), "ffn_down": (5, 1)},
        {"ffn_gu": (6, 2), "ffn_down": (6, 2), "attn_w_q": (1, 1), "attn_w_o": (1, 1)},
    ]
    bwd_groups = []
    for l in range(DEPTH):
        g = {"ffn_gu": (2 * l, 2), "ffn_down": (2 * l, 2)}
        if l < N_A:
            g.update({"gmlp_w_in": (l, 1), "gmlp_w_out": (l, 1)})
        else:
            g.update({"attn_w_q": (l - N_A, 1), "attn_w_o": (l - N_A, 1)})
        if l == N_A - 1:
            g["w_kv"] = (0, 1)
        bwd_groups.append(g)

    def slot_of(groups, name, slot):
        for gi, g in enumerate(groups):
            if name in g and g[name][0] <= slot < g[name][0] + g[name][1]:
                return gi, slot - g[name][0]
        raise KeyError((name, slot))

    def start_group(gi, after=None):
        lands = []
        for name, (s0, cnt) in fwd_groups[gi].items():
            shard = big[name][s0:s0 + cnt].astype(BF16)
            land = lax.empty((N_DEV,) + shard.shape, BF16)
            lands.append(lax.dynamic_update_slice(land, shard[None], (me,) + (0,) * shard.ndim))
        return _gather_ici_start(f"gather_ici_start_{gi}", lands, after)

    flights = [start_group(0)]
    gathered = [None] * len(fwd_groups)

    def land_group(gi, after):
        send_sems, recv_sems, lands, _ = flights[gi]
        lands = _gather_ici_wait(f"gather_ici_wait_{gi}", lands, send_sems, recv_sems, after)
        gathered[gi] = dict(zip(fwd_groups[gi], _gather_d2d(lands)))

    def weight(name, slot):
        gi, local = slot_of(fwd_groups, name, slot)
        return gathered[gi][name], local

    swapped = ("ffn_gu",)

    def grad_shape(name):
        s = big[name].shape[1:]
        return s[:-2] + (s[-1], s[-2]) if name in swapped else s

    partial = [{name: lax.empty((N_DEV, cnt) + grad_shape(name), BF16) for name, (_, cnt) in g.items()}
               for g in bwd_groups]

    c_all = _all_gather_small(_pack([c]) + flights[0][3][0, 0])
    c_all = _unpack(c_all, [(d,)], lead=(N_DEV,))[0]
    c4 = _as4(c_all)
    mod_part = _matmul("ada_fwd", c4, w_ada[:, None], (DEPTH, 1, N_DEV, mod_w), F32, a_silu=True)
    kv_part = _matmul("ada_kv_fwd", c4, _as4(w_ada_kv), (1, 1, N_DEV, kv_w), F32, a_silu=True)
    small_shapes = [mod_part.shape, kv_part.shape, ln_g.shape, ln_b.shape, gmlp_b_in.shape,
                    gmlp_ln_g.shape, gmlp_ln_b.shape, attn_rel_bias.shape]
    small = _all_gather_small(_pack([mod_part, kv_part, ln_g, ln_b, gmlp_b_in, gmlp_ln_g, gmlp_ln_b,
                                     attn_rel_bias]))
    flights += [start_group(gi, after=small) for gi in range(1, len(fwd_groups))]
    start_token = sum(f[3][0, 0] for f in flights[1:])
    (mod_g, kvm_g, ln_g_g, ln_b_g, b_in_g, gln_g_g, gln_b_g, rel_g) = _unpack(small, small_shapes, lead=(N_DEV,))
    mod_mine = lax.dynamic_index_in_dim(mod_g[:, :, 0], me, axis=2, keepdims=False)
    mod = _unshard_last(mod_mine) + b_ada
    mod = mod.reshape(DEPTH, n_mod, 1, d)
    kvm_mine = lax.dynamic_index_in_dim(kvm_g[:, 0, 0], me, axis=1, keepdims=False)
    mkv = (_unshard_last(kvm_mine) + b_ada_kv).reshape(2, 1, d)
    ln_g_f = _unshard_last(ln_g_g)
    ln_b_f = _unshard_last(ln_b_g)
    half = N_DEV // 2
    b_in_f = jnp.transpose(b_in_g, (1, 0, 2))[:, :, None, :]
    gln_g_f = _unshard_last(gln_g_g).reshape(N_A, half, 1, -1)
    gln_b_f = _unshard_last(gln_b_g).reshape(N_A, half, 1, -1)
    rel_f = _unshard_last(rel_g)

    def shard_act(a):
        return a.reshape(a.shape[0], a.shape[2], a.shape[3])

    def grad_into(name, slot, mm):
        gi, local = slot_of(bwd_groups, name, slot)
        partial[gi][name] = mm(partial[gi][name], local)

    def ffn_fwd(h, lw):
        w_gu, l_gu = weight("ffn_gu", lw)
        w_dn, l_dn = weight("ffn_down", lw)
        gu, a = _ffn_up_fwd(h, w_gu, l_gu)
        y = _matmul("ffn_down_fwd", a[:, None], w_dn, (1, 1, t, d), F32, lb=l_dn, b_merge=2, reduce=True)
        return y[0, 0], (gu, a)

    def ffn_bwd(dy, h, saved, lw):
        gu, a = saved
        w_gu, l_gu = weight("ffn_gu", lw)
        w_dn, l_dn = weight("ffn_down", lw)
        dgu = _ffn_down_bwd_a(dy, w_dn, l_dn, gu).reshape((N_DEV,) + gu.shape[2:])
        grad_into("ffn_down", lw, lambda buf, lo: _matmul(
            "ffn_down_bwd_w", a[:, None], _as4(dy), buf.shape, BF16, ta=True, lo=lo, out_merge=2, out_buf=buf))
        dh = _matmul("ffn_gu_bwd_a", dgu[:, None], w_gu, (1, 1, t, d), F32, lb=l_gu, tb=True, reduce=True)
        grad_into("ffn_gu", lw, lambda buf, lo: _matmul(
            "ffn_gu_bwd_w", dgu[:, None], _as4(h), buf.shape, BF16, ta=True, lo=lo, out_buf=buf))
        return dh[0, 0], {}

    def gmlp_params(l):
        return (b_in_f[l], gln_g_f[l], gln_b_f[l], gmlp_w_s[l], gmlp_b_s[l][:, :, None])

    def gmlp_fwd(h, l):
        w_in, l_in = weight("gmlp_w_in", l)
        w_out, l_out = weight("gmlp_w_out", l)
        n = w_in.shape[-1]
        zpre = _matmul("gmlp_in_fwd", _as4(h), w_in, (N_DEV, 1, t, n), F32, lb=l_in)
        gated = _gmlp_mid_fwd(shard_act(zpre), *gmlp_params(l))
        y = _matmul("gmlp_out_fwd", gated[:, None], w_out, (1, 1, t, d), F32, lb=l_out, b_merge=2, reduce=True)
        return y[0, 0], (zpre, gated)

    def gmlp_bwd(dy, h, saved, l):
        zpre, gated = saved
        w_in, l_in = weight("gmlp_w_in", l)
        w_out, l_out = weight("gmlp_w_out", l)
        n = w_in.shape[-1]
        dgated = _matmul("gmlp_out_bwd_a", _as4(dy), w_out, (half, 1, t, n), F32, lb=l_out, b_merge=2, tb=True)
        grad_into("gmlp_w_out", l, lambda buf, lo: _matmul(
            "gmlp_out_bwd_w", gated[:, None], _as4(dy), buf.shape, BF16, ta=True, lo=lo, out_merge=2, out_buf=buf))
        dz, dws, dbs, dlng, dlnb, dbin = _gmlp_mid_bwd(shard_act(zpre), shard_act(dgated), *gmlp_params(l))
        dh = _matmul("gmlp_in_bwd_a", dz[:, None], w_in, (1, 1, t, d), F32, lb=l_in, tb=True, reduce=True)
        grad_into("gmlp_w_in", l, lambda buf, lo: _matmul(
            "gmlp_in_bwd_w", _as4(h), dz[:, None], buf.shape, BF16, ta=True, lo=lo, out_buf=buf))
        small_grads = dict(gmlp_w_s=dws, gmlp_b_s=dbs[:, :, 0], gmlp_ln_g=dlng.reshape(-1),
                           gmlp_ln_b=dlnb.reshape(-1), gmlp_b_in=dbin.reshape(-1))
        return dh[0, 0], small_grads

    def attn_fwd(h, j, kvp):
        rel_vec = _rel_vector(rel_f[j])
        w_q, l_q = weight("attn_w_q", j)
        w_o, l_o = weight("attn_w_o", j)
        q = _matmul("attn_q_fwd", _as4(h), w_q, (1, 1, t, d), BF16, lb=l_q, b_merge=N_DEV, reduce=True)[0, 0]
        o = _attn_fwd(q, kvp, rel_vec)
        y = _matmul("attn_o_fwd", _as4(o), w_o, (1, 1, t, d), F32, lb=l_o, b_merge=N_DEV, reduce=True)
        return y[0, 0], (q, o, rel_vec)

    def attn_bwd(dy, h, saved, j, kvp, dkv_acc):
        q, o, rel_vec = saved
        w_q, l_q = weight("attn_w_q", j)
        w_o, l_o = weight("attn_w_o", j)
        do = _matmul("attn_o_bwd_a", _as4(dy), w_o, (1, 1, t, d), BF16, lb=l_o, b_merge=N_DEV, tb=True)[0, 0]
        grad_into("attn_w_o", j, lambda buf, lo: _matmul(
            "attn_o_bwd_w", _as4(o), _as4(dy), buf.shape, BF16, ta=True, lo=lo, out_merge=N_DEV, out_buf=buf))
        dq, dk, dv, dsc = _attn_bwd(q, do, kvp, rel_vec, *dkv_acc)
        drel = _rel_bias_grad(_skew_diagonals(dsc))
        dh = _matmul("attn_q_bwd_a", _as4(dq), w_q, (1, 1, t, d), F32, lb=l_q, b_merge=N_DEV, tb=True)
        grad_into("attn_w_q", j, lambda buf, lo: _matmul(
            "attn_q_bwd_w", _as4(h), _as4(dq), buf.shape, BF16, ta=True, lo=lo, out_merge=N_DEV, out_buf=buf))
        return dh[0, 0], dict(attn_rel_bias=drel, dkv=(dk, dv))

    tape = []
    kvp = None
    kv_tape = None
    first_use = {(0, 0): 0, (0, 1): 1, (0, 2): 2, (1, 0): 3, (1, 1): 4, (1, 2): 5, (2, 0): 6, (2, 1): 7,
                 (2, 2): 8, (3, 0): 9}
    h = _modulate(x, mod[0, 1], mod[0, 0] + start_token)
    for l in range(DEPTH):
        for i in range(3):
            if (l, i) in first_use:
                land_group(first_use[l, i], x)
            scl, gate = mod[l, 3 * i + 1], mod[l, 3 * i + 2]
            wgt = 1.0 if i == 1 else 0.5
            gw = wgt * (1.0 + gate)
            if i != 1:
                y, saved = ffn_fwd(h, 2 * l + i // 2)
            elif l < N_A:
                y, saved = gmlp_fwd(h, l)
            else:
                y, saved = attn_fwd(h, l - N_A, kvp)
            nl, ni = (l, i + 1) if i < 2 else (l + 1, 0)
            readers = [(mod[nl, 3 * ni + 1], mod[nl, 3 * ni])] if nl < DEPTH else []
            shared_kv = (l, i) == (N_A - 1, 2)
            if shared_kv:
                readers.append((mkv[1], mkv[0]))
            outs = _ln_res_fwd(x, y, gw, ln_g_f[l, i][None], ln_b_f[l, i][None], readers)
            tape.append((x, h, y, gw, scl, saved))
            x = outs[0]
            h = outs[1] if nl < DEPTH else None
            if shared_kv:
                hkv = outs[-1]
                w_kvg, l_kv = weight("w_kv", 0)
                n = w_kvg.shape[-1]
                kv = _matmul("kv_fwd", _as4(hkv), w_kvg, (N_DEV, 1, t, n), BF16, lb=l_kv)
                kvp = jnp.pad(shard_act(kv), ((0, 0), (LEFT_PAD, 0), (0, 0)))
                kv_tape = hkv

    loss_part, dx = _loss_head(x, target)
    loss = lax.psum(loss_part[0, 0], MESH_AXES)

    d_mod = [[None] * n_mod for _ in range(DEPTH)]
    d_ln_g = [[None] * 3 for _ in range(DEPTH)]
    d_ln_b = [[None] * 3 for _ in range(DEPTH)]
    small_grads = {k: [None] * N_A for k in ("gmlp_w_s", "gmlp_b_s", "gmlp_ln_g", "gmlp_ln_b", "gmlp_b_in")}
    d_rel = [None] * n_b
    dkv_acc = ()
    d_mkv = None
    reductions = [None] * DEPTH
    sent_token = None
    readers = []
    for l in reversed(range(DEPTH)):
        if l == N_A - 1:
            hkv = kv_tape
            w_kvg, l_kv = weight("w_kv", 0)
            dkv = jnp.concatenate(dkv_acc)[:, LEFT_PAD:, :].astype(BF16)[:, None]
            dhkv = _matmul("kv_bwd_a", dkv, w_kvg, (1, 1, t, d), F32, lb=l_kv, tb=True, reduce=True)[0, 0]
            grad_into("w_kv", 0, lambda buf, lo: _matmul(
                "kv_bwd_w", _as4(hkv), dkv, buf.shape, BF16, ta=True, lo=lo, out_buf=buf))
            readers.append((dhkv, mkv[1], None))
        for i in reversed(range(3)):
            x_in, h, y, gw, scl, saved = tape[3 * l + i]
            wgt = 1.0 if i == 1 else 0.5
            if sent_token is not None:
                gw = gw + sent_token
                sent_token = None
            res = _ln_res_bwd(x_in, y, gw, ln_g_f[l, i][None], ln_b_f[l, i][None], dx,
                              [(r[0], r[1]) for r in readers])
            dx_res, dy, dgw, dg, db = res[:5]
            for k, (_, _, slot) in enumerate(readers):
                dscl_k, dshift_k = res[5 + 2 * k][0], res[6 + 2 * k][0]
                if slot is None:
                    d_mkv = jnp.concatenate([dshift_k, dscl_k])
                else:
                    d_mod[slot[0]][slot[1]], d_mod[slot[0]][slot[1] + 1] = dshift_k, dscl_k
            d_ln_g[l][i], d_ln_b[l][i] = dg[0], db[0]
            if i != 1:
                dh, extra = ffn_bwd(dy, h, saved, 2 * l + i // 2)
            elif l < N_A:
                dh, extra = gmlp_bwd(dy, h, saved, l)
                for k, g in extra.items():
                    small_grads[k][l] = g
            else:
                dh, extra = attn_bwd(dy, h, saved, l - N_A, kvp, dkv_acc)
                d_rel[l - N_A] = extra["attn_rel_bias"]
                dkv_acc = extra["dkv"]
            d_mod[l][3 * i + 2] = wgt * dgw[0]
            dx = dx_res
            readers = [(dh, scl, (l, 3 * i))]
        if l > 0:
            srcs = [partial[l][k] for k in bwd_groups[l]]
            lands = [lax.empty((N_DEV - 1,) + s.shape[1:], BF16) for s in srcs]
            reductions[l] = _partials_send_start(f"partials_send_start_{l}", srcs, lands, True)
            sent_token = reductions[l][4][0, 0]
    (dh, scl, _), = readers
    dx, dscl, dshift = _mod_bwd(dx, dh, tape[0][0], scl)
    d_mod[0][0], d_mod[0][1] = dshift[0], dscl[0]
    grad_x = dx[None]

    d_mod_arr = jnp.stack([jnp.concatenate(r) for r in d_mod])
    small_part = dict(
        b_ada=d_mod_arr, b_ada_kv=d_mkv,
        ln_g=jnp.stack([jnp.stack(r) for r in d_ln_g]), ln_b=jnp.stack([jnp.stack(r) for r in d_ln_b]),
        gmlp_b_in=jnp.stack(small_grads["gmlp_b_in"]), gmlp_ln_g=jnp.stack(small_grads["gmlp_ln_g"]),
        gmlp_ln_b=jnp.stack(small_grads["gmlp_ln_b"]), gmlp_w_s=jnp.stack(small_grads["gmlp_w_s"]),
        gmlp_b_s=jnp.stack(small_grads["gmlp_b_s"]), attn_rel_bias=jnp.stack(d_rel))
    small_names = list(small_part)
    sp_shapes = [small_part[k].shape for k in small_names]
    sp_all = _all_gather_small(_pack([small_part[k] for k in small_names]),
                               after=[partial[0][k] for k in bwd_groups[0]])

    from_sibling = _partials_d2d([partial[0][k] for k in bwd_groups[0]])
    sums = [_chip_sum(partial[0][k], r1, core) for k, r1 in zip(bwd_groups[0], from_sibling)]
    lands = [lax.empty((len(_ALL_CHIPS) - 1,) + s.shape[1:], BF16) for s in sums]
    reductions[0] = _partials_send_start("partials_send_start_0", sums, lands, False, after=sp_all)
    sent_token = reductions[0][4][0, 0]
    c4 = c4 + sent_token

    sp_sum = _sum_parts(sp_all)
    full_grads = dict(zip(small_names, _unpack(sp_sum, sp_shapes)))
    per_dev = dict(zip(small_names, _unpack(sp_all, sp_shapes, lead=(N_DEV,))))

    def my_cols(a, width):
        return lax.dynamic_slice_in_dim(a, me * width, width, axis=a.ndim - 1)

    grads = {}
    grads["b_ada"] = full_grads["b_ada"]
    grads["b_ada_kv"] = full_grads["b_ada_kv"]
    grads["gmlp_w_s"] = full_grads["gmlp_w_s"]
    grads["gmlp_b_s"] = full_grads["gmlp_b_s"]
    for k in ("ln_g", "ln_b", "gmlp_b_in", "gmlp_ln_g", "gmlp_ln_b", "attn_rel_bias"):
        grads[k] = my_cols(full_grads[k], weights[k].shape[-1])

    dmod_cols = jnp.transpose(my_cols(per_dev["b_ada"], mod_w), (1, 0, 2))[:, None]
    grads["w_ada"] = _matmul("ada_bwd_w", c4, dmod_cols, (DEPTH, 1, d, mod_w), F32, ta=True,
                             a_silu=True)[:, 0]
    dkv_cols = my_cols(per_dev["b_ada_kv"], kv_w)[None, None]
    grads["w_ada_kv"] = _matmul("ada_kv_bwd_w", c4, dkv_cols, (1, 1, d, kv_w), F32, ta=True,
                                a_silu=True)[0, 0]

    delta, new_m, new_v = {}, {}, {}
    first = jnp.zeros((1,), jnp.int32)

    def flat2(a, cols):
        return a.reshape(-1, cols)

    done = None
    for k in ("w_ada", "w_ada_kv"):
        w = weights[k]
        cols = w.shape[-1]
        res = _adamw(grads[k].reshape(1, -1, cols), first, None, flat2(w, cols), flat2(mom1[k], cols),
                     flat2(mom2[k], cols), after=done)
        grads[k], delta[k], new_m[k], new_v[k] = (a.reshape(w.shape) for a in res)
        done = res[0][:SUBLANES, :LANES]

    tiny = [k for k in order if k not in delta and k not in big_names]
    tiny_shapes = [weights[k].shape for k in tiny]
    tiny_out = _adamw((_pack([grads[k] for k in tiny]) + sent_token)[None], first, None,
                      _pack([weights[k] for k in tiny]), _pack([mom1[k] for k in tiny]),
                      _pack([mom2[k] for k in tiny]), after=done)
    for dst, arr in zip((grads, delta, new_m, new_v), tiny_out):
        for k, val in zip(tiny, _unpack(arr, tiny_shapes)):
            dst[k] = val

    def opt_view(k, a):
        a = jnp.swapaxes(a, -1, -2) if k in swapped else a
        return a.reshape(-1, a.shape[-1])

    def opt_unview(k, a):
        s = weights[k].shape
        return jnp.swapaxes(a.reshape(s[:-2] + (s[-1], s[-2])), -1, -2) if k in swapped else a.reshape(s)

    bufs = {k: [lax.empty(opt_view(k, weights[k]).shape, F32) for _ in range(4)] for k in big_names}
    done = tiny_out[0]
    me_idx = me.astype(jnp.int32).reshape(1)
    for l in reversed(range(DEPTH)):
        send_sems, recv_sems, srcs, lands, _ = reductions[l]
        srcs, lands = _partials_send_wait(f"partials_send_wait_{l}", srcs, lands, send_sems, recv_sems, l > 0, done)
        for k, own, got in zip(bwd_groups[l], srcs, lands):
            cols = own.shape[-1]
            slot_rows = int(np.prod(own.shape[2:-1]))
            bufs[k] = _adamw(own.reshape(own.shape[0], -1, cols), me_idx if l > 0 else chip,
                             got.reshape(got.shape[0], -1, cols),
                             opt_view(k, weights[k]), opt_view(k, mom1[k]), opt_view(k, mom2[k]),
                             row0=bwd_groups[l][k][0] * slot_rows, bufs=bufs[k], after=done)
            done = bufs[k][0][:SUBLANES, :LANES]
    for k in big_names:
        grads[k], delta[k], new_m[k], new_v[k] = (opt_unview(k, b) for b in bufs[k])

    return (loss, grad_x, *[grads[k] for k in order], *[delta[k] for k in order],
            *[new_m[k] for k in order], *[new_v[k] for k in order])
```

```python
import functools

import numpy as np
import jax
import jax.numpy as jnp
from jax import lax
from jax.experimental import pallas as pl
from jax.experimental.pallas import tpu as pltpu

F32 = jnp.float32
BF16 = jnp.bfloat16
MESH_AXES = ("x", "y", "c")
N_DEV = 8
MESH_ID = pl.DeviceIdType.MESH

DEPTH = 4
N_A = 2
CHUNK = 64
N_HEADS = 16
LEFT_CHUNKS = 8
BAND = (LEFT_CHUNKS + 1) * CHUNK
LEFT_PAD = LEFT_CHUNKS * CHUNK
MAX_REL = 4 * CHUNK
N_REL = (CHUNK - 1) + MAX_REL + 1
GMLP_WINDOW = 128
GMLP_GROUPS = 8
ALPHA = (2.0 * DEPTH) ** 0.25
LN_EPS = 1e-5
ADAM_LR = 0.001
ADAM_B1 = 0.9
ADAM_B2 = 0.999
ADAM_EPS = 1e-08
ADAM_WD = 0.01
ADAM_STEP = 10

V7X_VMEM_BYTES = 64 * 1024 * 1024
VMEM_LIMIT = V7X_VMEM_BYTES - 8 * 1024 * 1024
LANES = 128
SUBLANES = 8
MM_BLOCK = 2048
BIG_ROW_BLOCK = 1024
ROW_BLOCK = 512
OPT_ROW_BLOCK = 256

_ANY = pl.BlockSpec(memory_space=pl.ANY)
_VMEM = pl.BlockSpec(memory_space=pltpu.VMEM)


def _params(sem=None):
    return pltpu.CompilerParams(dimension_semantics=sem, vmem_limit_bytes=VMEM_LIMIT)


def _row_block(rows, target):
    for d in range(min(rows, target), 0, -1):
        if rows % d == 0 and (d % SUBLANES == 0 or d == rows):
            return d
    return rows


def _matmul(name, a, b, out_shape4, out_dtype, *, la=0, lb=0, lo=0, ta=False, tb=False,
            reduce=False, b_merge=1, out_merge=1, out_buf=None, a_silu=False):
    ja_n, _, a_r, a_c = a.shape
    jb_n, _, b_r, b_c = b.shape
    jo_n, _, o_r, o_c = out_shape4
    m_tot = a_c if ta else a_r
    k_a = a_r if ta else a_c
    b_rows = b_merge * b_r
    k_c = b_c if tb else b_rows
    n = b_rows if tb else b_c
    n_chunks = (jb_n // b_merge) if reduce else 1
    natural_k = reduce and ja_n == 1
    assert n == o_c, (name, n, o_c)
    assert k_a ==(k_c * n_chunks if natural_k else k_c), (name, k_a, k_c, n_chunks)
    bk = k_c if (k_c <= MM_BLOCK or (b_merge > 1 and not tb)) else MM_BLOCK
    assert k_c % bk == 0
    nkk = k_c // bk
    kg = 2 if (reduce and ja_n > 1 and nkk == 1 and not ta and n_chunks % 2 == 0) else 1
    nk = n_chunks * nkk // kg
    m_out = out_merge * o_r
    assert m_tot == m_out, (name, m_tot, m_out)
    bm = m_tot if (m_tot <= MM_BLOCK or out_merge > 1) else MM_BLOCK
    assert m_tot % bm == 0
    jo_blocks = jo_n // out_merge

    def a_index(j, m, k):
        kj, kk = k // nkk, k % nkk
        ja = 0 if ja_n == 1 else (kj if reduce else j)
        ke = kk + kj * nkk if natural_k else kk
        return (ja, la, ke, m) if ta else (ja, la, m, ke)

    def b_index(j, m, k):
        kj, kk = k // nkk, k % nkk
        jb = 0 if jb_n == b_merge else (kj if reduce else j)
        return (jb, lb, 0, kk) if tb else (jb, lb, kk, 0)

    def o_index(j, m, k):
        return (j, lo, 0, 0) if out_merge > 1 else (j, lo, m, 0)

    a_block = (None, None, bk, bm) if ta else (None if kg == 1 else kg, None, bm, bk)
    if b_merge > 1:
        b_block = (kg * b_merge, None, b_r, bk if tb else n)
    else:
        b_block = (None if kg == 1 else kg, None) + ((n, bk) if tb else (bk, n))
    o_block = (out_merge, None, o_r, n) if out_merge > 1 else (None, None, bm, n)
    dims = (((0 if ta else 1,), (1 if tb else 0,)), ((), ()))

    in_place = nk > 1 and out_dtype == F32 and out_merge == 1
    use_acc = nk > 1 and not in_place

    def body(a_ref, b_ref, *rest):
        o_ref = rest[-2] if use_acc else rest[-1]
        k = pl.program_id(2)
        av = a_ref[...]
        if a_silu:
            af = av.astype(F32)
            av = af * jax.nn.sigmoid(af)
        bv = b_ref[...]
        if kg > 1:
            bv = bv.reshape(kg, -1, bv.shape[-1])
            prod = sum(lax.dot_general(av[g].astype(BF16), bv[g].astype(BF16), dims, preferred_element_type=F32)
                       for g in range(kg))
        else:
            if b_merge > 1:
                bv = bv.reshape(b_rows, bv.shape[-1])
            prod = lax.dot_general(av.astype(BF16), bv.astype(BF16), dims, preferred_element_type=F32)

        def emit(val):
            val = val.astype(out_dtype)
            o_ref[...] = val.reshape(out_merge, o_r, n) if out_merge > 1 else val

        if nk == 1:
            emit(prod)
            return
        acc_ref = o_ref if in_place else rest[-1]

        @pl.when(k == 0)
        def _():
            acc_ref[...] = prod

        @pl.when(k > 0)
        def _():
            acc_ref[...] += prod

        if use_acc:
            @pl.when(k == nk - 1)
            def _():
                emit(acc_ref[...])

    in_specs = [pl.BlockSpec(a_block, a_index), pl.BlockSpec(b_block, b_index)]
    operands = [a, b]
    aliases = {}
    if out_buf is not None:
        assert out_buf.shape == tuple(out_shape4) and out_buf.dtype == out_dtype
        in_specs.append(_ANY)
        operands.append(out_buf)
        aliases = {2: 0}
    return pl.pallas_call(
        body, name=name,
        grid=(jo_blocks, m_tot // bm, nk),
        in_specs=in_specs,
        out_specs=pl.BlockSpec(o_block, o_index),
        out_shape=jax.ShapeDtypeStruct(tuple(out_shape4), out_dtype),
        scratch_shapes=[pltpu.VMEM((bm, n), F32)] if use_acc else [],
        input_output_aliases=aliases,
        compiler_params=_params(("parallel", "parallel", "arbitrary")),
    )(*operands)


def _as4(a):
    return a.reshape((1,) * (4 - a.ndim) + a.shape)


def _row_call(name, body, ins, outs, t, *, acc_outs=()):
    bt = _row_block(t, ROW_BLOCK)

    def spec(arr, tiled):
        if tiled:
            return pl.BlockSpec((bt,) + tuple(arr.shape[1:]), lambda i: (i,) + (0,) * (arr.ndim - 1))
        return pl.BlockSpec(tuple(arr.shape), lambda i: (0,) * arr.ndim)

    return pl.pallas_call(
        body, name=name, grid=(t // bt,),
        in_specs=[spec(a, tl) for a, tl in ins],
        out_specs=[spec(o, tl) for o, tl in outs],
        out_shape=[jax.ShapeDtypeStruct(o.shape, o.dtype) for o, _ in outs],
        compiler_params=_params(("arbitrary",) if acc_outs else ("parallel",)),
    )(*[a for a, _ in ins])


def _sds(shape, dtype):
    return jax.ShapeDtypeStruct(tuple(shape), dtype)


def _modulate(x, scl, shift):
    t, d = x.shape

    def body(x_ref, s_ref, b_ref, h_ref):
        h_ref[...] = (x_ref[...] * (1.0 + s_ref[...]) + b_ref[...]).astype(BF16)

    return _row_call("modulate", body, [(x, True), (scl, False), (shift, False)],
                     [(_sds((t, d), BF16), True)], t)[0]


def _ln_stats(r):
    mu = jnp.mean(r, axis=-1, keepdims=True)
    rc = r - mu
    var = jnp.mean(rc * rc, axis=-1, keepdims=True)
    rstd = lax.rsqrt(var + LN_EPS)
    return rc * rstd, rstd


def _ln_res_fwd(x, y, gw, g, b, mods=()):
    t, d = x.shape
    n_mod = len(mods)

    def body(x_ref, y_ref, gw_ref, g_ref, b_ref, *rest):
        mod_refs, o_ref, h_refs = rest[:2 * n_mod], rest[2 * n_mod], rest[2 * n_mod + 1:]
        r = ALPHA * x_ref[...] + gw_ref[...] * y_ref[...]
        xhat, _ = _ln_stats(r)
        xn = xhat * g_ref[...] + b_ref[...]
        o_ref[...] = xn
        for k in range(n_mod):
            h_refs[k][...] = (xn * (1.0 + mod_refs[2 * k][...]) + mod_refs[2 * k + 1][...]).astype(BF16)

    vecs = [(v, False) for pair in mods for v in pair]
    return _row_call("ln_res_fwd", body,
                     [(x, True), (y, True), (gw, False), (g, False), (b, False)] + vecs,
                     [(_sds((t, d), F32), True)] + [(_sds((t, d), BF16), True)] * n_mod, t)


def _ln_res_bwd(x, y, gw, g, b, dx_base, pairs=()):
    t, d = x.shape
    n_pair = len(pairs)

    def body(x_ref, y_ref, gw_ref, g_ref, b_ref, dxb_ref, *rest):
        pair_refs, outs = rest[:2 * n_pair], rest[2 * n_pair:]
        dx_ref, dy_ref = outs[0], outs[1]
        sums = outs[2:]

        @pl.when(pl.program_id(0) == 0)
        def _():
            for r in sums:
                r[...] = jnp.zeros_like(r)

        yv = y_ref[...]
        gwv = gw_ref[...]
        gv = g_ref[...]
        xhat, rstd = _ln_stats(ALPHA * x_ref[...] + gwv * yv)
        dxn = dxb_ref[...]
        if n_pair:
            xn = xhat * gv + b_ref[...]
            for k in range(n_pair):
                dh = pair_refs[2 * k][...]
                dxn = dxn + dh * (1.0 + pair_refs[2 * k + 1][...])
                sums[3 + 2 * k][...] += jnp.sum(dh * xn, axis=0, keepdims=True)
                sums[4 + 2 * k][...] += jnp.sum(dh, axis=0, keepdims=True)
        dxh = dxn * gv
        m1 = jnp.mean(dxh, axis=-1, keepdims=True)
        m2 = jnp.mean(dxh * xhat, axis=-1, keepdims=True)
        dr = rstd * (dxh - m1 - xhat * m2)
        dx_ref[...] = ALPHA * dr
        dy_ref[...] = (gwv * dr).astype(BF16)
        sums[0][...] += jnp.sum(dr * yv, axis=0, keepdims=True)
        sums[1][...] += jnp.sum(dxn * xhat, axis=0, keepdims=True)
        sums[2][...] += jnp.sum(dxn, axis=0, keepdims=True)

    vec = _sds((1, d), F32)
    n_sum = 3 + 2 * n_pair
    ins = [(x, True), (y, True), (gw, False), (g, False), (b, False), (dx_base, True)]
    for dh, scl in pairs:
        ins += [(dh, True), (scl, False)]
    return _row_call("ln_res_bwd", body, ins,
                     [(_sds((t, d), F32), True), (_sds((t, d), BF16), True)] + [(vec, False)] * n_sum, t,
                     acc_outs=tuple(range(2, 2 + n_sum)))


def _mod_bwd(dx_res, dh, x, scl):
    t, d = x.shape

    def body(dxr_ref, dh_ref, x_ref, s_ref, dx_ref, ds_ref, db_ref):
        @pl.when(pl.program_id(0) == 0)
        def _():
            ds_ref[...] = jnp.zeros_like(ds_ref)
            db_ref[...] = jnp.zeros_like(db_ref)

        dh = dh_ref[...]
        dx_ref[...] = dxr_ref[...] + dh * (1.0 + s_ref[...])
        ds_ref[...] += jnp.sum(dh * x_ref[...], axis=0, keepdims=True)
        db_ref[...] += jnp.sum(dh, axis=0, keepdims=True)

    vec = _sds((1, d), F32)
    return _row_call("mod_bwd", body, [(dx_res, True), (dh, True), (x, True), (scl, False)],
                     [(_sds((t, d), F32), True), (vec, False), (vec, False)], t, acc_outs=(1, 2))


def _loss_head(y, target):
    t, d = y.shape

    def body(y_ref, t_ref, l_ref, dy_ref):
        @pl.when(pl.program_id(0) == 0)
        def _():
            l_ref[...] = jnp.zeros_like(l_ref)

        err = y_ref[...] - t_ref[...]
        dy_ref[...] = err * (1.0 / d)
        part = 0.5 * jnp.sum(jnp.mean(err * err, axis=-1, keepdims=True), axis=0, keepdims=True)
        l_ref[...] += jnp.broadcast_to(part, l_ref.shape)

    return _row_call("loss_head", body, [(y, True), (target, True)],
                     [(_sds((SUBLANES, LANES), F32), False), (_sds((t, d), F32), True)], t,
                     acc_outs=(0,))


def _sigmoid(x):
    return 0.5 * jnp.tanh(0.5 * x) + 0.5


def _ffn_up_fwd(h, w_gu, lb):
    t, d = h.shape
    n = w_gu.shape[-1]
    half = N_DEV // 2
    bt = _row_block(t, BIG_ROW_BLOCK)

    def body(h_ref, wg_ref, wu_ref, fac_ref, a_ref):
        hv = h_ref[...]
        g = jnp.dot(hv, wg_ref[...], preferred_element_type=F32)
        u = jnp.dot(hv, wu_ref[...], preferred_element_type=F32)
        sig = _sigmoid(g)
        silu = g * sig
        fac_ref[0] = u * (sig + silu * (1.0 - sig))
        fac_ref[1] = silu
        a_ref[...] = (silu * u).astype(BF16)

    return pl.pallas_call(
        body, name="ffn_up_fwd", grid=(half, t // bt),
        in_specs=[pl.BlockSpec((bt, d), lambda j, i: (i, 0)),
                  pl.BlockSpec((None, None, d, n), lambda j, i: (j, lb, 0, 0)),
                  pl.BlockSpec((None, None, d, n), lambda j, i: (half + j, lb, 0, 0))],
        out_specs=[pl.BlockSpec((2, None, bt, n), lambda j, i: (0, j, i, 0)),
                   pl.BlockSpec((None, bt, n), lambda j, i: (j, i, 0))],
        out_shape=[_sds((2, half, t, n), F32), _sds((half, t, n), BF16)],
        compiler_params=_params(("parallel", "parallel")),
    )(h, w_gu, w_gu)


def _ffn_down_bwd_a(dy, w_down, lb, fac):
    t, d = dy.shape
    _, half, _, n = fac.shape
    r = w_down.shape[2]
    bt = _row_block(t, BIG_ROW_BLOCK)

    def body(dy_ref, w_ref, fac_ref, d_ref):
        da = lax.dot_general(dy_ref[...], w_ref[...].reshape(2 * r, d), (((1,), (1,)), ((), ())),
                             preferred_element_type=F32)
        d_ref[0] = (da * fac_ref[0]).astype(BF16)
        d_ref[1] = (da * fac_ref[1]).astype(BF16)

    return pl.pallas_call(
        body, name="ffn_down_bwd_a", grid=(half, t // bt),
        in_specs=[pl.BlockSpec((bt, d), lambda j, i: (i, 0)),
                  pl.BlockSpec((2, None, r, d), lambda j, i: (j, lb, 0, 0)),
                  pl.BlockSpec((2, None, bt, n), lambda j, i: (0, j, i, 0))],
        out_specs=pl.BlockSpec((2, None, bt, n), lambda j, i: (0, j, i, 0)),
        out_shape=_sds((2, half, t, n), BF16),
        compiler_params=_params(("parallel", "parallel")),
    )(dy, w_down, fac)


_INV_SQRT2 = 0.7071067811865476
_INV_SQRT_2PI = 0.3989422804014327


def _gelu(z):
    return 0.5 * z * (1.0 + lax.erf(z * _INV_SQRT2))


def _gelu_grad(z):
    return 0.5 * (1.0 + lax.erf(z * _INV_SQRT2)) + z * jnp.exp(-0.5 * z * z) * _INV_SQRT_2PI


def _window_mask():
    t_out = lax.broadcasted_iota(jnp.int32, (GMLP_WINDOW, GMLP_WINDOW), 0)
    s_in = lax.broadcasted_iota(jnp.int32, (GMLP_WINDOW, GMLP_WINDOW), 1)
    return (s_in // CHUNK) <= (t_out // CHUNK)


def _gmlp_recompute(z_ref, bin_ref, lng_ref, lnb_ref):
    half = N_DEV // 2
    z = z_ref[...] + bin_ref[...]
    ge = _gelu(z)
    u = ge[:half]
    v = ge[half:]
    width = half * v.shape[-1]
    mu = jnp.sum(jnp.sum(v, axis=0), axis=-1, keepdims=True) / width
    vc = v - mu
    var = jnp.sum(jnp.sum(vc * vc, axis=0), axis=-1, keepdims=True) / width
    rstd = lax.rsqrt(var + LN_EPS)
    xhat = vc * rstd
    vn = xhat * lng_ref[...] + lnb_ref[...]
    return z, u, xhat, rstd, vn


def _gmlp_mid_fwd(zpre, b_in, ln_g, ln_b, w_s, b_s):
    _, t, n = zpre.shape
    half = N_DEV // 2
    gd = half * n // GMLP_GROUPS
    per = n // gd
    w = GMLP_WINDOW

    def body(z_ref, bin_ref, lng_ref, lnb_ref, ws_ref, bs_ref, o_ref):
        _, u, _, _, vn = _gmlp_recompute(z_ref, bin_ref, lng_ref, lnb_ref)
        mask = _window_mask()
        for g in range(GMLP_GROUPS):
            sh, c0 = g // per, (g % per) * gd
            wsm = jnp.where(mask, ws_ref[g], 0.0).astype(BF16)
            s = jnp.dot(wsm, vn[sh][:, c0:c0 + gd].astype(BF16), preferred_element_type=F32) + bs_ref[g]
            o_ref[sh, :, c0:c0 + gd] = (u[sh][:, c0:c0 + gd] * s).astype(BF16)

    whole = lambda a: pl.BlockSpec(tuple(a.shape), lambda i: (0,) * a.ndim)
    return pl.pallas_call(
        body, name="gmlp_mid_fwd", grid=(t // w,),
        in_specs=[pl.BlockSpec((N_DEV, w, n), lambda i: (0, i, 0)),
                  whole(b_in), whole(ln_g), whole(ln_b), whole(w_s), whole(b_s)],
        out_specs=pl.BlockSpec((half, w, n), lambda i: (0, i, 0)),
        out_shape=_sds((half, t, n), BF16),
        compiler_params=_params(("parallel",)),
    )(zpre, b_in, ln_g, ln_b, w_s, b_s)


def _gmlp_mid_bwd(zpre, dgated, b_in, ln_g, ln_b, w_s, b_s):
    _, t, n = zpre.shape
    half = N_DEV // 2
    gd = half * n // GMLP_GROUPS
    per = n // gd
    w = GMLP_WINDOW
    width = half * n

    def body(z_ref, dg_ref, bin_ref, lng_ref, lnb_ref, ws_ref, bs_ref,
             dz_ref, dws_ref, dbs_ref, dlng_ref, dlnb_ref, dbin_ref, du_ref, dvn_ref):
        @pl.when(pl.program_id(0) == 0)
        def _():
            for r in (dws_ref, dbs_ref, dlng_ref, dlnb_ref, dbin_ref):
                r[...] = jnp.zeros_like(r)

        z, u, xhat, rstd, vn = _gmlp_recompute(z_ref, bin_ref, lng_ref, lnb_ref)
        mask = _window_mask()
        for g in range(GMLP_GROUPS):
            sh, c0 = g // per, (g % per) * gd
            wsm = jnp.where(mask, ws_ref[g], 0.0).astype(BF16)
            vg = vn[sh][:, c0:c0 + gd].astype(BF16)
            s = jnp.dot(wsm, vg, preferred_element_type=F32) + bs_ref[g]
            dgt = dg_ref[sh, :, c0:c0 + gd]
            ds = dgt * u[sh][:, c0:c0 + gd]
            du_ref[sh, :, c0:c0 + gd] = dgt * s
            dsb = ds.astype(BF16)
            dws = lax.dot_general(dsb, vg, (((1,), (1,)), ((), ())), preferred_element_type=F32)
            dws_ref[g] += jnp.where(mask, dws, 0.0)
            dbs_ref[g] += jnp.sum(ds, axis=-1, keepdims=True)
            dvn_ref[sh, :, c0:c0 + gd] = lax.dot_general(wsm, dsb, (((0,), (0,)), ((), ())),
                                                         preferred_element_type=F32)
        dvn = dvn_ref[...]
        dlng_ref[...] += jnp.sum(dvn * xhat, axis=1, keepdims=True)
        dlnb_ref[...] += jnp.sum(dvn, axis=1, keepdims=True)
        dxh = dvn * lng_ref[...]
        m1 = jnp.sum(jnp.sum(dxh, axis=0), axis=-1, keepdims=True) / width
        m2 = jnp.sum(jnp.sum(dxh * xhat, axis=0), axis=-1, keepdims=True) / width
        dv = rstd * (dxh - m1 - xhat * m2)
        gg = _gelu_grad(z)
        dzu = du_ref[...] * gg[:half]
        dzv = dv * gg[half:]
        dz_ref[:half] = dzu.astype(BF16)
        dz_ref[half:] = dzv.astype(BF16)
        dbin_ref[:half] += jnp.sum(dzu, axis=1, keepdims=True)
        dbin_ref[half:] += jnp.sum(dzv, axis=1, keepdims=True)

    whole = lambda a: pl.BlockSpec(tuple(a.shape), lambda i: (0,) * a.ndim)
    outs = [_sds((N_DEV, t, n), BF16), _sds(w_s.shape, F32), _sds(b_s.shape, F32),
            _sds(ln_g.shape, F32), _sds(ln_b.shape, F32), _sds(b_in.shape, F32)]
    return pl.pallas_call(
        body, name="gmlp_mid_bwd", grid=(t // w,),
        in_specs=[pl.BlockSpec((N_DEV, w, n), lambda i: (0, i, 0)),
                  pl.BlockSpec((half, w, n), lambda i: (0, i, 0)),
                  whole(b_in), whole(ln_g), whole(ln_b), whole(w_s), whole(b_s)],
        out_specs=[pl.BlockSpec((N_DEV, w, n), lambda i: (0, i, 0))] + [whole(o) for o in outs[1:]],
        out_shape=outs,
        scratch_shapes=[pltpu.VMEM((half, w, n), F32), pltpu.VMEM((half, w, n), F32)],
        compiler_params=_params(("arbitrary",)),
    )(zpre, dgated, b_in, ln_g, ln_b, w_s, b_s)


ATTN_CHUNKS = 4
ATTN_ROWS = ATTN_CHUNKS * CHUNK
ATTN_WINDOW = ATTN_ROWS + LEFT_PAD
ATTN_DIAGS = 1024
ATTN_ROLL = ATTN_DIAGS - (ATTN_ROWS - 1)


def _rel_vector(rel):
    j = np.arange(ATTN_DIAGS)
    idx = np.clip(ATTN_WINDOW - 1 - j, -(CHUNK - 1), MAX_REL) + (CHUNK - 1)
    return rel[:, idx]


def _attn_bias_mask(rel_ref, bm_ref):
    tt = lax.broadcasted_iota(jnp.int32, (ATTN_ROWS, ATTN_WINDOW), 0) // CHUNK
    rr = lax.broadcasted_iota(jnp.int32, (ATTN_ROWS, ATTN_WINDOW), 1) // CHUNK
    band = (rr >= tt) & (rr <= tt + LEFT_CHUNKS)
    for j in range(bm_ref.shape[0]):
        vec = jnp.broadcast_to(rel_ref[j:j + 1, :], (ATTN_ROWS, ATTN_DIAGS))
        toeplitz = pltpu.roll(vec, ATTN_ROLL, 1, stride=1, stride_axis=0)[:, :ATTN_WINDOW]
        bm_ref[j] = jnp.where(band, toeplitz, -jnp.inf)


def _attn_probs(q_ref, k_ref, bm_ref, j, hd, start, valid):
    qh = q_ref[:, j * hd:(j + 1) * hd]
    kb = k_ref[pl.ds(start, ATTN_WINDOW), j * hd:(j + 1) * hd]
    sc = lax.dot_general(qh, kb, (((1,), (1,)), ((), ())), preferred_element_type=F32)
    sc = sc * (hd ** -0.5) + bm_ref[j]
    sc = jnp.where(valid, sc, -jnp.inf)
    sc = sc - jnp.max(sc, axis=-1, keepdims=True)
    e = jnp.exp(sc)
    return e / jnp.sum(e, axis=-1, keepdims=True), qh, kb


def _window_valid(start):
    r = lax.broadcasted_iota(jnp.int32, (1, ATTN_WINDOW), 1)
    return (start + r) >= LEFT_PAD


def _attn_fwd(q, kvp, rel_vec):
    t, d = q.shape
    hd = d // N_HEADS
    half = N_DEV // 2
    n = kvp.shape[-1]
    per = n // hd
    rows = kvp.shape[1]

    def body(q_ref, k_ref, v_ref, rel_ref, o_ref, bm_ref):
        @pl.when(pl.program_id(1) == 0)
        def _():
            _attn_bias_mask(rel_ref, bm_ref)

        start = pl.multiple_of(pl.program_id(1) * ATTN_ROWS, ATTN_ROWS)
        valid = _window_valid(start)
        for j in range(per):
            p, _, _ = _attn_probs(q_ref, k_ref, bm_ref, j, hd, start, valid)
            vb = v_ref[pl.ds(start, ATTN_WINDOW), j * hd:(j + 1) * hd]
            o_ref[:, j * hd:(j + 1) * hd] = jnp.dot(p.astype(BF16), vb, preferred_element_type=F32).astype(BF16)

    return pl.pallas_call(
        body, name="attn_fwd", grid=(half, t // ATTN_ROWS),
        in_specs=[pl.BlockSpec((ATTN_ROWS, n), lambda g, i: (i, g)),
                  pl.BlockSpec((None, rows, n), lambda g, i: (g, 0, 0)),
                  pl.BlockSpec((None, rows, n), lambda g, i: (half + g, 0, 0)),
                  pl.BlockSpec((None, per, ATTN_DIAGS), lambda g, i: (g, 0, 0))],
        out_specs=pl.BlockSpec((ATTN_ROWS, n), lambda g, i: (i, g)),
        out_shape=_sds((t, d), BF16),
        scratch_shapes=[pltpu.VMEM((per, ATTN_ROWS, ATTN_WINDOW), F32)],
        compiler_params=_params(("arbitrary", "arbitrary")),
    )(q, kvp, kvp, rel_vec.reshape(half, per, ATTN_DIAGS))


def _attn_bwd(q, dout, kvp, rel_vec, dk_in=None, dv_in=None):
    t, d = q.shape
    hd = d // N_HEADS
    half = N_DEV // 2
    n = kvp.shape[-1]
    per = n // hd
    rows = kvp.shape[1]
    scale = hd ** -0.5
    carry = dk_in is not None

    def body(q_ref, do_ref, k_ref, v_ref, rel_ref, *rest):
        dq_ref, dk_ref, dv_ref, dsc_ref, bm_ref = rest[-5:]

        @pl.when(pl.program_id(1) == 0)
        def _():
            _attn_bias_mask(rel_ref, bm_ref)
            dk_ref[...] = rest[0][...] if carry else jnp.zeros_like(dk_ref)
            dv_ref[...] = rest[1][...] if carry else jnp.zeros_like(dv_ref)
            dsc_ref[...] = jnp.zeros_like(dsc_ref)

        start = pl.multiple_of(pl.program_id(1) * ATTN_ROWS, ATTN_ROWS)
        valid = _window_valid(start)
        for j in range(per):
            cols = slice(j * hd, (j + 1) * hd)
            p, qh, kb = _attn_probs(q_ref, k_ref, bm_ref, j, hd, start, valid)
            vb = v_ref[pl.ds(start, ATTN_WINDOW), cols]
            doh = do_ref[:, cols]
            dp = lax.dot_general(doh, vb, (((1,), (1,)), ((), ())), preferred_element_type=F32)
            ds = p * (dp - jnp.sum(dp * p, axis=-1, keepdims=True))
            dsc_ref[j] += sum(ds[a * CHUNK:(a + 1) * CHUNK, a * CHUNK:a * CHUNK + BAND]
                              for a in range(ATTN_CHUNKS))
            dsb = (ds * scale).astype(BF16)
            dq_ref[:, cols] = jnp.dot(dsb, kb, preferred_element_type=F32).astype(BF16)
            dk_ref[pl.ds(start, ATTN_WINDOW), cols] += lax.dot_general(
                dsb, qh, (((0,), (0,)), ((), ())), preferred_element_type=F32)
            dv_ref[pl.ds(start, ATTN_WINDOW), cols] += lax.dot_general(
                p.astype(BF16), doh, (((0,), (0,)), ((), ())), preferred_element_type=F32)

    tile = pl.BlockSpec((ATTN_ROWS, n), lambda g, i: (i, g))
    shard = pl.BlockSpec((None, rows, n), lambda g, i: (g, 0, 0))
    in_specs = [tile, tile, shard, pl.BlockSpec((None, rows, n), lambda g, i: (half + g, 0, 0)),
                pl.BlockSpec((None, per, ATTN_DIAGS), lambda g, i: (g, 0, 0))]
    operands = [q, dout, kvp, kvp, rel_vec.reshape(half, per, ATTN_DIAGS)]
    if carry:
        in_specs += [shard, shard]
        operands += [dk_in, dv_in]
    acc = _sds((half, rows, n), F32)
    return pl.pallas_call(
        body, name="attn_bwd", grid=(half, t // ATTN_ROWS),
        in_specs=in_specs,
        out_specs=[tile, shard, shard, pl.BlockSpec((per, CHUNK, BAND), lambda g, i: (g, 0, 0))],
        out_shape=[_sds((t, d), BF16), acc, acc, _sds((N_HEADS, CHUNK, BAND), F32)],
        scratch_shapes=[pltpu.VMEM((per, ATTN_ROWS, ATTN_WINDOW), F32)],
        compiler_params=_params(("arbitrary", "arbitrary")),
    )(*operands)


SKEW_PITCH = 640
SKEW = SKEW_PITCH + 1
SKEW_LANES = -(-SKEW // LANES) * LANES


def _skew_diagonals(dsc):
    h = dsc.shape[0]
    wide = jnp.pad(dsc, ((0, 0), (0, 0), (0, SKEW_PITCH - BAND))).reshape(h, CHUNK * SKEW_PITCH)
    wide = jnp.pad(wide, ((0, 0), (0, CHUNK))).reshape(h, CHUNK, SKEW)
    return jnp.pad(wide, ((0, 0), (0, 0), (0, SKEW_LANES - SKEW)))


def _rel_bias_grad(skewed):
    heads = skewed.shape[0]
    hb = SUBLANES

    def body(d_ref, o_ref):
        col = lax.broadcasted_iota(jnp.int32, (SKEW_LANES, N_REL), 0)
        bucket = lax.broadcasted_iota(jnp.int32, (SKEW_LANES, N_REL), 1)
        diag = jnp.where(col < BAND, col, col - SKEW)
        idx = jnp.clip(LEFT_PAD - diag, -(CHUNK - 1), MAX_REL) + (CHUNK - 1)
        oh = ((idx == bucket) & (col < SKEW)).astype(BF16)
        dv = jnp.sum(d_ref[...], axis=1)
        hi = dv.astype(BF16)
        rest = dv - hi.astype(F32)
        mid = rest.astype(BF16)
        lo = (rest - mid.astype(F32)).astype(BF16)
        acc = jnp.dot(hi, oh, preferred_element_type=F32)
        acc += jnp.dot(mid, oh, preferred_element_type=F32)
        acc += jnp.dot(lo, oh, preferred_element_type=F32)
        o_ref[...] = acc

    return pl.pallas_call(
        body, name="rel_bias_grad", grid=(heads // hb,),
        in_specs=[pl.BlockSpec((hb, CHUNK, SKEW_LANES), lambda i: (i, 0, 0))],
        out_specs=pl.BlockSpec((hb, N_REL), lambda i: (i, 0)),
        out_shape=_sds((heads, N_REL), F32),
        compiler_params=_params(("parallel",)),
    )(skewed)


def _sum_parts(parts):
    s_n, rows, c = parts.shape
    br = _row_block(rows, OPT_ROW_BLOCK)

    def body(p_ref, o_ref):
        acc = p_ref[0].astype(F32)
        for s in range(1, s_n):
            acc = acc + p_ref[s].astype(F32)
        o_ref[...] = acc

    return pl.pallas_call(
        body, name="sum_parts", grid=(rows // br,),
        in_specs=[pl.BlockSpec((s_n, br, c), lambda i: (0, i, 0))],
        out_specs=pl.BlockSpec((br, c), lambda i: (i, 0)),
        out_shape=_sds((rows, c), F32),
        compiler_params=_params(("parallel",)),
    )(parts)


def _adamw(own, own_idx, parts, w, m, v, row0=0, bufs=None, after=None):
    _, rows, c = own.shape
    s_n = 0 if parts is None else parts.shape[0]
    total = w.shape[0]
    br = _row_block(rows, OPT_ROW_BLOCK)
    assert row0 % br == 0 and (bufs is not None or (row0 == 0 and total == rows))
    b0 = row0 // br
    m_corr = 1.0 - ADAM_B1 ** ADAM_STEP
    v_corr = 1.0 - ADAM_B2 ** ADAM_STEP

    def body(idx_ref, own_ref, *refs):
        if s_n:
            p_ref, refs = refs[0], refs[1:]
        w_ref, m_ref, v_ref = refs[:3]
        g_ref, d_ref, nm_ref, nv_ref = refs[-4:]
        g = own_ref[...].astype(F32)
        for s in range(s_n):
            g = g + p_ref[s].astype(F32)
        nm = ADAM_B1 * m_ref[...] + (1.0 - ADAM_B1) * g
        nv = ADAM_B2 * v_ref[...] + (1.0 - ADAM_B2) * (g * g)
        g_ref[...] = g
        nm_ref[...] = nm
        nv_ref[...] = nv
        d_ref[...] = -ADAM_LR * ((nm / m_corr) / (jnp.sqrt(nv / v_corr) + ADAM_EPS) + ADAM_WD * w_ref[...])

    tile = pl.BlockSpec((br, c), lambda i, idx: (i + b0, 0))
    in_specs = [pl.BlockSpec((None, br, c), lambda i, idx: (idx[0], i, 0))]
    operands = [own_idx, own]
    if s_n:
        in_specs.append(pl.BlockSpec((s_n, br, c), lambda i, idx: (0, i, 0)))
        operands.append(parts)
    in_specs += [tile, tile, tile]
    operands += [w, m, v]
    aliases = {}
    if bufs is not None:
        aliases = {len(operands) + j: j for j in range(4)}
        in_specs += [_ANY] * 4
        operands += list(bufs)
    if after is not None:
        in_specs.append(_ANY)
        operands.append(after)
    out = _sds((total, c), F32)
    return pl.pallas_call(
        body, name="adamw",
        grid_spec=pltpu.PrefetchScalarGridSpec(
            num_scalar_prefetch=1, grid=(rows // br,), in_specs=in_specs,
            out_specs=[tile, tile, tile, tile]),
        out_shape=[out, out, out, out],
        input_output_aliases=aliases,
        compiler_params=_params(("parallel",)),
    )(*operands)


def _chip_sum(p, r1, core):
    half = N_DEV // 2
    c = p.shape[-1]
    rows = int(np.prod(p.shape[1:-1]))
    br = _row_block(rows, BIG_ROW_BLOCK)

    def body(core_ref, p_ref, r_ref, o_ref):
        o_ref[...] = (p_ref[...].astype(F32) + r_ref[...].astype(F32)).astype(BF16)

    out = pl.pallas_call(
        body, name="chip_sum",
        grid_spec=pltpu.PrefetchScalarGridSpec(
            num_scalar_prefetch=1, grid=(half, rows // br),
            in_specs=[pl.BlockSpec((None, None, br, c), lambda q, i, cr: (q, cr[0], i, 0)),
                      pl.BlockSpec((None, br, c), lambda q, i, cr: (q, i, 0))],
            out_specs=pl.BlockSpec((None, br, c), lambda q, i, cr: (q, i, 0))),
        out_shape=_sds((half, rows, c), BF16),
        compiler_params=_params(("parallel", "parallel")),
    )(core, p.reshape(half, 2, rows, c), r1.reshape(half, rows, c))
    return out.reshape((half,) + p.shape[1:])


def _position():
    return tuple(lax.axis_index(a) for a in MESH_AXES)


def _linear(px, py, pc):
    return 4 * px + 2 * py + pc


def _all_gather_small(v, after=()):
    rows, lanes = v.shape

    def body(x_ref, *rest):
        out_ref, send_sems, recv_sems, local_sem = rest[-4:]
        x, y, c = _position()
        me, sibling = (x, y, c), (x, y, 1 - c)
        chips = [(1 - x, y), (x, 1 - y), (1 - x, 1 - y)]

        def copy(k, block, to, src=None):
            dst = out_ref.at[_linear(*block)]
            return pltpu.make_async_remote_copy(
                src_ref=dst if src is None else src, dst_ref=dst,
                send_sem=send_sems.at[k], recv_sem=recv_sems.at[k],
                device_id=to, device_id_type=MESH_ID)

        mine = pltpu.make_async_copy(x_ref, out_ref.at[_linear(*me)], local_sem)
        mine.start()
        first = [copy(0, me, sibling, src=x_ref)]
        first += [copy(1 + j, me, (*chip, c), src=x_ref) for j, chip in enumerate(chips)]
        for cp in first:
            cp.start()
        passed = [copy(4 + j, (*chip, c), sibling) for j, chip in enumerate(chips)]
        for j, chip in enumerate(chips):
            copy(1 + j, (*chip, c), me).wait_recv()
            passed[j].start()
        copy(0, sibling, me).wait_recv()
        for j, chip in enumerate(chips):
            copy(4 + j, (*chip, 1 - c), me).wait_recv()
        for cp in first + passed:
            cp.wait_send()
        mine.wait()

    return pl.pallas_call(
        body, name="all_gather_small",
        out_shape=_sds((N_DEV, rows, lanes), v.dtype),
        in_specs=[_VMEM] + [_ANY] * len(after), out_specs=_VMEM,
        scratch_shapes=[pltpu.SemaphoreType.DMA((7,)), pltpu.SemaphoreType.DMA((7,)),
                        pltpu.SemaphoreType.DMA],
        compiler_params=pltpu.CompilerParams(vmem_limit_bytes=VMEM_LIMIT),
    )(v, *after)


_HBM = pl.BlockSpec(memory_space=pltpu.HBM)
_SEM = pl.BlockSpec(memory_space=pltpu.SEMAPHORE)
_EFFECT = pltpu.SideEffectType.DATAFLOW_SIDE_EFFECTING
_ALL_CHIPS = [(0, 0), (0, 1), (1, 0), (1, 1)]


def _other_chips(x, y):
    return [(1 - x, y), (x, 1 - y), (1 - x, 1 - y)]


def _in_hbm(a):
    return pltpu.with_memory_space_constraint(a, pltpu.HBM)


def _token():
    return _sds((SUBLANES, LANES), F32)


def _gather_ici_copy(ref, i, k, chip, c, block, send_sems, recv_sems):
    return pltpu.make_async_remote_copy(
        src_ref=ref.at[block], dst_ref=ref.at[block],
        send_sem=send_sems.at[3 * i + k], recv_sem=recv_sems.at[3 * i + k],
        device_id=(*chip, c), device_id_type=MESH_ID)


def _gather_ici_start(name, lands, after=None):
    n = len(lands)
    extra = [] if after is None else [after]

    def body(*refs):
        ins, send_sems, recv_sems, token = refs[:n], refs[-n - 3], refs[-n - 2], refs[-1]
        x, y, c = _position()
        me = _linear(x, y, c)
        for i in range(n):
            for k, chip in enumerate(_other_chips(x, y)):
                _gather_ici_copy(ins[i], i, k, chip, c, me, send_sems, recv_sems).start()
        token[...] = jnp.zeros_like(token)

    out = pl.pallas_call(
        body, name=name,
        out_shape=(pltpu.SemaphoreType.DMA((3 * n,)), pltpu.SemaphoreType.DMA((3 * n,)),
                   *[pltpu.HBM(a.shape, a.dtype) for a in lands], _token()),
        in_specs=[_HBM] * n + [_ANY] * len(extra), out_specs=(_SEM, _SEM, *[_HBM] * n, _VMEM),
        input_output_aliases={i: 2 + i for i in range(n)},
        compiler_params=pltpu.CompilerParams(has_side_effects=_EFFECT),
    )(*[_in_hbm(a) for a in lands], *extra)
    return out[0], out[1], list(out[2:2 + n]), out[-1]


def _gather_ici_wait(name, lands, send_sems, recv_sems, after):
    n = len(lands)

    def body(*refs):
        ins, ss, rs = refs[:n], refs[n], refs[n + 1]
        x, y, c = _position()
        me = _linear(x, y, c)
        for i in range(n):
            for k, chip in enumerate(_other_chips(x, y)):
                _gather_ici_copy(ins[i], i, k, chip, c, me, ss, rs).wait_send()
                _gather_ici_copy(ins[i], i, k, chip, c, _linear(*chip, c), ss, rs).wait_recv()

    out = pl.pallas_call(
        body, name=name,
        out_shape=[pltpu.HBM(a.shape, a.dtype) for a in lands],
        in_specs=[_HBM] * n + [_SEM, _SEM, _ANY], out_specs=[_HBM] * n,
        input_output_aliases={i: i for i in range(n)},
        compiler_params=pltpu.CompilerParams(has_side_effects=_EFFECT),
    )(*lands, send_sems, recv_sems, after)
    return list(out)


def _gather_d2d(lands):
    n = len(lands)

    def body(*refs):
        ins, outs, send_sems, recv_sems = refs[:n], refs[n:2 * n], refs[2 * n], refs[2 * n + 1]
        x, y, c = _position()

        def copy(i, q, core):
            block = _linear(*_ALL_CHIPS[q], core)
            return pltpu.make_async_remote_copy(
                src_ref=ins[i].at[block], dst_ref=outs[i].at[block],
                send_sem=send_sems.at[i, q], recv_sem=recv_sems.at[i, q],
                device_id=(x, y, 1 - c), device_id_type=MESH_ID)

        sent = [copy(i, q, c) for i in range(n) for q in range(len(_ALL_CHIPS))]
        for cp in sent:
            cp.start()
        for i in range(n):
            for q in range(len(_ALL_CHIPS)):
                copy(i, q, 1 - c).wait_recv()
        for cp in sent:
            cp.wait_send()

    return pl.pallas_call(
        body, name="gather_d2d",
        out_shape=[_sds(a.shape, a.dtype) for a in lands],
        in_specs=[_ANY] * n, out_specs=[_ANY] * n,
        input_output_aliases={i: i for i in range(n)},
        scratch_shapes=[pltpu.SemaphoreType.DMA((n, 4)), pltpu.SemaphoreType.DMA((n, 4))],
    )(*lands)


def _partials_d2d(parts):
    n = len(parts)
    half = N_DEV // 2

    def body(*refs):
        ins, outs, send_sems, recv_sems = refs[:n], refs[n:2 * n], refs[2 * n], refs[2 * n + 1]
        x, y, c = _position()

        def copy(i, q):
            return pltpu.make_async_remote_copy(
                src_ref=ins[i].at[_linear(*_ALL_CHIPS[q], 1 - c)], dst_ref=outs[i].at[q],
                send_sem=send_sems.at[i, q], recv_sem=recv_sems.at[i, q],
                device_id=(x, y, 1 - c), device_id_type=MESH_ID)

        sent = [copy(i, q) for i in range(n) for q in range(half)]
        for cp in sent:
            cp.start()
        for cp in sent:
            cp.wait_recv()
        for cp in sent:
            cp.wait_send()

    return pl.pallas_call(
        body, name="partials_d2d",
        out_shape=[_sds((half,) + p.shape[1:], p.dtype) for p in parts],
        in_specs=[_ANY] * n, out_specs=[_ANY] * n,
        scratch_shapes=[pltpu.SemaphoreType.DMA((n, half)), pltpu.SemaphoreType.DMA((n, half))],
    )(*parts)


def _partials_peers(x, y, c, direct):
    chips = _other_chips(x, y)
    if not direct:
        return [((*ch, c), 2 * ch[0] + ch[1]) for ch in chips]
    peers = [(x, y, 1 - c)] + [(*ch, c) for ch in chips] + [(*ch, 1 - c) for ch in chips]
    return [(p, _linear(*p)) for p in peers]


def _partials_copies(srcs, lands, send_sems, recv_sems, direct):
    x, y, c = _position()
    peers = _partials_peers(x, y, c, direct)
    return [pltpu.make_async_remote_copy(
        src_ref=srcs[i].at[block], dst_ref=lands[i].at[k],
        send_sem=send_sems.at[len(peers) * i + k], recv_sem=recv_sems.at[len(peers) * i + k],
        device_id=peer, device_id_type=MESH_ID)
        for i in range(len(srcs)) for k, (peer, block) in enumerate(peers)]


def _partials_send_start(name, srcs, lands, direct, after=None):
    n = len(srcs)
    n_sem = n * (N_DEV - 1 if direct else len(_ALL_CHIPS) - 1)

    def body(*refs):
        _, send_sems, recv_sems = refs[:2 * n], refs[-2 * n - 3], refs[-2 * n - 2]
        for cp in _partials_copies(refs[:n], refs[n:2 * n], send_sems, recv_sems, direct):
            cp.start()
        refs[-1][...] = jnp.zeros_like(refs[-1])

    both = list(srcs) + list(lands)
    extra = [] if after is None else [after]
    out = pl.pallas_call(
        body, name=name,
        out_shape=(pltpu.SemaphoreType.DMA((n_sem,)), pltpu.SemaphoreType.DMA((n_sem,)),
                   *[pltpu.HBM(a.shape, a.dtype) for a in both], _token()),
        in_specs=[_HBM] * (2 * n) + [_ANY] * len(extra), out_specs=(_SEM, _SEM, *[_HBM] * (2 * n), _VMEM),
        input_output_aliases={i: 2 + i for i in range(2 * n)},
        compiler_params=pltpu.CompilerParams(has_side_effects=_EFFECT),
    )(*[_in_hbm(a) for a in both], *extra)
    return out[0], out[1], list(out[2:2 + n]), list(out[2 + n:2 + 2 * n]), out[-1]


def _partials_send_wait(name, srcs, lands, send_sems, recv_sems, direct, after):
    n = len(srcs)

    def body(*refs):
        for cp in _partials_copies(refs[:n], refs[n:2 * n], refs[2 * n], refs[2 * n + 1], direct):
            cp.wait_send()
            cp.wait_recv()

    both = list(srcs) + list(lands)
    out = pl.pallas_call(
        body, name=name,
        out_shape=[pltpu.HBM(a.shape, a.dtype) for a in both],
        in_specs=[_HBM] * (2 * n) + [_SEM, _SEM, _ANY], out_specs=[_HBM] * (2 * n),
        input_output_aliases={i: i for i in range(2 * n)},
        compiler_params=pltpu.CompilerParams(has_side_effects=_EFFECT),
    )(*both, send_sems, recv_sems, after)
    return list(out[:n]), list(out[n:])


def _pack(arrs):
    flat = jnp.concatenate([a.reshape(-1).astype(F32) for a in arrs])
    block = OPT_ROW_BLOCK if flat.shape[0] > OPT_ROW_BLOCK * LANES else SUBLANES
    pad = (-flat.shape[0]) % (block * LANES)
    if pad:
        flat = jnp.concatenate([flat, jnp.zeros((pad,), F32)])
    return flat.reshape(-1, LANES)


def _unpack(packed, shapes, lead=()):
    flat = packed.reshape(lead + (-1,))
    out, off = [], 0
    for s in shapes:
        size = int(np.prod(s))
        out.append(flat[..., off:off + size].reshape(lead + tuple(s)))
        off += size
    return out


def _unshard_last(g):
    nd = g.ndim
    perm = tuple(range(1, nd - 1)) + (0, nd - 1)
    t = jnp.transpose(g, perm)
    return t.reshape(t.shape[:-2] + (N_DEV * g.shape[-1],))


def kernel(x, c, w_ada, b_ada, ln_g, ln_b, ffn_gu, ffn_down, gmlp_w_in, gmlp_b_in, gmlp_ln_g, gmlp_ln_b, gmlp_w_s, gmlp_b_s, gmlp_w_out, w_ada_kv, b_ada_kv, w_kv, attn_w_q, attn_rel_bias, attn_w_o, loss_target, m_w_ada, m_b_ada, m_ln_g, m_ln_b, m_ffn_gu, m_ffn_down, m_gmlp_w_in, m_gmlp_b_in, m_gmlp_ln_g, m_gmlp_ln_b, m_gmlp_w_s, m_gmlp_b_s, m_gmlp_w_out, m_w_ada_kv, m_b_ada_kv, m_w_kv, m_attn_w_q, m_attn_rel_bias, m_attn_w_o, v_w_ada, v_b_ada, v_ln_g, v_ln_b, v_ffn_gu, v_ffn_down, v_gmlp_w_in, v_gmlp_b_in, v_gmlp_ln_g, v_gmlp_ln_b, v_gmlp_w_s, v_gmlp_b_s, v_gmlp_w_out, v_w_ada_kv, v_b_ada_kv, v_w_kv, v_attn_w_q, v_attn_rel_bias, v_attn_w_o):
    weights = dict(w_ada=w_ada, b_ada=b_ada, ln_g=ln_g, ln_b=ln_b, ffn_gu=ffn_gu, ffn_down=ffn_down,
                   gmlp_w_in=gmlp_w_in, gmlp_b_in=gmlp_b_in, gmlp_ln_g=gmlp_ln_g, gmlp_ln_b=gmlp_ln_b,
                   gmlp_w_s=gmlp_w_s, gmlp_b_s=gmlp_b_s, gmlp_w_out=gmlp_w_out, w_ada_kv=w_ada_kv,
                   b_ada_kv=b_ada_kv, w_kv=w_kv, attn_w_q=attn_w_q, attn_rel_bias=attn_rel_bias,
                   attn_w_o=attn_w_o)
    mom1 = dict(w_ada=m_w_ada, b_ada=m_b_ada, ln_g=m_ln_g, ln_b=m_ln_b, ffn_gu=m_ffn_gu, ffn_down=m_ffn_down,
                gmlp_w_in=m_gmlp_w_in, gmlp_b_in=m_gmlp_b_in, gmlp_ln_g=m_gmlp_ln_g, gmlp_ln_b=m_gmlp_ln_b,
                gmlp_w_s=m_gmlp_w_s, gmlp_b_s=m_gmlp_b_s, gmlp_w_out=m_gmlp_w_out, w_ada_kv=m_w_ada_kv,
                b_ada_kv=m_b_ada_kv, w_kv=m_w_kv, attn_w_q=m_attn_w_q, attn_rel_bias=m_attn_rel_bias,
                attn_w_o=m_attn_w_o)
    mom2 = dict(w_ada=v_w_ada, b_ada=v_b_ada, ln_g=v_ln_g, ln_b=v_ln_b, ffn_gu=v_ffn_gu, ffn_down=v_ffn_down,
                gmlp_w_in=v_gmlp_w_in, gmlp_b_in=v_gmlp_b_in, gmlp_ln_g=v_gmlp_ln_g, gmlp_ln_b=v_gmlp_ln_b,
                gmlp_w_s=v_gmlp_w_s, gmlp_b_s=v_gmlp_b_s, gmlp_w_out=v_gmlp_w_out, w_ada_kv=v_w_ada_kv,
                b_ada_kv=v_b_ada_kv, w_kv=v_w_kv, attn_w_q=v_attn_w_q, attn_rel_bias=v_attn_rel_bias,
                attn_w_o=v_attn_w_o)
    order = list(weights)

    x = x[0]
    target = loss_target[0]
    t, d = x.shape
    n_mod = w_ada.shape[-1] * N_DEV // d
    mod_w = w_ada.shape[-1]
    kv_w = w_ada_kv.shape[-1]
    n_b = DEPTH - N_A
    me = _linear(*_position())

    l2 = DEPTH * 2
    big = dict(
        ffn_gu=ffn_gu.reshape((l2,) + ffn_gu.shape[2:]),
        ffn_down=ffn_down.reshape((l2,) + ffn_down.shape[2:]),
        gmlp_w_in=gmlp_w_in, gmlp_w_out=gmlp_w_out, w_kv=w_kv[None],
        attn_w_q=attn_w_q, attn_w_o=attn_w_o)
    big_names = list(big)
    core = lax.axis_index("c").astype(jnp.int32).reshape(1)
    chip = (2 * lax.axis_index("x") + lax.axis_index("y")).astype(jnp.int32).reshape(1)

    fwd_groups = [
        {"ffn_gu": (0, 1), "ffn_down": (0, 1)},
        {"gmlp_w_in": (0, 1), "gmlp_w_out": (0, 1)},
        {"ffn_gu": (1, 1), "ffn_down": (1, 1)},
        {"ffn_gu": (2, 1), "ffn_down": (2, 1)},
        {"gmlp_w_in": (1, 1), "gmlp_w_out": (1, 1)},
        {"ffn_gu": (3, 1), "ffn_down": (3, 1), "w_kv": (0, 1)},
        {"ffn_gu": (4, 1), "ffn_down": (4, 1)},
        {"attn_w_q": (0, 1), "attn_w_o": (0, 1)},
        {"ffn_gu": (5, 1), "ffn_down": (5, 1)},
        {"ffn_gu": (6, 2), "ffn_down": (6, 2), "attn_w_q": (1, 1), "attn_w_o": (1, 1)},
    ]
    bwd_groups = []
    for l in range(DEPTH):
        g = {"ffn_gu": (2 * l, 2), "ffn_down": (2 * l, 2)}
        if l < N_A:
            g.update({"gmlp_w_in": (l, 1), "gmlp_w_out": (l, 1)})
        else:
            g.update({"attn_w_q": (l - N_A, 1), "attn_w_o": (l - N_A, 1)})
        if l == N_A - 1:
            g["w_kv"] = (0, 1)
        bwd_groups.append(g)

    def slot_of(groups, name, slot):
        for gi, g in enumerate(groups):
            if name in g and g[name][0] <= slot < g[name][0] + g[name][1]:
                return gi, slot - g[name][0]
        raise KeyError((name, slot))

    def start_group(gi, after=None):
        lands = []
        for name, (s0, cnt) in fwd_groups[gi].items():
            shard = big[name][s0:s0 + cnt].astype(BF16)
            land = lax.empty((N_DEV,) + shard.shape, BF16)
            lands.append(lax.dynamic_update_slice(land, shard[None], (me,) + (0,) * shard.ndim))
        return _gather_ici_start(f"gather_ici_start_{gi}", lands, after)

    gathered = [None] * len(fwd_groups)

    def land_group(gi, after):
        send_sems, recv_sems, lands, _ = flights[gi]
        lands = _gather_ici_wait(f"gather_ici_wait_{gi}", lands, send_sems, recv_sems, after)
        gathered[gi] = dict(zip(fwd_groups[gi], _gather_d2d(lands)))

    def weight(name, slot):
        gi, local = slot_of(fwd_groups, name, slot)
        return gathered[gi][name], local

    swapped = ("ffn_gu",)

    def grad_shape(name):
        s = big[name].shape[1:]
        return s[:-2] + (s[-1], s[-2]) if name in swapped else s

    partial = [{name: lax.empty((N_DEV, cnt) + grad_shape(name), BF16) for name, (_, cnt) in g.items()}
               for g in bwd_groups]

    c_all = _all_gather_small(_pack([c]))
    flights = [start_group(0, after=c_all)]
    c_all = _unpack(c_all, [(d,)], lead=(N_DEV,))[0]
    c4 = _as4(c_all)
    mod_part = _matmul("ada_fwd", c4, w_ada[:, None], (DEPTH, 1, N_DEV, mod_w), F32, a_silu=True)
    kv_part = _matmul("ada_kv_fwd", c4, _as4(w_ada_kv), (1, 1, N_DEV, kv_w), F32, a_silu=True)
    small_shapes = [mod_part.shape, kv_part.shape, ln_g.shape, ln_b.shape, gmlp_b_in.shape,
                    gmlp_ln_g.shape, gmlp_ln_b.shape, attn_rel_bias.shape]
    small = _all_gather_small(_pack([mod_part, kv_part, ln_g, ln_b, gmlp_b_in, gmlp_ln_g, gmlp_ln_b,
                                     attn_rel_bias]) + flights[0][3][0, 0])
    flights += [start_group(gi, after=small) for gi in range(1, len(fwd_groups))]
    start_token = sum(f[3][0, 0] for f in flights[1:])
    (mod_g, kvm_g, ln_g_g, ln_b_g, b_in_g, gln_g_g, gln_b_g, rel_g) = _unpack(small, small_shapes, lead=(N_DEV,))
    mod_mine = lax.dynamic_index_in_dim(mod_g[:, :, 0], me, axis=2, keepdims=False)
    mod = _unshard_last(mod_mine) + b_ada
    mod = mod.reshape(DEPTH, n_mod, 1, d)
    kvm_mine = lax.dynamic_index_in_dim(kvm_g[:, 0, 0], me, axis=1, keepdims=False)
    mkv = (_unshard_last(kvm_mine) + b_ada_kv).reshape(2, 1, d)
    ln_g_f = _unshard_last(ln_g_g)
    ln_b_f = _unshard_last(ln_b_g)
    half = N_DEV // 2
    b_in_f = jnp.transpose(b_in_g, (1, 0, 2))[:, :, None, :]
    gln_g_f = _unshard_last(gln_g_g).reshape(N_A, half, 1, -1)
    gln_b_f = _unshard_last(gln_b_g).reshape(N_A, half, 1, -1)
    rel_f = _unshard_last(rel_g)

    def shard_act(a):
        return a.reshape(a.shape[0], a.shape[2], a.shape[3])

    def grad_into(name, slot, mm):
        gi, local = slot_of(bwd_groups, name, slot)
        partial[gi][name] = mm(partial[gi][name], local)

    def ffn_fwd(h, lw):
        w_gu, l_gu = weight("ffn_gu", lw)
        w_dn, l_dn = weight("ffn_down", lw)
        gu, a = _ffn_up_fwd(h, w_gu, l_gu)
        y = _matmul("ffn_down_fwd", a[:, None], w_dn, (1, 1, t, d), F32, lb=l_dn, b_merge=2, reduce=True)
        return y[0, 0], (gu, a)

    def ffn_bwd(dy, h, saved, lw):
        gu, a = saved
        w_gu, l_gu = weight("ffn_gu", lw)
        w_dn, l_dn = weight("ffn_down", lw)
        dgu = _ffn_down_bwd_a(dy, w_dn, l_dn, gu).reshape((N_DEV,) + gu.shape[2:])
        grad_into("ffn_down", lw, lambda buf, lo: _matmul(
            "ffn_down_bwd_w", a[:, None], _as4(dy), buf.shape, BF16, ta=True, lo=lo, out_merge=2, out_buf=buf))
        dh = _matmul("ffn_gu_bwd_a", dgu[:, None], w_gu, (1, 1, t, d), F32, lb=l_gu, tb=True, reduce=True)
        grad_into("ffn_gu", lw, lambda buf, lo: _matmul(
            "ffn_gu_bwd_w", dgu[:, None], _as4(h), buf.shape, BF16, ta=True, lo=lo, out_buf=buf))
        return dh[0, 0], {}

    def gmlp_params(l):
        return (b_in_f[l], gln_g_f[l], gln_b_f[l], gmlp_w_s[l], gmlp_b_s[l][:, :, None])

    def gmlp_fwd(h, l):
        w_in, l_in = weight("gmlp_w_in", l)
        w_out, l_out = weight("gmlp_w_out", l)
        n = w_in.shape[-1]
        zpre = _matmul("gmlp_in_fwd", _as4(h), w_in, (N_DEV, 1, t, n), F32, lb=l_in)
        gated = _gmlp_mid_fwd(shard_act(zpre), *gmlp_params(l))
        y = _matmul("gmlp_out_fwd", gated[:, None], w_out, (1, 1, t, d), F32, lb=l_out, b_merge=2, reduce=True)
        return y[0, 0], (zpre, gated)

    def gmlp_bwd(dy, h, saved, l):
        zpre, gated = saved
        w_in, l_in = weight("gmlp_w_in", l)
        w_out, l_out = weight("gmlp_w_out", l)
        n = w_in.shape[-1]
        dgated = _matmul("gmlp_out_bwd_a", _as4(dy), w_out, (half, 1, t, n), F32, lb=l_out, b_merge=2, tb=True)
        grad_into("gmlp_w_out", l, lambda buf, lo: _matmul(
            "gmlp_out_bwd_w", gated[:, None], _as4(dy), buf.shape, BF16, ta=True, lo=lo, out_merge=2, out_buf=buf))
        dz, dws, dbs, dlng, dlnb, dbin = _gmlp_mid_bwd(shard_act(zpre), shard_act(dgated), *gmlp_params(l))
        dh = _matmul("gmlp_in_bwd_a", dz[:, None], w_in, (1, 1, t, d), F32, lb=l_in, tb=True, reduce=True)
        grad_into("gmlp_w_in", l, lambda buf, lo: _matmul(
            "gmlp_in_bwd_w", _as4(h), dz[:, None], buf.shape, BF16, ta=True, lo=lo, out_buf=buf))
        small_grads = dict(gmlp_w_s=dws, gmlp_b_s=dbs[:, :, 0], gmlp_ln_g=dlng.reshape(-1),
                           gmlp_ln_b=dlnb.reshape(-1), gmlp_b_in=dbin.reshape(-1))
        return dh[0, 0], small_grads

    def attn_fwd(h, j, kvp):
        rel_vec = _rel_vector(rel_f[j])
        w_q, l_q = weight("attn_w_q", j)
        w_o, l_o = weight("attn_w_o", j)
        q = _matmul("attn_q_fwd", _as4(h), w_q, (1, 1, t, d), BF16, lb=l_q, b_merge=N_DEV, reduce=True)[0, 0]
        o = _attn_fwd(q, kvp, rel_vec)
        y = _matmul("attn_o_fwd", _as4(o), w_o, (1, 1, t, d), F32, lb=l_o, b_merge=N_DEV, reduce=True)
        return y[0, 0], (q, o, rel_vec)

    def attn_bwd(dy, h, saved, j, kvp, dkv_acc):
        q, o, rel_vec = saved
        w_q, l_q = weight("attn_w_q", j)
        w_o, l_o = weight("attn_w_o", j)
        do = _matmul("attn_o_bwd_a", _as4(dy), w_o, (1, 1, t, d), BF16, lb=l_o, b_merge=N_DEV, tb=True)[0, 0]
        grad_into("attn_w_o", j, lambda buf, lo: _matmul(
            "attn_o_bwd_w", _as4(o), _as4(dy), buf.shape, BF16, ta=True, lo=lo, out_merge=N_DEV, out_buf=buf))
        dq, dk, dv, dsc = _attn_bwd(q, do, kvp, rel_vec, *dkv_acc)
        drel = _rel_bias_grad(_skew_diagonals(dsc))
        dh = _matmul("attn_q_bwd_a", _as4(dq), w_q, (1, 1, t, d), F32, lb=l_q, b_merge=N_DEV, tb=True)
        grad_into("attn_w_q", j, lambda buf, lo: _matmul(
            "attn_q_bwd_w", _as4(h), _as4(dq), buf.shape, BF16, ta=True, lo=lo, out_merge=N_DEV, out_buf=buf))
        return dh[0, 0], dict(attn_rel_bias=drel, dkv=(dk, dv))

    tape = []
    kvp = None
    kv_tape = None
    first_use = {(0, 0): 0, (0, 1): 1, (0, 2): 2, (1, 0): 3, (1, 1): 4, (1, 2): 5, (2, 0): 6, (2, 1): 7,
                 (2, 2): 8, (3, 0): 9}
    h = _modulate(x, mod[0, 1], mod[0, 0] + start_token)
    for l in range(DEPTH):
        for i in range(3):
            if (l, i) in first_use:
                land_group(first_use[l, i], x)
            scl, gate = mod[l, 3 * i + 1], mod[l, 3 * i + 2]
            wgt = 1.0 if i == 1 else 0.5
            gw = wgt * (1.0 + gate)
            if i != 1:
                y, saved = ffn_fwd(h, 2 * l + i // 2)
            elif l < N_A:
                y, saved = gmlp_fwd(h, l)
            else:
                y, saved = attn_fwd(h, l - N_A, kvp)
            nl, ni = (l, i + 1) if i < 2 else (l + 1, 0)
            readers = [(mod[nl, 3 * ni + 1], mod[nl, 3 * ni])] if nl < DEPTH else []
            shared_kv = (l, i) == (N_A - 1, 2)
            if shared_kv:
                readers.append((mkv[1], mkv[0]))
            outs = _ln_res_fwd(x, y, gw, ln_g_f[l, i][None], ln_b_f[l, i][None], readers)
            tape.append((x, h, y, gw, scl, saved))
            x = outs[0]
            h = outs[1] if nl < DEPTH else None
            if shared_kv:
                hkv = outs[-1]
                w_kvg, l_kv = weight("w_kv", 0)
                n = w_kvg.shape[-1]
                kv = _matmul("kv_fwd", _as4(hkv), w_kvg, (N_DEV, 1, t, n), BF16, lb=l_kv)
                kvp = jnp.pad(shard_act(kv), ((0, 0), (LEFT_PAD, 0), (0, 0)))
                kv_tape = hkv

    loss_part, dx = _loss_head(x, target)
    loss = lax.psum(loss_part[0, 0], MESH_AXES)

    d_mod = [[None] * n_mod for _ in range(DEPTH)]
    d_ln_g = [[None] * 3 for _ in range(DEPTH)]
    d_ln_b = [[None] * 3 for _ in range(DEPTH)]
    small_grads = {k: [None] * N_A for k in ("gmlp_w_s", "gmlp_b_s", "gmlp_ln_g", "gmlp_ln_b", "gmlp_b_in")}
    d_rel = [None] * n_b
    dkv_acc = ()
    d_mkv = None
    reductions = [None] * DEPTH
    sent_token = None
    readers = []
    for l in reversed(range(DEPTH)):
        if l == N_A - 1:
            hkv = kv_tape
            w_kvg, l_kv = weight("w_kv", 0)
            dkv = jnp.concatenate(dkv_acc)[:, LEFT_PAD:, :].astype(BF16)[:, None]
            dhkv = _matmul("kv_bwd_a", dkv, w_kvg, (1, 1, t, d), F32, lb=l_kv, tb=True, reduce=True)[0, 0]
            grad_into("w_kv", 0, lambda buf, lo: _matmul(
                "kv_bwd_w", _as4(hkv), dkv, buf.shape, BF16, ta=True, lo=lo, out_buf=buf))
            readers.append((dhkv, mkv[1], None))
        for i in reversed(range(3)):
            x_in, h, y, gw, scl, saved = tape[3 * l + i]
            wgt = 1.0 if i == 1 else 0.5
            if sent_token is not None:
                gw = gw + sent_token
                sent_token = None
            res = _ln_res_bwd(x_in, y, gw, ln_g_f[l, i][None], ln_b_f[l, i][None], dx,
                              [(r[0], r[1]) for r in readers])
            dx_res, dy, dgw, dg, db = res[:5]
            for k, (_, _, slot) in enumerate(readers):
                dscl_k, dshift_k = res[5 + 2 * k][0], res[6 + 2 * k][0]
                if slot is None:
                    d_mkv = jnp.concatenate([dshift_k, dscl_k])
                else:
                    d_mod[slot[0]][slot[1]], d_mod[slot[0]][slot[1] + 1] = dshift_k, dscl_k
            d_ln_g[l][i], d_ln_b[l][i] = dg[0], db[0]
            if i != 1:
                dh, extra = ffn_bwd(dy, h, saved, 2 * l + i // 2)
            elif l < N_A:
                dh, extra = gmlp_bwd(dy, h, saved, l)
                for k, g in extra.items():
                    small_grads[k][l] = g
            else:
                dh, extra = attn_bwd(dy, h, saved, l - N_A, kvp, dkv_acc)
                d_rel[l - N_A] = extra["attn_rel_bias"]
                dkv_acc = extra["dkv"]
            d_mod[l][3 * i + 2] = wgt * dgw[0]
            dx = dx_res
            readers = [(dh, scl, (l, 3 * i))]
        if l > 0:
            srcs = [partial[l][k] for k in bwd_groups[l]]
            lands = [lax.empty((N_DEV - 1,) + s.shape[1:], BF16) for s in srcs]
            reductions[l] = _partials_send_start(f"partials_send_start_{l}", srcs, lands, True)
            sent_token = reductions[l][4][0, 0]
    (dh, scl, _), = readers
    dx, dscl, dshift = _mod_bwd(dx, dh, tape[0][0], scl)
    d_mod[0][0], d_mod[0][1] = dshift[0], dscl[0]
    grad_x = dx[None]

    d_mod_arr = jnp.stack([jnp.concatenate(r) for r in d_mod])
    small_part = dict(
        b_ada=d_mod_arr, b_ada_kv=d_mkv,
        ln_g=jnp.stack([jnp.stack(r) for r in d_ln_g]), ln_b=jnp.stack([jnp.stack(r) for r in d_ln_b]),
        gmlp_b_in=jnp.stack(small_grads["gmlp_b_in"]), gmlp_ln_g=jnp.stack(small_grads["gmlp_ln_g"]),
        gmlp_ln_b=jnp.stack(small_grads["gmlp_ln_b"]), gmlp_w_s=jnp.stack(small_grads["gmlp_w_s"]),
        gmlp_b_s=jnp.stack(small_grads["gmlp_b_s"]), attn_rel_bias=jnp.stack(d_rel))
    small_names = list(small_part)
    sp_shapes = [small_part[k].shape for k in small_names]
    sp_all = _all_gather_small(_pack([small_part[k] for k in small_names]),
                               after=[partial[0][k] for k in bwd_groups[0]])

    from_sibling = _partials_d2d([partial[0][k] for k in bwd_groups[0]])
    sums = [_chip_sum(partial[0][k], r1, core) for k, r1 in zip(bwd_groups[0], from_sibling)]
    lands = [lax.empty((len(_ALL_CHIPS) - 1,) + s.shape[1:], BF16) for s in sums]
    reductions[0] = _partials_send_start("partials_send_start_0", sums, lands, False, after=sp_all)
    sent_token = reductions[0][4][0, 0]
    c4 = c4 + sent_token

    sp_sum = _sum_parts(sp_all)
    full_grads = dict(zip(small_names, _unpack(sp_sum, sp_shapes)))
    per_dev = dict(zip(small_names, _unpack(sp_all, sp_shapes, lead=(N_DEV,))))

    def my_cols(a, width):
        return lax.dynamic_slice_in_dim(a, me * width, width, axis=a.ndim - 1)

    grads = {}
    grads["b_ada"] = full_grads["b_ada"]
    grads["b_ada_kv"] = full_grads["b_ada_kv"]
    grads["gmlp_w_s"] = full_grads["gmlp_w_s"]
    grads["gmlp_b_s"] = full_grads["gmlp_b_s"]
    for k in ("ln_g", "ln_b", "gmlp_b_in", "gmlp_ln_g", "gmlp_ln_b", "attn_rel_bias"):
        grads[k] = my_cols(full_grads[k], weights[k].shape[-1])

    dmod_cols = jnp.transpose(my_cols(per_dev["b_ada"], mod_w), (1, 0, 2))[:, None]
    grads["w_ada"] = _matmul("ada_bwd_w", c4, dmod_cols, (DEPTH, 1, d, mod_w), F32, ta=True,
                             a_silu=True)[:, 0]
    dkv_cols = my_cols(per_dev["b_ada_kv"], kv_w)[None, None]
    grads["w_ada_kv"] = _matmul("ada_kv_bwd_w", c4, dkv_cols, (1, 1, d, kv_w), F32, ta=True,
                                a_silu=True)[0, 0]

    delta, new_m, new_v = {}, {}, {}
    first = jnp.zeros((1,), jnp.int32)

    def flat2(a, cols):
        return a.reshape(-1, cols)

    done = None
    for k in ("w_ada", "w_ada_kv"):
        w = weights[k]
        cols = w.shape[-1]
        res = _adamw(grads[k].reshape(1, -1, cols), first, None, flat2(w, cols), flat2(mom1[k], cols),
                     flat2(mom2[k], cols), after=done)
        grads[k], delta[k], new_m[k], new_v[k] = (a.reshape(w.shape) for a in res)
        done = res[0][:SUBLANES, :LANES]

    tiny = [k for k in order if k not in delta and k not in big_names]
    tiny_shapes = [weights[k].shape for k in tiny]
    tiny_out = _adamw((_pack([grads[k] for k in tiny]) + sent_token)[None], first, None,
                      _pack([weights[k] for k in tiny]), _pack([mom1[k] for k in tiny]),
                      _pack([mom2[k] for k in tiny]), after=done)
    for dst, arr in zip((grads, delta, new_m, new_v), tiny_out):
        for k, val in zip(tiny, _unpack(arr, tiny_shapes)):
            dst[k] = val

    def opt_view(k, a):
        a = jnp.swapaxes(a, -1, -2) if k in swapped else a
        return a.reshape(-1, a.shape[-1])

    def opt_unview(k, a):
        s = weights[k].shape
        return jnp.swapaxes(a.reshape(s[:-2] + (s[-1], s[-2])), -1, -2) if k in swapped else a.reshape(s)

    bufs = {k: [lax.empty(opt_view(k, weights[k]).shape, F32) for _ in range(4)] for k in big_names}
    done = tiny_out[0]
    me_idx = me.astype(jnp.int32).reshape(1)
    for l in reversed(range(DEPTH)):
        send_sems, recv_sems, srcs, lands, _ = reductions[l]
        srcs, lands = _partials_send_wait(f"partials_send_wait_{l}", srcs, lands, send_sems, recv_sems, l > 0, done)
        for k, own, got in zip(bwd_groups[l], srcs, lands):
            cols = own.shape[-1]
            slot_rows = int(np.prod(own.shape[2:-1]))
            bufs[k] = _adamw(own.reshape(own.shape[0], -1, cols), me_idx if l > 0 else chip,
                             got.reshape(got.shape[0], -1, cols),
                             opt_view(k, weights[k]), opt_view(k, mom1[k]), opt_view(k, mom2[k]),
                             row0=bwd_groups[l][k][0] * slot_rows, bufs=bufs[k], after=done)
            done = bufs[k][0][:SUBLANES, :LANES]
    for k in big_names:
        grads[k], delta[k], new_m[k], new_v[k] = (opt_unview(k, b) for b in bufs[k])

    return (loss, grad_x, *[grads[k] for k in order], *[delta[k] for k in order],
            *[new_m[k] for k in order], *[new_v[k] for k in order])
```

```python
import functools

import numpy as np
import jax
import jax.numpy as jnp
from jax import lax
from jax.experimental import pallas as pl
from jax.experimental.pallas import tpu as pltpu

F32 = jnp.float32
BF16 = jnp.bfloat16
MESH_AXES = ("x", "y", "c")
N_DEV = 8
MESH_ID = pl.DeviceIdType.MESH

DEPTH = 4
N_A = 2
CHUNK = 64
N_HEADS = 16
LEFT_CHUNKS = 8
BAND = (LEFT_CHUNKS + 1) * CHUNK
LEFT_PAD = LEFT_CHUNKS * CHUNK
MAX_REL = 4 * CHUNK
N_REL = (CHUNK - 1) + MAX_REL + 1
GMLP_WINDOW = 128
GMLP_GROUPS = 8
ALPHA = (2.0 * DEPTH) ** 0.25
LN_EPS = 1e-5
ADAM_LR = 0.001
ADAM_B1 = 0.9
ADAM_B2 = 0.999
ADAM_EPS = 1e-08
ADAM_WD = 0.01
ADAM_STEP = 10

V7X_VMEM_BYTES = 64 * 1024 * 1024
VMEM_LIMIT = V7X_VMEM_BYTES - 8 * 1024 * 1024
LANES = 128
SUBLANES = 8
MM_BLOCK = 2048
BIG_ROW_BLOCK = 1024
ROW_BLOCK = 512
OPT_ROW_BLOCK = 256

_ANY = pl.BlockSpec(memory_space=pl.ANY)
_VMEM = pl.BlockSpec(memory_space=pltpu.VMEM)


def _params(sem=None):
    return pltpu.CompilerParams(dimension_semantics=sem, vmem_limit_bytes=VMEM_LIMIT)


def _row_block(rows, target):
    for d in range(min(rows, target), 0, -1):
        if rows % d == 0 and (d % SUBLANES == 0 or d == rows):
            return d
    return rows


def _matmul(name, a, b, out_shape4, out_dtype, *, la=0, lb=0, lo=0, ta=False, tb=False,
            reduce=False, b_merge=1, out_merge=1, out_buf=None, a_silu=False):
    ja_n, _, a_r, a_c = a.shape
    jb_n, _, b_r, b_c = b.shape
    jo_n, _, o_r, o_c = out_shape4
    m_tot = a_c if ta else a_r
    k_a = a_r if ta else a_c
    b_rows = b_merge * b_r
    k_c = b_c if tb else b_rows
    n = b_rows if tb else b_c
    n_chunks = (jb_n // b_merge) if reduce else 1
    natural_k = reduce and ja_n == 1
    assert n == o_c, (name, n, o_c)
    assert k_a ==(k_c * n_chunks if natural_k else k_c), (name, k_a, k_c, n_chunks)
    bk = k_c if (k_c <= MM_BLOCK or (b_merge > 1 and not tb)) else MM_BLOCK
    assert k_c % bk == 0
    nkk = k_c // bk
    kg = 2 if (reduce and ja_n > 1 and nkk == 1 and not ta and n_chunks % 2 == 0) else 1
    nk = n_chunks * nkk // kg
    m_out = out_merge * o_r
    assert m_tot == m_out, (name, m_tot, m_out)
    bm = m_tot if (m_tot <= MM_BLOCK or out_merge > 1) else MM_BLOCK
    assert m_tot % bm == 0
    jo_blocks = jo_n // out_merge

    def a_index(j, m, k):
        kj, kk = k // nkk, k % nkk
        ja = 0 if ja_n == 1 else (kj if reduce else j)
        ke = kk + kj * nkk if natural_k else kk
        return (ja, la, ke, m) if ta else (ja, la, m, ke)

    def b_index(j, m, k):
        kj, kk = k // nkk, k % nkk
        jb = 0 if jb_n == b_merge else (kj if reduce else j)
        return (jb, lb, 0, kk) if tb else (jb, lb, kk, 0)

    def o_index(j, m, k):
        return (j, lo, 0, 0) if out_merge > 1 else (j, lo, m, 0)

    a_block = (None, None, bk, bm) if ta else (None if kg == 1 else kg, None, bm, bk)
    if b_merge > 1:
        b_block = (kg * b_merge, None, b_r, bk if tb else n)
    else:
        b_block = (None if kg == 1 else kg, None) + ((n, bk) if tb else (bk, n))
    o_block = (out_merge, None, o_r, n) if out_merge > 1 else (None, None, bm, n)
    dims = (((0 if ta else 1,), (1 if tb else 0,)), ((), ()))

    in_place = nk > 1 and out_dtype == F32 and out_merge == 1
    use_acc = nk > 1 and not in_place

    def body(a_ref, b_ref, *rest):
        o_ref = rest[-2] if use_acc else rest[-1]
        k = pl.program_id(2)
        av = a_ref[...]
        if a_silu:
            af = av.astype(F32)
            av = af * jax.nn.sigmoid(af)
        bv = b_ref[...]
        if kg > 1:
            bv = bv.reshape(kg, -1, bv.shape[-1])
            prod = sum(lax.dot_general(av[g].astype(BF16), bv[g].astype(BF16), dims, preferred_element_type=F32)
                       for g in range(kg))
        else:
            if b_merge > 1:
                bv = bv.reshape(b_rows, bv.shape[-1])
            prod = lax.dot_general(av.astype(BF16), bv.astype(BF16), dims, preferred_element_type=F32)

        def emit(val):
            val = val.astype(out_dtype)
            o_ref[...] = val.reshape(out_merge, o_r, n) if out_merge > 1 else val

        if nk == 1:
            emit(prod)
            return
        acc_ref = o_ref if in_place else rest[-1]

        @pl.when(k == 0)
        def _():
            acc_ref[...] = prod

        @pl.when(k > 0)
        def _():
            acc_ref[...] += prod

        if use_acc:
            @pl.when(k == nk - 1)
            def _():
                emit(acc_ref[...])

    in_specs = [pl.BlockSpec(a_block, a_index), pl.BlockSpec(b_block, b_index)]
    operands = [a, b]
    aliases = {}
    if out_buf is not None:
        assert out_buf.shape == tuple(out_shape4) and out_buf.dtype == out_dtype
        in_specs.append(_ANY)
        operands.append(out_buf)
        aliases = {2: 0}
    return pl.pallas_call(
        body, name=name,
        grid=(jo_blocks, m_tot // bm, nk),
        in_specs=in_specs,
        out_specs=pl.BlockSpec(o_block, o_index),
        out_shape=jax.ShapeDtypeStruct(tuple(out_shape4), out_dtype),
        scratch_shapes=[pltpu.VMEM((bm, n), F32)] if use_acc else [],
        input_output_aliases=aliases,
        compiler_params=_params(("parallel", "parallel", "arbitrary")),
    )(*operands)


def _as4(a):
    return a.reshape((1,) * (4 - a.ndim) + a.shape)


def _row_call(name, body, ins, outs, t, *, acc_outs=()):
    bt = _row_block(t, ROW_BLOCK)

    def spec(arr, tiled):
        if tiled:
            return pl.BlockSpec((bt,) + tuple(arr.shape[1:]), lambda i: (i,) + (0,) * (arr.ndim - 1))
        return pl.BlockSpec(tuple(arr.shape), lambda i: (0,) * arr.ndim)

    return pl.pallas_call(
        body, name=name, grid=(t // bt,),
        in_specs=[spec(a, tl) for a, tl in ins],
        out_specs=[spec(o, tl) for o, tl in outs],
        out_shape=[jax.ShapeDtypeStruct(o.shape, o.dtype) for o, _ in outs],
        compiler_params=_params(("arbitrary",) if acc_outs else ("parallel",)),
    )(*[a for a, _ in ins])


def _sds(shape, dtype):
    return jax.ShapeDtypeStruct(tuple(shape), dtype)


def _modulate(x, scl, shift):
    t, d = x.shape

    def body(x_ref, s_ref, b_ref, h_ref):
        h_ref[...] = (x_ref[...] * (1.0 + s_ref[...]) + b_ref[...]).astype(BF16)

    return _row_call("modulate", body, [(x, True), (scl, False), (shift, False)],
                     [(_sds((t, d), BF16), True)], t)[0]


def _ln_stats(r):
    mu = jnp.mean(r, axis=-1, keepdims=True)
    rc = r - mu
    var = jnp.mean(rc * rc, axis=-1, keepdims=True)
    rstd = lax.rsqrt(var + LN_EPS)
    return rc * rstd, rstd


def _ln_res_fwd(x, y, gw, g, b, mods=()):
    t, d = x.shape
    n_mod = len(mods)

    def body(x_ref, y_ref, gw_ref, g_ref, b_ref, *rest):
        mod_refs, o_ref, h_refs = rest[:2 * n_mod], rest[2 * n_mod], rest[2 * n_mod + 1:]
        r = ALPHA * x_ref[...] + gw_ref[...] * y_ref[...]
        xhat, _ = _ln_stats(r)
        xn = xhat * g_ref[...] + b_ref[...]
        o_ref[...] = xn
        for k in range(n_mod):
            h_refs[k][...] = (xn * (1.0 + mod_refs[2 * k][...]) + mod_refs[2 * k + 1][...]).astype(BF16)

    vecs = [(v, False) for pair in mods for v in pair]
    return _row_call("ln_res_fwd", body,
                     [(x, True), (y, True), (gw, False), (g, False), (b, False)] + vecs,
                     [(_sds((t, d), F32), True)] + [(_sds((t, d), BF16), True)] * n_mod, t)


def _ln_res_bwd(x, y, gw, g, b, dx_base, pairs=()):
    t, d = x.shape
    n_pair = len(pairs)

    def body(x_ref, y_ref, gw_ref, g_ref, b_ref, dxb_ref, *rest):
        pair_refs, outs = rest[:2 * n_pair], rest[2 * n_pair:]
        dx_ref, dy_ref = outs[0], outs[1]
        sums = outs[2:]

        @pl.when(pl.program_id(0) == 0)
        def _():
            for r in sums:
                r[...] = jnp.zeros_like(r)

        yv = y_ref[...]
        gwv = gw_ref[...]
        gv = g_ref[...]
        xhat, rstd = _ln_stats(ALPHA * x_ref[...] + gwv * yv)
        dxn = dxb_ref[...]
        if n_pair:
            xn = xhat * gv + b_ref[...]
            for k in range(n_pair):
                dh = pair_refs[2 * k][...]
                dxn = dxn + dh * (1.0 + pair_refs[2 * k + 1][...])
                sums[3 + 2 * k][...] += jnp.sum(dh * xn, axis=0, keepdims=True)
                sums[4 + 2 * k][...] += jnp.sum(dh, axis=0, keepdims=True)
        dxh = dxn * gv
        m1 = jnp.mean(dxh, axis=-1, keepdims=True)
        m2 = jnp.mean(dxh * xhat, axis=-1, keepdims=True)
        dr = rstd * (dxh - m1 - xhat * m2)
        dx_ref[...] = ALPHA * dr
        dy_ref[...] = (gwv * dr).astype(BF16)
        sums[0][...] += jnp.sum(dr * yv, axis=0, keepdims=True)
        sums[1][...] += jnp.sum(dxn * xhat, axis=0, keepdims=True)
        sums[2][...] += jnp.sum(dxn, axis=0, keepdims=True)

    vec = _sds((1, d), F32)
    n_sum = 3 + 2 * n_pair
    ins = [(x, True), (y, True), (gw, False), (g, False), (b, False), (dx_base, True)]
    for dh, scl in pairs:
        ins += [(dh, True), (scl, False)]
    return _row_call("ln_res_bwd", body, ins,
                     [(_sds((t, d), F32), True), (_sds((t, d), BF16), True)] + [(vec, False)] * n_sum, t,
                     acc_outs=tuple(range(2, 2 + n_sum)))


def _mod_bwd(dx_res, dh, x, scl):
    t, d = x.shape

    def body(dxr_ref, dh_ref, x_ref, s_ref, dx_ref, ds_ref, db_ref):
        @pl.when(pl.program_id(0) == 0)
        def _():
            ds_ref[...] = jnp.zeros_like(ds_ref)
            db_ref[...] = jnp.zeros_like(db_ref)

        dh = dh_ref[...]
        dx_ref[...] = dxr_ref[...] + dh * (1.0 + s_ref[...])
        ds_ref[...] += jnp.sum(dh * x_ref[...], axis=0, keepdims=True)
        db_ref[...] += jnp.sum(dh, axis=0, keepdims=True)

    vec = _sds((1, d), F32)
    return _row_call("mod_bwd", body, [(dx_res, True), (dh, True), (x, True), (scl, False)],
                     [(_sds((t, d), F32), True), (vec, False), (vec, False)], t, acc_outs=(1, 2))


def _loss_head(y, target):
    t, d = y.shape

    def body(y_ref, t_ref, l_ref, dy_ref):
        @pl.when(pl.program_id(0) == 0)
        def _():
            l_ref[...] = jnp.zeros_like(l_ref)

        err = y_ref[...] - t_ref[...]
        dy_ref[...] = err * (1.0 / d)
        part = 0.5 * jnp.sum(jnp.mean(err * err, axis=-1, keepdims=True), axis=0, keepdims=True)
        l_ref[...] += jnp.broadcast_to(part, l_ref.shape)

    return _row_call("loss_head", body, [(y, True), (target, True)],
                     [(_sds((SUBLANES, LANES), F32), False), (_sds((t, d), F32), True)], t,
                     acc_outs=(0,))


def _sigmoid(x):
    return 0.5 * jnp.tanh(0.5 * x) + 0.5


def _ffn_up_fwd(h, w_gu, lb):
    t, d = h.shape
    n = w_gu.shape[-1]
    half = N_DEV // 2
    bt = _row_block(t, BIG_ROW_BLOCK)

    def body(h_ref, wg_ref, wu_ref, fac_ref, a_ref):
        hv = h_ref[...]
        g = jnp.dot(hv, wg_ref[...], preferred_element_type=F32)
        u = jnp.dot(hv, wu_ref[...], preferred_element_type=F32)
        sig = _sigmoid(g)
        silu = g * sig
        fac_ref[0] = u * (sig + silu * (1.0 - sig))
        fac_ref[1] = silu
        a_ref[...] = (silu * u).astype(BF16)

    return pl.pallas_call(
        body, name="ffn_up_fwd", grid=(half, t // bt),
        in_specs=[pl.BlockSpec((bt, d), lambda j, i: (i, 0)),
                  pl.BlockSpec((None, None, d, n), lambda j, i: (j, lb, 0, 0)),
                  pl.BlockSpec((None, None, d, n), lambda j, i: (half + j, lb, 0, 0))],
        out_specs=[pl.BlockSpec((2, None, bt, n), lambda j, i: (0, j, i, 0)),
                   pl.BlockSpec((None, bt, n), lambda j, i: (j, i, 0))],
        out_shape=[_sds((2, half, t, n), F32), _sds((half, t, n), BF16)],
        compiler_params=_params(("parallel", "parallel")),
    )(h, w_gu, w_gu)


def _ffn_down_bwd_a(dy, w_down, lb, fac):
    t, d = dy.shape
    _, half, _, n = fac.shape
    r = w_down.shape[2]
    bt = _row_block(t, BIG_ROW_BLOCK)

    def body(dy_ref, w_ref, fac_ref, d_ref):
        da = lax.dot_general(dy_ref[...], w_ref[...].reshape(2 * r, d), (((1,), (1,)), ((), ())),
                             preferred_element_type=F32)
        d_ref[0] = (da * fac_ref[0]).astype(BF16)
        d_ref[1] = (da * fac_ref[1]).astype(BF16)

    return pl.pallas_call(
        body, name="ffn_down_bwd_a", grid=(half, t // bt),
        in_specs=[pl.BlockSpec((bt, d), lambda j, i: (i, 0)),
                  pl.BlockSpec((2, None, r, d), lambda j, i: (j, lb, 0, 0)),
                  pl.BlockSpec((2, None, bt, n), lambda j, i: (0, j, i, 0))],
        out_specs=pl.BlockSpec((2, None, bt, n), lambda j, i: (0, j, i, 0)),
        out_shape=_sds((2, half, t, n), BF16),
        compiler_params=_params(("parallel", "parallel")),
    )(dy, w_down, fac)


_INV_SQRT2 = 0.7071067811865476
_INV_SQRT_2PI = 0.3989422804014327


def _gelu(z):
    return 0.5 * z * (1.0 + lax.erf(z * _INV_SQRT2))


def _gelu_grad(z):
    return 0.5 * (1.0 + lax.erf(z * _INV_SQRT2)) + z * jnp.exp(-0.5 * z * z) * _INV_SQRT_2PI


def _window_mask():
    t_out = lax.broadcasted_iota(jnp.int32, (GMLP_WINDOW, GMLP_WINDOW), 0)
    s_in = lax.broadcasted_iota(jnp.int32, (GMLP_WINDOW, GMLP_WINDOW), 1)
    return (s_in // CHUNK) <= (t_out // CHUNK)


def _gmlp_recompute(z_ref, bin_ref, lng_ref, lnb_ref):
    half = N_DEV // 2
    z = z_ref[...] + bin_ref[...]
    ge = _gelu(z)
    u = ge[:half]
    v = ge[half:]
    width = half * v.shape[-1]
    mu = jnp.sum(jnp.sum(v, axis=0), axis=-1, keepdims=True) / width
    vc = v - mu
    var = jnp.sum(jnp.sum(vc * vc, axis=0), axis=-1, keepdims=True) / width
    rstd = lax.rsqrt(var + LN_EPS)
    xhat = vc * rstd
    vn = xhat * lng_ref[...] + lnb_ref[...]
    return z, u, xhat, rstd, vn


def _gmlp_mid_fwd(zpre, b_in, ln_g, ln_b, w_s, b_s):
    _, t, n = zpre.shape
    half = N_DEV // 2
    gd = half * n // GMLP_GROUPS
    per = n // gd
    w = GMLP_WINDOW

    def body(z_ref, bin_ref, lng_ref, lnb_ref, ws_ref, bs_ref, o_ref):
        _, u, _, _, vn = _gmlp_recompute(z_ref, bin_ref, lng_ref, lnb_ref)
        mask = _window_mask()
        for g in range(GMLP_GROUPS):
            sh, c0 = g // per, (g % per) * gd
            wsm = jnp.where(mask, ws_ref[g], 0.0).astype(BF16)
            s = jnp.dot(wsm, vn[sh][:, c0:c0 + gd].astype(BF16), preferred_element_type=F32) + bs_ref[g]
            o_ref[sh, :, c0:c0 + gd] = (u[sh][:, c0:c0 + gd] * s).astype(BF16)

    whole = lambda a: pl.BlockSpec(tuple(a.shape), lambda i: (0,) * a.ndim)
    return pl.pallas_call(
        body, name="gmlp_mid_fwd", grid=(t // w,),
        in_specs=[pl.BlockSpec((N_DEV, w, n), lambda i: (0, i, 0)),
                  whole(b_in), whole(ln_g), whole(ln_b), whole(w_s), whole(b_s)],
        out_specs=pl.BlockSpec((half, w, n), lambda i: (0, i, 0)),
        out_shape=_sds((half, t, n), BF16),
        compiler_params=_params(("parallel",)),
    )(zpre, b_in, ln_g, ln_b, w_s, b_s)


def _gmlp_mid_bwd(zpre, dgated, b_in, ln_g, ln_b, w_s, b_s):
    _, t, n = zpre.shape
    half = N_DEV // 2
    gd = half * n // GMLP_GROUPS
    per = n // gd
    w = GMLP_WINDOW
    width = half * n

    def body(z_ref, dg_ref, bin_ref, lng_ref, lnb_ref, ws_ref, bs_ref,
             dz_ref, dws_ref, dbs_ref, dlng_ref, dlnb_ref, dbin_ref, du_ref, dvn_ref):
        @pl.when(pl.program_id(0) == 0)
        def _():
            for r in (dws_ref, dbs_ref, dlng_ref, dlnb_ref, dbin_ref):
                r[...] = jnp.zeros_like(r)

        z, u, xhat, rstd, vn = _gmlp_recompute(z_ref, bin_ref, lng_ref, lnb_ref)
        mask = _window_mask()
        for g in range(GMLP_GROUPS):
            sh, c0 = g // per, (g % per) * gd
            wsm = jnp.where(mask, ws_ref[g], 0.0).astype(BF16)
            vg = vn[sh][:, c0:c0 + gd].astype(BF16)
            s = jnp.dot(wsm, vg, preferred_element_type=F32) + bs_ref[g]
            dgt = dg_ref[sh, :, c0:c0 + gd]
            ds = dgt * u[sh][:, c0:c0 + gd]
            du_ref[sh, :, c0:c0 + gd] = dgt * s
            dsb = ds.astype(BF16)
            dws = lax.dot_general(dsb, vg, (((1,), (1,)), ((), ())), preferred_element_type=F32)
            dws_ref[g] += jnp.where(mask, dws, 0.0)
            dbs_ref[g] += jnp.sum(ds, axis=-1, keepdims=True)
            dvn_ref[sh, :, c0:c0 + gd] = lax.dot_general(wsm, dsb, (((0,), (0,)), ((), ())),
                                                         preferred_element_type=F32)
        dvn = dvn_ref[...]
        dlng_ref[...] += jnp.sum(dvn * xhat, axis=1, keepdims=True)
        dlnb_ref[...] += jnp.sum(dvn, axis=1, keepdims=True)
        dxh = dvn * lng_ref[...]
        m1 = jnp.sum(jnp.sum(dxh, axis=0), axis=-1, keepdims=True) / width
        m2 = jnp.sum(jnp.sum(dxh * xhat, axis=0), axis=-1, keepdims=True) / width
        dv = rstd * (dxh - m1 - xhat * m2)
        gg = _gelu_grad(z)
        dzu = du_ref[...] * gg[:half]
        dzv = dv * gg[half:]
        dz_ref[:half] = dzu.astype(BF16)
        dz_ref[half:] = dzv.astype(BF16)
        dbin_ref[:half] += jnp.sum(dzu, axis=1, keepdims=True)
        dbin_ref[half:] += jnp.sum(dzv, axis=1, keepdims=True)

    whole = lambda a: pl.BlockSpec(tuple(a.shape), lambda i: (0,) * a.ndim)
    outs = [_sds((N_DEV, t, n), BF16), _sds(w_s.shape, F32), _sds(b_s.shape, F32),
            _sds(ln_g.shape, F32), _sds(ln_b.shape, F32), _sds(b_in.shape, F32)]
    return pl.pallas_call(
        body, name="gmlp_mid_bwd", grid=(t // w,),
        in_specs=[pl.BlockSpec((N_DEV, w, n), lambda i: (0, i, 0)),
                  pl.BlockSpec((half, w, n), lambda i: (0, i, 0)),
                  whole(b_in), whole(ln_g), whole(ln_b), whole(w_s), whole(b_s)],
        out_specs=[pl.BlockSpec((N_DEV, w, n), lambda i: (0, i, 0))] + [whole(o) for o in outs[1:]],
        out_shape=outs,
        scratch_shapes=[pltpu.VMEM((half, w, n), F32), pltpu.VMEM((half, w, n), F32)],
        compiler_params=_params(("arbitrary",)),
    )(zpre, dgated, b_in, ln_g, ln_b, w_s, b_s)


ATTN_CHUNKS = 2
ATTN_ROWS = ATTN_CHUNKS * CHUNK
ATTN_WINDOW = ATTN_ROWS + LEFT_PAD
ATTN_DIAGS = -(-(ATTN_ROWS + ATTN_WINDOW - 1) // LANES) * LANES
ATTN_ROLL = ATTN_DIAGS - (ATTN_ROWS - 1)


def _rel_vector(rel):
    j = np.arange(ATTN_DIAGS)
    idx = np.clip(ATTN_WINDOW - 1 - j, -(CHUNK - 1), MAX_REL) + (CHUNK - 1)
    return rel[:, idx]


def _attn_bias_mask(rel_ref, bm_ref):
    tt = lax.broadcasted_iota(jnp.int32, (ATTN_ROWS, ATTN_WINDOW), 0) // CHUNK
    rr = lax.broadcasted_iota(jnp.int32, (ATTN_ROWS, ATTN_WINDOW), 1) // CHUNK
    band = (rr >= tt) & (rr <= tt + LEFT_CHUNKS)
    for j in range(bm_ref.shape[0]):
        vec = jnp.broadcast_to(rel_ref[j:j + 1, :], (ATTN_ROWS, ATTN_DIAGS))
        toeplitz = pltpu.roll(vec, ATTN_ROLL, 1, stride=1, stride_axis=0)[:, :ATTN_WINDOW]
        bm_ref[j] = jnp.where(band, toeplitz, -jnp.inf)


def _attn_probs(q_ref, k_ref, bm_ref, j, hd, start, valid):
    qh = q_ref[:, j * hd:(j + 1) * hd]
    kb = k_ref[pl.ds(start, ATTN_WINDOW), j * hd:(j + 1) * hd]
    sc = lax.dot_general(qh, kb, (((1,), (1,)), ((), ())), preferred_element_type=F32)
    sc = sc * (hd ** -0.5) + bm_ref[j]
    sc = jnp.where(valid, sc, -jnp.inf)
    sc = sc - jnp.max(sc, axis=-1, keepdims=True)
    e = jnp.exp(sc)
    return e / jnp.sum(e, axis=-1, keepdims=True), qh, kb


def _window_valid(start):
    r = lax.broadcasted_iota(jnp.int32, (1, ATTN_WINDOW), 1)
    return (start + r) >= LEFT_PAD


def _attn_fwd(q, kvp, rel_vec):
    t, d = q.shape
    hd = d // N_HEADS
    half = N_DEV // 2
    n = kvp.shape[-1]
    per = n // hd
    rows = kvp.shape[1]

    def body(q_ref, k_ref, v_ref, rel_ref, o_ref, bm_ref):
        @pl.when(pl.program_id(1) == 0)
        def _():
            _attn_bias_mask(rel_ref, bm_ref)

        start = pl.multiple_of(pl.program_id(1) * ATTN_ROWS, ATTN_ROWS)
        valid = _window_valid(start)
        for j in range(per):
            p, _, _ = _attn_probs(q_ref, k_ref, bm_ref, j, hd, start, valid)
            vb = v_ref[pl.ds(start, ATTN_WINDOW), j * hd:(j + 1) * hd]
            o_ref[:, j * hd:(j + 1) * hd] = jnp.dot(p.astype(BF16), vb, preferred_element_type=F32).astype(BF16)

    return pl.pallas_call(
        body, name="attn_fwd", grid=(half, t // ATTN_ROWS),
        in_specs=[pl.BlockSpec((ATTN_ROWS, n), lambda g, i: (i, g)),
                  pl.BlockSpec((None, rows, n), lambda g, i: (g, 0, 0)),
                  pl.BlockSpec((None, rows, n), lambda g, i: (half + g, 0, 0)),
                  pl.BlockSpec((None, per, ATTN_DIAGS), lambda g, i: (g, 0, 0))],
        out_specs=pl.BlockSpec((ATTN_ROWS, n), lambda g, i: (i, g)),
        out_shape=_sds((t, d), BF16),
        scratch_shapes=[pltpu.VMEM((per, ATTN_ROWS, ATTN_WINDOW), F32)],
        compiler_params=_params(("arbitrary", "arbitrary")),
    )(q, kvp, kvp, rel_vec.reshape(half, per, ATTN_DIAGS))


def _attn_bwd(q, dout, kvp, rel_vec, dk_in=None, dv_in=None):
    t, d = q.shape
    hd = d // N_HEADS
    half = N_DEV // 2
    n = kvp.shape[-1]
    per = n // hd
    rows = kvp.shape[1]
    scale = hd ** -0.5
    carry = dk_in is not None

    def body(q_ref, do_ref, k_ref, v_ref, rel_ref, *rest):
        dq_ref, dk_ref, dv_ref, dsc_ref, bm_ref = rest[-5:]

        @pl.when(pl.program_id(1) == 0)
        def _():
            _attn_bias_mask(rel_ref, bm_ref)
            dk_ref[...] = rest[0][...] if carry else jnp.zeros_like(dk_ref)
            dv_ref[...] = rest[1][...] if carry else jnp.zeros_like(dv_ref)
            dsc_ref[...] = jnp.zeros_like(dsc_ref)

        start = pl.multiple_of(pl.program_id(1) * ATTN_ROWS, ATTN_ROWS)
        valid = _window_valid(start)
        for j in range(per):
            cols = slice(j * hd, (j + 1) * hd)
            p, qh, kb = _attn_probs(q_ref, k_ref, bm_ref, j, hd, start, valid)
            vb = v_ref[pl.ds(start, ATTN_WINDOW), cols]
            doh = do_ref[:, cols]
            dp = lax.dot_general(doh, vb, (((1,), (1,)), ((), ())), preferred_element_type=F32)
            ds = p * (dp - jnp.sum(dp * p, axis=-1, keepdims=True))
            dsc_ref[j] += sum(ds[a * CHUNK:(a + 1) * CHUNK, a * CHUNK:a * CHUNK + BAND]
                              for a in range(ATTN_CHUNKS))
            dsb = (ds * scale).astype(BF16)
            dq_ref[:, cols] = jnp.dot(dsb, kb, preferred_element_type=F32).astype(BF16)
            dk_ref[pl.ds(start, ATTN_WINDOW), cols] += lax.dot_general(
                dsb, qh, (((0,), (0,)), ((), ())), preferred_element_type=F32)
            dv_ref[pl.ds(start, ATTN_WINDOW), cols] += lax.dot_general(
                p.astype(BF16), doh, (((0,), (0,)), ((), ())), preferred_element_type=F32)

    tile = pl.BlockSpec((ATTN_ROWS, n), lambda g, i: (i, g))
    shard = pl.BlockSpec((None, rows, n), lambda g, i: (g, 0, 0))
    in_specs = [tile, tile, shard, pl.BlockSpec((None, rows, n), lambda g, i: (half + g, 0, 0)),
                pl.BlockSpec((None, per, ATTN_DIAGS), lambda g, i: (g, 0, 0))]
    operands = [q, dout, kvp, kvp, rel_vec.reshape(half, per, ATTN_DIAGS)]
    if carry:
        in_specs += [shard, shard]
        operands += [dk_in, dv_in]
    acc = _sds((half, rows, n), F32)
    return pl.pallas_call(
        body, name="attn_bwd", grid=(half, t // ATTN_ROWS),
        in_specs=in_specs,
        out_specs=[tile, shard, shard, pl.BlockSpec((per, CHUNK, BAND), lambda g, i: (g, 0, 0))],
        out_shape=[_sds((t, d), BF16), acc, acc, _sds((N_HEADS, CHUNK, BAND), F32)],
        scratch_shapes=[pltpu.VMEM((per, ATTN_ROWS, ATTN_WINDOW), F32)],
        compiler_params=_params(("arbitrary", "arbitrary")),
    )(*operands)


SKEW_PITCH = 640
SKEW = SKEW_PITCH + 1
SKEW_LANES = -(-SKEW // LANES) * LANES


def _skew_diagonals(dsc):
    h = dsc.shape[0]
    wide = jnp.pad(dsc, ((0, 0), (0, 0), (0, SKEW_PITCH - BAND))).reshape(h, CHUNK * SKEW_PITCH)
    wide = jnp.pad(wide, ((0, 0), (0, CHUNK))).reshape(h, CHUNK, SKEW)
    return jnp.pad(wide, ((0, 0), (0, 0), (0, SKEW_LANES - SKEW)))


def _rel_bias_grad(skewed):
    heads = skewed.shape[0]
    hb = SUBLANES

    def body(d_ref, o_ref):
        col = lax.broadcasted_iota(jnp.int32, (SKEW_LANES, N_REL), 0)
        bucket = lax.broadcasted_iota(jnp.int32, (SKEW_LANES, N_REL), 1)
        diag = jnp.where(col < BAND, col, col - SKEW)
        idx = jnp.clip(LEFT_PAD - diag, -(CHUNK - 1), MAX_REL) + (CHUNK - 1)
        oh = ((idx == bucket) & (col < SKEW)).astype(BF16)
        dv = jnp.sum(d_ref[...], axis=1)
        hi = dv.astype(BF16)
        rest = dv - hi.astype(F32)
        mid = rest.astype(BF16)
        lo = (rest - mid.astype(F32)).astype(BF16)
        acc = jnp.dot(hi, oh, preferred_element_type=F32)
        acc += jnp.dot(mid, oh, preferred_element_type=F32)
        acc += jnp.dot(lo, oh, preferred_element_type=F32)
        o_ref[...] = acc

    return pl.pallas_call(
        body, name="rel_bias_grad", grid=(heads // hb,),
        in_specs=[pl.BlockSpec((hb, CHUNK, SKEW_LANES), lambda i: (i, 0, 0))],
        out_specs=pl.BlockSpec((hb, N_REL), lambda i: (i, 0)),
        out_shape=_sds((heads, N_REL), F32),
        compiler_params=_params(("parallel",)),
    )(skewed)


def _sum_parts(parts):
    s_n, rows, c = parts.shape
    br = _row_block(rows, OPT_ROW_BLOCK)

    def body(p_ref, o_ref):
        acc = p_ref[0].astype(F32)
        for s in range(1, s_n):
            acc = acc + p_ref[s].astype(F32)
        o_ref[...] = acc

    return pl.pallas_call(
        body, name="sum_parts", grid=(rows // br,),
        in_specs=[pl.BlockSpec((s_n, br, c), lambda i: (0, i, 0))],
        out_specs=pl.BlockSpec((br, c), lambda i: (i, 0)),
        out_shape=_sds((rows, c), F32),
        compiler_params=_params(("parallel",)),
    )(parts)


def _adamw(own, own_idx, parts, w, m, v, row0=0, bufs=None, after=None):
    _, rows, c = own.shape
    s_n = 0 if parts is None else parts.shape[0]
    total = w.shape[0]
    br = _row_block(rows, OPT_ROW_BLOCK)
    assert row0 % br == 0 and (bufs is not None or (row0 == 0 and total == rows))
    b0 = row0 // br
    m_corr = 1.0 - ADAM_B1 ** ADAM_STEP
    v_corr = 1.0 - ADAM_B2 ** ADAM_STEP

    def body(idx_ref, own_ref, *refs):
        if s_n:
            p_ref, refs = refs[0], refs[1:]
        w_ref, m_ref, v_ref = refs[:3]
        g_ref, d_ref, nm_ref, nv_ref = refs[-4:]
        g = own_ref[...].astype(F32)
        for s in range(s_n):
            g = g + p_ref[s].astype(F32)
        nm = ADAM_B1 * m_ref[...] + (1.0 - ADAM_B1) * g
        nv = ADAM_B2 * v_ref[...] + (1.0 - ADAM_B2) * (g * g)
        g_ref[...] = g
        nm_ref[...] = nm
        nv_ref[...] = nv
        d_ref[...] = -ADAM_LR * ((nm / m_corr) / (jnp.sqrt(nv / v_corr) + ADAM_EPS) + ADAM_WD * w_ref[...])

    tile = pl.BlockSpec((br, c), lambda i, idx: (i + b0, 0))
    in_specs = [pl.BlockSpec((None, br, c), lambda i, idx: (idx[0], i, 0))]
    operands = [own_idx, own]
    if s_n:
        in_specs.append(pl.BlockSpec((s_n, br, c), lambda i, idx: (0, i, 0)))
        operands.append(parts)
    in_specs += [tile, tile, tile]
    operands += [w, m, v]
    aliases = {}
    if bufs is not None:
        aliases = {len(operands) + j: j for j in range(4)}
        in_specs += [_ANY] * 4
        operands += list(bufs)
    if after is not None:
        in_specs.append(_ANY)
        operands.append(after)
    out = _sds((total, c), F32)
    return pl.pallas_call(
        body, name="adamw",
        grid_spec=pltpu.PrefetchScalarGridSpec(
            num_scalar_prefetch=1, grid=(rows // br,), in_specs=in_specs,
            out_specs=[tile, tile, tile, tile]),
        out_shape=[out, out, out, out],
        input_output_aliases=aliases,
        compiler_params=_params(("parallel",)),
    )(*operands)


def _chip_sum(p, r1, core):
    half = N_DEV // 2
    c = p.shape[-1]
    rows = int(np.prod(p.shape[1:-1]))
    br = _row_block(rows, BIG_ROW_BLOCK)

    def body(core_ref, p_ref, r_ref, o_ref):
        o_ref[...] = (p_ref[...].astype(F32) + r_ref[...].astype(F32)).astype(BF16)

    out = pl.pallas_call(
        body, name="chip_sum",
        grid_spec=pltpu.PrefetchScalarGridSpec(
            num_scalar_prefetch=1, grid=(half, rows // br),
            in_specs=[pl.BlockSpec((None, None, br, c), lambda q, i, cr: (q, cr[0], i, 0)),
                      pl.BlockSpec((None, br, c), lambda q, i, cr: (q, i, 0))],
            out_specs=pl.BlockSpec((None, br, c), lambda q, i, cr: (q, i, 0))),
        out_shape=_sds((half, rows, c), BF16),
        compiler_params=_params(("parallel", "parallel")),
    )(core, p.reshape(half, 2, rows, c), r1.reshape(half, rows, c))
    return out.reshape((half,) + p.shape[1:])


def _position():
    return tuple(lax.axis_index(a) for a in MESH_AXES)


def _linear(px, py, pc):
    return 4 * px + 2 * py + pc


def _all_gather_small(v, after=()):
    rows, lanes = v.shape

    def body(x_ref, *rest):
        out_ref, send_sems, recv_sems, local_sem = rest[-4:]
        x, y, c = _position()
        me, sibling = (x, y, c), (x, y, 1 - c)
        chips = [(1 - x, y), (x, 1 - y), (1 - x, 1 - y)]

        def copy(k, block, to, src=None):
            dst = out_ref.at[_linear(*block)]
            return pltpu.make_async_remote_copy(
                src_ref=dst if src is None else src, dst_ref=dst,
                send_sem=send_sems.at[k], recv_sem=recv_sems.at[k],
                device_id=to, device_id_type=MESH_ID)

        mine = pltpu.make_async_copy(x_ref, out_ref.at[_linear(*me)], local_sem)
        mine.start()
        first = [copy(0, me, sibling, src=x_ref)]
        first += [copy(1 + j, me, (*chip, c), src=x_ref) for j, chip in enumerate(chips)]
        for cp in first:
            cp.start()
        passed = [copy(4 + j, (*chip, c), sibling) for j, chip in enumerate(chips)]
        for j, chip in enumerate(chips):
            copy(1 + j, (*chip, c), me).wait_recv()
            passed[j].start()
        copy(0, sibling, me).wait_recv()
        for j, chip in enumerate(chips):
            copy(4 + j, (*chip, 1 - c), me).wait_recv()
        for cp in first + passed:
            cp.wait_send()
        mine.wait()

    return pl.pallas_call(
        body, name="all_gather_small",
        out_shape=_sds((N_DEV, rows, lanes), v.dtype),
        in_specs=[_VMEM] + [_ANY] * len(after), out_specs=_VMEM,
        scratch_shapes=[pltpu.SemaphoreType.DMA((7,)), pltpu.SemaphoreType.DMA((7,)),
                        pltpu.SemaphoreType.DMA],
        compiler_params=pltpu.CompilerParams(vmem_limit_bytes=VMEM_LIMIT),
    )(v, *after)


_HBM = pl.BlockSpec(memory_space=pltpu.HBM)
_SEM = pl.BlockSpec(memory_space=pltpu.SEMAPHORE)
_EFFECT = pltpu.SideEffectType.DATAFLOW_SIDE_EFFECTING
_ALL_CHIPS = [(0, 0), (0, 1), (1, 0), (1, 1)]


def _other_chips(x, y):
    return [(1 - x, y), (x, 1 - y), (1 - x, 1 - y)]


def _in_hbm(a):
    return pltpu.with_memory_space_constraint(a, pltpu.HBM)


def _token():
    return _sds((SUBLANES, LANES), F32)


def _gather_ici_copy(ref, i, k, chip, c, block, send_sems, recv_sems):
    return pltpu.make_async_remote_copy(
        src_ref=ref.at[block], dst_ref=ref.at[block],
        send_sem=send_sems.at[3 * i + k], recv_sem=recv_sems.at[3 * i + k],
        device_id=(*chip, c), device_id_type=MESH_ID)


def _gather_ici_start(name, lands, after=None):
    n = len(lands)
    extra = [] if after is None else [after]

    def body(*refs):
        ins, send_sems, recv_sems, token = refs[:n], refs[-n - 3], refs[-n - 2], refs[-1]
        x, y, c = _position()
        me = _linear(x, y, c)
        for i in range(n):
            for k, chip in enumerate(_other_chips(x, y)):
                _gather_ici_copy(ins[i], i, k, chip, c, me, send_sems, recv_sems).start()
        token[...] = jnp.zeros_like(token)

    out = pl.pallas_call(
        body, name=name,
        out_shape=(pltpu.SemaphoreType.DMA((3 * n,)), pltpu.SemaphoreType.DMA((3 * n,)),
                   *[pltpu.HBM(a.shape, a.dtype) for a in lands], _token()),
        in_specs=[_HBM] * n + [_ANY] * len(extra), out_specs=(_SEM, _SEM, *[_HBM] * n, _VMEM),
        input_output_aliases={i: 2 + i for i in range(n)},
        compiler_params=pltpu.CompilerParams(has_side_effects=_EFFECT),
    )(*[_in_hbm(a) for a in lands], *extra)
    return out[0], out[1], list(out[2:2 + n]), out[-1]


def _gather_ici_wait(name, lands, send_sems, recv_sems, after):
    n = len(lands)

    def body(*refs):
        ins, ss, rs = refs[:n], refs[n], refs[n + 1]
        x, y, c = _position()
        me = _linear(x, y, c)
        for i in range(n):
            for k, chip in enumerate(_other_chips(x, y)):
                _gather_ici_copy(ins[i], i, k, chip, c, me, ss, rs).wait_send()
                _gather_ici_copy(ins[i], i, k, chip, c, _linear(*chip, c), ss, rs).wait_recv()

    out = pl.pallas_call(
        body, name=name,
        out_shape=[pltpu.HBM(a.shape, a.dtype) for a in lands],
        in_specs=[_HBM] * n + [_SEM, _SEM, _ANY], out_specs=[_HBM] * n,
        input_output_aliases={i: i for i in range(n)},
        compiler_params=pltpu.CompilerParams(has_side_effects=_EFFECT),
    )(*lands, send_sems, recv_sems, after)
    return list(out)


def _gather_d2d(lands):
    n = len(lands)

    def body(*refs):
        ins, outs, send_sems, recv_sems = refs[:n], refs[n:2 * n], refs[2 * n], refs[2 * n + 1]
        x, y, c = _position()

        def copy(i, q, core):
            block = _linear(*_ALL_CHIPS[q], core)
            return pltpu.make_async_remote_copy(
                src_ref=ins[i].at[block], dst_ref=outs[i].at[block],
                send_sem=send_sems.at[i, q], recv_sem=recv_sems.at[i, q],
                device_id=(x, y, 1 - c), device_id_type=MESH_ID)

        sent = [copy(i, q, c) for i in range(n) for q in range(len(_ALL_CHIPS))]
        for cp in sent:
            cp.start()
        for i in range(n):
            for q in range(len(_ALL_CHIPS)):
                copy(i, q, 1 - c).wait_recv()
        for cp in sent:
            cp.wait_send()

    return pl.pallas_call(
        body, name="gather_d2d",
        out_shape=[_sds(a.shape, a.dtype) for a in lands],
        in_specs=[_ANY] * n, out_specs=[_ANY] * n,
        input_output_aliases={i: i for i in range(n)},
        scratch_shapes=[pltpu.SemaphoreType.DMA((n, 4)), pltpu.SemaphoreType.DMA((n, 4))],
    )(*lands)


def _partials_d2d(parts):
    n = len(parts)
    half = N_DEV // 2

    def body(*refs):
        ins, outs, send_sems, recv_sems = refs[:n], refs[n:2 * n], refs[2 * n], refs[2 * n + 1]
        x, y, c = _position()

        def copy(i, q):
            return pltpu.make_async_remote_copy(
                src_ref=ins[i].at[_linear(*_ALL_CHIPS[q], 1 - c)], dst_ref=outs[i].at[q],
                send_sem=send_sems.at[i, q], recv_sem=recv_sems.at[i, q],
                device_id=(x, y, 1 - c), device_id_type=MESH_ID)

        sent = [copy(i, q) for i in range(n) for q in range(half)]
        for cp in sent:
            cp.start()
        for cp in sent:
            cp.wait_recv()
        for cp in sent:
            cp.wait_send()

    return pl.pallas_call(
        body, name="partials_d2d",
        out_shape=[_sds((half,) + p.shape[1:], p.dtype) for p in parts],
        in_specs=[_ANY] * n, out_specs=[_ANY] * n,
        scratch_shapes=[pltpu.SemaphoreType.DMA((n, half)), pltpu.SemaphoreType.DMA((n, half))],
    )(*parts)


def _partials_peers(x, y, c, direct):
    chips = _other_chips(x, y)
    if not direct:
        return [((*ch, c), 2 * ch[0] + ch[1]) for ch in chips]
    peers = [(x, y, 1 - c)] + [(*ch, c) for ch in chips] + [(*ch, 1 - c) for ch in chips]
    return [(p, _linear(*p)) for p in peers]


def _partials_copies(srcs, lands, send_sems, recv_sems, direct):
    x, y, c = _position()
    peers = _partials_peers(x, y, c, direct)
    return [pltpu.make_async_remote_copy(
        src_ref=srcs[i].at[block], dst_ref=lands[i].at[k],
        send_sem=send_sems.at[len(peers) * i + k], recv_sem=recv_sems.at[len(peers) * i + k],
        device_id=peer, device_id_type=MESH_ID)
        for i in range(len(srcs)) for k, (peer, block) in enumerate(peers)]


def _partials_send_start(name, srcs, lands, direct, after=None):
    n = len(srcs)
    n_sem = n * (N_DEV - 1 if direct else len(_ALL_CHIPS) - 1)

    def body(*refs):
        _, send_sems, recv_sems = refs[:2 * n], refs[-2 * n - 3], refs[-2 * n - 2]
        for cp in _partials_copies(refs[:n], refs[n:2 * n], send_sems, recv_sems, direct):
            cp.start()
        refs[-1][...] = jnp.zeros_like(refs[-1])

    both = list(srcs) + list(lands)
    extra = [] if after is None else [after]
    out = pl.pallas_call(
        body, name=name,
        out_shape=(pltpu.SemaphoreType.DMA((n_sem,)), pltpu.SemaphoreType.DMA((n_sem,)),
                   *[pltpu.HBM(a.shape, a.dtype) for a in both], _token()),
        in_specs=[_HBM] * (2 * n) + [_ANY] * len(extra), out_specs=(_SEM, _SEM, *[_HBM] * (2 * n), _VMEM),
        input_output_aliases={i: 2 + i for i in range(2 * n)},
        compiler_params=pltpu.CompilerParams(has_side_effects=_EFFECT),
    )(*[_in_hbm(a) for a in both], *extra)
    return out[0], out[1], list(out[2:2 + n]), list(out[2 + n:2 + 2 * n]), out[-1]


def _partials_send_wait(name, srcs, lands, send_sems, recv_sems, direct, after):
    n = len(srcs)

    def body(*refs):
        for cp in _partials_copies(refs[:n], refs[n:2 * n], refs[2 * n], refs[2 * n + 1], direct):
            cp.wait_send()
            cp.wait_recv()

    both = list(srcs) + list(lands)
    out = pl.pallas_call(
        body, name=name,
        out_shape=[pltpu.HBM(a.shape, a.dtype) for a in both],
        in_specs=[_HBM] * (2 * n) + [_SEM, _SEM, _ANY], out_specs=[_HBM] * (2 * n),
        input_output_aliases={i: i for i in range(2 * n)},
        compiler_params=pltpu.CompilerParams(has_side_effects=_EFFECT),
    )(*both, send_sems, recv_sems, after)
    return list(out[:n]), list(out[n:])


def _pack(arrs):
    flat = jnp.concatenate([a.reshape(-1).astype(F32) for a in arrs])
    block = OPT_ROW_BLOCK if flat.shape[0] > OPT_ROW_BLOCK * LANES else SUBLANES
    pad = (-flat.shape[0]) % (block * LANES)
    if pad:
        flat = jnp.concatenate([flat, jnp.zeros((pad,), F32)])
    return flat.reshape(-1, LANES)


def _unpack(packed, shapes, lead=()):
    flat = packed.reshape(lead + (-1,))
    out, off = [], 0
    for s in shapes:
        size = int(np.prod(s))
        out.append(flat[..., off:off + size].reshape(lead + tuple(s)))
        off += size
    return out


def _unshard_last(g):
    nd = g.ndim
    perm = tuple(range(1, nd - 1)) + (0, nd - 1)
    t = jnp.transpose(g, perm)
    return t.reshape(t.shape[:-2] + (N_DEV * g.shape[-1],))


def kernel(x, c, w_ada, b_ada, ln_g, ln_b, ffn_gu, ffn_down, gmlp_w_in, gmlp_b_in, gmlp_ln_g, gmlp_ln_b, gmlp_w_s, gmlp_b_s, gmlp_w_out, w_ada_kv, b_ada_kv, w_kv, attn_w_q, attn_rel_bias, attn_w_o, loss_target, m_w_ada, m_b_ada, m_ln_g, m_ln_b, m_ffn_gu, m_ffn_down, m_gmlp_w_in, m_gmlp_b_in, m_gmlp_ln_g, m_gmlp_ln_b, m_gmlp_w_s, m_gmlp_b_s, m_gmlp_w_out, m_w_ada_kv, m_b_ada_kv, m_w_kv, m_attn_w_q, m_attn_rel_bias, m_attn_w_o, v_w_ada, v_b_ada, v_ln_g, v_ln_b, v_ffn_gu, v_ffn_down, v_gmlp_w_in, v_gmlp_b_in, v_gmlp_ln_g, v_gmlp_ln_b, v_gmlp_w_s, v_gmlp_b_s, v_gmlp_w_out, v_w_ada_kv, v_b_ada_kv, v_w_kv, v_attn_w_q, v_attn_rel_bias, v_attn_w_o):
    weights = dict(w_ada=w_ada, b_ada=b_ada, ln_g=ln_g, ln_b=ln_b, ffn_gu=ffn_gu, ffn_down=ffn_down,
                   gmlp_w_in=gmlp_w_in, gmlp_b_in=gmlp_b_in, gmlp_ln_g=gmlp_ln_g, gmlp_ln_b=gmlp_ln_b,
                   gmlp_w_s=gmlp_w_s, gmlp_b_s=gmlp_b_s, gmlp_w_out=gmlp_w_out, w_ada_kv=w_ada_kv,
                   b_ada_kv=b_ada_kv, w_kv=w_kv, attn_w_q=attn_w_q, attn_rel_bias=attn_rel_bias,
                   attn_w_o=attn_w_o)
    mom1 = dict(w_ada=m_w_ada, b_ada=m_b_ada, ln_g=m_ln_g, ln_b=m_ln_b, ffn_gu=m_ffn_gu, ffn_down=m_ffn_down,
                gmlp_w_in=m_gmlp_w_in, gmlp_b_in=m_gmlp_b_in, gmlp_ln_g=m_gmlp_ln_g, gmlp_ln_b=m_gmlp_ln_b,
                gmlp_w_s=m_gmlp_w_s, gmlp_b_s=m_gmlp_b_s, gmlp_w_out=m_gmlp_w_out, w_ada_kv=m_w_ada_kv,
                b_ada_kv=m_b_ada_kv, w_kv=m_w_kv, attn_w_q=m_attn_w_q, attn_rel_bias=m_attn_rel_bias,
                attn_w_o=m_attn_w_o)
    mom2 = dict(w_ada=v_w_ada, b_ada=v_b_ada, ln_g=v_ln_g, ln_b=v_ln_b, ffn_gu=v_ffn_gu, ffn_down=v_ffn_down,
                gmlp_w_in=v_gmlp_w_in, gmlp_b_in=v_gmlp_b_in, gmlp_ln_g=v_gmlp_ln_g, gmlp_ln_b=v_gmlp_ln_b,
                gmlp_w_s=v_gmlp_w_s, gmlp_b_s=v_gmlp_b_s, gmlp_w_out=v_gmlp_w_out, w_ada_kv=v_w_ada_kv,
                b_ada_kv=v_b_ada_kv, w_kv=v_w_kv, attn_w_q=v_attn_w_q, attn_rel_bias=v_attn_rel_bias,
                attn_w_o=v_attn_w_o)
    order = list(weights)

    x = x[0]
    target = loss_target[0]
    t, d = x.shape
    n_mod = w_ada.shape[-1] * N_DEV // d
    mod_w = w_ada.shape[-1]
    kv_w = w_ada_kv.shape[-1]
    n_b = DEPTH - N_A
    me = _linear(*_position())

    l2 = DEPTH * 2
    big = dict(
        ffn_gu=ffn_gu.reshape((l2,) + ffn_gu.shape[2:]),
        ffn_down=ffn_down.reshape((l2,) + ffn_down.shape[2:]),
        gmlp_w_in=gmlp_w_in, gmlp_w_out=gmlp_w_out, w_kv=w_kv[None],
        attn_w_q=attn_w_q, attn_w_o=attn_w_o)
    big_names = list(big)
    core = lax.axis_index("c").astype(jnp.int32).reshape(1)
    chip = (2 * lax.axis_index("x") + lax.axis_index("y")).astype(jnp.int32).reshape(1)

    fwd_groups = [
        {"ffn_gu": (0, 1), "ffn_down": (0, 1)},
        {"gmlp_w_in": (0, 1), "gmlp_w_out": (0, 1)},
        {"ffn_gu": (1, 1), "ffn_down": (1, 1)},
        {"ffn_gu": (2, 1), "ffn_down": (2, 1)},
        {"gmlp_w_in": (1, 1), "gmlp_w_out": (1, 1)},
        {"ffn_gu": (3, 1), "ffn_down": (3, 1), "w_kv": (0, 1)},
        {"ffn_gu": (4, 1), "ffn_down": (4, 1)},
        {"attn_w_q": (0, 1), "attn_w_o": (0, 1)},
        {"ffn_gu": (5, 1), "ffn_down": (5, 1)},
        {"ffn_gu": (6, 2), "ffn_down": (6, 2), "attn_w_q": (1, 1), "attn_w_o": (1, 1)},
    ]
    bwd_groups = []
    for l in range(DEPTH):
        g = {"ffn_gu": (2 * l, 2), "ffn_down": (2 * l, 2)}
        if l < N_A:
            g.update({"gmlp_w_in": (l, 1), "gmlp_w_out": (l, 1)})
        else:
            g.update({"attn_w_q": (l - N_A, 1), "attn_w_o": (l - N_A, 1)})
        if l == N_A - 1:
            g["w_kv"] = (0, 1)
        bwd_groups.append(g)

    def slot_of(groups, name, slot):
        for gi, g in enumerate(groups):
            if name in g and g[name][0] <= slot < g[name][0] + g[name][1]:
                return gi, slot - g[name][0]
        raise KeyError((name, slot))

    def start_group(gi, after=None):
        lands = []
        for name, (s0, cnt) in fwd_groups[gi].items():
            shard = big[name][s0:s0 + cnt].astype(BF16)
            land = lax.empty((N_DEV,) + shard.shape, BF16)
            lands.append(lax.dynamic_update_slice(land, shard[None], (me,) + (0,) * shard.ndim))
        return _gather_ici_start(f"gather_ici_start_{gi}", lands, after)

    gathered = [None] * len(fwd_groups)

    def land_group(gi, after):
        send_sems, recv_sems, lands, _ = flights[gi]
        lands = _gather_ici_wait(f"gather_ici_wait_{gi}", lands, send_sems, recv_sems, after)
        gathered[gi] = dict(zip(fwd_groups[gi], _gather_d2d(lands)))

    def weight(name, slot):
        gi, local = slot_of(fwd_groups, name, slot)
        return gathered[gi][name], local

    swapped = ("ffn_gu",)

    def grad_shape(name):
        s = big[name].shape[1:]
        return s[:-2] + (s[-1], s[-2]) if name in swapped else s

    partial = [{name: lax.empty((N_DEV, cnt) + grad_shape(name), BF16) for name, (_, cnt) in g.items()}
               for g in bwd_groups]

    c_all = _all_gather_small(_pack([c]))
    c_all = _unpack(c_all, [(d,)], lead=(N_DEV,))[0]
    c4 = _as4(c_all)
    mod_part = _matmul("ada_fwd", c4, w_ada[:, None], (DEPTH, 1, N_DEV, mod_w), F32, a_silu=True)
    kv_part = _matmul("ada_kv_fwd", c4, _as4(w_ada_kv), (1, 1, N_DEV, kv_w), F32, a_silu=True)
    small_shapes = [mod_part.shape, kv_part.shape, ln_g.shape, ln_b.shape, gmlp_b_in.shape,
                    gmlp_ln_g.shape, gmlp_ln_b.shape, attn_rel_bias.shape]
    small = _all_gather_small(_pack([mod_part, kv_part, ln_g, ln_b, gmlp_b_in, gmlp_ln_g, gmlp_ln_b,
                                     attn_rel_bias]))
    flights = [start_group(0, after=small)]
    flights += [start_group(gi, after=flights[0][3]) for gi in range(1, len(fwd_groups))]
    start_token = sum(f[3][0, 0] for f in flights)
    (mod_g, kvm_g, ln_g_g, ln_b_g, b_in_g, gln_g_g, gln_b_g, rel_g) = _unpack(small, small_shapes, lead=(N_DEV,))
    mod_mine = lax.dynamic_index_in_dim(mod_g[:, :, 0], me, axis=2, keepdims=False)
    mod = _unshard_last(mod_mine) + b_ada
    mod = mod.reshape(DEPTH, n_mod, 1, d)
    kvm_mine = lax.dynamic_index_in_dim(kvm_g[:, 0, 0], me, axis=1, keepdims=False)
    mkv = (_unshard_last(kvm_mine) + b_ada_kv).reshape(2, 1, d)
    ln_g_f = _unshard_last(ln_g_g)
    ln_b_f = _unshard_last(ln_b_g)
    half = N_DEV // 2
    b_in_f = jnp.transpose(b_in_g, (1, 0, 2))[:, :, None, :]
    gln_g_f = _unshard_last(gln_g_g).reshape(N_A, half, 1, -1)
    gln_b_f = _unshard_last(gln_b_g).reshape(N_A, half, 1, -1)
    rel_f = _unshard_last(rel_g)

    def shard_act(a):
        return a.reshape(a.shape[0], a.shape[2], a.shape[3])

    def grad_into(name, slot, mm):
        gi, local = slot_of(bwd_groups, name, slot)
        partial[gi][name] = mm(partial[gi][name], local)

    def ffn_fwd(h, lw):
        w_gu, l_gu = weight("ffn_gu", lw)
        w_dn, l_dn = weight("ffn_down", lw)
        gu, a = _ffn_up_fwd(h, w_gu, l_gu)
        y = _matmul("ffn_down_fwd", a[:, None], w_dn, (1, 1, t, d), F32, lb=l_dn, b_merge=2, reduce=True)
        return y[0, 0], (gu, a)

    def ffn_bwd(dy, h, saved, lw):
        gu, a = saved
        w_gu, l_gu = weight("ffn_gu", lw)
        w_dn, l_dn = weight("ffn_down", lw)
        dgu = _ffn_down_bwd_a(dy, w_dn, l_dn, gu).reshape((N_DEV,) + gu.shape[2:])
        grad_into("ffn_down", lw, lambda buf, lo: _matmul(
            "ffn_down_bwd_w", a[:, None], _as4(dy), buf.shape, BF16, ta=True, lo=lo, out_merge=2, out_buf=buf))
        dh = _matmul("ffn_gu_bwd_a", dgu[:, None], w_gu, (1, 1, t, d), F32, lb=l_gu, tb=True, reduce=True)
        grad_into("ffn_gu", lw, lambda buf, lo: _matmul(
            "ffn_gu_bwd_w", dgu[:, None], _as4(h), buf.shape, BF16, ta=True, lo=lo, out_buf=buf))
        return dh[0, 0], {}

    def gmlp_params(l):
        return (b_in_f[l], gln_g_f[l], gln_b_f[l], gmlp_w_s[l], gmlp_b_s[l][:, :, None])

    def gmlp_fwd(h, l):
        w_in, l_in = weight("gmlp_w_in", l)
        w_out, l_out = weight("gmlp_w_out", l)
        n = w_in.shape[-1]
        zpre = _matmul("gmlp_in_fwd", _as4(h), w_in, (N_DEV, 1, t, n), F32, lb=l_in)
        gated = _gmlp_mid_fwd(shard_act(zpre), *gmlp_params(l))
        y = _matmul("gmlp_out_fwd", gated[:, None], w_out, (1, 1, t, d), F32, lb=l_out, b_merge=2, reduce=True)
        return y[0, 0], (zpre, gated)

    def gmlp_bwd(dy, h, saved, l):
        zpre, gated = saved
        w_in, l_in = weight("gmlp_w_in", l)
        w_out, l_out = weight("gmlp_w_out", l)
        n = w_in.shape[-1]
        dgated = _matmul("gmlp_out_bwd_a", _as4(dy), w_out, (half, 1, t, n), F32, lb=l_out, b_merge=2, tb=True)
        grad_into("gmlp_w_out", l, lambda buf, lo: _matmul(
            "gmlp_out_bwd_w", gated[:, None], _as4(dy), buf.shape, BF16, ta=True, lo=lo, out_merge=2, out_buf=buf))
        dz, dws, dbs, dlng, dlnb, dbin = _gmlp_mid_bwd(shard_act(zpre), shard_act(dgated), *gmlp_params(l))
        dh = _matmul("gmlp_in_bwd_a", dz[:, None], w_in, (1, 1, t, d), F32, lb=l_in, tb=True, reduce=True)
        grad_into("gmlp_w_in", l, lambda buf, lo: _matmul(
            "gmlp_in_bwd_w", _as4(h), dz[:, None], buf.shape, BF16, ta=True, lo=lo, out_buf=buf))
        small_grads = dict(gmlp_w_s=dws, gmlp_b_s=dbs[:, :, 0], gmlp_ln_g=dlng.reshape(-1),
                           gmlp_ln_b=dlnb.reshape(-1), gmlp_b_in=dbin.reshape(-1))
        return dh[0, 0], small_grads

    def attn_fwd(h, j, kvp):
        rel_vec = _rel_vector(rel_f[j])
        w_q, l_q = weight("attn_w_q", j)
        w_o, l_o = weight("attn_w_o", j)
        q = _matmul("attn_q_fwd", _as4(h), w_q, (1, 1, t, d), BF16, lb=l_q, b_merge=N_DEV, reduce=True)[0, 0]
        o = _attn_fwd(q, kvp, rel_vec)
        y = _matmul("attn_o_fwd", _as4(o), w_o, (1, 1, t, d), F32, lb=l_o, b_merge=N_DEV, reduce=True)
        return y[0, 0], (q, o, rel_vec)

    def attn_bwd(dy, h, saved, j, kvp, dkv_acc):
        q, o, rel_vec = saved
        w_q, l_q = weight("attn_w_q", j)
        w_o, l_o = weight("attn_w_o", j)
        do = _matmul("attn_o_bwd_a", _as4(dy), w_o, (1, 1, t, d), BF16, lb=l_o, b_merge=N_DEV, tb=True)[0, 0]
        grad_into("attn_w_o", j, lambda buf, lo: _matmul(
            "attn_o_bwd_w", _as4(o), _as4(dy), buf.shape, BF16, ta=True, lo=lo, out_merge=N_DEV, out_buf=buf))
        dq, dk, dv, dsc = _attn_bwd(q, do, kvp, rel_vec, *dkv_acc)
        drel = _rel_bias_grad(_skew_diagonals(dsc))
        dh = _matmul("attn_q_bwd_a", _as4(dq), w_q, (1, 1, t, d), F32, lb=l_q, b_merge=N_DEV, tb=True)
        grad_into("attn_w_q", j, lambda buf, lo: _matmul(
            "attn_q_bwd_w", _as4(h), _as4(dq), buf.shape, BF16, ta=True, lo=lo, out_merge=N_DEV, out_buf=buf))
        return dh[0, 0], dict(attn_rel_bias=drel, dkv=(dk, dv))

    tape = []
    kvp = None
    kv_tape = None
    first_use = {(0, 0): 0, (0, 1): 1, (0, 2): 2, (1, 0): 3, (1, 1): 4, (1, 2): 5, (2, 0): 6, (2, 1): 7,
                 (2, 2): 8, (3, 0): 9}
    h = _modulate(x, mod[0, 1], mod[0, 0] + start_token)
    for l in range(DEPTH):
        for i in range(3):
            if (l, i) in first_use:
                land_group(first_use[l, i], x)
            scl, gate = mod[l, 3 * i + 1], mod[l, 3 * i + 2]
            wgt = 1.0 if i == 1 else 0.5
            gw = wgt * (1.0 + gate)
            if i != 1:
                y, saved = ffn_fwd(h, 2 * l + i // 2)
            elif l < N_A:
                y, saved = gmlp_fwd(h, l)
            else:
                y, saved = attn_fwd(h, l - N_A, kvp)
            nl, ni = (l, i + 1) if i < 2 else (l + 1, 0)
            readers = [(mod[nl, 3 * ni + 1], mod[nl, 3 * ni])] if nl < DEPTH else []
            shared_kv = (l, i) == (N_A - 1, 2)
            if shared_kv:
                readers.append((mkv[1], mkv[0]))
            outs = _ln_res_fwd(x, y, gw, ln_g_f[l, i][None], ln_b_f[l, i][None], readers)
            tape.append((x, h, y, gw, scl, saved))
            x = outs[0]
            h = outs[1] if nl < DEPTH else None
            if shared_kv:
                hkv = outs[-1]
                w_kvg, l_kv = weight("w_kv", 0)
                n = w_kvg.shape[-1]
                kv = _matmul("kv_fwd", _as4(hkv), w_kvg, (N_DEV, 1, t, n), BF16, lb=l_kv)
                kvp = jnp.pad(shard_act(kv), ((0, 0), (LEFT_PAD, 0), (0, 0)))
                kv_tape = hkv

    loss_part, dx = _loss_head(x, target)
    loss = lax.psum(loss_part[0, 0], MESH_AXES)

    d_mod = [[None] * n_mod for _ in range(DEPTH)]
    d_ln_g = [[None] * 3 for _ in range(DEPTH)]
    d_ln_b = [[None] * 3 for _ in range(DEPTH)]
    small_grads = {k: [None] * N_A for k in ("gmlp_w_s", "gmlp_b_s", "gmlp_ln_g", "gmlp_ln_b", "gmlp_b_in")}
    d_rel = [None] * n_b
    dkv_acc = ()
    d_mkv = None
    reductions = [None] * DEPTH
    sent_token = None
    readers = []
    for l in reversed(range(DEPTH)):
        if l == N_A - 1:
            hkv = kv_tape
            w_kvg, l_kv = weight("w_kv", 0)
            dkv = jnp.concatenate(dkv_acc)[:, LEFT_PAD:, :].astype(BF16)[:, None]
            dhkv = _matmul("kv_bwd_a", dkv, w_kvg, (1, 1, t, d), F32, lb=l_kv, tb=True, reduce=True)[0, 0]
            grad_into("w_kv", 0, lambda buf, lo: _matmul(
                "kv_bwd_w", _as4(hkv), dkv, buf.shape, BF16, ta=True, lo=lo, out_buf=buf))
            readers.append((dhkv, mkv[1], None))
        for i in reversed(range(3)):
            x_in, h, y, gw, scl, saved = tape[3 * l + i]
            wgt = 1.0 if i == 1 else 0.5
            if sent_token is not None:
                gw = gw + sent_token
                sent_token = None
            res = _ln_res_bwd(x_in, y, gw, ln_g_f[l, i][None], ln_b_f[l, i][None], dx,
                              [(r[0], r[1]) for r in readers])
            dx_res, dy, dgw, dg, db = res[:5]
            for k, (_, _, slot) in enumerate(readers):
                dscl_k, dshift_k = res[5 + 2 * k][0], res[6 + 2 * k][0]
                if slot is None:
                    d_mkv = jnp.concatenate([dshift_k, dscl_k])
                else:
                    d_mod[slot[0]][slot[1]], d_mod[slot[0]][slot[1] + 1] = dshift_k, dscl_k
            d_ln_g[l][i], d_ln_b[l][i] = dg[0], db[0]
            if i != 1:
                dh, extra = ffn_bwd(dy, h, saved, 2 * l + i // 2)
            elif l < N_A:
                dh, extra = gmlp_bwd(dy, h, saved, l)
                for k, g in extra.items():
                    small_grads[k][l] = g
            else:
                dh, extra = attn_bwd(dy, h, saved, l - N_A, kvp, dkv_acc)
                d_rel[l - N_A] = extra["attn_rel_bias"]
                dkv_acc = extra["dkv"]
            d_mod[l][3 * i + 2] = wgt * dgw[0]
            dx = dx_res
            readers = [(dh, scl, (l, 3 * i))]
        if l > 0:
            srcs = [partial[l][k] for k in bwd_groups[l]]
            lands = [lax.empty((N_DEV - 1,) + s.shape[1:], BF16) for s in srcs]
            reductions[l] = _partials_send_start(f"partials_send_start_{l}", srcs, lands, True)
            sent_token = reductions[l][4][0, 0]
    (dh, scl, _), = readers
    dx, dscl, dshift = _mod_bwd(dx, dh, tape[0][0], scl)
    d_mod[0][0], d_mod[0][1] = dshift[0], dscl[0]
    grad_x = dx[None]

    d_mod_arr = jnp.stack([jnp.concatenate(r) for r in d_mod])
    small_part = dict(
        b_ada=d_mod_arr, b_ada_kv=d_mkv,
        ln_g=jnp.stack([jnp.stack(r) for r in d_ln_g]), ln_b=jnp.stack([jnp.stack(r) for r in d_ln_b]),
        gmlp_b_in=jnp.stack(small_grads["gmlp_b_in"]), gmlp_ln_g=jnp.stack(small_grads["gmlp_ln_g"]),
        gmlp_ln_b=jnp.stack(small_grads["gmlp_ln_b"]), gmlp_w_s=jnp.stack(small_grads["gmlp_w_s"]),
        gmlp_b_s=jnp.stack(small_grads["gmlp_b_s"]), attn_rel_bias=jnp.stack(d_rel))
    small_names = list(small_part)
    sp_shapes = [small_part[k].shape for k in small_names]
    sp_all = _all_gather_small(_pack([small_part[k] for k in small_names]),
                               after=[partial[0][k] for k in bwd_groups[0]])

    from_sibling = _partials_d2d([partial[0][k] for k in bwd_groups[0]])
    sums = [_chip_sum(partial[0][k], r1, core) for k, r1 in zip(bwd_groups[0], from_sibling)]
    lands = [lax.empty((len(_ALL_CHIPS) - 1,) + s.shape[1:], BF16) for s in sums]
    reductions[0] = _partials_send_start("partials_send_start_0", sums, lands, False, after=sp_all)
    sent_token = reductions[0][4][0, 0]
    c4 = c4 + sent_token

    sp_sum = _sum_parts(sp_all)
    full_grads = dict(zip(small_names, _unpack(sp_sum, sp_shapes)))
    per_dev = dict(zip(small_names, _unpack(sp_all, sp_shapes, lead=(N_DEV,))))

    def my_cols(a, width):
        return lax.dynamic_slice_in_dim(a, me * width, width, axis=a.ndim - 1)

    grads = {}
    grads["b_ada"] = full_grads["b_ada"]
    grads["b_ada_kv"] = full_grads["b_ada_kv"]
    grads["gmlp_w_s"] = full_grads["gmlp_w_s"]
    grads["gmlp_b_s"] = full_grads["gmlp_b_s"]
    for k in ("ln_g", "ln_b", "gmlp_b_in", "gmlp_ln_g", "gmlp_ln_b", "attn_rel_bias"):
        grads[k] = my_cols(full_grads[k], weights[k].shape[-1])

    dmod_cols = jnp.transpose(my_cols(per_dev["b_ada"], mod_w), (1, 0, 2))[:, None]
    grads["w_ada"] = _matmul("ada_bwd_w", c4, dmod_cols, (DEPTH, 1, d, mod_w), F32, ta=True,
                             a_silu=True)[:, 0]
    dkv_cols = my_cols(per_dev["b_ada_kv"], kv_w)[None, None]
    grads["w_ada_kv"] = _matmul("ada_kv_bwd_w", c4, dkv_cols, (1, 1, d, kv_w), F32, ta=True,
                                a_silu=True)[0, 0]

    delta, new_m, new_v = {}, {}, {}
    first = jnp.zeros((1,), jnp.int32)

    def flat2(a, cols):
        return a.reshape(-1, cols)

    done = None
    for k in ("w_ada", "w_ada_kv"):
        w = weights[k]
        cols = w.shape[-1]
        res = _adamw(grads[k].reshape(1, -1, cols), first, None, flat2(w, cols), flat2(mom1[k], cols),
                     flat2(mom2[k], cols), after=done)
        grads[k], delta[k], new_m[k], new_v[k] = (a.reshape(w.shape) for a in res)
        done = res[0][:SUBLANES, :LANES]

    tiny = [k for k in order if k not in delta and k not in big_names]
    tiny_shapes = [weights[k].shape for k in tiny]
    tiny_out = _adamw((_pack([grads[k] for k in tiny]) + sent_token)[None], first, None,
                      _pack([weights[k] for k in tiny]), _pack([mom1[k] for k in tiny]),
                      _pack([mom2[k] for k in tiny]), after=done)
    for dst, arr in zip((grads, delta, new_m, new_v), tiny_out):
        for k, val in zip(tiny, _unpack(arr, tiny_shapes)):
            dst[k] = val

    def opt_view(k, a):
        a = jnp.swapaxes(a, -1, -2) if k in swapped else a
        return a.reshape(-1, a.shape[-1])

    def opt_unview(k, a):
        s = weights[k].shape
        return jnp.swapaxes(a.reshape(s[:-2] + (s[-1], s[-2])), -1, -2) if k in swapped else a.reshape(s)

    bufs = {k: [lax.empty(opt_view(k, weights[k]).shape, F32) for _ in range(4)] for k in big_names}
    done = tiny_out[0]
    me_idx = me.astype(jnp.int32).reshape(1)
    for l in reversed(range(DEPTH)):
        send_sems, recv_sems, srcs, lands, _ = reductions[l]
        srcs, lands = _partials_send_wait(f"partials_send_wait_{l}", srcs, lands, send_sems, recv_sems, l > 0, done)
        for k, own, got in zip(bwd_groups[l], srcs, lands):
            cols = own.shape[-1]
            slot_rows = int(np.prod(own.shape[2:-1]))
            bufs[k] = _adamw(own.reshape(own.shape[0], -1, cols), me_idx if l > 0 else chip,
                             got.reshape(got.shape[0], -1, cols),
                             opt_view(k, weights[k]), opt_view(k, mom1[k]), opt_view(k, mom2[k]),
                             row0=bwd_groups[l][k][0] * slot_rows, bufs=bufs[k], after=done)
            done = bufs[k][0][:SUBLANES, :LANES]
    for k in big_names:
        grads[k], delta[k], new_m[k], new_v[k] = (opt_unview(k, b) for b in bufs[k])

    return (loss, grad_x, *[grads[k] for k in order], *[delta[k] for k in order],
            *[new_m[k] for k in order], *[new_v[k] for k in order])
```

```python
import functools

import numpy as np
import jax
import jax.numpy as jnp
from jax import lax
from jax.experimental import pallas as pl
from jax.experimental.pallas import tpu as pltpu

F32 = jnp.float32
BF16 = jnp.bfloat16
MESH_AXES = ("x", "y", "c")
N_DEV = 8
MESH_ID = pl.DeviceIdType.MESH

DEPTH = 4
N_A = 2
CHUNK = 64
N_HEADS = 16
LEFT_CHUNKS = 8
BAND = (LEFT_CHUNKS + 1) * CHUNK
LEFT_PAD = LEFT_CHUNKS * CHUNK
MAX_REL = 4 * CHUNK
N_REL = (CHUNK - 1) + MAX_REL + 1
GMLP_WINDOW = 128
GMLP_GROUPS = 8
ALPHA = (2.0 * DEPTH) ** 0.25
LN_EPS = 1e-5
ADAM_LR = 0.001
ADAM_B1 = 0.9
ADAM_B2 = 0.999
ADAM_EPS = 1e-08
ADAM_WD = 0.01
ADAM_STEP = 10

V7X_VMEM_BYTES = 64 * 1024 * 1024
VMEM_LIMIT = V7X_VMEM_BYTES - 8 * 1024 * 1024
LANES = 128
SUBLANES = 8
MM_BLOCK = 2048
BIG_ROW_BLOCK = 1024
ROW_BLOCK = 512
OPT_ROW_BLOCK = 256

_ANY = pl.BlockSpec(memory_space=pl.ANY)
_VMEM = pl.BlockSpec(memory_space=pltpu.VMEM)


def _params(sem=None):
    return pltpu.CompilerParams(dimension_semantics=sem, vmem_limit_bytes=VMEM_LIMIT)


def _row_block(rows, target):
    for d in range(min(rows, target), 0, -1):
        if rows % d == 0 and (d % SUBLANES == 0 or d == rows):
            return d
    return rows


def _matmul(name, a, b, out_shape4, out_dtype, *, la=0, lb=0, lo=0, ta=False, tb=False,
            reduce=False, b_merge=1, out_merge=1, out_buf=None, a_silu=False):
    ja_n, _, a_r, a_c = a.shape
    jb_n, _, b_r, b_c = b.shape
    jo_n, _, o_r, o_c = out_shape4
    m_tot = a_c if ta else a_r
    k_a = a_r if ta else a_c
    b_rows = b_merge * b_r
    k_c = b_c if tb else b_rows
    n = b_rows if tb else b_c
    n_chunks = (jb_n // b_merge) if reduce else 1
    natural_k = reduce and ja_n == 1
    assert n == o_c, (name, n, o_c)
    assert k_a ==(k_c * n_chunks if natural_k else k_c), (name, k_a, k_c, n_chunks)
    bk = k_c if (k_c <= MM_BLOCK or (b_merge > 1 and not tb)) else MM_BLOCK
    assert k_c % bk == 0
    nkk = k_c // bk
    kg = 2 if (reduce and ja_n > 1 and nkk == 1 and not ta and n_chunks % 2 == 0) else 1
    nk = n_chunks * nkk // kg
    m_out = out_merge * o_r
    assert m_tot == m_out, (name, m_tot, m_out)
    bm = m_tot if (m_tot <= MM_BLOCK or out_merge > 1) else MM_BLOCK
    assert m_tot % bm == 0
    jo_blocks = jo_n // out_merge

    def a_index(j, m, k):
        kj, kk = k // nkk, k % nkk
        ja = 0 if ja_n == 1 else (kj if reduce else j)
        ke = kk + kj * nkk if natural_k else kk
        return (ja, la, ke, m) if ta else (ja, la, m, ke)

    def b_index(j, m, k):
        kj, kk = k // nkk, k % nkk
        jb = 0 if jb_n == b_merge else (kj if reduce else j)
        return (jb, lb, 0, kk) if tb else (jb, lb, kk, 0)

    def o_index(j, m, k):
        return (j, lo, 0, 0) if out_merge > 1 else (j, lo, m, 0)

    a_block = (None, None, bk, bm) if ta else (None if kg == 1 else kg, None, bm, bk)
    if b_merge > 1:
        b_block = (kg * b_merge, None, b_r, bk if tb else n)
    else:
        b_block = (None if kg == 1 else kg, None) + ((n, bk) if tb else (bk, n))
    o_block = (out_merge, None, o_r, n) if out_merge > 1 else (None, None, bm, n)
    dims = (((0 if ta else 1,), (1 if tb else 0,)), ((), ()))

    in_place = nk > 1 and out_dtype == F32 and out_merge == 1
    use_acc = nk > 1 and not in_place

    def body(a_ref, b_ref, *rest):
        o_ref = rest[-2] if use_acc else rest[-1]
        k = pl.program_id(2)
        av = a_ref[...]
        if a_silu:
            af = av.astype(F32)
            av = af * jax.nn.sigmoid(af)
        bv = b_ref[...]
        if kg > 1:
            bv = bv.reshape(kg, -1, bv.shape[-1])
            prod = sum(lax.dot_general(av[g].astype(BF16), bv[g].astype(BF16), dims, preferred_element_type=F32)
                       for g in range(kg))
        else:
            if b_merge > 1:
                bv = bv.reshape(b_rows, bv.shape[-1])
            prod = lax.dot_general(av.astype(BF16), bv.astype(BF16), dims, preferred_element_type=F32)

        def emit(val):
            val = val.astype(out_dtype)
            o_ref[...] = val.reshape(out_merge, o_r, n) if out_merge > 1 else val

        if nk == 1:
            emit(prod)
            return
        acc_ref = o_ref if in_place else rest[-1]

        @pl.when(k == 0)
        def _():
            acc_ref[...] = prod

        @pl.when(k > 0)
        def _():
            acc_ref[...] += prod

        if use_acc:
            @pl.when(k == nk - 1)
            def _():
                emit(acc_ref[...])

    in_specs = [pl.BlockSpec(a_block, a_index), pl.BlockSpec(b_block, b_index)]
    operands = [a, b]
    aliases = {}
    if out_buf is not None:
        assert out_buf.shape == tuple(out_shape4) and out_buf.dtype == out_dtype
        in_specs.append(_ANY)
        operands.append(out_buf)
        aliases = {2: 0}
    return pl.pallas_call(
        body, name=name,
        grid=(jo_blocks, m_tot // bm, nk),
        in_specs=in_specs,
        out_specs=pl.BlockSpec(o_block, o_index),
        out_shape=jax.ShapeDtypeStruct(tuple(out_shape4), out_dtype),
        scratch_shapes=[pltpu.VMEM((bm, n), F32)] if use_acc else [],
        input_output_aliases=aliases,
        compiler_params=_params(("parallel", "parallel", "arbitrary")),
    )(*operands)


def _as4(a):
    return a.reshape((1,) * (4 - a.ndim) + a.shape)


def _row_call(name, body, ins, outs, t, *, acc_outs=()):
    bt = _row_block(t, ROW_BLOCK)

    def spec(arr, tiled):
        if tiled:
            return pl.BlockSpec((bt,) + tuple(arr.shape[1:]), lambda i: (i,) + (0,) * (arr.ndim - 1))
        return pl.BlockSpec(tuple(arr.shape), lambda i: (0,) * arr.ndim)

    return pl.pallas_call(
        body, name=name, grid=(t // bt,),
        in_specs=[spec(a, tl) for a, tl in ins],
        out_specs=[spec(o, tl) for o, tl in outs],
        out_shape=[jax.ShapeDtypeStruct(o.shape, o.dtype) for o, _ in outs],
        compiler_params=_params(("arbitrary",) if acc_outs else ("parallel",)),
    )(*[a for a, _ in ins])


def _sds(shape, dtype):
    return jax.ShapeDtypeStruct(tuple(shape), dtype)


def _modulate(x, scl, shift):
    t, d = x.shape

    def body(x_ref, s_ref, b_ref, h_ref):
        h_ref[...] = (x_ref[...] * (1.0 + s_ref[...]) + b_ref[...]).astype(BF16)

    return _row_call("modulate", body, [(x, True), (scl, False), (shift, False)],
                     [(_sds((t, d), BF16), True)], t)[0]


def _ln_stats(r):
    mu = jnp.mean(r, axis=-1, keepdims=True)
    rc = r - mu
    var = jnp.mean(rc * rc, axis=-1, keepdims=True)
    rstd = lax.rsqrt(var + LN_EPS)
    return rc * rstd, rstd


def _ln_res_fwd(x, y, gw, g, b, mods=()):
    t, d = x.shape
    n_mod = len(mods)

    def body(x_ref, y_ref, gw_ref, g_ref, b_ref, *rest):
        mod_refs, o_ref, h_refs = rest[:2 * n_mod], rest[2 * n_mod], rest[2 * n_mod + 1:]
        r = ALPHA * x_ref[...] + gw_ref[...] * y_ref[...]
        xhat, _ = _ln_stats(r)
        xn = xhat * g_ref[...] + b_ref[...]
        o_ref[...] = xn
        for k in range(n_mod):
            h_refs[k][...] = (xn * (1.0 + mod_refs[2 * k][...]) + mod_refs[2 * k + 1][...]).astype(BF16)

    vecs = [(v, False) for pair in mods for v in pair]
    return _row_call("ln_res_fwd", body,
                     [(x, True), (y, True), (gw, False), (g, False), (b, False)] + vecs,
                     [(_sds((t, d), F32), True)] + [(_sds((t, d), BF16), True)] * n_mod, t)


def _ln_res_bwd(x, y, gw, g, b, dx_base, pairs=()):
    t, d = x.shape
    n_pair = len(pairs)

    def body(x_ref, y_ref, gw_ref, g_ref, b_ref, dxb_ref, *rest):
        pair_refs, outs = rest[:2 * n_pair], rest[2 * n_pair:]
        dx_ref, dy_ref = outs[0], outs[1]
        sums = outs[2:]

        @pl.when(pl.program_id(0) == 0)
        def _():
            for r in sums:
                r[...] = jnp.zeros_like(r)

        yv = y_ref[...]
        gwv = gw_ref[...]
        gv = g_ref[...]
        xhat, rstd = _ln_stats(ALPHA * x_ref[...] + gwv * yv)
        dxn = dxb_ref[...]
        if n_pair:
            xn = xhat * gv + b_ref[...]
            for k in range(n_pair):
                dh = pair_refs[2 * k][...]
                dxn = dxn + dh * (1.0 + pair_refs[2 * k + 1][...])
                sums[3 + 2 * k][...] += jnp.sum(dh * xn, axis=0, keepdims=True)
                sums[4 + 2 * k][...] += jnp.sum(dh, axis=0, keepdims=True)
        dxh = dxn * gv
        m1 = jnp.mean(dxh, axis=-1, keepdims=True)
        m2 = jnp.mean(dxh * xhat, axis=-1, keepdims=True)
        dr = rstd * (dxh - m1 - xhat * m2)
        dx_ref[...] = ALPHA * dr
        dy_ref[...] = (gwv * dr).astype(BF16)
        sums[0][...] += jnp.sum(dr * yv, axis=0, keepdims=True)
        sums[1][...] += jnp.sum(dxn * xhat, axis=0, keepdims=True)
        sums[2][...] += jnp.sum(dxn, axis=0, keepdims=True)

    vec = _sds((1, d), F32)
    n_sum = 3 + 2 * n_pair
    ins = [(x, True), (y, True), (gw, False), (g, False), (b, False), (dx_base, True)]
    for dh, scl in pairs:
        ins += [(dh, True), (scl, False)]
    return _row_call("ln_res_bwd", body, ins,
                     [(_sds((t, d), F32), True), (_sds((t, d), BF16), True)] + [(vec, False)] * n_sum, t,
                     acc_outs=tuple(range(2, 2 + n_sum)))


def _mod_bwd(dx_res, dh, x, scl):
    t, d = x.shape

    def body(dxr_ref, dh_ref, x_ref, s_ref, dx_ref, ds_ref, db_ref):
        @pl.when(pl.program_id(0) == 0)
        def _():
            ds_ref[...] = jnp.zeros_like(ds_ref)
            db_ref[...] = jnp.zeros_like(db_ref)

        dh = dh_ref[...]
        dx_ref[...] = dxr_ref[...] + dh * (1.0 + s_ref[...])
        ds_ref[...] += jnp.sum(dh * x_ref[...], axis=0, keepdims=True)
        db_ref[...] += jnp.sum(dh, axis=0, keepdims=True)

    vec = _sds((1, d), F32)
    return _row_call("mod_bwd", body, [(dx_res, True), (dh, True), (x, True), (scl, False)],
                     [(_sds((t, d), F32), True), (vec, False), (vec, False)], t, acc_outs=(1, 2))


def _loss_head(y, target):
    t, d = y.shape

    def body(y_ref, t_ref, l_ref, dy_ref):
        @pl.when(pl.program_id(0) == 0)
        def _():
            l_ref[...] = jnp.zeros_like(l_ref)

        err = y_ref[...] - t_ref[...]
        dy_ref[...] = err * (1.0 / d)
        part = 0.5 * jnp.sum(jnp.mean(err * err, axis=-1, keepdims=True), axis=0, keepdims=True)
        l_ref[...] += jnp.broadcast_to(part, l_ref.shape)

    return _row_call("loss_head", body, [(y, True), (target, True)],
                     [(_sds((SUBLANES, LANES), F32), False), (_sds((t, d), F32), True)], t,
                     acc_outs=(0,))


def _sigmoid(x):
    return 0.5 * jnp.tanh(0.5 * x) + 0.5


def _ffn_up_fwd(h, w_gu, lb):
    t, d = h.shape
    n = w_gu.shape[-1]
    half = N_DEV // 2
    bt = _row_block(t, MM_BLOCK)

    def body(h_ref, wg_ref, wu_ref, fac_ref, a_ref):
        hv = h_ref[...]
        g = jnp.dot(hv, wg_ref[...], preferred_element_type=F32)
        u = jnp.dot(hv, wu_ref[...], preferred_element_type=F32)
        sig = _sigmoid(g)
        silu = g * sig
        fac_ref[0] = u * (sig + silu * (1.0 - sig))
        fac_ref[1] = silu
        a_ref[...] = (silu * u).astype(BF16)

    return pl.pallas_call(
        body, name="ffn_up_fwd", grid=(half, t // bt),
        in_specs=[pl.BlockSpec((bt, d), lambda j, i: (i, 0)),
                  pl.BlockSpec((None, None, d, n), lambda j, i: (j, lb, 0, 0)),
                  pl.BlockSpec((None, None, d, n), lambda j, i: (half + j, lb, 0, 0))],
        out_specs=[pl.BlockSpec((2, None, bt, n), lambda j, i: (0, j, i, 0)),
                   pl.BlockSpec((None, bt, n), lambda j, i: (j, i, 0))],
        out_shape=[_sds((2, half, t, n), F32), _sds((half, t, n), BF16)],
        compiler_params=_params(("parallel", "parallel")),
    )(h, w_gu, w_gu)


def _ffn_down_bwd_a(dy, w_down, lb, fac):
    t, d = dy.shape
    _, half, _, n = fac.shape
    r = w_down.shape[2]
    bt = _row_block(t, MM_BLOCK)

    def body(dy_ref, w_ref, fac_ref, d_ref):
        da = lax.dot_general(dy_ref[...], w_ref[...].reshape(2 * r, d), (((1,), (1,)), ((), ())),
                             preferred_element_type=F32)
        d_ref[0] = (da * fac_ref[0]).astype(BF16)
        d_ref[1] = (da * fac_ref[1]).astype(BF16)

    return pl.pallas_call(
        body, name="ffn_down_bwd_a", grid=(half, t // bt),
        in_specs=[pl.BlockSpec((bt, d), lambda j, i: (i, 0)),
                  pl.BlockSpec((2, None, r, d), lambda j, i: (j, lb, 0, 0)),
                  pl.BlockSpec((2, None, bt, n), lambda j, i: (0, j, i, 0))],
        out_specs=pl.BlockSpec((2, None, bt, n), lambda j, i: (0, j, i, 0)),
        out_shape=_sds((2, half, t, n), BF16),
        compiler_params=_params(("parallel", "parallel")),
    )(dy, w_down, fac)


_INV_SQRT2 = 0.7071067811865476
_INV_SQRT_2PI = 0.3989422804014327


def _gelu(z):
    return 0.5 * z * (1.0 + lax.erf(z * _INV_SQRT2))


def _gelu_grad(z):
    return 0.5 * (1.0 + lax.erf(z * _INV_SQRT2)) + z * jnp.exp(-0.5 * z * z) * _INV_SQRT_2PI


def _window_mask():
    t_out = lax.broadcasted_iota(jnp.int32, (GMLP_WINDOW, GMLP_WINDOW), 0)
    s_in = lax.broadcasted_iota(jnp.int32, (GMLP_WINDOW, GMLP_WINDOW), 1)
    return (s_in // CHUNK) <= (t_out // CHUNK)


def _gmlp_recompute(z_ref, bin_ref, lng_ref, lnb_ref):
    half = N_DEV // 2
    z = z_ref[...] + bin_ref[...]
    ge = _gelu(z)
    u = ge[:half]
    v = ge[half:]
    width = half * v.shape[-1]
    mu = jnp.sum(jnp.sum(v, axis=0), axis=-1, keepdims=True) / width
    vc = v - mu
    var = jnp.sum(jnp.sum(vc * vc, axis=0), axis=-1, keepdims=True) / width
    rstd = lax.rsqrt(var + LN_EPS)
    xhat = vc * rstd
    vn = xhat * lng_ref[...] + lnb_ref[...]
    return z, u, xhat, rstd, vn


def _gmlp_mid_fwd(zpre, b_in, ln_g, ln_b, w_s, b_s):
    _, t, n = zpre.shape
    half = N_DEV // 2
    gd = half * n // GMLP_GROUPS
    per = n // gd
    w = GMLP_WINDOW

    def body(z_ref, bin_ref, lng_ref, lnb_ref, ws_ref, bs_ref, o_ref):
        _, u, _, _, vn = _gmlp_recompute(z_ref, bin_ref, lng_ref, lnb_ref)
        mask = _window_mask()
        for g in range(GMLP_GROUPS):
            sh, c0 = g // per, (g % per) * gd
            wsm = jnp.where(mask, ws_ref[g], 0.0).astype(BF16)
            s = jnp.dot(wsm, vn[sh][:, c0:c0 + gd].astype(BF16), preferred_element_type=F32) + bs_ref[g]
            o_ref[sh, :, c0:c0 + gd] = (u[sh][:, c0:c0 + gd] * s).astype(BF16)

    whole = lambda a: pl.BlockSpec(tuple(a.shape), lambda i: (0,) * a.ndim)
    return pl.pallas_call(
        body, name="gmlp_mid_fwd", grid=(t // w,),
        in_specs=[pl.BlockSpec((N_DEV, w, n), lambda i: (0, i, 0)),
                  whole(b_in), whole(ln_g), whole(ln_b), whole(w_s), whole(b_s)],
        out_specs=pl.BlockSpec((half, w, n), lambda i: (0, i, 0)),
        out_shape=_sds((half, t, n), BF16),
        compiler_params=_params(("parallel",)),
    )(zpre, b_in, ln_g, ln_b, w_s, b_s)


def _gmlp_mid_bwd(zpre, dgated, b_in, ln_g, ln_b, w_s, b_s):
    _, t, n = zpre.shape
    half = N_DEV // 2
    gd = half * n // GMLP_GROUPS
    per = n // gd
    w = GMLP_WINDOW
    width = half * n

    def body(z_ref, dg_ref, bin_ref, lng_ref, lnb_ref, ws_ref, bs_ref,
             dz_ref, dws_ref, dbs_ref, dlng_ref, dlnb_ref, dbin_ref, du_ref, dvn_ref):
        @pl.when(pl.program_id(0) == 0)
        def _():
            for r in (dws_ref, dbs_ref, dlng_ref, dlnb_ref, dbin_ref):
                r[...] = jnp.zeros_like(r)

        z, u, xhat, rstd, vn = _gmlp_recompute(z_ref, bin_ref, lng_ref, lnb_ref)
        mask = _window_mask()
        for g in range(GMLP_GROUPS):
            sh, c0 = g // per, (g % per) * gd
            wsm = jnp.where(mask, ws_ref[g], 0.0).astype(BF16)
            vg = vn[sh][:, c0:c0 + gd].astype(BF16)
            s = jnp.dot(wsm, vg, preferred_element_type=F32) + bs_ref[g]
            dgt = dg_ref[sh, :, c0:c0 + gd]
            ds = dgt * u[sh][:, c0:c0 + gd]
            du_ref[sh, :, c0:c0 + gd] = dgt * s
            dsb = ds.astype(BF16)
            dws = lax.dot_general(dsb, vg, (((1,), (1,)), ((), ())), preferred_element_type=F32)
            dws_ref[g] += jnp.where(mask, dws, 0.0)
            dbs_ref[g] += jnp.sum(ds, axis=-1, keepdims=True)
            dvn_ref[sh, :, c0:c0 + gd] = lax.dot_general(wsm, dsb, (((0,), (0,)), ((), ())),
                                                         preferred_element_type=F32)
        dvn = dvn_ref[...]
        dlng_ref[...] += jnp.sum(dvn * xhat, axis=1, keepdims=True)
        dlnb_ref[...] += jnp.sum(dvn, axis=1, keepdims=True)
        dxh = dvn * lng_ref[...]
        m1 = jnp.sum(jnp.sum(dxh, axis=0), axis=-1, keepdims=True) / width
        m2 = jnp.sum(jnp.sum(dxh * xhat, axis=0), axis=-1, keepdims=True) / width
        dv = rstd * (dxh - m1 - xhat * m2)
        gg = _gelu_grad(z)
        dzu = du_ref[...] * gg[:half]
        dzv = dv * gg[half:]
        dz_ref[:half] = dzu.astype(BF16)
        dz_ref[half:] = dzv.astype(BF16)
        dbin_ref[:half] += jnp.sum(dzu, axis=1, keepdims=True)
        dbin_ref[half:] += jnp.sum(dzv, axis=1, keepdims=True)

    whole = lambda a: pl.BlockSpec(tuple(a.shape), lambda i: (0,) * a.ndim)
    outs = [_sds((N_DEV, t, n), BF16), _sds(w_s.shape, F32), _sds(b_s.shape, F32),
            _sds(ln_g.shape, F32), _sds(ln_b.shape, F32), _sds(b_in.shape, F32)]
    return pl.pallas_call(
        body, name="gmlp_mid_bwd", grid=(t // w,),
        in_specs=[pl.BlockSpec((N_DEV, w, n), lambda i: (0, i, 0)),
                  pl.BlockSpec((half, w, n), lambda i: (0, i, 0)),
                  whole(b_in), whole(ln_g), whole(ln_b), whole(w_s), whole(b_s)],
        out_specs=[pl.BlockSpec((N_DEV, w, n), lambda i: (0, i, 0))] + [whole(o) for o in outs[1:]],
        out_shape=outs,
        scratch_shapes=[pltpu.VMEM((half, w, n), F32), pltpu.VMEM((half, w, n), F32)],
        compiler_params=_params(("arbitrary",)),
    )(zpre, dgated, b_in, ln_g, ln_b, w_s, b_s)


ATTN_CHUNKS = 4
ATTN_ROWS = ATTN_CHUNKS * CHUNK
ATTN_WINDOW = ATTN_ROWS + LEFT_PAD
ATTN_DIAGS = -(-(ATTN_ROWS + ATTN_WINDOW - 1) // LANES) * LANES
ATTN_ROLL = ATTN_DIAGS - (ATTN_ROWS - 1)


def _rel_vector(rel):
    j = np.arange(ATTN_DIAGS)
    idx = np.clip(ATTN_WINDOW - 1 - j, -(CHUNK - 1), MAX_REL) + (CHUNK - 1)
    return rel[:, idx]


def _attn_bias_mask(rel_ref, bm_ref):
    tt = lax.broadcasted_iota(jnp.int32, (ATTN_ROWS, ATTN_WINDOW), 0) // CHUNK
    rr = lax.broadcasted_iota(jnp.int32, (ATTN_ROWS, ATTN_WINDOW), 1) // CHUNK
    band = (rr >= tt) & (rr <= tt + LEFT_CHUNKS)
    for j in range(bm_ref.shape[0]):
        vec = jnp.broadcast_to(rel_ref[j:j + 1, :], (ATTN_ROWS, ATTN_DIAGS))
        toeplitz = pltpu.roll(vec, ATTN_ROLL, 1, stride=1, stride_axis=0)[:, :ATTN_WINDOW]
        bm_ref[j] = jnp.where(band, toeplitz, -jnp.inf)


def _attn_probs(q_ref, k_ref, bm_ref, j, hd, start, valid):
    qh = q_ref[:, j * hd:(j + 1) * hd]
    kb = k_ref[pl.ds(start, ATTN_WINDOW), j * hd:(j + 1) * hd]
    sc = lax.dot_general(qh, kb, (((1,), (1,)), ((), ())), preferred_element_type=F32)
    sc = sc * (hd ** -0.5) + bm_ref[j]
    sc = jnp.where(valid, sc, -jnp.inf)
    sc = sc - jnp.max(sc, axis=-1, keepdims=True)
    e = jnp.exp(sc)
    return e / jnp.sum(e, axis=-1, keepdims=True), qh, kb


def _window_valid(start):
    r = lax.broadcasted_iota(jnp.int32, (1, ATTN_WINDOW), 1)
    return (start + r) >= LEFT_PAD


def _attn_fwd(q, kvp, rel_vec):
    t, d = q.shape
    hd = d // N_HEADS
    half = N_DEV // 2
    n = kvp.shape[-1]
    per = n // hd
    rows = kvp.shape[1]

    def body(q_ref, k_ref, v_ref, rel_ref, o_ref, bm_ref):
        @pl.when(pl.program_id(1) == 0)
        def _():
            _attn_bias_mask(rel_ref, bm_ref)

        start = pl.multiple_of(pl.program_id(1) * ATTN_ROWS, ATTN_ROWS)
        valid = _window_valid(start)
        for j in range(per):
            p, _, _ = _attn_probs(q_ref, k_ref, bm_ref, j, hd, start, valid)
            vb = v_ref[pl.ds(start, ATTN_WINDOW), j * hd:(j + 1) * hd]
            o_ref[:, j * hd:(j + 1) * hd] = jnp.dot(p.astype(BF16), vb, preferred_element_type=F32).astype(BF16)

    return pl.pallas_call(
        body, name="attn_fwd", grid=(half, t // ATTN_ROWS),
        in_specs=[pl.BlockSpec((ATTN_ROWS, n), lambda g, i: (i, g)),
                  pl.BlockSpec((None, rows, n), lambda g, i: (g, 0, 0)),
                  pl.BlockSpec((None, rows, n), lambda g, i: (half + g, 0, 0)),
                  pl.BlockSpec((None, per, ATTN_DIAGS), lambda g, i: (g, 0, 0))],
        out_specs=pl.BlockSpec((ATTN_ROWS, n), lambda g, i: (i, g)),
        out_shape=_sds((t, d), BF16),
        scratch_shapes=[pltpu.VMEM((per, ATTN_ROWS, ATTN_WINDOW), F32)],
        compiler_params=_params(("arbitrary", "arbitrary")),
    )(q, kvp, kvp, rel_vec.reshape(half, per, ATTN_DIAGS))


def _attn_bwd(q, dout, kvp, rel_vec, dk_in=None, dv_in=None):
    t, d = q.shape
    hd = d // N_HEADS
    half = N_DEV // 2
    n = kvp.shape[-1]
    per = n // hd
    rows = kvp.shape[1]
    scale = hd ** -0.5
    carry = dk_in is not None

    def body(q_ref, do_ref, k_ref, v_ref, rel_ref, *rest):
        dq_ref, dk_ref, dv_ref, dsc_ref, bm_ref = rest[-5:]

        @pl.when(pl.program_id(1) == 0)
        def _():
            _attn_bias_mask(rel_ref, bm_ref)
            dk_ref[...] = rest[0][...] if carry else jnp.zeros_like(dk_ref)
            dv_ref[...] = rest[1][...] if carry else jnp.zeros_like(dv_ref)
            dsc_ref[...] = jnp.zeros_like(dsc_ref)

        start = pl.multiple_of(pl.program_id(1) * ATTN_ROWS, ATTN_ROWS)
        valid = _window_valid(start)
        for j in range(per):
            cols = slice(j * hd, (j + 1) * hd)
            p, qh, kb = _attn_probs(q_ref, k_ref, bm_ref, j, hd, start, valid)
            vb = v_ref[pl.ds(start, ATTN_WINDOW), cols]
            doh = do_ref[:, cols]
            dp = lax.dot_general(doh, vb, (((1,), (1,)), ((), ())), preferred_element_type=F32)
            ds = p * (dp - jnp.sum(dp * p, axis=-1, keepdims=True))
            dsc_ref[j] += sum(ds[a * CHUNK:(a + 1) * CHUNK, a * CHUNK:a * CHUNK + BAND]
                              for a in range(ATTN_CHUNKS))
            dsb = (ds * scale).astype(BF16)
            dq_ref[:, cols] = jnp.dot(dsb, kb, preferred_element_type=F32).astype(BF16)
            dk_ref[pl.ds(start, ATTN_WINDOW), cols] += lax.dot_general(
                dsb, qh, (((0,), (0,)), ((), ())), preferred_element_type=F32)
            dv_ref[pl.ds(start, ATTN_WINDOW), cols] += lax.dot_general(
                p.astype(BF16), doh, (((0,), (0,)), ((), ())), preferred_element_type=F32)

    tile = pl.BlockSpec((ATTN_ROWS, n), lambda g, i: (i, g))
    shard = pl.BlockSpec((None, rows, n), lambda g, i: (g, 0, 0))
    in_specs = [tile, tile, shard, pl.BlockSpec((None, rows, n), lambda g, i: (half + g, 0, 0)),
                pl.BlockSpec((None, per, ATTN_DIAGS), lambda g, i: (g, 0, 0))]
    operands = [q, dout, kvp, kvp, rel_vec.reshape(half, per, ATTN_DIAGS)]
    if carry:
        in_specs += [shard, shard]
        operands += [dk_in, dv_in]
    acc = _sds((half, rows, n), F32)
    return pl.pallas_call(
        body, name="attn_bwd", grid=(half, t // ATTN_ROWS),
        in_specs=in_specs,
        out_specs=[tile, shard, shard, pl.BlockSpec((per, CHUNK, BAND), lambda g, i: (g, 0, 0))],
        out_shape=[_sds((t, d), BF16), acc, acc, _sds((N_HEADS, CHUNK, BAND), F32)],
        scratch_shapes=[pltpu.VMEM((per, ATTN_ROWS, ATTN_WINDOW), F32)],
        compiler_params=_params(("arbitrary", "arbitrary")),
    )(*operands)


SKEW_PITCH = 640
SKEW = SKEW_PITCH + 1
SKEW_LANES = -(-SKEW // LANES) * LANES


def _skew_diagonals(dsc):
    h = dsc.shape[0]
    wide = jnp.pad(dsc, ((0, 0), (0, 0), (0, SKEW_PITCH - BAND))).reshape(h, CHUNK * SKEW_PITCH)
    wide = jnp.pad(wide, ((0, 0), (0, CHUNK))).reshape(h, CHUNK, SKEW)
    return jnp.pad(wide, ((0, 0), (0, 0), (0, SKEW_LANES - SKEW)))


def _rel_bias_grad(skewed):
    heads = skewed.shape[0]
    hb = SUBLANES

    def body(d_ref, o_ref):
        col = lax.broadcasted_iota(jnp.int32, (SKEW_LANES, N_REL), 0)
        bucket = lax.broadcasted_iota(jnp.int32, (SKEW_LANES, N_REL), 1)
        diag = jnp.where(col < BAND, col, col - SKEW)
        idx = jnp.clip(LEFT_PAD - diag, -(CHUNK - 1), MAX_REL) + (CHUNK - 1)
        oh = ((idx == bucket) & (col < SKEW)).astype(BF16)
        dv = jnp.sum(d_ref[...], axis=1)
        hi = dv.astype(BF16)
        rest = dv - hi.astype(F32)
        mid = rest.astype(BF16)
        lo = (rest - mid.astype(F32)).astype(BF16)
        acc = jnp.dot(hi, oh, preferred_element_type=F32)
        acc += jnp.dot(mid, oh, preferred_element_type=F32)
        acc += jnp.dot(lo, oh, preferred_element_type=F32)
        o_ref[...] = acc

    return pl.pallas_call(
        body, name="rel_bias_grad", grid=(heads // hb,),
        in_specs=[pl.BlockSpec((hb, CHUNK, SKEW_LANES), lambda i: (i, 0, 0))],
        out_specs=pl.BlockSpec((hb, N_REL), lambda i: (i, 0)),
        out_shape=_sds((heads, N_REL), F32),
        compiler_params=_params(("parallel",)),
    )(skewed)


def _sum_parts(parts):
    s_n, rows, c = parts.shape
    br = _row_block(rows, OPT_ROW_BLOCK)

    def body(p_ref, o_ref):
        acc = p_ref[0].astype(F32)
        for s in range(1, s_n):
            acc = acc + p_ref[s].astype(F32)
        o_ref[...] = acc

    return pl.pallas_call(
        body, name="sum_parts", grid=(rows // br,),
        in_specs=[pl.BlockSpec((s_n, br, c), lambda i: (0, i, 0))],
        out_specs=pl.BlockSpec((br, c), lambda i: (i, 0)),
        out_shape=_sds((rows, c), F32),
        compiler_params=_params(("parallel",)),
    )(parts)


def _adamw(own, own_idx, parts, w, m, v, row0=0, bufs=None, after=None):
    _, rows, c = own.shape
    s_n = 0 if parts is None else parts.shape[0]
    total = w.shape[0]
    br = _row_block(rows, OPT_ROW_BLOCK)
    assert row0 % br == 0 and (bufs is not None or (row0 == 0 and total == rows))
    b0 = row0 // br
    m_corr = 1.0 - ADAM_B1 ** ADAM_STEP
    v_corr = 1.0 - ADAM_B2 ** ADAM_STEP

    def body(idx_ref, own_ref, *refs):
        if s_n:
            p_ref, refs = refs[0], refs[1:]
        w_ref, m_ref, v_ref = refs[:3]
        g_ref, d_ref, nm_ref, nv_ref = refs[-4:]
        g = own_ref[...].astype(F32)
        for s in range(s_n):
            g = g + p_ref[s].astype(F32)
        nm = ADAM_B1 * m_ref[...] + (1.0 - ADAM_B1) * g
        nv = ADAM_B2 * v_ref[...] + (1.0 - ADAM_B2) * (g * g)
        g_ref[...] = g
        nm_ref[...] = nm
        nv_ref[...] = nv
        d_ref[...] = -ADAM_LR * ((nm / m_corr) / (jnp.sqrt(nv / v_corr) + ADAM_EPS) + ADAM_WD * w_ref[...])

    tile = pl.BlockSpec((br, c), lambda i, idx: (i + b0, 0))
    in_specs = [pl.BlockSpec((None, br, c), lambda i, idx: (idx[0], i, 0))]
    operands = [own_idx, own]
    if s_n:
        in_specs.append(pl.BlockSpec((s_n, br, c), lambda i, idx: (0, i, 0)))
        operands.append(parts)
    in_specs += [tile, tile, tile]
    operands += [w, m, v]
    aliases = {}
    if bufs is not None:
        aliases = {len(operands) + j: j for j in range(4)}
        in_specs += [_ANY] * 4
        operands += list(bufs)
    if after is not None:
        in_specs.append(_ANY)
        operands.append(after)
    out = _sds((total, c), F32)
    return pl.pallas_call(
        body, name="adamw",
        grid_spec=pltpu.PrefetchScalarGridSpec(
            num_scalar_prefetch=1, grid=(rows // br,), in_specs=in_specs,
            out_specs=[tile, tile, tile, tile]),
        out_shape=[out, out, out, out],
        input_output_aliases=aliases,
        compiler_params=_params(("parallel",)),
    )(*operands)


def _chip_sum(p, r1, core):
    half = N_DEV // 2
    c = p.shape[-1]
    rows = int(np.prod(p.shape[1:-1]))
    br = _row_block(rows, BIG_ROW_BLOCK)

    def body(core_ref, p_ref, r_ref, o_ref):
        o_ref[...] = (p_ref[...].astype(F32) + r_ref[...].astype(F32)).astype(BF16)

    out = pl.pallas_call(
        body, name="chip_sum",
        grid_spec=pltpu.PrefetchScalarGridSpec(
            num_scalar_prefetch=1, grid=(half, rows // br),
            in_specs=[pl.BlockSpec((None, None, br, c), lambda q, i, cr: (q, cr[0], i, 0)),
                      pl.BlockSpec((None, br, c), lambda q, i, cr: (q, i, 0))],
            out_specs=pl.BlockSpec((None, br, c), lambda q, i, cr: (q, i, 0))),
        out_shape=_sds((half, rows, c), BF16),
        compiler_params=_params(("parallel", "parallel")),
    )(core, p.reshape(half, 2, rows, c), r1.reshape(half, rows, c))
    return out.reshape((half,) + p.shape[1:])


def _position():
    return tuple(lax.axis_index(a) for a in MESH_AXES)


def _linear(px, py, pc):
    return 4 * px + 2 * py + pc


def _all_gather_small(v, after=()):
    rows, lanes = v.shape

    def body(x_ref, *rest):
        out_ref, send_sems, recv_sems, local_sem = rest[-4:]
        x, y, c = _position()
        me, sibling = (x, y, c), (x, y, 1 - c)
        chips = [(1 - x, y), (x, 1 - y), (1 - x, 1 - y)]

        def copy(k, block, to, src=None):
            dst = out_ref.at[_linear(*block)]
            return pltpu.make_async_remote_copy(
                src_ref=dst if src is None else src, dst_ref=dst,
                send_sem=send_sems.at[k], recv_sem=recv_sems.at[k],
                device_id=to, device_id_type=MESH_ID)

        mine = pltpu.make_async_copy(x_ref, out_ref.at[_linear(*me)], local_sem)
        mine.start()
        first = [copy(0, me, sibling, src=x_ref)]
        first += [copy(1 + j, me, (*chip, c), src=x_ref) for j, chip in enumerate(chips)]
        for cp in first:
            cp.start()
        passed = [copy(4 + j, (*chip, c), sibling) for j, chip in enumerate(chips)]
        for j, chip in enumerate(chips):
            copy(1 + j, (*chip, c), me).wait_recv()
            passed[j].start()
        copy(0, sibling, me).wait_recv()
        for j, chip in enumerate(chips):
            copy(4 + j, (*chip, 1 - c), me).wait_recv()
        for cp in first + passed:
            cp.wait_send()
        mine.wait()

    return pl.pallas_call(
        body, name="all_gather_small",
        out_shape=_sds((N_DEV, rows, lanes), v.dtype),
        in_specs=[_VMEM] + [_ANY] * len(after), out_specs=_VMEM,
        scratch_shapes=[pltpu.SemaphoreType.DMA((7,)), pltpu.SemaphoreType.DMA((7,)),
                        pltpu.SemaphoreType.DMA],
        compiler_params=pltpu.CompilerParams(vmem_limit_bytes=VMEM_LIMIT),
    )(v, *after)


_HBM = pl.BlockSpec(memory_space=pltpu.HBM)
_SEM = pl.BlockSpec(memory_space=pltpu.SEMAPHORE)
_EFFECT = pltpu.SideEffectType.DATAFLOW_SIDE_EFFECTING
_ALL_CHIPS = [(0, 0), (0, 1), (1, 0), (1, 1)]


def _other_chips(x, y):
    return [(1 - x, y), (x, 1 - y), (1 - x, 1 - y)]


def _in_hbm(a):
    return pltpu.with_memory_space_constraint(a, pltpu.HBM)


def _token():
    return _sds((SUBLANES, LANES), F32)


def _gather_ici_copy(ref, i, k, chip, c, block, send_sems, recv_sems):
    return pltpu.make_async_remote_copy(
        src_ref=ref.at[block], dst_ref=ref.at[block],
        send_sem=send_sems.at[3 * i + k], recv_sem=recv_sems.at[3 * i + k],
        device_id=(*chip, c), device_id_type=MESH_ID)


def _gather_ici_start(name, lands, after=None):
    n = len(lands)
    extra = [] if after is None else [after]

    def body(*refs):
        ins, send_sems, recv_sems, token = refs[:n], refs[-n - 3], refs[-n - 2], refs[-1]
        x, y, c = _position()
        me = _linear(x, y, c)
        for i in range(n):
            for k, chip in enumerate(_other_chips(x, y)):
                _gather_ici_copy(ins[i], i, k, chip, c, me, send_sems, recv_sems).start()
        token[...] = jnp.zeros_like(token)

    out = pl.pallas_call(
        body, name=name,
        out_shape=(pltpu.SemaphoreType.DMA((3 * n,)), pltpu.SemaphoreType.DMA((3 * n,)),
                   *[pltpu.HBM(a.shape, a.dtype) for a in lands], _token()),
        in_specs=[_HBM] * n + [_ANY] * len(extra), out_specs=(_SEM, _SEM, *[_HBM] * n, _VMEM),
        input_output_aliases={i: 2 + i for i in range(n)},
        compiler_params=pltpu.CompilerParams(has_side_effects=_EFFECT),
    )(*[_in_hbm(a) for a in lands], *extra)
    return out[0], out[1], list(out[2:2 + n]), out[-1]


def _gather_ici_wait(name, lands, send_sems, recv_sems, after):
    n = len(lands)

    def body(*refs):
        ins, ss, rs = refs[:n], refs[n], refs[n + 1]
        x, y, c = _position()
        me = _linear(x, y, c)
        for i in range(n):
            for k, chip in enumerate(_other_chips(x, y)):
                _gather_ici_copy(ins[i], i, k, chip, c, me, ss, rs).wait_send()
                _gather_ici_copy(ins[i], i, k, chip, c, _linear(*chip, c), ss, rs).wait_recv()

    out = pl.pallas_call(
        body, name=name,
        out_shape=[pltpu.HBM(a.shape, a.dtype) for a in lands],
        in_specs=[_HBM] * n + [_SEM, _SEM, _ANY], out_specs=[_HBM] * n,
        input_output_aliases={i: i for i in range(n)},
        compiler_params=pltpu.CompilerParams(has_side_effects=_EFFECT),
    )(*lands, send_sems, recv_sems, after)
    return list(out)


def _gather_d2d(lands):
    n = len(lands)

    def body(*refs):
        ins, outs, send_sems, recv_sems = refs[:n], refs[n:2 * n], refs[2 * n], refs[2 * n + 1]
        x, y, c = _position()

        def copy(i, q, core):
            block = _linear(*_ALL_CHIPS[q], core)
            return pltpu.make_async_remote_copy(
                src_ref=ins[i].at[block], dst_ref=outs[i].at[block],
                send_sem=send_sems.at[i, q], recv_sem=recv_sems.at[i, q],
                device_id=(x, y, 1 - c), device_id_type=MESH_ID)

        sent = [copy(i, q, c) for i in range(n) for q in range(len(_ALL_CHIPS))]
        for cp in sent:
            cp.start()
        for i in range(n):
            for q in range(len(_ALL_CHIPS)):
                copy(i, q, 1 - c).wait_recv()
        for cp in sent:
            cp.wait_send()

    return pl.pallas_call(
        body, name="gather_d2d",
        out_shape=[_sds(a.shape, a.dtype) for a in lands],
        in_specs=[_ANY] * n, out_specs=[_ANY] * n,
        input_output_aliases={i: i for i in range(n)},
        scratch_shapes=[pltpu.SemaphoreType.DMA((n, 4)), pltpu.SemaphoreType.DMA((n, 4))],
    )(*lands)


def _partials_d2d(parts):
    n = len(parts)
    half = N_DEV // 2

    def body(*refs):
        ins, outs, send_sems, recv_sems = refs[:n], refs[n:2 * n], refs[2 * n], refs[2 * n + 1]
        x, y, c = _position()

        def copy(i, q):
            return pltpu.make_async_remote_copy(
                src_ref=ins[i].at[_linear(*_ALL_CHIPS[q], 1 - c)], dst_ref=outs[i].at[q],
                send_sem=send_sems.at[i, q], recv_sem=recv_sems.at[i, q],
                device_id=(x, y, 1 - c), device_id_type=MESH_ID)

        sent = [copy(i, q) for i in range(n) for q in range(half)]
        for cp in sent:
            cp.start()
        for cp in sent:
            cp.wait_recv()
        for cp in sent:
            cp.wait_send()

    return pl.pallas_call(
        body, name="partials_d2d",
        out_shape=[_sds((half,) + p.shape[1:], p.dtype) for p in parts],
        in_specs=[_ANY] * n, out_specs=[_ANY] * n,
        scratch_shapes=[pltpu.SemaphoreType.DMA((n, half)), pltpu.SemaphoreType.DMA((n, half))],
    )(*parts)


def _partials_peers(x, y, c, direct):
    chips = _other_chips(x, y)
    if not direct:
        return [((*ch, c), 2 * ch[0] + ch[1]) for ch in chips]
    peers = [(x, y, 1 - c)] + [(*ch, c) for ch in chips] + [(*ch, 1 - c) for ch in chips]
    return [(p, _linear(*p)) for p in peers]


def _partials_copies(srcs, lands, send_sems, recv_sems, direct):
    x, y, c = _position()
    peers = _partials_peers(x, y, c, direct)
    return [pltpu.make_async_remote_copy(
        src_ref=srcs[i].at[block], dst_ref=lands[i].at[k],
        send_sem=send_sems.at[len(peers) * i + k], recv_sem=recv_sems.at[len(peers) * i + k],
        device_id=peer, device_id_type=MESH_ID)
        for i in range(len(srcs)) for k, (peer, block) in enumerate(peers)]


def _partials_send_start(name, srcs, lands, direct, after=None):
    n = len(srcs)
    n_sem = n * (N_DEV - 1 if direct else len(_ALL_CHIPS) - 1)

    def body(*refs):
        _, send_sems, recv_sems = refs[:2 * n], refs[-2 * n - 3], refs[-2 * n - 2]
        for cp in _partials_copies(refs[:n], refs[n:2 * n], send_sems, recv_sems, direct):
            cp.start()
        refs[-1][...] = jnp.zeros_like(refs[-1])

    both = list(srcs) + list(lands)
    extra = [] if after is None else [after]
    out = pl.pallas_call(
        body, name=name,
        out_shape=(pltpu.SemaphoreType.DMA((n_sem,)), pltpu.SemaphoreType.DMA((n_sem,)),
                   *[pltpu.HBM(a.shape, a.dtype) for a in both], _token()),
        in_specs=[_HBM] * (2 * n) + [_ANY] * len(extra), out_specs=(_SEM, _SEM, *[_HBM] * (2 * n), _VMEM),
        input_output_aliases={i: 2 + i for i in range(2 * n)},
        compiler_params=pltpu.CompilerParams(has_side_effects=_EFFECT),
    )(*[_in_hbm(a) for a in both], *extra)
    return out[0], out[1], list(out[2:2 + n]), list(out[2 + n:2 + 2 * n]), out[-1]


def _partials_send_wait(name, srcs, lands, send_sems, recv_sems, direct, after):
    n = len(srcs)

    def body(*refs):
        for cp in _partials_copies(refs[:n], refs[n:2 * n], refs[2 * n], refs[2 * n + 1], direct):
            cp.wait_send()
            cp.wait_recv()

    both = list(srcs) + list(lands)
    out = pl.pallas_call(
        body, name=name,
        out_shape=[pltpu.HBM(a.shape, a.dtype) for a in both],
        in_specs=[_HBM] * (2 * n) + [_SEM, _SEM, _ANY], out_specs=[_HBM] * (2 * n),
        input_output_aliases={i: i for i in range(2 * n)},
        compiler_params=pltpu.CompilerParams(has_side_effects=_EFFECT),
    )(*both, send_sems, recv_sems, after)
    return list(out[:n]), list(out[n:])


def _pack(arrs):
    flat = jnp.concatenate([a.reshape(-1).astype(F32) for a in arrs])
    block = OPT_ROW_BLOCK if flat.shape[0] > OPT_ROW_BLOCK * LANES else SUBLANES
    pad = (-flat.shape[0]) % (block * LANES)
    if pad:
        flat = jnp.concatenate([flat, jnp.zeros((pad,), F32)])
    return flat.reshape(-1, LANES)


def _unpack(packed, shapes, lead=()):
    flat = packed.reshape(lead + (-1,))
    out, off = [], 0
    for s in shapes:
        size = int(np.prod(s))
        out.append(flat[..., off:off + size].reshape(lead + tuple(s)))
        off += size
    return out


def _unshard_last(g):
    nd = g.ndim
    perm = tuple(range(1, nd - 1)) + (0, nd - 1)
    t = jnp.transpose(g, perm)
    return t.reshape(t.shape[:-2] + (N_DEV * g.shape[-1],))


def kernel(x, c, w_ada, b_ada, ln_g, ln_b, ffn_gu, ffn_down, gmlp_w_in, gmlp_b_in, gmlp_ln_g, gmlp_ln_b, gmlp_w_s, gmlp_b_s, gmlp_w_out, w_ada_kv, b_ada_kv, w_kv, attn_w_q, attn_rel_bias, attn_w_o, loss_target, m_w_ada, m_b_ada, m_ln_g, m_ln_b, m_ffn_gu, m_ffn_down, m_gmlp_w_in, m_gmlp_b_in, m_gmlp_ln_g, m_gmlp_ln_b, m_gmlp_w_s, m_gmlp_b_s, m_gmlp_w_out, m_w_ada_kv, m_b_ada_kv, m_w_kv, m_attn_w_q, m_attn_rel_bias, m_attn_w_o, v_w_ada, v_b_ada, v_ln_g, v_ln_b, v_ffn_gu, v_ffn_down, v_gmlp_w_in, v_gmlp_b_in, v_gmlp_ln_g, v_gmlp_ln_b, v_gmlp_w_s, v_gmlp_b_s, v_gmlp_w_out, v_w_ada_kv, v_b_ada_kv, v_w_kv, v_attn_w_q, v_attn_rel_bias, v_attn_w_o):
    weights = dict(w_ada=w_ada, b_ada=b_ada, ln_g=ln_g, ln_b=ln_b, ffn_gu=ffn_gu, ffn_down=ffn_down,
                   gmlp_w_in=gmlp_w_in, gmlp_b_in=gmlp_b_in, gmlp_ln_g=gmlp_ln_g, gmlp_ln_b=gmlp_ln_b,
                   gmlp_w_s=gmlp_w_s, gmlp_b_s=gmlp_b_s, gmlp_w_out=gmlp_w_out, w_ada_kv=w_ada_kv,
                   b_ada_kv=b_ada_kv, w_kv=w_kv, attn_w_q=attn_w_q, attn_rel_bias=attn_rel_bias,
                   attn_w_o=attn_w_o)
    mom1 = dict(w_ada=m_w_ada, b_ada=m_b_ada, ln_g=m_ln_g, ln_b=m_ln_b, ffn_gu=m_ffn_gu, ffn_down=m_ffn_down,
                gmlp_w_in=m_gmlp_w_in, gmlp_b_in=m_gmlp_b_in, gmlp_ln_g=m_gmlp_ln_g, gmlp_ln_b=m_gmlp_ln_b,
                gmlp_w_s=m_gmlp_w_s, gmlp_b_s=m_gmlp_b_s, gmlp_w_out=m_gmlp_w_out, w_ada_kv=m_w_ada_kv,
                b_ada_kv=m_b_ada_kv, w_kv=m_w_kv, attn_w_q=m_attn_w_q, attn_rel_bias=m_attn_rel_bias,
                attn_w_o=m_attn_w_o)
    mom2 = dict(w_ada=v_w_ada, b_ada=v_b_ada, ln_g=v_ln_g, ln_b=v_ln_b, ffn_gu=v_ffn_gu, ffn_down=v_ffn_down,
                gmlp_w_in=v_gmlp_w_in, gmlp_b_in=v_gmlp_b_in, gmlp_ln_g=v_gmlp_ln_g, gmlp_ln_b=v_gmlp_ln_b,
                gmlp_w_s=v_gmlp_w_s, gmlp_b_s=v_gmlp_b_s, gmlp_w_out=v_gmlp_w_out, w_ada_kv=v_w_ada_kv,
                b_ada_kv=v_b_ada_kv, w_kv=v_w_kv, attn_w_q=v_attn_w_q, attn_rel_bias=v_attn_rel_bias,
                attn_w_o=v_attn_w_o)
    order = list(weights)

    x = x[0]
    target = loss_target[0]
    t, d = x.shape
    n_mod = w_ada.shape[-1] * N_DEV // d
    mod_w = w_ada.shape[-1]
    kv_w = w_ada_kv.shape[-1]
    n_b = DEPTH - N_A
    me = _linear(*_position())

    l2 = DEPTH * 2
    big = dict(
        ffn_gu=ffn_gu.reshape((l2,) + ffn_gu.shape[2:]),
        ffn_down=ffn_down.reshape((l2,) + ffn_down.shape[2:]),
        gmlp_w_in=gmlp_w_in, gmlp_w_out=gmlp_w_out, w_kv=w_kv[None],
        attn_w_q=attn_w_q, attn_w_o=attn_w_o)
    big_names = list(big)
    core = lax.axis_index("c").astype(jnp.int32).reshape(1)
    chip = (2 * lax.axis_index("x") + lax.axis_index("y")).astype(jnp.int32).reshape(1)

    fwd_groups = [
        {"ffn_gu": (0, 1), "ffn_down": (0, 1)},
        {"gmlp_w_in": (0, 1), "gmlp_w_out": (0, 1)},
        {"ffn_gu": (1, 1), "ffn_down": (1, 1)},
        {"ffn_gu": (2, 1), "ffn_down": (2, 1)},
        {"gmlp_w_in": (1, 1), "gmlp_w_out": (1, 1)},
        {"ffn_gu": (3, 1), "ffn_down": (3, 1), "w_kv": (0, 1)},
        {"ffn_gu": (4, 1), "ffn_down": (4, 1)},
        {"attn_w_q": (0, 1), "attn_w_o": (0, 1)},
        {"ffn_gu": (5, 1), "ffn_down": (5, 1)},
        {"ffn_gu": (6, 2), "ffn_down": (6, 2), "attn_w_q": (1, 1), "attn_w_o": (1, 1)},
    ]
    bwd_groups = []
    for l in range(DEPTH):
        g = {"ffn_gu": (2 * l, 2), "ffn_down": (2 * l, 2)}
        if l < N_A:
            g.update({"gmlp_w_in": (l, 1), "gmlp_w_out": (l, 1)})
        else:
            g.update({"attn_w_q": (l - N_A, 1), "attn_w_o": (l - N_A, 1)})
        if l == N_A - 1:
            g["w_kv"] = (0, 1)
        bwd_groups.append(g)

    def slot_of(groups, name, slot):
        for gi, g in enumerate(groups):
            if name in g and g[name][0] <= slot < g[name][0] + g[name][1]:
                return gi, slot - g[name][0]
        raise KeyError((name, slot))

    def start_group(gi, after=None):
        lands = []
        for name, (s0, cnt) in fwd_groups[gi].items():
            shard = big[name][s0:s0 + cnt].astype(BF16)
            land = lax.empty((N_DEV,) + shard.shape, BF16)
            lands.append(lax.dynamic_update_slice(land, shard[None], (me,) + (0,) * shard.ndim))
        return _gather_ici_start(f"gather_ici_start_{gi}", lands, after)

    gathered = [None] * len(fwd_groups)

    def land_group(gi, after):
        send_sems, recv_sems, lands, _ = flights[gi]
        lands = _gather_ici_wait(f"gather_ici_wait_{gi}", lands, send_sems, recv_sems, after)
        gathered[gi] = dict(zip(fwd_groups[gi], _gather_d2d(lands)))

    def weight(name, slot):
        gi, local = slot_of(fwd_groups, name, slot)
        return gathered[gi][name], local

    swapped = ("ffn_gu",)

    def grad_shape(name):
        s = big[name].shape[1:]
        return s[:-2] + (s[-1], s[-2]) if name in swapped else s

    partial = [{name: lax.empty((N_DEV, cnt) + grad_shape(name), BF16) for name, (_, cnt) in g.items()}
               for g in bwd_groups]

    c_all = _all_gather_small(_pack([c]))
    c_all = _unpack(c_all, [(d,)], lead=(N_DEV,))[0]
    c4 = _as4(c_all)
    mod_part = _matmul("ada_fwd", c4, w_ada[:, None], (DEPTH, 1, N_DEV, mod_w), F32, a_silu=True)
    kv_part = _matmul("ada_kv_fwd", c4, _as4(w_ada_kv), (1, 1, N_DEV, kv_w), F32, a_silu=True)
    small_shapes = [mod_part.shape, kv_part.shape, ln_g.shape, ln_b.shape, gmlp_b_in.shape,
                    gmlp_ln_g.shape, gmlp_ln_b.shape, attn_rel_bias.shape]
    small = _all_gather_small(_pack([mod_part, kv_part, ln_g, ln_b, gmlp_b_in, gmlp_ln_g, gmlp_ln_b,
                                     attn_rel_bias]))
    flights = [start_group(0, after=small)]
    flights += [start_group(gi, after=flights[0][3]) for gi in range(1, len(fwd_groups))]
    start_token = sum(f[3][0, 0] for f in flights)
    (mod_g, kvm_g, ln_g_g, ln_b_g, b_in_g, gln_g_g, gln_b_g, rel_g) = _unpack(small, small_shapes, lead=(N_DEV,))
    mod_mine = lax.dynamic_index_in_dim(mod_g[:, :, 0], me, axis=2, keepdims=False)
    mod = _unshard_last(mod_mine) + b_ada
    mod = mod.reshape(DEPTH, n_mod, 1, d)
    kvm_mine = lax.dynamic_index_in_dim(kvm_g[:, 0, 0], me, axis=1, keepdims=False)
    mkv = (_unshard_last(kvm_mine) + b_ada_kv).reshape(2, 1, d)
    ln_g_f = _unshard_last(ln_g_g)
    ln_b_f = _unshard_last(ln_b_g)
    half = N_DEV // 2
    b_in_f = jnp.transpose(b_in_g, (1, 0, 2))[:, :, None, :]
    gln_g_f = _unshard_last(gln_g_g).reshape(N_A, half, 1, -1)
    gln_b_f = _unshard_last(gln_b_g).reshape(N_A, half, 1, -1)
    rel_f = _unshard_last(rel_g)

    def shard_act(a):
        return a.reshape(a.shape[0], a.shape[2], a.shape[3])

    def grad_into(name, slot, mm):
        gi, local = slot_of(bwd_groups, name, slot)
        partial[gi][name] = mm(partial[gi][name], local)

    def ffn_fwd(h, lw):
        w_gu, l_gu = weight("ffn_gu", lw)
        w_dn, l_dn = weight("ffn_down", lw)
        gu, a = _ffn_up_fwd(h, w_gu, l_gu)
        y = _matmul("ffn_down_fwd", a[:, None], w_dn, (1, 1, t, d), F32, lb=l_dn, b_merge=2, reduce=True)
        return y[0, 0], (gu, a)

    def ffn_bwd(dy, h, saved, lw):
        gu, a = saved
        w_gu, l_gu = weight("ffn_gu", lw)
        w_dn, l_dn = weight("ffn_down", lw)
        dgu = _ffn_down_bwd_a(dy, w_dn, l_dn, gu).reshape((N_DEV,) + gu.shape[2:])
        grad_into("ffn_down", lw, lambda buf, lo: _matmul(
            "ffn_down_bwd_w", a[:, None], _as4(dy), buf.shape, BF16, ta=True, lo=lo, out_merge=2, out_buf=buf))
        dh = _matmul("ffn_gu_bwd_a", dgu[:, None], w_gu, (1, 1, t, d), F32, lb=l_gu, tb=True, reduce=True)
        grad_into("ffn_gu", lw, lambda buf, lo: _matmul(
            "ffn_gu_bwd_w", dgu[:, None], _as4(h), buf.shape, BF16, ta=True, lo=lo, out_buf=buf))
        return dh[0, 0], {}

    def gmlp_params(l):
        return (b_in_f[l], gln_g_f[l], gln_b_f[l], gmlp_w_s[l], gmlp_b_s[l][:, :, None])

    def gmlp_fwd(h, l):
        w_in, l_in = weight("gmlp_w_in", l)
        w_out, l_out = weight("gmlp_w_out", l)
        n = w_in.shape[-1]
        zpre = _matmul("gmlp_in_fwd", _as4(h), w_in, (N_DEV, 1, t, n), F32, lb=l_in)
        gated = _gmlp_mid_fwd(shard_act(zpre), *gmlp_params(l))
        y = _matmul("gmlp_out_fwd", gated[:, None], w_out, (1, 1, t, d), F32, lb=l_out, b_merge=2, reduce=True)
        return y[0, 0], (zpre, gated)

    def gmlp_bwd(dy, h, saved, l):
        zpre, gated = saved
        w_in, l_in = weight("gmlp_w_in", l)
        w_out, l_out = weight("gmlp_w_out", l)
        n = w_in.shape[-1]
        dgated = _matmul("gmlp_out_bwd_a", _as4(dy), w_out, (half, 1, t, n), F32, lb=l_out, b_merge=2, tb=True)
        grad_into("gmlp_w_out", l, lambda buf, lo: _matmul(
            "gmlp_out_bwd_w", gated[:, None], _as4(dy), buf.shape, BF16, ta=True, lo=lo, out_merge=2, out_buf=buf))
        dz, dws, dbs, dlng, dlnb, dbin = _gmlp_mid_bwd(shard_act(zpre), shard_act(dgated), *gmlp_params(l))
        dh = _matmul("gmlp_in_bwd_a", dz[:, None], w_in, (1, 1, t, d), F32, lb=l_in, tb=True, reduce=True)
        grad_into("gmlp_w_in", l, lambda buf, lo: _matmul(
            "gmlp_in_bwd_w", _as4(h), dz[:, None], buf.shape, BF16, ta=True, lo=lo, out_buf=buf))
        small_grads = dict(gmlp_w_s=dws, gmlp_b_s=dbs[:, :, 0], gmlp_ln_g=dlng.reshape(-1),
                           gmlp_ln_b=dlnb.reshape(-1), gmlp_b_in=dbin.reshape(-1))
        return dh[0, 0], small_grads

    def attn_fwd(h, j, kvp):
        rel_vec = _rel_vector(rel_f[j])
        w_q, l_q = weight("attn_w_q", j)
        w_o, l_o = weight("attn_w_o", j)
        q = _matmul("attn_q_fwd", _as4(h), w_q, (1, 1, t, d), BF16, lb=l_q, b_merge=N_DEV, reduce=True)[0, 0]
        o = _attn_fwd(q, kvp, rel_vec)
        y = _matmul("attn_o_fwd", _as4(o), w_o, (1, 1, t, d), F32, lb=l_o, b_merge=N_DEV, reduce=True)
        return y[0, 0], (q, o, rel_vec)

    def attn_bwd(dy, h, saved, j, kvp, dkv_acc):
        q, o, rel_vec = saved
        w_q, l_q = weight("attn_w_q", j)
        w_o, l_o = weight("attn_w_o", j)
        do = _matmul("attn_o_bwd_a", _as4(dy), w_o, (1, 1, t, d), BF16, lb=l_o, b_merge=N_DEV, tb=True)[0, 0]
        grad_into("attn_w_o", j, lambda buf, lo: _matmul(
            "attn_o_bwd_w", _as4(o), _as4(dy), buf.shape, BF16, ta=True, lo=lo, out_merge=N_DEV, out_buf=buf))
        dq, dk, dv, dsc = _attn_bwd(q, do, kvp, rel_vec, *dkv_acc)
        drel = _rel_bias_grad(_skew_diagonals(dsc))
        dh = _matmul("attn_q_bwd_a", _as4(dq), w_q, (1, 1, t, d), F32, lb=l_q, b_merge=N_DEV, tb=True)
        grad_into("attn_w_q", j, lambda buf, lo: _matmul(
            "attn_q_bwd_w", _as4(h), _as4(dq), buf.shape, BF16, ta=True, lo=lo, out_merge=N_DEV, out_buf=buf))
        return dh[0, 0], dict(attn_rel_bias=drel, dkv=(dk, dv))

    tape = []
    kvp = None
    kv_tape = None
    first_use = {(0, 0): 0, (0, 1): 1, (0, 2): 2, (1, 0): 3, (1, 1): 4, (1, 2): 5, (2, 0): 6, (2, 1): 7,
                 (2, 2): 8, (3, 0): 9}
    h = _modulate(x, mod[0, 1], mod[0, 0] + start_token)
    for l in range(DEPTH):
        for i in range(3):
            if (l, i) in first_use:
                land_group(first_use[l, i], x)
            scl, gate = mod[l, 3 * i + 1], mod[l, 3 * i + 2]
            wgt = 1.0 if i == 1 else 0.5
            gw = wgt * (1.0 + gate)
            if i != 1:
                y, saved = ffn_fwd(h, 2 * l + i // 2)
            elif l < N_A:
                y, saved = gmlp_fwd(h, l)
            else:
                y, saved = attn_fwd(h, l - N_A, kvp)
            nl, ni = (l, i + 1) if i < 2 else (l + 1, 0)
            readers = [(mod[nl, 3 * ni + 1], mod[nl, 3 * ni])] if nl < DEPTH else []
            shared_kv = (l, i) == (N_A - 1, 2)
            if shared_kv:
                readers.append((mkv[1], mkv[0]))
            outs = _ln_res_fwd(x, y, gw, ln_g_f[l, i][None], ln_b_f[l, i][None], readers)
            tape.append((x, h, y, gw, scl, saved))
            x = outs[0]
            h = outs[1] if nl < DEPTH else None
            if shared_kv:
                hkv = outs[-1]
                w_kvg, l_kv = weight("w_kv", 0)
                n = w_kvg.shape[-1]
                kv = _matmul("kv_fwd", _as4(hkv), w_kvg, (N_DEV, 1, t, n), BF16, lb=l_kv)
                kvp = jnp.pad(shard_act(kv), ((0, 0), (LEFT_PAD, 0), (0, 0)))
                kv_tape = hkv

    loss_part, dx = _loss_head(x, target)
    loss = lax.psum(loss_part[0, 0], MESH_AXES)

    d_mod = [[None] * n_mod for _ in range(DEPTH)]
    d_ln_g = [[None] * 3 for _ in range(DEPTH)]
    d_ln_b = [[None] * 3 for _ in range(DEPTH)]
    small_grads = {k: [None] * N_A for k in ("gmlp_w_s", "gmlp_b_s", "gmlp_ln_g", "gmlp_ln_b", "gmlp_b_in")}
    d_rel = [None] * n_b
    dkv_acc = ()
    d_mkv = None
    reductions = [None] * DEPTH
    sent_token = None
    readers = []
    for l in reversed(range(DEPTH)):
        if l == N_A - 1:
            hkv = kv_tape
            w_kvg, l_kv = weight("w_kv", 0)
            dkv = jnp.concatenate(dkv_acc)[:, LEFT_PAD:, :].astype(BF16)[:, None]
            dhkv = _matmul("kv_bwd_a", dkv, w_kvg, (1, 1, t, d), F32, lb=l_kv, tb=True, reduce=True)[0, 0]
            grad_into("w_kv", 0, lambda buf, lo: _matmul(
                "kv_bwd_w", _as4(hkv), dkv, buf.shape, BF16, ta=True, lo=lo, out_buf=buf))
            readers.append((dhkv, mkv[1], None))
        for i in reversed(range(3)):
            x_in, h, y, gw, scl, saved = tape[3 * l + i]
            wgt = 1.0 if i == 1 else 0.5
            if sent_token is not None:
                gw = gw + sent_token
                sent_token = None
            res = _ln_res_bwd(x_in, y, gw, ln_g_f[l, i][None], ln_b_f[l, i][None], dx,
                              [(r[0], r[1]) for r in readers])
            dx_res, dy, dgw, dg, db = res[:5]
            for k, (_, _, slot) in enumerate(readers):
                dscl_k, dshift_k = res[5 + 2 * k][0], res[6 + 2 * k][0]
                if slot is None:
                    d_mkv = jnp.concatenate([dshift_k, dscl_k])
                else:
                    d_mod[slot[0]][slot[1]], d_mod[slot[0]][slot[1] + 1] = dshift_k, dscl_k
            d_ln_g[l][i], d_ln_b[l][i] = dg[0], db[0]
            if i != 1:
                dh, extra = ffn_bwd(dy, h, saved, 2 * l + i // 2)
            elif l < N_A:
                dh, extra = gmlp_bwd(dy, h, saved, l)
                for k, g in extra.items():
                    small_grads[k][l] = g
            else:
                dh, extra = attn_bwd(dy, h, saved, l - N_A, kvp, dkv_acc)
                d_rel[l - N_A] = extra["attn_rel_bias"]
                dkv_acc = extra["dkv"]
            d_mod[l][3 * i + 2] = wgt * dgw[0]
            dx = dx_res
            readers = [(dh, scl, (l, 3 * i))]
        if l > 0:
            srcs = [partial[l][k] for k in bwd_groups[l]]
            lands = [lax.empty((N_DEV - 1,) + s.shape[1:], BF16) for s in srcs]
            reductions[l] = _partials_send_start(f"partials_send_start_{l}", srcs, lands, True)
            sent_token = reductions[l][4][0, 0]
    (dh, scl, _), = readers
    dx, dscl, dshift = _mod_bwd(dx, dh, tape[0][0], scl)
    d_mod[0][0], d_mod[0][1] = dshift[0], dscl[0]
    grad_x = dx[None]

    d_mod_arr = jnp.stack([jnp.concatenate(r) for r in d_mod])
    small_part = dict(
        b_ada=d_mod_arr, b_ada_kv=d_mkv,
        ln_g=jnp.stack([jnp.stack(r) for r in d_ln_g]), ln_b=jnp.stack([jnp.stack(r) for r in d_ln_b]),
        gmlp_b_in=jnp.stack(small_grads["gmlp_b_in"]), gmlp_ln_g=jnp.stack(small_grads["gmlp_ln_g"]),
        gmlp_ln_b=jnp.stack(small_grads["gmlp_ln_b"]), gmlp_w_s=jnp.stack(small_grads["gmlp_w_s"]),
        gmlp_b_s=jnp.stack(small_grads["gmlp_b_s"]), attn_rel_bias=jnp.stack(d_rel))
    small_names = list(small_part)
    sp_shapes = [small_part[k].shape for k in small_names]
    sp_all = _all_gather_small(_pack([small_part[k] for k in small_names]),
                               after=[partial[0][k] for k in bwd_groups[0]])

    from_sibling = _partials_d2d([partial[0][k] for k in bwd_groups[0]])
    sums = [_chip_sum(partial[0][k], r1, core) for k, r1 in zip(bwd_groups[0], from_sibling)]
    lands = [lax.empty((len(_ALL_CHIPS) - 1,) + s.shape[1:], BF16) for s in sums]
    reductions[0] = _partials_send_start("partials_send_start_0", sums, lands, False, after=sp_all)
    sent_token = reductions[0][4][0, 0]
    c4 = c4 + sent_token

    sp_sum = _sum_parts(sp_all)
    full_grads = dict(zip(small_names, _unpack(sp_sum, sp_shapes)))
    per_dev = dict(zip(small_names, _unpack(sp_all, sp_shapes, lead=(N_DEV,))))

    def my_cols(a, width):
        return lax.dynamic_slice_in_dim(a, me * width, width, axis=a.ndim - 1)

    grads = {}
    grads["b_ada"] = full_grads["b_ada"]
    grads["b_ada_kv"] = full_grads["b_ada_kv"]
    grads["gmlp_w_s"] = full_grads["gmlp_w_s"]
    grads["gmlp_b_s"] = full_grads["gmlp_b_s"]
    for k in ("ln_g", "ln_b", "gmlp_b_in", "gmlp_ln_g", "gmlp_ln_b", "attn_rel_bias"):
        grads[k] = my_cols(full_grads[k], weights[k].shape[-1])

    dmod_cols = jnp.transpose(my_cols(per_dev["b_ada"], mod_w), (1, 0, 2))[:, None]
    grads["w_ada"] = _matmul("ada_bwd_w", c4, dmod_cols, (DEPTH, 1, d, mod_w), F32, ta=True,
                             a_silu=True)[:, 0]
    dkv_cols = my_cols(per_dev["b_ada_kv"], kv_w)[None, None]
    grads["w_ada_kv"] = _matmul("ada_kv_bwd_w", c4, dkv_cols, (1, 1, d, kv_w), F32, ta=True,
                                a_silu=True)[0, 0]

    delta, new_m, new_v = {}, {}, {}
    first = jnp.zeros((1,), jnp.int32)

    def flat2(a, cols):
        return a.reshape(-1, cols)

    done = None
    for k in ("w_ada", "w_ada_kv"):
        w = weights[k]
        cols = w.shape[-1]
        res = _adamw(grads[k].reshape(1, -1, cols), first, None, flat2(w, cols), flat2(mom1[k], cols),
                     flat2(mom2[k], cols), after=done)
        grads[k], delta[k], new_m[k], new_v[k] = (a.reshape(w.shape) for a in res)
        done = res[0][:SUBLANES, :LANES]

    tiny = [k for k in order if k not in delta and k not in big_names]
    tiny_shapes = [weights[k].shape for k in tiny]
    tiny_out = _adamw((_pack([grads[k] for k in tiny]) + sent_token)[None], first, None,
                      _pack([weights[k] for k in tiny]), _pack([mom1[k] for k in tiny]),
                      _pack([mom2[k] for k in tiny]), after=done)
    for dst, arr in zip((grads, delta, new_m, new_v), tiny_out):
        for k, val in zip(tiny, _unpack(arr, tiny_shapes)):
            dst[k] = val

    def opt_view(k, a):
        a = jnp.swapaxes(a, -1, -2) if k in swapped else a
        return a.reshape(-1, a.shape[-1])

    def opt_unview(k, a):
        s = weights[k].shape
        return jnp.swapaxes(a.reshape(s[:-2] + (s[-1], s[-2])), -1, -2) if k in swapped else a.reshape(s)

    bufs = {k: [lax.empty(opt_view(k, weights[k]).shape, F32) for _ in range(4)] for k in big_names}
    done = tiny_out[0]
    me_idx = me.astype(jnp.int32).reshape(1)
    for l in reversed(range(DEPTH)):
        send_sems, recv_sems, srcs, lands, _ = reductions[l]
        srcs, lands = _partials_send_wait(f"partials_send_wait_{l}", srcs, lands, send_sems, recv_sems, l > 0, done)
        for k, own, got in zip(bwd_groups[l], srcs, lands):
            cols = own.shape[-1]
            slot_rows = int(np.prod(own.shape[2:-1]))
            bufs[k] = _adamw(own.reshape(own.shape[0], -1, cols), me_idx if l > 0 else chip,
                             got.reshape(got.shape[0], -1, cols),
                             opt_view(k, weights[k]), opt_view(k, mom1[k]), opt_view(k, mom2[k]),
                             row0=bwd_groups[l][k][0] * slot_rows, bufs=bufs[k], after=done)
            done = bufs[k][0][:SUBLANES, :LANES]
    for k in big_names:
        grads[k], delta[k], new_m[k], new_v[k] = (opt_unview(k, b) for b in bufs[k])

    return (loss, grad_x, *[grads[k] for k in order], *[delta[k] for k in order],
            *[new_m[k] for k in order], *[new_v[k] for k in order])
```

```python
import functools

import numpy as np
import jax
import jax.numpy as jnp
from jax import lax
from jax.experimental import pallas as pl
from jax.experimental.pallas import tpu as pltpu

F32 = jnp.float32
BF16 = jnp.bfloat16
MESH_AXES = ("x", "y", "c")
N_DEV = 8
MESH_ID = pl.DeviceIdType.MESH

DEPTH = 4
N_A = 2
CHUNK = 64
N_HEADS = 16
LEFT_CHUNKS = 8
BAND = (LEFT_CHUNKS + 1) * CHUNK
LEFT_PAD = LEFT_CHUNKS * CHUNK
MAX_REL = 4 * CHUNK
N_REL = (CHUNK - 1) + MAX_REL + 1
GMLP_WINDOW = 128
GMLP_GROUPS = 8
ALPHA = (2.0 * DEPTH) ** 0.25
LN_EPS = 1e-5
ADAM_LR = 0.001
ADAM_B1 = 0.9
ADAM_B2 = 0.999
ADAM_EPS = 1e-08
ADAM_WD = 0.01
ADAM_STEP = 10

V7X_VMEM_BYTES = 64 * 1024 * 1024
VMEM_LIMIT = V7X_VMEM_BYTES - 8 * 1024 * 1024
LANES = 128
SUBLANES = 8
MM_BLOCK = 2048
BIG_ROW_BLOCK = 1024
ROW_BLOCK = 512
OPT_ROW_BLOCK = 256

_ANY = pl.BlockSpec(memory_space=pl.ANY)
_VMEM = pl.BlockSpec(memory_space=pltpu.VMEM)


def _params(sem=None):
    return pltpu.CompilerParams(dimension_semantics=sem, vmem_limit_bytes=VMEM_LIMIT)


def _row_block(rows, target):
    for d in range(min(rows, target), 0, -1):
        if rows % d == 0 and (d % SUBLANES == 0 or d == rows):
            return d
    return rows


def _matmul(name, a, b, out_shape4, out_dtype, *, la=0, lb=0, lo=0, ta=False, tb=False,
            reduce=False, b_merge=1, out_merge=1, out_buf=None, a_silu=False, after=None):
    ja_n, _, a_r, a_c = a.shape
    jb_n, _, b_r, b_c = b.shape
    jo_n, _, o_r, o_c = out_shape4
    m_tot = a_c if ta else a_r
    k_a = a_r if ta else a_c
    b_rows = b_merge * b_r
    k_c = b_c if tb else b_rows
    n = b_rows if tb else b_c
    n_chunks = (jb_n // b_merge) if reduce else 1
    natural_k = reduce and ja_n == 1
    assert n == o_c, (name, n, o_c)
    assert k_a ==(k_c * n_chunks if natural_k else k_c), (name, k_a, k_c, n_chunks)
    bk = k_c if (k_c <= MM_BLOCK or (b_merge > 1 and not tb)) else MM_BLOCK
    assert k_c % bk == 0
    nkk = k_c // bk
    kg = 2 if (reduce and ja_n > 1 and nkk == 1 and not ta and n_chunks % 2 == 0) else 1
    nk = n_chunks * nkk // kg
    m_out = out_merge * o_r
    assert m_tot == m_out, (name, m_tot, m_out)
    bm = m_tot if (m_tot <= MM_BLOCK or out_merge > 1) else MM_BLOCK
    assert m_tot % bm == 0
    jo_blocks = jo_n // out_merge

    def a_index(j, m, k):
        kj, kk = k // nkk, k % nkk
        ja = 0 if ja_n == 1 else (kj if reduce else j)
        ke = kk + kj * nkk if natural_k else kk
        return (ja, la, ke, m) if ta else (ja, la, m, ke)

    def b_index(j, m, k):
        kj, kk = k // nkk, k % nkk
        jb = 0 if jb_n == b_merge else (kj if reduce else j)
        return (jb, lb, 0, kk) if tb else (jb, lb, kk, 0)

    def o_index(j, m, k):
        return (j, lo, 0, 0) if out_merge > 1 else (j, lo, m, 0)

    a_block = (None, None, bk, bm) if ta else (None if kg == 1 else kg, None, bm, bk)
    if b_merge > 1:
        b_block = (kg * b_merge, None, b_r, bk if tb else n)
    else:
        b_block = (None if kg == 1 else kg, None) + ((n, bk) if tb else (bk, n))
    o_block = (out_merge, None, o_r, n) if out_merge > 1 else (None, None, bm, n)
    dims = (((0 if ta else 1,), (1 if tb else 0,)), ((), ()))

    in_place = nk > 1 and out_dtype == F32 and out_merge == 1
    use_acc = nk > 1 and not in_place

    def body(a_ref, b_ref, *rest):
        o_ref = rest[-2] if use_acc else rest[-1]
        k = pl.program_id(2)
        av = a_ref[...]
        if a_silu:
            af = av.astype(F32)
            av = af * jax.nn.sigmoid(af)
        bv = b_ref[...]
        if kg > 1:
            bv = bv.reshape(kg, -1, bv.shape[-1])
            prod = sum(lax.dot_general(av[g].astype(BF16), bv[g].astype(BF16), dims, preferred_element_type=F32)
                       for g in range(kg))
        else:
            if b_merge > 1:
                bv = bv.reshape(b_rows, bv.shape[-1])
            prod = lax.dot_general(av.astype(BF16), bv.astype(BF16), dims, preferred_element_type=F32)

        def emit(val):
            val = val.astype(out_dtype)
            o_ref[...] = val.reshape(out_merge, o_r, n) if out_merge > 1 else val

        if nk == 1:
            emit(prod)
            return
        acc_ref = o_ref if in_place else rest[-1]

        @pl.when(k == 0)
        def _():
            acc_ref[...] = prod

        @pl.when(k > 0)
        def _():
            acc_ref[...] += prod

        if use_acc:
            @pl.when(k == nk - 1)
            def _():
                emit(acc_ref[...])

    in_specs = [pl.BlockSpec(a_block, a_index), pl.BlockSpec(b_block, b_index)]
    operands = [a, b]
    aliases = {}
    if out_buf is not None:
        assert out_buf.shape == tuple(out_shape4) and out_buf.dtype == out_dtype
        in_specs.append(_ANY)
        operands.append(out_buf)
        aliases = {2: 0}
    if after is not None:
        in_specs.append(_ANY)
        operands.append(after)
    return pl.pallas_call(
        body, name=name,
        grid=(jo_blocks, m_tot // bm, nk),
        in_specs=in_specs,
        out_specs=pl.BlockSpec(o_block, o_index),
        out_shape=jax.ShapeDtypeStruct(tuple(out_shape4), out_dtype),
        scratch_shapes=[pltpu.VMEM((bm, n), F32)] if use_acc else [],
        input_output_aliases=aliases,
        compiler_params=_params(("parallel", "parallel", "arbitrary")),
    )(*operands)


def _as4(a):
    return a.reshape((1,) * (4 - a.ndim) + a.shape)


def _row_call(name, body, ins, outs, t, *, acc_outs=()):
    bt = _row_block(t, ROW_BLOCK)

    def spec(arr, tiled):
        if tiled:
            return pl.BlockSpec((bt,) + tuple(arr.shape[1:]), lambda i: (i,) + (0,) * (arr.ndim - 1))
        return pl.BlockSpec(tuple(arr.shape), lambda i: (0,) * arr.ndim)

    return pl.pallas_call(
        body, name=name, grid=(t // bt,),
        in_specs=[spec(a, tl) for a, tl in ins],
        out_specs=[spec(o, tl) for o, tl in outs],
        out_shape=[jax.ShapeDtypeStruct(o.shape, o.dtype) for o, _ in outs],
        compiler_params=_params(("arbitrary",) if acc_outs else ("parallel",)),
    )(*[a for a, _ in ins])


def _sds(shape, dtype):
    return jax.ShapeDtypeStruct(tuple(shape), dtype)


def _modulate(x, scl, shift):
    t, d = x.shape

    def body(x_ref, s_ref, b_ref, h_ref):
        h_ref[...] = (x_ref[...] * (1.0 + s_ref[...]) + b_ref[...]).astype(BF16)

    return _row_call("modulate", body, [(x, True), (scl, False), (shift, False)],
                     [(_sds((t, d), BF16), True)], t)[0]


def _ln_stats(r):
    mu = jnp.mean(r, axis=-1, keepdims=True)
    rc = r - mu
    var = jnp.mean(rc * rc, axis=-1, keepdims=True)
    rstd = lax.rsqrt(var + LN_EPS)
    return rc * rstd, rstd


def _ln_res_fwd(x, y, gw, g, b, mods=()):
    t, d = x.shape
    n_mod = len(mods)

    def body(x_ref, y_ref, gw_ref, g_ref, b_ref, *rest):
        mod_refs, o_ref, h_refs = rest[:2 * n_mod], rest[2 * n_mod], rest[2 * n_mod + 1:]
        r = ALPHA * x_ref[...] + gw_ref[...] * y_ref[...]
        xhat, _ = _ln_stats(r)
        xn = xhat * g_ref[...] + b_ref[...]
        o_ref[...] = xn
        for k in range(n_mod):
            h_refs[k][...] = (xn * (1.0 + mod_refs[2 * k][...]) + mod_refs[2 * k + 1][...]).astype(BF16)

    vecs = [(v, False) for pair in mods for v in pair]
    return _row_call("ln_res_fwd", body,
                     [(x, True), (y, True), (gw, False), (g, False), (b, False)] + vecs,
                     [(_sds((t, d), F32), True)] + [(_sds((t, d), BF16), True)] * n_mod, t)


def _ln_res_bwd(x, y, gw, g, b, dx_base, pairs=()):
    t, d = x.shape
    n_pair = len(pairs)

    def body(x_ref, y_ref, gw_ref, g_ref, b_ref, dxb_ref, *rest):
        pair_refs, outs = rest[:2 * n_pair], rest[2 * n_pair:]
        dx_ref, dy_ref = outs[0], outs[1]
        sums = outs[2:]

        @pl.when(pl.program_id(0) == 0)
        def _():
            for r in sums:
                r[...] = jnp.zeros_like(r)

        yv = y_ref[...]
        gwv = gw_ref[...]
        gv = g_ref[...]
        xhat, rstd = _ln_stats(ALPHA * x_ref[...] + gwv * yv)
        dxn = dxb_ref[...]
        if n_pair:
            xn = xhat * gv + b_ref[...]
            for k in range(n_pair):
                dh = pair_refs[2 * k][...]
                dxn = dxn + dh * (1.0 + pair_refs[2 * k + 1][...])
                sums[3 + 2 * k][...] += jnp.sum(dh * xn, axis=0, keepdims=True)
                sums[4 + 2 * k][...] += jnp.sum(dh, axis=0, keepdims=True)
        dxh = dxn * gv
        m1 = jnp.mean(dxh, axis=-1, keepdims=True)
        m2 = jnp.mean(dxh * xhat, axis=-1, keepdims=True)
        dr = rstd * (dxh - m1 - xhat * m2)
        dx_ref[...] = ALPHA * dr
        dy_ref[...] = (gwv * dr).astype(BF16)
        sums[0][...] += jnp.sum(dr * yv, axis=0, keepdims=True)
        sums[1][...] += jnp.sum(dxn * xhat, axis=0, keepdims=True)
        sums[2][...] += jnp.sum(dxn, axis=0, keepdims=True)

    vec = _sds((1, d), F32)
    n_sum = 3 + 2 * n_pair
    ins = [(x, True), (y, True), (gw, False), (g, False), (b, False), (dx_base, True)]
    for dh, scl in pairs:
        ins += [(dh, True), (scl, False)]
    return _row_call("ln_res_bwd", body, ins,
                     [(_sds((t, d), F32), True), (_sds((t, d), BF16), True)] + [(vec, False)] * n_sum, t,
                     acc_outs=tuple(range(2, 2 + n_sum)))


def _mod_bwd(dx_res, dh, x, scl):
    t, d = x.shape

    def body(dxr_ref, dh_ref, x_ref, s_ref, dx_ref, ds_ref, db_ref):
        @pl.when(pl.program_id(0) == 0)
        def _():
            ds_ref[...] = jnp.zeros_like(ds_ref)
            db_ref[...] = jnp.zeros_like(db_ref)

        dh = dh_ref[...]
        dx_ref[...] = dxr_ref[...] + dh * (1.0 + s_ref[...])
        ds_ref[...] += jnp.sum(dh * x_ref[...], axis=0, keepdims=True)
        db_ref[...] += jnp.sum(dh, axis=0, keepdims=True)

    vec = _sds((1, d), F32)
    return _row_call("mod_bwd", body, [(dx_res, True), (dh, True), (x, True), (scl, False)],
                     [(_sds((t, d), F32), True), (vec, False), (vec, False)], t, acc_outs=(1, 2))


def _loss_head(y, target):
    t, d = y.shape

    def body(y_ref, t_ref, l_ref, dy_ref):
        @pl.when(pl.program_id(0) == 0)
        def _():
            l_ref[...] = jnp.zeros_like(l_ref)

        err = y_ref[...] - t_ref[...]
        dy_ref[...] = err * (1.0 / d)
        part = 0.5 * jnp.sum(jnp.mean(err * err, axis=-1, keepdims=True), axis=0, keepdims=True)
        l_ref[...] += jnp.broadcast_to(part, l_ref.shape)

    return _row_call("loss_head", body, [(y, True), (target, True)],
                     [(_sds((SUBLANES, LANES), F32), False), (_sds((t, d), F32), True)], t,
                     acc_outs=(0,))


def _sigmoid(x):
    return 0.5 * jnp.tanh(0.5 * x) + 0.5


def _ffn_up_fwd(h, w_gu, lb, after=None):
    t, d = h.shape
    n = w_gu.shape[-1]
    half = N_DEV // 2
    bt = _row_block(t, BIG_ROW_BLOCK)
    extra = [] if after is None else [after]

    def body(h_ref, wg_ref, wu_ref, *rest):
        fac_ref, a_ref = rest[-2:]
        hv = h_ref[...]
        g = jnp.dot(hv, wg_ref[...], preferred_element_type=F32)
        u = jnp.dot(hv, wu_ref[...], preferred_element_type=F32)
        sig = _sigmoid(g)
        silu = g * sig
        fac_ref[0] = u * (sig + silu * (1.0 - sig))
        fac_ref[1] = silu
        a_ref[...] = (silu * u).astype(BF16)

    return pl.pallas_call(
        body, name="ffn_up_fwd", grid=(half, t // bt),
        in_specs=[pl.BlockSpec((bt, d), lambda j, i: (i, 0)),
                  pl.BlockSpec((None, None, d, n), lambda j, i: (j, lb, 0, 0)),
                  pl.BlockSpec((None, None, d, n), lambda j, i: (half + j, lb, 0, 0))] + [_ANY] * len(extra),
        out_specs=[pl.BlockSpec((2, None, bt, n), lambda j, i: (0, j, i, 0)),
                   pl.BlockSpec((None, bt, n), lambda j, i: (j, i, 0))],
        out_shape=[_sds((2, half, t, n), F32), _sds((half, t, n), BF16)],
        compiler_params=_params(("parallel", "parallel")),
    )(h, w_gu, w_gu, *extra)


def _ffn_down_bwd_a(dy, w_down, lb, fac):
    t, d = dy.shape
    _, half, _, n = fac.shape
    r = w_down.shape[2]
    bt = _row_block(t, MM_BLOCK)

    def body(dy_ref, w_ref, fac_ref, d_ref):
        da = lax.dot_general(dy_ref[...], w_ref[...].reshape(2 * r, d), (((1,), (1,)), ((), ())),
                             preferred_element_type=F32)
        d_ref[0] = (da * fac_ref[0]).astype(BF16)
        d_ref[1] = (da * fac_ref[1]).astype(BF16)

    return pl.pallas_call(
        body, name="ffn_down_bwd_a", grid=(half, t // bt),
        in_specs=[pl.BlockSpec((bt, d), lambda j, i: (i, 0)),
                  pl.BlockSpec((2, None, r, d), lambda j, i: (j, lb, 0, 0)),
                  pl.BlockSpec((2, None, bt, n), lambda j, i: (0, j, i, 0))],
        out_specs=pl.BlockSpec((2, None, bt, n), lambda j, i: (0, j, i, 0)),
        out_shape=_sds((2, half, t, n), BF16),
        compiler_params=_params(("parallel", "parallel")),
    )(dy, w_down, fac)


_INV_SQRT2 = 0.7071067811865476
_INV_SQRT_2PI = 0.3989422804014327


def _gelu(z):
    return 0.5 * z * (1.0 + lax.erf(z * _INV_SQRT2))


def _gelu_grad(z):
    return 0.5 * (1.0 + lax.erf(z * _INV_SQRT2)) + z * jnp.exp(-0.5 * z * z) * _INV_SQRT_2PI


def _window_mask():
    t_out = lax.broadcasted_iota(jnp.int32, (GMLP_WINDOW, GMLP_WINDOW), 0)
    s_in = lax.broadcasted_iota(jnp.int32, (GMLP_WINDOW, GMLP_WINDOW), 1)
    return (s_in // CHUNK) <= (t_out // CHUNK)


def _gmlp_recompute(z_ref, bin_ref, lng_ref, lnb_ref):
    half = N_DEV // 2
    z = z_ref[...] + bin_ref[...]
    ge = _gelu(z)
    u = ge[:half]
    v = ge[half:]
    width = half * v.shape[-1]
    mu = jnp.sum(jnp.sum(v, axis=0), axis=-1, keepdims=True) / width
    vc = v - mu
    var = jnp.sum(jnp.sum(vc * vc, axis=0), axis=-1, keepdims=True) / width
    rstd = lax.rsqrt(var + LN_EPS)
    xhat = vc * rstd
    vn = xhat * lng_ref[...] + lnb_ref[...]
    return z, u, xhat, rstd, vn


def _gmlp_mid_fwd(zpre, b_in, ln_g, ln_b, w_s, b_s):
    _, t, n = zpre.shape
    half = N_DEV // 2
    gd = half * n // GMLP_GROUPS
    per = n // gd
    w = GMLP_WINDOW

    def body(z_ref, bin_ref, lng_ref, lnb_ref, ws_ref, bs_ref, o_ref):
        _, u, _, _, vn = _gmlp_recompute(z_ref, bin_ref, lng_ref, lnb_ref)
        mask = _window_mask()
        for g in range(GMLP_GROUPS):
            sh, c0 = g // per, (g % per) * gd
            wsm = jnp.where(mask, ws_ref[g], 0.0).astype(BF16)
            s = jnp.dot(wsm, vn[sh][:, c0:c0 + gd].astype(BF16), preferred_element_type=F32) + bs_ref[g]
            o_ref[sh, :, c0:c0 + gd] = (u[sh][:, c0:c0 + gd] * s).astype(BF16)

    whole = lambda a: pl.BlockSpec(tuple(a.shape), lambda i: (0,) * a.ndim)
    return pl.pallas_call(
        body, name="gmlp_mid_fwd", grid=(t // w,),
        in_specs=[pl.BlockSpec((N_DEV, w, n), lambda i: (0, i, 0)),
                  whole(b_in), whole(ln_g), whole(ln_b), whole(w_s), whole(b_s)],
        out_specs=pl.BlockSpec((half, w, n), lambda i: (0, i, 0)),
        out_shape=_sds((half, t, n), BF16),
        compiler_params=_params(("parallel",)),
    )(zpre, b_in, ln_g, ln_b, w_s, b_s)


def _gmlp_mid_bwd(zpre, dgated, b_in, ln_g, ln_b, w_s, b_s):
    _, t, n = zpre.shape
    half = N_DEV // 2
    gd = half * n // GMLP_GROUPS
    per = n // gd
    w = GMLP_WINDOW
    width = half * n

    def body(z_ref, dg_ref, bin_ref, lng_ref, lnb_ref, ws_ref, bs_ref,
             dz_ref, dws_ref, dbs_ref, dlng_ref, dlnb_ref, dbin_ref, du_ref, dvn_ref):
        @pl.when(pl.program_id(0) == 0)
        def _():
            for r in (dws_ref, dbs_ref, dlng_ref, dlnb_ref, dbin_ref):
                r[...] = jnp.zeros_like(r)

        z, u, xhat, rstd, vn = _gmlp_recompute(z_ref, bin_ref, lng_ref, lnb_ref)
        mask = _window_mask()
        for g in range(GMLP_GROUPS):
            sh, c0 = g // per, (g % per) * gd
            wsm = jnp.where(mask, ws_ref[g], 0.0).astype(BF16)
            vg = vn[sh][:, c0:c0 + gd].astype(BF16)
            s = jnp.dot(wsm, vg, preferred_element_type=F32) + bs_ref[g]
            dgt = dg_ref[sh, :, c0:c0 + gd]
            ds = dgt * u[sh][:, c0:c0 + gd]
            du_ref[sh, :, c0:c0 + gd] = dgt * s
            dsb = ds.astype(BF16)
            dws = lax.dot_general(dsb, vg, (((1,), (1,)), ((), ())), preferred_element_type=F32)
            dws_ref[g] += jnp.where(mask, dws, 0.0)
            dbs_ref[g] += jnp.sum(ds, axis=-1, keepdims=True)
            dvn_ref[sh, :, c0:c0 + gd] = lax.dot_general(wsm, dsb, (((0,), (0,)), ((), ())),
                                                         preferred_element_type=F32)
        dvn = dvn_ref[...]
        dlng_ref[...] += jnp.sum(dvn * xhat, axis=1, keepdims=True)
        dlnb_ref[...] += jnp.sum(dvn, axis=1, keepdims=True)
        dxh = dvn * lng_ref[...]
        m1 = jnp.sum(jnp.sum(dxh, axis=0), axis=-1, keepdims=True) / width
        m2 = jnp.sum(jnp.sum(dxh * xhat, axis=0), axis=-1, keepdims=True) / width
        dv = rstd * (dxh - m1 - xhat * m2)
        gg = _gelu_grad(z)
        dzu = du_ref[...] * gg[:half]
        dzv = dv * gg[half:]
        dz_ref[:half] = dzu.astype(BF16)
        dz_ref[half:] = dzv.astype(BF16)
        dbin_ref[:half] += jnp.sum(dzu, axis=1, keepdims=True)
        dbin_ref[half:] += jnp.sum(dzv, axis=1, keepdims=True)

    whole = lambda a: pl.BlockSpec(tuple(a.shape), lambda i: (0,) * a.ndim)
    outs = [_sds((N_DEV, t, n), BF16), _sds(w_s.shape, F32), _sds(b_s.shape, F32),
            _sds(ln_g.shape, F32), _sds(ln_b.shape, F32), _sds(b_in.shape, F32)]
    return pl.pallas_call(
        body, name="gmlp_mid_bwd", grid=(t // w,),
        in_specs=[pl.BlockSpec((N_DEV, w, n), lambda i: (0, i, 0)),
                  pl.BlockSpec((half, w, n), lambda i: (0, i, 0)),
                  whole(b_in), whole(ln_g), whole(ln_b), whole(w_s), whole(b_s)],
        out_specs=[pl.BlockSpec((N_DEV, w, n), lambda i: (0, i, 0))] + [whole(o) for o in outs[1:]],
        out_shape=outs,
        scratch_shapes=[pltpu.VMEM((half, w, n), F32), pltpu.VMEM((half, w, n), F32)],
        compiler_params=_params(("arbitrary",)),
    )(zpre, dgated, b_in, ln_g, ln_b, w_s, b_s)


ATTN_CHUNKS = 4
ATTN_ROWS = ATTN_CHUNKS * CHUNK
ATTN_WINDOW = ATTN_ROWS + LEFT_PAD
ATTN_DIAGS = -(-(ATTN_ROWS + ATTN_WINDOW - 1) // LANES) * LANES
ATTN_ROLL = ATTN_DIAGS - (ATTN_ROWS - 1)


def _rel_vector(rel):
    j = np.arange(ATTN_DIAGS)
    idx = np.clip(ATTN_WINDOW - 1 - j, -(CHUNK - 1), MAX_REL) + (CHUNK - 1)
    return rel[:, idx]


def _attn_bias_mask(rel_ref, bm_ref):
    tt = lax.broadcasted_iota(jnp.int32, (ATTN_ROWS, ATTN_WINDOW), 0) // CHUNK
    rr = lax.broadcasted_iota(jnp.int32, (ATTN_ROWS, ATTN_WINDOW), 1) // CHUNK
    band = (rr >= tt) & (rr <= tt + LEFT_CHUNKS)
    for j in range(bm_ref.shape[0]):
        vec = jnp.broadcast_to(rel_ref[j:j + 1, :], (ATTN_ROWS, ATTN_DIAGS))
        toeplitz = pltpu.roll(vec, ATTN_ROLL, 1, stride=1, stride_axis=0)[:, :ATTN_WINDOW]
        bm_ref[j] = jnp.where(band, toeplitz, -jnp.inf)


def _attn_probs(q_ref, k_ref, bm_ref, j, hd, start, valid):
    qh = q_ref[:, j * hd:(j + 1) * hd]
    kb = k_ref[pl.ds(start, ATTN_WINDOW), j * hd:(j + 1) * hd]
    sc = lax.dot_general(qh, kb, (((1,), (1,)), ((), ())), preferred_element_type=F32)
    sc = sc * (hd ** -0.5) + bm_ref[j]
    sc = jnp.where(valid, sc, -jnp.inf)
    sc = sc - jnp.max(sc, axis=-1, keepdims=True)
    e = jnp.exp(sc)
    return e / jnp.sum(e, axis=-1, keepdims=True), qh, kb


def _window_valid(start):
    r = lax.broadcasted_iota(jnp.int32, (1, ATTN_WINDOW), 1)
    return (start + r) >= LEFT_PAD


def _attn_fwd(q, kvp, rel_vec):
    t, d = q.shape
    hd = d // N_HEADS
    half = N_DEV // 2
    n = kvp.shape[-1]
    per = n // hd
    rows = kvp.shape[1]

    def body(q_ref, k_ref, v_ref, rel_ref, o_ref, bm_ref):
        @pl.when(pl.program_id(1) == 0)
        def _():
            _attn_bias_mask(rel_ref, bm_ref)

        start = pl.multiple_of(pl.program_id(1) * ATTN_ROWS, ATTN_ROWS)
        valid = _window_valid(start)
        for j in range(per):
            p, _, _ = _attn_probs(q_ref, k_ref, bm_ref, j, hd, start, valid)
            vb = v_ref[pl.ds(start, ATTN_WINDOW), j * hd:(j + 1) * hd]
            o_ref[:, j * hd:(j + 1) * hd] = jnp.dot(p.astype(BF16), vb, preferred_element_type=F32).astype(BF16)

    return pl.pallas_call(
        body, name="attn_fwd", grid=(half, t // ATTN_ROWS),
        in_specs=[pl.BlockSpec((ATTN_ROWS, n), lambda g, i: (i, g)),
                  pl.BlockSpec((None, rows, n), lambda g, i: (g, 0, 0)),
                  pl.BlockSpec((None, rows, n), lambda g, i: (half + g, 0, 0)),
                  pl.BlockSpec((None, per, ATTN_DIAGS), lambda g, i: (g, 0, 0))],
        out_specs=pl.BlockSpec((ATTN_ROWS, n), lambda g, i: (i, g)),
        out_shape=_sds((t, d), BF16),
        scratch_shapes=[pltpu.VMEM((per, ATTN_ROWS, ATTN_WINDOW), F32)],
        compiler_params=_params(("arbitrary", "arbitrary")),
    )(q, kvp, kvp, rel_vec.reshape(half, per, ATTN_DIAGS))


def _attn_bwd(q, dout, kvp, rel_vec, dk_in=None, dv_in=None):
    t, d = q.shape
    hd = d // N_HEADS
    half = N_DEV // 2
    n = kvp.shape[-1]
    per = n // hd
    rows = kvp.shape[1]
    scale = hd ** -0.5
    carry = dk_in is not None

    def body(q_ref, do_ref, k_ref, v_ref, rel_ref, *rest):
        dq_ref, dk_ref, dv_ref, dsc_ref, bm_ref = rest[-5:]

        @pl.when(pl.program_id(1) == 0)
        def _():
            _attn_bias_mask(rel_ref, bm_ref)
            dk_ref[...] = rest[0][...] if carry else jnp.zeros_like(dk_ref)
            dv_ref[...] = rest[1][...] if carry else jnp.zeros_like(dv_ref)
            dsc_ref[...] = jnp.zeros_like(dsc_ref)

        start = pl.multiple_of(pl.program_id(1) * ATTN_ROWS, ATTN_ROWS)
        valid = _window_valid(start)
        for j in range(per):
            cols = slice(j * hd, (j + 1) * hd)
            p, qh, kb = _attn_probs(q_ref, k_ref, bm_ref, j, hd, start, valid)
            vb = v_ref[pl.ds(start, ATTN_WINDOW), cols]
            doh = do_ref[:, cols]
            dp = lax.dot_general(doh, vb, (((1,), (1,)), ((), ())), preferred_element_type=F32)
            ds = p * (dp - jnp.sum(dp * p, axis=-1, keepdims=True))
            dsc_ref[j] += sum(ds[a * CHUNK:(a + 1) * CHUNK, a * CHUNK:a * CHUNK + BAND]
                              for a in range(ATTN_CHUNKS))
            dsb = (ds * scale).astype(BF16)
            dq_ref[:, cols] = jnp.dot(dsb, kb, preferred_element_type=F32).astype(BF16)
            dk_ref[pl.ds(start, ATTN_WINDOW), cols] += lax.dot_general(
                dsb, qh, (((0,), (0,)), ((), ())), preferred_element_type=F32)
            dv_ref[pl.ds(start, ATTN_WINDOW), cols] += lax.dot_general(
                p.astype(BF16), doh, (((0,), (0,)), ((), ())), preferred_element_type=F32)

    tile = pl.BlockSpec((ATTN_ROWS, n), lambda g, i: (i, g))
    shard = pl.BlockSpec((None, rows, n), lambda g, i: (g, 0, 0))
    in_specs = [tile, tile, shard, pl.BlockSpec((None, rows, n), lambda g, i: (half + g, 0, 0)),
                pl.BlockSpec((None, per, ATTN_DIAGS), lambda g, i: (g, 0, 0))]
    operands = [q, dout, kvp, kvp, rel_vec.reshape(half, per, ATTN_DIAGS)]
    if carry:
        in_specs += [shard, shard]
        operands += [dk_in, dv_in]
    acc = _sds((half, rows, n), F32)
    return pl.pallas_call(
        body, name="attn_bwd", grid=(half, t // ATTN_ROWS),
        in_specs=in_specs,
        out_specs=[tile, shard, shard, pl.BlockSpec((per, CHUNK, BAND), lambda g, i: (g, 0, 0))],
        out_shape=[_sds((t, d), BF16), acc, acc, _sds((N_HEADS, CHUNK, BAND), F32)],
        scratch_shapes=[pltpu.VMEM((per, ATTN_ROWS, ATTN_WINDOW), F32)],
        compiler_params=_params(("arbitrary", "arbitrary")),
    )(*operands)


SKEW_PITCH = 640
SKEW = SKEW_PITCH + 1
SKEW_LANES = -(-SKEW // LANES) * LANES


def _skew_diagonals(dsc):
    h = dsc.shape[0]
    wide = jnp.pad(dsc, ((0, 0), (0, 0), (0, SKEW_PITCH - BAND))).reshape(h, CHUNK * SKEW_PITCH)
    wide = jnp.pad(wide, ((0, 0), (0, CHUNK))).reshape(h, CHUNK, SKEW)
    return jnp.pad(wide, ((0, 0), (0, 0), (0, SKEW_LANES - SKEW)))


def _rel_bias_grad(skewed):
    heads = skewed.shape[0]
    hb = SUBLANES

    def body(d_ref, o_ref):
        col = lax.broadcasted_iota(jnp.int32, (SKEW_LANES, N_REL), 0)
        bucket = lax.broadcasted_iota(jnp.int32, (SKEW_LANES, N_REL), 1)
        diag = jnp.where(col < BAND, col, col - SKEW)
        idx = jnp.clip(LEFT_PAD - diag, -(CHUNK - 1), MAX_REL) + (CHUNK - 1)
        oh = ((idx == bucket) & (col < SKEW)).astype(BF16)
        dv = jnp.sum(d_ref[...], axis=1)
        hi = dv.astype(BF16)
        rest = dv - hi.astype(F32)
        mid = rest.astype(BF16)
        lo = (rest - mid.astype(F32)).astype(BF16)
        acc = jnp.dot(hi, oh, preferred_element_type=F32)
        acc += jnp.dot(mid, oh, preferred_element_type=F32)
        acc += jnp.dot(lo, oh, preferred_element_type=F32)
        o_ref[...] = acc

    return pl.pallas_call(
        body, name="rel_bias_grad", grid=(heads // hb,),
        in_specs=[pl.BlockSpec((hb, CHUNK, SKEW_LANES), lambda i: (i, 0, 0))],
        out_specs=pl.BlockSpec((hb, N_REL), lambda i: (i, 0)),
        out_shape=_sds((heads, N_REL), F32),
        compiler_params=_params(("parallel",)),
    )(skewed)


def _sum_parts(parts):
    s_n, rows, c = parts.shape
    br = _row_block(rows, OPT_ROW_BLOCK)

    def body(p_ref, o_ref):
        acc = p_ref[0].astype(F32)
        for s in range(1, s_n):
            acc = acc + p_ref[s].astype(F32)
        o_ref[...] = acc

    return pl.pallas_call(
        body, name="sum_parts", grid=(rows // br,),
        in_specs=[pl.BlockSpec((s_n, br, c), lambda i: (0, i, 0))],
        out_specs=pl.BlockSpec((br, c), lambda i: (i, 0)),
        out_shape=_sds((rows, c), F32),
        compiler_params=_params(("parallel",)),
    )(parts)


def _adamw(own, own_idx, parts, w, m, v, row0=0, bufs=None, after=None):
    _, rows, c = own.shape
    s_n = 0 if parts is None else parts.shape[0]
    total = w.shape[0]
    br = _row_block(rows, OPT_ROW_BLOCK)
    assert row0 % br == 0 and (bufs is not None or (row0 == 0 and total == rows))
    b0 = row0 // br
    m_corr = 1.0 - ADAM_B1 ** ADAM_STEP
    v_corr = 1.0 - ADAM_B2 ** ADAM_STEP

    def body(idx_ref, own_ref, *refs):
        if s_n:
            p_ref, refs = refs[0], refs[1:]
        w_ref, m_ref, v_ref = refs[:3]
        g_ref, d_ref, nm_ref, nv_ref = refs[-4:]
        g = own_ref[...].astype(F32)
        for s in range(s_n):
            g = g + p_ref[s].astype(F32)
        nm = ADAM_B1 * m_ref[...] + (1.0 - ADAM_B1) * g
        nv = ADAM_B2 * v_ref[...] + (1.0 - ADAM_B2) * (g * g)
        g_ref[...] = g
        nm_ref[...] = nm
        nv_ref[...] = nv
        d_ref[...] = -ADAM_LR * ((nm / m_corr) / (jnp.sqrt(nv / v_corr) + ADAM_EPS) + ADAM_WD * w_ref[...])

    tile = pl.BlockSpec((br, c), lambda i, idx: (i + b0, 0))
    in_specs = [pl.BlockSpec((None, br, c), lambda i, idx: (idx[0], i, 0))]
    operands = [own_idx, own]
    if s_n:
        in_specs.append(pl.BlockSpec((s_n, br, c), lambda i, idx: (0, i, 0)))
        operands.append(parts)
    in_specs += [tile, tile, tile]
    operands += [w, m, v]
    aliases = {}
    if bufs is not None:
        aliases = {len(operands) + j: j for j in range(4)}
        in_specs += [_ANY] * 4
        operands += list(bufs)
    if after is not None:
        in_specs.append(_ANY)
        operands.append(after)
    out = _sds((total, c), F32)
    return pl.pallas_call(
        body, name="adamw",
        grid_spec=pltpu.PrefetchScalarGridSpec(
            num_scalar_prefetch=1, grid=(rows // br,), in_specs=in_specs,
            out_specs=[tile, tile, tile, tile]),
        out_shape=[out, out, out, out],
        input_output_aliases=aliases,
        compiler_params=_params(("parallel",)),
    )(*operands)


def _chip_sum(p, r1, core):
    half = N_DEV // 2
    c = p.shape[-1]
    rows = int(np.prod(p.shape[1:-1]))
    br = _row_block(rows, BIG_ROW_BLOCK)

    def body(core_ref, p_ref, r_ref, o_ref):
        o_ref[...] = (p_ref[...].astype(F32) + r_ref[...].astype(F32)).astype(BF16)

    out = pl.pallas_call(
        body, name="chip_sum",
        grid_spec=pltpu.PrefetchScalarGridSpec(
            num_scalar_prefetch=1, grid=(half, rows // br),
            in_specs=[pl.BlockSpec((None, None, br, c), lambda q, i, cr: (q, cr[0], i, 0)),
                      pl.BlockSpec((None, br, c), lambda q, i, cr: (q, i, 0))],
            out_specs=pl.BlockSpec((None, br, c), lambda q, i, cr: (q, i, 0))),
        out_shape=_sds((half, rows, c), BF16),
        compiler_params=_params(("parallel", "parallel")),
    )(core, p.reshape(half, 2, rows, c), r1.reshape(half, rows, c))
    return out.reshape((half,) + p.shape[1:])


def _position():
    return tuple(lax.axis_index(a) for a in MESH_AXES)


def _linear(px, py, pc):
    return 4 * px + 2 * py + pc


def _all_gather_small(v, after=()):
    rows, lanes = v.shape

    def body(x_ref, *rest):
        out_ref, send_sems, recv_sems, local_sem = rest[-4:]
        x, y, c = _position()
        me, sibling = (x, y, c), (x, y, 1 - c)
        chips = [(1 - x, y), (x, 1 - y), (1 - x, 1 - y)]

        def copy(k, block, to, src=None):
            dst = out_ref.at[_linear(*block)]
            return pltpu.make_async_remote_copy(
                src_ref=dst if src is None else src, dst_ref=dst,
                send_sem=send_sems.at[k], recv_sem=recv_sems.at[k],
                device_id=to, device_id_type=MESH_ID)

        mine = pltpu.make_async_copy(x_ref, out_ref.at[_linear(*me)], local_sem)
        mine.start()
        first = [copy(0, me, sibling, src=x_ref)]
        first += [copy(1 + j, me, (*chip, c), src=x_ref) for j, chip in enumerate(chips)]
        for cp in first:
            cp.start()
        passed = [copy(4 + j, (*chip, c), sibling) for j, chip in enumerate(chips)]
        for j, chip in enumerate(chips):
            copy(1 + j, (*chip, c), me).wait_recv()
            passed[j].start()
        copy(0, sibling, me).wait_recv()
        for j, chip in enumerate(chips):
            copy(4 + j, (*chip, 1 - c), me).wait_recv()
        for cp in first + passed:
            cp.wait_send()
        mine.wait()

    return pl.pallas_call(
        body, name="all_gather_small",
        out_shape=_sds((N_DEV, rows, lanes), v.dtype),
        in_specs=[_VMEM] + [_ANY] * len(after), out_specs=_VMEM,
        scratch_shapes=[pltpu.SemaphoreType.DMA((7,)), pltpu.SemaphoreType.DMA((7,)),
                        pltpu.SemaphoreType.DMA],
        compiler_params=pltpu.CompilerParams(vmem_limit_bytes=VMEM_LIMIT),
    )(v, *after)


_HBM = pl.BlockSpec(memory_space=pltpu.HBM)
_SEM = pl.BlockSpec(memory_space=pltpu.SEMAPHORE)
_EFFECT = pltpu.SideEffectType.DATAFLOW_SIDE_EFFECTING
_ALL_CHIPS = [(0, 0), (0, 1), (1, 0), (1, 1)]


def _other_chips(x, y):
    return [(1 - x, y), (x, 1 - y), (1 - x, 1 - y)]


def _in_hbm(a):
    return pltpu.with_memory_space_constraint(a, pltpu.HBM)


def _token():
    return _sds((SUBLANES, LANES), F32)


def _gather_copies(refs, send_sems, recv_sems, to_sibling):
    x, y, c = _position()
    if to_sibling:
        plan = [((x, y, 1 - c), _linear(*q, c), _linear(*q, 1 - c)) for q in _ALL_CHIPS]
    else:
        plan = [((*ch, c), _linear(x, y, c), _linear(*ch, c)) for ch in _other_chips(x, y)]

    def copy(ref, i, k, peer, block):
        return pltpu.make_async_remote_copy(
            src_ref=ref.at[block], dst_ref=ref.at[block],
            send_sem=send_sems.at[len(plan) * i + k], recv_sem=recv_sems.at[len(plan) * i + k],
            device_id=peer, device_id_type=MESH_ID)

    return [(copy(ref, i, k, peer, sent), copy(ref, i, k, peer, landed))
            for i, ref in enumerate(refs) for k, (peer, sent, landed) in enumerate(plan)]


def _gather_ici_start(name, lands, after=None, to_sibling=False):
    n = len(lands)
    extra = [] if after is None else [after]
    n_sem = n * (len(_ALL_CHIPS) if to_sibling else len(_ALL_CHIPS) - 1)

    def body(*refs):
        send_sems, recv_sems, token = refs[-n - 3], refs[-n - 2], refs[-1]
        for sent, _ in _gather_copies(refs[:n], send_sems, recv_sems, to_sibling):
            sent.start()
        token[...] = jnp.zeros_like(token)

    out = pl.pallas_call(
        body, name=name,
        out_shape=(pltpu.SemaphoreType.DMA((n_sem,)), pltpu.SemaphoreType.DMA((n_sem,)),
                   *[pltpu.HBM(a.shape, a.dtype) for a in lands], _token()),
        in_specs=[_HBM] * n + [_ANY] * len(extra), out_specs=(_SEM, _SEM, *[_HBM] * n, _VMEM),
        input_output_aliases={i: 2 + i for i in range(n)},
        compiler_params=pltpu.CompilerParams(has_side_effects=_EFFECT),
    )(*[_in_hbm(a) for a in lands], *extra)
    return out[0], out[1], list(out[2:2 + n]), out[-1]


def _gather_ici_wait(name, lands, send_sems, recv_sems, after, to_sibling=False):
    n = len(lands)

    def body(*refs):
        for sent, landed in _gather_copies(refs[:n], refs[n], refs[n + 1], to_sibling):
            sent.wait_send()
            landed.wait_recv()

    out = pl.pallas_call(
        body, name=name,
        out_shape=[pltpu.HBM(a.shape, a.dtype) for a in lands],
        in_specs=[_HBM] * n + [_SEM, _SEM, _ANY], out_specs=[_HBM] * n,
        input_output_aliases={i: i for i in range(n)},
        compiler_params=pltpu.CompilerParams(has_side_effects=_EFFECT),
    )(*lands, send_sems, recv_sems, after)
    return list(out)


def _gather_d2d(lands):
    n = len(lands)

    def body(*refs):
        ins, outs, send_sems, recv_sems = refs[:n], refs[n:2 * n], refs[2 * n], refs[2 * n + 1]
        x, y, c = _position()

        def copy(i, q, core):
            block = _linear(*_ALL_CHIPS[q], core)
            return pltpu.make_async_remote_copy(
                src_ref=ins[i].at[block], dst_ref=outs[i].at[block],
                send_sem=send_sems.at[i, q], recv_sem=recv_sems.at[i, q],
                device_id=(x, y, 1 - c), device_id_type=MESH_ID)

        sent = [copy(i, q, c) for i in range(n) for q in range(len(_ALL_CHIPS))]
        for cp in sent:
            cp.start()
        for i in range(n):
            for q in range(len(_ALL_CHIPS)):
                copy(i, q, 1 - c).wait_recv()
        for cp in sent:
            cp.wait_send()

    return pl.pallas_call(
        body, name="gather_d2d",
        out_shape=[_sds(a.shape, a.dtype) for a in lands],
        in_specs=[_ANY] * n, out_specs=[_ANY] * n,
        input_output_aliases={i: i for i in range(n)},
        scratch_shapes=[pltpu.SemaphoreType.DMA((n, 4)), pltpu.SemaphoreType.DMA((n, 4))],
    )(*lands)


def _partials_d2d(parts):
    n = len(parts)
    half = N_DEV // 2

    def body(*refs):
        ins, outs, send_sems, recv_sems = refs[:n], refs[n:2 * n], refs[2 * n], refs[2 * n + 1]
        x, y, c = _position()

        def copy(i, q):
            return pltpu.make_async_remote_copy(
                src_ref=ins[i].at[_linear(*_ALL_CHIPS[q], 1 - c)], dst_ref=outs[i].at[q],
                send_sem=send_sems.at[i, q], recv_sem=recv_sems.at[i, q],
                device_id=(x, y, 1 - c), device_id_type=MESH_ID)

        sent = [copy(i, q) for i in range(n) for q in range(half)]
        for cp in sent:
            cp.start()
        for cp in sent:
            cp.wait_recv()
        for cp in sent:
            cp.wait_send()

    return pl.pallas_call(
        body, name="partials_d2d",
        out_shape=[_sds((half,) + p.shape[1:], p.dtype) for p in parts],
        in_specs=[_ANY] * n, out_specs=[_ANY] * n,
        scratch_shapes=[pltpu.SemaphoreType.DMA((n, half)), pltpu.SemaphoreType.DMA((n, half))],
    )(*parts)


def _partials_peers(x, y, c, direct):
    chips = _other_chips(x, y)
    if not direct:
        return [((*ch, c), 2 * ch[0] + ch[1]) for ch in chips]
    peers = [(x, y, 1 - c)] + [(*ch, c) for ch in chips] + [(*ch, 1 - c) for ch in chips]
    return [(p, _linear(*p)) for p in peers]


def _partials_copies(srcs, lands, send_sems, recv_sems, direct):
    x, y, c = _position()
    peers = _partials_peers(x, y, c, direct)
    return [pltpu.make_async_remote_copy(
        src_ref=srcs[i].at[block], dst_ref=lands[i].at[k],
        send_sem=send_sems.at[len(peers) * i + k], recv_sem=recv_sems.at[len(peers) * i + k],
        device_id=peer, device_id_type=MESH_ID)
        for i in range(len(srcs)) for k, (peer, block) in enumerate(peers)]


def _partials_send_start(name, srcs, lands, direct, after=None):
    n = len(srcs)
    n_sem = n * (N_DEV - 1 if direct else len(_ALL_CHIPS) - 1)

    def body(*refs):
        _, send_sems, recv_sems = refs[:2 * n], refs[-2 * n - 3], refs[-2 * n - 2]
        for cp in _partials_copies(refs[:n], refs[n:2 * n], send_sems, recv_sems, direct):
            cp.start()
        refs[-1][...] = jnp.zeros_like(refs[-1])

    both = list(srcs) + list(lands)
    extra = [] if after is None else [after]
    out = pl.pallas_call(
        body, name=name,
        out_shape=(pltpu.SemaphoreType.DMA((n_sem,)), pltpu.SemaphoreType.DMA((n_sem,)),
                   *[pltpu.HBM(a.shape, a.dtype) for a in both], _token()),
        in_specs=[_HBM] * (2 * n) + [_ANY] * len(extra), out_specs=(_SEM, _SEM, *[_HBM] * (2 * n), _VMEM),
        input_output_aliases={i: 2 + i for i in range(2 * n)},
        compiler_params=pltpu.CompilerParams(has_side_effects=_EFFECT),
    )(*[_in_hbm(a) for a in both], *extra)
    return out[0], out[1], list(out[2:2 + n]), list(out[2 + n:2 + 2 * n]), out[-1]


def _partials_send_wait(name, srcs, lands, send_sems, recv_sems, direct, after):
    n = len(srcs)

    def body(*refs):
        for cp in _partials_copies(refs[:n], refs[n:2 * n], refs[2 * n], refs[2 * n + 1], direct):
            cp.wait_send()
            cp.wait_recv()

    both = list(srcs) + list(lands)
    out = pl.pallas_call(
        body, name=name,
        out_shape=[pltpu.HBM(a.shape, a.dtype) for a in both],
        in_specs=[_HBM] * (2 * n) + [_SEM, _SEM, _ANY], out_specs=[_HBM] * (2 * n),
        input_output_aliases={i: i for i in range(2 * n)},
        compiler_params=pltpu.CompilerParams(has_side_effects=_EFFECT),
    )(*both, send_sems, recv_sems, after)
    return list(out[:n]), list(out[n:])


def _pack(arrs):
    flat = jnp.concatenate([a.reshape(-1).astype(F32) for a in arrs])
    block = OPT_ROW_BLOCK if flat.shape[0] > OPT_ROW_BLOCK * LANES else SUBLANES
    pad = (-flat.shape[0]) % (block * LANES)
    if pad:
        flat = jnp.concatenate([flat, jnp.zeros((pad,), F32)])
    return flat.reshape(-1, LANES)


def _unpack(packed, shapes, lead=()):
    flat = packed.reshape(lead + (-1,))
    out, off = [], 0
    for s in shapes:
        size = int(np.prod(s))
        out.append(flat[..., off:off + size].reshape(lead + tuple(s)))
        off += size
    return out


def _unshard_last(g):
    nd = g.ndim
    perm = tuple(range(1, nd - 1)) + (0, nd - 1)
    t = jnp.transpose(g, perm)
    return t.reshape(t.shape[:-2] + (N_DEV * g.shape[-1],))


def kernel(x, c, w_ada, b_ada, ln_g, ln_b, ffn_gu, ffn_down, gmlp_w_in, gmlp_b_in, gmlp_ln_g, gmlp_ln_b, gmlp_w_s, gmlp_b_s, gmlp_w_out, w_ada_kv, b_ada_kv, w_kv, attn_w_q, attn_rel_bias, attn_w_o, loss_target, m_w_ada, m_b_ada, m_ln_g, m_ln_b, m_ffn_gu, m_ffn_down, m_gmlp_w_in, m_gmlp_b_in, m_gmlp_ln_g, m_gmlp_ln_b, m_gmlp_w_s, m_gmlp_b_s, m_gmlp_w_out, m_w_ada_kv, m_b_ada_kv, m_w_kv, m_attn_w_q, m_attn_rel_bias, m_attn_w_o, v_w_ada, v_b_ada, v_ln_g, v_ln_b, v_ffn_gu, v_ffn_down, v_gmlp_w_in, v_gmlp_b_in, v_gmlp_ln_g, v_gmlp_ln_b, v_gmlp_w_s, v_gmlp_b_s, v_gmlp_w_out, v_w_ada_kv, v_b_ada_kv, v_w_kv, v_attn_w_q, v_attn_rel_bias, v_attn_w_o):
    weights = dict(w_ada=w_ada, b_ada=b_ada, ln_g=ln_g, ln_b=ln_b, ffn_gu=ffn_gu, ffn_down=ffn_down,
                   gmlp_w_in=gmlp_w_in, gmlp_b_in=gmlp_b_in, gmlp_ln_g=gmlp_ln_g, gmlp_ln_b=gmlp_ln_b,
                   gmlp_w_s=gmlp_w_s, gmlp_b_s=gmlp_b_s, gmlp_w_out=gmlp_w_out, w_ada_kv=w_ada_kv,
                   b_ada_kv=b_ada_kv, w_kv=w_kv, attn_w_q=attn_w_q, attn_rel_bias=attn_rel_bias,
                   attn_w_o=attn_w_o)
    mom1 = dict(w_ada=m_w_ada, b_ada=m_b_ada, ln_g=m_ln_g, ln_b=m_ln_b, ffn_gu=m_ffn_gu, ffn_down=m_ffn_down,
                gmlp_w_in=m_gmlp_w_in, gmlp_b_in=m_gmlp_b_in, gmlp_ln_g=m_gmlp_ln_g, gmlp_ln_b=m_gmlp_ln_b,
                gmlp_w_s=m_gmlp_w_s, gmlp_b_s=m_gmlp_b_s, gmlp_w_out=m_gmlp_w_out, w_ada_kv=m_w_ada_kv,
                b_ada_kv=m_b_ada_kv, w_kv=m_w_kv, attn_w_q=m_attn_w_q, attn_rel_bias=m_attn_rel_bias,
                attn_w_o=m_attn_w_o)
    mom2 = dict(w_ada=v_w_ada, b_ada=v_b_ada, ln_g=v_ln_g, ln_b=v_ln_b, ffn_gu=v_ffn_gu, ffn_down=v_ffn_down,
                gmlp_w_in=v_gmlp_w_in, gmlp_b_in=v_gmlp_b_in, gmlp_ln_g=v_gmlp_ln_g, gmlp_ln_b=v_gmlp_ln_b,
                gmlp_w_s=v_gmlp_w_s, gmlp_b_s=v_gmlp_b_s, gmlp_w_out=v_gmlp_w_out, w_ada_kv=v_w_ada_kv,
                b_ada_kv=v_b_ada_kv, w_kv=v_w_kv, attn_w_q=v_attn_w_q, attn_rel_bias=v_attn_rel_bias,
                attn_w_o=v_attn_w_o)
    order = list(weights)

    x = x[0]
    target = loss_target[0]
    t, d = x.shape
    n_mod = w_ada.shape[-1] * N_DEV // d
    mod_w = w_ada.shape[-1]
    kv_w = w_ada_kv.shape[-1]
    n_b = DEPTH - N_A
    me = _linear(*_position())

    l2 = DEPTH * 2
    big = dict(
        ffn_gu=ffn_gu.reshape((l2,) + ffn_gu.shape[2:]),
        ffn_down=ffn_down.reshape((l2,) + ffn_down.shape[2:]),
        gmlp_w_in=gmlp_w_in, gmlp_w_out=gmlp_w_out, w_kv=w_kv[None],
        attn_w_q=attn_w_q, attn_w_o=attn_w_o)
    big_names = list(big)
    core = lax.axis_index("c").astype(jnp.int32).reshape(1)
    chip = (2 * lax.axis_index("x") + lax.axis_index("y")).astype(jnp.int32).reshape(1)

    fwd_groups = [
        {"ffn_gu": (0, 1), "ffn_down": (0, 1)},
        {"gmlp_w_in": (0, 1), "gmlp_w_out": (0, 1)},
        {"ffn_gu": (1, 1), "ffn_down": (1, 1)},
        {"ffn_gu": (2, 1), "ffn_down": (2, 1)},
        {"gmlp_w_in": (1, 1), "gmlp_w_out": (1, 1)},
        {"ffn_gu": (3, 1), "ffn_down": (3, 1), "w_kv": (0, 1)},
        {"ffn_gu": (4, 1), "ffn_down": (4, 1)},
        {"attn_w_q": (0, 1), "attn_w_o": (0, 1)},
        {"ffn_gu": (5, 1), "ffn_down": (5, 1)},
        {"ffn_gu": (6, 2), "ffn_down": (6, 2), "attn_w_q": (1, 1), "attn_w_o": (1, 1)},
    ]
    bwd_groups = []
    for l in range(DEPTH):
        g = {"ffn_gu": (2 * l, 2), "ffn_down": (2 * l, 2)}
        if l < N_A:
            g.update({"gmlp_w_in": (l, 1), "gmlp_w_out": (l, 1)})
        else:
            g.update({"attn_w_q": (l - N_A, 1), "attn_w_o": (l - N_A, 1)})
        if l == N_A - 1:
            g["w_kv"] = (0, 1)
        bwd_groups.append(g)

    def slot_of(groups, name, slot):
        for gi, g in enumerate(groups):
            if name in g and g[name][0] <= slot < g[name][0] + g[name][1]:
                return gi, slot - g[name][0]
        raise KeyError((name, slot))

    def start_group(gi, after=None):
        lands = []
        for name, (s0, cnt) in fwd_groups[gi].items():
            shard = big[name][s0:s0 + cnt].astype(BF16)
            land = lax.empty((N_DEV,) + shard.shape, BF16)
            lands.append(lax.dynamic_update_slice(land, shard[None], (me,) + (0,) * shard.ndim))
        return _gather_ici_start(f"gather_ici_start_{gi}", lands, after)

    gathered = [None] * len(fwd_groups)

    passing = {}

    def pass_on_early(gi, after):
        send_sems, recv_sems, lands, _ = flights[gi]
        lands = _gather_ici_wait(f"gather_ici_wait_{gi}", lands, send_sems, recv_sems, after)
        passing[gi] = _gather_ici_start(f"gather_d2d_start_{gi}", lands, to_sibling=True)
        return passing[gi][3]

    def land_group(gi, after):
        if gi in passing:
            send_sems, recv_sems, lands, _ = passing.pop(gi)
            lands = _gather_ici_wait(f"gather_d2d_wait_{gi}", lands, send_sems, recv_sems, after, to_sibling=True)
        else:
            send_sems, recv_sems, lands, _ = flights[gi]
            lands = _gather_d2d(_gather_ici_wait(f"gather_ici_wait_{gi}", lands, send_sems, recv_sems, after))
        gathered[gi] = dict(zip(fwd_groups[gi], lands))

    def weight(name, slot):
        gi, local = slot_of(fwd_groups, name, slot)
        return gathered[gi][name], local

    swapped = ("ffn_gu",)

    def grad_shape(name):
        s = big[name].shape[1:]
        return s[:-2] + (s[-1], s[-2]) if name in swapped else s

    partial = [{name: lax.empty((N_DEV, cnt) + grad_shape(name), BF16) for name, (_, cnt) in g.items()}
               for g in bwd_groups]

    c_all = _all_gather_small(_pack([c]))
    c_all = _unpack(c_all, [(d,)], lead=(N_DEV,))[0]
    c4 = _as4(c_all)
    mod_part = _matmul("ada_fwd", c4, w_ada[:, None], (DEPTH, 1, N_DEV, mod_w), F32, a_silu=True)
    kv_part = _matmul("ada_kv_fwd", c4, _as4(w_ada_kv), (1, 1, N_DEV, kv_w), F32, a_silu=True)
    small_shapes = [mod_part.shape, kv_part.shape, ln_g.shape, ln_b.shape, gmlp_b_in.shape,
                    gmlp_ln_g.shape, gmlp_ln_b.shape, attn_rel_bias.shape]
    small = _all_gather_small(_pack([mod_part, kv_part, ln_g, ln_b, gmlp_b_in, gmlp_ln_g, gmlp_ln_b,
                                     attn_rel_bias]))
    flights = [start_group(0, after=small)]
    flights += [start_group(gi, after=flights[0][3]) for gi in range(1, len(fwd_groups))]
    start_token = sum(f[3][0, 0] for f in flights)
    (mod_g, kvm_g, ln_g_g, ln_b_g, b_in_g, gln_g_g, gln_b_g, rel_g) = _unpack(small, small_shapes, lead=(N_DEV,))
    mod_mine = lax.dynamic_index_in_dim(mod_g[:, :, 0], me, axis=2, keepdims=False)
    mod = _unshard_last(mod_mine) + b_ada
    mod = mod.reshape(DEPTH, n_mod, 1, d)
    kvm_mine = lax.dynamic_index_in_dim(kvm_g[:, 0, 0], me, axis=1, keepdims=False)
    mkv = (_unshard_last(kvm_mine) + b_ada_kv).reshape(2, 1, d)
    ln_g_f = _unshard_last(ln_g_g)
    ln_b_f = _unshard_last(ln_b_g)
    half = N_DEV // 2
    b_in_f = jnp.transpose(b_in_g, (1, 0, 2))[:, :, None, :]
    gln_g_f = _unshard_last(gln_g_g).reshape(N_A, half, 1, -1)
    gln_b_f = _unshard_last(gln_b_g).reshape(N_A, half, 1, -1)
    rel_f = _unshard_last(rel_g)

    def shard_act(a):
        return a.reshape(a.shape[0], a.shape[2], a.shape[3])

    def grad_into(name, slot, mm):
        gi, local = slot_of(bwd_groups, name, slot)
        partial[gi][name] = mm(partial[gi][name], local)

    def ffn_fwd(h, lw, after=None):
        w_gu, l_gu = weight("ffn_gu", lw)
        w_dn, l_dn = weight("ffn_down", lw)
        gu, a = _ffn_up_fwd(h, w_gu, l_gu, after)
        y = _matmul("ffn_down_fwd", a[:, None], w_dn, (1, 1, t, d), F32, lb=l_dn, b_merge=2, reduce=True)
        return y[0, 0], (gu, a)

    def ffn_bwd(dy, h, saved, lw):
        gu, a = saved
        w_gu, l_gu = weight("ffn_gu", lw)
        w_dn, l_dn = weight("ffn_down", lw)
        dgu = _ffn_down_bwd_a(dy, w_dn, l_dn, gu).reshape((N_DEV,) + gu.shape[2:])
        grad_into("ffn_down", lw, lambda buf, lo: _matmul(
            "ffn_down_bwd_w", a[:, None], _as4(dy), buf.shape, BF16, ta=True, lo=lo, out_merge=2, out_buf=buf))
        dh = _matmul("ffn_gu_bwd_a", dgu[:, None], w_gu, (1, 1, t, d), F32, lb=l_gu, tb=True, reduce=True)
        grad_into("ffn_gu", lw, lambda buf, lo: _matmul(
            "ffn_gu_bwd_w", dgu[:, None], _as4(h), buf.shape, BF16, ta=True, lo=lo, out_buf=buf))
        return dh[0, 0], {}

    def gmlp_params(l):
        return (b_in_f[l], gln_g_f[l], gln_b_f[l], gmlp_w_s[l], gmlp_b_s[l][:, :, None])

    def gmlp_fwd(h, l, after=None):
        w_in, l_in = weight("gmlp_w_in", l)
        w_out, l_out = weight("gmlp_w_out", l)
        n = w_in.shape[-1]
        zpre = _matmul("gmlp_in_fwd", _as4(h), w_in, (N_DEV, 1, t, n), F32, lb=l_in, after=after)
        gated = _gmlp_mid_fwd(shard_act(zpre), *gmlp_params(l))
        y = _matmul("gmlp_out_fwd", gated[:, None], w_out, (1, 1, t, d), F32, lb=l_out, b_merge=2, reduce=True)
        return y[0, 0], (zpre, gated)

    def gmlp_bwd(dy, h, saved, l):
        zpre, gated = saved
        w_in, l_in = weight("gmlp_w_in", l)
        w_out, l_out = weight("gmlp_w_out", l)
        n = w_in.shape[-1]
        dgated = _matmul("gmlp_out_bwd_a", _as4(dy), w_out, (half, 1, t, n), F32, lb=l_out, b_merge=2, tb=True)
        grad_into("gmlp_w_out", l, lambda buf, lo: _matmul(
            "gmlp_out_bwd_w", gated[:, None], _as4(dy), buf.shape, BF16, ta=True, lo=lo, out_merge=2, out_buf=buf))
        dz, dws, dbs, dlng, dlnb, dbin = _gmlp_mid_bwd(shard_act(zpre), shard_act(dgated), *gmlp_params(l))
        dh = _matmul("gmlp_in_bwd_a", dz[:, None], w_in, (1, 1, t, d), F32, lb=l_in, tb=True, reduce=True)
        grad_into("gmlp_w_in", l, lambda buf, lo: _matmul(
            "gmlp_in_bwd_w", _as4(h), dz[:, None], buf.shape, BF16, ta=True, lo=lo, out_buf=buf))
        small_grads = dict(gmlp_w_s=dws, gmlp_b_s=dbs[:, :, 0], gmlp_ln_g=dlng.reshape(-1),
                           gmlp_ln_b=dlnb.reshape(-1), gmlp_b_in=dbin.reshape(-1))
        return dh[0, 0], small_grads

    def attn_fwd(h, j, kvp, after=None):
        rel_vec = _rel_vector(rel_f[j])
        w_q, l_q = weight("attn_w_q", j)
        w_o, l_o = weight("attn_w_o", j)
        q = _matmul("attn_q_fwd", _as4(h), w_q, (1, 1, t, d), BF16, lb=l_q, b_merge=N_DEV, reduce=True,
                    after=after)[0, 0]
        o = _attn_fwd(q, kvp, rel_vec)
        y = _matmul("attn_o_fwd", _as4(o), w_o, (1, 1, t, d), F32, lb=l_o, b_merge=N_DEV, reduce=True)
        return y[0, 0], (q, o, rel_vec)

    def attn_bwd(dy, h, saved, j, kvp, dkv_acc):
        q, o, rel_vec = saved
        w_q, l_q = weight("attn_w_q", j)
        w_o, l_o = weight("attn_w_o", j)
        do = _matmul("attn_o_bwd_a", _as4(dy), w_o, (1, 1, t, d), BF16, lb=l_o, b_merge=N_DEV, tb=True)[0, 0]
        grad_into("attn_w_o", j, lambda buf, lo: _matmul(
            "attn_o_bwd_w", _as4(o), _as4(dy), buf.shape, BF16, ta=True, lo=lo, out_merge=N_DEV, out_buf=buf))
        dq, dk, dv, dsc = _attn_bwd(q, do, kvp, rel_vec, *dkv_acc)
        drel = _rel_bias_grad(_skew_diagonals(dsc))
        dh = _matmul("attn_q_bwd_a", _as4(dq), w_q, (1, 1, t, d), F32, lb=l_q, b_merge=N_DEV, tb=True)
        grad_into("attn_w_q", j, lambda buf, lo: _matmul(
            "attn_q_bwd_w", _as4(h), _as4(dq), buf.shape, BF16, ta=True, lo=lo, out_merge=N_DEV, out_buf=buf))
        return dh[0, 0], dict(attn_rel_bias=drel, dkv=(dk, dv))

    tape = []
    kvp = None
    kv_tape = None
    first_use = {(0, 0): 0, (0, 1): 1, (0, 2): 2, (1, 0): 3, (1, 1): 4, (1, 2): 5, (2, 0): 6, (2, 1): 7,
                 (2, 2): 8, (3, 0): 9}
    PASS_ON_EARLY_FROM = 5
    h = _modulate(x, mod[0, 1], mod[0, 0] + start_token)
    for l in range(DEPTH):
        for i in range(3):
            if (l, i) in first_use:
                land_group(first_use[l, i], x)
            nl, ni = (l, i + 1) if i < 2 else (l + 1, 0)
            ahead = first_use.get((nl, ni), 0)
            started = pass_on_early(ahead, x) if ahead >= PASS_ON_EARLY_FROM else None
            scl, gate = mod[l, 3 * i + 1], mod[l, 3 * i + 2]
            wgt = 1.0 if i == 1 else 0.5
            gw = wgt * (1.0 + gate)
            if i != 1:
                y, saved = ffn_fwd(h, 2 * l + i // 2, started)
            elif l < N_A:
                y, saved = gmlp_fwd(h, l, started)
            else:
                y, saved = attn_fwd(h, l - N_A, kvp, started)
            readers = [(mod[nl, 3 * ni + 1], mod[nl, 3 * ni])] if nl < DEPTH else []
            shared_kv = (l, i) == (N_A - 1, 2)
            if shared_kv:
                readers.append((mkv[1], mkv[0]))
            outs = _ln_res_fwd(x, y, gw, ln_g_f[l, i][None], ln_b_f[l, i][None], readers)
            tape.append((x, h, y, gw, scl, saved))
            x = outs[0]
            h = outs[1] if nl < DEPTH else None
            if shared_kv:
                hkv = outs[-1]
                w_kvg, l_kv = weight("w_kv", 0)
                n = w_kvg.shape[-1]
                kv = _matmul("kv_fwd", _as4(hkv), w_kvg, (N_DEV, 1, t, n), BF16, lb=l_kv)
                kvp = jnp.pad(shard_act(kv), ((0, 0), (LEFT_PAD, 0), (0, 0)))
                kv_tape = hkv

    loss_part, dx = _loss_head(x, target)
    loss = lax.psum(loss_part[0, 0], MESH_AXES)

    d_mod = [[None] * n_mod for _ in range(DEPTH)]
    d_ln_g = [[None] * 3 for _ in range(DEPTH)]
    d_ln_b = [[None] * 3 for _ in range(DEPTH)]
    small_grads = {k: [None] * N_A for k in ("gmlp_w_s", "gmlp_b_s", "gmlp_ln_g", "gmlp_ln_b", "gmlp_b_in")}
    d_rel = [None] * n_b
    dkv_acc = ()
    d_mkv = None
    reductions = [None] * DEPTH
    sent_token = None
    readers = []
    for l in reversed(range(DEPTH)):
        if l == N_A - 1:
            hkv = kv_tape
            w_kvg, l_kv = weight("w_kv", 0)
            dkv = jnp.concatenate(dkv_acc)[:, LEFT_PAD:, :].astype(BF16)[:, None]
            dhkv = _matmul("kv_bwd_a", dkv, w_kvg, (1, 1, t, d), F32, lb=l_kv, tb=True, reduce=True)[0, 0]
            grad_into("w_kv", 0, lambda buf, lo: _matmul(
                "kv_bwd_w", _as4(hkv), dkv, buf.shape, BF16, ta=True, lo=lo, out_buf=buf))
            readers.append((dhkv, mkv[1], None))
        for i in reversed(range(3)):
            x_in, h, y, gw, scl, saved = tape[3 * l + i]
            wgt = 1.0 if i == 1 else 0.5
            if sent_token is not None:
                gw = gw + sent_token
                sent_token = None
            res = _ln_res_bwd(x_in, y, gw, ln_g_f[l, i][None], ln_b_f[l, i][None], dx,
                              [(r[0], r[1]) for r in readers])
            dx_res, dy, dgw, dg, db = res[:5]
            for k, (_, _, slot) in enumerate(readers):
                dscl_k, dshift_k = res[5 + 2 * k][0], res[6 + 2 * k][0]
                if slot is None:
                    d_mkv = jnp.concatenate([dshift_k, dscl_k])
                else:
                    d_mod[slot[0]][slot[1]], d_mod[slot[0]][slot[1] + 1] = dshift_k, dscl_k
            d_ln_g[l][i], d_ln_b[l][i] = dg[0], db[0]
            if i != 1:
                dh, extra = ffn_bwd(dy, h, saved, 2 * l + i // 2)
            elif l < N_A:
                dh, extra = gmlp_bwd(dy, h, saved, l)
                for k, g in extra.items():
                    small_grads[k][l] = g
            else:
                dh, extra = attn_bwd(dy, h, saved, l - N_A, kvp, dkv_acc)
                d_rel[l - N_A] = extra["attn_rel_bias"]
                dkv_acc = extra["dkv"]
            d_mod[l][3 * i + 2] = wgt * dgw[0]
            dx = dx_res
            readers = [(dh, scl, (l, 3 * i))]
        if l > 0:
            srcs = [partial[l][k] for k in bwd_groups[l]]
            lands = [lax.empty((N_DEV - 1,) + s.shape[1:], BF16) for s in srcs]
            reductions[l] = _partials_send_start(f"partials_send_start_{l}", srcs, lands, True)
            sent_token = reductions[l][4][0, 0]
    (dh, scl, _), = readers
    dx, dscl, dshift = _mod_bwd(dx, dh, tape[0][0], scl)
    d_mod[0][0], d_mod[0][1] = dshift[0], dscl[0]
    grad_x = dx[None]

    d_mod_arr = jnp.stack([jnp.concatenate(r) for r in d_mod])
    small_part = dict(
        b_ada=d_mod_arr, b_ada_kv=d_mkv,
        ln_g=jnp.stack([jnp.stack(r) for r in d_ln_g]), ln_b=jnp.stack([jnp.stack(r) for r in d_ln_b]),
        gmlp_b_in=jnp.stack(small_grads["gmlp_b_in"]), gmlp_ln_g=jnp.stack(small_grads["gmlp_ln_g"]),
        gmlp_ln_b=jnp.stack(small_grads["gmlp_ln_b"]), gmlp_w_s=jnp.stack(small_grads["gmlp_w_s"]),
        gmlp_b_s=jnp.stack(small_grads["gmlp_b_s"]), attn_rel_bias=jnp.stack(d_rel))
    small_names = list(small_part)
    sp_shapes = [small_part[k].shape for k in small_names]
    sp_all = _all_gather_small(_pack([small_part[k] for k in small_names]),
                               after=[partial[0][k] for k in bwd_groups[0]])

    from_sibling = _partials_d2d([partial[0][k] for k in bwd_groups[0]])
    sums = [_chip_sum(partial[0][k], r1, core) for k, r1 in zip(bwd_groups[0], from_sibling)]
    lands = [lax.empty((len(_ALL_CHIPS) - 1,) + s.shape[1:], BF16) for s in sums]
    reductions[0] = _partials_send_start("partials_send_start_0", sums, lands, False, after=sp_all)
    sent_token = reductions[0][4][0, 0]
    c4 = c4 + sent_token

    sp_sum = _sum_parts(sp_all)
    full_grads = dict(zip(small_names, _unpack(sp_sum, sp_shapes)))
    per_dev = dict(zip(small_names, _unpack(sp_all, sp_shapes, lead=(N_DEV,))))

    def my_cols(a, width):
        return lax.dynamic_slice_in_dim(a, me * width, width, axis=a.ndim - 1)

    grads = {}
    grads["b_ada"] = full_grads["b_ada"]
    grads["b_ada_kv"] = full_grads["b_ada_kv"]
    grads["gmlp_w_s"] = full_grads["gmlp_w_s"]
    grads["gmlp_b_s"] = full_grads["gmlp_b_s"]
    for k in ("ln_g", "ln_b", "gmlp_b_in", "gmlp_ln_g", "gmlp_ln_b", "attn_rel_bias"):
        grads[k] = my_cols(full_grads[k], weights[k].shape[-1])

    dmod_cols = jnp.transpose(my_cols(per_dev["b_ada"], mod_w), (1, 0, 2))[:, None]
    grads["w_ada"] = _matmul("ada_bwd_w", c4, dmod_cols, (DEPTH, 1, d, mod_w), F32, ta=True,
                             a_silu=True)[:, 0]
    dkv_cols = my_cols(per_dev["b_ada_kv"], kv_w)[None, None]
    grads["w_ada_kv"] = _matmul("ada_kv_bwd_w", c4, dkv_cols, (1, 1, d, kv_w), F32, ta=True,
                                a_silu=True)[0, 0]

    delta, new_m, new_v = {}, {}, {}
    first = jnp.zeros((1,), jnp.int32)

    def flat2(a, cols):
        return a.reshape(-1, cols)

    done = None
    for k in ("w_ada", "w_ada_kv"):
        w = weights[k]
        cols = w.shape[-1]
        res = _adamw(grads[k].reshape(1, -1, cols), first, None, flat2(w, cols), flat2(mom1[k], cols),
                     flat2(mom2[k], cols), after=done)
        grads[k], delta[k], new_m[k], new_v[k] = (a.reshape(w.shape) for a in res)
        done = res[0][:SUBLANES, :LANES]

    tiny = [k for k in order if k not in delta and k not in big_names]
    tiny_shapes = [weights[k].shape for k in tiny]
    tiny_out = _adamw((_pack([grads[k] for k in tiny]) + sent_token)[None], first, None,
                      _pack([weights[k] for k in tiny]), _pack([mom1[k] for k in tiny]),
                      _pack([mom2[k] for k in tiny]), after=done)
    for dst, arr in zip((grads, delta, new_m, new_v), tiny_out):
        for k, val in zip(tiny, _unpack(arr, tiny_shapes)):
            dst[k] = val

    def opt_view(k, a):
        a = jnp.swapaxes(a, -1, -2) if k in swapped else a
        return a.reshape(-1, a.shape[-1])

    def opt_unview(k, a):
        s = weights[k].shape
        return jnp.swapaxes(a.reshape(s[:-2] + (s[-1], s[-2])), -1, -2) if k in swapped else a.reshape(s)

    bufs = {k: [lax.empty(opt_view(k, weights[k]).shape, F32) for _ in range(4)] for k in big_names}
    done = tiny_out[0]
    me_idx = me.astype(jnp.int32).reshape(1)
    for l in reversed(range(DEPTH)):
        send_sems, recv_sems, srcs, lands, _ = reductions[l]
        srcs, lands = _partials_send_wait(f"partials_send_wait_{l}", srcs, lands, send_sems, recv_sems, l > 0, done)
        for k, own, got in zip(bwd_groups[l], srcs, lands):
            cols = own.shape[-1]
            slot_rows = int(np.prod(own.shape[2:-1]))
            bufs[k] = _adamw(own.reshape(own.shape[0], -1, cols), me_idx if l > 0 else chip,
                             got.reshape(got.shape[0], -1, cols),
                             opt_view(k, weights[k]), opt_view(k, mom1[k]), opt_view(k, mom2[k]),
                             row0=bwd_groups[l][k][0] * slot_rows, bufs=bufs[k], after=done)
            done = bufs[k][0][:SUBLANES, :LANES]
    for k in big_names:
        grads[k], delta[k], new_m[k], new_v[k] = (opt_unview(k, b) for b in bufs[k])

    return (loss, grad_x, *[grads[k] for k in order], *[delta[k] for k in order],
            *[new_m[k] for k in order], *[new_v[k] for k in order])
```

```python
import numpy as np
import jax
import jax.numpy as jnp
from jax import lax
from jax.experimental import pallas as pl
from jax.experimental.pallas import tpu as pltpu

F32 = jnp.float32
BF16 = jnp.bfloat16
MESH_AXES = ("x", "y", "c")
N_DEV = 8
MESH_ID = pl.DeviceIdType.MESH

DEPTH = 4
N_A = 2
CHUNK = 64
N_HEADS = 16
LEFT_CHUNKS = 8
BAND = (LEFT_CHUNKS + 1) * CHUNK
LEFT_PAD = LEFT_CHUNKS * CHUNK
MAX_REL = 4 * CHUNK
N_REL = (CHUNK - 1) + MAX_REL + 1
GMLP_WINDOW = 128
GMLP_GROUPS = 8
ALPHA = (2.0 * DEPTH) ** 0.25
LN_EPS = 1e-5
ADAM_LR = 0.001
ADAM_B1 = 0.9
ADAM_B2 = 0.999
ADAM_EPS = 1e-08
ADAM_WD = 0.01
ADAM_STEP = 10

V7X_VMEM_BYTES = 64 * 1024 * 1024
VMEM_LIMIT = V7X_VMEM_BYTES - 8 * 1024 * 1024
LANES = 128
SUBLANES = 8
MM_BLOCK = 2048
BIG_ROW_BLOCK = 1024
ROW_BLOCK = 512
OPT_ROW_BLOCK = 256

_ANY = pl.BlockSpec(memory_space=pl.ANY)
_VMEM = pl.BlockSpec(memory_space=pltpu.VMEM)


def _params(sem=None):
    return pltpu.CompilerParams(dimension_semantics=sem, vmem_limit_bytes=VMEM_LIMIT)


def _row_block(rows, target):
    for d in range(min(rows, target), 0, -1):
        if rows % d == 0 and (d % SUBLANES == 0 or d == rows):
            return d
    return rows


def _matmul(name, a, b, out_shape4, out_dtype, *, la=0, lb=0, lo=0, ta=False, tb=False,
            reduce=False, b_merge=1, out_merge=1, out_buf=None, a_silu=False, after=None):
    ja_n, _, a_r, a_c = a.shape
    jb_n, _, b_r, b_c = b.shape
    jo_n, _, o_r, o_c = out_shape4
    m_tot = a_c if ta else a_r
    k_a = a_r if ta else a_c
    b_rows = b_merge * b_r
    k_c = b_c if tb else b_rows
    n = b_rows if tb else b_c
    n_chunks = (jb_n // b_merge) if reduce else 1
    natural_k = reduce and ja_n == 1
    assert n == o_c, (name, n, o_c)
    assert k_a ==(k_c * n_chunks if natural_k else k_c), (name, k_a, k_c, n_chunks)
    bk = k_c if (k_c <= MM_BLOCK or (b_merge > 1 and not tb)) else MM_BLOCK
    assert k_c % bk == 0
    nkk = k_c // bk
    kg = 2 if (reduce and ja_n > 1 and nkk == 1 and not ta and n_chunks % 2 == 0) else 1
    nk = n_chunks * nkk // kg
    m_out = out_merge * o_r
    assert m_tot == m_out, (name, m_tot, m_out)
    bm = m_tot if (m_tot <= MM_BLOCK or out_merge > 1) else MM_BLOCK
    assert m_tot % bm == 0
    jo_blocks = jo_n // out_merge

    def a_index(j, m, k):
        kj, kk = k // nkk, k % nkk
        ja = 0 if ja_n == 1 else (kj if reduce else j)
        ke = kk + kj * nkk if natural_k else kk
        return (ja, la, ke, m) if ta else (ja, la, m, ke)

    def b_index(j, m, k):
        kj, kk = k // nkk, k % nkk
        jb = 0 if jb_n == b_merge else (kj if reduce else j)
        return (jb, lb, 0, kk) if tb else (jb, lb, kk, 0)

    def o_index(j, m, k):
        return (j, lo, 0, 0) if out_merge > 1 else (j, lo, m, 0)

    a_block = (None, None, bk, bm) if ta else (None if kg == 1 else kg, None, bm, bk)
    if b_merge > 1:
        b_block = (kg * b_merge, None, b_r, bk if tb else n)
    else:
        b_block = (None if kg == 1 else kg, None) + ((n, bk) if tb else (bk, n))
    o_block = (out_merge, None, o_r, n) if out_merge > 1 else (None, None, bm, n)
    dims = (((0 if ta else 1,), (1 if tb else 0,)), ((), ()))

    in_place = nk > 1 and out_dtype == F32 and out_merge == 1
    use_acc = nk > 1 and not in_place

    def body(a_ref, b_ref, *rest):
        o_ref = rest[-2] if use_acc else rest[-1]
        k = pl.program_id(2)
        av = a_ref[...]
        if a_silu:
            af = av.astype(F32)
            av = af * jax.nn.sigmoid(af)
        bv = b_ref[...]
        if kg > 1:
            bv = bv.reshape(kg, -1, bv.shape[-1])
            prod = sum(lax.dot_general(av[g].astype(BF16), bv[g].astype(BF16), dims, preferred_element_type=F32)
                       for g in range(kg))
        else:
            if b_merge > 1:
                bv = bv.reshape(b_rows, bv.shape[-1])
            prod = lax.dot_general(av.astype(BF16), bv.astype(BF16), dims, preferred_element_type=F32)

        def emit(val):
            val = val.astype(out_dtype)
            o_ref[...] = val.reshape(out_merge, o_r, n) if out_merge > 1 else val

        if nk == 1:
            emit(prod)
            return
        acc_ref = o_ref if in_place else rest[-1]

        @pl.when(k == 0)
        def _():
            acc_ref[...] = prod

        @pl.when(k > 0)
        def _():
            acc_ref[...] += prod

        if use_acc:
            @pl.when(k == nk - 1)
            def _():
                emit(acc_ref[...])

    in_specs = [pl.BlockSpec(a_block, a_index), pl.BlockSpec(b_block, b_index)]
    operands = [a, b]
    aliases = {}
    if out_buf is not None:
        assert out_buf.shape == tuple(out_shape4) and out_buf.dtype == out_dtype
        in_specs.append(_ANY)
        operands.append(out_buf)
        aliases = {2: 0}
    if after is not None:
        in_specs.append(_ANY)
        operands.append(after)
    return pl.pallas_call(
        body, name=name,
        grid=(jo_blocks, m_tot // bm, nk),
        in_specs=in_specs,
        out_specs=pl.BlockSpec(o_block, o_index),
        out_shape=jax.ShapeDtypeStruct(tuple(out_shape4), out_dtype),
        scratch_shapes=[pltpu.VMEM((bm, n), F32)] if use_acc else [],
        input_output_aliases=aliases,
        compiler_params=_params(("parallel", "parallel", "arbitrary")),
    )(*operands)


def _as4(a):
    return a.reshape((1,) * (4 - a.ndim) + a.shape)


def _row_call(name, body, ins, outs, t, *, acc_outs=()):
    bt = _row_block(t, ROW_BLOCK)

    def spec(arr, tiled):
        if tiled:
            return pl.BlockSpec((bt,) + tuple(arr.shape[1:]), lambda i: (i,) + (0,) * (arr.ndim - 1))
        return pl.BlockSpec(tuple(arr.shape), lambda i: (0,) * arr.ndim)

    return pl.pallas_call(
        body, name=name, grid=(t // bt,),
        in_specs=[spec(a, tl) for a, tl in ins],
        out_specs=[spec(o, tl) for o, tl in outs],
        out_shape=[jax.ShapeDtypeStruct(o.shape, o.dtype) for o, _ in outs],
        compiler_params=_params(("arbitrary",) if acc_outs else ("parallel",)),
    )(*[a for a, _ in ins])


def _sds(shape, dtype):
    return jax.ShapeDtypeStruct(tuple(shape), dtype)


def _modulate(x, scl, shift):
    t, d = x.shape

    def body(x_ref, s_ref, b_ref, h_ref):
        h_ref[...] = (x_ref[...] * (1.0 + s_ref[...]) + b_ref[...]).astype(BF16)

    return _row_call("modulate", body, [(x, True), (scl, False), (shift, False)],
                     [(_sds((t, d), BF16), True)], t)[0]


def _ln_stats(r):
    mu = jnp.mean(r, axis=-1, keepdims=True)
    rc = r - mu
    var = jnp.mean(rc * rc, axis=-1, keepdims=True)
    rstd = lax.rsqrt(var + LN_EPS)
    return rc * rstd, rstd


def _ln_res_fwd(x, y, gw, g, b, mods=()):
    t, d = x.shape
    n_mod = len(mods)

    def body(x_ref, y_ref, gw_ref, g_ref, b_ref, *rest):
        mod_refs, o_ref, h_refs = rest[:2 * n_mod], rest[2 * n_mod], rest[2 * n_mod + 1:]
        r = ALPHA * x_ref[...] + gw_ref[...] * y_ref[...]
        xhat, _ = _ln_stats(r)
        xn = xhat * g_ref[...] + b_ref[...]
        o_ref[...] = xn
        for k in range(n_mod):
            h_refs[k][...] = (xn * (1.0 + mod_refs[2 * k][...]) + mod_refs[2 * k + 1][...]).astype(BF16)

    vecs = [(v, False) for pair in mods for v in pair]
    return _row_call("ln_res_fwd", body,
                     [(x, True), (y, True), (gw, False), (g, False), (b, False)] + vecs,
                     [(_sds((t, d), F32), True)] + [(_sds((t, d), BF16), True)] * n_mod, t)


def _ln_res_bwd(x, y, gw, g, b, dx_base, pairs=()):
    t, d = x.shape
    n_pair = len(pairs)

    def body(x_ref, y_ref, gw_ref, g_ref, b_ref, dxb_ref, *rest):
        pair_refs, outs = rest[:2 * n_pair], rest[2 * n_pair:]
        dx_ref, dy_ref = outs[0], outs[1]
        sums = outs[2:]

        @pl.when(pl.program_id(0) == 0)
        def _():
            for r in sums:
                r[...] = jnp.zeros_like(r)

        yv = y_ref[...]
        gwv = gw_ref[...]
        gv = g_ref[...]
        xhat, rstd = _ln_stats(ALPHA * x_ref[...] + gwv * yv)
        dxn = dxb_ref[...]
        if n_pair:
            xn = xhat * gv + b_ref[...]
            for k in range(n_pair):
                dh = pair_refs[2 * k][...]
                dxn = dxn + dh * (1.0 + pair_refs[2 * k + 1][...])
                sums[3 + 2 * k][...] += jnp.sum(dh * xn, axis=0, keepdims=True)
                sums[4 + 2 * k][...] += jnp.sum(dh, axis=0, keepdims=True)
        dxh = dxn * gv
        m1 = jnp.mean(dxh, axis=-1, keepdims=True)
        m2 = jnp.mean(dxh * xhat, axis=-1, keepdims=True)
        dr = rstd * (dxh - m1 - xhat * m2)
        dx_ref[...] = ALPHA * dr
        dy_ref[...] = (gwv * dr).astype(BF16)
        sums[0][...] += jnp.sum(dr * yv, axis=0, keepdims=True)
        sums[1][...] += jnp.sum(dxn * xhat, axis=0, keepdims=True)
        sums[2][...] += jnp.sum(dxn, axis=0, keepdims=True)

    vec = _sds((1, d), F32)
    n_sum = 3 + 2 * n_pair
    ins = [(x, True), (y, True), (gw, False), (g, False), (b, False), (dx_base, True)]
    for dh, scl in pairs:
        ins += [(dh, True), (scl, False)]
    return _row_call("ln_res_bwd", body, ins,
                     [(_sds((t, d), F32), True), (_sds((t, d), BF16), True)] + [(vec, False)] * n_sum, t,
                     acc_outs=tuple(range(2, 2 + n_sum)))


def _mod_bwd(dx_res, dh, x, scl):
    t, d = x.shape

    def body(dxr_ref, dh_ref, x_ref, s_ref, dx_ref, ds_ref, db_ref):
        @pl.when(pl.program_id(0) == 0)
        def _():
            ds_ref[...] = jnp.zeros_like(ds_ref)
            db_ref[...] = jnp.zeros_like(db_ref)

        dh = dh_ref[...]
        dx_ref[...] = dxr_ref[...] + dh * (1.0 + s_ref[...])
        ds_ref[...] += jnp.sum(dh * x_ref[...], axis=0, keepdims=True)
        db_ref[...] += jnp.sum(dh, axis=0, keepdims=True)

    vec = _sds((1, d), F32)
    return _row_call("mod_bwd", body, [(dx_res, True), (dh, True), (x, True), (scl, False)],
                     [(_sds((t, d), F32), True), (vec, False), (vec, False)], t, acc_outs=(1, 2))


def _loss_head(y, target):
    t, d = y.shape

    def body(y_ref, t_ref, l_ref, dy_ref):
        @pl.when(pl.program_id(0) == 0)
        def _():
            l_ref[...] = jnp.zeros_like(l_ref)

        err = y_ref[...] - t_ref[...]
        dy_ref[...] = err * (1.0 / d)
        part = 0.5 * jnp.sum(jnp.mean(err * err, axis=-1, keepdims=True), axis=0, keepdims=True)
        l_ref[...] += jnp.broadcast_to(part, l_ref.shape)

    return _row_call("loss_head", body, [(y, True), (target, True)],
                     [(_sds((SUBLANES, LANES), F32), False), (_sds((t, d), F32), True)], t,
                     acc_outs=(0,))


def _sigmoid(x):
    return 0.5 * jnp.tanh(0.5 * x) + 0.5


def _ffn_up_fwd(h, w_gu, lb, after=None):
    t, d = h.shape
    n = w_gu.shape[-1]
    half = N_DEV // 2
    bt = _row_block(t, BIG_ROW_BLOCK)
    extra = [] if after is None else [after]

    def body(h_ref, wg_ref, wu_ref, *rest):
        fac_ref, a_ref = rest[-2:]
        hv = h_ref[...]
        g = jnp.dot(hv, wg_ref[...], preferred_element_type=F32)
        u = jnp.dot(hv, wu_ref[...], preferred_element_type=F32)
        sig = _sigmoid(g)
        silu = g * sig
        fac_ref[0] = u * (sig + silu * (1.0 - sig))
        fac_ref[1] = silu
        a_ref[...] = (silu * u).astype(BF16)

    return pl.pallas_call(
        body, name="ffn_up_fwd", grid=(half, t // bt),
        in_specs=[pl.BlockSpec((bt, d), lambda j, i: (i, 0)),
                  pl.BlockSpec((None, None, d, n), lambda j, i: (j, lb, 0, 0)),
                  pl.BlockSpec((None, None, d, n), lambda j, i: (half + j, lb, 0, 0))] + [_ANY] * len(extra),
        out_specs=[pl.BlockSpec((2, None, bt, n), lambda j, i: (0, j, i, 0)),
                   pl.BlockSpec((None, bt, n), lambda j, i: (j, i, 0))],
        out_shape=[_sds((2, half, t, n), F32), _sds((half, t, n), BF16)],
        compiler_params=_params(("parallel", "parallel")),
    )(h, w_gu, w_gu, *extra)


def _ffn_down_bwd_a(dy, w_down, lb, fac):
    t, d = dy.shape
    _, half, _, n = fac.shape
    r = w_down.shape[2]
    bt = _row_block(t, MM_BLOCK)

    def body(dy_ref, w_ref, fac_ref, d_ref):
        da = lax.dot_general(dy_ref[...], w_ref[...].reshape(2 * r, d), (((1,), (1,)), ((), ())),
                             preferred_element_type=F32)
        d_ref[0] = (da * fac_ref[0]).astype(BF16)
        d_ref[1] = (da * fac_ref[1]).astype(BF16)

    return pl.pallas_call(
        body, name="ffn_down_bwd_a", grid=(half, t // bt),
        in_specs=[pl.BlockSpec((bt, d), lambda j, i: (i, 0)),
                  pl.BlockSpec((2, None, r, d), lambda j, i: (j, lb, 0, 0)),
                  pl.BlockSpec((2, None, bt, n), lambda j, i: (0, j, i, 0))],
        out_specs=pl.BlockSpec((2, None, bt, n), lambda j, i: (0, j, i, 0)),
        out_shape=_sds((2, half, t, n), BF16),
        compiler_params=_params(("parallel", "parallel")),
    )(dy, w_down, fac)


_INV_SQRT2 = 0.7071067811865476
_INV_SQRT_2PI = 0.3989422804014327


def _gelu(z):
    return 0.5 * z * (1.0 + lax.erf(z * _INV_SQRT2))


def _gelu_grad(z):
    return 0.5 * (1.0 + lax.erf(z * _INV_SQRT2)) + z * jnp.exp(-0.5 * z * z) * _INV_SQRT_2PI


def _window_mask():
    t_out = lax.broadcasted_iota(jnp.int32, (GMLP_WINDOW, GMLP_WINDOW), 0)
    s_in = lax.broadcasted_iota(jnp.int32, (GMLP_WINDOW, GMLP_WINDOW), 1)
    return (s_in // CHUNK) <= (t_out // CHUNK)


def _gmlp_recompute(z_ref, bin_ref, lng_ref, lnb_ref):
    half = N_DEV // 2
    z = z_ref[...] + bin_ref[...]
    ge = _gelu(z)
    u = ge[:half]
    v = ge[half:]
    width = half * v.shape[-1]
    mu = jnp.sum(jnp.sum(v, axis=0), axis=-1, keepdims=True) / width
    vc = v - mu
    var = jnp.sum(jnp.sum(vc * vc, axis=0), axis=-1, keepdims=True) / width
    rstd = lax.rsqrt(var + LN_EPS)
    xhat = vc * rstd
    vn = xhat * lng_ref[...] + lnb_ref[...]
    return z, u, xhat, rstd, vn


def _gmlp_mid_fwd(zpre, b_in, ln_g, ln_b, w_s, b_s):
    _, t, n = zpre.shape
    half = N_DEV // 2
    gd = half * n // GMLP_GROUPS
    per = n // gd
    w = GMLP_WINDOW

    def body(z_ref, bin_ref, lng_ref, lnb_ref, ws_ref, bs_ref, o_ref):
        _, u, _, _, vn = _gmlp_recompute(z_ref, bin_ref, lng_ref, lnb_ref)
        mask = _window_mask()
        for g in range(GMLP_GROUPS):
            sh, c0 = g // per, (g % per) * gd
            wsm = jnp.where(mask, ws_ref[g], 0.0).astype(BF16)
            s = jnp.dot(wsm, vn[sh][:, c0:c0 + gd].astype(BF16), preferred_element_type=F32) + bs_ref[g]
            o_ref[sh, :, c0:c0 + gd] = (u[sh][:, c0:c0 + gd] * s).astype(BF16)

    whole = lambda a: pl.BlockSpec(tuple(a.shape), lambda i: (0,) * a.ndim)
    return pl.pallas_call(
        body, name="gmlp_mid_fwd", grid=(t // w,),
        in_specs=[pl.BlockSpec((N_DEV, w, n), lambda i: (0, i, 0)),
                  whole(b_in), whole(ln_g), whole(ln_b), whole(w_s), whole(b_s)],
        out_specs=pl.BlockSpec((half, w, n), lambda i: (0, i, 0)),
        out_shape=_sds((half, t, n), BF16),
        compiler_params=_params(("parallel",)),
    )(zpre, b_in, ln_g, ln_b, w_s, b_s)


def _gmlp_mid_bwd(zpre, dgated, b_in, ln_g, ln_b, w_s, b_s):
    _, t, n = zpre.shape
    half = N_DEV // 2
    gd = half * n // GMLP_GROUPS
    per = n // gd
    w = GMLP_WINDOW
    width = half * n

    def body(z_ref, dg_ref, bin_ref, lng_ref, lnb_ref, ws_ref, bs_ref,
             dz_ref, dws_ref, dbs_ref, dlng_ref, dlnb_ref, dbin_ref, du_ref, dvn_ref):
        @pl.when(pl.program_id(0) == 0)
        def _():
            for r in (dws_ref, dbs_ref, dlng_ref, dlnb_ref, dbin_ref):
                r[...] = jnp.zeros_like(r)

        z, u, xhat, rstd, vn = _gmlp_recompute(z_ref, bin_ref, lng_ref, lnb_ref)
        mask = _window_mask()
        for g in range(GMLP_GROUPS):
            sh, c0 = g // per, (g % per) * gd
            wsm = jnp.where(mask, ws_ref[g], 0.0).astype(BF16)
            vg = vn[sh][:, c0:c0 + gd].astype(BF16)
            s = jnp.dot(wsm, vg, preferred_element_type=F32) + bs_ref[g]
            dgt = dg_ref[sh, :, c0:c0 + gd]
            ds = dgt * u[sh][:, c0:c0 + gd]
            du_ref[sh, :, c0:c0 + gd] = dgt * s
            dsb = ds.astype(BF16)
            dws = lax.dot_general(dsb, vg, (((1,), (1,)), ((), ())), preferred_element_type=F32)
            dws_ref[g] += jnp.where(mask, dws, 0.0)
            dbs_ref[g] += jnp.sum(ds, axis=-1, keepdims=True)
            dvn_ref[sh, :, c0:c0 + gd] = lax.dot_general(wsm, dsb, (((0,), (0,)), ((), ())),
                                                         preferred_element_type=F32)
        dvn = dvn_ref[...]
        dlng_ref[...] += jnp.sum(dvn * xhat, axis=1, keepdims=True)
        dlnb_ref[...] += jnp.sum(dvn, axis=1, keepdims=True)
        dxh = dvn * lng_ref[...]
        m1 = jnp.sum(jnp.sum(dxh, axis=0), axis=-1, keepdims=True) / width
        m2 = jnp.sum(jnp.sum(dxh * xhat, axis=0), axis=-1, keepdims=True) / width
        dv = rstd * (dxh - m1 - xhat * m2)
        gg = _gelu_grad(z)
        dzu = du_ref[...] * gg[:half]
        dzv = dv * gg[half:]
        dz_ref[:half] = dzu.astype(BF16)
        dz_ref[half:] = dzv.astype(BF16)
        dbin_ref[:half] += jnp.sum(dzu, axis=1, keepdims=True)
        dbin_ref[half:] += jnp.sum(dzv, axis=1, keepdims=True)

    whole = lambda a: pl.BlockSpec(tuple(a.shape), lambda i: (0,) * a.ndim)
    outs = [_sds((N_DEV, t, n), BF16), _sds(w_s.shape, F32), _sds(b_s.shape, F32),
            _sds(ln_g.shape, F32), _sds(ln_b.shape, F32), _sds(b_in.shape, F32)]
    return pl.pallas_call(
        body, name="gmlp_mid_bwd", grid=(t // w,),
        in_specs=[pl.BlockSpec((N_DEV, w, n), lambda i: (0, i, 0)),
                  pl.BlockSpec((half, w, n), lambda i: (0, i, 0)),
                  whole(b_in), whole(ln_g), whole(ln_b), whole(w_s), whole(b_s)],
        out_specs=[pl.BlockSpec((N_DEV, w, n), lambda i: (0, i, 0))] + [whole(o) for o in outs[1:]],
        out_shape=outs,
        scratch_shapes=[pltpu.VMEM((half, w, n), F32), pltpu.VMEM((half, w, n), F32)],
        compiler_params=_params(("arbitrary",)),
    )(zpre, dgated, b_in, ln_g, ln_b, w_s, b_s)


ATTN_CHUNKS = 4
ATTN_ROWS = ATTN_CHUNKS * CHUNK
ATTN_WINDOW = ATTN_ROWS + LEFT_PAD
ATTN_DIAGS = -(-(ATTN_ROWS + ATTN_WINDOW - 1) // LANES) * LANES
ATTN_ROLL = ATTN_DIAGS - (ATTN_ROWS - 1)


def _rel_vector(rel):
    j = np.arange(ATTN_DIAGS)
    idx = np.clip(ATTN_WINDOW - 1 - j, -(CHUNK - 1), MAX_REL) + (CHUNK - 1)
    return rel[:, idx]


def _attn_bias_mask(rel_ref, bm_ref):
    tt = lax.broadcasted_iota(jnp.int32, (ATTN_ROWS, ATTN_WINDOW), 0) // CHUNK
    rr = lax.broadcasted_iota(jnp.int32, (ATTN_ROWS, ATTN_WINDOW), 1) // CHUNK
    band = (rr >= tt) & (rr <= tt + LEFT_CHUNKS)
    for j in range(bm_ref.shape[0]):
        vec = jnp.broadcast_to(rel_ref[j:j + 1, :], (ATTN_ROWS, ATTN_DIAGS))
        toeplitz = pltpu.roll(vec, ATTN_ROLL, 1, stride=1, stride_axis=0)[:, :ATTN_WINDOW]
        bm_ref[j] = jnp.where(band, toeplitz, -jnp.inf)


def _attn_probs(q_ref, k_ref, bm_ref, j, hd, start, valid):
    qh = q_ref[:, j * hd:(j + 1) * hd]
    kb = k_ref[pl.ds(start, ATTN_WINDOW), j * hd:(j + 1) * hd]
    sc = lax.dot_general(qh, kb, (((1,), (1,)), ((), ())), preferred_element_type=F32)
    sc = sc * (hd ** -0.5) + bm_ref[j]
    sc = jnp.where(valid, sc, -jnp.inf)
    sc = sc - jnp.max(sc, axis=-1, keepdims=True)
    e = jnp.exp(sc)
    return e / jnp.sum(e, axis=-1, keepdims=True), qh, kb


def _window_valid(start):
    r = lax.broadcasted_iota(jnp.int32, (1, ATTN_WINDOW), 1)
    return (start + r) >= LEFT_PAD


def _attn_fwd(q, kvp, rel_vec):
    t, d = q.shape
    hd = d // N_HEADS
    half = N_DEV // 2
    n = kvp.shape[-1]
    per = n // hd
    rows = kvp.shape[1]

    def body(q_ref, k_ref, v_ref, rel_ref, o_ref, bm_ref):
        @pl.when(pl.program_id(1) == 0)
        def _():
            _attn_bias_mask(rel_ref, bm_ref)

        start = pl.multiple_of(pl.program_id(1) * ATTN_ROWS, ATTN_ROWS)
        valid = _window_valid(start)
        for j in range(per):
            p, _, _ = _attn_probs(q_ref, k_ref, bm_ref, j, hd, start, valid)
            vb = v_ref[pl.ds(start, ATTN_WINDOW), j * hd:(j + 1) * hd]
            o_ref[:, j * hd:(j + 1) * hd] = jnp.dot(p.astype(BF16), vb, preferred_element_type=F32).astype(BF16)

    return pl.pallas_call(
        body, name="attn_fwd", grid=(half, t // ATTN_ROWS),
        in_specs=[pl.BlockSpec((ATTN_ROWS, n), lambda g, i: (i, g)),
                  pl.BlockSpec((None, rows, n), lambda g, i: (g, 0, 0)),
                  pl.BlockSpec((None, rows, n), lambda g, i: (half + g, 0, 0)),
                  pl.BlockSpec((None, per, ATTN_DIAGS), lambda g, i: (g, 0, 0))],
        out_specs=pl.BlockSpec((ATTN_ROWS, n), lambda g, i: (i, g)),
        out_shape=_sds((t, d), BF16),
        scratch_shapes=[pltpu.VMEM((per, ATTN_ROWS, ATTN_WINDOW), F32)],
        compiler_params=_params(("arbitrary", "arbitrary")),
    )(q, kvp, kvp, rel_vec.reshape(half, per, ATTN_DIAGS))


def _attn_bwd(q, dout, kvp, rel_vec, dk_in=None, dv_in=None):
    t, d = q.shape
    hd = d // N_HEADS
    half = N_DEV // 2
    n = kvp.shape[-1]
    per = n // hd
    rows = kvp.shape[1]
    scale = hd ** -0.5
    carry = dk_in is not None

    def body(q_ref, do_ref, k_ref, v_ref, rel_ref, *rest):
        dq_ref, dk_ref, dv_ref, dsc_ref, bm_ref = rest[-5:]

        @pl.when(pl.program_id(1) == 0)
        def _():
            _attn_bias_mask(rel_ref, bm_ref)
            dk_ref[...] = rest[0][...] if carry else jnp.zeros_like(dk_ref)
            dv_ref[...] = rest[1][...] if carry else jnp.zeros_like(dv_ref)
            dsc_ref[...] = jnp.zeros_like(dsc_ref)

        start = pl.multiple_of(pl.program_id(1) * ATTN_ROWS, ATTN_ROWS)
        valid = _window_valid(start)
        for j in range(per):
            cols = slice(j * hd, (j + 1) * hd)
            p, qh, kb = _attn_probs(q_ref, k_ref, bm_ref, j, hd, start, valid)
            vb = v_ref[pl.ds(start, ATTN_WINDOW), cols]
            doh = do_ref[:, cols]
            dp = lax.dot_general(doh, vb, (((1,), (1,)), ((), ())), preferred_element_type=F32)
            ds = p * (dp - jnp.sum(dp * p, axis=-1, keepdims=True))
            dsc_ref[j] += sum(ds[a * CHUNK:(a + 1) * CHUNK, a * CHUNK:a * CHUNK + BAND]
                              for a in range(ATTN_CHUNKS))
            dsb = (ds * scale).astype(BF16)
            dq_ref[:, cols] = jnp.dot(dsb, kb, preferred_element_type=F32).astype(BF16)
            dk_ref[pl.ds(start, ATTN_WINDOW), cols] += lax.dot_general(
                dsb, qh, (((0,), (0,)), ((), ())), preferred_element_type=F32)
            dv_ref[pl.ds(start, ATTN_WINDOW), cols] += lax.dot_general(
                p.astype(BF16), doh, (((0,), (0,)), ((), ())), preferred_element_type=F32)

    tile = pl.BlockSpec((ATTN_ROWS, n), lambda g, i: (i, g))
    shard = pl.BlockSpec((None, rows, n), lambda g, i: (g, 0, 0))
    in_specs = [tile, tile, shard, pl.BlockSpec((None, rows, n), lambda g, i: (half + g, 0, 0)),
                pl.BlockSpec((None, per, ATTN_DIAGS), lambda g, i: (g, 0, 0))]
    operands = [q, dout, kvp, kvp, rel_vec.reshape(half, per, ATTN_DIAGS)]
    if carry:
        in_specs += [shard, shard]
        operands += [dk_in, dv_in]
    acc = _sds((half, rows, n), F32)
    return pl.pallas_call(
        body, name="attn_bwd", grid=(half, t // ATTN_ROWS),
        in_specs=in_specs,
        out_specs=[tile, shard, shard, pl.BlockSpec((per, CHUNK, BAND), lambda g, i: (g, 0, 0))],
        out_shape=[_sds((t, d), BF16), acc, acc, _sds((N_HEADS, CHUNK, BAND), F32)],
        scratch_shapes=[pltpu.VMEM((per, ATTN_ROWS, ATTN_WINDOW), F32)],
        compiler_params=_params(("arbitrary", "arbitrary")),
    )(*operands)


SKEW_PITCH = 640
SKEW = SKEW_PITCH + 1
SKEW_LANES = -(-SKEW // LANES) * LANES


def _skew_diagonals(dsc):
    h = dsc.shape[0]
    wide = jnp.pad(dsc, ((0, 0), (0, 0), (0, SKEW_PITCH - BAND))).reshape(h, CHUNK * SKEW_PITCH)
    wide = jnp.pad(wide, ((0, 0), (0, CHUNK))).reshape(h, CHUNK, SKEW)
    return jnp.pad(wide, ((0, 0), (0, 0), (0, SKEW_LANES - SKEW)))


def _rel_bias_grad(skewed):
    heads = skewed.shape[0]
    hb = SUBLANES

    def body(d_ref, o_ref):
        col = lax.broadcasted_iota(jnp.int32, (SKEW_LANES, N_REL), 0)
        bucket = lax.broadcasted_iota(jnp.int32, (SKEW_LANES, N_REL), 1)
        diag = jnp.where(col < BAND, col, col - SKEW)
        idx = jnp.clip(LEFT_PAD - diag, -(CHUNK - 1), MAX_REL) + (CHUNK - 1)
        oh = ((idx == bucket) & (col < SKEW)).astype(BF16)
        dv = jnp.sum(d_ref[...], axis=1)
        hi = dv.astype(BF16)
        rest = dv - hi.astype(F32)
        mid = rest.astype(BF16)
        lo = (rest - mid.astype(F32)).astype(BF16)
        acc = jnp.dot(hi, oh, preferred_element_type=F32)
        acc += jnp.dot(mid, oh, preferred_element_type=F32)
        acc += jnp.dot(lo, oh, preferred_element_type=F32)
        o_ref[...] = acc

    return pl.pallas_call(
        body, name="rel_bias_grad", grid=(heads // hb,),
        in_specs=[pl.BlockSpec((hb, CHUNK, SKEW_LANES), lambda i: (i, 0, 0))],
        out_specs=pl.BlockSpec((hb, N_REL), lambda i: (i, 0)),
        out_shape=_sds((heads, N_REL), F32),
        compiler_params=_params(("parallel",)),
    )(skewed)


def _sum_parts(parts):
    s_n, rows, c = parts.shape
    br = _row_block(rows, OPT_ROW_BLOCK)

    def body(p_ref, o_ref):
        acc = p_ref[0].astype(F32)
        for s in range(1, s_n):
            acc = acc + p_ref[s].astype(F32)
        o_ref[...] = acc

    return pl.pallas_call(
        body, name="sum_parts", grid=(rows // br,),
        in_specs=[pl.BlockSpec((s_n, br, c), lambda i: (0, i, 0))],
        out_specs=pl.BlockSpec((br, c), lambda i: (i, 0)),
        out_shape=_sds((rows, c), F32),
        compiler_params=_params(("parallel",)),
    )(parts)


def _adamw(own, own_idx, parts, w, m, v, row0=0, bufs=None, after=None):
    _, rows, c = own.shape
    s_n = 0 if parts is None else parts.shape[0]
    total = w.shape[0]
    br = _row_block(rows, OPT_ROW_BLOCK)
    assert row0 % br == 0 and (bufs is not None or (row0 == 0 and total == rows))
    b0 = row0 // br
    m_corr = 1.0 - ADAM_B1 ** ADAM_STEP
    v_corr = 1.0 - ADAM_B2 ** ADAM_STEP

    def body(idx_ref, own_ref, *refs):
        if s_n:
            p_ref, refs = refs[0], refs[1:]
        w_ref, m_ref, v_ref = refs[:3]
        g_ref, d_ref, nm_ref, nv_ref = refs[-4:]
        g = own_ref[...].astype(F32)
        for s in range(s_n):
            g = g + p_ref[s].astype(F32)
        nm = ADAM_B1 * m_ref[...] + (1.0 - ADAM_B1) * g
        nv = ADAM_B2 * v_ref[...] + (1.0 - ADAM_B2) * (g * g)
        g_ref[...] = g
        nm_ref[...] = nm
        nv_ref[...] = nv
        d_ref[...] = -ADAM_LR * ((nm / m_corr) / (jnp.sqrt(nv / v_corr) + ADAM_EPS) + ADAM_WD * w_ref[...])

    tile = pl.BlockSpec((br, c), lambda i, idx: (i + b0, 0))
    in_specs = [pl.BlockSpec((None, br, c), lambda i, idx: (idx[0], i, 0))]
    operands = [own_idx, own]
    if s_n:
        in_specs.append(pl.BlockSpec((s_n, br, c), lambda i, idx: (0, i, 0)))
        operands.append(parts)
    in_specs += [tile, tile, tile]
    operands += [w, m, v]
    aliases = {}
    if bufs is not None:
        aliases = {len(operands) + j: j for j in range(4)}
        in_specs += [_ANY] * 4
        operands += list(bufs)
    if after is not None:
        in_specs.append(_ANY)
        operands.append(after)
    out = _sds((total, c), F32)
    return pl.pallas_call(
        body, name="adamw",
        grid_spec=pltpu.PrefetchScalarGridSpec(
            num_scalar_prefetch=1, grid=(rows // br,), in_specs=in_specs,
            out_specs=[tile, tile, tile, tile]),
        out_shape=[out, out, out, out],
        input_output_aliases=aliases,
        compiler_params=_params(("parallel",)),
    )(*operands)


def _chip_sum(p, r1, core):
    half = N_DEV // 2
    c = p.shape[-1]
    rows = int(np.prod(p.shape[1:-1]))
    br = _row_block(rows, BIG_ROW_BLOCK)

    def body(core_ref, p_ref, r_ref, o_ref):
        o_ref[...] = (p_ref[...].astype(F32) + r_ref[...].astype(F32)).astype(BF16)

    out = pl.pallas_call(
        body, name="chip_sum",
        grid_spec=pltpu.PrefetchScalarGridSpec(
            num_scalar_prefetch=1, grid=(half, rows // br),
            in_specs=[pl.BlockSpec((None, None, br, c), lambda q, i, cr: (q, cr[0], i, 0)),
                      pl.BlockSpec((None, br, c), lambda q, i, cr: (q, i, 0))],
            out_specs=pl.BlockSpec((None, br, c), lambda q, i, cr: (q, i, 0))),
        out_shape=_sds((half, rows, c), BF16),
        compiler_params=_params(("parallel", "parallel")),
    )(core, p.reshape(half, 2, rows, c), r1.reshape(half, rows, c))
    return out.reshape((half,) + p.shape[1:])


def _position():
    return tuple(lax.axis_index(a) for a in MESH_AXES)


def _linear(px, py, pc):
    return 4 * px + 2 * py + pc


def _all_gather_small(v, after=()):
    rows, lanes = v.shape

    def body(x_ref, *rest):
        out_ref, send_sems, recv_sems, local_sem = rest[-4:]
        x, y, c = _position()
        me, sibling = (x, y, c), (x, y, 1 - c)
        chips = [(1 - x, y), (x, 1 - y), (1 - x, 1 - y)]

        def copy(k, block, to, src=None):
            dst = out_ref.at[_linear(*block)]
            return pltpu.make_async_remote_copy(
                src_ref=dst if src is None else src, dst_ref=dst,
                send_sem=send_sems.at[k], recv_sem=recv_sems.at[k],
                device_id=to, device_id_type=MESH_ID)

        mine = pltpu.make_async_copy(x_ref, out_ref.at[_linear(*me)], local_sem)
        mine.start()
        first = [copy(0, me, sibling, src=x_ref)]
        first += [copy(1 + j, me, (*chip, c), src=x_ref) for j, chip in enumerate(chips)]
        for cp in first:
            cp.start()
        passed = [copy(4 + j, (*chip, c), sibling) for j, chip in enumerate(chips)]
        for j, chip in enumerate(chips):
            copy(1 + j, (*chip, c), me).wait_recv()
            passed[j].start()
        copy(0, sibling, me).wait_recv()
        for j, chip in enumerate(chips):
            copy(4 + j, (*chip, 1 - c), me).wait_recv()
        for cp in first + passed:
            cp.wait_send()
        mine.wait()

    return pl.pallas_call(
        body, name="all_gather_small",
        out_shape=_sds((N_DEV, rows, lanes), v.dtype),
        in_specs=[_VMEM] + [_ANY] * len(after), out_specs=_VMEM,
        scratch_shapes=[pltpu.SemaphoreType.DMA((7,)), pltpu.SemaphoreType.DMA((7,)),
                        pltpu.SemaphoreType.DMA],
        compiler_params=pltpu.CompilerParams(vmem_limit_bytes=VMEM_LIMIT),
    )(v, *after)


_HBM = pl.BlockSpec(memory_space=pltpu.HBM)
_SEM = pl.BlockSpec(memory_space=pltpu.SEMAPHORE)
_EFFECT = pltpu.SideEffectType.DATAFLOW_SIDE_EFFECTING
_ALL_CHIPS = [(0, 0), (0, 1), (1, 0), (1, 1)]


def _other_chips(x, y):
    return [(1 - x, y), (x, 1 - y), (1 - x, 1 - y)]


def _in_hbm(a):
    return pltpu.with_memory_space_constraint(a, pltpu.HBM)


def _token():
    return _sds((SUBLANES, LANES), F32)


def _gather_copies(refs, send_sems, recv_sems, to_sibling):
    x, y, c = _position()
    if to_sibling:
        plan = [((x, y, 1 - c), _linear(*q, c), _linear(*q, 1 - c)) for q in _ALL_CHIPS]
    else:
        plan = [((*ch, c), _linear(x, y, c), _linear(*ch, c)) for ch in _other_chips(x, y)]

    def copy(ref, i, k, peer, block):
        return pltpu.make_async_remote_copy(
            src_ref=ref.at[block], dst_ref=ref.at[block],
            send_sem=send_sems.at[len(plan) * i + k], recv_sem=recv_sems.at[len(plan) * i + k],
            device_id=peer, device_id_type=MESH_ID)

    return [(copy(ref, i, k, peer, sent), copy(ref, i, k, peer, landed))
            for i, ref in enumerate(refs) for k, (peer, sent, landed) in enumerate(plan)]


def _gather_ici_start(name, lands, after=None, to_sibling=False):
    n = len(lands)
    extra = [] if after is None else [after]
    n_sem = n * (len(_ALL_CHIPS) if to_sibling else len(_ALL_CHIPS) - 1)

    def body(*refs):
        send_sems, recv_sems, token = refs[-n - 3], refs[-n - 2], refs[-1]
        for sent, _ in _gather_copies(refs[:n], send_sems, recv_sems, to_sibling):
            sent.start()
        token[...] = jnp.zeros_like(token)

    out = pl.pallas_call(
        body, name=name,
        out_shape=(pltpu.SemaphoreType.DMA((n_sem,)), pltpu.SemaphoreType.DMA((n_sem,)),
                   *[pltpu.HBM(a.shape, a.dtype) for a in lands], _token()),
        in_specs=[_HBM] * n + [_ANY] * len(extra), out_specs=(_SEM, _SEM, *[_HBM] * n, _VMEM),
        input_output_aliases={i: 2 + i for i in range(n)},
        compiler_params=pltpu.CompilerParams(has_side_effects=_EFFECT),
    )(*[_in_hbm(a) for a in lands], *extra)
    return out[0], out[1], list(out[2:2 + n]), out[-1]


def _gather_ici_wait(name, lands, send_sems, recv_sems, after, to_sibling=False):
    n = len(lands)

    def body(*refs):
        for sent, landed in _gather_copies(refs[:n], refs[n], refs[n + 1], to_sibling):
            sent.wait_send()
            landed.wait_recv()

    out = pl.pallas_call(
        body, name=name,
        out_shape=[pltpu.HBM(a.shape, a.dtype) for a in lands],
        in_specs=[_HBM] * n + [_SEM, _SEM, _ANY], out_specs=[_HBM] * n,
        input_output_aliases={i: i for i in range(n)},
        compiler_params=pltpu.CompilerParams(has_side_effects=_EFFECT),
    )(*lands, send_sems, recv_sems, after)
    return list(out)


def _gather_d2d(lands):
    n = len(lands)

    def body(*refs):
        ins, outs, send_sems, recv_sems = refs[:n], refs[n:2 * n], refs[2 * n], refs[2 * n + 1]
        x, y, c = _position()

        def copy(i, q, core):
            block = _linear(*_ALL_CHIPS[q], core)
            return pltpu.make_async_remote_copy(
                src_ref=ins[i].at[block], dst_ref=outs[i].at[block],
                send_sem=send_sems.at[i, q], recv_sem=recv_sems.at[i, q],
                device_id=(x, y, 1 - c), device_id_type=MESH_ID)

        sent = [copy(i, q, c) for i in range(n) for q in range(len(_ALL_CHIPS))]
        for cp in sent:
            cp.start()
        for i in range(n):
            for q in range(len(_ALL_CHIPS)):
                copy(i, q, 1 - c).wait_recv()
        for cp in sent:
            cp.wait_send()

    return pl.pallas_call(
        body, name="gather_d2d",
        out_shape=[_sds(a.shape, a.dtype) for a in lands],
        in_specs=[_ANY] * n, out_specs=[_ANY] * n,
        input_output_aliases={i: i for i in range(n)},
        scratch_shapes=[pltpu.SemaphoreType.DMA((n, 4)), pltpu.SemaphoreType.DMA((n, 4))],
    )(*lands)


def _partials_d2d(parts):
    n = len(parts)
    half = N_DEV // 2

    def body(*refs):
        ins, outs, send_sems, recv_sems = refs[:n], refs[n:2 * n], refs[2 * n], refs[2 * n + 1]
        x, y, c = _position()

        def copy(i, q):
            return pltpu.make_async_remote_copy(
                src_ref=ins[i].at[_linear(*_ALL_CHIPS[q], 1 - c)], dst_ref=outs[i].at[q],
                send_sem=send_sems.at[i, q], recv_sem=recv_sems.at[i, q],
                device_id=(x, y, 1 - c), device_id_type=MESH_ID)

        sent = [copy(i, q) for i in range(n) for q in range(half)]
        for cp in sent:
            cp.start()
        for cp in sent:
            cp.wait_recv()
        for cp in sent:
            cp.wait_send()

    return pl.pallas_call(
        body, name="partials_d2d",
        out_shape=[_sds((half,) + p.shape[1:], p.dtype) for p in parts],
        in_specs=[_ANY] * n, out_specs=[_ANY] * n,
        scratch_shapes=[pltpu.SemaphoreType.DMA((n, half)), pltpu.SemaphoreType.DMA((n, half))],
    )(*parts)


def _partials_peers(x, y, c, direct):
    chips = _other_chips(x, y)
    if not direct:
        return [((*ch, c), 2 * ch[0] + ch[1]) for ch in chips]
    peers = [(x, y, 1 - c)] + [(*ch, c) for ch in chips] + [(*ch, 1 - c) for ch in chips]
    return [(p, _linear(*p)) for p in peers]


def _partials_copies(srcs, lands, send_sems, recv_sems, direct):
    x, y, c = _position()
    peers = _partials_peers(x, y, c, direct)
    return [pltpu.make_async_remote_copy(
        src_ref=srcs[i].at[block], dst_ref=lands[i].at[k],
        send_sem=send_sems.at[len(peers) * i + k], recv_sem=recv_sems.at[len(peers) * i + k],
        device_id=peer, device_id_type=MESH_ID)
        for i in range(len(srcs)) for k, (peer, block) in enumerate(peers)]


def _partials_send_start(name, srcs, lands, direct, after=None):
    n = len(srcs)
    n_sem = n * (N_DEV - 1 if direct else len(_ALL_CHIPS) - 1)

    def body(*refs):
        _, send_sems, recv_sems = refs[:2 * n], refs[-2 * n - 3], refs[-2 * n - 2]
        for cp in _partials_copies(refs[:n], refs[n:2 * n], send_sems, recv_sems, direct):
            cp.start()
        refs[-1][...] = jnp.zeros_like(refs[-1])

    both = list(srcs) + list(lands)
    extra = [] if after is None else [after]
    out = pl.pallas_call(
        body, name=name,
        out_shape=(pltpu.SemaphoreType.DMA((n_sem,)), pltpu.SemaphoreType.DMA((n_sem,)),
                   *[pltpu.HBM(a.shape, a.dtype) for a in both], _token()),
        in_specs=[_HBM] * (2 * n) + [_ANY] * len(extra), out_specs=(_SEM, _SEM, *[_HBM] * (2 * n), _VMEM),
        input_output_aliases={i: 2 + i for i in range(2 * n)},
        compiler_params=pltpu.CompilerParams(has_side_effects=_EFFECT),
    )(*[_in_hbm(a) for a in both], *extra)
    return out[0], out[1], list(out[2:2 + n]), list(out[2 + n:2 + 2 * n]), out[-1]


def _partials_send_wait(name, srcs, lands, send_sems, recv_sems, direct, after):
    n = len(srcs)

    def body(*refs):
        for cp in _partials_copies(refs[:n], refs[n:2 * n], refs[2 * n], refs[2 * n + 1], direct):
            cp.wait_send()
            cp.wait_recv()

    both = list(srcs) + list(lands)
    out = pl.pallas_call(
        body, name=name,
        out_shape=[pltpu.HBM(a.shape, a.dtype) for a in both],
        in_specs=[_HBM] * (2 * n) + [_SEM, _SEM, _ANY], out_specs=[_HBM] * (2 * n),
        input_output_aliases={i: i for i in range(2 * n)},
        compiler_params=pltpu.CompilerParams(has_side_effects=_EFFECT),
    )(*both, send_sems, recv_sems, after)
    return list(out[:n]), list(out[n:])


def _pack(arrs):
    flat = jnp.concatenate([a.reshape(-1).astype(F32) for a in arrs])
    block = OPT_ROW_BLOCK if flat.shape[0] > OPT_ROW_BLOCK * LANES else SUBLANES
    pad = (-flat.shape[0]) % (block * LANES)
    if pad:
        flat = jnp.concatenate([flat, jnp.zeros((pad,), F32)])
    return flat.reshape(-1, LANES)


def _unpack(packed, shapes, lead=()):
    flat = packed.reshape(lead + (-1,))
    out, off = [], 0
    for s in shapes:
        size = int(np.prod(s))
        out.append(flat[..., off:off + size].reshape(lead + tuple(s)))
        off += size
    return out


def _unshard_last(g):
    nd = g.ndim
    perm = tuple(range(1, nd - 1)) + (0, nd - 1)
    t = jnp.transpose(g, perm)
    return t.reshape(t.shape[:-2] + (N_DEV * g.shape[-1],))


def kernel(x, c, w_ada, b_ada, ln_g, ln_b, ffn_gu, ffn_down, gmlp_w_in, gmlp_b_in, gmlp_ln_g, gmlp_ln_b, gmlp_w_s, gmlp_b_s, gmlp_w_out, w_ada_kv, b_ada_kv, w_kv, attn_w_q, attn_rel_bias, attn_w_o, loss_target, m_w_ada, m_b_ada, m_ln_g, m_ln_b, m_ffn_gu, m_ffn_down, m_gmlp_w_in, m_gmlp_b_in, m_gmlp_ln_g, m_gmlp_ln_b, m_gmlp_w_s, m_gmlp_b_s, m_gmlp_w_out, m_w_ada_kv, m_b_ada_kv, m_w_kv, m_attn_w_q, m_attn_rel_bias, m_attn_w_o, v_w_ada, v_b_ada, v_ln_g, v_ln_b, v_ffn_gu, v_ffn_down, v_gmlp_w_in, v_gmlp_b_in, v_gmlp_ln_g, v_gmlp_ln_b, v_gmlp_w_s, v_gmlp_b_s, v_gmlp_w_out, v_w_ada_kv, v_b_ada_kv, v_w_kv, v_attn_w_q, v_attn_rel_bias, v_attn_w_o):
    weights = dict(w_ada=w_ada, b_ada=b_ada, ln_g=ln_g, ln_b=ln_b, ffn_gu=ffn_gu, ffn_down=ffn_down,
                   gmlp_w_in=gmlp_w_in, gmlp_b_in=gmlp_b_in, gmlp_ln_g=gmlp_ln_g, gmlp_ln_b=gmlp_ln_b,
                   gmlp_w_s=gmlp_w_s, gmlp_b_s=gmlp_b_s, gmlp_w_out=gmlp_w_out, w_ada_kv=w_ada_kv,
                   b_ada_kv=b_ada_kv, w_kv=w_kv, attn_w_q=attn_w_q, attn_rel_bias=attn_rel_bias,
                   attn_w_o=attn_w_o)
    mom1 = dict(w_ada=m_w_ada, b_ada=m_b_ada, ln_g=m_ln_g, ln_b=m_ln_b, ffn_gu=m_ffn_gu, ffn_down=m_ffn_down,
                gmlp_w_in=m_gmlp_w_in, gmlp_b_in=m_gmlp_b_in, gmlp_ln_g=m_gmlp_ln_g, gmlp_ln_b=m_gmlp_ln_b,
                gmlp_w_s=m_gmlp_w_s, gmlp_b_s=m_gmlp_b_s, gmlp_w_out=m_gmlp_w_out, w_ada_kv=m_w_ada_kv,
                b_ada_kv=m_b_ada_kv, w_kv=m_w_kv, attn_w_q=m_attn_w_q, attn_rel_bias=m_attn_rel_bias,
                attn_w_o=m_attn_w_o)
    mom2 = dict(w_ada=v_w_ada, b_ada=v_b_ada, ln_g=v_ln_g, ln_b=v_ln_b, ffn_gu=v_ffn_gu, ffn_down=v_ffn_down,
                gmlp_w_in=v_gmlp_w_in, gmlp_b_in=v_gmlp_b_in, gmlp_ln_g=v_gmlp_ln_g, gmlp_ln_b=v_gmlp_ln_b,
                gmlp_w_s=v_gmlp_w_s, gmlp_b_s=v_gmlp_b_s, gmlp_w_out=v_gmlp_w_out, w_ada_kv=v_w_ada_kv,
                b_ada_kv=v_b_ada_kv, w_kv=v_w_kv, attn_w_q=v_attn_w_q, attn_rel_bias=v_attn_rel_bias,
                attn_w_o=v_attn_w_o)
    order = list(weights)

    x = x[0]
    target = loss_target[0]
    t, d = x.shape
    n_mod = w_ada.shape[-1] * N_DEV // d
    mod_w = w_ada.shape[-1]
    kv_w = w_ada_kv.shape[-1]
    n_b = DEPTH - N_A
    me = _linear(*_position())

    l2 = DEPTH * 2
    big = dict(
        ffn_gu=ffn_gu.reshape((l2,) + ffn_gu.shape[2:]),
        ffn_down=ffn_down.reshape((l2,) + ffn_down.shape[2:]),
        gmlp_w_in=gmlp_w_in, gmlp_w_out=gmlp_w_out, w_kv=w_kv[None],
        attn_w_q=attn_w_q, attn_w_o=attn_w_o)
    big_names = list(big)
    core = lax.axis_index("c").astype(jnp.int32).reshape(1)
    chip = (2 * lax.axis_index("x") + lax.axis_index("y")).astype(jnp.int32).reshape(1)

    fwd_groups = [
        {"ffn_gu": (0, 1), "ffn_down": (0, 1)},
        {"gmlp_w_in": (0, 1), "gmlp_w_out": (0, 1)},
        {"ffn_gu": (1, 1), "ffn_down": (1, 1)},
        {"ffn_gu": (2, 1), "ffn_down": (2, 1)},
        {"gmlp_w_in": (1, 1), "gmlp_w_out": (1, 1)},
        {"ffn_gu": (3, 1), "ffn_down": (3, 1), "w_kv": (0, 1)},
        {"ffn_gu": (4, 1), "ffn_down": (4, 1)},
        {"attn_w_q": (0, 1), "attn_w_o": (0, 1)},
        {"ffn_gu": (5, 1), "ffn_down": (5, 1)},
        {"ffn_gu": (6, 1), "ffn_down": (6, 1)},
        {"attn_w_q": (1, 1), "attn_w_o": (1, 1)},
        {"ffn_gu": (7, 1), "ffn_down": (7, 1)},
    ]
    bwd_groups = []
    for l in range(DEPTH):
        g = {"ffn_gu": (2 * l, 2), "ffn_down": (2 * l, 2)}
        if l < N_A:
            g.update({"gmlp_w_in": (l, 1), "gmlp_w_out": (l, 1)})
        else:
            g.update({"attn_w_q": (l - N_A, 1), "attn_w_o": (l - N_A, 1)})
        if l == N_A - 1:
            g["w_kv"] = (0, 1)
        bwd_groups.append(g)

    def slot_of(groups, name, slot):
        for gi, g in enumerate(groups):
            if name in g and g[name][0] <= slot < g[name][0] + g[name][1]:
                return gi, slot - g[name][0]
        raise KeyError((name, slot))

    def start_group(gi, after=None):
        lands = []
        for name, (s0, cnt) in fwd_groups[gi].items():
            shard = big[name][s0:s0 + cnt].astype(BF16)
            land = lax.empty((N_DEV,) + shard.shape, BF16)
            lands.append(lax.dynamic_update_slice(land, shard[None], (me,) + (0,) * shard.ndim))
        return _gather_ici_start(f"gather_ici_start_{gi}", lands, after)

    gathered = [None] * len(fwd_groups)

    passing = {}

    def pass_on_early(gi, after):
        send_sems, recv_sems, lands, _ = flights[gi]
        lands = _gather_ici_wait(f"gather_ici_wait_{gi}", lands, send_sems, recv_sems, after)
        passing[gi] = _gather_ici_start(f"gather_d2d_start_{gi}", lands, to_sibling=True)
        return passing[gi][3]

    def land_group(gi, after):
        if gi in passing:
            send_sems, recv_sems, lands, _ = passing.pop(gi)
            lands = _gather_ici_wait(f"gather_d2d_wait_{gi}", lands, send_sems, recv_sems, after, to_sibling=True)
        else:
            send_sems, recv_sems, lands, _ = flights[gi]
            lands = _gather_d2d(_gather_ici_wait(f"gather_ici_wait_{gi}", lands, send_sems, recv_sems, after))
        gathered[gi] = dict(zip(fwd_groups[gi], lands))

    def weight(name, slot):
        gi, local = slot_of(fwd_groups, name, slot)
        return gathered[gi][name], local

    swapped = ("ffn_gu",)

    def grad_shape(name):
        s = big[name].shape[1:]
        return s[:-2] + (s[-1], s[-2]) if name in swapped else s

    partial = [{name: lax.empty((N_DEV, cnt) + grad_shape(name), BF16) for name, (_, cnt) in g.items()}
               for g in bwd_groups]

    c_all = _all_gather_small(_pack([c]))
    c_all = _unpack(c_all, [(d,)], lead=(N_DEV,))[0]
    c4 = _as4(c_all)
    mod_part = _matmul("ada_fwd", c4, w_ada[:, None], (DEPTH, 1, N_DEV, mod_w), F32, a_silu=True)
    kv_part = _matmul("ada_kv_fwd", c4, _as4(w_ada_kv), (1, 1, N_DEV, kv_w), F32, a_silu=True)
    small_shapes = [mod_part.shape, kv_part.shape, ln_g.shape, ln_b.shape, gmlp_b_in.shape,
                    gmlp_ln_g.shape, gmlp_ln_b.shape, attn_rel_bias.shape]
    small = _all_gather_small(_pack([mod_part, kv_part, ln_g, ln_b, gmlp_b_in, gmlp_ln_g, gmlp_ln_b,
                                     attn_rel_bias]))
    flights = [start_group(0, after=small)]
    flights += [start_group(gi, after=flights[0][3]) for gi in range(1, len(fwd_groups))]
    start_token = sum(f[3][0, 0] for f in flights)
    (mod_g, kvm_g, ln_g_g, ln_b_g, b_in_g, gln_g_g, gln_b_g, rel_g) = _unpack(small, small_shapes, lead=(N_DEV,))
    mod_mine = lax.dynamic_index_in_dim(mod_g[:, :, 0], me, axis=2, keepdims=False)
    mod = _unshard_last(mod_mine) + b_ada
    mod = mod.reshape(DEPTH, n_mod, 1, d)
    kvm_mine = lax.dynamic_index_in_dim(kvm_g[:, 0, 0], me, axis=1, keepdims=False)
    mkv = (_unshard_last(kvm_mine) + b_ada_kv).reshape(2, 1, d)
    ln_g_f = _unshard_last(ln_g_g)
    ln_b_f = _unshard_last(ln_b_g)
    half = N_DEV // 2
    b_in_f = jnp.transpose(b_in_g, (1, 0, 2))[:, :, None, :]
    gln_g_f = _unshard_last(gln_g_g).reshape(N_A, half, 1, -1)
    gln_b_f = _unshard_last(gln_b_g).reshape(N_A, half, 1, -1)
    rel_f = _unshard_last(rel_g)

    def shard_act(a):
        return a.reshape(a.shape[0], a.shape[2], a.shape[3])

    def grad_into(name, slot, mm):
        gi, local = slot_of(bwd_groups, name, slot)
        partial[gi][name] = mm(partial[gi][name], local)

    def ffn_fwd(h, lw, after=None):
        w_gu, l_gu = weight("ffn_gu", lw)
        w_dn, l_dn = weight("ffn_down", lw)
        gu, a = _ffn_up_fwd(h, w_gu, l_gu, after)
        y = _matmul("ffn_down_fwd", a[:, None], w_dn, (1, 1, t, d), F32, lb=l_dn, b_merge=2, reduce=True)
        return y[0, 0], (gu, a)

    def ffn_bwd(dy, h, saved, lw):
        gu, a = saved
        w_gu, l_gu = weight("ffn_gu", lw)
        w_dn, l_dn = weight("ffn_down", lw)
        dgu = _ffn_down_bwd_a(dy, w_dn, l_dn, gu).reshape((N_DEV,) + gu.shape[2:])
        grad_into("ffn_down", lw, lambda buf, lo: _matmul(
            "ffn_down_bwd_w", a[:, None], _as4(dy), buf.shape, BF16, ta=True, lo=lo, out_merge=2, out_buf=buf))
        dh = _matmul("ffn_gu_bwd_a", dgu[:, None], w_gu, (1, 1, t, d), F32, lb=l_gu, tb=True, reduce=True)
        grad_into("ffn_gu", lw, lambda buf, lo: _matmul(
            "ffn_gu_bwd_w", dgu[:, None], _as4(h), buf.shape, BF16, ta=True, lo=lo, out_buf=buf))
        return dh[0, 0], {}

    def gmlp_params(l):
        return (b_in_f[l], gln_g_f[l], gln_b_f[l], gmlp_w_s[l], gmlp_b_s[l][:, :, None])

    def gmlp_fwd(h, l, after=None):
        w_in, l_in = weight("gmlp_w_in", l)
        w_out, l_out = weight("gmlp_w_out", l)
        n = w_in.shape[-1]
        zpre = _matmul("gmlp_in_fwd", _as4(h), w_in, (N_DEV, 1, t, n), F32, lb=l_in, after=after)
        gated = _gmlp_mid_fwd(shard_act(zpre), *gmlp_params(l))
        y = _matmul("gmlp_out_fwd", gated[:, None], w_out, (1, 1, t, d), F32, lb=l_out, b_merge=2, reduce=True)
        return y[0, 0], (zpre, gated)

    def gmlp_bwd(dy, h, saved, l):
        zpre, gated = saved
        w_in, l_in = weight("gmlp_w_in", l)
        w_out, l_out = weight("gmlp_w_out", l)
        n = w_in.shape[-1]
        dgated = _matmul("gmlp_out_bwd_a", _as4(dy), w_out, (half, 1, t, n), F32, lb=l_out, b_merge=2, tb=True)
        grad_into("gmlp_w_out", l, lambda buf, lo: _matmul(
            "gmlp_out_bwd_w", gated[:, None], _as4(dy), buf.shape, BF16, ta=True, lo=lo, out_merge=2, out_buf=buf))
        dz, dws, dbs, dlng, dlnb, dbin = _gmlp_mid_bwd(shard_act(zpre), shard_act(dgated), *gmlp_params(l))
        dh = _matmul("gmlp_in_bwd_a", dz[:, None], w_in, (1, 1, t, d), F32, lb=l_in, tb=True, reduce=True)
        grad_into("gmlp_w_in", l, lambda buf, lo: _matmul(
            "gmlp_in_bwd_w", _as4(h), dz[:, None], buf.shape, BF16, ta=True, lo=lo, out_buf=buf))
        small_grads = dict(gmlp_w_s=dws, gmlp_b_s=dbs[:, :, 0], gmlp_ln_g=dlng.reshape(-1),
                           gmlp_ln_b=dlnb.reshape(-1), gmlp_b_in=dbin.reshape(-1))
        return dh[0, 0], small_grads

    def attn_fwd(h, j, kvp, after=None):
        rel_vec = _rel_vector(rel_f[j])
        w_q, l_q = weight("attn_w_q", j)
        w_o, l_o = weight("attn_w_o", j)
        q = _matmul("attn_q_fwd", _as4(h), w_q, (1, 1, t, d), BF16, lb=l_q, b_merge=N_DEV, reduce=True,
                    after=after)[0, 0]
        o = _attn_fwd(q, kvp, rel_vec)
        y = _matmul("attn_o_fwd", _as4(o), w_o, (1, 1, t, d), F32, lb=l_o, b_merge=N_DEV, reduce=True)
        return y[0, 0], (q, o, rel_vec)

    def attn_bwd(dy, h, saved, j, kvp, dkv_acc):
        q, o, rel_vec = saved
        w_q, l_q = weight("attn_w_q", j)
        w_o, l_o = weight("attn_w_o", j)
        do = _matmul("attn_o_bwd_a", _as4(dy), w_o, (1, 1, t, d), BF16, lb=l_o, b_merge=N_DEV, tb=True)[0, 0]
        grad_into("attn_w_o", j, lambda buf, lo: _matmul(
            "attn_o_bwd_w", _as4(o), _as4(dy), buf.shape, BF16, ta=True, lo=lo, out_merge=N_DEV, out_buf=buf))
        dq, dk, dv, dsc = _attn_bwd(q, do, kvp, rel_vec, *dkv_acc)
        drel = _rel_bias_grad(_skew_diagonals(dsc))
        dh = _matmul("attn_q_bwd_a", _as4(dq), w_q, (1, 1, t, d), F32, lb=l_q, b_merge=N_DEV, tb=True)
        grad_into("attn_w_q", j, lambda buf, lo: _matmul(
            "attn_q_bwd_w", _as4(h), _as4(dq), buf.shape, BF16, ta=True, lo=lo, out_merge=N_DEV, out_buf=buf))
        return dh[0, 0], dict(attn_rel_bias=drel, dkv=(dk, dv))

    tape = []
    kvp = None
    kv_tape = None
    first_use = {(l, i): 3 * l + i for l in range(DEPTH) for i in range(3)}
    PASS_ON_EARLY_FROM = 6
    h = _modulate(x, mod[0, 1], mod[0, 0] + start_token)
    for l in range(DEPTH):
        for i in range(3):
            if (l, i) in first_use:
                land_group(first_use[l, i], x)
            nl, ni = (l, i + 1) if i < 2 else (l + 1, 0)
            ahead = first_use.get((nl, ni), 0)
            started = pass_on_early(ahead, x) if ahead >= PASS_ON_EARLY_FROM else None
            scl, gate = mod[l, 3 * i + 1], mod[l, 3 * i + 2]
            wgt = 1.0 if i == 1 else 0.5
            gw = wgt * (1.0 + gate)
            if i != 1:
                y, saved = ffn_fwd(h, 2 * l + i // 2, started)
            elif l < N_A:
                y, saved = gmlp_fwd(h, l, started)
            else:
                y, saved = attn_fwd(h, l - N_A, kvp, started)
            readers = [(mod[nl, 3 * ni + 1], mod[nl, 3 * ni])] if nl < DEPTH else []
            shared_kv = (l, i) == (N_A - 1, 2)
            if shared_kv:
                readers.append((mkv[1], mkv[0]))
            outs = _ln_res_fwd(x, y, gw, ln_g_f[l, i][None], ln_b_f[l, i][None], readers)
            tape.append((x, h, y, gw, scl, saved))
            x = outs[0]
            h = outs[1] if nl < DEPTH else None
            if shared_kv:
                hkv = outs[-1]
                w_kvg, l_kv = weight("w_kv", 0)
                n = w_kvg.shape[-1]
                kv = _matmul("kv_fwd", _as4(hkv), w_kvg, (N_DEV, 1, t, n), BF16, lb=l_kv)
                kvp = jnp.pad(shard_act(kv), ((0, 0), (LEFT_PAD, 0), (0, 0)))
                kv_tape = hkv

    loss_part, dx = _loss_head(x, target)
    loss = lax.psum(loss_part[0, 0], MESH_AXES)

    d_mod = [[None] * n_mod for _ in range(DEPTH)]
    d_ln_g = [[None] * 3 for _ in range(DEPTH)]
    d_ln_b = [[None] * 3 for _ in range(DEPTH)]
    small_grads = {k: [None] * N_A for k in ("gmlp_w_s", "gmlp_b_s", "gmlp_ln_g", "gmlp_ln_b", "gmlp_b_in")}
    d_rel = [None] * n_b
    dkv_acc = ()
    d_mkv = None
    reductions = [None] * DEPTH
    sent_token = None
    readers = []
    for l in reversed(range(DEPTH)):
        if l == N_A - 1:
            hkv = kv_tape
            w_kvg, l_kv = weight("w_kv", 0)
            dkv = jnp.concatenate(dkv_acc)[:, LEFT_PAD:, :].astype(BF16)[:, None]
            dhkv = _matmul("kv_bwd_a", dkv, w_kvg, (1, 1, t, d), F32, lb=l_kv, tb=True, reduce=True)[0, 0]
            grad_into("w_kv", 0, lambda buf, lo: _matmul(
                "kv_bwd_w", _as4(hkv), dkv, buf.shape, BF16, ta=True, lo=lo, out_buf=buf))
            readers.append((dhkv, mkv[1], None))
        for i in reversed(range(3)):
            x_in, h, y, gw, scl, saved = tape[3 * l + i]
            wgt = 1.0 if i == 1 else 0.5
            if sent_token is not None:
                gw = gw + sent_token
                sent_token = None
            res = _ln_res_bwd(x_in, y, gw, ln_g_f[l, i][None], ln_b_f[l, i][None], dx,
                              [(r[0], r[1]) for r in readers])
            dx_res, dy, dgw, dg, db = res[:5]
            for k, (_, _, slot) in enumerate(readers):
                dscl_k, dshift_k = res[5 + 2 * k][0], res[6 + 2 * k][0]
                if slot is None:
                    d_mkv = jnp.concatenate([dshift_k, dscl_k])
                else:
                    d_mod[slot[0]][slot[1]], d_mod[slot[0]][slot[1] + 1] = dshift_k, dscl_k
            d_ln_g[l][i], d_ln_b[l][i] = dg[0], db[0]
            if i != 1:
                dh, extra = ffn_bwd(dy, h, saved, 2 * l + i // 2)
            elif l < N_A:
                dh, extra = gmlp_bwd(dy, h, saved, l)
                for k, g in extra.items():
                    small_grads[k][l] = g
            else:
                dh, extra = attn_bwd(dy, h, saved, l - N_A, kvp, dkv_acc)
                d_rel[l - N_A] = extra["attn_rel_bias"]
                dkv_acc = extra["dkv"]
            d_mod[l][3 * i + 2] = wgt * dgw[0]
            dx = dx_res
            readers = [(dh, scl, (l, 3 * i))]
        if l > 0:
            srcs = [partial[l][k] for k in bwd_groups[l]]
            lands = [lax.empty((N_DEV - 1,) + s.shape[1:], BF16) for s in srcs]
            reductions[l] = _partials_send_start(f"partials_send_start_{l}", srcs, lands, True)
            sent_token = reductions[l][4][0, 0]
    (dh, scl, _), = readers
    dx, dscl, dshift = _mod_bwd(dx, dh, tape[0][0], scl)
    d_mod[0][0], d_mod[0][1] = dshift[0], dscl[0]
    grad_x = dx[None]

    d_mod_arr = jnp.stack([jnp.concatenate(r) for r in d_mod])
    small_part = dict(
        b_ada=d_mod_arr, b_ada_kv=d_mkv,
        ln_g=jnp.stack([jnp.stack(r) for r in d_ln_g]), ln_b=jnp.stack([jnp.stack(r) for r in d_ln_b]),
        gmlp_b_in=jnp.stack(small_grads["gmlp_b_in"]), gmlp_ln_g=jnp.stack(small_grads["gmlp_ln_g"]),
        gmlp_ln_b=jnp.stack(small_grads["gmlp_ln_b"]), gmlp_w_s=jnp.stack(small_grads["gmlp_w_s"]),
        gmlp_b_s=jnp.stack(small_grads["gmlp_b_s"]), attn_rel_bias=jnp.stack(d_rel))
    small_names = list(small_part)
    sp_shapes = [small_part[k].shape for k in small_names]
    sp_all = _all_gather_small(_pack([small_part[k] for k in small_names]),
                               after=[partial[0][k] for k in bwd_groups[0]])

    from_sibling = _partials_d2d([partial[0][k] for k in bwd_groups[0]])
    sums = [_chip_sum(partial[0][k], r1, core) for k, r1 in zip(bwd_groups[0], from_sibling)]
    lands = [lax.empty((len(_ALL_CHIPS) - 1,) + s.shape[1:], BF16) for s in sums]
    reductions[0] = _partials_send_start("partials_send_start_0", sums, lands, False, after=sp_all)
    sent_token = reductions[0][4][0, 0]
    c4 = c4 + sent_token

    sp_sum = _sum_parts(sp_all)
    full_grads = dict(zip(small_names, _unpack(sp_sum, sp_shapes)))
    per_dev = dict(zip(small_names, _unpack(sp_all, sp_shapes, lead=(N_DEV,))))

    def my_cols(a, width):
        return lax.dynamic_slice_in_dim(a, me * width, width, axis=a.ndim - 1)

    grads = {}
    grads["b_ada"] = full_grads["b_ada"]
    grads["b_ada_kv"] = full_grads["b_ada_kv"]
    grads["gmlp_w_s"] = full_grads["gmlp_w_s"]
    grads["gmlp_b_s"] = full_grads["gmlp_b_s"]
    for k in ("ln_g", "ln_b", "gmlp_b_in", "gmlp_ln_g", "gmlp_ln_b", "attn_rel_bias"):
        grads[k] = my_cols(full_grads[k], weights[k].shape[-1])

    dmod_cols = jnp.transpose(my_cols(per_dev["b_ada"], mod_w), (1, 0, 2))[:, None]
    grads["w_ada"] = _matmul("ada_bwd_w", c4, dmod_cols, (DEPTH, 1, d, mod_w), F32, ta=True,
                             a_silu=True)[:, 0]
    dkv_cols = my_cols(per_dev["b_ada_kv"], kv_w)[None, None]
    grads["w_ada_kv"] = _matmul("ada_kv_bwd_w", c4, dkv_cols, (1, 1, d, kv_w), F32, ta=True,
                                a_silu=True)[0, 0]

    delta, new_m, new_v = {}, {}, {}
    first = jnp.zeros((1,), jnp.int32)

    def flat2(a, cols):
        return a.reshape(-1, cols)

    done = None
    for k in ("w_ada", "w_ada_kv"):
        w = weights[k]
        cols = w.shape[-1]
        res = _adamw(grads[k].reshape(1, -1, cols), first, None, flat2(w, cols), flat2(mom1[k], cols),
                     flat2(mom2[k], cols), after=done)
        grads[k], delta[k], new_m[k], new_v[k] = (a.reshape(w.shape) for a in res)
        done = res[0][:SUBLANES, :LANES]

    tiny = [k for k in order if k not in delta and k not in big_names]
    tiny_shapes = [weights[k].shape for k in tiny]
    tiny_out = _adamw((_pack([grads[k] for k in tiny]) + sent_token)[None], first, None,
                      _pack([weights[k] for k in tiny]), _pack([mom1[k] for k in tiny]),
                      _pack([mom2[k] for k in tiny]), after=done)
    for dst, arr in zip((grads, delta, new_m, new_v), tiny_out):
        for k, val in zip(tiny, _unpack(arr, tiny_shapes)):
            dst[k] = val

    def opt_view(k, a):
        a = jnp.swapaxes(a, -1, -2) if k in swapped else a
        return a.reshape(-1, a.shape[-1])

    def opt_unview(k, a):
        s = weights[k].shape
        return jnp.swapaxes(a.reshape(s[:-2] + (s[-1], s[-2])), -1, -2) if k in swapped else a.reshape(s)

    bufs = {k: [lax.empty(opt_view(k, weights[k]).shape, F32) for _ in range(4)] for k in big_names}
    done = tiny_out[0]
    me_idx = me.astype(jnp.int32).reshape(1)
    for l in reversed(range(DEPTH)):
        send_sems, recv_sems, srcs, lands, _ = reductions[l]
        srcs, lands = _partials_send_wait(f"partials_send_wait_{l}", srcs, lands, send_sems, recv_sems, l > 0, done)
        for k, own, got in zip(bwd_groups[l], srcs, lands):
            cols = own.shape[-1]
            slot_rows = int(np.prod(own.shape[2:-1]))
            bufs[k] = _adamw(own.reshape(own.shape[0], -1, cols), me_idx if l > 0 else chip,
                             got.reshape(got.shape[0], -1, cols),
                             opt_view(k, weights[k]), opt_view(k, mom1[k]), opt_view(k, mom2[k]),
                             row0=bwd_groups[l][k][0] * slot_rows, bufs=bufs[k], after=done)
            done = bufs[k][0][:SUBLANES, :LANES]
    for k in big_names:
        grads[k], delta[k], new_m[k], new_v[k] = (opt_unview(k, b) for b in bufs[k])

    return (loss, grad_x, *[grads[k] for k in order], *[delta[k] for k in order],
            *[new_m[k] for k in order], *[new_v[k] for k in order])
```

```python
import numpy as np
import jax
import jax.numpy as jnp
from jax import lax
from jax.experimental import pallas as pl
from jax.experimental.pallas import tpu as pltpu

F32 = jnp.float32
BF16 = jnp.bfloat16
MESH_AXES = ("x", "y", "c")
N_DEV = 8
MESH_ID = pl.DeviceIdType.MESH

DEPTH = 4
N_A = 2
CHUNK = 64
N_HEADS = 16
LEFT_CHUNKS = 8
BAND = (LEFT_CHUNKS + 1) * CHUNK
LEFT_PAD = LEFT_CHUNKS * CHUNK
MAX_REL = 4 * CHUNK
N_REL = (CHUNK - 1) + MAX_REL + 1
GMLP_WINDOW = 128
GMLP_GROUPS = 8
ALPHA = (2.0 * DEPTH) ** 0.25
LN_EPS = 1e-5
ADAM_LR = 0.001
ADAM_B1 = 0.9
ADAM_B2 = 0.999
ADAM_EPS = 1e-08
ADAM_WD = 0.01
ADAM_STEP = 10

V7X_VMEM_BYTES = 64 * 1024 * 1024
VMEM_LIMIT = V7X_VMEM_BYTES - 8 * 1024 * 1024
LANES = 128
SUBLANES = 8
MM_BLOCK = 2048
BIG_ROW_BLOCK = 1024
ROW_BLOCK = 512
OPT_ROW_BLOCK = 256

_ANY = pl.BlockSpec(memory_space=pl.ANY)
_VMEM = pl.BlockSpec(memory_space=pltpu.VMEM)


def _params(sem=None):
    return pltpu.CompilerParams(dimension_semantics=sem, vmem_limit_bytes=VMEM_LIMIT)


def _row_block(rows, target):
    for d in range(min(rows, target), 0, -1):
        if rows % d == 0 and (d % SUBLANES == 0 or d == rows):
            return d
    return rows


def _matmul(name, a, b, out_shape4, out_dtype, *, la=0, lb=0, lo=0, ta=False, tb=False,
            reduce=False, b_merge=1, out_merge=1, out_buf=None, a_silu=False, after=None):
    ja_n, _, a_r, a_c = a.shape
    jb_n, _, b_r, b_c = b.shape
    jo_n, _, o_r, o_c = out_shape4
    m_tot = a_c if ta else a_r
    k_a = a_r if ta else a_c
    b_rows = b_merge * b_r
    k_c = b_c if tb else b_rows
    n = b_rows if tb else b_c
    n_chunks = (jb_n // b_merge) if reduce else 1
    natural_k = reduce and ja_n == 1
    assert n == o_c, (name, n, o_c)
    assert k_a ==(k_c * n_chunks if natural_k else k_c), (name, k_a, k_c, n_chunks)
    bk = k_c if (k_c <= MM_BLOCK or (b_merge > 1 and not tb)) else MM_BLOCK
    assert k_c % bk == 0
    nkk = k_c // bk
    kg = 2 if (reduce and ja_n > 1 and nkk == 1 and not ta and n_chunks % 2 == 0) else 1
    nk = n_chunks * nkk // kg
    m_out = out_merge * o_r
    assert m_tot == m_out, (name, m_tot, m_out)
    bm = m_tot if (m_tot <= MM_BLOCK or out_merge > 1) else MM_BLOCK
    assert m_tot % bm == 0
    jo_blocks = jo_n // out_merge

    def a_index(j, m, k):
        kj, kk = k // nkk, k % nkk
        ja = 0 if ja_n == 1 else (kj if reduce else j)
        ke = kk + kj * nkk if natural_k else kk
        return (ja, la, ke, m) if ta else (ja, la, m, ke)

    def b_index(j, m, k):
        kj, kk = k // nkk, k % nkk
        jb = 0 if jb_n == b_merge else (kj if reduce else j)
        return (jb, lb, 0, kk) if tb else (jb, lb, kk, 0)

    def o_index(j, m, k):
        return (j, lo, 0, 0) if out_merge > 1 else (j, lo, m, 0)

    a_block = (None, None, bk, bm) if ta else (None if kg == 1 else kg, None, bm, bk)
    if b_merge > 1:
        b_block = (kg * b_merge, None, b_r, bk if tb else n)
    else:
        b_block = (None if kg == 1 else kg, None) + ((n, bk) if tb else (bk, n))
    o_block = (out_merge, None, o_r, n) if out_merge > 1 else (None, None, bm, n)
    dims = (((0 if ta else 1,), (1 if tb else 0,)), ((), ()))

    in_place = nk > 1 and out_dtype == F32 and out_merge == 1
    use_acc = nk > 1 and not in_place

    def body(a_ref, b_ref, *rest):
        o_ref = rest[-2] if use_acc else rest[-1]
        k = pl.program_id(2)
        av = a_ref[...]
        if a_silu:
            af = av.astype(F32)
            av = af * jax.nn.sigmoid(af)
        bv = b_ref[...]
        if kg > 1:
            bv = bv.reshape(kg, -1, bv.shape[-1])
            prod = sum(lax.dot_general(av[g].astype(BF16), bv[g].astype(BF16), dims, preferred_element_type=F32)
                       for g in range(kg))
        else:
            if b_merge > 1:
                bv = bv.reshape(b_rows, bv.shape[-1])
            prod = lax.dot_general(av.astype(BF16), bv.astype(BF16), dims, preferred_element_type=F32)

        def emit(val):
            val = val.astype(out_dtype)
            o_ref[...] = val.reshape(out_merge, o_r, n) if out_merge > 1 else val

        if nk == 1:
            emit(prod)
            return
        acc_ref = o_ref if in_place else rest[-1]

        @pl.when(k == 0)
        def _():
            acc_ref[...] = prod

        @pl.when(k > 0)
        def _():
            acc_ref[...] += prod

        if use_acc:
            @pl.when(k == nk - 1)
            def _():
                emit(acc_ref[...])

    in_specs = [pl.BlockSpec(a_block, a_index), pl.BlockSpec(b_block, b_index)]
    operands = [a, b]
    aliases = {}
    if out_buf is not None:
        assert out_buf.shape == tuple(out_shape4) and out_buf.dtype == out_dtype
        in_specs.append(_ANY)
        operands.append(out_buf)
        aliases = {2: 0}
    if after is not None:
        in_specs.append(_ANY)
        operands.append(after)
    return pl.pallas_call(
        body, name=name,
        grid=(jo_blocks, m_tot // bm, nk),
        in_specs=in_specs,
        out_specs=pl.BlockSpec(o_block, o_index),
        out_shape=jax.ShapeDtypeStruct(tuple(out_shape4), out_dtype),
        scratch_shapes=[pltpu.VMEM((bm, n), F32)] if use_acc else [],
        input_output_aliases=aliases,
        compiler_params=_params(("parallel", "parallel", "arbitrary")),
    )(*operands)


def _as4(a):
    return a.reshape((1,) * (4 - a.ndim) + a.shape)


def _row_call(name, body, ins, outs, t, *, acc_outs=()):
    bt = _row_block(t, ROW_BLOCK)

    def spec(arr, tiled):
        if tiled:
            return pl.BlockSpec((bt,) + tuple(arr.shape[1:]), lambda i: (i,) + (0,) * (arr.ndim - 1))
        return pl.BlockSpec(tuple(arr.shape), lambda i: (0,) * arr.ndim)

    return pl.pallas_call(
        body, name=name, grid=(t // bt,),
        in_specs=[spec(a, tl) for a, tl in ins],
        out_specs=[spec(o, tl) for o, tl in outs],
        out_shape=[jax.ShapeDtypeStruct(o.shape, o.dtype) for o, _ in outs],
        compiler_params=_params(("arbitrary",) if acc_outs else ("parallel",)),
    )(*[a for a, _ in ins])


def _sds(shape, dtype):
    return jax.ShapeDtypeStruct(tuple(shape), dtype)


def _modulate(x, scl, shift):
    t, d = x.shape

    def body(x_ref, s_ref, b_ref, h_ref):
        h_ref[...] = (x_ref[...] * (1.0 + s_ref[...]) + b_ref[...]).astype(BF16)

    return _row_call("modulate", body, [(x, True), (scl, False), (shift, False)],
                     [(_sds((t, d), BF16), True)], t)[0]


def _ln_stats(r):
    mu = jnp.mean(r, axis=-1, keepdims=True)
    rc = r - mu
    var = jnp.mean(rc * rc, axis=-1, keepdims=True)
    rstd = lax.rsqrt(var + LN_EPS)
    return rc * rstd, rstd


def _ln_res_fwd(x, y, gw, g, b, mods=()):
    t, d = x.shape
    n_mod = len(mods)

    def body(x_ref, y_ref, gw_ref, g_ref, b_ref, *rest):
        mod_refs, o_ref, h_refs = rest[:2 * n_mod], rest[2 * n_mod], rest[2 * n_mod + 1:]
        r = ALPHA * x_ref[...] + gw_ref[...] * y_ref[...]
        xhat, _ = _ln_stats(r)
        xn = xhat * g_ref[...] + b_ref[...]
        o_ref[...] = xn
        for k in range(n_mod):
            h_refs[k][...] = (xn * (1.0 + mod_refs[2 * k][...]) + mod_refs[2 * k + 1][...]).astype(BF16)

    vecs = [(v, False) for pair in mods for v in pair]
    return _row_call("ln_res_fwd", body,
                     [(x, True), (y, True), (gw, False), (g, False), (b, False)] + vecs,
                     [(_sds((t, d), F32), True)] + [(_sds((t, d), BF16), True)] * n_mod, t)


def _ln_res_bwd(x, y, gw, g, b, dx_base, pairs=()):
    t, d = x.shape
    n_pair = len(pairs)

    def body(x_ref, y_ref, gw_ref, g_ref, b_ref, dxb_ref, *rest):
        pair_refs, outs = rest[:2 * n_pair], rest[2 * n_pair:]
        dx_ref, dy_ref = outs[0], outs[1]
        sums = outs[2:]

        @pl.when(pl.program_id(0) == 0)
        def _():
            for r in sums:
                r[...] = jnp.zeros_like(r)

        yv = y_ref[...]
        gwv = gw_ref[...]
        gv = g_ref[...]
        xhat, rstd = _ln_stats(ALPHA * x_ref[...] + gwv * yv)
        dxn = dxb_ref[...]
        if n_pair:
            xn = xhat * gv + b_ref[...]
            for k in range(n_pair):
                dh = pair_refs[2 * k][...]
                dxn = dxn + dh * (1.0 + pair_refs[2 * k + 1][...])
                sums[3 + 2 * k][...] += jnp.sum(dh * xn, axis=0, keepdims=True)
                sums[4 + 2 * k][...] += jnp.sum(dh, axis=0, keepdims=True)
        dxh = dxn * gv
        m1 = jnp.mean(dxh, axis=-1, keepdims=True)
        m2 = jnp.mean(dxh * xhat, axis=-1, keepdims=True)
        dr = rstd * (dxh - m1 - xhat * m2)
        dx_ref[...] = ALPHA * dr
        dy_ref[...] = (gwv * dr).astype(BF16)
        sums[0][...] += jnp.sum(dr * yv, axis=0, keepdims=True)
        sums[1][...] += jnp.sum(dxn * xhat, axis=0, keepdims=True)
        sums[2][...] += jnp.sum(dxn, axis=0, keepdims=True)

    vec = _sds((1, d), F32)
    n_sum = 3 + 2 * n_pair
    ins = [(x, True), (y, True), (gw, False), (g, False), (b, False), (dx_base, True)]
    for dh, scl in pairs:
        ins += [(dh, True), (scl, False)]
    return _row_call("ln_res_bwd", body, ins,
                     [(_sds((t, d), F32), True), (_sds((t, d), BF16), True)] + [(vec, False)] * n_sum, t,
                     acc_outs=tuple(range(2, 2 + n_sum)))


def _mod_bwd(dx_res, dh, x, scl):
    t, d = x.shape

    def body(dxr_ref, dh_ref, x_ref, s_ref, dx_ref, ds_ref, db_ref):
        @pl.when(pl.program_id(0) == 0)
        def _():
            ds_ref[...] = jnp.zeros_like(ds_ref)
            db_ref[...] = jnp.zeros_like(db_ref)

        dh = dh_ref[...]
        dx_ref[...] = dxr_ref[...] + dh * (1.0 + s_ref[...])
        ds_ref[...] += jnp.sum(dh * x_ref[...], axis=0, keepdims=True)
        db_ref[...] += jnp.sum(dh, axis=0, keepdims=True)

    vec = _sds((1, d), F32)
    return _row_call("mod_bwd", body, [(dx_res, True), (dh, True), (x, True), (scl, False)],
                     [(_sds((t, d), F32), True), (vec, False), (vec, False)], t, acc_outs=(1, 2))


def _loss_head(y, target):
    t, d = y.shape

    def body(y_ref, t_ref, l_ref, dy_ref):
        @pl.when(pl.program_id(0) == 0)
        def _():
            l_ref[...] = jnp.zeros_like(l_ref)

        err = y_ref[...] - t_ref[...]
        dy_ref[...] = err * (1.0 / d)
        part = 0.5 * jnp.sum(jnp.mean(err * err, axis=-1, keepdims=True), axis=0, keepdims=True)
        l_ref[...] += jnp.broadcast_to(part, l_ref.shape)

    return _row_call("loss_head", body, [(y, True), (target, True)],
                     [(_sds((SUBLANES, LANES), F32), False), (_sds((t, d), F32), True)], t,
                     acc_outs=(0,))


def _sigmoid(x):
    return 0.5 * jnp.tanh(0.5 * x) + 0.5


def _ffn_up_fwd(h, w_gu, lb, after=None):
    t, d = h.shape
    n = w_gu.shape[-1]
    half = N_DEV // 2
    bt = _row_block(t, BIG_ROW_BLOCK)
    extra = [] if after is None else [after]

    def body(h_ref, wg_ref, wu_ref, *rest):
        fac_ref, a_ref = rest[-2:]
        hv = h_ref[...]
        g = jnp.dot(hv, wg_ref[...], preferred_element_type=F32)
        u = jnp.dot(hv, wu_ref[...], preferred_element_type=F32)
        sig = _sigmoid(g)
        silu = g * sig
        fac_ref[0] = u * (sig + silu * (1.0 - sig))
        fac_ref[1] = silu
        a_ref[...] = (silu * u).astype(BF16)

    return pl.pallas_call(
        body, name="ffn_up_fwd", grid=(half, t // bt),
        in_specs=[pl.BlockSpec((bt, d), lambda j, i: (i, 0)),
                  pl.BlockSpec((None, None, d, n), lambda j, i: (j, lb, 0, 0)),
                  pl.BlockSpec((None, None, d, n), lambda j, i: (half + j, lb, 0, 0))] + [_ANY] * len(extra),
        out_specs=[pl.BlockSpec((2, None, bt, n), lambda j, i: (0, j, i, 0)),
                   pl.BlockSpec((None, bt, n), lambda j, i: (j, i, 0))],
        out_shape=[_sds((2, half, t, n), F32), _sds((half, t, n), BF16)],
        compiler_params=_params(("parallel", "parallel")),
    )(h, w_gu, w_gu, *extra)


def _ffn_down_bwd_a(dy, w_down, lb, fac):
    t, d = dy.shape
    _, half, _, n = fac.shape
    r = w_down.shape[2]
    bt = _row_block(t, MM_BLOCK)

    def body(dy_ref, w_ref, fac_ref, d_ref):
        da = lax.dot_general(dy_ref[...], w_ref[...].reshape(2 * r, d), (((1,), (1,)), ((), ())),
                             preferred_element_type=F32)
        d_ref[0] = (da * fac_ref[0]).astype(BF16)
        d_ref[1] = (da * fac_ref[1]).astype(BF16)

    return pl.pallas_call(
        body, name="ffn_down_bwd_a", grid=(half, t // bt),
        in_specs=[pl.BlockSpec((bt, d), lambda j, i: (i, 0)),
                  pl.BlockSpec((2, None, r, d), lambda j, i: (j, lb, 0, 0)),
                  pl.BlockSpec((2, None, bt, n), lambda j, i: (0, j, i, 0))],
        out_specs=pl.BlockSpec((2, None, bt, n), lambda j, i: (0, j, i, 0)),
        out_shape=_sds((2, half, t, n), BF16),
        compiler_params=_params(("parallel", "parallel")),
    )(dy, w_down, fac)


_INV_SQRT2 = 0.7071067811865476
_INV_SQRT_2PI = 0.3989422804014327


def _gelu(z):
    return 0.5 * z * (1.0 + lax.erf(z * _INV_SQRT2))


def _gelu_grad(z):
    return 0.5 * (1.0 + lax.erf(z * _INV_SQRT2)) + z * jnp.exp(-0.5 * z * z) * _INV_SQRT_2PI


def _window_mask():
    t_out = lax.broadcasted_iota(jnp.int32, (GMLP_WINDOW, GMLP_WINDOW), 0)
    s_in = lax.broadcasted_iota(jnp.int32, (GMLP_WINDOW, GMLP_WINDOW), 1)
    return (s_in // CHUNK) <= (t_out // CHUNK)


def _gmlp_recompute(z_ref, bin_ref, lng_ref, lnb_ref):
    half = N_DEV // 2
    z = z_ref[...] + bin_ref[...]
    ge = _gelu(z)
    u = ge[:half]
    v = ge[half:]
    width = half * v.shape[-1]
    mu = jnp.sum(jnp.sum(v, axis=0), axis=-1, keepdims=True) / width
    vc = v - mu
    var = jnp.sum(jnp.sum(vc * vc, axis=0), axis=-1, keepdims=True) / width
    rstd = lax.rsqrt(var + LN_EPS)
    xhat = vc * rstd
    vn = xhat * lng_ref[...] + lnb_ref[...]
    return z, u, xhat, rstd, vn


def _gmlp_mid_fwd(zpre, b_in, ln_g, ln_b, w_s, b_s):
    _, t, n = zpre.shape
    half = N_DEV // 2
    gd = half * n // GMLP_GROUPS
    per = n // gd
    w = GMLP_WINDOW

    def body(z_ref, bin_ref, lng_ref, lnb_ref, ws_ref, bs_ref, o_ref):
        _, u, _, _, vn = _gmlp_recompute(z_ref, bin_ref, lng_ref, lnb_ref)
        mask = _window_mask()
        for g in range(GMLP_GROUPS):
            sh, c0 = g // per, (g % per) * gd
            wsm = jnp.where(mask, ws_ref[g], 0.0).astype(BF16)
            s = jnp.dot(wsm, vn[sh][:, c0:c0 + gd].astype(BF16), preferred_element_type=F32) + bs_ref[g]
            o_ref[sh, :, c0:c0 + gd] = (u[sh][:, c0:c0 + gd] * s).astype(BF16)

    whole = lambda a: pl.BlockSpec(tuple(a.shape), lambda i: (0,) * a.ndim)
    return pl.pallas_call(
        body, name="gmlp_mid_fwd", grid=(t // w,),
        in_specs=[pl.BlockSpec((N_DEV, w, n), lambda i: (0, i, 0)),
                  whole(b_in), whole(ln_g), whole(ln_b), whole(w_s), whole(b_s)],
        out_specs=pl.BlockSpec((half, w, n), lambda i: (0, i, 0)),
        out_shape=_sds((half, t, n), BF16),
        compiler_params=_params(("parallel",)),
    )(zpre, b_in, ln_g, ln_b, w_s, b_s)


def _gmlp_mid_bwd(zpre, dgated, b_in, ln_g, ln_b, w_s, b_s):
    _, t, n = zpre.shape
    half = N_DEV // 2
    gd = half * n // GMLP_GROUPS
    per = n // gd
    w = GMLP_WINDOW
    width = half * n

    def body(z_ref, dg_ref, bin_ref, lng_ref, lnb_ref, ws_ref, bs_ref,
             dz_ref, dws_ref, dbs_ref, dlng_ref, dlnb_ref, dbin_ref, du_ref, dvn_ref):
        @pl.when(pl.program_id(0) == 0)
        def _():
            for r in (dws_ref, dbs_ref, dlng_ref, dlnb_ref, dbin_ref):
                r[...] = jnp.zeros_like(r)

        z, u, xhat, rstd, vn = _gmlp_recompute(z_ref, bin_ref, lng_ref, lnb_ref)
        mask = _window_mask()
        for g in range(GMLP_GROUPS):
            sh, c0 = g // per, (g % per) * gd
            wsm = jnp.where(mask, ws_ref[g], 0.0).astype(BF16)
            vg = vn[sh][:, c0:c0 + gd].astype(BF16)
            s = jnp.dot(wsm, vg, preferred_element_type=F32) + bs_ref[g]
            dgt = dg_ref[sh, :, c0:c0 + gd]
            ds = dgt * u[sh][:, c0:c0 + gd]
            du_ref[sh, :, c0:c0 + gd] = dgt * s
            dsb = ds.astype(BF16)
            dws = lax.dot_general(dsb, vg, (((1,), (1,)), ((), ())), preferred_element_type=F32)
            dws_ref[g] += jnp.where(mask, dws, 0.0)
            dbs_ref[g] += jnp.sum(ds, axis=-1, keepdims=True)
            dvn_ref[sh, :, c0:c0 + gd] = lax.dot_general(wsm, dsb, (((0,), (0,)), ((), ())),
                                                         preferred_element_type=F32)
        dvn = dvn_ref[...]
        dlng_ref[...] += jnp.sum(dvn * xhat, axis=1, keepdims=True)
        dlnb_ref[...] += jnp.sum(dvn, axis=1, keepdims=True)
        dxh = dvn * lng_ref[...]
        m1 = jnp.sum(jnp.sum(dxh, axis=0), axis=-1, keepdims=True) / width
        m2 = jnp.sum(jnp.sum(dxh * xhat, axis=0), axis=-1, keepdims=True) / width
        dv = rstd * (dxh - m1 - xhat * m2)
        gg = _gelu_grad(z)
        dzu = du_ref[...] * gg[:half]
        dzv = dv * gg[half:]
        dz_ref[:half] = dzu.astype(BF16)
        dz_ref[half:] = dzv.astype(BF16)
        dbin_ref[:half] += jnp.sum(dzu, axis=1, keepdims=True)
        dbin_ref[half:] += jnp.sum(dzv, axis=1, keepdims=True)

    whole = lambda a: pl.BlockSpec(tuple(a.shape), lambda i: (0,) * a.ndim)
    outs = [_sds((N_DEV, t, n), BF16), _sds(w_s.shape, F32), _sds(b_s.shape, F32),
            _sds(ln_g.shape, F32), _sds(ln_b.shape, F32), _sds(b_in.shape, F32)]
    return pl.pallas_call(
        body, name="gmlp_mid_bwd", grid=(t // w,),
        in_specs=[pl.BlockSpec((N_DEV, w, n), lambda i: (0, i, 0)),
                  pl.BlockSpec((half, w, n), lambda i: (0, i, 0)),
                  whole(b_in), whole(ln_g), whole(ln_b), whole(w_s), whole(b_s)],
        out_specs=[pl.BlockSpec((N_DEV, w, n), lambda i: (0, i, 0))] + [whole(o) for o in outs[1:]],
        out_shape=outs,
        scratch_shapes=[pltpu.VMEM((half, w, n), F32), pltpu.VMEM((half, w, n), F32)],
        compiler_params=_params(("arbitrary",)),
    )(zpre, dgated, b_in, ln_g, ln_b, w_s, b_s)


ATTN_CHUNKS = 4
ATTN_ROWS = ATTN_CHUNKS * CHUNK
ATTN_WINDOW = ATTN_ROWS + LEFT_PAD
ATTN_DIAGS = -(-(ATTN_ROWS + ATTN_WINDOW - 1) // LANES) * LANES
ATTN_ROLL = ATTN_DIAGS - (ATTN_ROWS - 1)


def _rel_vector(rel):
    j = np.arange(ATTN_DIAGS)
    idx = np.clip(ATTN_WINDOW - 1 - j, -(CHUNK - 1), MAX_REL) + (CHUNK - 1)
    return rel[:, idx]


def _attn_bias_mask(rel_ref, bm_ref):
    tt = lax.broadcasted_iota(jnp.int32, (ATTN_ROWS, ATTN_WINDOW), 0) // CHUNK
    rr = lax.broadcasted_iota(jnp.int32, (ATTN_ROWS, ATTN_WINDOW), 1) // CHUNK
    band = (rr >= tt) & (rr <= tt + LEFT_CHUNKS)
    for j in range(bm_ref.shape[0]):
        vec = jnp.broadcast_to(rel_ref[j:j + 1, :], (ATTN_ROWS, ATTN_DIAGS))
        toeplitz = pltpu.roll(vec, ATTN_ROLL, 1, stride=1, stride_axis=0)[:, :ATTN_WINDOW]
        bm_ref[j] = jnp.where(band, toeplitz, -jnp.inf)


def _attn_probs(q_ref, k_ref, bm_ref, j, hd, start, valid):
    qh = q_ref[:, j * hd:(j + 1) * hd]
    kb = k_ref[pl.ds(start, ATTN_WINDOW), j * hd:(j + 1) * hd]
    sc = lax.dot_general(qh, kb, (((1,), (1,)), ((), ())), preferred_element_type=F32)
    sc = sc * (hd ** -0.5) + bm_ref[j]
    sc = jnp.where(valid, sc, -jnp.inf)
    sc = sc - jnp.max(sc, axis=-1, keepdims=True)
    e = jnp.exp(sc)
    return e / jnp.sum(e, axis=-1, keepdims=True), qh, kb


def _window_valid(start):
    r = lax.broadcasted_iota(jnp.int32, (1, ATTN_WINDOW), 1)
    return (start + r) >= LEFT_PAD


def _attn_fwd(q, kvp, rel_vec):
    t, d = q.shape
    hd = d // N_HEADS
    half = N_DEV // 2
    n = kvp.shape[-1]
    per = n // hd
    rows = kvp.shape[1]

    def body(q_ref, k_ref, v_ref, rel_ref, o_ref, bm_ref):
        @pl.when(pl.program_id(1) == 0)
        def _():
            _attn_bias_mask(rel_ref, bm_ref)

        start = pl.multiple_of(pl.program_id(1) * ATTN_ROWS, ATTN_ROWS)
        valid = _window_valid(start)
        for j in range(per):
            p, _, _ = _attn_probs(q_ref, k_ref, bm_ref, j, hd, start, valid)
            vb = v_ref[pl.ds(start, ATTN_WINDOW), j * hd:(j + 1) * hd]
            o_ref[:, j * hd:(j + 1) * hd] = jnp.dot(p.astype(BF16), vb, preferred_element_type=F32).astype(BF16)

    return pl.pallas_call(
        body, name="attn_fwd", grid=(half, t // ATTN_ROWS),
        in_specs=[pl.BlockSpec((ATTN_ROWS, n), lambda g, i: (i, g)),
                  pl.BlockSpec((None, rows, n), lambda g, i: (g, 0, 0)),
                  pl.BlockSpec((None, rows, n), lambda g, i: (half + g, 0, 0)),
                  pl.BlockSpec((None, per, ATTN_DIAGS), lambda g, i: (g, 0, 0))],
        out_specs=pl.BlockSpec((ATTN_ROWS, n), lambda g, i: (i, g)),
        out_shape=_sds((t, d), BF16),
        scratch_shapes=[pltpu.VMEM((per, ATTN_ROWS, ATTN_WINDOW), F32)],
        compiler_params=_params(("arbitrary", "arbitrary")),
    )(q, kvp, kvp, rel_vec.reshape(half, per, ATTN_DIAGS))


def _attn_bwd(q, dout, kvp, rel_vec, dk_in=None, dv_in=None):
    t, d = q.shape
    hd = d // N_HEADS
    half = N_DEV // 2
    n = kvp.shape[-1]
    per = n // hd
    rows = kvp.shape[1]
    scale = hd ** -0.5
    carry = dk_in is not None

    def body(q_ref, do_ref, k_ref, v_ref, rel_ref, *rest):
        dq_ref, dk_ref, dv_ref, dsc_ref, bm_ref = rest[-5:]

        @pl.when(pl.program_id(1) == 0)
        def _():
            _attn_bias_mask(rel_ref, bm_ref)
            dk_ref[...] = rest[0][...] if carry else jnp.zeros_like(dk_ref)
            dv_ref[...] = rest[1][...] if carry else jnp.zeros_like(dv_ref)
            dsc_ref[...] = jnp.zeros_like(dsc_ref)

        start = pl.multiple_of(pl.program_id(1) * ATTN_ROWS, ATTN_ROWS)
        valid = _window_valid(start)
        for j in range(per):
            cols = slice(j * hd, (j + 1) * hd)
            p, qh, kb = _attn_probs(q_ref, k_ref, bm_ref, j, hd, start, valid)
            vb = v_ref[pl.ds(start, ATTN_WINDOW), cols]
            doh = do_ref[:, cols]
            dp = lax.dot_general(doh, vb, (((1,), (1,)), ((), ())), preferred_element_type=F32)
            ds = p * (dp - jnp.sum(dp * p, axis=-1, keepdims=True))
            dsc_ref[j] += sum(ds[a * CHUNK:(a + 1) * CHUNK, a * CHUNK:a * CHUNK + BAND]
                              for a in range(ATTN_CHUNKS))
            dsb = (ds * scale).astype(BF16)
            dq_ref[:, cols] = jnp.dot(dsb, kb, preferred_element_type=F32).astype(BF16)
            dk_ref[pl.ds(start, ATTN_WINDOW), cols] += lax.dot_general(
                dsb, qh, (((0,), (0,)), ((), ())), preferred_element_type=F32)
            dv_ref[pl.ds(start, ATTN_WINDOW), cols] += lax.dot_general(
                p.astype(BF16), doh, (((0,), (0,)), ((), ())), preferred_element_type=F32)

    tile = pl.BlockSpec((ATTN_ROWS, n), lambda g, i: (i, g))
    shard = pl.BlockSpec((None, rows, n), lambda g, i: (g, 0, 0))
    in_specs = [tile, tile, shard, pl.BlockSpec((None, rows, n), lambda g, i: (half + g, 0, 0)),
                pl.BlockSpec((None, per, ATTN_DIAGS), lambda g, i: (g, 0, 0))]
    operands = [q, dout, kvp, kvp, rel_vec.reshape(half, per, ATTN_DIAGS)]
    if carry:
        in_specs += [shard, shard]
        operands += [dk_in, dv_in]
    acc = _sds((half, rows, n), F32)
    return pl.pallas_call(
        body, name="attn_bwd", grid=(half, t // ATTN_ROWS),
        in_specs=in_specs,
        out_specs=[tile, shard, shard, pl.BlockSpec((per, CHUNK, BAND), lambda g, i: (g, 0, 0))],
        out_shape=[_sds((t, d), BF16), acc, acc, _sds((N_HEADS, CHUNK, BAND), F32)],
        scratch_shapes=[pltpu.VMEM((per, ATTN_ROWS, ATTN_WINDOW), F32)],
        compiler_params=_params(("arbitrary", "arbitrary")),
    )(*operands)


SKEW_PITCH = 640
SKEW = SKEW_PITCH + 1
SKEW_LANES = -(-SKEW // LANES) * LANES


def _skew_diagonals(dsc):
    h = dsc.shape[0]
    wide = jnp.pad(dsc, ((0, 0), (0, 0), (0, SKEW_PITCH - BAND))).reshape(h, CHUNK * SKEW_PITCH)
    wide = jnp.pad(wide, ((0, 0), (0, CHUNK))).reshape(h, CHUNK, SKEW)
    return jnp.pad(wide, ((0, 0), (0, 0), (0, SKEW_LANES - SKEW)))


def _rel_bias_grad(skewed):
    heads = skewed.shape[0]
    hb = SUBLANES

    def body(d_ref, o_ref):
        col = lax.broadcasted_iota(jnp.int32, (SKEW_LANES, N_REL), 0)
        bucket = lax.broadcasted_iota(jnp.int32, (SKEW_LANES, N_REL), 1)
        diag = jnp.where(col < BAND, col, col - SKEW)
        idx = jnp.clip(LEFT_PAD - diag, -(CHUNK - 1), MAX_REL) + (CHUNK - 1)
        oh = ((idx == bucket) & (col < SKEW)).astype(BF16)
        dv = jnp.sum(d_ref[...], axis=1)
        hi = dv.astype(BF16)
        rest = dv - hi.astype(F32)
        mid = rest.astype(BF16)
        lo = (rest - mid.astype(F32)).astype(BF16)
        acc = jnp.dot(hi, oh, preferred_element_type=F32)
        acc += jnp.dot(mid, oh, preferred_element_type=F32)
        acc += jnp.dot(lo, oh, preferred_element_type=F32)
        o_ref[...] = acc

    return pl.pallas_call(
        body, name="rel_bias_grad", grid=(heads // hb,),
        in_specs=[pl.BlockSpec((hb, CHUNK, SKEW_LANES), lambda i: (i, 0, 0))],
        out_specs=pl.BlockSpec((hb, N_REL), lambda i: (i, 0)),
        out_shape=_sds((heads, N_REL), F32),
        compiler_params=_params(("parallel",)),
    )(skewed)


def _sum_parts(parts):
    s_n, rows, c = parts.shape
    br = _row_block(rows, OPT_ROW_BLOCK)

    def body(p_ref, o_ref):
        acc = p_ref[0].astype(F32)
        for s in range(1, s_n):
            acc = acc + p_ref[s].astype(F32)
        o_ref[...] = acc

    return pl.pallas_call(
        body, name="sum_parts", grid=(rows // br,),
        in_specs=[pl.BlockSpec((s_n, br, c), lambda i: (0, i, 0))],
        out_specs=pl.BlockSpec((br, c), lambda i: (i, 0)),
        out_shape=_sds((rows, c), F32),
        compiler_params=_params(("parallel",)),
    )(parts)


def _adamw(own, own_idx, parts, w, m, v, row0=0, bufs=None, after=None):
    _, rows, c = own.shape
    s_n = 0 if parts is None else parts.shape[0]
    total = w.shape[0]
    br = _row_block(rows, OPT_ROW_BLOCK)
    assert row0 % br == 0 and (bufs is not None or (row0 == 0 and total == rows))
    b0 = row0 // br
    m_corr = 1.0 - ADAM_B1 ** ADAM_STEP
    v_corr = 1.0 - ADAM_B2 ** ADAM_STEP

    def body(idx_ref, own_ref, *refs):
        if s_n:
            p_ref, refs = refs[0], refs[1:]
        w_ref, m_ref, v_ref = refs[:3]
        g_ref, d_ref, nm_ref, nv_ref = refs[-4:]
        g = own_ref[...].astype(F32)
        for s in range(s_n):
            g = g + p_ref[s].astype(F32)
        nm = ADAM_B1 * m_ref[...] + (1.0 - ADAM_B1) * g
        nv = ADAM_B2 * v_ref[...] + (1.0 - ADAM_B2) * (g * g)
        g_ref[...] = g
        nm_ref[...] = nm
        nv_ref[...] = nv
        d_ref[...] = -ADAM_LR * ((nm / m_corr) / (jnp.sqrt(nv / v_corr) + ADAM_EPS) + ADAM_WD * w_ref[...])

    tile = pl.BlockSpec((br, c), lambda i, idx: (i + b0, 0))
    in_specs = [pl.BlockSpec((None, br, c), lambda i, idx: (idx[0], i, 0))]
    operands = [own_idx, own]
    if s_n:
        in_specs.append(pl.BlockSpec((s_n, br, c), lambda i, idx: (0, i, 0)))
        operands.append(parts)
    in_specs += [tile, tile, tile]
    operands += [w, m, v]
    aliases = {}
    if bufs is not None:
        aliases = {len(operands) + j: j for j in range(4)}
        in_specs += [_ANY] * 4
        operands += list(bufs)
    if after is not None:
        in_specs.append(_ANY)
        operands.append(after)
    out = _sds((total, c), F32)
    return pl.pallas_call(
        body, name="adamw",
        grid_spec=pltpu.PrefetchScalarGridSpec(
            num_scalar_prefetch=1, grid=(rows // br,), in_specs=in_specs,
            out_specs=[tile, tile, tile, tile]),
        out_shape=[out, out, out, out],
        input_output_aliases=aliases,
        compiler_params=_params(("parallel",)),
    )(*operands)


def _chip_sum(p, r1, core):
    half = N_DEV // 2
    c = p.shape[-1]
    rows = int(np.prod(p.shape[1:-1]))
    br = _row_block(rows, BIG_ROW_BLOCK)

    def body(core_ref, p_ref, r_ref, o_ref):
        o_ref[...] = (p_ref[...].astype(F32) + r_ref[...].astype(F32)).astype(BF16)

    out = pl.pallas_call(
        body, name="chip_sum",
        grid_spec=pltpu.PrefetchScalarGridSpec(
            num_scalar_prefetch=1, grid=(half, rows // br),
            in_specs=[pl.BlockSpec((None, None, br, c), lambda q, i, cr: (q, cr[0], i, 0)),
                      pl.BlockSpec((None, br, c), lambda q, i, cr: (q, i, 0))],
            out_specs=pl.BlockSpec((None, br, c), lambda q, i, cr: (q, i, 0))),
        out_shape=_sds((half, rows, c), BF16),
        compiler_params=_params(("parallel", "parallel")),
    )(core, p.reshape(half, 2, rows, c), r1.reshape(half, rows, c))
    return out.reshape((half,) + p.shape[1:])


def _position():
    return tuple(lax.axis_index(a) for a in MESH_AXES)


def _linear(px, py, pc):
    return 4 * px + 2 * py + pc


def _all_gather_small(v, after=()):
    rows, lanes = v.shape

    def body(x_ref, *rest):
        out_ref, send_sems, recv_sems, local_sem = rest[-4:]
        x, y, c = _position()
        me, sibling = (x, y, c), (x, y, 1 - c)
        chips = [(1 - x, y), (x, 1 - y), (1 - x, 1 - y)]

        def copy(k, block, to, src=None):
            dst = out_ref.at[_linear(*block)]
            return pltpu.make_async_remote_copy(
                src_ref=dst if src is None else src, dst_ref=dst,
                send_sem=send_sems.at[k], recv_sem=recv_sems.at[k],
                device_id=to, device_id_type=MESH_ID)

        mine = pltpu.make_async_copy(x_ref, out_ref.at[_linear(*me)], local_sem)
        mine.start()
        first = [copy(0, me, sibling, src=x_ref)]
        first += [copy(1 + j, me, (*chip, c), src=x_ref) for j, chip in enumerate(chips)]
        for cp in first:
            cp.start()
        passed = [copy(4 + j, (*chip, c), sibling) for j, chip in enumerate(chips)]
        for j, chip in enumerate(chips):
            copy(1 + j, (*chip, c), me).wait_recv()
            passed[j].start()
        copy(0, sibling, me).wait_recv()
        for j, chip in enumerate(chips):
            copy(4 + j, (*chip, 1 - c), me).wait_recv()
        for cp in first + passed:
            cp.wait_send()
        mine.wait()

    return pl.pallas_call(
        body, name="all_gather_small",
        out_shape=_sds((N_DEV, rows, lanes), v.dtype),
        in_specs=[_VMEM] + [_ANY] * len(after), out_specs=_VMEM,
        scratch_shapes=[pltpu.SemaphoreType.DMA((7,)), pltpu.SemaphoreType.DMA((7,)),
                        pltpu.SemaphoreType.DMA],
        compiler_params=pltpu.CompilerParams(vmem_limit_bytes=VMEM_LIMIT),
    )(v, *after)


_HBM = pl.BlockSpec(memory_space=pltpu.HBM)
_SEM = pl.BlockSpec(memory_space=pltpu.SEMAPHORE)
_EFFECT = pltpu.SideEffectType.DATAFLOW_SIDE_EFFECTING
_ALL_CHIPS = [(0, 0), (0, 1), (1, 0), (1, 1)]


def _other_chips(x, y):
    return [(1 - x, y), (x, 1 - y), (1 - x, 1 - y)]


def _in_hbm(a):
    return pltpu.with_memory_space_constraint(a, pltpu.HBM)


def _token():
    return _sds((SUBLANES, LANES), F32)


def _gather_copies(refs, send_sems, recv_sems, to_sibling):
    x, y, c = _position()
    if to_sibling:
        plan = [((x, y, 1 - c), _linear(*q, c), _linear(*q, 1 - c)) for q in _ALL_CHIPS]
    else:
        plan = [((*ch, c), _linear(x, y, c), _linear(*ch, c)) for ch in _other_chips(x, y)]

    def copy(ref, i, k, peer, block):
        return pltpu.make_async_remote_copy(
            src_ref=ref.at[block], dst_ref=ref.at[block],
            send_sem=send_sems.at[len(plan) * i + k], recv_sem=recv_sems.at[len(plan) * i + k],
            device_id=peer, device_id_type=MESH_ID)

    return [(copy(ref, i, k, peer, sent), copy(ref, i, k, peer, landed))
            for i, ref in enumerate(refs) for k, (peer, sent, landed) in enumerate(plan)]


def _gather_ici_start(name, lands, after=None, to_sibling=False):
    n = len(lands)
    extra = [] if after is None else [after]
    n_sem = n * (len(_ALL_CHIPS) if to_sibling else len(_ALL_CHIPS) - 1)

    def body(*refs):
        send_sems, recv_sems, token = refs[-n - 3], refs[-n - 2], refs[-1]
        for sent, _ in _gather_copies(refs[:n], send_sems, recv_sems, to_sibling):
            sent.start()
        token[...] = jnp.zeros_like(token)

    out = pl.pallas_call(
        body, name=name,
        out_shape=(pltpu.SemaphoreType.DMA((n_sem,)), pltpu.SemaphoreType.DMA((n_sem,)),
                   *[pltpu.HBM(a.shape, a.dtype) for a in lands], _token()),
        in_specs=[_HBM] * n + [_ANY] * len(extra), out_specs=(_SEM, _SEM, *[_HBM] * n, _VMEM),
        input_output_aliases={i: 2 + i for i in range(n)},
        compiler_params=pltpu.CompilerParams(has_side_effects=_EFFECT),
    )(*[_in_hbm(a) for a in lands], *extra)
    return out[0], out[1], list(out[2:2 + n]), out[-1]


def _gather_ici_wait(name, lands, send_sems, recv_sems, after, to_sibling=False):
    n = len(lands)

    def body(*refs):
        for sent, landed in _gather_copies(refs[:n], refs[n], refs[n + 1], to_sibling):
            sent.wait_send()
            landed.wait_recv()

    out = pl.pallas_call(
        body, name=name,
        out_shape=[pltpu.HBM(a.shape, a.dtype) for a in lands],
        in_specs=[_HBM] * n + [_SEM, _SEM, _ANY], out_specs=[_HBM] * n,
        input_output_aliases={i: i for i in range(n)},
        compiler_params=pltpu.CompilerParams(has_side_effects=_EFFECT),
    )(*lands, send_sems, recv_sems, after)
    return list(out)


def _gather_d2d(lands):
    n = len(lands)

    def body(*refs):
        ins, outs, send_sems, recv_sems = refs[:n], refs[n:2 * n], refs[2 * n], refs[2 * n + 1]
        x, y, c = _position()

        def copy(i, q, core):
            block = _linear(*_ALL_CHIPS[q], core)
            return pltpu.make_async_remote_copy(
                src_ref=ins[i].at[block], dst_ref=outs[i].at[block],
                send_sem=send_sems.at[i, q], recv_sem=recv_sems.at[i, q],
                device_id=(x, y, 1 - c), device_id_type=MESH_ID)

        sent = [copy(i, q, c) for i in range(n) for q in range(len(_ALL_CHIPS))]
        for cp in sent:
            cp.start()
        for i in range(n):
            for q in range(len(_ALL_CHIPS)):
                copy(i, q, 1 - c).wait_recv()
        for cp in sent:
            cp.wait_send()

    return pl.pallas_call(
        body, name="gather_d2d",
        out_shape=[_sds(a.shape, a.dtype) for a in lands],
        in_specs=[_ANY] * n, out_specs=[_ANY] * n,
        input_output_aliases={i: i for i in range(n)},
        scratch_shapes=[pltpu.SemaphoreType.DMA((n, 4)), pltpu.SemaphoreType.DMA((n, 4))],
    )(*lands)


def _partials_d2d(parts):
    n = len(parts)
    half = N_DEV // 2

    def body(*refs):
        ins, outs, send_sems, recv_sems = refs[:n], refs[n:2 * n], refs[2 * n], refs[2 * n + 1]
        x, y, c = _position()

        def copy(i, q):
            return pltpu.make_async_remote_copy(
                src_ref=ins[i].at[_linear(*_ALL_CHIPS[q], 1 - c)], dst_ref=outs[i].at[q],
                send_sem=send_sems.at[i, q], recv_sem=recv_sems.at[i, q],
                device_id=(x, y, 1 - c), device_id_type=MESH_ID)

        sent = [copy(i, q) for i in range(n) for q in range(half)]
        for cp in sent:
            cp.start()
        for cp in sent:
            cp.wait_recv()
        for cp in sent:
            cp.wait_send()

    return pl.pallas_call(
        body, name="partials_d2d",
        out_shape=[_sds((half,) + p.shape[1:], p.dtype) for p in parts],
        in_specs=[_ANY] * n, out_specs=[_ANY] * n,
        scratch_shapes=[pltpu.SemaphoreType.DMA((n, half)), pltpu.SemaphoreType.DMA((n, half))],
    )(*parts)


def _partials_peers(x, y, c, direct):
    chips = _other_chips(x, y)
    if not direct:
        return [((*ch, c), 2 * ch[0] + ch[1]) for ch in chips]
    peers = [(x, y, 1 - c)] + [(*ch, c) for ch in chips] + [(*ch, 1 - c) for ch in chips]
    return [(p, _linear(*p)) for p in peers]


def _partials_copies(srcs, lands, send_sems, recv_sems, direct):
    x, y, c = _position()
    peers = _partials_peers(x, y, c, direct)
    return [pltpu.make_async_remote_copy(
        src_ref=srcs[i].at[block], dst_ref=lands[i].at[k],
        send_sem=send_sems.at[len(peers) * i + k], recv_sem=recv_sems.at[len(peers) * i + k],
        device_id=peer, device_id_type=MESH_ID)
        for i in range(len(srcs)) for k, (peer, block) in enumerate(peers)]


def _partials_send_start(name, srcs, lands, direct, after=None):
    n = len(srcs)
    n_sem = n * (N_DEV - 1 if direct else len(_ALL_CHIPS) - 1)

    def body(*refs):
        _, send_sems, recv_sems = refs[:2 * n], refs[-2 * n - 3], refs[-2 * n - 2]
        for cp in _partials_copies(refs[:n], refs[n:2 * n], send_sems, recv_sems, direct):
            cp.start()
        refs[-1][...] = jnp.zeros_like(refs[-1])

    both = list(srcs) + list(lands)
    extra = [] if after is None else [after]
    out = pl.pallas_call(
        body, name=name,
        out_shape=(pltpu.SemaphoreType.DMA((n_sem,)), pltpu.SemaphoreType.DMA((n_sem,)),
                   *[pltpu.HBM(a.shape, a.dtype) for a in both], _token()),
        in_specs=[_HBM] * (2 * n) + [_ANY] * len(extra), out_specs=(_SEM, _SEM, *[_HBM] * (2 * n), _VMEM),
        input_output_aliases={i: 2 + i for i in range(2 * n)},
        compiler_params=pltpu.CompilerParams(has_side_effects=_EFFECT),
    )(*[_in_hbm(a) for a in both], *extra)
    return out[0], out[1], list(out[2:2 + n]), list(out[2 + n:2 + 2 * n]), out[-1]


def _partials_send_wait(name, srcs, lands, send_sems, recv_sems, direct, after):
    n = len(srcs)

    def body(*refs):
        for cp in _partials_copies(refs[:n], refs[n:2 * n], refs[2 * n], refs[2 * n + 1], direct):
            cp.wait_send()
            cp.wait_recv()

    both = list(srcs) + list(lands)
    out = pl.pallas_call(
        body, name=name,
        out_shape=[pltpu.HBM(a.shape, a.dtype) for a in both],
        in_specs=[_HBM] * (2 * n) + [_SEM, _SEM, _ANY], out_specs=[_HBM] * (2 * n),
        input_output_aliases={i: i for i in range(2 * n)},
        compiler_params=pltpu.CompilerParams(has_side_effects=_EFFECT),
    )(*both, send_sems, recv_sems, after)
    return list(out[:n]), list(out[n:])


def _pack(arrs):
    flat = jnp.concatenate([a.reshape(-1).astype(F32) for a in arrs])
    block = OPT_ROW_BLOCK if flat.shape[0] > OPT_ROW_BLOCK * LANES else SUBLANES
    pad = (-flat.shape[0]) % (block * LANES)
    if pad:
        flat = jnp.concatenate([flat, jnp.zeros((pad,), F32)])
    return flat.reshape(-1, LANES)


def _unpack(packed, shapes, lead=()):
    flat = packed.reshape(lead + (-1,))
    out, off = [], 0
    for s in shapes:
        size = int(np.prod(s))
        out.append(flat[..., off:off + size].reshape(lead + tuple(s)))
        off += size
    return out


def _unshard_last(g):
    nd = g.ndim
    perm = tuple(range(1, nd - 1)) + (0, nd - 1)
    t = jnp.transpose(g, perm)
    return t.reshape(t.shape[:-2] + (N_DEV * g.shape[-1],))


def kernel(x, c, w_ada, b_ada, ln_g, ln_b, ffn_gu, ffn_down, gmlp_w_in, gmlp_b_in, gmlp_ln_g, gmlp_ln_b, gmlp_w_s, gmlp_b_s, gmlp_w_out, w_ada_kv, b_ada_kv, w_kv, attn_w_q, attn_rel_bias, attn_w_o, loss_target, m_w_ada, m_b_ada, m_ln_g, m_ln_b, m_ffn_gu, m_ffn_down, m_gmlp_w_in, m_gmlp_b_in, m_gmlp_ln_g, m_gmlp_ln_b, m_gmlp_w_s, m_gmlp_b_s, m_gmlp_w_out, m_w_ada_kv, m_b_ada_kv, m_w_kv, m_attn_w_q, m_attn_rel_bias, m_attn_w_o, v_w_ada, v_b_ada, v_ln_g, v_ln_b, v_ffn_gu, v_ffn_down, v_gmlp_w_in, v_gmlp_b_in, v_gmlp_ln_g, v_gmlp_ln_b, v_gmlp_w_s, v_gmlp_b_s, v_gmlp_w_out, v_w_ada_kv, v_b_ada_kv, v_w_kv, v_attn_w_q, v_attn_rel_bias, v_attn_w_o):
    weights = dict(w_ada=w_ada, b_ada=b_ada, ln_g=ln_g, ln_b=ln_b, ffn_gu=ffn_gu, ffn_down=ffn_down,
                   gmlp_w_in=gmlp_w_in, gmlp_b_in=gmlp_b_in, gmlp_ln_g=gmlp_ln_g, gmlp_ln_b=gmlp_ln_b,
                   gmlp_w_s=gmlp_w_s, gmlp_b_s=gmlp_b_s, gmlp_w_out=gmlp_w_out, w_ada_kv=w_ada_kv,
                   b_ada_kv=b_ada_kv, w_kv=w_kv, attn_w_q=attn_w_q, attn_rel_bias=attn_rel_bias,
                   attn_w_o=attn_w_o)
    mom1 = dict(w_ada=m_w_ada, b_ada=m_b_ada, ln_g=m_ln_g, ln_b=m_ln_b, ffn_gu=m_ffn_gu, ffn_down=m_ffn_down,
                gmlp_w_in=m_gmlp_w_in, gmlp_b_in=m_gmlp_b_in, gmlp_ln_g=m_gmlp_ln_g, gmlp_ln_b=m_gmlp_ln_b,
                gmlp_w_s=m_gmlp_w_s, gmlp_b_s=m_gmlp_b_s, gmlp_w_out=m_gmlp_w_out, w_ada_kv=m_w_ada_kv,
                b_ada_kv=m_b_ada_kv, w_kv=m_w_kv, attn_w_q=m_attn_w_q, attn_rel_bias=m_attn_rel_bias,
                attn_w_o=m_attn_w_o)
    mom2 = dict(w_ada=v_w_ada, b_ada=v_b_ada, ln_g=v_ln_g, ln_b=v_ln_b, ffn_gu=v_ffn_gu, ffn_down=v_ffn_down,
                gmlp_w_in=v_gmlp_w_in, gmlp_b_in=v_gmlp_b_in, gmlp_ln_g=v_gmlp_ln_g, gmlp_ln_b=v_gmlp_ln_b,
                gmlp_w_s=v_gmlp_w_s, gmlp_b_s=v_gmlp_b_s, gmlp_w_out=v_gmlp_w_out, w_ada_kv=v_w_ada_kv,
                b_ada_kv=v_b_ada_kv, w_kv=v_w_kv, attn_w_q=v_attn_w_q, attn_rel_bias=v_attn_rel_bias,
                attn_w_o=v_attn_w_o)
    order = list(weights)

    x = x[0]
    target = loss_target[0]
    t, d = x.shape
    n_mod = w_ada.shape[-1] * N_DEV // d
    mod_w = w_ada.shape[-1]
    kv_w = w_ada_kv.shape[-1]
    n_b = DEPTH - N_A
    me = _linear(*_position())

    l2 = DEPTH * 2
    big = dict(
        ffn_gu=ffn_gu.reshape((l2,) + ffn_gu.shape[2:]),
        ffn_down=ffn_down.reshape((l2,) + ffn_down.shape[2:]),
        gmlp_w_in=gmlp_w_in, gmlp_w_out=gmlp_w_out, w_kv=w_kv[None],
        attn_w_q=attn_w_q, attn_w_o=attn_w_o)
    big_names = list(big)
    core = lax.axis_index("c").astype(jnp.int32).reshape(1)
    chip = (2 * lax.axis_index("x") + lax.axis_index("y")).astype(jnp.int32).reshape(1)

    fwd_groups = [
        {"ffn_gu": (0, 1), "ffn_down": (0, 1)},
        {"gmlp_w_in": (0, 1), "gmlp_w_out": (0, 1)},
        {"ffn_gu": (1, 1), "ffn_down": (1, 1)},
        {"ffn_gu": (2, 1), "ffn_down": (2, 1)},
        {"gmlp_w_in": (1, 1), "gmlp_w_out": (1, 1)},
        {"ffn_gu": (3, 1), "ffn_down": (3, 1), "w_kv": (0, 1)},
        {"ffn_gu": (4, 1), "ffn_down": (4, 1)},
        {"attn_w_q": (0, 1), "attn_w_o": (0, 1)},
        {"ffn_gu": (5, 1), "ffn_down": (5, 1)},
        {"ffn_gu": (6, 1), "ffn_down": (6, 1)},
        {"attn_w_q": (1, 1), "attn_w_o": (1, 1)},
        {"ffn_gu": (7, 1), "ffn_down": (7, 1)},
    ]
    bwd_groups = []
    for l in range(DEPTH):
        g = {"ffn_gu": (2 * l, 2), "ffn_down": (2 * l, 2)}
        if l < N_A:
            g.update({"gmlp_w_in": (l, 1), "gmlp_w_out": (l, 1)})
        else:
            g.update({"attn_w_q": (l - N_A, 1), "attn_w_o": (l - N_A, 1)})
        if l == N_A - 1:
            g["w_kv"] = (0, 1)
        bwd_groups.append(g)

    def slot_of(groups, name, slot):
        for gi, g in enumerate(groups):
            if name in g and g[name][0] <= slot < g[name][0] + g[name][1]:
                return gi, slot - g[name][0]
        raise KeyError((name, slot))

    def start_group(gi, after=None):
        lands = []
        for name, (s0, cnt) in fwd_groups[gi].items():
            shard = big[name][s0:s0 + cnt].astype(BF16)
            land = lax.empty((N_DEV,) + shard.shape, BF16)
            lands.append(lax.dynamic_update_slice(land, shard[None], (me,) + (0,) * shard.ndim))
        return _gather_ici_start(f"gather_ici_start_{gi}", lands, after)

    gathered = [None] * len(fwd_groups)

    passing = {}

    def pass_on_early(gi, after):
        send_sems, recv_sems, lands, _ = flights[gi]
        lands = _gather_ici_wait(f"gather_ici_wait_{gi}", lands, send_sems, recv_sems, after)
        passing[gi] = _gather_ici_start(f"gather_d2d_start_{gi}", lands, to_sibling=True)
        return passing[gi][3]

    def land_group(gi, after):
        if gi in passing:
            send_sems, recv_sems, lands, _ = passing.pop(gi)
            lands = _gather_ici_wait(f"gather_d2d_wait_{gi}", lands, send_sems, recv_sems, after, to_sibling=True)
        else:
            send_sems, recv_sems, lands, _ = flights[gi]
            lands = _gather_d2d(_gather_ici_wait(f"gather_ici_wait_{gi}", lands, send_sems, recv_sems, after))
        gathered[gi] = dict(zip(fwd_groups[gi], lands))

    def weight(name, slot):
        gi, local = slot_of(fwd_groups, name, slot)
        return gathered[gi][name], local

    swapped = ("ffn_gu",)

    def grad_shape(name):
        s = big[name].shape[1:]
        return s[:-2] + (s[-1], s[-2]) if name in swapped else s

    partial = [{name: lax.empty((N_DEV, cnt) + grad_shape(name), BF16) for name, (_, cnt) in g.items()}
               for g in bwd_groups]

    c_all = _all_gather_small(_pack([c]))
    c_all = _unpack(c_all, [(d,)], lead=(N_DEV,))[0]
    c4 = _as4(c_all)
    mod_part = _matmul("ada_fwd", c4, w_ada[:, None], (DEPTH, 1, N_DEV, mod_w), F32, a_silu=True)
    kv_part = _matmul("ada_kv_fwd", c4, _as4(w_ada_kv), (1, 1, N_DEV, kv_w), F32, a_silu=True)
    small_shapes = [mod_part.shape, kv_part.shape, ln_g.shape, ln_b.shape, gmlp_b_in.shape,
                    gmlp_ln_g.shape, gmlp_ln_b.shape, attn_rel_bias.shape]
    small = _all_gather_small(_pack([mod_part, kv_part, ln_g, ln_b, gmlp_b_in, gmlp_ln_g, gmlp_ln_b,
                                     attn_rel_bias]))
    flights = [start_group(0, after=small)]
    flights += [start_group(gi, after=flights[0][3]) for gi in range(1, len(fwd_groups))]
    start_token = sum(f[3][0, 0] for f in flights)
    (mod_g, kvm_g, ln_g_g, ln_b_g, b_in_g, gln_g_g, gln_b_g, rel_g) = _unpack(small, small_shapes, lead=(N_DEV,))
    mod_mine = lax.dynamic_index_in_dim(mod_g[:, :, 0], me, axis=2, keepdims=False)
    mod = _unshard_last(mod_mine) + b_ada
    mod = mod.reshape(DEPTH, n_mod, 1, d)
    kvm_mine = lax.dynamic_index_in_dim(kvm_g[:, 0, 0], me, axis=1, keepdims=False)
    mkv = (_unshard_last(kvm_mine) + b_ada_kv).reshape(2, 1, d)
    ln_g_f = _unshard_last(ln_g_g)
    ln_b_f = _unshard_last(ln_b_g)
    half = N_DEV // 2
    b_in_f = jnp.transpose(b_in_g, (1, 0, 2))[:, :, None, :]
    gln_g_f = _unshard_last(gln_g_g).reshape(N_A, half, 1, -1)
    gln_b_f = _unshard_last(gln_b_g).reshape(N_A, half, 1, -1)
    rel_f = _unshard_last(rel_g)

    def shard_act(a):
        return a.reshape(a.shape[0], a.shape[2], a.shape[3])

    def grad_into(name, slot, mm):
        gi, local = slot_of(bwd_groups, name, slot)
        partial[gi][name] = mm(partial[gi][name], local)

    def ffn_fwd(h, lw, after=None):
        w_gu, l_gu = weight("ffn_gu", lw)
        w_dn, l_dn = weight("ffn_down", lw)
        gu, a = _ffn_up_fwd(h, w_gu, l_gu, after)
        y = _matmul("ffn_down_fwd", a[:, None], w_dn, (1, 1, t, d), F32, lb=l_dn, b_merge=2, reduce=True)
        return y[0, 0], (gu, a)

    def ffn_bwd(dy, h, saved, lw):
        gu, a = saved
        w_gu, l_gu = weight("ffn_gu", lw)
        w_dn, l_dn = weight("ffn_down", lw)
        dgu = _ffn_down_bwd_a(dy, w_dn, l_dn, gu).reshape((N_DEV,) + gu.shape[2:])
        grad_into("ffn_down", lw, lambda buf, lo: _matmul(
            "ffn_down_bwd_w", a[:, None], _as4(dy), buf.shape, BF16, ta=True, lo=lo, out_merge=2, out_buf=buf))
        dh = _matmul("ffn_gu_bwd_a", dgu[:, None], w_gu, (1, 1, t, d), F32, lb=l_gu, tb=True, reduce=True)
        grad_into("ffn_gu", lw, lambda buf, lo: _matmul(
            "ffn_gu_bwd_w", dgu[:, None], _as4(h), buf.shape, BF16, ta=True, lo=lo, out_buf=buf))
        return dh[0, 0], {}

    def gmlp_params(l):
        return (b_in_f[l], gln_g_f[l], gln_b_f[l], gmlp_w_s[l], gmlp_b_s[l][:, :, None])

    def gmlp_fwd(h, l, after=None):
        w_in, l_in = weight("gmlp_w_in", l)
        w_out, l_out = weight("gmlp_w_out", l)
        n = w_in.shape[-1]
        zpre = _matmul("gmlp_in_fwd", _as4(h), w_in, (N_DEV, 1, t, n), F32, lb=l_in, after=after)
        gated = _gmlp_mid_fwd(shard_act(zpre), *gmlp_params(l))
        y = _matmul("gmlp_out_fwd", gated[:, None], w_out, (1, 1, t, d), F32, lb=l_out, b_merge=2, reduce=True)
        return y[0, 0], (zpre, gated)

    def gmlp_bwd(dy, h, saved, l):
        zpre, gated = saved
        w_in, l_in = weight("gmlp_w_in", l)
        w_out, l_out = weight("gmlp_w_out", l)
        n = w_in.shape[-1]
        dgated = _matmul("gmlp_out_bwd_a", _as4(dy), w_out, (half, 1, t, n), F32, lb=l_out, b_merge=2, tb=True)
        grad_into("gmlp_w_out", l, lambda buf, lo: _matmul(
            "gmlp_out_bwd_w", gated[:, None], _as4(dy), buf.shape, BF16, ta=True, lo=lo, out_merge=2, out_buf=buf))
        dz, dws, dbs, dlng, dlnb, dbin = _gmlp_mid_bwd(shard_act(zpre), shard_act(dgated), *gmlp_params(l))
        dh = _matmul("gmlp_in_bwd_a", dz[:, None], w_in, (1, 1, t, d), F32, lb=l_in, tb=True, reduce=True)
        grad_into("gmlp_w_in", l, lambda buf, lo: _matmul(
            "gmlp_in_bwd_w", _as4(h), dz[:, None], buf.shape, BF16, ta=True, lo=lo, out_buf=buf))
        small_grads = dict(gmlp_w_s=dws, gmlp_b_s=dbs[:, :, 0], gmlp_ln_g=dlng.reshape(-1),
                           gmlp_ln_b=dlnb.reshape(-1), gmlp_b_in=dbin.reshape(-1))
        return dh[0, 0], small_grads

    def attn_fwd(h, j, kvp, after=None):
        rel_vec = _rel_vector(rel_f[j])
        w_q, l_q = weight("attn_w_q", j)
        w_o, l_o = weight("attn_w_o", j)
        q = _matmul("attn_q_fwd", _as4(h), w_q, (1, 1, t, d), BF16, lb=l_q, b_merge=N_DEV, reduce=True,
                    after=after)[0, 0]
        o = _attn_fwd(q, kvp, rel_vec)
        y = _matmul("attn_o_fwd", _as4(o), w_o, (1, 1, t, d), F32, lb=l_o, b_merge=N_DEV, reduce=True)
        return y[0, 0], (q, o, rel_vec)

    def attn_bwd(dy, h, saved, j, kvp, dkv_acc):
        q, o, rel_vec = saved
        w_q, l_q = weight("attn_w_q", j)
        w_o, l_o = weight("attn_w_o", j)
        do = _matmul("attn_o_bwd_a", _as4(dy), w_o, (1, 1, t, d), BF16, lb=l_o, b_merge=N_DEV, tb=True)[0, 0]
        grad_into("attn_w_o", j, lambda buf, lo: _matmul(
            "attn_o_bwd_w", _as4(o), _as4(dy), buf.shape, BF16, ta=True, lo=lo, out_merge=N_DEV, out_buf=buf))
        dq, dk, dv, dsc = _attn_bwd(q, do, kvp, rel_vec, *dkv_acc)
        drel = _rel_bias_grad(_skew_diagonals(dsc))
        dh = _matmul("attn_q_bwd_a", _as4(dq), w_q, (1, 1, t, d), F32, lb=l_q, b_merge=N_DEV, tb=True)
        grad_into("attn_w_q", j, lambda buf, lo: _matmul(
            "attn_q_bwd_w", _as4(h), _as4(dq), buf.shape, BF16, ta=True, lo=lo, out_merge=N_DEV, out_buf=buf))
        return dh[0, 0], dict(attn_rel_bias=drel, dkv=(dk, dv))

    tape = []
    kvp = None
    kv_tape = None
    first_use = {(l, i): 3 * l + i for l in range(DEPTH) for i in range(3)}
    PASS_ON_EARLY_FROM = 7
    h = _modulate(x, mod[0, 1], mod[0, 0] + start_token)
    for l in range(DEPTH):
        for i in range(3):
            if (l, i) in first_use:
                land_group(first_use[l, i], x)
            nl, ni = (l, i + 1) if i < 2 else (l + 1, 0)
            ahead = first_use.get((nl, ni), 0)
            started = pass_on_early(ahead, x) if ahead >= PASS_ON_EARLY_FROM else None
            scl, gate = mod[l, 3 * i + 1], mod[l, 3 * i + 2]
            wgt = 1.0 if i == 1 else 0.5
            gw = wgt * (1.0 + gate)
            if i != 1:
                y, saved = ffn_fwd(h, 2 * l + i // 2, started)
            elif l < N_A:
                y, saved = gmlp_fwd(h, l, started)
            else:
                y, saved = attn_fwd(h, l - N_A, kvp, started)
            readers = [(mod[nl, 3 * ni + 1], mod[nl, 3 * ni])] if nl < DEPTH else []
            shared_kv = (l, i) == (N_A - 1, 2)
            if shared_kv:
                readers.append((mkv[1], mkv[0]))
            outs = _ln_res_fwd(x, y, gw, ln_g_f[l, i][None], ln_b_f[l, i][None], readers)
            tape.append((x, h, y, gw, scl, saved))
            x = outs[0]
            h = outs[1] if nl < DEPTH else None
            if shared_kv:
                hkv = outs[-1]
                w_kvg, l_kv = weight("w_kv", 0)
                n = w_kvg.shape[-1]
                kv = _matmul("kv_fwd", _as4(hkv), w_kvg, (N_DEV, 1, t, n), BF16, lb=l_kv)
                kvp = jnp.pad(shard_act(kv), ((0, 0), (LEFT_PAD, 0), (0, 0)))
                kv_tape = hkv

    loss_part, dx = _loss_head(x, target)
    loss = lax.psum(loss_part[0, 0], MESH_AXES)

    d_mod = [[None] * n_mod for _ in range(DEPTH)]
    d_ln_g = [[None] * 3 for _ in range(DEPTH)]
    d_ln_b = [[None] * 3 for _ in range(DEPTH)]
    small_grads = {k: [None] * N_A for k in ("gmlp_w_s", "gmlp_b_s", "gmlp_ln_g", "gmlp_ln_b", "gmlp_b_in")}
    d_rel = [None] * n_b
    dkv_acc = ()
    d_mkv = None
    reductions = [None] * DEPTH
    sent_token = None
    readers = []
    for l in reversed(range(DEPTH)):
        if l == N_A - 1:
            hkv = kv_tape
            w_kvg, l_kv = weight("w_kv", 0)
            dkv = jnp.concatenate(dkv_acc)[:, LEFT_PAD:, :].astype(BF16)[:, None]
            dhkv = _matmul("kv_bwd_a", dkv, w_kvg, (1, 1, t, d), F32, lb=l_kv, tb=True, reduce=True)[0, 0]
            grad_into("w_kv", 0, lambda buf, lo: _matmul(
                "kv_bwd_w", _as4(hkv), dkv, buf.shape, BF16, ta=True, lo=lo, out_buf=buf))
            readers.append((dhkv, mkv[1], None))
        for i in reversed(range(3)):
            x_in, h, y, gw, scl, saved = tape[3 * l + i]
            wgt = 1.0 if i == 1 else 0.5
            if sent_token is not None:
                gw = gw + sent_token
                sent_token = None
            res = _ln_res_bwd(x_in, y, gw, ln_g_f[l, i][None], ln_b_f[l, i][None], dx,
                              [(r[0], r[1]) for r in readers])
            dx_res, dy, dgw, dg, db = res[:5]
            for k, (_, _, slot) in enumerate(readers):
                dscl_k, dshift_k = res[5 + 2 * k][0], res[6 + 2 * k][0]
                if slot is None:
                    d_mkv = jnp.concatenate([dshift_k, dscl_k])
                else:
                    d_mod[slot[0]][slot[1]], d_mod[slot[0]][slot[1] + 1] = dshift_k, dscl_k
            d_ln_g[l][i], d_ln_b[l][i] = dg[0], db[0]
            if i != 1:
                dh, extra = ffn_bwd(dy, h, saved, 2 * l + i // 2)
            elif l < N_A:
                dh, extra = gmlp_bwd(dy, h, saved, l)
                for k, g in extra.items():
                    small_grads[k][l] = g
            else:
                dh, extra = attn_bwd(dy, h, saved, l - N_A, kvp, dkv_acc)
                d_rel[l - N_A] = extra["attn_rel_bias"]
                dkv_acc = extra["dkv"]
            d_mod[l][3 * i + 2] = wgt * dgw[0]
            dx = dx_res
            readers = [(dh, scl, (l, 3 * i))]
        if l > 0:
            srcs = [partial[l][k] for k in bwd_groups[l]]
            lands = [lax.empty((N_DEV - 1,) + s.shape[1:], BF16) for s in srcs]
            reductions[l] = _partials_send_start(f"partials_send_start_{l}", srcs, lands, True)
            sent_token = reductions[l][4][0, 0]
    (dh, scl, _), = readers
    dx, dscl, dshift = _mod_bwd(dx, dh, tape[0][0], scl)
    d_mod[0][0], d_mod[0][1] = dshift[0], dscl[0]
    grad_x = dx[None]

    d_mod_arr = jnp.stack([jnp.concatenate(r) for r in d_mod])
    small_part = dict(
        b_ada=d_mod_arr, b_ada_kv=d_mkv,
        ln_g=jnp.stack([jnp.stack(r) for r in d_ln_g]), ln_b=jnp.stack([jnp.stack(r) for r in d_ln_b]),
        gmlp_b_in=jnp.stack(small_grads["gmlp_b_in"]), gmlp_ln_g=jnp.stack(small_grads["gmlp_ln_g"]),
        gmlp_ln_b=jnp.stack(small_grads["gmlp_ln_b"]), gmlp_w_s=jnp.stack(small_grads["gmlp_w_s"]),
        gmlp_b_s=jnp.stack(small_grads["gmlp_b_s"]), attn_rel_bias=jnp.stack(d_rel))
    small_names = list(small_part)
    sp_shapes = [small_part[k].shape for k in small_names]
    sp_all = _all_gather_small(_pack([small_part[k] for k in small_names]),
                               after=[partial[0][k] for k in bwd_groups[0]])

    from_sibling = _partials_d2d([partial[0][k] for k in bwd_groups[0]])
    sums = [_chip_sum(partial[0][k], r1, core) for k, r1 in zip(bwd_groups[0], from_sibling)]
    lands = [lax.empty((len(_ALL_CHIPS) - 1,) + s.shape[1:], BF16) for s in sums]
    reductions[0] = _partials_send_start("partials_send_start_0", sums, lands, False, after=sp_all)
    sent_token = reductions[0][4][0, 0]
    c4 = c4 + sent_token

    sp_sum = _sum_parts(sp_all)
    full_grads = dict(zip(small_names, _unpack(sp_sum, sp_shapes)))
    per_dev = dict(zip(small_names, _unpack(sp_all, sp_shapes, lead=(N_DEV,))))

    def my_cols(a, width):
        return lax.dynamic_slice_in_dim(a, me * width, width, axis=a.ndim - 1)

    grads = {}
    grads["b_ada"] = full_grads["b_ada"]
    grads["b_ada_kv"] = full_grads["b_ada_kv"]
    grads["gmlp_w_s"] = full_grads["gmlp_w_s"]
    grads["gmlp_b_s"] = full_grads["gmlp_b_s"]
    for k in ("ln_g", "ln_b", "gmlp_b_in", "gmlp_ln_g", "gmlp_ln_b", "attn_rel_bias"):
        grads[k] = my_cols(full_grads[k], weights[k].shape[-1])

    dmod_cols = jnp.transpose(my_cols(per_dev["b_ada"], mod_w), (1, 0, 2))[:, None]
    grads["w_ada"] = _matmul("ada_bwd_w", c4, dmod_cols, (DEPTH, 1, d, mod_w), F32, ta=True,
                             a_silu=True)[:, 0]
    dkv_cols = my_cols(per_dev["b_ada_kv"], kv_w)[None, None]
    grads["w_ada_kv"] = _matmul("ada_kv_bwd_w", c4, dkv_cols, (1, 1, d, kv_w), F32, ta=True,
                                a_silu=True)[0, 0]

    delta, new_m, new_v = {}, {}, {}
    first = jnp.zeros((1,), jnp.int32)

    def flat2(a, cols):
        return a.reshape(-1, cols)

    done = None
    for k in ("w_ada", "w_ada_kv"):
        w = weights[k]
        cols = w.shape[-1]
        res = _adamw(grads[k].reshape(1, -1, cols), first, None, flat2(w, cols), flat2(mom1[k], cols),
                     flat2(mom2[k], cols), after=done)
        grads[k], delta[k], new_m[k], new_v[k] = (a.reshape(w.shape) for a in res)
        done = res[0][:SUBLANES, :LANES]

    tiny = [k for k in order if k not in delta and k not in big_names]
    tiny_shapes = [weights[k].shape for k in tiny]
    tiny_out = _adamw((_pack([grads[k] for k in tiny]) + sent_token)[None], first, None,
                      _pack([weights[k] for k in tiny]), _pack([mom1[k] for k in tiny]),
                      _pack([mom2[k] for k in tiny]), after=done)
    for dst, arr in zip((grads, delta, new_m, new_v), tiny_out):
        for k, val in zip(tiny, _unpack(arr, tiny_shapes)):
            dst[k] = val

    def opt_view(k, a):
        a = jnp.swapaxes(a, -1, -2) if k in swapped else a
        return a.reshape(-1, a.shape[-1])

    def opt_unview(k, a):
        s = weights[k].shape
        return jnp.swapaxes(a.reshape(s[:-2] + (s[-1], s[-2])), -1, -2) if k in swapped else a.reshape(s)

    bufs = {k: [lax.empty(opt_view(k, weights[k]).shape, F32) for _ in range(4)] for k in big_names}
    done = tiny_out[0]
    me_idx = me.astype(jnp.int32).reshape(1)
    for l in reversed(range(DEPTH)):
        send_sems, recv_sems, srcs, lands, _ = reductions[l]
        srcs, lands = _partials_send_wait(f"partials_send_wait_{l}", srcs, lands, send_sems, recv_sems, l > 0, done)
        for k, own, got in zip(bwd_groups[l], srcs, lands):
            cols = own.shape[-1]
            slot_rows = int(np.prod(own.shape[2:-1]))
            bufs[k] = _adamw(own.reshape(own.shape[0], -1, cols), me_idx if l > 0 else chip,
                             got.reshape(got.shape[0], -1, cols),
                             opt_view(k, weights[k]), opt_view(k, mom1[k]), opt_view(k, mom2[k]),
                             row0=bwd_groups[l][k][0] * slot_rows, bufs=bufs[k], after=done)
            done = bufs[k][0][:SUBLANES, :LANES]
    for k in big_names:
        grads[k], delta[k], new_m[k], new_v[k] = (opt_unview(k, b) for b in bufs[k])

    return (loss, grad_x, *[grads[k] for k in order], *[delta[k] for k in order],
            *[new_m[k] for k in order], *[new_v[k] for k in order])
```

```python
import numpy as np
import jax
import jax.numpy as jnp
from jax import lax
from jax.experimental import pallas as pl
from jax.experimental.pallas import tpu as pltpu

F32 = jnp.float32
BF16 = jnp.bfloat16
MESH_AXES = ("x", "y", "c")
N_DEV = 8
MESH_ID = pl.DeviceIdType.MESH

DEPTH = 4
N_A = 2
CHUNK = 64
N_HEADS = 16
LEFT_CHUNKS = 8
BAND = (LEFT_CHUNKS + 1) * CHUNK
LEFT_PAD = LEFT_CHUNKS * CHUNK
MAX_REL = 4 * CHUNK
N_REL = (CHUNK - 1) + MAX_REL + 1
GMLP_WINDOW = 128
GMLP_GROUPS = 8
ALPHA = (2.0 * DEPTH) ** 0.25
LN_EPS = 1e-5
ADAM_LR = 0.001
ADAM_B1 = 0.9
ADAM_B2 = 0.999
ADAM_EPS = 1e-08
ADAM_WD = 0.01
ADAM_STEP = 10

V7X_VMEM_BYTES = 64 * 1024 * 1024
VMEM_LIMIT = V7X_VMEM_BYTES - 8 * 1024 * 1024
LANES = 128
SUBLANES = 8
MM_BLOCK = 2048
BIG_ROW_BLOCK = 1024
ROW_BLOCK = 512
OPT_ROW_BLOCK = 256

_ANY = pl.BlockSpec(memory_space=pl.ANY)
_VMEM = pl.BlockSpec(memory_space=pltpu.VMEM)


def _params(sem=None):
    return pltpu.CompilerParams(dimension_semantics=sem, vmem_limit_bytes=VMEM_LIMIT)


def _row_block(rows, target):
    for d in range(min(rows, target), 0, -1):
        if rows % d == 0 and (d % SUBLANES == 0 or d == rows):
            return d
    return rows


def _matmul(name, a, b, out_shape4, out_dtype, *, la=0, lb=0, lo=0, ta=False, tb=False,
            reduce=False, b_merge=1, out_merge=1, out_buf=None, a_silu=False, after=None):
    ja_n, _, a_r, a_c = a.shape
    jb_n, _, b_r, b_c = b.shape
    jo_n, _, o_r, o_c = out_shape4
    m_tot = a_c if ta else a_r
    k_a = a_r if ta else a_c
    b_rows = b_merge * b_r
    k_c = b_c if tb else b_rows
    n = b_rows if tb else b_c
    n_chunks = (jb_n // b_merge) if reduce else 1
    natural_k = reduce and ja_n == 1
    assert n == o_c, (name, n, o_c)
    assert k_a ==(k_c * n_chunks if natural_k else k_c), (name, k_a, k_c, n_chunks)
    bk = k_c if (k_c <= MM_BLOCK or (b_merge > 1 and not tb)) else MM_BLOCK
    assert k_c % bk == 0
    nkk = k_c // bk
    kg = 2 if (reduce and ja_n > 1 and nkk == 1 and not ta and n_chunks % 2 == 0) else 1
    nk = n_chunks * nkk // kg
    m_out = out_merge * o_r
    assert m_tot == m_out, (name, m_tot, m_out)
    bm = m_tot if (m_tot <= MM_BLOCK or out_merge > 1) else MM_BLOCK
    assert m_tot % bm == 0
    jo_blocks = jo_n // out_merge

    def a_index(j, m, k):
        kj, kk = k // nkk, k % nkk
        ja = 0 if ja_n == 1 else (kj if reduce else j)
        ke = kk + kj * nkk if natural_k else kk
        return (ja, la, ke, m) if ta else (ja, la, m, ke)

    def b_index(j, m, k):
        kj, kk = k // nkk, k % nkk
        jb = 0 if jb_n == b_merge else (kj if reduce else j)
        return (jb, lb, 0, kk) if tb else (jb, lb, kk, 0)

    def o_index(j, m, k):
        return (j, lo, 0, 0) if out_merge > 1 else (j, lo, m, 0)

    a_block = (None, None, bk, bm) if ta else (None if kg == 1 else kg, None, bm, bk)
    if b_merge > 1:
        b_block = (kg * b_merge, None, b_r, bk if tb else n)
    else:
        b_block = (None if kg == 1 else kg, None) + ((n, bk) if tb else (bk, n))
    o_block = (out_merge, None, o_r, n) if out_merge > 1 else (None, None, bm, n)
    dims = (((0 if ta else 1,), (1 if tb else 0,)), ((), ()))

    in_place = nk > 1 and out_dtype == F32 and out_merge == 1
    use_acc = nk > 1 and not in_place

    def body(a_ref, b_ref, *rest):
        o_ref = rest[-2] if use_acc else rest[-1]
        k = pl.program_id(2)
        av = a_ref[...]
        if a_silu:
            af = av.astype(F32)
            av = af * jax.nn.sigmoid(af)
        bv = b_ref[...]
        if kg > 1:
            bv = bv.reshape(kg, -1, bv.shape[-1])
            prod = sum(lax.dot_general(av[g].astype(BF16), bv[g].astype(BF16), dims, preferred_element_type=F32)
                       for g in range(kg))
        else:
            if b_merge > 1:
                bv = bv.reshape(b_rows, bv.shape[-1])
            prod = lax.dot_general(av.astype(BF16), bv.astype(BF16), dims, preferred_element_type=F32)

        def emit(val):
            val = val.astype(out_dtype)
            o_ref[...] = val.reshape(out_merge, o_r, n) if out_merge > 1 else val

        if nk == 1:
            emit(prod)
            return
        acc_ref = o_ref if in_place else rest[-1]

        @pl.when(k == 0)
        def _():
            acc_ref[...] = prod

        @pl.when(k > 0)
        def _():
            acc_ref[...] += prod

        if use_acc:
            @pl.when(k == nk - 1)
            def _():
                emit(acc_ref[...])

    in_specs = [pl.BlockSpec(a_block, a_index), pl.BlockSpec(b_block, b_index)]
    operands = [a, b]
    aliases = {}
    if out_buf is not None:
        assert out_buf.shape == tuple(out_shape4) and out_buf.dtype == out_dtype
        in_specs.append(_ANY)
        operands.append(out_buf)
        aliases = {2: 0}
    if after is not None:
        in_specs.append(_ANY)
        operands.append(after)
    return pl.pallas_call(
        body, name=name,
        grid=(jo_blocks, m_tot // bm, nk),
        in_specs=in_specs,
        out_specs=pl.BlockSpec(o_block, o_index),
        out_shape=jax.ShapeDtypeStruct(tuple(out_shape4), out_dtype),
        scratch_shapes=[pltpu.VMEM((bm, n), F32)] if use_acc else [],
        input_output_aliases=aliases,
        compiler_params=_params(("parallel", "parallel", "arbitrary")),
    )(*operands)


def _as4(a):
    return a.reshape((1,) * (4 - a.ndim) + a.shape)


def _row_call(name, body, ins, outs, t, *, acc_outs=()):
    bt = _row_block(t, ROW_BLOCK)

    def spec(arr, tiled):
        if tiled:
            return pl.BlockSpec((bt,) + tuple(arr.shape[1:]), lambda i: (i,) + (0,) * (arr.ndim - 1))
        return pl.BlockSpec(tuple(arr.shape), lambda i: (0,) * arr.ndim)

    return pl.pallas_call(
        body, name=name, grid=(t // bt,),
        in_specs=[spec(a, tl) for a, tl in ins],
        out_specs=[spec(o, tl) for o, tl in outs],
        out_shape=[jax.ShapeDtypeStruct(o.shape, o.dtype) for o, _ in outs],
        compiler_params=_params(("arbitrary",) if acc_outs else ("parallel",)),
    )(*[a for a, _ in ins])


def _sds(shape, dtype):
    return jax.ShapeDtypeStruct(tuple(shape), dtype)


def _modulate(x, scl, shift):
    t, d = x.shape

    def body(x_ref, s_ref, b_ref, h_ref):
        h_ref[...] = (x_ref[...] * (1.0 + s_ref[...]) + b_ref[...]).astype(BF16)

    return _row_call("modulate", body, [(x, True), (scl, False), (shift, False)],
                     [(_sds((t, d), BF16), True)], t)[0]


def _ln_stats(r):
    mu = jnp.mean(r, axis=-1, keepdims=True)
    rc = r - mu
    var = jnp.mean(rc * rc, axis=-1, keepdims=True)
    rstd = lax.rsqrt(var + LN_EPS)
    return rc * rstd, rstd


def _ln_res_fwd(x, y, gw, g, b, mods=()):
    t, d = x.shape
    n_mod = len(mods)

    def body(x_ref, y_ref, gw_ref, g_ref, b_ref, *rest):
        mod_refs, o_ref, h_refs = rest[:2 * n_mod], rest[2 * n_mod], rest[2 * n_mod + 1:]
        r = ALPHA * x_ref[...] + gw_ref[...] * y_ref[...]
        xhat, _ = _ln_stats(r)
        xn = xhat * g_ref[...] + b_ref[...]
        o_ref[...] = xn
        for k in range(n_mod):
            h_refs[k][...] = (xn * (1.0 + mod_refs[2 * k][...]) + mod_refs[2 * k + 1][...]).astype(BF16)

    vecs = [(v, False) for pair in mods for v in pair]
    return _row_call("ln_res_fwd", body,
                     [(x, True), (y, True), (gw, False), (g, False), (b, False)] + vecs,
                     [(_sds((t, d), F32), True)] + [(_sds((t, d), BF16), True)] * n_mod, t)


def _ln_res_bwd(x, y, gw, g, b, dx_base, pairs=()):
    t, d = x.shape
    n_pair = len(pairs)

    def body(x_ref, y_ref, gw_ref, g_ref, b_ref, dxb_ref, *rest):
        pair_refs, outs = rest[:2 * n_pair], rest[2 * n_pair:]
        dx_ref, dy_ref = outs[0], outs[1]
        sums = outs[2:]

        @pl.when(pl.program_id(0) == 0)
        def _():
            for r in sums:
                r[...] = jnp.zeros_like(r)

        yv = y_ref[...]
        gwv = gw_ref[...]
        gv = g_ref[...]
        xhat, rstd = _ln_stats(ALPHA * x_ref[...] + gwv * yv)
        dxn = dxb_ref[...]
        if n_pair:
            xn = xhat * gv + b_ref[...]
            for k in range(n_pair):
                dh = pair_refs[2 * k][...]
                dxn = dxn + dh * (1.0 + pair_refs[2 * k + 1][...])
                sums[3 + 2 * k][...] += jnp.sum(dh * xn, axis=0, keepdims=True)
                sums[4 + 2 * k][...] += jnp.sum(dh, axis=0, keepdims=True)
        dxh = dxn * gv
        m1 = jnp.mean(dxh, axis=-1, keepdims=True)
        m2 = jnp.mean(dxh * xhat, axis=-1, keepdims=True)
        dr = rstd * (dxh - m1 - xhat * m2)
        dx_ref[...] = ALPHA * dr
        dy_ref[...] = (gwv * dr).astype(BF16)
        sums[0][...] += jnp.sum(dr * yv, axis=0, keepdims=True)
        sums[1][...] += jnp.sum(dxn * xhat, axis=0, keepdims=True)
        sums[2][...] += jnp.sum(dxn, axis=0, keepdims=True)

    vec = _sds((1, d), F32)
    n_sum = 3 + 2 * n_pair
    ins = [(x, True), (y, True), (gw, False), (g, False), (b, False), (dx_base, True)]
    for dh, scl in pairs:
        ins += [(dh, True), (scl, False)]
    return _row_call("ln_res_bwd", body, ins,
                     [(_sds((t, d), F32), True), (_sds((t, d), BF16), True)] + [(vec, False)] * n_sum, t,
                     acc_outs=tuple(range(2, 2 + n_sum)))


def _mod_bwd(dx_res, dh, x, scl):
    t, d = x.shape

    def body(dxr_ref, dh_ref, x_ref, s_ref, dx_ref, ds_ref, db_ref):
        @pl.when(pl.program_id(0) == 0)
        def _():
            ds_ref[...] = jnp.zeros_like(ds_ref)
            db_ref[...] = jnp.zeros_like(db_ref)

        dh = dh_ref[...]
        dx_ref[...] = dxr_ref[...] + dh * (1.0 + s_ref[...])
        ds_ref[...] += jnp.sum(dh * x_ref[...], axis=0, keepdims=True)
        db_ref[...] += jnp.sum(dh, axis=0, keepdims=True)

    vec = _sds((1, d), F32)
    return _row_call("mod_bwd", body, [(dx_res, True), (dh, True), (x, True), (scl, False)],
                     [(_sds((t, d), F32), True), (vec, False), (vec, False)], t, acc_outs=(1, 2))


def _loss_head(y, target):
    t, d = y.shape

    def body(y_ref, t_ref, l_ref, dy_ref):
        @pl.when(pl.program_id(0) == 0)
        def _():
            l_ref[...] = jnp.zeros_like(l_ref)

        err = y_ref[...] - t_ref[...]
        dy_ref[...] = err * (1.0 / d)
        part = 0.5 * jnp.sum(jnp.mean(err * err, axis=-1, keepdims=True), axis=0, keepdims=True)
        l_ref[...] += jnp.broadcast_to(part, l_ref.shape)

    return _row_call("loss_head", body, [(y, True), (target, True)],
                     [(_sds((SUBLANES, LANES), F32), False), (_sds((t, d), F32), True)], t,
                     acc_outs=(0,))


def _sigmoid(x):
    return 0.5 * jnp.tanh(0.5 * x) + 0.5


def _ffn_up_fwd(h, w_gu, lb, after=None):
    t, d = h.shape
    n = w_gu.shape[-1]
    half = N_DEV // 2
    bt = _row_block(t, BIG_ROW_BLOCK)
    extra = [] if after is None else [after]

    def body(h_ref, wg_ref, wu_ref, *rest):
        fac_ref, a_ref = rest[-2:]
        hv = h_ref[...]
        g = jnp.dot(hv, wg_ref[...], preferred_element_type=F32)
        u = jnp.dot(hv, wu_ref[...], preferred_element_type=F32)
        sig = _sigmoid(g)
        silu = g * sig
        fac_ref[0] = u * (sig + silu * (1.0 - sig))
        fac_ref[1] = silu
        a_ref[...] = (silu * u).astype(BF16)

    return pl.pallas_call(
        body, name="ffn_up_fwd", grid=(half, t // bt),
        in_specs=[pl.BlockSpec((bt, d), lambda j, i: (i, 0)),
                  pl.BlockSpec((None, None, d, n), lambda j, i: (j, lb, 0, 0)),
                  pl.BlockSpec((None, None, d, n), lambda j, i: (half + j, lb, 0, 0))] + [_ANY] * len(extra),
        out_specs=[pl.BlockSpec((2, None, bt, n), lambda j, i: (0, j, i, 0)),
                   pl.BlockSpec((None, bt, n), lambda j, i: (j, i, 0))],
        out_shape=[_sds((2, half, t, n), F32), _sds((half, t, n), BF16)],
        compiler_params=_params(("parallel", "parallel")),
    )(h, w_gu, w_gu, *extra)


def _ffn_down_bwd_a(dy, w_down, lb, fac):
    t, d = dy.shape
    _, half, _, n = fac.shape
    r = w_down.shape[2]
    bt = _row_block(t, MM_BLOCK)

    def body(dy_ref, w_ref, fac_ref, d_ref):
        da = lax.dot_general(dy_ref[...], w_ref[...].reshape(2 * r, d), (((1,), (1,)), ((), ())),
                             preferred_element_type=F32)
        d_ref[0] = (da * fac_ref[0]).astype(BF16)
        d_ref[1] = (da * fac_ref[1]).astype(BF16)

    return pl.pallas_call(
        body, name="ffn_down_bwd_a", grid=(half, t // bt),
        in_specs=[pl.BlockSpec((bt, d), lambda j, i: (i, 0)),
                  pl.BlockSpec((2, None, r, d), lambda j, i: (j, lb, 0, 0)),
                  pl.BlockSpec((2, None, bt, n), lambda j, i: (0, j, i, 0))],
        out_specs=pl.BlockSpec((2, None, bt, n), lambda j, i: (0, j, i, 0)),
        out_shape=_sds((2, half, t, n), BF16),
        compiler_params=_params(("parallel", "parallel")),
    )(dy, w_down, fac)


_INV_SQRT2 = 0.7071067811865476
_INV_SQRT_2PI = 0.3989422804014327


def _gelu(z):
    return 0.5 * z * (1.0 + lax.erf(z * _INV_SQRT2))


def _gelu_grad(z):
    return 0.5 * (1.0 + lax.erf(z * _INV_SQRT2)) + z * jnp.exp(-0.5 * z * z) * _INV_SQRT_2PI


def _window_mask():
    t_out = lax.broadcasted_iota(jnp.int32, (GMLP_WINDOW, GMLP_WINDOW), 0)
    s_in = lax.broadcasted_iota(jnp.int32, (GMLP_WINDOW, GMLP_WINDOW), 1)
    return (s_in // CHUNK) <= (t_out // CHUNK)


def _gmlp_recompute(z_ref, bin_ref, lng_ref, lnb_ref):
    half = N_DEV // 2
    z = z_ref[...] + bin_ref[...]
    ge = _gelu(z)
    u = ge[:half]
    v = ge[half:]
    width = half * v.shape[-1]
    mu = jnp.sum(jnp.sum(v, axis=0), axis=-1, keepdims=True) / width
    vc = v - mu
    var = jnp.sum(jnp.sum(vc * vc, axis=0), axis=-1, keepdims=True) / width
    rstd = lax.rsqrt(var + LN_EPS)
    xhat = vc * rstd
    vn = xhat * lng_ref[...] + lnb_ref[...]
    return z, u, xhat, rstd, vn


def _gmlp_mid_fwd(zpre, b_in, ln_g, ln_b, w_s, b_s):
    _, t, n = zpre.shape
    half = N_DEV // 2
    gd = half * n // GMLP_GROUPS
    per = n // gd
    w = GMLP_WINDOW

    def body(z_ref, bin_ref, lng_ref, lnb_ref, ws_ref, bs_ref, o_ref):
        _, u, _, _, vn = _gmlp_recompute(z_ref, bin_ref, lng_ref, lnb_ref)
        mask = _window_mask()
        for g in range(GMLP_GROUPS):
            sh, c0 = g // per, (g % per) * gd
            wsm = jnp.where(mask, ws_ref[g], 0.0).astype(BF16)
            s = jnp.dot(wsm, vn[sh][:, c0:c0 + gd].astype(BF16), preferred_element_type=F32) + bs_ref[g]
            o_ref[sh, :, c0:c0 + gd] = (u[sh][:, c0:c0 + gd] * s).astype(BF16)

    whole = lambda a: pl.BlockSpec(tuple(a.shape), lambda i: (0,) * a.ndim)
    return pl.pallas_call(
        body, name="gmlp_mid_fwd", grid=(t // w,),
        in_specs=[pl.BlockSpec((N_DEV, w, n), lambda i: (0, i, 0)),
                  whole(b_in), whole(ln_g), whole(ln_b), whole(w_s), whole(b_s)],
        out_specs=pl.BlockSpec((half, w, n), lambda i: (0, i, 0)),
        out_shape=_sds((half, t, n), BF16),
        compiler_params=_params(("parallel",)),
    )(zpre, b_in, ln_g, ln_b, w_s, b_s)


def _gmlp_mid_bwd(zpre, dgated, b_in, ln_g, ln_b, w_s, b_s):
    _, t, n = zpre.shape
    half = N_DEV // 2
    gd = half * n // GMLP_GROUPS
    per = n // gd
    w = GMLP_WINDOW
    width = half * n

    def body(z_ref, dg_ref, bin_ref, lng_ref, lnb_ref, ws_ref, bs_ref,
             dz_ref, dws_ref, dbs_ref, dlng_ref, dlnb_ref, dbin_ref, du_ref, dvn_ref):
        @pl.when(pl.program_id(0) == 0)
        def _():
            for r in (dws_ref, dbs_ref, dlng_ref, dlnb_ref, dbin_ref):
                r[...] = jnp.zeros_like(r)

        z, u, xhat, rstd, vn = _gmlp_recompute(z_ref, bin_ref, lng_ref, lnb_ref)
        mask = _window_mask()
        for g in range(GMLP_GROUPS):
            sh, c0 = g // per, (g % per) * gd
            wsm = jnp.where(mask, ws_ref[g], 0.0).astype(BF16)
            vg = vn[sh][:, c0:c0 + gd].astype(BF16)
            s = jnp.dot(wsm, vg, preferred_element_type=F32) + bs_ref[g]
            dgt = dg_ref[sh, :, c0:c0 + gd]
            ds = dgt * u[sh][:, c0:c0 + gd]
            du_ref[sh, :, c0:c0 + gd] = dgt * s
            dsb = ds.astype(BF16)
            dws = lax.dot_general(dsb, vg, (((1,), (1,)), ((), ())), preferred_element_type=F32)
            dws_ref[g] += jnp.where(mask, dws, 0.0)
            dbs_ref[g] += jnp.sum(ds, axis=-1, keepdims=True)
            dvn_ref[sh, :, c0:c0 + gd] = lax.dot_general(wsm, dsb, (((0,), (0,)), ((), ())),
                                                         preferred_element_type=F32)
        dvn = dvn_ref[...]
        dlng_ref[...] += jnp.sum(dvn * xhat, axis=1, keepdims=True)
        dlnb_ref[...] += jnp.sum(dvn, axis=1, keepdims=True)
        dxh = dvn * lng_ref[...]
        m1 = jnp.sum(jnp.sum(dxh, axis=0), axis=-1, keepdims=True) / width
        m2 = jnp.sum(jnp.sum(dxh * xhat, axis=0), axis=-1, keepdims=True) / width
        dv = rstd * (dxh - m1 - xhat * m2)
        gg = _gelu_grad(z)
        dzu = du_ref[...] * gg[:half]
        dzv = dv * gg[half:]
        dz_ref[:half] = dzu.astype(BF16)
        dz_ref[half:] = dzv.astype(BF16)
        dbin_ref[:half] += jnp.sum(dzu, axis=1, keepdims=True)
        dbin_ref[half:] += jnp.sum(dzv, axis=1, keepdims=True)

    whole = lambda a: pl.BlockSpec(tuple(a.shape), lambda i: (0,) * a.ndim)
    outs = [_sds((N_DEV, t, n), BF16), _sds(w_s.shape, F32), _sds(b_s.shape, F32),
            _sds(ln_g.shape, F32), _sds(ln_b.shape, F32), _sds(b_in.shape, F32)]
    return pl.pallas_call(
        body, name="gmlp_mid_bwd", grid=(t // w,),
        in_specs=[pl.BlockSpec((N_DEV, w, n), lambda i: (0, i, 0)),
                  pl.BlockSpec((half, w, n), lambda i: (0, i, 0)),
                  whole(b_in), whole(ln_g), whole(ln_b), whole(w_s), whole(b_s)],
        out_specs=[pl.BlockSpec((N_DEV, w, n), lambda i: (0, i, 0))] + [whole(o) for o in outs[1:]],
        out_shape=outs,
        scratch_shapes=[pltpu.VMEM((half, w, n), F32), pltpu.VMEM((half, w, n), F32)],
        compiler_params=_params(("arbitrary",)),
    )(zpre, dgated, b_in, ln_g, ln_b, w_s, b_s)


ATTN_CHUNKS = 4
ATTN_ROWS = ATTN_CHUNKS * CHUNK
ATTN_WINDOW = ATTN_ROWS + LEFT_PAD
ATTN_DIAGS = -(-(ATTN_ROWS + ATTN_WINDOW - 1) // LANES) * LANES
ATTN_ROLL = ATTN_DIAGS - (ATTN_ROWS - 1)


def _rel_vector(rel):
    j = np.arange(ATTN_DIAGS)
    idx = np.clip(ATTN_WINDOW - 1 - j, -(CHUNK - 1), MAX_REL) + (CHUNK - 1)
    return rel[:, idx]


def _attn_bias_mask(rel_ref, bm_ref):
    tt = lax.broadcasted_iota(jnp.int32, (ATTN_ROWS, ATTN_WINDOW), 0) // CHUNK
    rr = lax.broadcasted_iota(jnp.int32, (ATTN_ROWS, ATTN_WINDOW), 1) // CHUNK
    band = (rr >= tt) & (rr <= tt + LEFT_CHUNKS)
    for j in range(bm_ref.shape[0]):
        vec = jnp.broadcast_to(rel_ref[j:j + 1, :], (ATTN_ROWS, ATTN_DIAGS))
        toeplitz = pltpu.roll(vec, ATTN_ROLL, 1, stride=1, stride_axis=0)[:, :ATTN_WINDOW]
        bm_ref[j] = jnp.where(band, toeplitz, -jnp.inf)


def _attn_probs(q_ref, k_ref, bm_ref, j, hd, start, valid):
    qh = q_ref[:, j * hd:(j + 1) * hd]
    kb = k_ref[pl.ds(start, ATTN_WINDOW), j * hd:(j + 1) * hd]
    sc = lax.dot_general(qh, kb, (((1,), (1,)), ((), ())), preferred_element_type=F32)
    sc = sc * (hd ** -0.5) + bm_ref[j]
    sc = jnp.where(valid, sc, -jnp.inf)
    sc = sc - jnp.max(sc, axis=-1, keepdims=True)
    e = jnp.exp(sc)
    return e / jnp.sum(e, axis=-1, keepdims=True), qh, kb


def _window_valid(start):
    r = lax.broadcasted_iota(jnp.int32, (1, ATTN_WINDOW), 1)
    return (start + r) >= LEFT_PAD


def _attn_fwd(q, kvp, rel_vec):
    t, d = q.shape
    hd = d // N_HEADS
    half = N_DEV // 2
    n = kvp.shape[-1]
    per = n // hd
    rows = kvp.shape[1]

    def body(q_ref, k_ref, v_ref, rel_ref, o_ref, bm_ref):
        @pl.when(pl.program_id(1) == 0)
        def _():
            _attn_bias_mask(rel_ref, bm_ref)

        start = pl.multiple_of(pl.program_id(1) * ATTN_ROWS, ATTN_ROWS)
        valid = _window_valid(start)
        for j in range(per):
            p, _, _ = _attn_probs(q_ref, k_ref, bm_ref, j, hd, start, valid)
            vb = v_ref[pl.ds(start, ATTN_WINDOW), j * hd:(j + 1) * hd]
            o_ref[:, j * hd:(j + 1) * hd] = jnp.dot(p.astype(BF16), vb, preferred_element_type=F32).astype(BF16)

    return pl.pallas_call(
        body, name="attn_fwd", grid=(half, t // ATTN_ROWS),
        in_specs=[pl.BlockSpec((ATTN_ROWS, n), lambda g, i: (i, g)),
                  pl.BlockSpec((None, rows, n), lambda g, i: (g, 0, 0)),
                  pl.BlockSpec((None, rows, n), lambda g, i: (half + g, 0, 0)),
                  pl.BlockSpec((None, per, ATTN_DIAGS), lambda g, i: (g, 0, 0))],
        out_specs=pl.BlockSpec((ATTN_ROWS, n), lambda g, i: (i, g)),
        out_shape=_sds((t, d), BF16),
        scratch_shapes=[pltpu.VMEM((per, ATTN_ROWS, ATTN_WINDOW), F32)],
        compiler_params=_params(("arbitrary", "arbitrary")),
    )(q, kvp, kvp, rel_vec.reshape(half, per, ATTN_DIAGS))


def _attn_bwd(q, dout, kvp, rel_vec, dk_in=None, dv_in=None):
    t, d = q.shape
    hd = d // N_HEADS
    half = N_DEV // 2
    n = kvp.shape[-1]
    per = n // hd
    rows = kvp.shape[1]
    scale = hd ** -0.5
    carry = dk_in is not None

    def body(q_ref, do_ref, k_ref, v_ref, rel_ref, *rest):
        dq_ref, dk_ref, dv_ref, dsc_ref, bm_ref = rest[-5:]

        @pl.when(pl.program_id(1) == 0)
        def _():
            _attn_bias_mask(rel_ref, bm_ref)
            dk_ref[...] = rest[0][...] if carry else jnp.zeros_like(dk_ref)
            dv_ref[...] = rest[1][...] if carry else jnp.zeros_like(dv_ref)
            dsc_ref[...] = jnp.zeros_like(dsc_ref)

        start = pl.multiple_of(pl.program_id(1) * ATTN_ROWS, ATTN_ROWS)
        valid = _window_valid(start)
        for j in range(per):
            cols = slice(j * hd, (j + 1) * hd)
            p, qh, kb = _attn_probs(q_ref, k_ref, bm_ref, j, hd, start, valid)
            vb = v_ref[pl.ds(start, ATTN_WINDOW), cols]
            doh = do_ref[:, cols]
            dp = lax.dot_general(doh, vb, (((1,), (1,)), ((), ())), preferred_element_type=F32)
            ds = p * (dp - jnp.sum(dp * p, axis=-1, keepdims=True))
            dsc_ref[j] += sum(ds[a * CHUNK:(a + 1) * CHUNK, a * CHUNK:a * CHUNK + BAND]
                              for a in range(ATTN_CHUNKS))
            dsb = (ds * scale).astype(BF16)
            dq_ref[:, cols] = jnp.dot(dsb, kb, preferred_element_type=F32).astype(BF16)
            dk_ref[pl.ds(start, ATTN_WINDOW), cols] += lax.dot_general(
                dsb, qh, (((0,), (0,)), ((), ())), preferred_element_type=F32)
            dv_ref[pl.ds(start, ATTN_WINDOW), cols] += lax.dot_general(
                p.astype(BF16), doh, (((0,), (0,)), ((), ())), preferred_element_type=F32)

    tile = pl.BlockSpec((ATTN_ROWS, n), lambda g, i: (i, g))
    shard = pl.BlockSpec((None, rows, n), lambda g, i: (g, 0, 0))
    in_specs = [tile, tile, shard, pl.BlockSpec((None, rows, n), lambda g, i: (half + g, 0, 0)),
                pl.BlockSpec((None, per, ATTN_DIAGS), lambda g, i: (g, 0, 0))]
    operands = [q, dout, kvp, kvp, rel_vec.reshape(half, per, ATTN_DIAGS)]
    if carry:
        in_specs += [shard, shard]
        operands += [dk_in, dv_in]
    acc = _sds((half, rows, n), F32)
    return pl.pallas_call(
        body, name="attn_bwd", grid=(half, t // ATTN_ROWS),
        in_specs=in_specs,
        out_specs=[tile, shard, shard, pl.BlockSpec((per, CHUNK, BAND), lambda g, i: (g, 0, 0))],
        out_shape=[_sds((t, d), BF16), acc, acc, _sds((N_HEADS, CHUNK, BAND), F32)],
        scratch_shapes=[pltpu.VMEM((per, ATTN_ROWS, ATTN_WINDOW), F32)],
        compiler_params=_params(("arbitrary", "arbitrary")),
    )(*operands)


SKEW_PITCH = 640
SKEW = SKEW_PITCH + 1
SKEW_LANES = -(-SKEW // LANES) * LANES


def _skew_diagonals(dsc):
    h = dsc.shape[0]
    wide = jnp.pad(dsc, ((0, 0), (0, 0), (0, SKEW_PITCH - BAND))).reshape(h, CHUNK * SKEW_PITCH)
    wide = jnp.pad(wide, ((0, 0), (0, CHUNK))).reshape(h, CHUNK, SKEW)
    return jnp.pad(wide, ((0, 0), (0, 0), (0, SKEW_LANES - SKEW)))


def _rel_bias_grad(skewed):
    heads = skewed.shape[0]
    hb = SUBLANES

    def body(d_ref, o_ref):
        col = lax.broadcasted_iota(jnp.int32, (SKEW_LANES, N_REL), 0)
        bucket = lax.broadcasted_iota(jnp.int32, (SKEW_LANES, N_REL), 1)
        diag = jnp.where(col < BAND, col, col - SKEW)
        idx = jnp.clip(LEFT_PAD - diag, -(CHUNK - 1), MAX_REL) + (CHUNK - 1)
        oh = ((idx == bucket) & (col < SKEW)).astype(BF16)
        dv = jnp.sum(d_ref[...], axis=1)
        hi = dv.astype(BF16)
        rest = dv - hi.astype(F32)
        mid = rest.astype(BF16)
        lo = (rest - mid.astype(F32)).astype(BF16)
        acc = jnp.dot(hi, oh, preferred_element_type=F32)
        acc += jnp.dot(mid, oh, preferred_element_type=F32)
        acc += jnp.dot(lo, oh, preferred_element_type=F32)
        o_ref[...] = acc

    return pl.pallas_call(
        body, name="rel_bias_grad", grid=(heads // hb,),
        in_specs=[pl.BlockSpec((hb, CHUNK, SKEW_LANES), lambda i: (i, 0, 0))],
        out_specs=pl.BlockSpec((hb, N_REL), lambda i: (i, 0)),
        out_shape=_sds((heads, N_REL), F32),
        compiler_params=_params(("parallel",)),
    )(skewed)


def _sum_parts(parts):
    s_n, rows, c = parts.shape
    br = _row_block(rows, OPT_ROW_BLOCK)

    def body(p_ref, o_ref):
        acc = p_ref[0].astype(F32)
        for s in range(1, s_n):
            acc = acc + p_ref[s].astype(F32)
        o_ref[...] = acc

    return pl.pallas_call(
        body, name="sum_parts", grid=(rows // br,),
        in_specs=[pl.BlockSpec((s_n, br, c), lambda i: (0, i, 0))],
        out_specs=pl.BlockSpec((br, c), lambda i: (i, 0)),
        out_shape=_sds((rows, c), F32),
        compiler_params=_params(("parallel",)),
    )(parts)


def _adamw(own, own_idx, parts, w, m, v, row0=0, bufs=None, after=None):
    _, rows, c = own.shape
    s_n = 0 if parts is None else parts.shape[0]
    total = w.shape[0]
    br = _row_block(rows, OPT_ROW_BLOCK)
    assert row0 % br == 0 and (bufs is not None or (row0 == 0 and total == rows))
    b0 = row0 // br
    m_corr = 1.0 - ADAM_B1 ** ADAM_STEP
    v_corr = 1.0 - ADAM_B2 ** ADAM_STEP

    def body(idx_ref, own_ref, *refs):
        if s_n:
            p_ref, refs = refs[0], refs[1:]
        w_ref, m_ref, v_ref = refs[:3]
        g_ref, d_ref, nm_ref, nv_ref, done_ref = refs[-5:]
        done_ref[...] = jnp.zeros_like(done_ref)
        g = own_ref[...].astype(F32)
        for s in range(s_n):
            g = g + p_ref[s].astype(F32)
        nm = ADAM_B1 * m_ref[...] + (1.0 - ADAM_B1) * g
        nv = ADAM_B2 * v_ref[...] + (1.0 - ADAM_B2) * (g * g)
        g_ref[...] = g
        nm_ref[...] = nm
        nv_ref[...] = nv
        d_ref[...] = -ADAM_LR * ((nm / m_corr) / (jnp.sqrt(nv / v_corr) + ADAM_EPS) + ADAM_WD * w_ref[...])

    tile = pl.BlockSpec((br, c), lambda i, idx: (i + b0, 0))
    in_specs = [pl.BlockSpec((None, br, c), lambda i, idx: (idx[0], i, 0))]
    operands = [own_idx, own]
    if s_n:
        in_specs.append(pl.BlockSpec((s_n, br, c), lambda i, idx: (0, i, 0)))
        operands.append(parts)
    in_specs += [tile, tile, tile]
    operands += [w, m, v]
    aliases = {}
    if bufs is not None:
        aliases = {len(operands) + j: j for j in range(4)}
        in_specs += [_ANY] * 4
        operands += list(bufs)
    if after is not None:
        in_specs.append(_ANY)
        operands.append(after)
    out = _sds((total, c), F32)
    return pl.pallas_call(
        body, name="adamw",
        grid_spec=pltpu.PrefetchScalarGridSpec(
            num_scalar_prefetch=1, grid=(rows // br,), in_specs=in_specs,
            out_specs=[tile, tile, tile, tile,
                       pl.BlockSpec((SUBLANES, LANES), lambda i, idx: (0, 0))]),
        out_shape=[out, out, out, out, _token()],
        input_output_aliases=aliases,
        compiler_params=_params(("arbitrary",)),
    )(*operands)


def _chip_sum(p, r1, core):
    half = N_DEV // 2
    c = p.shape[-1]
    rows = int(np.prod(p.shape[1:-1]))
    br = _row_block(rows, BIG_ROW_BLOCK)

    def body(core_ref, p_ref, r_ref, o_ref):
        o_ref[...] = (p_ref[...].astype(F32) + r_ref[...].astype(F32)).astype(BF16)

    out = pl.pallas_call(
        body, name="chip_sum",
        grid_spec=pltpu.PrefetchScalarGridSpec(
            num_scalar_prefetch=1, grid=(half, rows // br),
            in_specs=[pl.BlockSpec((None, None, br, c), lambda q, i, cr: (q, cr[0], i, 0)),
                      pl.BlockSpec((None, br, c), lambda q, i, cr: (q, i, 0))],
            out_specs=pl.BlockSpec((None, br, c), lambda q, i, cr: (q, i, 0))),
        out_shape=_sds((half, rows, c), BF16),
        compiler_params=_params(("parallel", "parallel")),
    )(core, p.reshape(half, 2, rows, c), r1.reshape(half, rows, c))
    return out.reshape((half,) + p.shape[1:])


def _position():
    return tuple(lax.axis_index(a) for a in MESH_AXES)


def _linear(px, py, pc):
    return 4 * px + 2 * py + pc


def _all_gather_small(v, after=()):
    rows, lanes = v.shape

    def body(x_ref, *rest):
        out_ref, send_sems, recv_sems, local_sem = rest[-4:]
        x, y, c = _position()
        me, sibling = (x, y, c), (x, y, 1 - c)
        chips = [(1 - x, y), (x, 1 - y), (1 - x, 1 - y)]

        def copy(k, block, to, src=None):
            dst = out_ref.at[_linear(*block)]
            return pltpu.make_async_remote_copy(
                src_ref=dst if src is None else src, dst_ref=dst,
                send_sem=send_sems.at[k], recv_sem=recv_sems.at[k],
                device_id=to, device_id_type=MESH_ID)

        mine = pltpu.make_async_copy(x_ref, out_ref.at[_linear(*me)], local_sem)
        mine.start()
        first = [copy(0, me, sibling, src=x_ref)]
        first += [copy(1 + j, me, (*chip, c), src=x_ref) for j, chip in enumerate(chips)]
        for cp in first:
            cp.start()
        passed = [copy(4 + j, (*chip, c), sibling) for j, chip in enumerate(chips)]
        for j, chip in enumerate(chips):
            copy(1 + j, (*chip, c), me).wait_recv()
            passed[j].start()
        copy(0, sibling, me).wait_recv()
        for j, chip in enumerate(chips):
            copy(4 + j, (*chip, 1 - c), me).wait_recv()
        for cp in first + passed:
            cp.wait_send()
        mine.wait()

    return pl.pallas_call(
        body, name="all_gather_small",
        out_shape=_sds((N_DEV, rows, lanes), v.dtype),
        in_specs=[_VMEM] + [_ANY] * len(after), out_specs=_VMEM,
        scratch_shapes=[pltpu.SemaphoreType.DMA((7,)), pltpu.SemaphoreType.DMA((7,)),
                        pltpu.SemaphoreType.DMA],
        compiler_params=pltpu.CompilerParams(vmem_limit_bytes=VMEM_LIMIT),
    )(v, *after)


_HBM = pl.BlockSpec(memory_space=pltpu.HBM)
_SEM = pl.BlockSpec(memory_space=pltpu.SEMAPHORE)
_EFFECT = pltpu.SideEffectType.DATAFLOW_SIDE_EFFECTING
_ALL_CHIPS = [(0, 0), (0, 1), (1, 0), (1, 1)]


def _other_chips(x, y):
    return [(1 - x, y), (x, 1 - y), (1 - x, 1 - y)]


def _in_hbm(a):
    return pltpu.with_memory_space_constraint(a, pltpu.HBM)


def _token():
    return _sds((SUBLANES, LANES), F32)


def _gather_copies(refs, send_sems, recv_sems, to_sibling):
    x, y, c = _position()
    if to_sibling:
        plan = [((x, y, 1 - c), _linear(*q, c), _linear(*q, 1 - c)) for q in _ALL_CHIPS]
    else:
        plan = [((*ch, c), _linear(x, y, c), _linear(*ch, c)) for ch in _other_chips(x, y)]

    def copy(ref, i, k, peer, block):
        return pltpu.make_async_remote_copy(
            src_ref=ref.at[block], dst_ref=ref.at[block],
            send_sem=send_sems.at[len(plan) * i + k], recv_sem=recv_sems.at[len(plan) * i + k],
            device_id=peer, device_id_type=MESH_ID)

    return [(copy(ref, i, k, peer, sent), copy(ref, i, k, peer, landed))
            for i, ref in enumerate(refs) for k, (peer, sent, landed) in enumerate(plan)]


def _gather_ici_start(name, lands, after=None, to_sibling=False):
    n = len(lands)
    extra = [] if after is None else [after]
    n_sem = n * (len(_ALL_CHIPS) if to_sibling else len(_ALL_CHIPS) - 1)

    def body(*refs):
        send_sems, recv_sems, token = refs[-n - 3], refs[-n - 2], refs[-1]
        for sent, _ in _gather_copies(refs[:n], send_sems, recv_sems, to_sibling):
            sent.start()
        token[...] = jnp.zeros_like(token)

    out = pl.pallas_call(
        body, name=name,
        out_shape=(pltpu.SemaphoreType.DMA((n_sem,)), pltpu.SemaphoreType.DMA((n_sem,)),
                   *[pltpu.HBM(a.shape, a.dtype) for a in lands], _token()),
        in_specs=[_HBM] * n + [_ANY] * len(extra), out_specs=(_SEM, _SEM, *[_HBM] * n, _VMEM),
        input_output_aliases={i: 2 + i for i in range(n)},
        compiler_params=pltpu.CompilerParams(has_side_effects=_EFFECT),
    )(*[_in_hbm(a) for a in lands], *extra)
    return out[0], out[1], list(out[2:2 + n]), out[-1]


def _gather_ici_wait(name, lands, send_sems, recv_sems, after, to_sibling=False):
    n = len(lands)

    def body(*refs):
        for sent, landed in _gather_copies(refs[:n], refs[n], refs[n + 1], to_sibling):
            sent.wait_send()
            landed.wait_recv()

    out = pl.pallas_call(
        body, name=name,
        out_shape=[pltpu.HBM(a.shape, a.dtype) for a in lands],
        in_specs=[_HBM] * n + [_SEM, _SEM, _ANY], out_specs=[_HBM] * n,
        input_output_aliases={i: i for i in range(n)},
        compiler_params=pltpu.CompilerParams(has_side_effects=_EFFECT),
    )(*lands, send_sems, recv_sems, after)
    return list(out)


def _gather_d2d(lands):
    n = len(lands)

    def body(*refs):
        ins, outs, send_sems, recv_sems = refs[:n], refs[n:2 * n], refs[2 * n], refs[2 * n + 1]
        x, y, c = _position()

        def copy(i, q, core):
            block = _linear(*_ALL_CHIPS[q], core)
            return pltpu.make_async_remote_copy(
                src_ref=ins[i].at[block], dst_ref=outs[i].at[block],
                send_sem=send_sems.at[i, q], recv_sem=recv_sems.at[i, q],
                device_id=(x, y, 1 - c), device_id_type=MESH_ID)

        sent = [copy(i, q, c) for i in range(n) for q in range(len(_ALL_CHIPS))]
        for cp in sent:
            cp.start()
        for i in range(n):
            for q in range(len(_ALL_CHIPS)):
                copy(i, q, 1 - c).wait_recv()
        for cp in sent:
            cp.wait_send()

    return pl.pallas_call(
        body, name="gather_d2d",
        out_shape=[_sds(a.shape, a.dtype) for a in lands],
        in_specs=[_ANY] * n, out_specs=[_ANY] * n,
        input_output_aliases={i: i for i in range(n)},
        scratch_shapes=[pltpu.SemaphoreType.DMA((n, 4)), pltpu.SemaphoreType.DMA((n, 4))],
    )(*lands)


def _partials_d2d(parts):
    n = len(parts)
    half = N_DEV // 2

    def body(*refs):
        ins, outs, send_sems, recv_sems = refs[:n], refs[n:2 * n], refs[2 * n], refs[2 * n + 1]
        x, y, c = _position()

        def copy(i, q):
            return pltpu.make_async_remote_copy(
                src_ref=ins[i].at[_linear(*_ALL_CHIPS[q], 1 - c)], dst_ref=outs[i].at[q],
                send_sem=send_sems.at[i, q], recv_sem=recv_sems.at[i, q],
                device_id=(x, y, 1 - c), device_id_type=MESH_ID)

        sent = [copy(i, q) for i in range(n) for q in range(half)]
        for cp in sent:
            cp.start()
        for cp in sent:
            cp.wait_recv()
        for cp in sent:
            cp.wait_send()

    return pl.pallas_call(
        body, name="partials_d2d",
        out_shape=[_sds((half,) + p.shape[1:], p.dtype) for p in parts],
        in_specs=[_ANY] * n, out_specs=[_ANY] * n,
        scratch_shapes=[pltpu.SemaphoreType.DMA((n, half)), pltpu.SemaphoreType.DMA((n, half))],
    )(*parts)


def _partials_peers(x, y, c, direct):
    chips = _other_chips(x, y)
    if not direct:
        return [((*ch, c), 2 * ch[0] + ch[1]) for ch in chips]
    peers = [(x, y, 1 - c)] + [(*ch, c) for ch in chips] + [(*ch, 1 - c) for ch in chips]
    return [(p, _linear(*p)) for p in peers]


def _partials_copies(srcs, lands, send_sems, recv_sems, direct):
    x, y, c = _position()
    peers = _partials_peers(x, y, c, direct)
    return [pltpu.make_async_remote_copy(
        src_ref=srcs[i].at[block], dst_ref=lands[i].at[k],
        send_sem=send_sems.at[len(peers) * i + k], recv_sem=recv_sems.at[len(peers) * i + k],
        device_id=peer, device_id_type=MESH_ID)
        for i in range(len(srcs)) for k, (peer, block) in enumerate(peers)]


def _partials_send_start(name, srcs, lands, direct, after=None):
    n = len(srcs)
    n_sem = n * (N_DEV - 1 if direct else len(_ALL_CHIPS) - 1)

    def body(*refs):
        _, send_sems, recv_sems = refs[:2 * n], refs[-2 * n - 3], refs[-2 * n - 2]
        for cp in _partials_copies(refs[:n], refs[n:2 * n], send_sems, recv_sems, direct):
            cp.start()
        refs[-1][...] = jnp.zeros_like(refs[-1])

    both = list(srcs) + list(lands)
    extra = [] if after is None else [after]
    out = pl.pallas_call(
        body, name=name,
        out_shape=(pltpu.SemaphoreType.DMA((n_sem,)), pltpu.SemaphoreType.DMA((n_sem,)),
                   *[pltpu.HBM(a.shape, a.dtype) for a in both], _token()),
        in_specs=[_HBM] * (2 * n) + [_ANY] * len(extra), out_specs=(_SEM, _SEM, *[_HBM] * (2 * n), _VMEM),
        input_output_aliases={i: 2 + i for i in range(2 * n)},
        compiler_params=pltpu.CompilerParams(has_side_effects=_EFFECT),
    )(*[_in_hbm(a) for a in both], *extra)
    return out[0], out[1], list(out[2:2 + n]), list(out[2 + n:2 + 2 * n]), out[-1]


def _partials_send_wait(name, srcs, lands, send_sems, recv_sems, direct, after):
    n = len(srcs)

    def body(*refs):
        for cp in _partials_copies(refs[:n], refs[n:2 * n], refs[2 * n], refs[2 * n + 1], direct):
            cp.wait_send()
            cp.wait_recv()

    both = list(srcs) + list(lands)
    out = pl.pallas_call(
        body, name=name,
        out_shape=[pltpu.HBM(a.shape, a.dtype) for a in both],
        in_specs=[_HBM] * (2 * n) + [_SEM, _SEM, _ANY], out_specs=[_HBM] * (2 * n),
        input_output_aliases={i: i for i in range(2 * n)},
        compiler_params=pltpu.CompilerParams(has_side_effects=_EFFECT),
    )(*both, send_sems, recv_sems, after)
    return list(out[:n]), list(out[n:])


def _pack(arrs):
    flat = jnp.concatenate([a.reshape(-1).astype(F32) for a in arrs])
    block = OPT_ROW_BLOCK if flat.shape[0] > OPT_ROW_BLOCK * LANES else SUBLANES
    pad = (-flat.shape[0]) % (block * LANES)
    if pad:
        flat = jnp.concatenate([flat, jnp.zeros((pad,), F32)])
    return flat.reshape(-1, LANES)


def _unpack(packed, shapes, lead=()):
    flat = packed.reshape(lead + (-1,))
    out, off = [], 0
    for s in shapes:
        size = int(np.prod(s))
        out.append(flat[..., off:off + size].reshape(lead + tuple(s)))
        off += size
    return out


def _unshard_last(g):
    nd = g.ndim
    perm = tuple(range(1, nd - 1)) + (0, nd - 1)
    t = jnp.transpose(g, perm)
    return t.reshape(t.shape[:-2] + (N_DEV * g.shape[-1],))


def kernel(x, c, w_ada, b_ada, ln_g, ln_b, ffn_gu, ffn_down, gmlp_w_in, gmlp_b_in, gmlp_ln_g, gmlp_ln_b, gmlp_w_s, gmlp_b_s, gmlp_w_out, w_ada_kv, b_ada_kv, w_kv, attn_w_q, attn_rel_bias, attn_w_o, loss_target, m_w_ada, m_b_ada, m_ln_g, m_ln_b, m_ffn_gu, m_ffn_down, m_gmlp_w_in, m_gmlp_b_in, m_gmlp_ln_g, m_gmlp_ln_b, m_gmlp_w_s, m_gmlp_b_s, m_gmlp_w_out, m_w_ada_kv, m_b_ada_kv, m_w_kv, m_attn_w_q, m_attn_rel_bias, m_attn_w_o, v_w_ada, v_b_ada, v_ln_g, v_ln_b, v_ffn_gu, v_ffn_down, v_gmlp_w_in, v_gmlp_b_in, v_gmlp_ln_g, v_gmlp_ln_b, v_gmlp_w_s, v_gmlp_b_s, v_gmlp_w_out, v_w_ada_kv, v_b_ada_kv, v_w_kv, v_attn_w_q, v_attn_rel_bias, v_attn_w_o):
    weights = dict(w_ada=w_ada, b_ada=b_ada, ln_g=ln_g, ln_b=ln_b, ffn_gu=ffn_gu, ffn_down=ffn_down,
                   gmlp_w_in=gmlp_w_in, gmlp_b_in=gmlp_b_in, gmlp_ln_g=gmlp_ln_g, gmlp_ln_b=gmlp_ln_b,
                   gmlp_w_s=gmlp_w_s, gmlp_b_s=gmlp_b_s, gmlp_w_out=gmlp_w_out, w_ada_kv=w_ada_kv,
                   b_ada_kv=b_ada_kv, w_kv=w_kv, attn_w_q=attn_w_q, attn_rel_bias=attn_rel_bias,
                   attn_w_o=attn_w_o)
    mom1 = dict(w_ada=m_w_ada, b_ada=m_b_ada, ln_g=m_ln_g, ln_b=m_ln_b, ffn_gu=m_ffn_gu, ffn_down=m_ffn_down,
                gmlp_w_in=m_gmlp_w_in, gmlp_b_in=m_gmlp_b_in, gmlp_ln_g=m_gmlp_ln_g, gmlp_ln_b=m_gmlp_ln_b,
                gmlp_w_s=m_gmlp_w_s, gmlp_b_s=m_gmlp_b_s, gmlp_w_out=m_gmlp_w_out, w_ada_kv=m_w_ada_kv,
                b_ada_kv=m_b_ada_kv, w_kv=m_w_kv, attn_w_q=m_attn_w_q, attn_rel_bias=m_attn_rel_bias,
                attn_w_o=m_attn_w_o)
    mom2 = dict(w_ada=v_w_ada, b_ada=v_b_ada, ln_g=v_ln_g, ln_b=v_ln_b, ffn_gu=v_ffn_gu, ffn_down=v_ffn_down,
                gmlp_w_in=v_gmlp_w_in, gmlp_b_in=v_gmlp_b_in, gmlp_ln_g=v_gmlp_ln_g, gmlp_ln_b=v_gmlp_ln_b,
                gmlp_w_s=v_gmlp_w_s, gmlp_b_s=v_gmlp_b_s, gmlp_w_out=v_gmlp_w_out, w_ada_kv=v_w_ada_kv,
                b_ada_kv=v_b_ada_kv, w_kv=v_w_kv, attn_w_q=v_attn_w_q, attn_rel_bias=v_attn_rel_bias,
                attn_w_o=v_attn_w_o)
    order = list(weights)

    x = x[0]
    target = loss_target[0]
    t, d = x.shape
    n_mod = w_ada.shape[-1] * N_DEV // d
    mod_w = w_ada.shape[-1]
    kv_w = w_ada_kv.shape[-1]
    n_b = DEPTH - N_A
    me = _linear(*_position())

    l2 = DEPTH * 2
    big = dict(
        ffn_gu=ffn_gu.reshape((l2,) + ffn_gu.shape[2:]),
        ffn_down=ffn_down.reshape((l2,) + ffn_down.shape[2:]),
        gmlp_w_in=gmlp_w_in, gmlp_w_out=gmlp_w_out, w_kv=w_kv[None],
        attn_w_q=attn_w_q, attn_w_o=attn_w_o)
    big_names = list(big)
    core = lax.axis_index("c").astype(jnp.int32).reshape(1)
    chip = (2 * lax.axis_index("x") + lax.axis_index("y")).astype(jnp.int32).reshape(1)

    fwd_groups = [
        {"ffn_gu": (0, 1), "ffn_down": (0, 1)},
        {"gmlp_w_in": (0, 1), "gmlp_w_out": (0, 1)},
        {"ffn_gu": (1, 1), "ffn_down": (1, 1)},
        {"ffn_gu": (2, 1), "ffn_down": (2, 1)},
        {"gmlp_w_in": (1, 1), "gmlp_w_out": (1, 1)},
        {"ffn_gu": (3, 1), "ffn_down": (3, 1), "w_kv": (0, 1)},
        {"ffn_gu": (4, 1), "ffn_down": (4, 1)},
        {"attn_w_q": (0, 1), "attn_w_o": (0, 1)},
        {"ffn_gu": (5, 1), "ffn_down": (5, 1)},
        {"ffn_gu": (6, 1), "ffn_down": (6, 1)},
        {"attn_w_q": (1, 1), "attn_w_o": (1, 1)},
        {"ffn_gu": (7, 1), "ffn_down": (7, 1)},
    ]
    bwd_groups = []
    for l in range(DEPTH):
        g = {"ffn_gu": (2 * l, 2), "ffn_down": (2 * l, 2)}
        if l < N_A:
            g.update({"gmlp_w_in": (l, 1), "gmlp_w_out": (l, 1)})
        else:
            g.update({"attn_w_q": (l - N_A, 1), "attn_w_o": (l - N_A, 1)})
        if l == N_A - 1:
            g["w_kv"] = (0, 1)
        bwd_groups.append(g)

    def slot_of(groups, name, slot):
        for gi, g in enumerate(groups):
            if name in g and g[name][0] <= slot < g[name][0] + g[name][1]:
                return gi, slot - g[name][0]
        raise KeyError((name, slot))

    def start_group(gi, after=None):
        lands = []
        for name, (s0, cnt) in fwd_groups[gi].items():
            shard = big[name][s0:s0 + cnt].astype(BF16)
            land = lax.empty((N_DEV,) + shard.shape, BF16)
            lands.append(lax.dynamic_update_slice(land, shard[None], (me,) + (0,) * shard.ndim))
        return _gather_ici_start(f"gather_ici_start_{gi}", lands, after)

    gathered = [None] * len(fwd_groups)

    passing = {}

    def pass_on_early(gi, after):
        send_sems, recv_sems, lands, _ = flights[gi]
        lands = _gather_ici_wait(f"gather_ici_wait_{gi}", lands, send_sems, recv_sems, after)
        passing[gi] = _gather_ici_start(f"gather_d2d_start_{gi}", lands, to_sibling=True)
        return passing[gi][3]

    def land_group(gi, after):
        if gi in passing:
            send_sems, recv_sems, lands, _ = passing.pop(gi)
            lands = _gather_ici_wait(f"gather_d2d_wait_{gi}", lands, send_sems, recv_sems, after, to_sibling=True)
        else:
            send_sems, recv_sems, lands, _ = flights[gi]
            lands = _gather_d2d(_gather_ici_wait(f"gather_ici_wait_{gi}", lands, send_sems, recv_sems, after))
        gathered[gi] = dict(zip(fwd_groups[gi], lands))

    def weight(name, slot):
        gi, local = slot_of(fwd_groups, name, slot)
        return gathered[gi][name], local

    swapped = ("ffn_gu",)

    def grad_shape(name):
        s = big[name].shape[1:]
        return s[:-2] + (s[-1], s[-2]) if name in swapped else s

    partial = [{name: lax.empty((N_DEV, cnt) + grad_shape(name), BF16) for name, (_, cnt) in g.items()}
               for g in bwd_groups]

    c_all = _all_gather_small(_pack([c]))
    c_all = _unpack(c_all, [(d,)], lead=(N_DEV,))[0]
    c4 = _as4(c_all)
    mod_part = _matmul("ada_fwd", c4, w_ada[:, None], (DEPTH, 1, N_DEV, mod_w), F32, a_silu=True)
    kv_part = _matmul("ada_kv_fwd", c4, _as4(w_ada_kv), (1, 1, N_DEV, kv_w), F32, a_silu=True)
    small_shapes = [mod_part.shape, kv_part.shape, ln_g.shape, ln_b.shape, gmlp_b_in.shape,
                    gmlp_ln_g.shape, gmlp_ln_b.shape, attn_rel_bias.shape]
    small = _all_gather_small(_pack([mod_part, kv_part, ln_g, ln_b, gmlp_b_in, gmlp_ln_g, gmlp_ln_b,
                                     attn_rel_bias]))
    flights = [start_group(0, after=small)]
    flights += [start_group(gi, after=flights[0][3]) for gi in range(1, len(fwd_groups))]
    start_token = sum(f[3][0, 0] for f in flights)
    (mod_g, kvm_g, ln_g_g, ln_b_g, b_in_g, gln_g_g, gln_b_g, rel_g) = _unpack(small, small_shapes, lead=(N_DEV,))
    mod_mine = lax.dynamic_index_in_dim(mod_g[:, :, 0], me, axis=2, keepdims=False)
    mod = _unshard_last(mod_mine) + b_ada
    mod = mod.reshape(DEPTH, n_mod, 1, d)
    kvm_mine = lax.dynamic_index_in_dim(kvm_g[:, 0, 0], me, axis=1, keepdims=False)
    mkv = (_unshard_last(kvm_mine) + b_ada_kv).reshape(2, 1, d)
    ln_g_f = _unshard_last(ln_g_g)
    ln_b_f = _unshard_last(ln_b_g)
    half = N_DEV // 2
    b_in_f = jnp.transpose(b_in_g, (1, 0, 2))[:, :, None, :]
    gln_g_f = _unshard_last(gln_g_g).reshape(N_A, half, 1, -1)
    gln_b_f = _unshard_last(gln_b_g).reshape(N_A, half, 1, -1)
    rel_f = _unshard_last(rel_g)

    def shard_act(a):
        return a.reshape(a.shape[0], a.shape[2], a.shape[3])

    def grad_into(name, slot, mm):
        gi, local = slot_of(bwd_groups, name, slot)
        partial[gi][name] = mm(partial[gi][name], local)

    def ffn_fwd(h, lw, after=None):
        w_gu, l_gu = weight("ffn_gu", lw)
        w_dn, l_dn = weight("ffn_down", lw)
        gu, a = _ffn_up_fwd(h, w_gu, l_gu, after)
        y = _matmul("ffn_down_fwd", a[:, None], w_dn, (1, 1, t, d), F32, lb=l_dn, b_merge=2, reduce=True)
        return y[0, 0], (gu, a)

    def ffn_bwd(dy, h, saved, lw):
        gu, a = saved
        w_gu, l_gu = weight("ffn_gu", lw)
        w_dn, l_dn = weight("ffn_down", lw)
        dgu = _ffn_down_bwd_a(dy, w_dn, l_dn, gu).reshape((N_DEV,) + gu.shape[2:])
        grad_into("ffn_down", lw, lambda buf, lo: _matmul(
            "ffn_down_bwd_w", a[:, None], _as4(dy), buf.shape, BF16, ta=True, lo=lo, out_merge=2, out_buf=buf))
        dh = _matmul("ffn_gu_bwd_a", dgu[:, None], w_gu, (1, 1, t, d), F32, lb=l_gu, tb=True, reduce=True)
        grad_into("ffn_gu", lw, lambda buf, lo: _matmul(
            "ffn_gu_bwd_w", dgu[:, None], _as4(h), buf.shape, BF16, ta=True, lo=lo, out_buf=buf))
        return dh[0, 0], {}

    def gmlp_params(l):
        return (b_in_f[l], gln_g_f[l], gln_b_f[l], gmlp_w_s[l], gmlp_b_s[l][:, :, None])

    def gmlp_fwd(h, l, after=None):
        w_in, l_in = weight("gmlp_w_in", l)
        w_out, l_out = weight("gmlp_w_out", l)
        n = w_in.shape[-1]
        zpre = _matmul("gmlp_in_fwd", _as4(h), w_in, (N_DEV, 1, t, n), F32, lb=l_in, after=after)
        gated = _gmlp_mid_fwd(shard_act(zpre), *gmlp_params(l))
        y = _matmul("gmlp_out_fwd", gated[:, None], w_out, (1, 1, t, d), F32, lb=l_out, b_merge=2, reduce=True)
        return y[0, 0], (zpre, gated)

    def gmlp_bwd(dy, h, saved, l):
        zpre, gated = saved
        w_in, l_in = weight("gmlp_w_in", l)
        w_out, l_out = weight("gmlp_w_out", l)
        n = w_in.shape[-1]
        dgated = _matmul("gmlp_out_bwd_a", _as4(dy), w_out, (half, 1, t, n), F32, lb=l_out, b_merge=2, tb=True)
        grad_into("gmlp_w_out", l, lambda buf, lo: _matmul(
            "gmlp_out_bwd_w", gated[:, None], _as4(dy), buf.shape, BF16, ta=True, lo=lo, out_merge=2, out_buf=buf))
        dz, dws, dbs, dlng, dlnb, dbin = _gmlp_mid_bwd(shard_act(zpre), shard_act(dgated), *gmlp_params(l))
        dh = _matmul("gmlp_in_bwd_a", dz[:, None], w_in, (1, 1, t, d), F32, lb=l_in, tb=True, reduce=True)
        grad_into("gmlp_w_in", l, lambda buf, lo: _matmul(
            "gmlp_in_bwd_w", _as4(h), dz[:, None], buf.shape, BF16, ta=True, lo=lo, out_buf=buf))
        small_grads = dict(gmlp_w_s=dws, gmlp_b_s=dbs[:, :, 0], gmlp_ln_g=dlng.reshape(-1),
                           gmlp_ln_b=dlnb.reshape(-1), gmlp_b_in=dbin.reshape(-1))
        return dh[0, 0], small_grads

    def attn_fwd(h, j, kvp, after=None):
        rel_vec = _rel_vector(rel_f[j])
        w_q, l_q = weight("attn_w_q", j)
        w_o, l_o = weight("attn_w_o", j)
        q = _matmul("attn_q_fwd", _as4(h), w_q, (1, 1, t, d), BF16, lb=l_q, b_merge=N_DEV, reduce=True,
                    after=after)[0, 0]
        o = _attn_fwd(q, kvp, rel_vec)
        y = _matmul("attn_o_fwd", _as4(o), w_o, (1, 1, t, d), F32, lb=l_o, b_merge=N_DEV, reduce=True)
        return y[0, 0], (q, o, rel_vec)

    def attn_bwd(dy, h, saved, j, kvp, dkv_acc):
        q, o, rel_vec = saved
        w_q, l_q = weight("attn_w_q", j)
        w_o, l_o = weight("attn_w_o", j)
        do = _matmul("attn_o_bwd_a", _as4(dy), w_o, (1, 1, t, d), BF16, lb=l_o, b_merge=N_DEV, tb=True)[0, 0]
        grad_into("attn_w_o", j, lambda buf, lo: _matmul(
            "attn_o_bwd_w", _as4(o), _as4(dy), buf.shape, BF16, ta=True, lo=lo, out_merge=N_DEV, out_buf=buf))
        dq, dk, dv, dsc = _attn_bwd(q, do, kvp, rel_vec, *dkv_acc)
        drel = _rel_bias_grad(_skew_diagonals(dsc))
        dh = _matmul("attn_q_bwd_a", _as4(dq), w_q, (1, 1, t, d), F32, lb=l_q, b_merge=N_DEV, tb=True)
        grad_into("attn_w_q", j, lambda buf, lo: _matmul(
            "attn_q_bwd_w", _as4(h), _as4(dq), buf.shape, BF16, ta=True, lo=lo, out_merge=N_DEV, out_buf=buf))
        return dh[0, 0], dict(attn_rel_bias=drel, dkv=(dk, dv))

    tape = []
    kvp = None
    kv_tape = None
    first_use = {(l, i): 3 * l + i for l in range(DEPTH) for i in range(3)}
    PASS_ON_EARLY_FROM = 7
    h = _modulate(x, mod[0, 1], mod[0, 0] + start_token)
    for l in range(DEPTH):
        for i in range(3):
            if (l, i) in first_use:
                land_group(first_use[l, i], x)
            nl, ni = (l, i + 1) if i < 2 else (l + 1, 0)
            ahead = first_use.get((nl, ni), 0)
            started = pass_on_early(ahead, x) if ahead >= PASS_ON_EARLY_FROM else None
            scl, gate = mod[l, 3 * i + 1], mod[l, 3 * i + 2]
            wgt = 1.0 if i == 1 else 0.5
            gw = wgt * (1.0 + gate)
            if i != 1:
                y, saved = ffn_fwd(h, 2 * l + i // 2, started)
            elif l < N_A:
                y, saved = gmlp_fwd(h, l, started)
            else:
                y, saved = attn_fwd(h, l - N_A, kvp, started)
            readers = [(mod[nl, 3 * ni + 1], mod[nl, 3 * ni])] if nl < DEPTH else []
            shared_kv = (l, i) == (N_A - 1, 2)
            if shared_kv:
                readers.append((mkv[1], mkv[0]))
            outs = _ln_res_fwd(x, y, gw, ln_g_f[l, i][None], ln_b_f[l, i][None], readers)
            tape.append((x, h, y, gw, scl, saved))
            x = outs[0]
            h = outs[1] if nl < DEPTH else None
            if shared_kv:
                hkv = outs[-1]
                w_kvg, l_kv = weight("w_kv", 0)
                n = w_kvg.shape[-1]
                kv = _matmul("kv_fwd", _as4(hkv), w_kvg, (N_DEV, 1, t, n), BF16, lb=l_kv)
                kvp = jnp.pad(shard_act(kv), ((0, 0), (LEFT_PAD, 0), (0, 0)))
                kv_tape = hkv

    loss_part, dx = _loss_head(x, target)
    loss = lax.psum(loss_part[0, 0], MESH_AXES)

    d_mod = [[None] * n_mod for _ in range(DEPTH)]
    d_ln_g = [[None] * 3 for _ in range(DEPTH)]
    d_ln_b = [[None] * 3 for _ in range(DEPTH)]
    small_grads = {k: [None] * N_A for k in ("gmlp_w_s", "gmlp_b_s", "gmlp_ln_g", "gmlp_ln_b", "gmlp_b_in")}
    d_rel = [None] * n_b
    dkv_acc = ()
    d_mkv = None
    reductions = [None] * DEPTH
    sent_token = None
    readers = []
    for l in reversed(range(DEPTH)):
        if l == N_A - 1:
            hkv = kv_tape
            w_kvg, l_kv = weight("w_kv", 0)
            dkv = jnp.concatenate(dkv_acc)[:, LEFT_PAD:, :].astype(BF16)[:, None]
            dhkv = _matmul("kv_bwd_a", dkv, w_kvg, (1, 1, t, d), F32, lb=l_kv, tb=True, reduce=True)[0, 0]
            grad_into("w_kv", 0, lambda buf, lo: _matmul(
                "kv_bwd_w", _as4(hkv), dkv, buf.shape, BF16, ta=True, lo=lo, out_buf=buf))
            readers.append((dhkv, mkv[1], None))
        for i in reversed(range(3)):
            x_in, h, y, gw, scl, saved = tape[3 * l + i]
            wgt = 1.0 if i == 1 else 0.5
            if sent_token is not None:
                gw = gw + sent_token
                sent_token = None
            res = _ln_res_bwd(x_in, y, gw, ln_g_f[l, i][None], ln_b_f[l, i][None], dx,
                              [(r[0], r[1]) for r in readers])
            dx_res, dy, dgw, dg, db = res[:5]
            for k, (_, _, slot) in enumerate(readers):
                dscl_k, dshift_k = res[5 + 2 * k][0], res[6 + 2 * k][0]
                if slot is None:
                    d_mkv = jnp.concatenate([dshift_k, dscl_k])
                else:
                    d_mod[slot[0]][slot[1]], d_mod[slot[0]][slot[1] + 1] = dshift_k, dscl_k
            d_ln_g[l][i], d_ln_b[l][i] = dg[0], db[0]
            if i != 1:
                dh, extra = ffn_bwd(dy, h, saved, 2 * l + i // 2)
            elif l < N_A:
                dh, extra = gmlp_bwd(dy, h, saved, l)
                for k, g in extra.items():
                    small_grads[k][l] = g
            else:
                dh, extra = attn_bwd(dy, h, saved, l - N_A, kvp, dkv_acc)
                d_rel[l - N_A] = extra["attn_rel_bias"]
                dkv_acc = extra["dkv"]
            d_mod[l][3 * i + 2] = wgt * dgw[0]
            dx = dx_res
            readers = [(dh, scl, (l, 3 * i))]
        if l > 0:
            srcs = [partial[l][k] for k in bwd_groups[l]]
            lands = [lax.empty((N_DEV - 1,) + s.shape[1:], BF16) for s in srcs]
            reductions[l] = _partials_send_start(f"partials_send_start_{l}", srcs, lands, True)
            sent_token = reductions[l][4][0, 0]
    (dh, scl, _), = readers
    dx, dscl, dshift = _mod_bwd(dx, dh, tape[0][0], scl)
    d_mod[0][0], d_mod[0][1] = dshift[0], dscl[0]
    grad_x = dx[None]

    d_mod_arr = jnp.stack([jnp.concatenate(r) for r in d_mod])
    small_part = dict(
        b_ada=d_mod_arr, b_ada_kv=d_mkv,
        ln_g=jnp.stack([jnp.stack(r) for r in d_ln_g]), ln_b=jnp.stack([jnp.stack(r) for r in d_ln_b]),
        gmlp_b_in=jnp.stack(small_grads["gmlp_b_in"]), gmlp_ln_g=jnp.stack(small_grads["gmlp_ln_g"]),
        gmlp_ln_b=jnp.stack(small_grads["gmlp_ln_b"]), gmlp_w_s=jnp.stack(small_grads["gmlp_w_s"]),
        gmlp_b_s=jnp.stack(small_grads["gmlp_b_s"]), attn_rel_bias=jnp.stack(d_rel))
    small_names = list(small_part)
    sp_shapes = [small_part[k].shape for k in small_names]
    sp_all = _all_gather_small(_pack([small_part[k] for k in small_names]),
                               after=[partial[0][k] for k in bwd_groups[0]])

    from_sibling = _partials_d2d([partial[0][k] for k in bwd_groups[0]])
    sums = [_chip_sum(partial[0][k], r1, core) for k, r1 in zip(bwd_groups[0], from_sibling)]
    lands = [lax.empty((len(_ALL_CHIPS) - 1,) + s.shape[1:], BF16) for s in sums]
    reductions[0] = _partials_send_start("partials_send_start_0", sums, lands, False, after=sp_all)
    sent_token = reductions[0][4][0, 0]
    c4 = c4 + sent_token

    sp_sum = _sum_parts(sp_all)
    full_grads = dict(zip(small_names, _unpack(sp_sum, sp_shapes)))
    per_dev = dict(zip(small_names, _unpack(sp_all, sp_shapes, lead=(N_DEV,))))

    def my_cols(a, width):
        return lax.dynamic_slice_in_dim(a, me * width, width, axis=a.ndim - 1)

    grads = {}
    grads["b_ada"] = full_grads["b_ada"]
    grads["b_ada_kv"] = full_grads["b_ada_kv"]
    grads["gmlp_w_s"] = full_grads["gmlp_w_s"]
    grads["gmlp_b_s"] = full_grads["gmlp_b_s"]
    for k in ("ln_g", "ln_b", "gmlp_b_in", "gmlp_ln_g", "gmlp_ln_b", "attn_rel_bias"):
        grads[k] = my_cols(full_grads[k], weights[k].shape[-1])

    dmod_cols = jnp.transpose(my_cols(per_dev["b_ada"], mod_w), (1, 0, 2))[:, None]
    grads["w_ada"] = _matmul("ada_bwd_w", c4, dmod_cols, (DEPTH, 1, d, mod_w), F32, ta=True,
                             a_silu=True)[:, 0]
    dkv_cols = my_cols(per_dev["b_ada_kv"], kv_w)[None, None]
    grads["w_ada_kv"] = _matmul("ada_kv_bwd_w", c4, dkv_cols, (1, 1, d, kv_w), F32, ta=True,
                                a_silu=True)[0, 0]

    delta, new_m, new_v = {}, {}, {}
    first = jnp.zeros((1,), jnp.int32)

    def flat2(a, cols):
        return a.reshape(-1, cols)

    done = None
    for k in ("w_ada", "w_ada_kv"):
        w = weights[k]
        cols = w.shape[-1]
        res = _adamw(grads[k].reshape(1, -1, cols), first, None, flat2(w, cols), flat2(mom1[k], cols),
                     flat2(mom2[k], cols), after=done)
        grads[k], delta[k], new_m[k], new_v[k] = (a.reshape(w.shape) for a in res[:4])
        done = res[4]

    tiny = [k for k in order if k not in delta and k not in big_names]
    tiny_shapes = [weights[k].shape for k in tiny]
    tiny_out = _adamw((_pack([grads[k] for k in tiny]) + sent_token)[None], first, None,
                      _pack([weights[k] for k in tiny]), _pack([mom1[k] for k in tiny]),
                      _pack([mom2[k] for k in tiny]), after=done)
    for dst, arr in zip((grads, delta, new_m, new_v), tiny_out):
        for k, val in zip(tiny, _unpack(arr, tiny_shapes)):
            dst[k] = val

    def opt_view(k, a):
        a = jnp.swapaxes(a, -1, -2) if k in swapped else a
        return a.reshape(-1, a.shape[-1])

    def opt_unview(k, a):
        s = weights[k].shape
        return jnp.swapaxes(a.reshape(s[:-2] + (s[-1], s[-2])), -1, -2) if k in swapped else a.reshape(s)

    bufs = {k: [lax.empty(opt_view(k, weights[k]).shape, F32) for _ in range(4)] for k in big_names}
    done = tiny_out[4]
    me_idx = me.astype(jnp.int32).reshape(1)
    for l in reversed(range(DEPTH)):
        send_sems, recv_sems, srcs, lands, _ = reductions[l]
        srcs, lands = _partials_send_wait(f"partials_send_wait_{l}", srcs, lands, send_sems, recv_sems, l > 0, done)
        for k, own, got in zip(bwd_groups[l], srcs, lands):
            cols = own.shape[-1]
            slot_rows = int(np.prod(own.shape[2:-1]))
            *bufs[k], done = _adamw(own.reshape(own.shape[0], -1, cols), me_idx if l > 0 else chip,
                                    got.reshape(got.shape[0], -1, cols),
                                    opt_view(k, weights[k]), opt_view(k, mom1[k]), opt_view(k, mom2[k]),
                                    row0=bwd_groups[l][k][0] * slot_rows, bufs=bufs[k], after=done)
    for k in big_names:
        grads[k], delta[k], new_m[k], new_v[k] = (opt_unview(k, b) for b in bufs[k])

    return (loss, grad_x, *[grads[k] for k in order], *[delta[k] for k in order],
            *[new_m[k] for k in order], *[new_v[k] for k in order])
```

```python
import numpy as np
import jax
import jax.numpy as jnp
from jax import lax
from jax.experimental import pallas as pl
from jax.experimental.pallas import tpu as pltpu

F32 = jnp.float32
BF16 = jnp.bfloat16
MESH_AXES = ("x", "y", "c")
N_DEV = 8
MESH_ID = pl.DeviceIdType.MESH

DEPTH = 4
N_A = 2
CHUNK = 64
N_HEADS = 16
LEFT_CHUNKS = 8
BAND = (LEFT_CHUNKS + 1) * CHUNK
LEFT_PAD = LEFT_CHUNKS * CHUNK
MAX_REL = 4 * CHUNK
N_REL = (CHUNK - 1) + MAX_REL + 1
GMLP_WINDOW = 128
GMLP_GROUPS = 8
ALPHA = (2.0 * DEPTH) ** 0.25
LN_EPS = 1e-5
ADAM_LR = 0.001
ADAM_B1 = 0.9
ADAM_B2 = 0.999
ADAM_EPS = 1e-08
ADAM_WD = 0.01
ADAM_STEP = 10

V7X_VMEM_BYTES = 64 * 1024 * 1024
VMEM_LIMIT = V7X_VMEM_BYTES - 8 * 1024 * 1024
LANES = 128
SUBLANES = 8
MM_BLOCK = 2048
BIG_ROW_BLOCK = 1024
ROW_BLOCK = 512
OPT_ROW_BLOCK = 256

_ANY = pl.BlockSpec(memory_space=pl.ANY)
_VMEM = pl.BlockSpec(memory_space=pltpu.VMEM)


def _params(sem=None):
    return pltpu.CompilerParams(dimension_semantics=sem, vmem_limit_bytes=VMEM_LIMIT)


def _row_block(rows, target):
    for d in range(min(rows, target), 0, -1):
        if rows % d == 0 and (d % SUBLANES == 0 or d == rows):
            return d
    return rows


def _matmul(name, a, b, out_shape4, out_dtype, *, la=0, lb=0, lo=0, ta=False, tb=False,
            reduce=False, b_merge=1, out_merge=1, out_buf=None, a_silu=False, after=None):
    ja_n, _, a_r, a_c = a.shape
    jb_n, _, b_r, b_c = b.shape
    jo_n, _, o_r, o_c = out_shape4
    m_tot = a_c if ta else a_r
    k_a = a_r if ta else a_c
    b_rows = b_merge * b_r
    k_c = b_c if tb else b_rows
    n = b_rows if tb else b_c
    n_chunks = (jb_n // b_merge) if reduce else 1
    natural_k = reduce and ja_n == 1
    assert n == o_c, (name, n, o_c)
    assert k_a ==(k_c * n_chunks if natural_k else k_c), (name, k_a, k_c, n_chunks)
    bk = k_c if (k_c <= MM_BLOCK or (b_merge > 1 and not tb)) else MM_BLOCK
    assert k_c % bk == 0
    nkk = k_c // bk
    kg = 2 if (reduce and ja_n > 1 and nkk == 1 and not ta and n_chunks % 2 == 0) else 1
    nk = n_chunks * nkk // kg
    m_out = out_merge * o_r
    assert m_tot == m_out, (name, m_tot, m_out)
    bm = m_tot if (m_tot <= MM_BLOCK or out_merge > 1) else MM_BLOCK
    assert m_tot % bm == 0
    jo_blocks = jo_n // out_merge

    def a_index(j, m, k):
        kj, kk = k // nkk, k % nkk
        ja = 0 if ja_n == 1 else (kj if reduce else j)
        ke = kk + kj * nkk if natural_k else kk
        return (ja, la, ke, m) if ta else (ja, la, m, ke)

    def b_index(j, m, k):
        kj, kk = k // nkk, k % nkk
        jb = 0 if jb_n == b_merge else (kj if reduce else j)
        return (jb, lb, 0, kk) if tb else (jb, lb, kk, 0)

    def o_index(j, m, k):
        return (j, lo, 0, 0) if out_merge > 1 else (j, lo, m, 0)

    a_block = (None, None, bk, bm) if ta else (None if kg == 1 else kg, None, bm, bk)
    if b_merge > 1:
        b_block = (kg * b_merge, None, b_r, bk if tb else n)
    else:
        b_block = (None if kg == 1 else kg, None) + ((n, bk) if tb else (bk, n))
    o_block = (out_merge, None, o_r, n) if out_merge > 1 else (None, None, bm, n)
    dims = (((0 if ta else 1,), (1 if tb else 0,)), ((), ()))

    in_place = nk > 1 and out_dtype == F32 and out_merge == 1
    use_acc = nk > 1 and not in_place

    def body(a_ref, b_ref, *rest):
        o_ref = rest[-2] if use_acc else rest[-1]
        k = pl.program_id(2)
        av = a_ref[...]
        if a_silu:
            af = av.astype(F32)
            av = af * jax.nn.sigmoid(af)
        bv = b_ref[...]
        if kg > 1:
            bv = bv.reshape(kg, -1, bv.shape[-1])
            prod = sum(lax.dot_general(av[g].astype(BF16), bv[g].astype(BF16), dims, preferred_element_type=F32)
                       for g in range(kg))
        else:
            if b_merge > 1:
                bv = bv.reshape(b_rows, bv.shape[-1])
            prod = lax.dot_general(av.astype(BF16), bv.astype(BF16), dims, preferred_element_type=F32)

        def emit(val):
            val = val.astype(out_dtype)
            o_ref[...] = val.reshape(out_merge, o_r, n) if out_merge > 1 else val

        if nk == 1:
            emit(prod)
            return
        acc_ref = o_ref if in_place else rest[-1]

        @pl.when(k == 0)
        def _():
            acc_ref[...] = prod

        @pl.when(k > 0)
        def _():
            acc_ref[...] += prod

        if use_acc:
            @pl.when(k == nk - 1)
            def _():
                emit(acc_ref[...])

    in_specs = [pl.BlockSpec(a_block, a_index), pl.BlockSpec(b_block, b_index)]
    operands = [a, b]
    aliases = {}
    if out_buf is not None:
        assert out_buf.shape == tuple(out_shape4) and out_buf.dtype == out_dtype
        in_specs.append(_ANY)
        operands.append(out_buf)
        aliases = {2: 0}
    if after is not None:
        in_specs.append(_ANY)
        operands.append(after)
    return pl.pallas_call(
        body, name=name,
        grid=(jo_blocks, m_tot // bm, nk),
        in_specs=in_specs,
        out_specs=pl.BlockSpec(o_block, o_index),
        out_shape=jax.ShapeDtypeStruct(tuple(out_shape4), out_dtype),
        scratch_shapes=[pltpu.VMEM((bm, n), F32)] if use_acc else [],
        input_output_aliases=aliases,
        compiler_params=_params(("parallel", "parallel", "arbitrary")),
    )(*operands)


def _as4(a):
    return a.reshape((1,) * (4 - a.ndim) + a.shape)


def _row_call(name, body, ins, outs, t, *, acc_outs=()):
    bt = _row_block(t, ROW_BLOCK)

    def spec(arr, tiled):
        if tiled:
            return pl.BlockSpec((bt,) + tuple(arr.shape[1:]), lambda i: (i,) + (0,) * (arr.ndim - 1))
        return pl.BlockSpec(tuple(arr.shape), lambda i: (0,) * arr.ndim)

    return pl.pallas_call(
        body, name=name, grid=(t // bt,),
        in_specs=[spec(a, tl) for a, tl in ins],
        out_specs=[spec(o, tl) for o, tl in outs],
        out_shape=[jax.ShapeDtypeStruct(o.shape, o.dtype) for o, _ in outs],
        compiler_params=_params(("arbitrary",) if acc_outs else ("parallel",)),
    )(*[a for a, _ in ins])


def _sds(shape, dtype):
    return jax.ShapeDtypeStruct(tuple(shape), dtype)


def _modulate(x, scl, shift, after=()):
    t, d = x.shape

    def body(x_ref, s_ref, b_ref, *rest):
        rest[-1][...] = (x_ref[...] * (1.0 + s_ref[...]) + b_ref[...]).astype(BF16)

    return _row_call("modulate", body, [(x, True), (scl, False), (shift, False)] + [(a, False) for a in after],
                     [(_sds((t, d), BF16), True)], t)[0]


def _ln_stats(r):
    mu = jnp.mean(r, axis=-1, keepdims=True)
    rc = r - mu
    var = jnp.mean(rc * rc, axis=-1, keepdims=True)
    rstd = lax.rsqrt(var + LN_EPS)
    return rc * rstd, rstd


def _ln_res_fwd(x, y, gw, g, b, mods=()):
    t, d = x.shape
    n_mod = len(mods)

    def body(x_ref, y_ref, gw_ref, g_ref, b_ref, *rest):
        mod_refs, o_ref, h_refs = rest[:2 * n_mod], rest[2 * n_mod], rest[2 * n_mod + 1:]
        r = ALPHA * x_ref[...] + gw_ref[...] * y_ref[...]
        xhat, _ = _ln_stats(r)
        xn = xhat * g_ref[...] + b_ref[...]
        o_ref[...] = xn
        for k in range(n_mod):
            h_refs[k][...] = (xn * (1.0 + mod_refs[2 * k][...]) + mod_refs[2 * k + 1][...]).astype(BF16)

    vecs = [(v, False) for pair in mods for v in pair]
    return _row_call("ln_res_fwd", body,
                     [(x, True), (y, True), (gw, False), (g, False), (b, False)] + vecs,
                     [(_sds((t, d), F32), True)] + [(_sds((t, d), BF16), True)] * n_mod, t)


def _ln_res_bwd(x, y, gw, g, b, dx_base, pairs=()):
    t, d = x.shape
    n_pair = len(pairs)

    def body(x_ref, y_ref, gw_ref, g_ref, b_ref, dxb_ref, *rest):
        pair_refs, outs = rest[:2 * n_pair], rest[2 * n_pair:]
        dx_ref, dy_ref = outs[0], outs[1]
        sums = outs[2:]

        @pl.when(pl.program_id(0) == 0)
        def _():
            for r in sums:
                r[...] = jnp.zeros_like(r)

        yv = y_ref[...]
        gwv = gw_ref[...]
        gv = g_ref[...]
        xhat, rstd = _ln_stats(ALPHA * x_ref[...] + gwv * yv)
        dxn = dxb_ref[...]
        if n_pair:
            xn = xhat * gv + b_ref[...]
            for k in range(n_pair):
                dh = pair_refs[2 * k][...]
                dxn = dxn + dh * (1.0 + pair_refs[2 * k + 1][...])
                sums[3 + 2 * k][...] += jnp.sum(dh * xn, axis=0, keepdims=True)
                sums[4 + 2 * k][...] += jnp.sum(dh, axis=0, keepdims=True)
        dxh = dxn * gv
        m1 = jnp.mean(dxh, axis=-1, keepdims=True)
        m2 = jnp.mean(dxh * xhat, axis=-1, keepdims=True)
        dr = rstd * (dxh - m1 - xhat * m2)
        dx_ref[...] = ALPHA * dr
        dy_ref[...] = (gwv * dr).astype(BF16)
        sums[0][...] += jnp.sum(dr * yv, axis=0, keepdims=True)
        sums[1][...] += jnp.sum(dxn * xhat, axis=0, keepdims=True)
        sums[2][...] += jnp.sum(dxn, axis=0, keepdims=True)

    vec = _sds((1, d), F32)
    n_sum = 3 + 2 * n_pair
    ins = [(x, True), (y, True), (gw, False), (g, False), (b, False), (dx_base, True)]
    for dh, scl in pairs:
        ins += [(dh, True), (scl, False)]
    return _row_call("ln_res_bwd", body, ins,
                     [(_sds((t, d), F32), True), (_sds((t, d), BF16), True)] + [(vec, False)] * n_sum, t,
                     acc_outs=tuple(range(2, 2 + n_sum)))


def _mod_bwd(dx_res, dh, x, scl):
    t, d = x.shape

    def body(dxr_ref, dh_ref, x_ref, s_ref, dx_ref, ds_ref, db_ref):
        @pl.when(pl.program_id(0) == 0)
        def _():
            ds_ref[...] = jnp.zeros_like(ds_ref)
            db_ref[...] = jnp.zeros_like(db_ref)

        dh = dh_ref[...]
        dx_ref[...] = dxr_ref[...] + dh * (1.0 + s_ref[...])
        ds_ref[...] += jnp.sum(dh * x_ref[...], axis=0, keepdims=True)
        db_ref[...] += jnp.sum(dh, axis=0, keepdims=True)

    vec = _sds((1, d), F32)
    return _row_call("mod_bwd", body, [(dx_res, True), (dh, True), (x, True), (scl, False)],
                     [(_sds((t, d), F32), True), (vec, False), (vec, False)], t, acc_outs=(1, 2))


def _loss_head(y, target):
    t, d = y.shape

    def body(y_ref, t_ref, l_ref, dy_ref):
        @pl.when(pl.program_id(0) == 0)
        def _():
            l_ref[...] = jnp.zeros_like(l_ref)

        err = y_ref[...] - t_ref[...]
        dy_ref[...] = err * (1.0 / d)
        part = 0.5 * jnp.sum(jnp.mean(err * err, axis=-1, keepdims=True), axis=0, keepdims=True)
        l_ref[...] += jnp.broadcast_to(part, l_ref.shape)

    return _row_call("loss_head", body, [(y, True), (target, True)],
                     [(_sds((SUBLANES, LANES), F32), False), (_sds((t, d), F32), True)], t,
                     acc_outs=(0,))


def _sigmoid(x):
    return 0.5 * jnp.tanh(0.5 * x) + 0.5


def _ffn_up_fwd(h, w_gu, lb, after=None):
    t, d = h.shape
    n = w_gu.shape[-1]
    half = N_DEV // 2
    bt = _row_block(t, BIG_ROW_BLOCK)
    extra = [] if after is None else [after]

    def body(h_ref, wg_ref, wu_ref, *rest):
        fac_ref, a_ref = rest[-2:]
        hv = h_ref[...]
        g = jnp.dot(hv, wg_ref[...], preferred_element_type=F32)
        u = jnp.dot(hv, wu_ref[...], preferred_element_type=F32)
        sig = _sigmoid(g)
        silu = g * sig
        fac_ref[0] = u * (sig + silu * (1.0 - sig))
        fac_ref[1] = silu
        a_ref[...] = (silu * u).astype(BF16)

    return pl.pallas_call(
        body, name="ffn_up_fwd", grid=(half, t // bt),
        in_specs=[pl.BlockSpec((bt, d), lambda j, i: (i, 0)),
                  pl.BlockSpec((None, None, d, n), lambda j, i: (j, lb, 0, 0)),
                  pl.BlockSpec((None, None, d, n), lambda j, i: (half + j, lb, 0, 0))] + [_ANY] * len(extra),
        out_specs=[pl.BlockSpec((2, None, bt, n), lambda j, i: (0, j, i, 0)),
                   pl.BlockSpec((None, bt, n), lambda j, i: (j, i, 0))],
        out_shape=[_sds((2, half, t, n), F32), _sds((half, t, n), BF16)],
        compiler_params=_params(("parallel", "parallel")),
    )(h, w_gu, w_gu, *extra)


def _ffn_down_bwd_a(dy, w_down, lb, fac):
    t, d = dy.shape
    _, half, _, n = fac.shape
    r = w_down.shape[2]
    bt = _row_block(t, MM_BLOCK)

    def body(dy_ref, w_ref, fac_ref, d_ref):
        da = lax.dot_general(dy_ref[...], w_ref[...].reshape(2 * r, d), (((1,), (1,)), ((), ())),
                             preferred_element_type=F32)
        d_ref[0] = (da * fac_ref[0]).astype(BF16)
        d_ref[1] = (da * fac_ref[1]).astype(BF16)

    return pl.pallas_call(
        body, name="ffn_down_bwd_a", grid=(half, t // bt),
        in_specs=[pl.BlockSpec((bt, d), lambda j, i: (i, 0)),
                  pl.BlockSpec((2, None, r, d), lambda j, i: (j, lb, 0, 0)),
                  pl.BlockSpec((2, None, bt, n), lambda j, i: (0, j, i, 0))],
        out_specs=pl.BlockSpec((2, None, bt, n), lambda j, i: (0, j, i, 0)),
        out_shape=_sds((2, half, t, n), BF16),
        compiler_params=_params(("parallel", "parallel")),
    )(dy, w_down, fac)


_INV_SQRT2 = 0.7071067811865476
_INV_SQRT_2PI = 0.3989422804014327


def _gelu(z):
    return 0.5 * z * (1.0 + lax.erf(z * _INV_SQRT2))


def _gelu_grad(z):
    return 0.5 * (1.0 + lax.erf(z * _INV_SQRT2)) + z * jnp.exp(-0.5 * z * z) * _INV_SQRT_2PI


def _window_mask():
    t_out = lax.broadcasted_iota(jnp.int32, (GMLP_WINDOW, GMLP_WINDOW), 0)
    s_in = lax.broadcasted_iota(jnp.int32, (GMLP_WINDOW, GMLP_WINDOW), 1)
    return (s_in // CHUNK) <= (t_out // CHUNK)


def _gmlp_recompute(z_ref, bin_ref, lng_ref, lnb_ref):
    half = N_DEV // 2
    z = z_ref[...] + bin_ref[...]
    ge = _gelu(z)
    u = ge[:half]
    v = ge[half:]
    width = half * v.shape[-1]
    mu = jnp.sum(jnp.sum(v, axis=0), axis=-1, keepdims=True) / width
    vc = v - mu
    var = jnp.sum(jnp.sum(vc * vc, axis=0), axis=-1, keepdims=True) / width
    rstd = lax.rsqrt(var + LN_EPS)
    xhat = vc * rstd
    vn = xhat * lng_ref[...] + lnb_ref[...]
    return z, u, xhat, rstd, vn


def _gmlp_mid_fwd(zpre, b_in, ln_g, ln_b, w_s, b_s):
    _, t, n = zpre.shape
    half = N_DEV // 2
    gd = half * n // GMLP_GROUPS
    per = n // gd
    w = GMLP_WINDOW

    def body(z_ref, bin_ref, lng_ref, lnb_ref, ws_ref, bs_ref, o_ref):
        _, u, _, _, vn = _gmlp_recompute(z_ref, bin_ref, lng_ref, lnb_ref)
        mask = _window_mask()
        for g in range(GMLP_GROUPS):
            sh, c0 = g // per, (g % per) * gd
            wsm = jnp.where(mask, ws_ref[g], 0.0).astype(BF16)
            s = jnp.dot(wsm, vn[sh][:, c0:c0 + gd].astype(BF16), preferred_element_type=F32) + bs_ref[g]
            o_ref[sh, :, c0:c0 + gd] = (u[sh][:, c0:c0 + gd] * s).astype(BF16)

    whole = lambda a: pl.BlockSpec(tuple(a.shape), lambda i: (0,) * a.ndim)
    return pl.pallas_call(
        body, name="gmlp_mid_fwd", grid=(t // w,),
        in_specs=[pl.BlockSpec((N_DEV, w, n), lambda i: (0, i, 0)),
                  whole(b_in), whole(ln_g), whole(ln_b), whole(w_s), whole(b_s)],
        out_specs=pl.BlockSpec((half, w, n), lambda i: (0, i, 0)),
        out_shape=_sds((half, t, n), BF16),
        compiler_params=_params(("parallel",)),
    )(zpre, b_in, ln_g, ln_b, w_s, b_s)


def _gmlp_mid_bwd(zpre, dgated, b_in, ln_g, ln_b, w_s, b_s):
    _, t, n = zpre.shape
    half = N_DEV // 2
    gd = half * n // GMLP_GROUPS
    per = n // gd
    w = GMLP_WINDOW
    width = half * n

    def body(z_ref, dg_ref, bin_ref, lng_ref, lnb_ref, ws_ref, bs_ref,
             dz_ref, dws_ref, dbs_ref, dlng_ref, dlnb_ref, dbin_ref, du_ref, dvn_ref):
        @pl.when(pl.program_id(0) == 0)
        def _():
            for r in (dws_ref, dbs_ref, dlng_ref, dlnb_ref, dbin_ref):
                r[...] = jnp.zeros_like(r)

        z, u, xhat, rstd, vn = _gmlp_recompute(z_ref, bin_ref, lng_ref, lnb_ref)
        mask = _window_mask()
        for g in range(GMLP_GROUPS):
            sh, c0 = g // per, (g % per) * gd
            wsm = jnp.where(mask, ws_ref[g], 0.0).astype(BF16)
            vg = vn[sh][:, c0:c0 + gd].astype(BF16)
            s = jnp.dot(wsm, vg, preferred_element_type=F32) + bs_ref[g]
            dgt = dg_ref[sh, :, c0:c0 + gd]
            ds = dgt * u[sh][:, c0:c0 + gd]
            du_ref[sh, :, c0:c0 + gd] = dgt * s
            dsb = ds.astype(BF16)
            dws = lax.dot_general(dsb, vg, (((1,), (1,)), ((), ())), preferred_element_type=F32)
            dws_ref[g] += jnp.where(mask, dws, 0.0)
            dbs_ref[g] += jnp.sum(ds, axis=-1, keepdims=True)
            dvn_ref[sh, :, c0:c0 + gd] = lax.dot_general(wsm, dsb, (((0,), (0,)), ((), ())),
                                                         preferred_element_type=F32)
        dvn = dvn_ref[...]
        dlng_ref[...] += jnp.sum(dvn * xhat, axis=1, keepdims=True)
        dlnb_ref[...] += jnp.sum(dvn, axis=1, keepdims=True)
        dxh = dvn * lng_ref[...]
        m1 = jnp.sum(jnp.sum(dxh, axis=0), axis=-1, keepdims=True) / width
        m2 = jnp.sum(jnp.sum(dxh * xhat, axis=0), axis=-1, keepdims=True) / width
        dv = rstd * (dxh - m1 - xhat * m2)
        gg = _gelu_grad(z)
        dzu = du_ref[...] * gg[:half]
        dzv = dv * gg[half:]
        dz_ref[:half] = dzu.astype(BF16)
        dz_ref[half:] = dzv.astype(BF16)
        dbin_ref[:half] += jnp.sum(dzu, axis=1, keepdims=True)
        dbin_ref[half:] += jnp.sum(dzv, axis=1, keepdims=True)

    whole = lambda a: pl.BlockSpec(tuple(a.shape), lambda i: (0,) * a.ndim)
    outs = [_sds((N_DEV, t, n), BF16), _sds(w_s.shape, F32), _sds(b_s.shape, F32),
            _sds(ln_g.shape, F32), _sds(ln_b.shape, F32), _sds(b_in.shape, F32)]
    return pl.pallas_call(
        body, name="gmlp_mid_bwd", grid=(t // w,),
        in_specs=[pl.BlockSpec((N_DEV, w, n), lambda i: (0, i, 0)),
                  pl.BlockSpec((half, w, n), lambda i: (0, i, 0)),
                  whole(b_in), whole(ln_g), whole(ln_b), whole(w_s), whole(b_s)],
        out_specs=[pl.BlockSpec((N_DEV, w, n), lambda i: (0, i, 0))] + [whole(o) for o in outs[1:]],
        out_shape=outs,
        scratch_shapes=[pltpu.VMEM((half, w, n), F32), pltpu.VMEM((half, w, n), F32)],
        compiler_params=_params(("arbitrary",)),
    )(zpre, dgated, b_in, ln_g, ln_b, w_s, b_s)


ATTN_CHUNKS = 4
ATTN_ROWS = ATTN_CHUNKS * CHUNK
ATTN_WINDOW = ATTN_ROWS + LEFT_PAD
ATTN_DIAGS = -(-(ATTN_ROWS + ATTN_WINDOW - 1) // LANES) * LANES
ATTN_ROLL = ATTN_DIAGS - (ATTN_ROWS - 1)


def _rel_vector(rel):
    j = np.arange(ATTN_DIAGS)
    idx = np.clip(ATTN_WINDOW - 1 - j, -(CHUNK - 1), MAX_REL) + (CHUNK - 1)
    return rel[:, idx]


def _attn_bias_mask(rel_ref, bm_ref):
    tt = lax.broadcasted_iota(jnp.int32, (ATTN_ROWS, ATTN_WINDOW), 0) // CHUNK
    rr = lax.broadcasted_iota(jnp.int32, (ATTN_ROWS, ATTN_WINDOW), 1) // CHUNK
    band = (rr >= tt) & (rr <= tt + LEFT_CHUNKS)
    for j in range(bm_ref.shape[0]):
        vec = jnp.broadcast_to(rel_ref[j:j + 1, :], (ATTN_ROWS, ATTN_DIAGS))
        toeplitz = pltpu.roll(vec, ATTN_ROLL, 1, stride=1, stride_axis=0)[:, :ATTN_WINDOW]
        bm_ref[j] = jnp.where(band, toeplitz, -jnp.inf)


def _attn_probs(q_ref, k_ref, bm_ref, j, hd, start, valid):
    qh = q_ref[:, j * hd:(j + 1) * hd]
    kb = k_ref[pl.ds(start, ATTN_WINDOW), j * hd:(j + 1) * hd]
    sc = lax.dot_general(qh, kb, (((1,), (1,)), ((), ())), preferred_element_type=F32)
    sc = sc * (hd ** -0.5) + bm_ref[j]
    sc = jnp.where(valid, sc, -jnp.inf)
    sc = sc - jnp.max(sc, axis=-1, keepdims=True)
    e = jnp.exp(sc)
    return e / jnp.sum(e, axis=-1, keepdims=True), qh, kb


def _window_valid(start):
    r = lax.broadcasted_iota(jnp.int32, (1, ATTN_WINDOW), 1)
    return (start + r) >= LEFT_PAD


def _attn_fwd(q, kvp, rel_vec):
    t, d = q.shape
    hd = d // N_HEADS
    half = N_DEV // 2
    n = kvp.shape[-1]
    per = n // hd
    rows = kvp.shape[1]

    def body(q_ref, k_ref, v_ref, rel_ref, o_ref, bm_ref):
        @pl.when(pl.program_id(1) == 0)
        def _():
            _attn_bias_mask(rel_ref, bm_ref)

        start = pl.multiple_of(pl.program_id(1) * ATTN_ROWS, ATTN_ROWS)
        valid = _window_valid(start)
        for j in range(per):
            p, _, _ = _attn_probs(q_ref, k_ref, bm_ref, j, hd, start, valid)
            vb = v_ref[pl.ds(start, ATTN_WINDOW), j * hd:(j + 1) * hd]
            o_ref[:, j * hd:(j + 1) * hd] = jnp.dot(p.astype(BF16), vb, preferred_element_type=F32).astype(BF16)

    return pl.pallas_call(
        body, name="attn_fwd", grid=(half, t // ATTN_ROWS),
        in_specs=[pl.BlockSpec((ATTN_ROWS, n), lambda g, i: (i, g)),
                  pl.BlockSpec((None, rows, n), lambda g, i: (g, 0, 0)),
                  pl.BlockSpec((None, rows, n), lambda g, i: (half + g, 0, 0)),
                  pl.BlockSpec((None, per, ATTN_DIAGS), lambda g, i: (g, 0, 0))],
        out_specs=pl.BlockSpec((ATTN_ROWS, n), lambda g, i: (i, g)),
        out_shape=_sds((t, d), BF16),
        scratch_shapes=[pltpu.VMEM((per, ATTN_ROWS, ATTN_WINDOW), F32)],
        compiler_params=_params(("arbitrary", "arbitrary")),
    )(q, kvp, kvp, rel_vec.reshape(half, per, ATTN_DIAGS))


def _attn_bwd(q, dout, kvp, rel_vec, dk_in=None, dv_in=None):
    t, d = q.shape
    hd = d // N_HEADS
    half = N_DEV // 2
    n = kvp.shape[-1]
    per = n // hd
    rows = kvp.shape[1]
    scale = hd ** -0.5
    carry = dk_in is not None

    def body(q_ref, do_ref, k_ref, v_ref, rel_ref, *rest):
        dq_ref, dk_ref, dv_ref, dsc_ref, bm_ref = rest[-5:]

        @pl.when(pl.program_id(1) == 0)
        def _():
            _attn_bias_mask(rel_ref, bm_ref)
            dk_ref[...] = rest[0][...] if carry else jnp.zeros_like(dk_ref)
            dv_ref[...] = rest[1][...] if carry else jnp.zeros_like(dv_ref)
            dsc_ref[...] = jnp.zeros_like(dsc_ref)

        start = pl.multiple_of(pl.program_id(1) * ATTN_ROWS, ATTN_ROWS)
        valid = _window_valid(start)
        for j in range(per):
            cols = slice(j * hd, (j + 1) * hd)
            p, qh, kb = _attn_probs(q_ref, k_ref, bm_ref, j, hd, start, valid)
            vb = v_ref[pl.ds(start, ATTN_WINDOW), cols]
            doh = do_ref[:, cols]
            dp = lax.dot_general(doh, vb, (((1,), (1,)), ((), ())), preferred_element_type=F32)
            ds = p * (dp - jnp.sum(dp * p, axis=-1, keepdims=True))
            dsc_ref[j] += sum(ds[a * CHUNK:(a + 1) * CHUNK, a * CHUNK:a * CHUNK + BAND]
                              for a in range(ATTN_CHUNKS))
            dsb = (ds * scale).astype(BF16)
            dq_ref[:, cols] = jnp.dot(dsb, kb, preferred_element_type=F32).astype(BF16)
            dk_ref[pl.ds(start, ATTN_WINDOW), cols] += lax.dot_general(
                dsb, qh, (((0,), (0,)), ((), ())), preferred_element_type=F32)
            dv_ref[pl.ds(start, ATTN_WINDOW), cols] += lax.dot_general(
                p.astype(BF16), doh, (((0,), (0,)), ((), ())), preferred_element_type=F32)

    tile = pl.BlockSpec((ATTN_ROWS, n), lambda g, i: (i, g))
    shard = pl.BlockSpec((None, rows, n), lambda g, i: (g, 0, 0))
    in_specs = [tile, tile, shard, pl.BlockSpec((None, rows, n), lambda g, i: (half + g, 0, 0)),
                pl.BlockSpec((None, per, ATTN_DIAGS), lambda g, i: (g, 0, 0))]
    operands = [q, dout, kvp, kvp, rel_vec.reshape(half, per, ATTN_DIAGS)]
    if carry:
        in_specs += [shard, shard]
        operands += [dk_in, dv_in]
    acc = _sds((half, rows, n), F32)
    return pl.pallas_call(
        body, name="attn_bwd", grid=(half, t // ATTN_ROWS),
        in_specs=in_specs,
        out_specs=[tile, shard, shard, pl.BlockSpec((per, CHUNK, BAND), lambda g, i: (g, 0, 0))],
        out_shape=[_sds((t, d), BF16), acc, acc, _sds((N_HEADS, CHUNK, BAND), F32)],
        scratch_shapes=[pltpu.VMEM((per, ATTN_ROWS, ATTN_WINDOW), F32)],
        compiler_params=_params(("arbitrary", "arbitrary")),
    )(*operands)


SKEW_PITCH = 640
SKEW = SKEW_PITCH + 1
SKEW_LANES = -(-SKEW // LANES) * LANES


def _skew_diagonals(dsc):
    h = dsc.shape[0]
    wide = jnp.pad(dsc, ((0, 0), (0, 0), (0, SKEW_PITCH - BAND))).reshape(h, CHUNK * SKEW_PITCH)
    wide = jnp.pad(wide, ((0, 0), (0, CHUNK))).reshape(h, CHUNK, SKEW)
    return jnp.pad(wide, ((0, 0), (0, 0), (0, SKEW_LANES - SKEW)))


def _rel_bias_grad(skewed):
    heads = skewed.shape[0]
    hb = SUBLANES

    def body(d_ref, o_ref):
        col = lax.broadcasted_iota(jnp.int32, (SKEW_LANES, N_REL), 0)
        bucket = lax.broadcasted_iota(jnp.int32, (SKEW_LANES, N_REL), 1)
        diag = jnp.where(col < BAND, col, col - SKEW)
        idx = jnp.clip(LEFT_PAD - diag, -(CHUNK - 1), MAX_REL) + (CHUNK - 1)
        oh = ((idx == bucket) & (col < SKEW)).astype(BF16)
        dv = jnp.sum(d_ref[...], axis=1)
        hi = dv.astype(BF16)
        rest = dv - hi.astype(F32)
        mid = rest.astype(BF16)
        lo = (rest - mid.astype(F32)).astype(BF16)
        acc = jnp.dot(hi, oh, preferred_element_type=F32)
        acc += jnp.dot(mid, oh, preferred_element_type=F32)
        acc += jnp.dot(lo, oh, preferred_element_type=F32)
        o_ref[...] = acc

    return pl.pallas_call(
        body, name="rel_bias_grad", grid=(heads // hb,),
        in_specs=[pl.BlockSpec((hb, CHUNK, SKEW_LANES), lambda i: (i, 0, 0))],
        out_specs=pl.BlockSpec((hb, N_REL), lambda i: (i, 0)),
        out_shape=_sds((heads, N_REL), F32),
        compiler_params=_params(("parallel",)),
    )(skewed)


def _sum_parts(parts):
    s_n, rows, c = parts.shape
    br = _row_block(rows, OPT_ROW_BLOCK)

    def body(p_ref, o_ref):
        acc = p_ref[0].astype(F32)
        for s in range(1, s_n):
            acc = acc + p_ref[s].astype(F32)
        o_ref[...] = acc

    return pl.pallas_call(
        body, name="sum_parts", grid=(rows // br,),
        in_specs=[pl.BlockSpec((s_n, br, c), lambda i: (0, i, 0))],
        out_specs=pl.BlockSpec((br, c), lambda i: (i, 0)),
        out_shape=_sds((rows, c), F32),
        compiler_params=_params(("parallel",)),
    )(parts)


def _adamw(own, own_idx, parts, w, m, v, row0=0, bufs=None, after=None):
    _, rows, c = own.shape
    s_n = 0 if parts is None else parts.shape[0]
    total = w.shape[0]
    br = _row_block(rows, OPT_ROW_BLOCK)
    assert row0 % br == 0 and (bufs is not None or (row0 == 0 and total == rows))
    b0 = row0 // br
    m_corr = 1.0 - ADAM_B1 ** ADAM_STEP
    v_corr = 1.0 - ADAM_B2 ** ADAM_STEP

    def body(idx_ref, own_ref, *refs):
        if s_n:
            p_ref, refs = refs[0], refs[1:]
        w_ref, m_ref, v_ref = refs[:3]
        g_ref, d_ref, nm_ref, nv_ref, done_ref = refs[-5:]
        done_ref[...] = jnp.zeros_like(done_ref)
        g = own_ref[...].astype(F32)
        for s in range(s_n):
            g = g + p_ref[s].astype(F32)
        nm = ADAM_B1 * m_ref[...] + (1.0 - ADAM_B1) * g
        nv = ADAM_B2 * v_ref[...] + (1.0 - ADAM_B2) * (g * g)
        g_ref[...] = g
        nm_ref[...] = nm
        nv_ref[...] = nv
        d_ref[...] = -ADAM_LR * ((nm / m_corr) / (jnp.sqrt(nv / v_corr) + ADAM_EPS) + ADAM_WD * w_ref[...])

    tile = pl.BlockSpec((br, c), lambda i, idx: (i + b0, 0))
    in_specs = [pl.BlockSpec((None, br, c), lambda i, idx: (idx[0], i, 0))]
    operands = [own_idx, own]
    if s_n:
        in_specs.append(pl.BlockSpec((s_n, br, c), lambda i, idx: (0, i, 0)))
        operands.append(parts)
    in_specs += [tile, tile, tile]
    operands += [w, m, v]
    aliases = {}
    if bufs is not None:
        aliases = {len(operands) + j: j for j in range(4)}
        in_specs += [_ANY] * 4
        operands += list(bufs)
    if after is not None:
        in_specs.append(_ANY)
        operands.append(after)
    out = _sds((total, c), F32)
    return pl.pallas_call(
        body, name="adamw",
        grid_spec=pltpu.PrefetchScalarGridSpec(
            num_scalar_prefetch=1, grid=(rows // br,), in_specs=in_specs,
            out_specs=[tile, tile, tile, tile,
                       pl.BlockSpec((SUBLANES, LANES), lambda i, idx: (0, 0))]),
        out_shape=[out, out, out, out, _token()],
        input_output_aliases=aliases,
        compiler_params=_params(("arbitrary",)),
    )(*operands)


def _chip_sum(p, r1, core):
    half = N_DEV // 2
    c = p.shape[-1]
    rows = int(np.prod(p.shape[1:-1]))
    br = _row_block(rows, BIG_ROW_BLOCK)

    def body(core_ref, p_ref, r_ref, o_ref):
        o_ref[...] = (p_ref[...].astype(F32) + r_ref[...].astype(F32)).astype(BF16)

    out = pl.pallas_call(
        body, name="chip_sum",
        grid_spec=pltpu.PrefetchScalarGridSpec(
            num_scalar_prefetch=1, grid=(half, rows // br),
            in_specs=[pl.BlockSpec((None, None, br, c), lambda q, i, cr: (q, cr[0], i, 0)),
                      pl.BlockSpec((None, br, c), lambda q, i, cr: (q, i, 0))],
            out_specs=pl.BlockSpec((None, br, c), lambda q, i, cr: (q, i, 0))),
        out_shape=_sds((half, rows, c), BF16),
        compiler_params=_params(("parallel", "parallel")),
    )(core, p.reshape(half, 2, rows, c), r1.reshape(half, rows, c))
    return out.reshape((half,) + p.shape[1:])


def _position():
    return tuple(lax.axis_index(a) for a in MESH_AXES)


def _linear(px, py, pc):
    return 4 * px + 2 * py + pc


def _all_gather_small(v, after=()):
    rows, lanes = v.shape

    def body(x_ref, *rest):
        out_ref, send_sems, recv_sems, local_sem = rest[-4:]
        x, y, c = _position()
        me, sibling = (x, y, c), (x, y, 1 - c)
        chips = [(1 - x, y), (x, 1 - y), (1 - x, 1 - y)]

        def copy(k, block, to, src=None):
            dst = out_ref.at[_linear(*block)]
            return pltpu.make_async_remote_copy(
                src_ref=dst if src is None else src, dst_ref=dst,
                send_sem=send_sems.at[k], recv_sem=recv_sems.at[k],
                device_id=to, device_id_type=MESH_ID)

        mine = pltpu.make_async_copy(x_ref, out_ref.at[_linear(*me)], local_sem)
        mine.start()
        first = [copy(0, me, sibling, src=x_ref)]
        first += [copy(1 + j, me, (*chip, c), src=x_ref) for j, chip in enumerate(chips)]
        for cp in first:
            cp.start()
        passed = [copy(4 + j, (*chip, c), sibling) for j, chip in enumerate(chips)]
        for j, chip in enumerate(chips):
            copy(1 + j, (*chip, c), me).wait_recv()
            passed[j].start()
        copy(0, sibling, me).wait_recv()
        for j, chip in enumerate(chips):
            copy(4 + j, (*chip, 1 - c), me).wait_recv()
        for cp in first + passed:
            cp.wait_send()
        mine.wait()

    return pl.pallas_call(
        body, name="all_gather_small",
        out_shape=_sds((N_DEV, rows, lanes), v.dtype),
        in_specs=[_VMEM] + [_ANY] * len(after), out_specs=_VMEM,
        scratch_shapes=[pltpu.SemaphoreType.DMA((7,)), pltpu.SemaphoreType.DMA((7,)),
                        pltpu.SemaphoreType.DMA],
        compiler_params=pltpu.CompilerParams(vmem_limit_bytes=VMEM_LIMIT),
    )(v, *after)


_HBM = pl.BlockSpec(memory_space=pltpu.HBM)
_SEM = pl.BlockSpec(memory_space=pltpu.SEMAPHORE)
_EFFECT = pltpu.SideEffectType.DATAFLOW_SIDE_EFFECTING
_ALL_CHIPS = [(0, 0), (0, 1), (1, 0), (1, 1)]


def _other_chips(x, y):
    return [(1 - x, y), (x, 1 - y), (1 - x, 1 - y)]


def _in_hbm(a):
    return pltpu.with_memory_space_constraint(a, pltpu.HBM)


def _token():
    return _sds((SUBLANES, LANES), F32)


def _gather_copies(refs, send_sems, recv_sems, to_sibling):
    x, y, c = _position()
    if to_sibling:
        plan = [((x, y, 1 - c), _linear(*q, c), _linear(*q, 1 - c)) for q in _ALL_CHIPS]
    else:
        plan = [((*ch, c), _linear(x, y, c), _linear(*ch, c)) for ch in _other_chips(x, y)]

    def copy(ref, i, k, peer, block):
        return pltpu.make_async_remote_copy(
            src_ref=ref.at[block], dst_ref=ref.at[block],
            send_sem=send_sems.at[len(plan) * i + k], recv_sem=recv_sems.at[len(plan) * i + k],
            device_id=peer, device_id_type=MESH_ID)

    return [(copy(ref, i, k, peer, sent), copy(ref, i, k, peer, landed))
            for i, ref in enumerate(refs) for k, (peer, sent, landed) in enumerate(plan)]


def _gather_ici_start(name, lands, after=None, to_sibling=False):
    n = len(lands)
    extra = [] if after is None else [after]
    n_sem = n * (len(_ALL_CHIPS) if to_sibling else len(_ALL_CHIPS) - 1)

    def body(*refs):
        send_sems, recv_sems, token = refs[-n - 3], refs[-n - 2], refs[-1]
        for sent, _ in _gather_copies(refs[:n], send_sems, recv_sems, to_sibling):
            sent.start()
        token[...] = jnp.zeros_like(token)

    out = pl.pallas_call(
        body, name=name,
        out_shape=(pltpu.SemaphoreType.DMA((n_sem,)), pltpu.SemaphoreType.DMA((n_sem,)),
                   *[pltpu.HBM(a.shape, a.dtype) for a in lands], _token()),
        in_specs=[_HBM] * n + [_ANY] * len(extra), out_specs=(_SEM, _SEM, *[_HBM] * n, _VMEM),
        input_output_aliases={i: 2 + i for i in range(n)},
        compiler_params=pltpu.CompilerParams(has_side_effects=_EFFECT),
    )(*[_in_hbm(a) for a in lands], *extra)
    return out[0], out[1], list(out[2:2 + n]), out[-1]


def _gather_ici_wait(name, lands, send_sems, recv_sems, after, to_sibling=False):
    n = len(lands)

    def body(*refs):
        for sent, landed in _gather_copies(refs[:n], refs[n], refs[n + 1], to_sibling):
            sent.wait_send()
            landed.wait_recv()

    out = pl.pallas_call(
        body, name=name,
        out_shape=[pltpu.HBM(a.shape, a.dtype) for a in lands],
        in_specs=[_HBM] * n + [_SEM, _SEM, _ANY], out_specs=[_HBM] * n,
        input_output_aliases={i: i for i in range(n)},
        compiler_params=pltpu.CompilerParams(has_side_effects=_EFFECT),
    )(*lands, send_sems, recv_sems, after)
    return list(out)


def _gather_d2d(lands):
    n = len(lands)

    def body(*refs):
        ins, outs, send_sems, recv_sems = refs[:n], refs[n:2 * n], refs[2 * n], refs[2 * n + 1]
        x, y, c = _position()

        def copy(i, q, core):
            block = _linear(*_ALL_CHIPS[q], core)
            return pltpu.make_async_remote_copy(
                src_ref=ins[i].at[block], dst_ref=outs[i].at[block],
                send_sem=send_sems.at[i, q], recv_sem=recv_sems.at[i, q],
                device_id=(x, y, 1 - c), device_id_type=MESH_ID)

        sent = [copy(i, q, c) for i in range(n) for q in range(len(_ALL_CHIPS))]
        for cp in sent:
            cp.start()
        for i in range(n):
            for q in range(len(_ALL_CHIPS)):
                copy(i, q, 1 - c).wait_recv()
        for cp in sent:
            cp.wait_send()

    return pl.pallas_call(
        body, name="gather_d2d",
        out_shape=[_sds(a.shape, a.dtype) for a in lands],
        in_specs=[_ANY] * n, out_specs=[_ANY] * n,
        input_output_aliases={i: i for i in range(n)},
        scratch_shapes=[pltpu.SemaphoreType.DMA((n, 4)), pltpu.SemaphoreType.DMA((n, 4))],
    )(*lands)


def _partials_d2d(parts):
    n = len(parts)
    half = N_DEV // 2

    def body(*refs):
        ins, outs, send_sems, recv_sems = refs[:n], refs[n:2 * n], refs[2 * n], refs[2 * n + 1]
        x, y, c = _position()

        def copy(i, q):
            return pltpu.make_async_remote_copy(
                src_ref=ins[i].at[_linear(*_ALL_CHIPS[q], 1 - c)], dst_ref=outs[i].at[q],
                send_sem=send_sems.at[i, q], recv_sem=recv_sems.at[i, q],
                device_id=(x, y, 1 - c), device_id_type=MESH_ID)

        sent = [copy(i, q) for i in range(n) for q in range(half)]
        for cp in sent:
            cp.start()
        for cp in sent:
            cp.wait_recv()
        for cp in sent:
            cp.wait_send()

    return pl.pallas_call(
        body, name="partials_d2d",
        out_shape=[_sds((half,) + p.shape[1:], p.dtype) for p in parts],
        in_specs=[_ANY] * n, out_specs=[_ANY] * n,
        scratch_shapes=[pltpu.SemaphoreType.DMA((n, half)), pltpu.SemaphoreType.DMA((n, half))],
    )(*parts)


def _partials_peers(x, y, c, direct):
    chips = _other_chips(x, y)
    if not direct:
        return [((*ch, c), 2 * ch[0] + ch[1]) for ch in chips]
    peers = [(x, y, 1 - c)] + [(*ch, c) for ch in chips] + [(*ch, 1 - c) for ch in chips]
    return [(p, _linear(*p)) for p in peers]


def _partials_copies(srcs, lands, send_sems, recv_sems, direct):
    x, y, c = _position()
    peers = _partials_peers(x, y, c, direct)
    return [pltpu.make_async_remote_copy(
        src_ref=srcs[i].at[block], dst_ref=lands[i].at[k],
        send_sem=send_sems.at[len(peers) * i + k], recv_sem=recv_sems.at[len(peers) * i + k],
        device_id=peer, device_id_type=MESH_ID)
        for i in range(len(srcs)) for k, (peer, block) in enumerate(peers)]


def _partials_send_start(name, srcs, lands, direct, after=None):
    n = len(srcs)
    n_sem = n * (N_DEV - 1 if direct else len(_ALL_CHIPS) - 1)

    def body(*refs):
        _, send_sems, recv_sems = refs[:2 * n], refs[-2 * n - 3], refs[-2 * n - 2]
        for cp in _partials_copies(refs[:n], refs[n:2 * n], send_sems, recv_sems, direct):
            cp.start()
        refs[-1][...] = jnp.zeros_like(refs[-1])

    both = list(srcs) + list(lands)
    extra = [] if after is None else [after]
    out = pl.pallas_call(
        body, name=name,
        out_shape=(pltpu.SemaphoreType.DMA((n_sem,)), pltpu.SemaphoreType.DMA((n_sem,)),
                   *[pltpu.HBM(a.shape, a.dtype) for a in both], _token()),
        in_specs=[_HBM] * (2 * n) + [_ANY] * len(extra), out_specs=(_SEM, _SEM, *[_HBM] * (2 * n), _VMEM),
        input_output_aliases={i: 2 + i for i in range(2 * n)},
        compiler_params=pltpu.CompilerParams(has_side_effects=_EFFECT),
    )(*[_in_hbm(a) for a in both], *extra)
    return out[0], out[1], list(out[2:2 + n]), list(out[2 + n:2 + 2 * n]), out[-1]


def _partials_send_wait(name, srcs, lands, send_sems, recv_sems, direct, after):
    n = len(srcs)

    def body(*refs):
        for cp in _partials_copies(refs[:n], refs[n:2 * n], refs[2 * n], refs[2 * n + 1], direct):
            cp.wait_send()
            cp.wait_recv()

    both = list(srcs) + list(lands)
    out = pl.pallas_call(
        body, name=name,
        out_shape=[pltpu.HBM(a.shape, a.dtype) for a in both],
        in_specs=[_HBM] * (2 * n) + [_SEM, _SEM, _ANY], out_specs=[_HBM] * (2 * n),
        input_output_aliases={i: i for i in range(2 * n)},
        compiler_params=pltpu.CompilerParams(has_side_effects=_EFFECT),
    )(*both, send_sems, recv_sems, after)
    return list(out[:n]), list(out[n:])


def _pack(arrs):
    flat = jnp.concatenate([a.reshape(-1).astype(F32) for a in arrs])
    block = OPT_ROW_BLOCK if flat.shape[0] > OPT_ROW_BLOCK * LANES else SUBLANES
    pad = (-flat.shape[0]) % (block * LANES)
    if pad:
        flat = jnp.concatenate([flat, jnp.zeros((pad,), F32)])
    return flat.reshape(-1, LANES)


def _unpack(packed, shapes, lead=()):
    flat = packed.reshape(lead + (-1,))
    out, off = [], 0
    for s in shapes:
        size = int(np.prod(s))
        out.append(flat[..., off:off + size].reshape(lead + tuple(s)))
        off += size
    return out


def _unshard_last(g):
    nd = g.ndim
    perm = tuple(range(1, nd - 1)) + (0, nd - 1)
    t = jnp.transpose(g, perm)
    return t.reshape(t.shape[:-2] + (N_DEV * g.shape[-1],))


def kernel(x, c, w_ada, b_ada, ln_g, ln_b, ffn_gu, ffn_down, gmlp_w_in, gmlp_b_in, gmlp_ln_g, gmlp_ln_b, gmlp_w_s, gmlp_b_s, gmlp_w_out, w_ada_kv, b_ada_kv, w_kv, attn_w_q, attn_rel_bias, attn_w_o, loss_target, m_w_ada, m_b_ada, m_ln_g, m_ln_b, m_ffn_gu, m_ffn_down, m_gmlp_w_in, m_gmlp_b_in, m_gmlp_ln_g, m_gmlp_ln_b, m_gmlp_w_s, m_gmlp_b_s, m_gmlp_w_out, m_w_ada_kv, m_b_ada_kv, m_w_kv, m_attn_w_q, m_attn_rel_bias, m_attn_w_o, v_w_ada, v_b_ada, v_ln_g, v_ln_b, v_ffn_gu, v_ffn_down, v_gmlp_w_in, v_gmlp_b_in, v_gmlp_ln_g, v_gmlp_ln_b, v_gmlp_w_s, v_gmlp_b_s, v_gmlp_w_out, v_w_ada_kv, v_b_ada_kv, v_w_kv, v_attn_w_q, v_attn_rel_bias, v_attn_w_o):
    weights = dict(w_ada=w_ada, b_ada=b_ada, ln_g=ln_g, ln_b=ln_b, ffn_gu=ffn_gu, ffn_down=ffn_down,
                   gmlp_w_in=gmlp_w_in, gmlp_b_in=gmlp_b_in, gmlp_ln_g=gmlp_ln_g, gmlp_ln_b=gmlp_ln_b,
                   gmlp_w_s=gmlp_w_s, gmlp_b_s=gmlp_b_s, gmlp_w_out=gmlp_w_out, w_ada_kv=w_ada_kv,
                   b_ada_kv=b_ada_kv, w_kv=w_kv, attn_w_q=attn_w_q, attn_rel_bias=attn_rel_bias,
                   attn_w_o=attn_w_o)
    mom1 = dict(w_ada=m_w_ada, b_ada=m_b_ada, ln_g=m_ln_g, ln_b=m_ln_b, ffn_gu=m_ffn_gu, ffn_down=m_ffn_down,
                gmlp_w_in=m_gmlp_w_in, gmlp_b_in=m_gmlp_b_in, gmlp_ln_g=m_gmlp_ln_g, gmlp_ln_b=m_gmlp_ln_b,
                gmlp_w_s=m_gmlp_w_s, gmlp_b_s=m_gmlp_b_s, gmlp_w_out=m_gmlp_w_out, w_ada_kv=m_w_ada_kv,
                b_ada_kv=m_b_ada_kv, w_kv=m_w_kv, attn_w_q=m_attn_w_q, attn_rel_bias=m_attn_rel_bias,
                attn_w_o=m_attn_w_o)
    mom2 = dict(w_ada=v_w_ada, b_ada=v_b_ada, ln_g=v_ln_g, ln_b=v_ln_b, ffn_gu=v_ffn_gu, ffn_down=v_ffn_down,
                gmlp_w_in=v_gmlp_w_in, gmlp_b_in=v_gmlp_b_in, gmlp_ln_g=v_gmlp_ln_g, gmlp_ln_b=v_gmlp_ln_b,
                gmlp_w_s=v_gmlp_w_s, gmlp_b_s=v_gmlp_b_s, gmlp_w_out=v_gmlp_w_out, w_ada_kv=v_w_ada_kv,
                b_ada_kv=v_b_ada_kv, w_kv=v_w_kv, attn_w_q=v_attn_w_q, attn_rel_bias=v_attn_rel_bias,
                attn_w_o=v_attn_w_o)
    order = list(weights)

    x = x[0]
    target = loss_target[0]
    t, d = x.shape
    n_mod = w_ada.shape[-1] * N_DEV // d
    mod_w = w_ada.shape[-1]
    kv_w = w_ada_kv.shape[-1]
    n_b = DEPTH - N_A
    me = _linear(*_position())

    l2 = DEPTH * 2
    big = dict(
        ffn_gu=ffn_gu.reshape((l2,) + ffn_gu.shape[2:]),
        ffn_down=ffn_down.reshape((l2,) + ffn_down.shape[2:]),
        gmlp_w_in=gmlp_w_in, gmlp_w_out=gmlp_w_out, w_kv=w_kv[None],
        attn_w_q=attn_w_q, attn_w_o=attn_w_o)
    big_names = list(big)
    core = lax.axis_index("c").astype(jnp.int32).reshape(1)
    chip = (2 * lax.axis_index("x") + lax.axis_index("y")).astype(jnp.int32).reshape(1)

    fwd_groups = [
        {"ffn_gu": (0, 1), "ffn_down": (0, 1)},
        {"gmlp_w_in": (0, 1), "gmlp_w_out": (0, 1)},
        {"ffn_gu": (1, 1), "ffn_down": (1, 1)},
        {"ffn_gu": (2, 1), "ffn_down": (2, 1)},
        {"gmlp_w_in": (1, 1), "gmlp_w_out": (1, 1)},
        {"ffn_gu": (3, 1), "ffn_down": (3, 1), "w_kv": (0, 1)},
        {"ffn_gu": (4, 1), "ffn_down": (4, 1)},
        {"attn_w_q": (0, 1), "attn_w_o": (0, 1)},
        {"ffn_gu": (5, 1), "ffn_down": (5, 1)},
        {"ffn_gu": (6, 1), "ffn_down": (6, 1)},
        {"attn_w_q": (1, 1), "attn_w_o": (1, 1)},
        {"ffn_gu": (7, 1), "ffn_down": (7, 1)},
    ]
    bwd_groups = []
    for l in range(DEPTH):
        g = {"ffn_gu": (2 * l, 2), "ffn_down": (2 * l, 2)}
        if l < N_A:
            g.update({"gmlp_w_in": (l, 1), "gmlp_w_out": (l, 1)})
        else:
            g.update({"attn_w_q": (l - N_A, 1), "attn_w_o": (l - N_A, 1)})
        if l == N_A - 1:
            g["w_kv"] = (0, 1)
        bwd_groups.append(g)

    def slot_of(groups, name, slot):
        for gi, g in enumerate(groups):
            if name in g and g[name][0] <= slot < g[name][0] + g[name][1]:
                return gi, slot - g[name][0]
        raise KeyError((name, slot))

    def start_group(gi, after=None):
        lands = []
        for name, (s0, cnt) in fwd_groups[gi].items():
            shard = big[name][s0:s0 + cnt].astype(BF16)
            land = lax.empty((N_DEV,) + shard.shape, BF16)
            lands.append(lax.dynamic_update_slice(land, shard[None], (me,) + (0,) * shard.ndim))
        return _gather_ici_start(f"gather_ici_start_{gi}", lands, after)

    gathered = [None] * len(fwd_groups)

    passing = {}

    def pass_on_early(gi, after):
        send_sems, recv_sems, lands, _ = flights[gi]
        lands = _gather_ici_wait(f"gather_ici_wait_{gi}", lands, send_sems, recv_sems, after)
        passing[gi] = _gather_ici_start(f"gather_d2d_start_{gi}", lands, to_sibling=True)
        return passing[gi][3]

    def land_group(gi, after):
        if gi in passing:
            send_sems, recv_sems, lands, _ = passing.pop(gi)
            lands = _gather_ici_wait(f"gather_d2d_wait_{gi}", lands, send_sems, recv_sems, after, to_sibling=True)
        else:
            send_sems, recv_sems, lands, _ = flights[gi]
            lands = _gather_d2d(_gather_ici_wait(f"gather_ici_wait_{gi}", lands, send_sems, recv_sems, after))
        gathered[gi] = dict(zip(fwd_groups[gi], lands))

    def weight(name, slot):
        gi, local = slot_of(fwd_groups, name, slot)
        return gathered[gi][name], local

    swapped = ("ffn_gu",)

    def grad_shape(name):
        s = big[name].shape[1:]
        return s[:-2] + (s[-1], s[-2]) if name in swapped else s

    partial = [{name: lax.empty((N_DEV, cnt) + grad_shape(name), BF16) for name, (_, cnt) in g.items()}
               for g in bwd_groups]

    c_all = _all_gather_small(_pack([c]))
    c_all = _unpack(c_all, [(d,)], lead=(N_DEV,))[0]
    c4 = _as4(c_all)
    mod_part = _matmul("ada_fwd", c4, w_ada[:, None], (DEPTH, 1, N_DEV, mod_w), F32, a_silu=True)
    kv_part = _matmul("ada_kv_fwd", c4, _as4(w_ada_kv), (1, 1, N_DEV, kv_w), F32, a_silu=True)
    small_shapes = [mod_part.shape, kv_part.shape, ln_g.shape, ln_b.shape, gmlp_b_in.shape,
                    gmlp_ln_g.shape, gmlp_ln_b.shape, attn_rel_bias.shape]
    small = _all_gather_small(_pack([mod_part, kv_part, ln_g, ln_b, gmlp_b_in, gmlp_ln_g, gmlp_ln_b,
                                     attn_rel_bias]))
    flights = [start_group(0, after=small)]
    flights += [start_group(gi, after=flights[0][3]) for gi in range(1, len(fwd_groups))]
    start_tokens = [f[3] for f in flights]
    (mod_g, kvm_g, ln_g_g, ln_b_g, b_in_g, gln_g_g, gln_b_g, rel_g) = _unpack(small, small_shapes, lead=(N_DEV,))
    mod_mine = lax.dynamic_index_in_dim(mod_g[:, :, 0], me, axis=2, keepdims=False)
    mod = _unshard_last(mod_mine) + b_ada
    mod = mod.reshape(DEPTH, n_mod, 1, d)
    kvm_mine = lax.dynamic_index_in_dim(kvm_g[:, 0, 0], me, axis=1, keepdims=False)
    mkv = (_unshard_last(kvm_mine) + b_ada_kv).reshape(2, 1, d)
    ln_g_f = _unshard_last(ln_g_g)
    ln_b_f = _unshard_last(ln_b_g)
    half = N_DEV // 2
    b_in_f = jnp.transpose(b_in_g, (1, 0, 2))[:, :, None, :]
    gln_g_f = _unshard_last(gln_g_g).reshape(N_A, half, 1, -1)
    gln_b_f = _unshard_last(gln_b_g).reshape(N_A, half, 1, -1)
    rel_f = _unshard_last(rel_g)

    def shard_act(a):
        return a.reshape(a.shape[0], a.shape[2], a.shape[3])

    def grad_into(name, slot, mm):
        gi, local = slot_of(bwd_groups, name, slot)
        partial[gi][name] = mm(partial[gi][name], local)

    def ffn_fwd(h, lw, after=None):
        w_gu, l_gu = weight("ffn_gu", lw)
        w_dn, l_dn = weight("ffn_down", lw)
        gu, a = _ffn_up_fwd(h, w_gu, l_gu, after)
        y = _matmul("ffn_down_fwd", a[:, None], w_dn, (1, 1, t, d), F32, lb=l_dn, b_merge=2, reduce=True)
        return y[0, 0], (gu, a)

    def ffn_bwd(dy, h, saved, lw):
        gu, a = saved
        w_gu, l_gu = weight("ffn_gu", lw)
        w_dn, l_dn = weight("ffn_down", lw)
        dgu = _ffn_down_bwd_a(dy, w_dn, l_dn, gu).reshape((N_DEV,) + gu.shape[2:])
        grad_into("ffn_down", lw, lambda buf, lo: _matmul(
            "ffn_down_bwd_w", a[:, None], _as4(dy), buf.shape, BF16, ta=True, lo=lo, out_merge=2, out_buf=buf))
        dh = _matmul("ffn_gu_bwd_a", dgu[:, None], w_gu, (1, 1, t, d), F32, lb=l_gu, tb=True, reduce=True)
        grad_into("ffn_gu", lw, lambda buf, lo: _matmul(
            "ffn_gu_bwd_w", dgu[:, None], _as4(h), buf.shape, BF16, ta=True, lo=lo, out_buf=buf))
        return dh[0, 0], {}

    def gmlp_params(l):
        return (b_in_f[l], gln_g_f[l], gln_b_f[l], gmlp_w_s[l], gmlp_b_s[l][:, :, None])

    def gmlp_fwd(h, l, after=None):
        w_in, l_in = weight("gmlp_w_in", l)
        w_out, l_out = weight("gmlp_w_out", l)
        n = w_in.shape[-1]
        zpre = _matmul("gmlp_in_fwd", _as4(h), w_in, (N_DEV, 1, t, n), F32, lb=l_in, after=after)
        gated = _gmlp_mid_fwd(shard_act(zpre), *gmlp_params(l))
        y = _matmul("gmlp_out_fwd", gated[:, None], w_out, (1, 1, t, d), F32, lb=l_out, b_merge=2, reduce=True)
        return y[0, 0], (zpre, gated)

    def gmlp_bwd(dy, h, saved, l):
        zpre, gated = saved
        w_in, l_in = weight("gmlp_w_in", l)
        w_out, l_out = weight("gmlp_w_out", l)
        n = w_in.shape[-1]
        dgated = _matmul("gmlp_out_bwd_a", _as4(dy), w_out, (half, 1, t, n), F32, lb=l_out, b_merge=2, tb=True)
        grad_into("gmlp_w_out", l, lambda buf, lo: _matmul(
            "gmlp_out_bwd_w", gated[:, None], _as4(dy), buf.shape, BF16, ta=True, lo=lo, out_merge=2, out_buf=buf))
        dz, dws, dbs, dlng, dlnb, dbin = _gmlp_mid_bwd(shard_act(zpre), shard_act(dgated), *gmlp_params(l))
        dh = _matmul("gmlp_in_bwd_a", dz[:, None], w_in, (1, 1, t, d), F32, lb=l_in, tb=True, reduce=True)
        grad_into("gmlp_w_in", l, lambda buf, lo: _matmul(
            "gmlp_in_bwd_w", _as4(h), dz[:, None], buf.shape, BF16, ta=True, lo=lo, out_buf=buf))
        small_grads = dict(gmlp_w_s=dws, gmlp_b_s=dbs[:, :, 0], gmlp_ln_g=dlng.reshape(-1),
                           gmlp_ln_b=dlnb.reshape(-1), gmlp_b_in=dbin.reshape(-1))
        return dh[0, 0], small_grads

    def attn_fwd(h, j, kvp, after=None):
        rel_vec = _rel_vector(rel_f[j])
        w_q, l_q = weight("attn_w_q", j)
        w_o, l_o = weight("attn_w_o", j)
        q = _matmul("attn_q_fwd", _as4(h), w_q, (1, 1, t, d), BF16, lb=l_q, b_merge=N_DEV, reduce=True,
                    after=after)[0, 0]
        o = _attn_fwd(q, kvp, rel_vec)
        y = _matmul("attn_o_fwd", _as4(o), w_o, (1, 1, t, d), F32, lb=l_o, b_merge=N_DEV, reduce=True)
        return y[0, 0], (q, o, rel_vec)

    def attn_bwd(dy, h, saved, j, kvp, dkv_acc):
        q, o, rel_vec = saved
        w_q, l_q = weight("attn_w_q", j)
        w_o, l_o = weight("attn_w_o", j)
        do = _matmul("attn_o_bwd_a", _as4(dy), w_o, (1, 1, t, d), BF16, lb=l_o, b_merge=N_DEV, tb=True)[0, 0]
        grad_into("attn_w_o", j, lambda buf, lo: _matmul(
            "attn_o_bwd_w", _as4(o), _as4(dy), buf.shape, BF16, ta=True, lo=lo, out_merge=N_DEV, out_buf=buf))
        dq, dk, dv, dsc = _attn_bwd(q, do, kvp, rel_vec, *dkv_acc)
        drel = _rel_bias_grad(_skew_diagonals(dsc))
        dh = _matmul("attn_q_bwd_a", _as4(dq), w_q, (1, 1, t, d), F32, lb=l_q, b_merge=N_DEV, tb=True)
        grad_into("attn_w_q", j, lambda buf, lo: _matmul(
            "attn_q_bwd_w", _as4(h), _as4(dq), buf.shape, BF16, ta=True, lo=lo, out_merge=N_DEV, out_buf=buf))
        return dh[0, 0], dict(attn_rel_bias=drel, dkv=(dk, dv))

    tape = []
    kvp = None
    kv_tape = None
    first_use = {(l, i): 3 * l + i for l in range(DEPTH) for i in range(3)}
    PASS_ON_EARLY_FROM = 7
    h = _modulate(x, mod[0, 1], mod[0, 0], after=start_tokens)
    for l in range(DEPTH):
        for i in range(3):
            if (l, i) in first_use:
                land_group(first_use[l, i], x)
            nl, ni = (l, i + 1) if i < 2 else (l + 1, 0)
            ahead = first_use.get((nl, ni), 0)
            started = pass_on_early(ahead, x) if ahead >= PASS_ON_EARLY_FROM else None
            scl, gate = mod[l, 3 * i + 1], mod[l, 3 * i + 2]
            wgt = 1.0 if i == 1 else 0.5
            gw = wgt * (1.0 + gate)
            if i != 1:
                y, saved = ffn_fwd(h, 2 * l + i // 2, started)
            elif l < N_A:
                y, saved = gmlp_fwd(h, l, started)
            else:
                y, saved = attn_fwd(h, l - N_A, kvp, started)
            readers = [(mod[nl, 3 * ni + 1], mod[nl, 3 * ni])] if nl < DEPTH else []
            shared_kv = (l, i) == (N_A - 1, 2)
            if shared_kv:
                readers.append((mkv[1], mkv[0]))
            outs = _ln_res_fwd(x, y, gw, ln_g_f[l, i][None], ln_b_f[l, i][None], readers)
            tape.append((x, h, y, gw, scl, saved))
            x = outs[0]
            h = outs[1] if nl < DEPTH else None
            if shared_kv:
                hkv = outs[-1]
                w_kvg, l_kv = weight("w_kv", 0)
                n = w_kvg.shape[-1]
                kv = _matmul("kv_fwd", _as4(hkv), w_kvg, (N_DEV, 1, t, n), BF16, lb=l_kv)
                kvp = jnp.pad(shard_act(kv), ((0, 0), (LEFT_PAD, 0), (0, 0)))
                kv_tape = hkv

    loss_part, dx = _loss_head(x, target)
    loss = lax.psum(loss_part[0, 0], MESH_AXES)

    d_mod = [[None] * n_mod for _ in range(DEPTH)]
    d_ln_g = [[None] * 3 for _ in range(DEPTH)]
    d_ln_b = [[None] * 3 for _ in range(DEPTH)]
    small_grads = {k: [None] * N_A for k in ("gmlp_w_s", "gmlp_b_s", "gmlp_ln_g", "gmlp_ln_b", "gmlp_b_in")}
    d_rel = [None] * n_b
    dkv_acc = ()
    d_mkv = None
    reductions = [None] * DEPTH
    sent_token = None
    readers = []
    for l in reversed(range(DEPTH)):
        if l == N_A - 1:
            hkv = kv_tape
            w_kvg, l_kv = weight("w_kv", 0)
            dkv = jnp.concatenate(dkv_acc)[:, LEFT_PAD:, :].astype(BF16)[:, None]
            dhkv = _matmul("kv_bwd_a", dkv, w_kvg, (1, 1, t, d), F32, lb=l_kv, tb=True, reduce=True)[0, 0]
            grad_into("w_kv", 0, lambda buf, lo: _matmul(
                "kv_bwd_w", _as4(hkv), dkv, buf.shape, BF16, ta=True, lo=lo, out_buf=buf))
            readers.append((dhkv, mkv[1], None))
        for i in reversed(range(3)):
            x_in, h, y, gw, scl, saved = tape[3 * l + i]
            wgt = 1.0 if i == 1 else 0.5
            if sent_token is not None:
                gw = gw + sent_token
                sent_token = None
            res = _ln_res_bwd(x_in, y, gw, ln_g_f[l, i][None], ln_b_f[l, i][None], dx,
                              [(r[0], r[1]) for r in readers])
            dx_res, dy, dgw, dg, db = res[:5]
            for k, (_, _, slot) in enumerate(readers):
                dscl_k, dshift_k = res[5 + 2 * k][0], res[6 + 2 * k][0]
                if slot is None:
                    d_mkv = jnp.concatenate([dshift_k, dscl_k])
                else:
                    d_mod[slot[0]][slot[1]], d_mod[slot[0]][slot[1] + 1] = dshift_k, dscl_k
            d_ln_g[l][i], d_ln_b[l][i] = dg[0], db[0]
            if i != 1:
                dh, extra = ffn_bwd(dy, h, saved, 2 * l + i // 2)
            elif l < N_A:
                dh, extra = gmlp_bwd(dy, h, saved, l)
                for k, g in extra.items():
                    small_grads[k][l] = g
            else:
                dh, extra = attn_bwd(dy, h, saved, l - N_A, kvp, dkv_acc)
                d_rel[l - N_A] = extra["attn_rel_bias"]
                dkv_acc = extra["dkv"]
            d_mod[l][3 * i + 2] = wgt * dgw[0]
            dx = dx_res
            readers = [(dh, scl, (l, 3 * i))]
        if l > 0:
            srcs = [partial[l][k] for k in bwd_groups[l]]
            lands = [lax.empty((N_DEV - 1,) + s.shape[1:], BF16) for s in srcs]
            reductions[l] = _partials_send_start(f"partials_send_start_{l}", srcs, lands, True)
            sent_token = reductions[l][4][0, 0]
    (dh, scl, _), = readers
    dx, dscl, dshift = _mod_bwd(dx, dh, tape[0][0], scl)
    d_mod[0][0], d_mod[0][1] = dshift[0], dscl[0]
    grad_x = dx[None]

    d_mod_arr = jnp.stack([jnp.concatenate(r) for r in d_mod])
    small_part = dict(
        b_ada=d_mod_arr, b_ada_kv=d_mkv,
        ln_g=jnp.stack([jnp.stack(r) for r in d_ln_g]), ln_b=jnp.stack([jnp.stack(r) for r in d_ln_b]),
        gmlp_b_in=jnp.stack(small_grads["gmlp_b_in"]), gmlp_ln_g=jnp.stack(small_grads["gmlp_ln_g"]),
        gmlp_ln_b=jnp.stack(small_grads["gmlp_ln_b"]), gmlp_w_s=jnp.stack(small_grads["gmlp_w_s"]),
        gmlp_b_s=jnp.stack(small_grads["gmlp_b_s"]), attn_rel_bias=jnp.stack(d_rel))
    small_names = list(small_part)
    sp_shapes = [small_part[k].shape for k in small_names]
    sp_all = _all_gather_small(_pack([small_part[k] for k in small_names]),
                               after=[partial[0][k] for k in bwd_groups[0]])

    from_sibling = _partials_d2d([partial[0][k] for k in bwd_groups[0]])
    sums = [_chip_sum(partial[0][k], r1, core) for k, r1 in zip(bwd_groups[0], from_sibling)]
    lands = [lax.empty((len(_ALL_CHIPS) - 1,) + s.shape[1:], BF16) for s in sums]
    reductions[0] = _partials_send_start("partials_send_start_0", sums, lands, False, after=sp_all)
    sent_token = reductions[0][4][0, 0]
    c4 = c4 + sent_token

    sp_sum = _sum_parts(sp_all)
    full_grads = dict(zip(small_names, _unpack(sp_sum, sp_shapes)))
    per_dev = dict(zip(small_names, _unpack(sp_all, sp_shapes, lead=(N_DEV,))))

    def my_cols(a, width):
        return lax.dynamic_slice_in_dim(a, me * width, width, axis=a.ndim - 1)

    grads = {}
    grads["b_ada"] = full_grads["b_ada"]
    grads["b_ada_kv"] = full_grads["b_ada_kv"]
    grads["gmlp_w_s"] = full_grads["gmlp_w_s"]
    grads["gmlp_b_s"] = full_grads["gmlp_b_s"]
    for k in ("ln_g", "ln_b", "gmlp_b_in", "gmlp_ln_g", "gmlp_ln_b", "attn_rel_bias"):
        grads[k] = my_cols(full_grads[k], weights[k].shape[-1])

    dmod_cols = jnp.transpose(my_cols(per_dev["b_ada"], mod_w), (1, 0, 2))[:, None]
    grads["w_ada"] = _matmul("ada_bwd_w", c4, dmod_cols, (DEPTH, 1, d, mod_w), F32, ta=True,
                             a_silu=True)[:, 0]
    dkv_cols = my_cols(per_dev["b_ada_kv"], kv_w)[None, None]
    grads["w_ada_kv"] = _matmul("ada_kv_bwd_w", c4, dkv_cols, (1, 1, d, kv_w), F32, ta=True,
                                a_silu=True)[0, 0]

    delta, new_m, new_v = {}, {}, {}
    first = jnp.zeros((1,), jnp.int32)

    def flat2(a, cols):
        return a.reshape(-1, cols)

    done = None
    for k in ("w_ada", "w_ada_kv"):
        w = weights[k]
        cols = w.shape[-1]
        res = _adamw(grads[k].reshape(1, -1, cols), first, None, flat2(w, cols), flat2(mom1[k], cols),
                     flat2(mom2[k], cols), after=done)
        grads[k], delta[k], new_m[k], new_v[k] = (a.reshape(w.shape) for a in res[:4])
        done = res[4]

    tiny = [k for k in order if k not in delta and k not in big_names]
    tiny_shapes = [weights[k].shape for k in tiny]
    tiny_out = _adamw((_pack([grads[k] for k in tiny]) + sent_token)[None], first, None,
                      _pack([weights[k] for k in tiny]), _pack([mom1[k] for k in tiny]),
                      _pack([mom2[k] for k in tiny]), after=done)
    for dst, arr in zip((grads, delta, new_m, new_v), tiny_out):
        for k, val in zip(tiny, _unpack(arr, tiny_shapes)):
            dst[k] = val

    def opt_view(k, a):
        a = jnp.swapaxes(a, -1, -2) if k in swapped else a
        return a.reshape(-1, a.shape[-1])

    def opt_unview(k, a):
        s = weights[k].shape
        return jnp.swapaxes(a.reshape(s[:-2] + (s[-1], s[-2])), -1, -2) if k in swapped else a.reshape(s)

    bufs = {k: [lax.empty(opt_view(k, weights[k]).shape, F32) for _ in range(4)] for k in big_names}
    done = tiny_out[4]
    me_idx = me.astype(jnp.int32).reshape(1)
    for l in reversed(range(DEPTH)):
        send_sems, recv_sems, srcs, lands, _ = reductions[l]
        srcs, lands = _partials_send_wait(f"partials_send_wait_{l}", srcs, lands, send_sems, recv_sems, l > 0, done)
        for k, own, got in zip(bwd_groups[l], srcs, lands):
            cols = own.shape[-1]
            slot_rows = int(np.prod(own.shape[2:-1]))
            *bufs[k], done = _adamw(own.reshape(own.shape[0], -1, cols), me_idx if l > 0 else chip,
                                    got.reshape(got.shape[0], -1, cols),
                                    opt_view(k, weights[k]), opt_view(k, mom1[k]), opt_view(k, mom2[k]),
                                    row0=bwd_groups[l][k][0] * slot_rows, bufs=bufs[k], after=done)
    for k in big_names:
        grads[k], delta[k], new_m[k], new_v[k] = (opt_unview(k, b) for b in bufs[k])

    return (loss, grad_x, *[grads[k] for k in order], *[delta[k] for k in order],
            *[new_m[k] for k in order], *[new_v[k] for k in order])
```

```python
import numpy as np
import jax
import jax.numpy as jnp
from jax import lax
from jax.experimental import pallas as pl
from jax.experimental.pallas import tpu as pltpu

F32 = jnp.float32
BF16 = jnp.bfloat16
MESH_AXES = ("x", "y", "c")
N_DEV = 8
MESH_ID = pl.DeviceIdType.MESH

DEPTH = 4
N_A = 2
CHUNK = 64
N_HEADS = 16
LEFT_CHUNKS = 8
BAND = (LEFT_CHUNKS + 1) * CHUNK
LEFT_PAD = LEFT_CHUNKS * CHUNK
MAX_REL = 4 * CHUNK
N_REL = (CHUNK - 1) + MAX_REL + 1
GMLP_WINDOW = 128
GMLP_GROUPS = 8
ALPHA = (2.0 * DEPTH) ** 0.25
LN_EPS = 1e-5
ADAM_LR = 0.001
ADAM_B1 = 0.9
ADAM_B2 = 0.999
ADAM_EPS = 1e-08
ADAM_WD = 0.01
ADAM_STEP = 10

V7X_VMEM_BYTES = 64 * 1024 * 1024
VMEM_LIMIT = V7X_VMEM_BYTES - 8 * 1024 * 1024
LANES = 128
SUBLANES = 8
MM_BLOCK = 2048
BIG_ROW_BLOCK = 1024
ROW_BLOCK = 512
OPT_ROW_BLOCK = 256

_ANY = pl.BlockSpec(memory_space=pl.ANY)
_VMEM = pl.BlockSpec(memory_space=pltpu.VMEM)


def _params(sem=None):
    return pltpu.CompilerParams(dimension_semantics=sem, vmem_limit_bytes=VMEM_LIMIT)


def _row_block(rows, target):
    for d in range(min(rows, target), 0, -1):
        if rows % d == 0 and (d % SUBLANES == 0 or d == rows):
            return d
    return rows


def _matmul(name, a, b, out_shape4, out_dtype, *, la=0, lb=0, lo=0, ta=False, tb=False,
            reduce=False, b_merge=1, out_merge=1, out_buf=None, a_silu=False, after=None):
    ja_n, _, a_r, a_c = a.shape
    jb_n, _, b_r, b_c = b.shape
    jo_n, _, o_r, o_c = out_shape4
    m_tot = a_c if ta else a_r
    k_a = a_r if ta else a_c
    b_rows = b_merge * b_r
    k_c = b_c if tb else b_rows
    n = b_rows if tb else b_c
    n_chunks = (jb_n // b_merge) if reduce else 1
    natural_k = reduce and ja_n == 1
    assert n == o_c, (name, n, o_c)
    assert k_a ==(k_c * n_chunks if natural_k else k_c), (name, k_a, k_c, n_chunks)
    bk = k_c if (k_c <= MM_BLOCK or (b_merge > 1 and not tb)) else MM_BLOCK
    assert k_c % bk == 0
    nkk = k_c // bk
    kg = 2 if (reduce and ja_n > 1 and nkk == 1 and not ta and n_chunks % 2 == 0) else 1
    nk = n_chunks * nkk // kg
    m_out = out_merge * o_r
    assert m_tot == m_out, (name, m_tot, m_out)
    bm = m_tot if (m_tot <= MM_BLOCK or out_merge > 1) else MM_BLOCK
    assert m_tot % bm == 0
    jo_blocks = jo_n // out_merge

    def a_index(j, m, k):
        kj, kk = k // nkk, k % nkk
        ja = 0 if ja_n == 1 else (kj if reduce else j)
        ke = kk + kj * nkk if natural_k else kk
        return (ja, la, ke, m) if ta else (ja, la, m, ke)

    def b_index(j, m, k):
        kj, kk = k // nkk, k % nkk
        jb = 0 if jb_n == b_merge else (kj if reduce else j)
        return (jb, lb, 0, kk) if tb else (jb, lb, kk, 0)

    def o_index(j, m, k):
        return (j, lo, 0, 0) if out_merge > 1 else (j, lo, m, 0)

    a_block = (None, None, bk, bm) if ta else (None if kg == 1 else kg, None, bm, bk)
    if b_merge > 1:
        b_block = (kg * b_merge, None, b_r, bk if tb else n)
    else:
        b_block = (None if kg == 1 else kg, None) + ((n, bk) if tb else (bk, n))
    o_block = (out_merge, None, o_r, n) if out_merge > 1 else (None, None, bm, n)
    dims = (((0 if ta else 1,), (1 if tb else 0,)), ((), ()))

    in_place = nk > 1 and out_dtype == F32 and out_merge == 1
    use_acc = nk > 1 and not in_place

    def body(a_ref, b_ref, *rest):
        o_ref = rest[-2] if use_acc else rest[-1]
        k = pl.program_id(2)
        av = a_ref[...]
        if a_silu:
            af = av.astype(F32)
            av = af * jax.nn.sigmoid(af)
        bv = b_ref[...]
        if kg > 1:
            bv = bv.reshape(kg, -1, bv.shape[-1])
            prod = sum(lax.dot_general(av[g].astype(BF16), bv[g].astype(BF16), dims, preferred_element_type=F32)
                       for g in range(kg))
        else:
            if b_merge > 1:
                bv = bv.reshape(b_rows, bv.shape[-1])
            prod = lax.dot_general(av.astype(BF16), bv.astype(BF16), dims, preferred_element_type=F32)

        def emit(val):
            val = val.astype(out_dtype)
            o_ref[...] = val.reshape(out_merge, o_r, n) if out_merge > 1 else val

        if nk == 1:
            emit(prod)
            return
        acc_ref = o_ref if in_place else rest[-1]

        @pl.when(k == 0)
        def _():
            acc_ref[...] = prod

        @pl.when(k > 0)
        def _():
            acc_ref[...] += prod

        if use_acc:
            @pl.when(k == nk - 1)
            def _():
                emit(acc_ref[...])

    in_specs = [pl.BlockSpec(a_block, a_index), pl.BlockSpec(b_block, b_index)]
    operands = [a, b]
    aliases = {}
    if out_buf is not None:
        assert out_buf.shape == tuple(out_shape4) and out_buf.dtype == out_dtype
        in_specs.append(_ANY)
        operands.append(out_buf)
        aliases = {2: 0}
    if after is not None:
        in_specs.append(_ANY)
        operands.append(after)
    return pl.pallas_call(
        body, name=name,
        grid=(jo_blocks, m_tot // bm, nk),
        in_specs=in_specs,
        out_specs=pl.BlockSpec(o_block, o_index),
        out_shape=jax.ShapeDtypeStruct(tuple(out_shape4), out_dtype),
        scratch_shapes=[pltpu.VMEM((bm, n), F32)] if use_acc else [],
        input_output_aliases=aliases,
        compiler_params=_params(("parallel", "parallel", "arbitrary")),
    )(*operands)


def _as4(a):
    return a.reshape((1,) * (4 - a.ndim) + a.shape)


def _row_call(name, body, ins, outs, t, *, acc_outs=()):
    bt = _row_block(t, ROW_BLOCK)

    def spec(arr, tiled):
        if tiled:
            return pl.BlockSpec((bt,) + tuple(arr.shape[1:]), lambda i: (i,) + (0,) * (arr.ndim - 1))
        return pl.BlockSpec(tuple(arr.shape), lambda i: (0,) * arr.ndim)

    return pl.pallas_call(
        body, name=name, grid=(t // bt,),
        in_specs=[spec(a, tl) for a, tl in ins],
        out_specs=[spec(o, tl) for o, tl in outs],
        out_shape=[jax.ShapeDtypeStruct(o.shape, o.dtype) for o, _ in outs],
        compiler_params=_params(("arbitrary",) if acc_outs else ("parallel",)),
    )(*[a for a, _ in ins])


def _sds(shape, dtype):
    return jax.ShapeDtypeStruct(tuple(shape), dtype)


def _modulate(x, scl, shift):
    t, d = x.shape

    def body(x_ref, s_ref, b_ref, h_ref):
        h_ref[...] = (x_ref[...] * (1.0 + s_ref[...]) + b_ref[...]).astype(BF16)

    return _row_call("modulate", body, [(x, True), (scl, False), (shift, False)],
                     [(_sds((t, d), BF16), True)], t)[0]


def _ln_stats(r):
    mu = jnp.mean(r, axis=-1, keepdims=True)
    rc = r - mu
    var = jnp.mean(rc * rc, axis=-1, keepdims=True)
    rstd = lax.rsqrt(var + LN_EPS)
    return rc * rstd, rstd


def _ln_res_fwd(x, y, gw, g, b, mods=()):
    t, d = x.shape
    n_mod = len(mods)

    def body(x_ref, y_ref, gw_ref, g_ref, b_ref, *rest):
        mod_refs, o_ref, h_refs = rest[:2 * n_mod], rest[2 * n_mod], rest[2 * n_mod + 1:]
        r = ALPHA * x_ref[...] + gw_ref[...] * y_ref[...]
        xhat, _ = _ln_stats(r)
        xn = xhat * g_ref[...] + b_ref[...]
        o_ref[...] = xn
        for k in range(n_mod):
            h_refs[k][...] = (xn * (1.0 + mod_refs[2 * k][...]) + mod_refs[2 * k + 1][...]).astype(BF16)

    vecs = [(v, False) for pair in mods for v in pair]
    return _row_call("ln_res_fwd", body,
                     [(x, True), (y, True), (gw, False), (g, False), (b, False)] + vecs,
                     [(_sds((t, d), F32), True)] + [(_sds((t, d), BF16), True)] * n_mod, t)


def _ln_res_bwd(x, y, gw, g, b, dx_base, pairs=()):
    t, d = x.shape
    n_pair = len(pairs)

    def body(x_ref, y_ref, gw_ref, g_ref, b_ref, dxb_ref, *rest):
        pair_refs, outs = rest[:2 * n_pair], rest[2 * n_pair:]
        dx_ref, dy_ref = outs[0], outs[1]
        sums = outs[2:]

        @pl.when(pl.program_id(0) == 0)
        def _():
            for r in sums:
                r[...] = jnp.zeros_like(r)

        yv = y_ref[...]
        gwv = gw_ref[...]
        gv = g_ref[...]
        xhat, rstd = _ln_stats(ALPHA * x_ref[...] + gwv * yv)
        dxn = dxb_ref[...]
        if n_pair:
            xn = xhat * gv + b_ref[...]
            for k in range(n_pair):
                dh = pair_refs[2 * k][...]
                dxn = dxn + dh * (1.0 + pair_refs[2 * k + 1][...])
                sums[3 + 2 * k][...] += jnp.sum(dh * xn, axis=0, keepdims=True)
                sums[4 + 2 * k][...] += jnp.sum(dh, axis=0, keepdims=True)
        dxh = dxn * gv
        m1 = jnp.mean(dxh, axis=-1, keepdims=True)
        m2 = jnp.mean(dxh * xhat, axis=-1, keepdims=True)
        dr = rstd * (dxh - m1 - xhat * m2)
        dx_ref[...] = ALPHA * dr
        dy_ref[...] = (gwv * dr).astype(BF16)
        sums[0][...] += jnp.sum(dr * yv, axis=0, keepdims=True)
        sums[1][...] += jnp.sum(dxn * xhat, axis=0, keepdims=True)
        sums[2][...] += jnp.sum(dxn, axis=0, keepdims=True)

    vec = _sds((1, d), F32)
    n_sum = 3 + 2 * n_pair
    ins = [(x, True), (y, True), (gw, False), (g, False), (b, False), (dx_base, True)]
    for dh, scl in pairs:
        ins += [(dh, True), (scl, False)]
    return _row_call("ln_res_bwd", body, ins,
                     [(_sds((t, d), F32), True), (_sds((t, d), BF16), True)] + [(vec, False)] * n_sum, t,
                     acc_outs=tuple(range(2, 2 + n_sum)))


def _mod_bwd(dx_res, dh, x, scl):
    t, d = x.shape

    def body(dxr_ref, dh_ref, x_ref, s_ref, dx_ref, ds_ref, db_ref):
        @pl.when(pl.program_id(0) == 0)
        def _():
            ds_ref[...] = jnp.zeros_like(ds_ref)
            db_ref[...] = jnp.zeros_like(db_ref)

        dh = dh_ref[...]
        dx_ref[...] = dxr_ref[...] + dh * (1.0 + s_ref[...])
        ds_ref[...] += jnp.sum(dh * x_ref[...], axis=0, keepdims=True)
        db_ref[...] += jnp.sum(dh, axis=0, keepdims=True)

    vec = _sds((1, d), F32)
    return _row_call("mod_bwd", body, [(dx_res, True), (dh, True), (x, True), (scl, False)],
                     [(_sds((t, d), F32), True), (vec, False), (vec, False)], t, acc_outs=(1, 2))


def _loss_head(y, target):
    t, d = y.shape

    def body(y_ref, t_ref, l_ref, dy_ref):
        @pl.when(pl.program_id(0) == 0)
        def _():
            l_ref[...] = jnp.zeros_like(l_ref)

        err = y_ref[...] - t_ref[...]
        dy_ref[...] = err * (1.0 / d)
        part = 0.5 * jnp.sum(jnp.mean(err * err, axis=-1, keepdims=True), axis=0, keepdims=True)
        l_ref[...] += jnp.broadcast_to(part, l_ref.shape)

    return _row_call("loss_head", body, [(y, True), (target, True)],
                     [(_sds((SUBLANES, LANES), F32), False), (_sds((t, d), F32), True)], t,
                     acc_outs=(0,))


def _sigmoid(x):
    return 0.5 * jnp.tanh(0.5 * x) + 0.5


def _ffn_up_fwd(h, w_gu, lb, after=None):
    t, d = h.shape
    n = w_gu.shape[-1]
    half = N_DEV // 2
    bt = _row_block(t, BIG_ROW_BLOCK)
    extra = [] if after is None else [after]

    def body(h_ref, wg_ref, wu_ref, *rest):
        fac_ref, a_ref = rest[-2:]
        hv = h_ref[...]
        g = jnp.dot(hv, wg_ref[...], preferred_element_type=F32)
        u = jnp.dot(hv, wu_ref[...], preferred_element_type=F32)
        sig = _sigmoid(g)
        silu = g * sig
        fac_ref[0] = u * (sig + silu * (1.0 - sig))
        fac_ref[1] = silu
        a_ref[...] = (silu * u).astype(BF16)

    return pl.pallas_call(
        body, name="ffn_up_fwd", grid=(half, t // bt),
        in_specs=[pl.BlockSpec((bt, d), lambda j, i: (i, 0)),
                  pl.BlockSpec((None, None, d, n), lambda j, i: (j, lb, 0, 0)),
                  pl.BlockSpec((None, None, d, n), lambda j, i: (half + j, lb, 0, 0))] + [_ANY] * len(extra),
        out_specs=[pl.BlockSpec((2, None, bt, n), lambda j, i: (0, j, i, 0)),
                   pl.BlockSpec((None, bt, n), lambda j, i: (j, i, 0))],
        out_shape=[_sds((2, half, t, n), F32), _sds((half, t, n), BF16)],
        compiler_params=_params(("parallel", "parallel")),
    )(h, w_gu, w_gu, *extra)


def _ffn_down_bwd_a(dy, w_down, lb, fac):
    t, d = dy.shape
    _, half, _, n = fac.shape
    r = w_down.shape[2]
    bt = _row_block(t, MM_BLOCK)

    def body(dy_ref, w_ref, fac_ref, d_ref):
        da = lax.dot_general(dy_ref[...], w_ref[...].reshape(2 * r, d), (((1,), (1,)), ((), ())),
                             preferred_element_type=F32)
        d_ref[0] = (da * fac_ref[0]).astype(BF16)
        d_ref[1] = (da * fac_ref[1]).astype(BF16)

    return pl.pallas_call(
        body, name="ffn_down_bwd_a", grid=(half, t // bt),
        in_specs=[pl.BlockSpec((bt, d), lambda j, i: (i, 0)),
                  pl.BlockSpec((2, None, r, d), lambda j, i: (j, lb, 0, 0)),
                  pl.BlockSpec((2, None, bt, n), lambda j, i: (0, j, i, 0))],
        out_specs=pl.BlockSpec((2, None, bt, n), lambda j, i: (0, j, i, 0)),
        out_shape=_sds((2, half, t, n), BF16),
        compiler_params=_params(("parallel", "parallel")),
    )(dy, w_down, fac)


_INV_SQRT2 = 0.7071067811865476
_INV_SQRT_2PI = 0.3989422804014327


def _gelu(z):
    return 0.5 * z * (1.0 + lax.erf(z * _INV_SQRT2))


def _gelu_grad(z):
    return 0.5 * (1.0 + lax.erf(z * _INV_SQRT2)) + z * jnp.exp(-0.5 * z * z) * _INV_SQRT_2PI


def _window_mask():
    t_out = lax.broadcasted_iota(jnp.int32, (GMLP_WINDOW, GMLP_WINDOW), 0)
    s_in = lax.broadcasted_iota(jnp.int32, (GMLP_WINDOW, GMLP_WINDOW), 1)
    return (s_in // CHUNK) <= (t_out // CHUNK)


def _gmlp_recompute(z_ref, bin_ref, lng_ref, lnb_ref):
    half = N_DEV // 2
    z = z_ref[...] + bin_ref[...]
    ge = _gelu(z)
    u = ge[:half]
    v = ge[half:]
    width = half * v.shape[-1]
    mu = jnp.sum(jnp.sum(v, axis=0), axis=-1, keepdims=True) / width
    vc = v - mu
    var = jnp.sum(jnp.sum(vc * vc, axis=0), axis=-1, keepdims=True) / width
    rstd = lax.rsqrt(var + LN_EPS)
    xhat = vc * rstd
    vn = xhat * lng_ref[...] + lnb_ref[...]
    return z, u, xhat, rstd, vn


def _gmlp_mid_fwd(zpre, b_in, ln_g, ln_b, w_s, b_s):
    _, t, n = zpre.shape
    half = N_DEV // 2
    gd = half * n // GMLP_GROUPS
    per = n // gd
    w = GMLP_WINDOW

    def body(z_ref, bin_ref, lng_ref, lnb_ref, ws_ref, bs_ref, o_ref):
        _, u, _, _, vn = _gmlp_recompute(z_ref, bin_ref, lng_ref, lnb_ref)
        mask = _window_mask()
        for g in range(GMLP_GROUPS):
            sh, c0 = g // per, (g % per) * gd
            wsm = jnp.where(mask, ws_ref[g], 0.0).astype(BF16)
            s = jnp.dot(wsm, vn[sh][:, c0:c0 + gd].astype(BF16), preferred_element_type=F32) + bs_ref[g]
            o_ref[sh, :, c0:c0 + gd] = (u[sh][:, c0:c0 + gd] * s).astype(BF16)

    whole = lambda a: pl.BlockSpec(tuple(a.shape), lambda i: (0,) * a.ndim)
    return pl.pallas_call(
        body, name="gmlp_mid_fwd", grid=(t // w,),
        in_specs=[pl.BlockSpec((N_DEV, w, n), lambda i: (0, i, 0)),
                  whole(b_in), whole(ln_g), whole(ln_b), whole(w_s), whole(b_s)],
        out_specs=pl.BlockSpec((half, w, n), lambda i: (0, i, 0)),
        out_shape=_sds((half, t, n), BF16),
        compiler_params=_params(("parallel",)),
    )(zpre, b_in, ln_g, ln_b, w_s, b_s)


def _gmlp_mid_bwd(zpre, dgated, b_in, ln_g, ln_b, w_s, b_s):
    _, t, n = zpre.shape
    half = N_DEV // 2
    gd = half * n // GMLP_GROUPS
    per = n // gd
    w = GMLP_WINDOW
    width = half * n

    def body(z_ref, dg_ref, bin_ref, lng_ref, lnb_ref, ws_ref, bs_ref,
             dz_ref, dws_ref, dbs_ref, dlng_ref, dlnb_ref, dbin_ref, du_ref, dvn_ref):
        @pl.when(pl.program_id(0) == 0)
        def _():
            for r in (dws_ref, dbs_ref, dlng_ref, dlnb_ref, dbin_ref):
                r[...] = jnp.zeros_like(r)

        z, u, xhat, rstd, vn = _gmlp_recompute(z_ref, bin_ref, lng_ref, lnb_ref)
        mask = _window_mask()
        for g in range(GMLP_GROUPS):
            sh, c0 = g // per, (g % per) * gd
            wsm = jnp.where(mask, ws_ref[g], 0.0).astype(BF16)
            vg = vn[sh][:, c0:c0 + gd].astype(BF16)
            s = jnp.dot(wsm, vg, preferred_element_type=F32) + bs_ref[g]
            dgt = dg_ref[sh, :, c0:c0 + gd]
            ds = dgt * u[sh][:, c0:c0 + gd]
            du_ref[sh, :, c0:c0 + gd] = dgt * s
            dsb = ds.astype(BF16)
            dws = lax.dot_general(dsb, vg, (((1,), (1,)), ((), ())), preferred_element_type=F32)
            dws_ref[g] += jnp.where(mask, dws, 0.0)
            dbs_ref[g] += jnp.sum(ds, axis=-1, keepdims=True)
            dvn_ref[sh, :, c0:c0 + gd] = lax.dot_general(wsm, dsb, (((0,), (0,)), ((), ())),
                                                         preferred_element_type=F32)
        dvn = dvn_ref[...]
        dlng_ref[...] += jnp.sum(dvn * xhat, axis=1, keepdims=True)
        dlnb_ref[...] += jnp.sum(dvn, axis=1, keepdims=True)
        dxh = dvn * lng_ref[...]
        m1 = jnp.sum(jnp.sum(dxh, axis=0), axis=-1, keepdims=True) / width
        m2 = jnp.sum(jnp.sum(dxh * xhat, axis=0), axis=-1, keepdims=True) / width
        dv = rstd * (dxh - m1 - xhat * m2)
        gg = _gelu_grad(z)
        dzu = du_ref[...] * gg[:half]
        dzv = dv * gg[half:]
        dz_ref[:half] = dzu.astype(BF16)
        dz_ref[half:] = dzv.astype(BF16)
        dbin_ref[:half] += jnp.sum(dzu, axis=1, keepdims=True)
        dbin_ref[half:] += jnp.sum(dzv, axis=1, keepdims=True)

    whole = lambda a: pl.BlockSpec(tuple(a.shape), lambda i: (0,) * a.ndim)
    outs = [_sds((N_DEV, t, n), BF16), _sds(w_s.shape, F32), _sds(b_s.shape, F32),
            _sds(ln_g.shape, F32), _sds(ln_b.shape, F32), _sds(b_in.shape, F32)]
    return pl.pallas_call(
        body, name="gmlp_mid_bwd", grid=(t // w,),
        in_specs=[pl.BlockSpec((N_DEV, w, n), lambda i: (0, i, 0)),
                  pl.BlockSpec((half, w, n), lambda i: (0, i, 0)),
                  whole(b_in), whole(ln_g), whole(ln_b), whole(w_s), whole(b_s)],
        out_specs=[pl.BlockSpec((N_DEV, w, n), lambda i: (0, i, 0))] + [whole(o) for o in outs[1:]],
        out_shape=outs,
        scratch_shapes=[pltpu.VMEM((half, w, n), F32), pltpu.VMEM((half, w, n), F32)],
        compiler_params=_params(("arbitrary",)),
    )(zpre, dgated, b_in, ln_g, ln_b, w_s, b_s)


ATTN_CHUNKS = 4
ATTN_ROWS = ATTN_CHUNKS * CHUNK
ATTN_WINDOW = ATTN_ROWS + LEFT_PAD
ATTN_DIAGS = -(-(ATTN_ROWS + ATTN_WINDOW - 1) // LANES) * LANES
ATTN_ROLL = ATTN_DIAGS - (ATTN_ROWS - 1)


def _rel_vector(rel):
    j = np.arange(ATTN_DIAGS)
    idx = np.clip(ATTN_WINDOW - 1 - j, -(CHUNK - 1), MAX_REL) + (CHUNK - 1)
    return rel[:, idx]


def _attn_bias_mask(rel_ref, bm_ref):
    tt = lax.broadcasted_iota(jnp.int32, (ATTN_ROWS, ATTN_WINDOW), 0) // CHUNK
    rr = lax.broadcasted_iota(jnp.int32, (ATTN_ROWS, ATTN_WINDOW), 1) // CHUNK
    band = (rr >= tt) & (rr <= tt + LEFT_CHUNKS)
    for j in range(bm_ref.shape[0]):
        vec = jnp.broadcast_to(rel_ref[j:j + 1, :], (ATTN_ROWS, ATTN_DIAGS))
        toeplitz = pltpu.roll(vec, ATTN_ROLL, 1, stride=1, stride_axis=0)[:, :ATTN_WINDOW]
        bm_ref[j] = jnp.where(band, toeplitz, -jnp.inf)


def _scale_is_exact(hd):
    return np.log2(hd) % 2 == 0


def _attn_probs(q_ref, k_ref, bm_ref, j, hd, start, valid):
    qh = q_ref[:, j * hd:(j + 1) * hd]
    kb = k_ref[pl.ds(start, ATTN_WINDOW), j * hd:(j + 1) * hd]
    scale = hd ** -0.5
    if _scale_is_exact(hd):
        qh = qh * scale
    sc = lax.dot_general(qh, kb, (((1,), (1,)), ((), ())), preferred_element_type=F32)
    sc = (sc if _scale_is_exact(hd) else sc * scale) + bm_ref[j]
    sc = jnp.where(valid, sc, -jnp.inf)
    sc = sc - jnp.max(sc, axis=-1, keepdims=True)
    e = jnp.exp(sc)
    return e / jnp.sum(e, axis=-1, keepdims=True), qh, kb


def _window_valid(start):
    r = lax.broadcasted_iota(jnp.int32, (1, ATTN_WINDOW), 1)
    return (start + r) >= LEFT_PAD


def _attn_fwd(q, kvp, rel_vec):
    t, d = q.shape
    hd = d // N_HEADS
    half = N_DEV // 2
    n = kvp.shape[-1]
    per = n // hd
    rows = kvp.shape[1]

    def body(q_ref, k_ref, v_ref, rel_ref, o_ref, bm_ref):
        @pl.when(pl.program_id(1) == 0)
        def _():
            _attn_bias_mask(rel_ref, bm_ref)

        start = pl.multiple_of(pl.program_id(1) * ATTN_ROWS, ATTN_ROWS)
        valid = _window_valid(start)
        for j in range(per):
            p, _, _ = _attn_probs(q_ref, k_ref, bm_ref, j, hd, start, valid)
            vb = v_ref[pl.ds(start, ATTN_WINDOW), j * hd:(j + 1) * hd]
            o_ref[:, j * hd:(j + 1) * hd] = jnp.dot(p.astype(BF16), vb, preferred_element_type=F32).astype(BF16)

    return pl.pallas_call(
        body, name="attn_fwd", grid=(half, t // ATTN_ROWS),
        in_specs=[pl.BlockSpec((ATTN_ROWS, n), lambda g, i: (i, g)),
                  pl.BlockSpec((None, rows, n), lambda g, i: (g, 0, 0)),
                  pl.BlockSpec((None, rows, n), lambda g, i: (half + g, 0, 0)),
                  pl.BlockSpec((None, per, ATTN_DIAGS), lambda g, i: (g, 0, 0))],
        out_specs=pl.BlockSpec((ATTN_ROWS, n), lambda g, i: (i, g)),
        out_shape=_sds((t, d), BF16),
        scratch_shapes=[pltpu.VMEM((per, ATTN_ROWS, ATTN_WINDOW), F32)],
        compiler_params=_params(("arbitrary", "arbitrary")),
    )(q, kvp, kvp, rel_vec.reshape(half, per, ATTN_DIAGS))


def _attn_bwd(q, dout, kvp, rel_vec, dk_in=None, dv_in=None):
    t, d = q.shape
    hd = d // N_HEADS
    half = N_DEV // 2
    n = kvp.shape[-1]
    per = n // hd
    rows = kvp.shape[1]
    scale = hd ** -0.5
    carry = dk_in is not None

    def body(q_ref, do_ref, k_ref, v_ref, rel_ref, *rest):
        dq_ref, dk_ref, dv_ref, dsc_ref, bm_ref = rest[-5:]

        @pl.when(pl.program_id(1) == 0)
        def _():
            _attn_bias_mask(rel_ref, bm_ref)
            dk_ref[...] = rest[0][...] if carry else jnp.zeros_like(dk_ref)
            dv_ref[...] = rest[1][...] if carry else jnp.zeros_like(dv_ref)
            dsc_ref[...] = jnp.zeros_like(dsc_ref)

        start = pl.multiple_of(pl.program_id(1) * ATTN_ROWS, ATTN_ROWS)
        valid = _window_valid(start)
        for j in range(per):
            cols = slice(j * hd, (j + 1) * hd)
            p, qh, kb = _attn_probs(q_ref, k_ref, bm_ref, j, hd, start, valid)
            vb = v_ref[pl.ds(start, ATTN_WINDOW), cols]
            doh = do_ref[:, cols]
            dp = lax.dot_general(doh, vb, (((1,), (1,)), ((), ())), preferred_element_type=F32)
            ds = p * (dp - jnp.sum(dp * p, axis=-1, keepdims=True))
            dsc_ref[j] += sum(ds[a * CHUNK:(a + 1) * CHUNK, a * CHUNK:a * CHUNK + BAND]
                              for a in range(ATTN_CHUNKS))
            if _scale_is_exact(hd):
                dsb = ds.astype(BF16)
                dq_ref[:, cols] = (jnp.dot(dsb, kb, preferred_element_type=F32) * scale).astype(BF16)
            else:
                dsb = (ds * scale).astype(BF16)
                dq_ref[:, cols] = jnp.dot(dsb, kb, preferred_element_type=F32).astype(BF16)
            dk_ref[pl.ds(start, ATTN_WINDOW), cols] += lax.dot_general(
                dsb, qh, (((0,), (0,)), ((), ())), preferred_element_type=F32)
            dv_ref[pl.ds(start, ATTN_WINDOW), cols] += lax.dot_general(
                p.astype(BF16), doh, (((0,), (0,)), ((), ())), preferred_element_type=F32)

    tile = pl.BlockSpec((ATTN_ROWS, n), lambda g, i: (i, g))
    shard = pl.BlockSpec((None, rows, n), lambda g, i: (g, 0, 0))
    in_specs = [tile, tile, shard, pl.BlockSpec((None, rows, n), lambda g, i: (half + g, 0, 0)),
                pl.BlockSpec((None, per, ATTN_DIAGS), lambda g, i: (g, 0, 0))]
    operands = [q, dout, kvp, kvp, rel_vec.reshape(half, per, ATTN_DIAGS)]
    if carry:
        in_specs += [shard, shard]
        operands += [dk_in, dv_in]
    acc = _sds((half, rows, n), F32)
    return pl.pallas_call(
        body, name="attn_bwd", grid=(half, t // ATTN_ROWS),
        in_specs=in_specs,
        out_specs=[tile, shard, shard, pl.BlockSpec((per, CHUNK, BAND), lambda g, i: (g, 0, 0))],
        out_shape=[_sds((t, d), BF16), acc, acc, _sds((N_HEADS, CHUNK, BAND), F32)],
        scratch_shapes=[pltpu.VMEM((per, ATTN_ROWS, ATTN_WINDOW), F32)],
        compiler_params=_params(("arbitrary", "arbitrary")),
    )(*operands)


SKEW_PITCH = 640
SKEW = SKEW_PITCH + 1
SKEW_LANES = -(-SKEW // LANES) * LANES


def _skew_diagonals(dsc):
    h = dsc.shape[0]
    wide = jnp.pad(dsc, ((0, 0), (0, 0), (0, SKEW_PITCH - BAND))).reshape(h, CHUNK * SKEW_PITCH)
    wide = jnp.pad(wide, ((0, 0), (0, CHUNK))).reshape(h, CHUNK, SKEW)
    return jnp.pad(wide, ((0, 0), (0, 0), (0, SKEW_LANES - SKEW)))


def _rel_bias_grad(skewed):
    heads = skewed.shape[0]
    hb = SUBLANES

    def body(d_ref, o_ref):
        col = lax.broadcasted_iota(jnp.int32, (SKEW_LANES, N_REL), 0)
        bucket = lax.broadcasted_iota(jnp.int32, (SKEW_LANES, N_REL), 1)
        diag = jnp.where(col < BAND, col, col - SKEW)
        idx = jnp.clip(LEFT_PAD - diag, -(CHUNK - 1), MAX_REL) + (CHUNK - 1)
        oh = ((idx == bucket) & (col < SKEW)).astype(BF16)
        dv = jnp.sum(d_ref[...], axis=1)
        hi = dv.astype(BF16)
        rest = dv - hi.astype(F32)
        mid = rest.astype(BF16)
        lo = (rest - mid.astype(F32)).astype(BF16)
        acc = jnp.dot(hi, oh, preferred_element_type=F32)
        acc += jnp.dot(mid, oh, preferred_element_type=F32)
        acc += jnp.dot(lo, oh, preferred_element_type=F32)
        o_ref[...] = acc

    return pl.pallas_call(
        body, name="rel_bias_grad", grid=(heads // hb,),
        in_specs=[pl.BlockSpec((hb, CHUNK, SKEW_LANES), lambda i: (i, 0, 0))],
        out_specs=pl.BlockSpec((hb, N_REL), lambda i: (i, 0)),
        out_shape=_sds((heads, N_REL), F32),
        compiler_params=_params(("parallel",)),
    )(skewed)


def _sum_parts(parts):
    s_n, rows, c = parts.shape
    br = _row_block(rows, OPT_ROW_BLOCK)

    def body(p_ref, o_ref):
        acc = p_ref[0].astype(F32)
        for s in range(1, s_n):
            acc = acc + p_ref[s].astype(F32)
        o_ref[...] = acc

    return pl.pallas_call(
        body, name="sum_parts", grid=(rows // br,),
        in_specs=[pl.BlockSpec((s_n, br, c), lambda i: (0, i, 0))],
        out_specs=pl.BlockSpec((br, c), lambda i: (i, 0)),
        out_shape=_sds((rows, c), F32),
        compiler_params=_params(("parallel",)),
    )(parts)


def _adamw(own, own_idx, parts, w, m, v, row0=0, bufs=None, after=None):
    _, rows, c = own.shape
    s_n = 0 if parts is None else parts.shape[0]
    total = w.shape[0]
    br = _row_block(rows, OPT_ROW_BLOCK)
    assert row0 % br == 0 and (bufs is not None or (row0 == 0 and total == rows))
    b0 = row0 // br
    m_corr = 1.0 - ADAM_B1 ** ADAM_STEP
    v_corr = 1.0 - ADAM_B2 ** ADAM_STEP

    def body(idx_ref, own_ref, *refs):
        if s_n:
            p_ref, refs = refs[0], refs[1:]
        w_ref, m_ref, v_ref = refs[:3]
        g_ref, d_ref, nm_ref, nv_ref, done_ref = refs[-5:]
        done_ref[...] = jnp.zeros_like(done_ref)
        g = own_ref[...].astype(F32)
        for s in range(s_n):
            g = g + p_ref[s].astype(F32)
        nm = ADAM_B1 * m_ref[...] + (1.0 - ADAM_B1) * g
        nv = ADAM_B2 * v_ref[...] + (1.0 - ADAM_B2) * (g * g)
        g_ref[...] = g
        nm_ref[...] = nm
        nv_ref[...] = nv
        d_ref[...] = -ADAM_LR * ((nm / m_corr) / (jnp.sqrt(nv / v_corr) + ADAM_EPS) + ADAM_WD * w_ref[...])

    tile = pl.BlockSpec((br, c), lambda i, idx: (i + b0, 0))
    in_specs = [pl.BlockSpec((None, br, c), lambda i, idx: (idx[0], i, 0))]
    operands = [own_idx, own]
    if s_n:
        in_specs.append(pl.BlockSpec((s_n, br, c), lambda i, idx: (0, i, 0)))
        operands.append(parts)
    in_specs += [tile, tile, tile]
    operands += [w, m, v]
    aliases = {}
    if bufs is not None:
        aliases = {len(operands) + j: j for j in range(4)}
        in_specs += [_ANY] * 4
        operands += list(bufs)
    if after is not None:
        in_specs.append(_ANY)
        operands.append(after)
    out = _sds((total, c), F32)
    return pl.pallas_call(
        body, name="adamw",
        grid_spec=pltpu.PrefetchScalarGridSpec(
            num_scalar_prefetch=1, grid=(rows // br,), in_specs=in_specs,
            out_specs=[tile, tile, tile, tile,
                       pl.BlockSpec((SUBLANES, LANES), lambda i, idx: (0, 0))]),
        out_shape=[out, out, out, out, _token()],
        input_output_aliases=aliases,
        compiler_params=_params(("arbitrary",)),
    )(*operands)


def _chip_sum(p, r1, core):
    half = N_DEV // 2
    c = p.shape[-1]
    rows = int(np.prod(p.shape[1:-1]))
    br = _row_block(rows, BIG_ROW_BLOCK)

    def body(core_ref, p_ref, r_ref, o_ref):
        o_ref[...] = (p_ref[...].astype(F32) + r_ref[...].astype(F32)).astype(BF16)

    out = pl.pallas_call(
        body, name="chip_sum",
        grid_spec=pltpu.PrefetchScalarGridSpec(
            num_scalar_prefetch=1, grid=(half, rows // br),
            in_specs=[pl.BlockSpec((None, None, br, c), lambda q, i, cr: (q, cr[0], i, 0)),
                      pl.BlockSpec((None, br, c), lambda q, i, cr: (q, i, 0))],
            out_specs=pl.BlockSpec((None, br, c), lambda q, i, cr: (q, i, 0))),
        out_shape=_sds((half, rows, c), BF16),
        compiler_params=_params(("parallel", "parallel")),
    )(core, p.reshape(half, 2, rows, c), r1.reshape(half, rows, c))
    return out.reshape((half,) + p.shape[1:])


def _position():
    return tuple(lax.axis_index(a) for a in MESH_AXES)


def _linear(px, py, pc):
    return 4 * px + 2 * py + pc


def _all_gather_small(v, after=()):
    rows, lanes = v.shape

    def body(x_ref, *rest):
        out_ref, send_sems, recv_sems, local_sem = rest[-4:]
        x, y, c = _position()
        me, sibling = (x, y, c), (x, y, 1 - c)
        chips = [(1 - x, y), (x, 1 - y), (1 - x, 1 - y)]

        def copy(k, block, to, src=None):
            dst = out_ref.at[_linear(*block)]
            return pltpu.make_async_remote_copy(
                src_ref=dst if src is None else src, dst_ref=dst,
                send_sem=send_sems.at[k], recv_sem=recv_sems.at[k],
                device_id=to, device_id_type=MESH_ID)

        mine = pltpu.make_async_copy(x_ref, out_ref.at[_linear(*me)], local_sem)
        mine.start()
        first = [copy(0, me, sibling, src=x_ref)]
        first += [copy(1 + j, me, (*chip, c), src=x_ref) for j, chip in enumerate(chips)]
        for cp in first:
            cp.start()
        passed = [copy(4 + j, (*chip, c), sibling) for j, chip in enumerate(chips)]
        for j, chip in enumerate(chips):
            copy(1 + j, (*chip, c), me).wait_recv()
            passed[j].start()
        copy(0, sibling, me).wait_recv()
        for j, chip in enumerate(chips):
            copy(4 + j, (*chip, 1 - c), me).wait_recv()
        for cp in first + passed:
            cp.wait_send()
        mine.wait()

    return pl.pallas_call(
        body, name="all_gather_small",
        out_shape=_sds((N_DEV, rows, lanes), v.dtype),
        in_specs=[_VMEM] + [_ANY] * len(after), out_specs=_VMEM,
        scratch_shapes=[pltpu.SemaphoreType.DMA((7,)), pltpu.SemaphoreType.DMA((7,)),
                        pltpu.SemaphoreType.DMA],
        compiler_params=pltpu.CompilerParams(vmem_limit_bytes=VMEM_LIMIT),
    )(v, *after)


_HBM = pl.BlockSpec(memory_space=pltpu.HBM)
_SEM = pl.BlockSpec(memory_space=pltpu.SEMAPHORE)
_EFFECT = pltpu.SideEffectType.DATAFLOW_SIDE_EFFECTING
_ALL_CHIPS = [(0, 0), (0, 1), (1, 0), (1, 1)]


def _other_chips(x, y):
    return [(1 - x, y), (x, 1 - y), (1 - x, 1 - y)]


def _in_hbm(a):
    return pltpu.with_memory_space_constraint(a, pltpu.HBM)


def _token():
    return _sds((SUBLANES, LANES), F32)


def _gather_copies(refs, send_sems, recv_sems, to_sibling):
    x, y, c = _position()
    if to_sibling:
        plan = [((x, y, 1 - c), _linear(*q, c), _linear(*q, 1 - c)) for q in _ALL_CHIPS]
    else:
        plan = [((*ch, c), _linear(x, y, c), _linear(*ch, c)) for ch in _other_chips(x, y)]

    def copy(ref, i, k, peer, block):
        return pltpu.make_async_remote_copy(
            src_ref=ref.at[block], dst_ref=ref.at[block],
            send_sem=send_sems.at[len(plan) * i + k], recv_sem=recv_sems.at[len(plan) * i + k],
            device_id=peer, device_id_type=MESH_ID)

    return [(copy(ref, i, k, peer, sent), copy(ref, i, k, peer, landed))
            for i, ref in enumerate(refs) for k, (peer, sent, landed) in enumerate(plan)]


def _gather_ici_start(name, lands, after=None, to_sibling=False):
    n = len(lands)
    extra = [] if after is None else [after]
    n_sem = n * (len(_ALL_CHIPS) if to_sibling else len(_ALL_CHIPS) - 1)

    def body(*refs):
        send_sems, recv_sems, token = refs[-n - 3], refs[-n - 2], refs[-1]
        for sent, _ in _gather_copies(refs[:n], send_sems, recv_sems, to_sibling):
            sent.start()
        token[...] = jnp.zeros_like(token)

    out = pl.pallas_call(
        body, name=name,
        out_shape=(pltpu.SemaphoreType.DMA((n_sem,)), pltpu.SemaphoreType.DMA((n_sem,)),
                   *[pltpu.HBM(a.shape, a.dtype) for a in lands], _token()),
        in_specs=[_HBM] * n + [_ANY] * len(extra), out_specs=(_SEM, _SEM, *[_HBM] * n, _VMEM),
        input_output_aliases={i: 2 + i for i in range(n)},
        compiler_params=pltpu.CompilerParams(has_side_effects=_EFFECT),
    )(*[_in_hbm(a) for a in lands], *extra)
    return out[0], out[1], list(out[2:2 + n]), out[-1]


def _gather_ici_wait(name, lands, send_sems, recv_sems, after, to_sibling=False):
    n = len(lands)

    def body(*refs):
        for sent, landed in _gather_copies(refs[:n], refs[n], refs[n + 1], to_sibling):
            sent.wait_send()
            landed.wait_recv()

    out = pl.pallas_call(
        body, name=name,
        out_shape=[pltpu.HBM(a.shape, a.dtype) for a in lands],
        in_specs=[_HBM] * n + [_SEM, _SEM, _ANY], out_specs=[_HBM] * n,
        input_output_aliases={i: i for i in range(n)},
        compiler_params=pltpu.CompilerParams(has_side_effects=_EFFECT),
    )(*lands, send_sems, recv_sems, after)
    return list(out)


def _gather_d2d(lands):
    n = len(lands)

    def body(*refs):
        ins, outs, send_sems, recv_sems = refs[:n], refs[n:2 * n], refs[2 * n], refs[2 * n + 1]
        x, y, c = _position()

        def copy(i, q, core):
            block = _linear(*_ALL_CHIPS[q], core)
            return pltpu.make_async_remote_copy(
                src_ref=ins[i].at[block], dst_ref=outs[i].at[block],
                send_sem=send_sems.at[i, q], recv_sem=recv_sems.at[i, q],
                device_id=(x, y, 1 - c), device_id_type=MESH_ID)

        sent = [copy(i, q, c) for i in range(n) for q in range(len(_ALL_CHIPS))]
        for cp in sent:
            cp.start()
        for i in range(n):
            for q in range(len(_ALL_CHIPS)):
                copy(i, q, 1 - c).wait_recv()
        for cp in sent:
            cp.wait_send()

    return pl.pallas_call(
        body, name="gather_d2d",
        out_shape=[_sds(a.shape, a.dtype) for a in lands],
        in_specs=[_ANY] * n, out_specs=[_ANY] * n,
        input_output_aliases={i: i for i in range(n)},
        scratch_shapes=[pltpu.SemaphoreType.DMA((n, 4)), pltpu.SemaphoreType.DMA((n, 4))],
    )(*lands)


def _partials_d2d(parts):
    n = len(parts)
    half = N_DEV // 2

    def body(*refs):
        ins, outs, send_sems, recv_sems = refs[:n], refs[n:2 * n], refs[2 * n], refs[2 * n + 1]
        x, y, c = _position()

        def copy(i, q):
            return pltpu.make_async_remote_copy(
                src_ref=ins[i].at[_linear(*_ALL_CHIPS[q], 1 - c)], dst_ref=outs[i].at[q],
                send_sem=send_sems.at[i, q], recv_sem=recv_sems.at[i, q],
                device_id=(x, y, 1 - c), device_id_type=MESH_ID)

        sent = [copy(i, q) for i in range(n) for q in range(half)]
        for cp in sent:
            cp.start()
        for cp in sent:
            cp.wait_recv()
        for cp in sent:
            cp.wait_send()

    return pl.pallas_call(
        body, name="partials_d2d",
        out_shape=[_sds((half,) + p.shape[1:], p.dtype) for p in parts],
        in_specs=[_ANY] * n, out_specs=[_ANY] * n,
        scratch_shapes=[pltpu.SemaphoreType.DMA((n, half)), pltpu.SemaphoreType.DMA((n, half))],
    )(*parts)


def _partials_peers(x, y, c, direct):
    chips = _other_chips(x, y)
    if not direct:
        return [((*ch, c), 2 * ch[0] + ch[1]) for ch in chips]
    peers = [(x, y, 1 - c)] + [(*ch, c) for ch in chips] + [(*ch, 1 - c) for ch in chips]
    return [(p, _linear(*p)) for p in peers]


def _partials_copies(srcs, lands, send_sems, recv_sems, direct):
    x, y, c = _position()
    peers = _partials_peers(x, y, c, direct)
    return [pltpu.make_async_remote_copy(
        src_ref=srcs[i].at[block], dst_ref=lands[i].at[k],
        send_sem=send_sems.at[len(peers) * i + k], recv_sem=recv_sems.at[len(peers) * i + k],
        device_id=peer, device_id_type=MESH_ID)
        for i in range(len(srcs)) for k, (peer, block) in enumerate(peers)]


def _partials_send_start(name, srcs, lands, direct, after=None):
    n = len(srcs)
    n_sem = n * (N_DEV - 1 if direct else len(_ALL_CHIPS) - 1)

    def body(*refs):
        _, send_sems, recv_sems = refs[:2 * n], refs[-2 * n - 3], refs[-2 * n - 2]
        for cp in _partials_copies(refs[:n], refs[n:2 * n], send_sems, recv_sems, direct):
            cp.start()
        refs[-1][...] = jnp.zeros_like(refs[-1])

    both = list(srcs) + list(lands)
    extra = [] if after is None else [after]
    out = pl.pallas_call(
        body, name=name,
        out_shape=(pltpu.SemaphoreType.DMA((n_sem,)), pltpu.SemaphoreType.DMA((n_sem,)),
                   *[pltpu.HBM(a.shape, a.dtype) for a in both], _token()),
        in_specs=[_HBM] * (2 * n) + [_ANY] * len(extra), out_specs=(_SEM, _SEM, *[_HBM] * (2 * n), _VMEM),
        input_output_aliases={i: 2 + i for i in range(2 * n)},
        compiler_params=pltpu.CompilerParams(has_side_effects=_EFFECT),
    )(*[_in_hbm(a) for a in both], *extra)
    return out[0], out[1], list(out[2:2 + n]), list(out[2 + n:2 + 2 * n]), out[-1]


def _partials_send_wait(name, srcs, lands, send_sems, recv_sems, direct, after):
    n = len(srcs)

    def body(*refs):
        for cp in _partials_copies(refs[:n], refs[n:2 * n], refs[2 * n], refs[2 * n + 1], direct):
            cp.wait_send()
            cp.wait_recv()

    both = list(srcs) + list(lands)
    out = pl.pallas_call(
        body, name=name,
        out_shape=[pltpu.HBM(a.shape, a.dtype) for a in both],
        in_specs=[_HBM] * (2 * n) + [_SEM, _SEM, _ANY], out_specs=[_HBM] * (2 * n),
        input_output_aliases={i: i for i in range(2 * n)},
        compiler_params=pltpu.CompilerParams(has_side_effects=_EFFECT),
    )(*both, send_sems, recv_sems, after)
    return list(out[:n]), list(out[n:])


def _pack(arrs):
    flat = jnp.concatenate([a.reshape(-1).astype(F32) for a in arrs])
    block = OPT_ROW_BLOCK if flat.shape[0] > OPT_ROW_BLOCK * LANES else SUBLANES
    pad = (-flat.shape[0]) % (block * LANES)
    if pad:
        flat = jnp.concatenate([flat, jnp.zeros((pad,), F32)])
    return flat.reshape(-1, LANES)


def _unpack(packed, shapes, lead=()):
    flat = packed.reshape(lead + (-1,))
    out, off = [], 0
    for s in shapes:
        size = int(np.prod(s))
        out.append(flat[..., off:off + size].reshape(lead + tuple(s)))
        off += size
    return out


def _unshard_last(g):
    nd = g.ndim
    perm = tuple(range(1, nd - 1)) + (0, nd - 1)
    t = jnp.transpose(g, perm)
    return t.reshape(t.shape[:-2] + (N_DEV * g.shape[-1],))


def kernel(x, c, w_ada, b_ada, ln_g, ln_b, ffn_gu, ffn_down, gmlp_w_in, gmlp_b_in, gmlp_ln_g, gmlp_ln_b, gmlp_w_s, gmlp_b_s, gmlp_w_out, w_ada_kv, b_ada_kv, w_kv, attn_w_q, attn_rel_bias, attn_w_o, loss_target, m_w_ada, m_b_ada, m_ln_g, m_ln_b, m_ffn_gu, m_ffn_down, m_gmlp_w_in, m_gmlp_b_in, m_gmlp_ln_g, m_gmlp_ln_b, m_gmlp_w_s, m_gmlp_b_s, m_gmlp_w_out, m_w_ada_kv, m_b_ada_kv, m_w_kv, m_attn_w_q, m_attn_rel_bias, m_attn_w_o, v_w_ada, v_b_ada, v_ln_g, v_ln_b, v_ffn_gu, v_ffn_down, v_gmlp_w_in, v_gmlp_b_in, v_gmlp_ln_g, v_gmlp_ln_b, v_gmlp_w_s, v_gmlp_b_s, v_gmlp_w_out, v_w_ada_kv, v_b_ada_kv, v_w_kv, v_attn_w_q, v_attn_rel_bias, v_attn_w_o):
    weights = dict(w_ada=w_ada, b_ada=b_ada, ln_g=ln_g, ln_b=ln_b, ffn_gu=ffn_gu, ffn_down=ffn_down,
                   gmlp_w_in=gmlp_w_in, gmlp_b_in=gmlp_b_in, gmlp_ln_g=gmlp_ln_g, gmlp_ln_b=gmlp_ln_b,
                   gmlp_w_s=gmlp_w_s, gmlp_b_s=gmlp_b_s, gmlp_w_out=gmlp_w_out, w_ada_kv=w_ada_kv,
                   b_ada_kv=b_ada_kv, w_kv=w_kv, attn_w_q=attn_w_q, attn_rel_bias=attn_rel_bias,
                   attn_w_o=attn_w_o)
    mom1 = dict(w_ada=m_w_ada, b_ada=m_b_ada, ln_g=m_ln_g, ln_b=m_ln_b, ffn_gu=m_ffn_gu, ffn_down=m_ffn_down,
                gmlp_w_in=m_gmlp_w_in, gmlp_b_in=m_gmlp_b_in, gmlp_ln_g=m_gmlp_ln_g, gmlp_ln_b=m_gmlp_ln_b,
                gmlp_w_s=m_gmlp_w_s, gmlp_b_s=m_gmlp_b_s, gmlp_w_out=m_gmlp_w_out, w_ada_kv=m_w_ada_kv,
                b_ada_kv=m_b_ada_kv, w_kv=m_w_kv, attn_w_q=m_attn_w_q, attn_rel_bias=m_attn_rel_bias,
                attn_w_o=m_attn_w_o)
    mom2 = dict(w_ada=v_w_ada, b_ada=v_b_ada, ln_g=v_ln_g, ln_b=v_ln_b, ffn_gu=v_ffn_gu, ffn_down=v_ffn_down,
                gmlp_w_in=v_gmlp_w_in, gmlp_b_in=v_gmlp_b_in, gmlp_ln_g=v_gmlp_ln_g, gmlp_ln_b=v_gmlp_ln_b,
                gmlp_w_s=v_gmlp_w_s, gmlp_b_s=v_gmlp_b_s, gmlp_w_out=v_gmlp_w_out, w_ada_kv=v_w_ada_kv,
                b_ada_kv=v_b_ada_kv, w_kv=v_w_kv, attn_w_q=v_attn_w_q, attn_rel_bias=v_attn_rel_bias,
                attn_w_o=v_attn_w_o)
    order = list(weights)

    x = x[0]
    target = loss_target[0]
    t, d = x.shape
    n_mod = w_ada.shape[-1] * N_DEV // d
    mod_w = w_ada.shape[-1]
    kv_w = w_ada_kv.shape[-1]
    n_b = DEPTH - N_A
    me = _linear(*_position())

    l2 = DEPTH * 2
    big = dict(
        ffn_gu=ffn_gu.reshape((l2,) + ffn_gu.shape[2:]),
        ffn_down=ffn_down.reshape((l2,) + ffn_down.shape[2:]),
        gmlp_w_in=gmlp_w_in, gmlp_w_out=gmlp_w_out, w_kv=w_kv[None],
        attn_w_q=attn_w_q, attn_w_o=attn_w_o)
    big_names = list(big)
    core = lax.axis_index("c").astype(jnp.int32).reshape(1)
    chip = (2 * lax.axis_index("x") + lax.axis_index("y")).astype(jnp.int32).reshape(1)

    fwd_groups = [
        {"ffn_gu": (0, 1), "ffn_down": (0, 1)},
        {"gmlp_w_in": (0, 1), "gmlp_w_out": (0, 1)},
        {"ffn_gu": (1, 1), "ffn_down": (1, 1)},
        {"ffn_gu": (2, 1), "ffn_down": (2, 1)},
        {"gmlp_w_in": (1, 1), "gmlp_w_out": (1, 1)},
        {"ffn_gu": (3, 1), "ffn_down": (3, 1), "w_kv": (0, 1)},
        {"ffn_gu": (4, 1), "ffn_down": (4, 1)},
        {"attn_w_q": (0, 1), "attn_w_o": (0, 1)},
        {"ffn_gu": (5, 1), "ffn_down": (5, 1)},
        {"ffn_gu": (6, 1), "ffn_down": (6, 1)},
        {"attn_w_q": (1, 1), "attn_w_o": (1, 1)},
        {"ffn_gu": (7, 1), "ffn_down": (7, 1)},
    ]
    bwd_groups = []
    for l in range(DEPTH):
        g = {"ffn_gu": (2 * l, 2), "ffn_down": (2 * l, 2)}
        if l < N_A:
            g.update({"gmlp_w_in": (l, 1), "gmlp_w_out": (l, 1)})
        else:
            g.update({"attn_w_q": (l - N_A, 1), "attn_w_o": (l - N_A, 1)})
        if l == N_A - 1:
            g["w_kv"] = (0, 1)
        bwd_groups.append(g)

    def slot_of(groups, name, slot):
        for gi, g in enumerate(groups):
            if name in g and g[name][0] <= slot < g[name][0] + g[name][1]:
                return gi, slot - g[name][0]
        raise KeyError((name, slot))

    def start_group(gi, after=None):
        lands = []
        for name, (s0, cnt) in fwd_groups[gi].items():
            shard = big[name][s0:s0 + cnt].astype(BF16)
            land = lax.empty((N_DEV,) + shard.shape, BF16)
            lands.append(lax.dynamic_update_slice(land, shard[None], (me,) + (0,) * shard.ndim))
        return _gather_ici_start(f"gather_ici_start_{gi}", lands, after)

    gathered = [None] * len(fwd_groups)

    passing = {}

    def pass_on_early(gi, after):
        send_sems, recv_sems, lands, _ = flights[gi]
        lands = _gather_ici_wait(f"gather_ici_wait_{gi}", lands, send_sems, recv_sems, after)
        passing[gi] = _gather_ici_start(f"gather_d2d_start_{gi}", lands, to_sibling=True)
        return passing[gi][3]

    def land_group(gi, after):
        if gi in passing:
            send_sems, recv_sems, lands, _ = passing.pop(gi)
            lands = _gather_ici_wait(f"gather_d2d_wait_{gi}", lands, send_sems, recv_sems, after, to_sibling=True)
        else:
            send_sems, recv_sems, lands, _ = flights[gi]
            lands = _gather_d2d(_gather_ici_wait(f"gather_ici_wait_{gi}", lands, send_sems, recv_sems, after))
        gathered[gi] = dict(zip(fwd_groups[gi], lands))

    def weight(name, slot):
        gi, local = slot_of(fwd_groups, name, slot)
        return gathered[gi][name], local

    swapped = ("ffn_gu",)

    def grad_shape(name):
        s = big[name].shape[1:]
        return s[:-2] + (s[-1], s[-2]) if name in swapped else s

    partial = [{name: lax.empty((N_DEV, cnt) + grad_shape(name), BF16) for name, (_, cnt) in g.items()}
               for g in bwd_groups]

    c_all = _all_gather_small(_pack([c]))
    c_all = _unpack(c_all, [(d,)], lead=(N_DEV,))[0]
    c4 = _as4(c_all)
    mod_part = _matmul("ada_fwd", c4, w_ada[:, None], (DEPTH, 1, N_DEV, mod_w), F32, a_silu=True)
    kv_part = _matmul("ada_kv_fwd", c4, _as4(w_ada_kv), (1, 1, N_DEV, kv_w), F32, a_silu=True)
    small_shapes = [mod_part.shape, kv_part.shape, ln_g.shape, ln_b.shape, gmlp_b_in.shape,
                    gmlp_ln_g.shape, gmlp_ln_b.shape, attn_rel_bias.shape]
    small = _all_gather_small(_pack([mod_part, kv_part, ln_g, ln_b, gmlp_b_in, gmlp_ln_g, gmlp_ln_b,
                                     attn_rel_bias]))
    flights = [start_group(0, after=small)]
    flights += [start_group(gi, after=flights[0][3]) for gi in range(1, len(fwd_groups))]
    start_token = sum(f[3][0, 0] for f in flights)
    (mod_g, kvm_g, ln_g_g, ln_b_g, b_in_g, gln_g_g, gln_b_g, rel_g) = _unpack(small, small_shapes, lead=(N_DEV,))
    mod_mine = lax.dynamic_index_in_dim(mod_g[:, :, 0], me, axis=2, keepdims=False)
    mod = _unshard_last(mod_mine) + b_ada
    mod = mod.reshape(DEPTH, n_mod, 1, d)
    kvm_mine = lax.dynamic_index_in_dim(kvm_g[:, 0, 0], me, axis=1, keepdims=False)
    mkv = (_unshard_last(kvm_mine) + b_ada_kv).reshape(2, 1, d)
    ln_g_f = _unshard_last(ln_g_g)
    ln_b_f = _unshard_last(ln_b_g)
    half = N_DEV // 2
    b_in_f = jnp.transpose(b_in_g, (1, 0, 2))[:, :, None, :]
    gln_g_f = _unshard_last(gln_g_g).reshape(N_A, half, 1, -1)
    gln_b_f = _unshard_last(gln_b_g).reshape(N_A, half, 1, -1)
    rel_f = _unshard_last(rel_g)

    def shard_act(a):
        return a.reshape(a.shape[0], a.shape[2], a.shape[3])

    def grad_into(name, slot, mm):
        gi, local = slot_of(bwd_groups, name, slot)
        partial[gi][name] = mm(partial[gi][name], local)

    def ffn_fwd(h, lw, after=None):
        w_gu, l_gu = weight("ffn_gu", lw)
        w_dn, l_dn = weight("ffn_down", lw)
        gu, a = _ffn_up_fwd(h, w_gu, l_gu, after)
        y = _matmul("ffn_down_fwd", a[:, None], w_dn, (1, 1, t, d), F32, lb=l_dn, b_merge=2, reduce=True)
        return y[0, 0], (gu, a)

    def ffn_bwd(dy, h, saved, lw):
        gu, a = saved
        w_gu, l_gu = weight("ffn_gu", lw)
        w_dn, l_dn = weight("ffn_down", lw)
        dgu = _ffn_down_bwd_a(dy, w_dn, l_dn, gu).reshape((N_DEV,) + gu.shape[2:])
        grad_into("ffn_down", lw, lambda buf, lo: _matmul(
            "ffn_down_bwd_w", a[:, None], _as4(dy), buf.shape, BF16, ta=True, lo=lo, out_merge=2, out_buf=buf))
        dh = _matmul("ffn_gu_bwd_a", dgu[:, None], w_gu, (1, 1, t, d), F32, lb=l_gu, tb=True, reduce=True)
        grad_into("ffn_gu", lw, lambda buf, lo: _matmul(
            "ffn_gu_bwd_w", dgu[:, None], _as4(h), buf.shape, BF16, ta=True, lo=lo, out_buf=buf))
        return dh[0, 0], {}

    def gmlp_params(l):
        return (b_in_f[l], gln_g_f[l], gln_b_f[l], gmlp_w_s[l], gmlp_b_s[l][:, :, None])

    def gmlp_fwd(h, l, after=None):
        w_in, l_in = weight("gmlp_w_in", l)
        w_out, l_out = weight("gmlp_w_out", l)
        n = w_in.shape[-1]
        zpre = _matmul("gmlp_in_fwd", _as4(h), w_in, (N_DEV, 1, t, n), F32, lb=l_in, after=after)
        gated = _gmlp_mid_fwd(shard_act(zpre), *gmlp_params(l))
        y = _matmul("gmlp_out_fwd", gated[:, None], w_out, (1, 1, t, d), F32, lb=l_out, b_merge=2, reduce=True)
        return y[0, 0], (zpre, gated)

    def gmlp_bwd(dy, h, saved, l):
        zpre, gated = saved
        w_in, l_in = weight("gmlp_w_in", l)
        w_out, l_out = weight("gmlp_w_out", l)
        n = w_in.shape[-1]
        dgated = _matmul("gmlp_out_bwd_a", _as4(dy), w_out, (half, 1, t, n), F32, lb=l_out, b_merge=2, tb=True)
        grad_into("gmlp_w_out", l, lambda buf, lo: _matmul(
            "gmlp_out_bwd_w", gated[:, None], _as4(dy), buf.shape, BF16, ta=True, lo=lo, out_merge=2, out_buf=buf))
        dz, dws, dbs, dlng, dlnb, dbin = _gmlp_mid_bwd(shard_act(zpre), shard_act(dgated), *gmlp_params(l))
        dh = _matmul("gmlp_in_bwd_a", dz[:, None], w_in, (1, 1, t, d), F32, lb=l_in, tb=True, reduce=True)
        grad_into("gmlp_w_in", l, lambda buf, lo: _matmul(
            "gmlp_in_bwd_w", _as4(h), dz[:, None], buf.shape, BF16, ta=True, lo=lo, out_buf=buf))
        small_grads = dict(gmlp_w_s=dws, gmlp_b_s=dbs[:, :, 0], gmlp_ln_g=dlng.reshape(-1),
                           gmlp_ln_b=dlnb.reshape(-1), gmlp_b_in=dbin.reshape(-1))
        return dh[0, 0], small_grads

    def attn_fwd(h, j, kvp, after=None):
        rel_vec = _rel_vector(rel_f[j])
        w_q, l_q = weight("attn_w_q", j)
        w_o, l_o = weight("attn_w_o", j)
        q = _matmul("attn_q_fwd", _as4(h), w_q, (1, 1, t, d), BF16, lb=l_q, b_merge=N_DEV, reduce=True,
                    after=after)[0, 0]
        o = _attn_fwd(q, kvp, rel_vec)
        y = _matmul("attn_o_fwd", _as4(o), w_o, (1, 1, t, d), F32, lb=l_o, b_merge=N_DEV, reduce=True)
        return y[0, 0], (q, o, rel_vec)

    def attn_bwd(dy, h, saved, j, kvp, dkv_acc):
        q, o, rel_vec = saved
        w_q, l_q = weight("attn_w_q", j)
        w_o, l_o = weight("attn_w_o", j)
        do = _matmul("attn_o_bwd_a", _as4(dy), w_o, (1, 1, t, d), BF16, lb=l_o, b_merge=N_DEV, tb=True)[0, 0]
        grad_into("attn_w_o", j, lambda buf, lo: _matmul(
            "attn_o_bwd_w", _as4(o), _as4(dy), buf.shape, BF16, ta=True, lo=lo, out_merge=N_DEV, out_buf=buf))
        dq, dk, dv, dsc = _attn_bwd(q, do, kvp, rel_vec, *dkv_acc)
        drel = _rel_bias_grad(_skew_diagonals(dsc))
        dh = _matmul("attn_q_bwd_a", _as4(dq), w_q, (1, 1, t, d), F32, lb=l_q, b_merge=N_DEV, tb=True)
        grad_into("attn_w_q", j, lambda buf, lo: _matmul(
            "attn_q_bwd_w", _as4(h), _as4(dq), buf.shape, BF16, ta=True, lo=lo, out_merge=N_DEV, out_buf=buf))
        return dh[0, 0], dict(attn_rel_bias=drel, dkv=(dk, dv))

    tape = []
    kvp = None
    kv_tape = None
    first_use = {(l, i): 3 * l + i for l in range(DEPTH) for i in range(3)}
    PASS_ON_EARLY_FROM = 7
    h = _modulate(x, mod[0, 1], mod[0, 0] + start_token)
    for l in range(DEPTH):
        for i in range(3):
            if (l, i) in first_use:
                land_group(first_use[l, i], x)
            nl, ni = (l, i + 1) if i < 2 else (l + 1, 0)
            ahead = first_use.get((nl, ni), 0)
            started = pass_on_early(ahead, x) if ahead >= PASS_ON_EARLY_FROM else None
            scl, gate = mod[l, 3 * i + 1], mod[l, 3 * i + 2]
            wgt = 1.0 if i == 1 else 0.5
            gw = wgt * (1.0 + gate)
            if i != 1:
                y, saved = ffn_fwd(h, 2 * l + i // 2, started)
            elif l < N_A:
                y, saved = gmlp_fwd(h, l, started)
            else:
                y, saved = attn_fwd(h, l - N_A, kvp, started)
            readers = [(mod[nl, 3 * ni + 1], mod[nl, 3 * ni])] if nl < DEPTH else []
            shared_kv = (l, i) == (N_A - 1, 2)
            if shared_kv:
                readers.append((mkv[1], mkv[0]))
            outs = _ln_res_fwd(x, y, gw, ln_g_f[l, i][None], ln_b_f[l, i][None], readers)
            tape.append((x, h, y, gw, scl, saved))
            x = outs[0]
            h = outs[1] if nl < DEPTH else None
            if shared_kv:
                hkv = outs[-1]
                w_kvg, l_kv = weight("w_kv", 0)
                n = w_kvg.shape[-1]
                kv = _matmul("kv_fwd", _as4(hkv), w_kvg, (N_DEV, 1, t, n), BF16, lb=l_kv)
                kvp = jnp.pad(shard_act(kv), ((0, 0), (LEFT_PAD, 0), (0, 0)))
                kv_tape = hkv

    loss_part, dx = _loss_head(x, target)
    loss = lax.psum(loss_part[0, 0], MESH_AXES)

    d_mod = [[None] * n_mod for _ in range(DEPTH)]
    d_ln_g = [[None] * 3 for _ in range(DEPTH)]
    d_ln_b = [[None] * 3 for _ in range(DEPTH)]
    small_grads = {k: [None] * N_A for k in ("gmlp_w_s", "gmlp_b_s", "gmlp_ln_g", "gmlp_ln_b", "gmlp_b_in")}
    d_rel = [None] * n_b
    dkv_acc = ()
    d_mkv = None
    reductions = [None] * DEPTH
    sent_token = None
    readers = []
    for l in reversed(range(DEPTH)):
        if l == N_A - 1:
            hkv = kv_tape
            w_kvg, l_kv = weight("w_kv", 0)
            dkv = jnp.concatenate(dkv_acc)[:, LEFT_PAD:, :].astype(BF16)[:, None]
            dhkv = _matmul("kv_bwd_a", dkv, w_kvg, (1, 1, t, d), F32, lb=l_kv, tb=True, reduce=True)[0, 0]
            grad_into("w_kv", 0, lambda buf, lo: _matmul(
                "kv_bwd_w", _as4(hkv), dkv, buf.shape, BF16, ta=True, lo=lo, out_buf=buf))
            readers.append((dhkv, mkv[1], None))
        for i in reversed(range(3)):
            x_in, h, y, gw, scl, saved = tape[3 * l + i]
            wgt = 1.0 if i == 1 else 0.5
            if sent_token is not None:
                gw = gw + sent_token
                sent_token = None
            res = _ln_res_bwd(x_in, y, gw, ln_g_f[l, i][None], ln_b_f[l, i][None], dx,
                              [(r[0], r[1]) for r in readers])
            dx_res, dy, dgw, dg, db = res[:5]
            for k, (_, _, slot) in enumerate(readers):
                dscl_k, dshift_k = res[5 + 2 * k][0], res[6 + 2 * k][0]
                if slot is None:
                    d_mkv = jnp.concatenate([dshift_k, dscl_k])
                else:
                    d_mod[slot[0]][slot[1]], d_mod[slot[0]][slot[1] + 1] = dshift_k, dscl_k
            d_ln_g[l][i], d_ln_b[l][i] = dg[0], db[0]
            if i != 1:
                dh, extra = ffn_bwd(dy, h, saved, 2 * l + i // 2)
            elif l < N_A:
                dh, extra = gmlp_bwd(dy, h, saved, l)
                for k, g in extra.items():
                    small_grads[k][l] = g
            else:
                dh, extra = attn_bwd(dy, h, saved, l - N_A, kvp, dkv_acc)
                d_rel[l - N_A] = extra["attn_rel_bias"]
                dkv_acc = extra["dkv"]
            d_mod[l][3 * i + 2] = wgt * dgw[0]
            dx = dx_res
            readers = [(dh, scl, (l, 3 * i))]
        if l > 0:
            srcs = [partial[l][k] for k in bwd_groups[l]]
            lands = [lax.empty((N_DEV - 1,) + s.shape[1:], BF16) for s in srcs]
            reductions[l] = _partials_send_start(f"partials_send_start_{l}", srcs, lands, True)
            sent_token = reductions[l][4][0, 0]
    (dh, scl, _), = readers
    dx, dscl, dshift = _mod_bwd(dx, dh, tape[0][0], scl)
    d_mod[0][0], d_mod[0][1] = dshift[0], dscl[0]
    grad_x = dx[None]

    d_mod_arr = jnp.stack([jnp.concatenate(r) for r in d_mod])
    small_part = dict(
        b_ada=d_mod_arr, b_ada_kv=d_mkv,
        ln_g=jnp.stack([jnp.stack(r) for r in d_ln_g]), ln_b=jnp.stack([jnp.stack(r) for r in d_ln_b]),
        gmlp_b_in=jnp.stack(small_grads["gmlp_b_in"]), gmlp_ln_g=jnp.stack(small_grads["gmlp_ln_g"]),
        gmlp_ln_b=jnp.stack(small_grads["gmlp_ln_b"]), gmlp_w_s=jnp.stack(small_grads["gmlp_w_s"]),
        gmlp_b_s=jnp.stack(small_grads["gmlp_b_s"]), attn_rel_bias=jnp.stack(d_rel))
    small_names = list(small_part)
    sp_shapes = [small_part[k].shape for k in small_names]
    sp_all = _all_gather_small(_pack([small_part[k] for k in small_names]),
                               after=[partial[0][k] for k in bwd_groups[0]])

    from_sibling = _partials_d2d([partial[0][k] for k in bwd_groups[0]])
    sums = [_chip_sum(partial[0][k], r1, core) for k, r1 in zip(bwd_groups[0], from_sibling)]
    lands = [lax.empty((len(_ALL_CHIPS) - 1,) + s.shape[1:], BF16) for s in sums]
    reductions[0] = _partials_send_start("partials_send_start_0", sums, lands, False, after=sp_all)
    sent_token = reductions[0][4][0, 0]
    c4 = c4 + sent_token

    sp_sum = _sum_parts(sp_all)
    full_grads = dict(zip(small_names, _unpack(sp_sum, sp_shapes)))
    per_dev = dict(zip(small_names, _unpack(sp_all, sp_shapes, lead=(N_DEV,))))

    def my_cols(a, width):
        return lax.dynamic_slice_in_dim(a, me * width, width, axis=a.ndim - 1)

    grads = {}
    grads["b_ada"] = full_grads["b_ada"]
    grads["b_ada_kv"] = full_grads["b_ada_kv"]
    grads["gmlp_w_s"] = full_grads["gmlp_w_s"]
    grads["gmlp_b_s"] = full_grads["gmlp_b_s"]
    for k in ("ln_g", "ln_b", "gmlp_b_in", "gmlp_ln_g", "gmlp_ln_b", "attn_rel_bias"):
        grads[k] = my_cols(full_grads[k], weights[k].shape[-1])

    dmod_cols = jnp.transpose(my_cols(per_dev["b_ada"], mod_w), (1, 0, 2))[:, None]
    grads["w_ada"] = _matmul("ada_bwd_w", c4, dmod_cols, (DEPTH, 1, d, mod_w), F32, ta=True,
                             a_silu=True)[:, 0]
    dkv_cols = my_cols(per_dev["b_ada_kv"], kv_w)[None, None]
    grads["w_ada_kv"] = _matmul("ada_kv_bwd_w", c4, dkv_cols, (1, 1, d, kv_w), F32, ta=True,
                                a_silu=True)[0, 0]

    delta, new_m, new_v = {}, {}, {}
    first = jnp.zeros((1,), jnp.int32)

    def flat2(a, cols):
        return a.reshape(-1, cols)

    done = None
    for k in ("w_ada", "w_ada_kv"):
        w = weights[k]
        cols = w.shape[-1]
        res = _adamw(grads[k].reshape(1, -1, cols), first, None, flat2(w, cols), flat2(mom1[k], cols),
                     flat2(mom2[k], cols), after=done)
        grads[k], delta[k], new_m[k], new_v[k] = (a.reshape(w.shape) for a in res[:4])
        done = res[4]

    tiny = [k for k in order if k not in delta and k not in big_names]
    tiny_shapes = [weights[k].shape for k in tiny]
    tiny_out = _adamw((_pack([grads[k] for k in tiny]) + sent_token)[None], first, None,
                      _pack([weights[k] for k in tiny]), _pack([mom1[k] for k in tiny]),
                      _pack([mom2[k] for k in tiny]), after=done)
    for dst, arr in zip((grads, delta, new_m, new_v), tiny_out):
        for k, val in zip(tiny, _unpack(arr, tiny_shapes)):
            dst[k] = val

    def opt_view(k, a):
        a = jnp.swapaxes(a, -1, -2) if k in swapped else a
        return a.reshape(-1, a.shape[-1])

    def opt_unview(k, a):
        s = weights[k].shape
        return jnp.swapaxes(a.reshape(s[:-2] + (s[-1], s[-2])), -1, -2) if k in swapped else a.reshape(s)

    bufs = {k: [lax.empty(opt_view(k, weights[k]).shape, F32) for _ in range(4)] for k in big_names}
    done = tiny_out[4]
    me_idx = me.astype(jnp.int32).reshape(1)
    for l in reversed(range(DEPTH)):
        send_sems, recv_sems, srcs, lands, _ = reductions[l]
        srcs, lands = _partials_send_wait(f"partials_send_wait_{l}", srcs, lands, send_sems, recv_sems, l > 0, done)
        for k, own, got in zip(bwd_groups[l], srcs, lands):
            cols = own.shape[-1]
            slot_rows = int(np.prod(own.shape[2:-1]))
            *bufs[k], done = _adamw(own.reshape(own.shape[0], -1, cols), me_idx if l > 0 else chip,
                                    got.reshape(got.shape[0], -1, cols),
                                    opt_view(k, weights[k]), opt_view(k, mom1[k]), opt_view(k, mom2[k]),
                                    row0=bwd_groups[l][k][0] * slot_rows, bufs=bufs[k], after=done)
    for k in big_names:
        grads[k], delta[k], new_m[k], new_v[k] = (opt_unview(k, b) for b in bufs[k])

    return (loss, grad_x, *[grads[k] for k in order], *[delta[k] for k in order],
            *[new_m[k] for k in order], *[new_v[k] for k in order])
```
